```python
import math
import jax, jax.numpy as jnp
from jax import lax
import numpy as np

D_MODEL = 1024
BATCH = 8
SEQ = 4096
DEPTH = 4

N_A_LAYERS = DEPTH // 2
N_B_LAYERS = DEPTH - N_A_LAYERS
CHUNK = 128
SGU_WIDTH = 2 * D_MODEL
SGU_GROUPS = 8
HEAD_DIM = 64
KV_HEADS = D_MODEL // 128
DILATED_GROUPS = ((128, 1), (512, 4), (2048, 16))
N_GROUPS = len(DILATED_GROUPS)
Q_HEADS = N_GROUPS * KV_HEADS
BLK = 128
REL_BUCKETS = 32
REL_MAX_DIST = 2048
D_FF = 2816
CONV_WIDTH = 3
ALPHA = (2 * DEPTH) ** 0.25
BETA = (8 * DEPTH) ** -0.25
LN_EPS = 1e-5
NEG = -1e30

kernel_name = 'hybrid_sgu_dilated_yoco'


def layer_norm(x, g, b):
    xf = x.astype(jnp.float32)
    mu = jnp.mean(xf, axis=-1, keepdims=True)
    var = jnp.mean(jnp.square(xf - mu), axis=-1, keepdims=True)
    y = (xf - mu) * lax.rsqrt(var + LN_EPS) * g.astype(jnp.float32) + b.astype(jnp.float32)
    return y.astype(x.dtype)


def conv_ffn(x, w_up, conv_w, conv_b, w_down):
    T = x.shape[1]
    h = x @ w_up
    hp = jnp.pad(h, ((0, 0), (CONV_WIDTH - 1, 0), (0, 0)))
    h = sum(hp[:, k:k + T] * conv_w[k] for k in range(CONV_WIDTH)) + conv_b
    a, g = jnp.split(h, 2, axis=-1)
    return (jax.nn.gelu(a, approximate=False) * g) @ w_down


def mixer_a(x, w_in, ln_g, ln_b, w_s, b_s, w_out):
    B, T, _ = x.shape
    z = jax.nn.gelu(x @ w_in, approximate=False)
    u, v = jnp.split(z, 2, axis=-1)
    v = layer_norm(v, ln_g, ln_b)
    v = v.reshape(B, T // CHUNK, CHUNK, SGU_GROUPS, SGU_WIDTH // SGU_GROUPS)
    ws = w_s * jnp.tril(jnp.ones((CHUNK, CHUNK), w_s.dtype))
    sv = jnp.einsum('gts,bnsgc->bntgc', ws, v) + b_s.T[None, None, :, :, None]
    y = u * sv.reshape(B, T, SGU_WIDTH)
    return y @ w_out


def t5_bucket(dist):
    n = dist.astype(jnp.int32)
    max_exact = REL_BUCKETS // 2
    nf = jnp.maximum(n, 1).astype(jnp.float32)
    large = max_exact + (jnp.log(nf / max_exact) / math.log(REL_MAX_DIST / max_exact)
                         * (REL_BUCKETS - max_exact)).astype(jnp.int32)
    large = jnp.minimum(large, REL_BUCKETS - 1)
    return jnp.where(n < max_exact, n, large)


def band_delta():
    iq = jnp.arange(BLK)[:, None]
    ik = jnp.arange(2 * BLK)[None, :]
    return iq + BLK - ik


def band_bias(table_g, dil):
    dist = jnp.clip(band_delta(), 0, None) * dil
    return jnp.transpose(table_g[t5_bucket(dist)], (2, 0, 1)).astype(jnp.float32)


def dilated_band_attention(q, k, v, dil, n_back, bias):
    B, T, H, Dh = q.shape
    span = dil * BLK
    Tp = -(-T // span) * span
    S = Tp // dil
    NB = S // BLK

    def to_blocks(a):
        a = jnp.pad(a, ((0, 0), (0, Tp - T), (0, 0), (0, 0)))
        a = a.reshape(B, S, dil, H, Dh).transpose(0, 2, 3, 1, 4)
        return a.reshape(B, dil, H, NB, BLK, Dh)

    def band(a):
        prev = jnp.pad(a[:, :, :, :-1], ((0, 0), (0, 0), (0, 0), (1, 0), (0, 0), (0, 0)))
        return jnp.concatenate([prev, a], axis=4)

    qb, kb, vb = to_blocks(q), band(to_blocks(k)), band(to_blocks(v))
    s = jnp.einsum('brhnqd,brhnkd->brhnqk', qb, kb).astype(jnp.float32) * (Dh ** -0.5)
    s = s + bias[:, None]
    delta = band_delta()
    valid = (delta >= 0) & (delta <= n_back)
    first = (jnp.arange(NB)[:, None, None] == 0) & (jnp.arange(2 * BLK)[None, None, :] < BLK)
    valid = valid[None] & ~first
    s = jnp.where(valid, s, NEG)
    m = jnp.max(s, axis=-1, keepdims=True)
    p = jnp.exp(s - m)
    den = jnp.sum(p, axis=-1, keepdims=True)
    o = jnp.einsum('brhnqk,brhnkd->brhnqd', (p / den).astype(v.dtype), vb)
    lse = (m + jnp.log(den))[..., 0]
    o = o.reshape(B, dil, H, S, Dh).transpose(0, 3, 1, 2, 4).reshape(B, Tp, H, Dh)[:, :T]
    lse = lse.reshape(B, dil, H, S).transpose(0, 3, 1, 2).reshape(B, Tp, H)[:, :T]
    return o, lse


def mixer_b(x, k, v, w_q, w_o, rel_table):
    B, T, _ = x.shape
    q = (x @ w_q).reshape(B, T, N_GROUPS, KV_HEADS, HEAD_DIM)
    outs, lses = [], []
    for g, (win, dil) in enumerate(DILATED_GROUPS):
        bias = band_bias(rel_table[:, g * KV_HEADS:(g + 1) * KV_HEADS], dil)
        o, lse = dilated_band_attention(q[:, :, g], k, v, dil, win // dil, bias)
        outs.append(o)
        lses.append(lse)
    wts = jax.nn.softmax(jnp.stack(lses, axis=0), axis=0)
    o = jnp.einsum('gbth,gbthd->bthd', wts, jnp.stack(outs, axis=0).astype(jnp.float32))
    return o.reshape(B, T, KV_HEADS * HEAD_DIM).astype(x.dtype) @ w_o


def _fwd_setup_inputs(seed: int = 0) -> dict:
    key = jax.random.key(seed)
    ks = jax.random.split(key, 20)
    f32 = jnp.float32
    nrm = lambda k, shape, s: jax.random.normal(k, shape, f32) * s
    E2 = 2 * SGU_WIDTH
    return {
        'x': nrm(ks[0], (BATCH, SEQ, D_MODEL), 1.0),
        'a_w_in': nrm(ks[1], (N_A_LAYERS, D_MODEL, E2), D_MODEL ** -0.5),
        'a_ln_g': 1.0 + nrm(ks[2], (N_A_LAYERS, SGU_WIDTH), 0.05),
        'a_ln_b': nrm(ks[3], (N_A_LAYERS, SGU_WIDTH), 0.02),
        'a_w_s': nrm(ks[4], (N_A_LAYERS, SGU_GROUPS, CHUNK, CHUNK), CHUNK ** -0.5),
        'a_b_s': 1.0 + nrm(ks[5], (N_A_LAYERS, SGU_GROUPS, CHUNK), 0.1),
        'a_w_out': nrm(ks[6], (N_A_LAYERS, SGU_WIDTH, D_MODEL), BETA * SGU_WIDTH ** -0.5),
        'kv_w': nrm(ks[7], (D_MODEL, 2 * KV_HEADS * HEAD_DIM), D_MODEL ** -0.5),
        'b_w_q': nrm(ks[8], (N_B_LAYERS, D_MODEL, Q_HEADS * HEAD_DIM), D_MODEL ** -0.5),
        'b_w_o': nrm(ks[9], (N_B_LAYERS, KV_HEADS * HEAD_DIM, D_MODEL), BETA * (KV_HEADS * HEAD_DIM) ** -0.5),
        'rel_table': nrm(ks[10], (REL_BUCKETS, Q_HEADS), 0.5),
        'ffn_w_up': nrm(ks[11], (DEPTH, D_MODEL, 2 * D_FF), D_MODEL ** -0.5),
        'ffn_conv_w': nrm(ks[12], (DEPTH, CONV_WIDTH, 2 * D_FF), CONV_WIDTH ** -0.5),
        'ffn_conv_b': nrm(ks[13], (DEPTH, 2 * D_FF), 0.02),
        'ffn_w_down': nrm(ks[14], (DEPTH, D_FF, D_MODEL), BETA * D_FF ** -0.5),
        'ln_g': 1.0 + nrm(ks[15], (DEPTH, 2, D_MODEL), 0.05),
        'ln_b': nrm(ks[16], (DEPTH, 2, D_MODEL), 0.02),
    }


def _fwd_reference(x, a_w_in, a_ln_g, a_ln_b, a_w_s, a_b_s, a_w_out, kv_w, b_w_q, b_w_o,
              rel_table, ffn_w_up, ffn_conv_w, ffn_conv_b, ffn_w_down, ln_g, ln_b):
    B, T, _ = x.shape
    k = v = None
    for i in range(DEPTH):
        if i < N_A_LAYERS:
            h = mixer_a(x, a_w_in[i], a_ln_g[i], a_ln_b[i], a_w_s[i], a_b_s[i], a_w_out[i])
        else:
            if i == N_A_LAYERS:
                kv = (x @ kv_w).reshape(B, T, 2, KV_HEADS, HEAD_DIM)
                k, v = kv[:, :, 0], kv[:, :, 1]
            j = i - N_A_LAYERS
            h = mixer_b(x, k, v, b_w_q[j], b_w_o[j], rel_table)
        x = layer_norm(ALPHA * x + h, ln_g[i, 0], ln_b[i, 0])
        f = conv_ffn(x, ffn_w_up[i], ffn_conv_w[i], ffn_conv_b[i], ffn_w_down[i])
        x = layer_norm(ALPHA * x + f, ln_g[i, 1], ln_b[i, 1])
    return x


import jax as _jax
import jax.numpy as _jnp

TWIN_FORMAT = 'train_step'
FWD_PARAMS = ['x', 'a_w_in', 'a_ln_g', 'a_ln_b', 'a_w_s', 'a_b_s', 'a_w_out', 'kv_w', 'b_w_q', 'b_w_o', 'rel_table', 'ffn_w_up', 'ffn_conv_w', 'ffn_conv_b', 'ffn_w_down', 'ln_g', 'ln_b']
TWIN_WEIGHTS = ['a_w_in', 'a_ln_g', 'a_ln_b', 'a_w_s', 'a_b_s', 'a_w_out', 'kv_w', 'b_w_q', 'b_w_o', 'rel_table', 'ffn_w_up', 'ffn_conv_w', 'ffn_conv_b', 'ffn_w_down', 'ln_g', 'ln_b']
TWIN_DIFF_INPUT = 'x'
TWIN_INPUTS = ['x', 'a_w_in', 'a_ln_g', 'a_ln_b', 'a_w_s', 'a_b_s', 'a_w_out', 'kv_w', 'b_w_q', 'b_w_o', 'rel_table', 'ffn_w_up', 'ffn_conv_w', 'ffn_conv_b', 'ffn_w_down', 'ln_g', 'ln_b', 'loss_target', 'm_a_w_in', 'm_a_ln_g', 'm_a_ln_b', 'm_a_w_s', 'm_a_b_s', 'm_a_w_out', 'm_kv_w', 'm_b_w_q', 'm_b_w_o', 'm_rel_table', 'm_ffn_w_up', 'm_ffn_conv_w', 'm_ffn_conv_b', 'm_ffn_w_down', 'm_ln_g', 'm_ln_b', 'v_a_w_in', 'v_a_ln_g', 'v_a_ln_b', 'v_a_w_s', 'v_a_b_s', 'v_a_w_out', 'v_kv_w', 'v_b_w_q', 'v_b_w_o', 'v_rel_table', 'v_ffn_w_up', 'v_ffn_conv_w', 'v_ffn_conv_b', 'v_ffn_w_down', 'v_ln_g', 'v_ln_b']
TWIN_OUTPUTS = ['loss', 'grad_x', 'grad_a_w_in', 'grad_a_ln_g', 'grad_a_ln_b', 'grad_a_w_s', 'grad_a_b_s', 'grad_a_w_out', 'grad_kv_w', 'grad_b_w_q', 'grad_b_w_o', 'grad_rel_table', 'grad_ffn_w_up', 'grad_ffn_conv_w', 'grad_ffn_conv_b', 'grad_ffn_w_down', 'grad_ln_g', 'grad_ln_b', 'delta_a_w_in', 'delta_a_ln_g', 'delta_a_ln_b', 'delta_a_w_s', 'delta_a_b_s', 'delta_a_w_out', 'delta_kv_w', 'delta_b_w_q', 'delta_b_w_o', 'delta_rel_table', 'delta_ffn_w_up', 'delta_ffn_conv_w', 'delta_ffn_conv_b', 'delta_ffn_w_down', 'delta_ln_g', 'delta_ln_b', 'new_m_a_w_in', 'new_m_a_ln_g', 'new_m_a_ln_b', 'new_m_a_w_s', 'new_m_a_b_s', 'new_m_a_w_out', 'new_m_kv_w', 'new_m_b_w_q', 'new_m_b_w_o', 'new_m_rel_table', 'new_m_ffn_w_up', 'new_m_ffn_conv_w', 'new_m_ffn_conv_b', 'new_m_ffn_w_down', 'new_m_ln_g', 'new_m_ln_b', 'new_v_a_w_in', 'new_v_a_ln_g', 'new_v_a_ln_b', 'new_v_a_w_s', 'new_v_a_b_s', 'new_v_a_w_out', 'new_v_kv_w', 'new_v_b_w_q', 'new_v_b_w_o', 'new_v_rel_table', 'new_v_ffn_w_up', 'new_v_ffn_conv_w', 'new_v_ffn_conv_b', 'new_v_ffn_w_down', 'new_v_ln_g', 'new_v_ln_b']
TWIN_LEAF_KINDS = {'loss': 'loss', 'grad_x': 'grad_x', 'grad_a_w_in': 'grad_w', 'grad_a_ln_g': 'grad_w', 'grad_a_ln_b': 'grad_w', 'grad_a_w_s': 'grad_w', 'grad_a_b_s': 'grad_w', 'grad_a_w_out': 'grad_w', 'grad_kv_w': 'grad_w', 'grad_b_w_q': 'grad_w', 'grad_b_w_o': 'grad_w', 'grad_rel_table': 'grad_w', 'grad_ffn_w_up': 'grad_w', 'grad_ffn_conv_w': 'grad_w', 'grad_ffn_conv_b': 'grad_w', 'grad_ffn_w_down': 'grad_w', 'grad_ln_g': 'grad_w', 'grad_ln_b': 'grad_w', 'delta_a_w_in': 'delta_w', 'delta_a_ln_g': 'delta_w', 'delta_a_ln_b': 'delta_w', 'delta_a_w_s': 'delta_w', 'delta_a_b_s': 'delta_w', 'delta_a_w_out': 'delta_w', 'delta_kv_w': 'delta_w', 'delta_b_w_q': 'delta_w', 'delta_b_w_o': 'delta_w', 'delta_rel_table': 'delta_w', 'delta_ffn_w_up': 'delta_w', 'delta_ffn_conv_w': 'delta_w', 'delta_ffn_conv_b': 'delta_w', 'delta_ffn_w_down': 'delta_w', 'delta_ln_g': 'delta_w', 'delta_ln_b': 'delta_w', 'new_m_a_w_in': 'new_m', 'new_m_a_ln_g': 'new_m', 'new_m_a_ln_b': 'new_m', 'new_m_a_w_s': 'new_m', 'new_m_a_b_s': 'new_m', 'new_m_a_w_out': 'new_m', 'new_m_kv_w': 'new_m', 'new_m_b_w_q': 'new_m', 'new_m_b_w_o': 'new_m', 'new_m_rel_table': 'new_m', 'new_m_ffn_w_up': 'new_m', 'new_m_ffn_conv_w': 'new_m', 'new_m_ffn_conv_b': 'new_m', 'new_m_ffn_w_down': 'new_m', 'new_m_ln_g': 'new_m', 'new_m_ln_b': 'new_m', 'new_v_a_w_in': 'new_v', 'new_v_a_ln_g': 'new_v', 'new_v_a_ln_b': 'new_v', 'new_v_a_w_s': 'new_v', 'new_v_a_b_s': 'new_v', 'new_v_a_w_out': 'new_v', 'new_v_kv_w': 'new_v', 'new_v_b_w_q': 'new_v', 'new_v_b_w_o': 'new_v', 'new_v_rel_table': 'new_v', 'new_v_ffn_w_up': 'new_v', 'new_v_ffn_conv_w': 'new_v', 'new_v_ffn_conv_b': 'new_v', 'new_v_ffn_w_down': 'new_v', 'new_v_ln_g': 'new_v', 'new_v_ln_b': 'new_v'}


def _forward(args):
    return _fwd_reference(*[args[k] for k in FWD_PARAMS])


def _output_shape():
    def fwd():
        inp = _fwd_setup_inputs(0)
        return _fwd_reference(*[inp[k] for k in FWD_PARAMS])
    out = _jax.eval_shape(fwd)
    return out.shape, out.dtype

N_MICROBATCH = 1
ADAM_LR = 0.001
ADAM_B1 = 0.9
ADAM_B2 = 0.999
ADAM_EPS = 1e-08
ADAM_WD = 0.01
ADAM_STEP = 10
PER_EXAMPLE_BATCH_AXIS = {'x': 0, 'loss_target': 0}
SHARED_INPUTS = []
_WEIGHT_DTYPES = {'a_w_in': _jnp.float32, 'a_ln_g': _jnp.float32, 'a_ln_b': _jnp.float32, 'a_w_s': _jnp.float32, 'a_b_s': _jnp.float32, 'a_w_out': _jnp.float32, 'kv_w': _jnp.float32, 'b_w_q': _jnp.float32, 'b_w_o': _jnp.float32, 'rel_table': _jnp.float32, 'ffn_w_up': _jnp.float32, 'ffn_conv_w': _jnp.float32, 'ffn_conv_b': _jnp.float32, 'ffn_w_down': _jnp.float32, 'ln_g': _jnp.float32, 'ln_b': _jnp.float32}
MOMENT_SCALE = {'a_w_in': 2.310214e-02, 'a_ln_g': 1.506478e-02, 'a_ln_b': 1.512254e-02, 'a_w_s': 2.132844e-02, 'a_b_s': 3.113841e-02, 'a_w_out': 1.286369e-01, 'kv_w': 2.058686e-02, 'b_w_q': 6.310291e-03, 'b_w_o': 3.096311e-02, 'rel_table': 1.232443e-02, 'ffn_w_up': 1.843971e-02, 'ffn_conv_w': 1.837467e-02, 'ffn_conv_b': 2.591960e-02, 'ffn_w_down': 7.222341e-02, 'ln_g': 1.205318e+01, 'ln_b': 1.427386e+00}


def _to_microbatches(a, axis):
    t = _jnp.moveaxis(a, axis, 0)
    t = t.reshape((N_MICROBATCH, t.shape[0] // N_MICROBATCH) + t.shape[1:])
    return _jnp.moveaxis(t, 1, axis + 1)


def setup_inputs(seed: int = 0) -> dict:
    inp = _fwd_setup_inputs(seed)
    key = _jax.random.fold_in(_jax.random.key(seed), 7919)
    shape, _ = _output_shape()
    out = dict(inp)
    out["loss_target"] = _jax.random.normal(_jax.random.fold_in(key, 0), shape, _jnp.float32)
    for i, name in enumerate(TWIN_WEIGHTS):
        w = inp[name].astype(_jnp.float32)
        if MOMENT_SCALE is None:
            s = _jnp.sqrt(_jnp.mean(_jnp.square(w)) + 1e-30)
        else:
            s = MOMENT_SCALE[name]
        km, kv = _jax.random.split(_jax.random.fold_in(key, i + 1))
        out[name] = w
        out["m_" + name] = s * _jax.random.normal(km, w.shape, _jnp.float32)
        out["v_" + name] = (s * s) * _jax.random.uniform(kv, w.shape, _jnp.float32, 0.5, 1.5)
    if N_MICROBATCH > 1:
        for name, axis in PER_EXAMPLE_BATCH_AXIS.items():
            out[name] = _to_microbatches(out[name], axis)
    return {'x': out['x'], 'a_w_in': out['a_w_in'], 'a_ln_g': out['a_ln_g'], 'a_ln_b': out['a_ln_b'], 'a_w_s': out['a_w_s'], 'a_b_s': out['a_b_s'], 'a_w_out': out['a_w_out'], 'kv_w': out['kv_w'], 'b_w_q': out['b_w_q'], 'b_w_o': out['b_w_o'], 'rel_table': out['rel_table'], 'ffn_w_up': out['ffn_w_up'], 'ffn_conv_w': out['ffn_conv_w'], 'ffn_conv_b': out['ffn_conv_b'], 'ffn_w_down': out['ffn_w_down'], 'ln_g': out['ln_g'], 'ln_b': out['ln_b'], 'loss_target': out['loss_target'], 'm_a_w_in': out['m_a_w_in'], 'm_a_ln_g': out['m_a_ln_g'], 'm_a_ln_b': out['m_a_ln_b'], 'm_a_w_s': out['m_a_w_s'], 'm_a_b_s': out['m_a_b_s'], 'm_a_w_out': out['m_a_w_out'], 'm_kv_w': out['m_kv_w'], 'm_b_w_q': out['m_b_w_q'], 'm_b_w_o': out['m_b_w_o'], 'm_rel_table': out['m_rel_table'], 'm_ffn_w_up': out['m_ffn_w_up'], 'm_ffn_conv_w': out['m_ffn_conv_w'], 'm_ffn_conv_b': out['m_ffn_conv_b'], 'm_ffn_w_down': out['m_ffn_w_down'], 'm_ln_g': out['m_ln_g'], 'm_ln_b': out['m_ln_b'], 'v_a_w_in': out['v_a_w_in'], 'v_a_ln_g': out['v_a_ln_g'], 'v_a_ln_b': out['v_a_ln_b'], 'v_a_w_s': out['v_a_w_s'], 'v_a_b_s': out['v_a_b_s'], 'v_a_w_out': out['v_a_w_out'], 'v_kv_w': out['v_kv_w'], 'v_b_w_q': out['v_b_w_q'], 'v_b_w_o': out['v_b_w_o'], 'v_rel_table': out['v_rel_table'], 'v_ffn_w_up': out['v_ffn_w_up'], 'v_ffn_conv_w': out['v_ffn_conv_w'], 'v_ffn_conv_b': out['v_ffn_conv_b'], 'v_ffn_w_down': out['v_ffn_w_down'], 'v_ln_g': out['v_ln_g'], 'v_ln_b': out['v_ln_b']}


def _loss(weights, diff, rest, loss_target):
    with _jax.named_scope("forward"):
        args = {**rest, TWIN_DIFF_INPUT: diff, **{k: w.astype(_WEIGHT_DTYPES[k]) for k, w in weights.items()}}
        y = _forward(args)
    with _jax.named_scope("loss_head"):
        err = _jnp.square(y.astype(_jnp.float32) - loss_target)
        return 0.5 * _jnp.sum(_jnp.mean(err, axis=-1)) if err.ndim else 0.5 * err


def _adamw(w, g, m, v):
    m = ADAM_B1 * m + (1.0 - ADAM_B1) * g
    v = ADAM_B2 * v + (1.0 - ADAM_B2) * _jnp.square(g)
    m_hat = m / (1.0 - ADAM_B1 ** ADAM_STEP)
    v_hat = v / (1.0 - ADAM_B2 ** ADAM_STEP)
    delta = -ADAM_LR * (m_hat / (_jnp.sqrt(v_hat) + ADAM_EPS) + ADAM_WD * w)
    return delta, m, v


def reference(x, a_w_in, a_ln_g, a_ln_b, a_w_s, a_b_s, a_w_out, kv_w, b_w_q, b_w_o, rel_table, ffn_w_up, ffn_conv_w, ffn_conv_b, ffn_w_down, ln_g, ln_b, loss_target, m_a_w_in, m_a_ln_g, m_a_ln_b, m_a_w_s, m_a_b_s, m_a_w_out, m_kv_w, m_b_w_q, m_b_w_o, m_rel_table, m_ffn_w_up, m_ffn_conv_w, m_ffn_conv_b, m_ffn_w_down, m_ln_g, m_ln_b, v_a_w_in, v_a_ln_g, v_a_ln_b, v_a_w_s, v_a_b_s, v_a_w_out, v_kv_w, v_b_w_q, v_b_w_o, v_rel_table, v_ffn_w_up, v_ffn_conv_w, v_ffn_conv_b, v_ffn_w_down, v_ln_g, v_ln_b):
    given = dict(x=x, a_w_in=a_w_in, a_ln_g=a_ln_g, a_ln_b=a_ln_b, a_w_s=a_w_s, a_b_s=a_b_s, a_w_out=a_w_out, kv_w=kv_w, b_w_q=b_w_q, b_w_o=b_w_o, rel_table=rel_table, ffn_w_up=ffn_w_up, ffn_conv_w=ffn_conv_w, ffn_conv_b=ffn_conv_b, ffn_w_down=ffn_w_down, ln_g=ln_g, ln_b=ln_b, loss_target=loss_target, m_a_w_in=m_a_w_in, m_a_ln_g=m_a_ln_g, m_a_ln_b=m_a_ln_b, m_a_w_s=m_a_w_s, m_a_b_s=m_a_b_s, m_a_w_out=m_a_w_out, m_kv_w=m_kv_w, m_b_w_q=m_b_w_q, m_b_w_o=m_b_w_o, m_rel_table=m_rel_table, m_ffn_w_up=m_ffn_w_up, m_ffn_conv_w=m_ffn_conv_w, m_ffn_conv_b=m_ffn_conv_b, m_ffn_w_down=m_ffn_w_down, m_ln_g=m_ln_g, m_ln_b=m_ln_b, v_a_w_in=v_a_w_in, v_a_ln_g=v_a_ln_g, v_a_ln_b=v_a_ln_b, v_a_w_s=v_a_w_s, v_a_b_s=v_a_b_s, v_a_w_out=v_a_w_out, v_kv_w=v_kv_w, v_b_w_q=v_b_w_q, v_b_w_o=v_b_w_o, v_rel_table=v_rel_table, v_ffn_w_up=v_ffn_w_up, v_ffn_conv_w=v_ffn_conv_w, v_ffn_conv_b=v_ffn_conv_b, v_ffn_w_down=v_ffn_w_down, v_ln_g=v_ln_g, v_ln_b=v_ln_b)
    weights = {n: given[n] for n in TWIN_WEIGHTS}
    shared = {n: given[n] for n in SHARED_INPUTS}
    per_example = {n: given[n] for n in ['x']}
    grad_fn = _jax.value_and_grad(_loss, argnums=(0, 1))

    def one_microbatch(ex, loss_target):
        ex = dict(ex)
        diff = ex.pop(TWIN_DIFF_INPUT)
        return grad_fn(weights, diff, {**shared, **ex}, loss_target)

    if N_MICROBATCH == 1:
        loss, (grad_w, grad_x) = one_microbatch(per_example, given["loss_target"])
    else:
        def body(carry, xs):
            loss_sum, grad_sum = carry
            l_k, (gw_k, gx_k) = one_microbatch(xs[0], xs[1])
            with _jax.named_scope("update"):
                return (loss_sum + l_k, _jax.tree.map(_jnp.add, grad_sum, gw_k)), gx_k

        init = (_jnp.zeros((), _jnp.float32), _jax.tree.map(_jnp.zeros_like, weights))
        (loss, grad_w), grad_x = _jax.lax.scan(body, init, (per_example, given["loss_target"]))
    with _jax.named_scope("update"):
        delta_w, new_m, new_v = {}, {}, {}
        for n in TWIN_WEIGHTS:
            delta_w[n], new_m[n], new_v[n] = _adamw(weights[n], grad_w[n], given["m_" + n], given["v_" + n])
    return (loss, grad_x, *[grad_w[n] for n in TWIN_WEIGHTS], *[delta_w[n] for n in TWIN_WEIGHTS],
            *[new_m[n] for n in TWIN_WEIGHTS], *[new_v[n] for n in TWIN_WEIGHTS])
```

```python
import math

import numpy as np
import jax
import jax.numpy as jnp
from jax import lax
from jax.experimental import pallas as pl
from jax.experimental.pallas import tpu as pltpu

F32 = jnp.float32
BF16 = jnp.bfloat16
MESH = pl.DeviceIdType.MESH

N_DEV = 8
DEPTH = 4
N_A = 2
CHUNK = 128
BLK = 128
HEAD_DIM = 64
DILATED_GROUPS = ((128, 1), (512, 4), (2048, 16))
N_GROUPS = 3
REL_BUCKETS = 32
REL_MAX_DIST = 2048
ALPHA = (2 * DEPTH) ** 0.25
LN_EPS = 1e-5
NEG = -1e30
ADAM_LR = 0.001
ADAM_B1 = 0.9
ADAM_B2 = 0.999
ADAM_EPS = 1e-08
ADAM_WD = 0.01
ADAM_STEP = 10

LANES = 128
VMEM_LIMIT = 56 * 1024 * 1024
MM_TILE_CAP = 1408
INV_SQRT2 = 1.0 / math.sqrt(2.0)
INV_SQRT_2PI = 1.0 / math.sqrt(2.0 * math.pi)


def _pick(n, cap):
    best = None
    for t in range(LANES, min(n, cap) + 1, LANES):
        if n % t == 0:
            best = t
    return best if best is not None else n


def _params(sem):
    return pltpu.CompilerParams(dimension_semantics=sem, vmem_limit_bytes=VMEM_LIMIT)


def _gelu(x):
    return 0.5 * x * (1.0 + lax.erf(x * INV_SQRT2))


def _gelu_grad(x):
    return 0.5 * (1.0 + lax.erf(x * INV_SQRT2)) + x * jnp.exp(-0.5 * x * x) * INV_SQRT_2PI


def _mm(a, b, *, ta=False, tb=False, out_dtype=F32, scale=None, name):
    if ta:
        K, M = a.shape
    else:
        M, K = a.shape
    if tb:
        N, Kb = b.shape
    else:
        Kb, N = b.shape
    assert K == Kb, (a.shape, b.shape, ta, tb)
    tm, tn, tk = _pick(M, MM_TILE_CAP), _pick(N, MM_TILE_CAP), _pick(K, MM_TILE_CAP)
    nk = K // tk
    dn = (((0 if ta else 1,), (1 if tb else 0,)), ((), ()))

    def body(a_ref, b_ref, o_ref, acc_ref):
        k = pl.program_id(2)
        part = lax.dot_general(a_ref[...].astype(BF16), b_ref[...].astype(BF16), dn,
                               preferred_element_type=F32)

        @pl.when(k == 0)
        def _():
            acc_ref[...] = part

        @pl.when(k > 0)
        def _():
            acc_ref[...] += part

        @pl.when(k == nk - 1)
        def _():
            r = acc_ref[...]
            if scale is not None:
                r = r * scale
            o_ref[...] = r.astype(out_dtype)

    a_spec = (pl.BlockSpec((tk, tm), lambda i, j, k: (k, i)) if ta
              else pl.BlockSpec((tm, tk), lambda i, j, k: (i, k)))
    b_spec = (pl.BlockSpec((tn, tk), lambda i, j, k: (j, k)) if tb
              else pl.BlockSpec((tk, tn), lambda i, j, k: (k, j)))
    return pl.pallas_call(
        body, grid=(M // tm, N // tn, nk), in_specs=[a_spec, b_spec],
        out_specs=pl.BlockSpec((tm, tn), lambda i, j, k: (i, j)),
        out_shape=jax.ShapeDtypeStruct((M, N), out_dtype),
        scratch_shapes=[pltpu.VMEM((tm, tn), F32)],
        compiler_params=_params(("parallel", "parallel", "arbitrary")), name=name,
    )(a, b)


def _add_ln_fwd(x, h, g, b, *, name):
    T, D = x.shape
    rb = _pick(T, 512)

    def body(x_ref, h_ref, g_ref, b_ref, o_ref, ob_ref):
        pre = ALPHA * x_ref[...] + h_ref[...]
        mu = jnp.mean(pre, axis=1, keepdims=True)
        cen = pre - mu
        var = jnp.mean(cen * cen, axis=1, keepdims=True)
        y = cen * lax.rsqrt(var + LN_EPS) * g_ref[...] + b_ref[...]
        o_ref[...] = y
        ob_ref[...] = y.astype(BF16)

    row = pl.BlockSpec((rb, D), lambda i: (i, 0))
    vec = pl.BlockSpec((1, D), lambda i: (0, 0))
    return pl.pallas_call(
        body, grid=(T // rb,), in_specs=[row, row, vec, vec], out_specs=[row, row],
        out_shape=[jax.ShapeDtypeStruct((T, D), F32), jax.ShapeDtypeStruct((T, D), BF16)],
        compiler_params=_params(("parallel",)), name=name,
    )(x, h, g.reshape(1, D), b.reshape(1, D))


def _add_ln_bwd(x, h, g, terms, *, name):
    T, D = x.shape
    rb = _pick(T, 512)
    coefs = [c for c, _ in terms]
    nt = len(terms)

    def body(*refs):
        x_ref, h_ref, g_ref = refs[:3]
        t_refs = refs[3:3 + nt]
        dp_ref, dpb_ref, dg_ref, db_ref = refs[3 + nt:]
        dy = None
        for c, r in zip(coefs, t_refs):
            v = r[...] if c == 1.0 else c * r[...]
            dy = v if dy is None else dy + v
        pre = ALPHA * x_ref[...] + h_ref[...]
        mu = jnp.mean(pre, axis=1, keepdims=True)
        cen = pre - mu
        var = jnp.mean(cen * cen, axis=1, keepdims=True)
        rstd = lax.rsqrt(var + LN_EPS)
        xhat = cen * rstd
        dxh = dy * g_ref[...]
        m1 = jnp.mean(dxh, axis=1, keepdims=True)
        m2 = jnp.mean(dxh * xhat, axis=1, keepdims=True)
        dpre = rstd * (dxh - m1 - xhat * m2)
        dp_ref[...] = dpre
        dpb_ref[...] = dpre.astype(BF16)
        dg = jnp.sum(dy * xhat, axis=0, keepdims=True)
        db = jnp.sum(dy, axis=0, keepdims=True)

        @pl.when(pl.program_id(0) == 0)
        def _():
            dg_ref[...] = dg
            db_ref[...] = db

        @pl.when(pl.program_id(0) > 0)
        def _():
            dg_ref[...] += dg
            db_ref[...] += db

    row = pl.BlockSpec((rb, D), lambda i: (i, 0))
    vec = pl.BlockSpec((1, D), lambda i: (0, 0))
    return pl.pallas_call(
        body, grid=(T // rb,), in_specs=[row, row, vec] + [row] * nt,
        out_specs=[row, row, vec, vec],
        out_shape=[jax.ShapeDtypeStruct((T, D), F32), jax.ShapeDtypeStruct((T, D), BF16),
                   jax.ShapeDtypeStruct((1, D), F32), jax.ShapeDtypeStruct((1, D), F32)],
        compiler_params=_params(("arbitrary",)), name=name,
    )(x, h, g.reshape(1, D), *[a for _, a in terms])


def _lincomb(terms, out_dtype, *, name):
    R, C = terms[0][1].shape
    rb = _pick(R, 512)
    coefs = [c for c, _ in terms]
    nt = len(terms)

    def body(*refs):
        acc = None
        for c, r in zip(coefs, refs[:nt]):
            v = r[...].astype(F32)
            v = v if c == 1.0 else c * v
            acc = v if acc is None else acc + v
        refs[nt][...] = acc.astype(out_dtype)

    row = pl.BlockSpec((rb, C), lambda i: (i, 0))
    return pl.pallas_call(
        body, grid=(R // rb,), in_specs=[row] * nt, out_specs=row,
        out_shape=jax.ShapeDtypeStruct((R, C), out_dtype),
        compiler_params=_params(("parallel",)), name=name,
    )(*[a for _, a in terms])


def _loss_grad(y, tgt, *, name):
    T, D = y.shape
    rb = _pick(T, 512)

    def body(y_ref, t_ref, dy_ref, l_ref):
        err = y_ref[...] - t_ref[...]
        dy_ref[...] = err * (1.0 / D)
        part = jnp.sum(jnp.sum(err * err, axis=1, keepdims=True), axis=0, keepdims=True) * (0.5 / D)
        part = jnp.broadcast_to(part, (1, LANES))

        @pl.when(pl.program_id(0) == 0)
        def _():
            l_ref[...] = part

        @pl.when(pl.program_id(0) > 0)
        def _():
            l_ref[...] += part

    row = pl.BlockSpec((rb, D), lambda i: (i, 0))
    return pl.pallas_call(
        body, grid=(T // rb,), in_specs=[row, row],
        out_specs=[row, pl.BlockSpec((1, LANES), lambda i: (0, 0))],
        out_shape=[jax.ShapeDtypeStruct((T, D), F32), jax.ShapeDtypeStruct((1, LANES), F32)],
        compiler_params=_params(("arbitrary",)), name=name,
    )(y, tgt)


def _sgu_fwd(zp, ln_g, ln_b, ws, bst, *, name):
    T, E2 = zp.shape
    E = E2 // 2
    G = ws.shape[0]
    cg = E // G
    rb = 2 * CHUNK

    def body(z_ref, g_ref, b_ref, ws_ref, bs_ref, y_ref):
        u = _gelu(z_ref[:, :E])
        v = _gelu(z_ref[:, E:])
        mu = jnp.mean(v, axis=1, keepdims=True)
        cen = v - mu
        var = jnp.mean(cen * cen, axis=1, keepdims=True)
        vn = (cen * lax.rsqrt(var + LN_EPS) * g_ref[...] + b_ref[...]).astype(BF16)
        for ci in range(rb // CHUNK):
            rows = slice(ci * CHUNK, (ci + 1) * CHUNK)
            for gi in range(G):
                cols = slice(gi * cg, (gi + 1) * cg)
                sv = jnp.dot(ws_ref[gi], vn[rows, cols], preferred_element_type=F32)
                sv = sv + bs_ref[:, gi:gi + 1]
                y_ref[rows, cols] = (u[rows, cols] * sv).astype(BF16)

    return pl.pallas_call(
        body, grid=(T // rb,),
        in_specs=[pl.BlockSpec((rb, E2), lambda i: (i, 0)),
                  pl.BlockSpec((1, E), lambda i: (0, 0)), pl.BlockSpec((1, E), lambda i: (0, 0)),
                  pl.BlockSpec((G, CHUNK, CHUNK), lambda i: (0, 0, 0)),
                  pl.BlockSpec((CHUNK, G), lambda i: (0, 0))],
        out_specs=pl.BlockSpec((rb, E), lambda i: (i, 0)),
        out_shape=jax.ShapeDtypeStruct((T, E), BF16),
        compiler_params=_params(("parallel",)), name=name,
    )(zp, ln_g.reshape(1, E), ln_b.reshape(1, E), ws, bst)


def _sgu_bwd(zp, dy, ln_g, ln_b, ws, wst, bst, *, name):
    T, E2 = zp.shape
    E = E2 // 2
    G = ws.shape[0]
    cg = E // G
    rb = CHUNK
    nsteps = T // rb

    def body(z_ref, dy_ref, g_ref, b_ref, ws_ref, wst_ref, bs_ref,
             dz_ref, dg_ref, db_ref, dws_ref, dbs_ref, dsv_acc):
        step = pl.program_id(0)

        @pl.when(step == 0)
        def _():
            dg_ref[...] = jnp.zeros_like(dg_ref)
            db_ref[...] = jnp.zeros_like(db_ref)
            dws_ref[...] = jnp.zeros_like(dws_ref)
            dsv_acc[...] = jnp.zeros_like(dsv_acc)

        zu = z_ref[:, :E]
        zv = z_ref[:, E:]
        u = _gelu(zu)
        v = _gelu(zv)
        mu = jnp.mean(v, axis=1, keepdims=True)
        cen = v - mu
        var = jnp.mean(cen * cen, axis=1, keepdims=True)
        rstd = lax.rsqrt(var + LN_EPS)
        xhat = cen * rstd
        vn = (xhat * g_ref[...] + b_ref[...]).astype(BF16)
        dyv = dy_ref[...]
        dsv = dyv * u
        dsv_acc[...] += dsv
        dsvb = dsv.astype(BF16)
        tril = (lax.broadcasted_iota(jnp.int32, (CHUNK, CHUNK), 0)
                >= lax.broadcasted_iota(jnp.int32, (CHUNK, CHUNK), 1))
        du_parts = []
        dvn_parts = []
        for gi in range(G):
            cols = slice(gi * cg, (gi + 1) * cg)
            sv = jnp.dot(ws_ref[gi], vn[:, cols], preferred_element_type=F32) + bs_ref[:, gi:gi + 1]
            du_parts.append(dyv[:, cols] * sv)
            dvn_parts.append(jnp.dot(wst_ref[gi], dsvb[:, cols], preferred_element_type=F32))
            dw = lax.dot_general(dsvb[:, cols], vn[:, cols], (((1,), (1,)), ((), ())),
                                 preferred_element_type=F32)
            dws_ref[gi] += jnp.where(tril, dw, 0.0)
        du = jnp.concatenate(du_parts, axis=1)
        dvn = jnp.concatenate(dvn_parts, axis=1)
        dg_ref[...] += jnp.sum(dvn * xhat, axis=0, keepdims=True)
        db_ref[...] += jnp.sum(dvn, axis=0, keepdims=True)
        dxh = dvn * g_ref[...]
        m1 = jnp.mean(dxh, axis=1, keepdims=True)
        m2 = jnp.mean(dxh * xhat, axis=1, keepdims=True)
        dv = rstd * (dxh - m1 - xhat * m2)
        dz_ref[:, :E] = (du * _gelu_grad(zu)).astype(BF16)
        dz_ref[:, E:] = (dv * _gelu_grad(zv)).astype(BF16)

        @pl.when(step == nsteps - 1)
        def _():
            lane = lax.broadcasted_iota(jnp.int32, (CHUNK, LANES), 1)
            out = jnp.zeros((CHUNK, LANES), F32)
            for gi in range(G):
                s = jnp.sum(dsv_acc[:, gi * cg:(gi + 1) * cg], axis=1, keepdims=True)
                out = jnp.where(lane == gi, s, out)
            dbs_ref[...] = out

    vecE = pl.BlockSpec((1, E), lambda i: (0, 0))
    wspec = pl.BlockSpec((G, CHUNK, CHUNK), lambda i: (0, 0, 0))
    return pl.pallas_call(
        body, grid=(nsteps,),
        in_specs=[pl.BlockSpec((rb, E2), lambda i: (i, 0)), pl.BlockSpec((rb, E), lambda i: (i, 0)),
                  vecE, vecE, wspec, wspec, pl.BlockSpec((CHUNK, G), lambda i: (0, 0))],
        out_specs=[pl.BlockSpec((rb, E2), lambda i: (i, 0)), vecE, vecE, wspec,
                   pl.BlockSpec((CHUNK, LANES), lambda i: (0, 0))],
        out_shape=[jax.ShapeDtypeStruct((T, E2), BF16), jax.ShapeDtypeStruct((1, E), F32),
                   jax.ShapeDtypeStruct((1, E), F32), jax.ShapeDtypeStruct((G, CHUNK, CHUNK), F32),
                   jax.ShapeDtypeStruct((CHUNK, LANES), F32)],
        scratch_shapes=[pltpu.VMEM((CHUNK, E), F32)],
        compiler_params=_params(("arbitrary",)), name=name,
    )(zp, dy, ln_g.reshape(1, E), ln_b.reshape(1, E), ws, wst, bst)


def _shift_down(x, k, row):
    return jnp.where(row >= k, pltpu.roll(x, k, 0), 0.0)


def _shift_up(x, k, row, T):
    return jnp.where(row < T - k, pltpu.roll(x, T - k, 0), 0.0)


def _conv3(x, w_ref, b_ref, row):
    return (w_ref[0:1, :] * _shift_down(x, 2, row) + w_ref[1:2, :] * _shift_down(x, 1, row)
            + w_ref[2:3, :] * x + b_ref[...])


def _convgate_fwd(hh, cw, cb, *, name):
    T, F2 = hh.shape
    F = F2 // 2
    ns = F // LANES

    def body(a_ref, g_ref, wa_ref, wg_ref, ba_ref, bg_ref, o_ref):
        row = lax.broadcasted_iota(jnp.int32, (T, LANES), 0)
        ca = _conv3(a_ref[...], wa_ref, ba_ref, row)
        cgv = _conv3(g_ref[...], wg_ref, bg_ref, row)
        o_ref[...] = (_gelu(ca) * cgv).astype(BF16)

    sa = lambda r: pl.BlockSpec((r, LANES), lambda j: (0, j))
    sg = lambda r: pl.BlockSpec((r, LANES), lambda j: (0, j + ns))
    return pl.pallas_call(
        body, grid=(ns,), in_specs=[sa(T), sg(T), sa(3), sg(3), sa(1), sg(1)],
        out_specs=sa(T), out_shape=jax.ShapeDtypeStruct((T, F), BF16),
        compiler_params=_params(("parallel",)), name=name,
    )(hh, hh, cw, cw, cb, cb)


def _convgate_bwd(hh, dact, cw, cb, *, name):
    T, F2 = hh.shape
    F = F2 // 2
    ns = F // LANES

    def body(a_ref, g_ref, d_ref, wa_ref, wg_ref, ba_ref, bg_ref,
             da_ref, dg_ref, dwa_ref, dwg_ref, dba_ref, dbg_ref):
        row = lax.broadcasted_iota(jnp.int32, (T, LANES), 0)
        d = d_ref[...].astype(F32)
        ca = _conv3(a_ref[...], wa_ref, ba_ref, row)
        cgv = _conv3(g_ref[...], wg_ref, bg_ref, row)
        cdf = 0.5 * (1.0 + lax.erf(ca * INV_SQRT2))
        dca = d * cgv * (cdf + ca * jnp.exp(-0.5 * ca * ca) * INV_SQRT_2PI)
        dcg = d * (ca * cdf)
        for x_ref, w_ref, dc, dx_ref, dw_ref, db_ref in (
                (a_ref, wa_ref, dca, da_ref, dwa_ref, dba_ref),
                (g_ref, wg_ref, dcg, dg_ref, dwg_ref, dbg_ref)):
            x = x_ref[...]
            dx = (w_ref[2:3, :] * dc + w_ref[1:2, :] * _shift_up(dc, 1, row, T)
                  + w_ref[0:1, :] * _shift_up(dc, 2, row, T))
            dx_ref[...] = dx.astype(BF16)
            dw_ref[0:1, :] = jnp.sum(dc * _shift_down(x, 2, row), axis=0, keepdims=True)
            dw_ref[1:2, :] = jnp.sum(dc * _shift_down(x, 1, row), axis=0, keepdims=True)
            dw_ref[2:3, :] = jnp.sum(dc * x, axis=0, keepdims=True)
            db_ref[...] = jnp.sum(dc, axis=0, keepdims=True)

    sa = lambda r: pl.BlockSpec((r, LANES), lambda j: (0, j))
    sg = lambda r: pl.BlockSpec((r, LANES), lambda j: (0, j + ns))
    return pl.pallas_call(
        body, grid=(ns,), in_specs=[sa(T), sg(T), sa(T), sa(3), sg(3), sa(1), sg(1)],
        out_specs=[sa(T), sa(T), sa(3), sa(3), sa(1), sa(1)],
        out_shape=[jax.ShapeDtypeStruct((T, F), BF16), jax.ShapeDtypeStruct((T, F), BF16),
                   jax.ShapeDtypeStruct((3, F), F32), jax.ShapeDtypeStruct((3, F), F32),
                   jax.ShapeDtypeStruct((1, F), F32), jax.ShapeDtypeStruct((1, F), F32)],
        compiler_params=_params(("parallel",)), name=name,
    )(hh, hh, dact, cw, cw, cb, cb)


def _bucket_maps():
    iq = np.arange(BLK)[:, None]
    ik = np.arange(2 * BLK)[None, :]
    delta = iq + BLK - ik
    maps = []
    for win, dil in DILATED_GROUPS:
        n = np.clip(delta, 0, None) * dil
        max_exact = REL_BUCKETS // 2
        nf = np.maximum(n, 1).astype(np.float32)
        large = max_exact + (np.log(nf / np.float32(max_exact)) / np.float32(math.log(REL_MAX_DIST / max_exact))
                             * np.float32(REL_BUCKETS - max_exact)).astype(np.int32)
        large = np.minimum(large, REL_BUCKETS - 1)
        bucket = np.where(n < max_exact, n, large)
        valid = (delta >= 0) & (delta <= win // dil)
        maps.append(np.where(valid, bucket, -1).astype(np.int32))
    return np.stack(maps)


def _band_bias(rel_table, bmap, H, *, name):
    def body(t_ref, m_ref, o_ref):
        g = pl.program_id(0)
        bm = m_ref[0]
        for h in range(H):
            acc = jnp.full((BLK, 2 * BLK), NEG, F32)
            for b in range(REL_BUCKETS):
                acc = jnp.where(bm == b, t_ref[b, g * H + h], acc)
            o_ref[0, h] = acc

    return pl.pallas_call(
        body, grid=(N_GROUPS,),
        in_specs=[pl.BlockSpec(memory_space=pltpu.SMEM),
                  pl.BlockSpec((1, BLK, 2 * BLK), lambda g: (g, 0, 0))],
        out_specs=pl.BlockSpec((1, H, BLK, 2 * BLK), lambda g: (g, 0, 0, 0)),
        out_shape=jax.ShapeDtypeStruct((N_GROUPS, H, BLK, 2 * BLK), F32),
        compiler_params=_params(("parallel",)), name=name,
    )(rel_table, bmap)


def _band_bias_bwd(dbias, bmap, H, *, name):
    def body(d_ref, m_ref, o_ref):
        bm = m_ref[0]
        rowi = lax.broadcasted_iota(jnp.int32, (REL_BUCKETS, LANES), 0)
        lane = lax.broadcasted_iota(jnp.int32, (REL_BUCKETS, LANES), 1)
        out = jnp.zeros((REL_BUCKETS, LANES), F32)
        for h in range(H):
            dv = d_ref[0, h]
            for b in range(REL_BUCKETS):
                s = jnp.sum(jnp.sum(jnp.where(bm == b, dv, 0.0), axis=1, keepdims=True),
                            axis=0, keepdims=True)
                out = jnp.where((rowi == b) & (lane == h), s, out)
        o_ref[0] = out

    return pl.pallas_call(
        body, grid=(N_GROUPS,),
        in_specs=[pl.BlockSpec((1, H, BLK, 2 * BLK), lambda g: (g, 0, 0, 0)),
                  pl.BlockSpec((1, BLK, 2 * BLK), lambda g: (g, 0, 0))],
        out_specs=pl.BlockSpec((1, REL_BUCKETS, LANES), lambda g: (g, 0, 0)),
        out_shape=jax.ShapeDtypeStruct((N_GROUPS, REL_BUCKETS, LANES), F32),
        compiler_params=_params(("parallel",)), name=name,
    )(dbias, bmap)


def _head_masks():
    lane = lax.broadcasted_iota(jnp.int32, (BLK, LANES), 1)
    return (lane < HEAD_DIM, lane >= HEAD_DIM)


def _attn_fwd(q, kv, bias, gi, *, name):
    T = q.shape[0]
    HD = kv.shape[1] // 2
    d = DILATED_GROUPS[gi][1]
    S = T // d
    NB = S // BLK
    H = HD // HEAD_DIM
    qv = q.reshape(S, d * 3 * HD)
    kvv = kv.reshape(S, d * 2 * HD)

    def body(q_ref, kp_ref, kc_ref, vp_ref, vc_ref, b_ref, o_ref, l_ref):
        n = pl.program_id(1)
        col = lax.broadcasted_iota(jnp.int32, (BLK, 2 * BLK), 1)
        first = (n == 0) & (col < BLK)
        hm = _head_masks()
        for p in range(HD // LANES):
            sl = slice(p * LANES, (p + 1) * LANES)
            qp = q_ref[:, sl]
            kc = jnp.concatenate([kp_ref[:, sl], kc_ref[:, sl]], axis=0)
            vc = jnp.concatenate([vp_ref[:, sl], vc_ref[:, sl]], axis=0)
            outs = []
            lses = []
            for hh in range(2):
                qm = jnp.where(hm[hh], qp, jnp.zeros_like(qp))
                s = lax.dot_general(qm, kc, (((1,), (1,)), ((), ())), preferred_element_type=F32)
                s = jnp.where(first, NEG, s + b_ref[2 * p + hh])
                m = jnp.max(s, axis=1, keepdims=True)
                e = jnp.exp(s - m)
                den = jnp.sum(e, axis=1, keepdims=True)
                outs.append(jnp.dot((e / den).astype(BF16), vc, preferred_element_type=F32))
                lses.append(m + jnp.log(den))
            o_ref[:, sl] = jnp.where(hm[0], outs[0], outs[1])
            l_ref[:, sl] = jnp.where(hm[0], lses[0], lses[1])

    blk = lambda f: pl.BlockSpec((BLK, HD), f)
    prev = lambda n: jnp.maximum(n - 1, 0)
    return pl.pallas_call(
        body, grid=(d, NB),
        in_specs=[blk(lambda r, n: (n, r * 3 + gi)),
                  blk(lambda r, n: (prev(n), r * 2)), blk(lambda r, n: (n, r * 2)),
                  blk(lambda r, n: (prev(n), r * 2 + 1)), blk(lambda r, n: (n, r * 2 + 1)),
                  pl.BlockSpec((H, BLK, 2 * BLK), lambda r, n: (0, 0, 0))],
        out_specs=[blk(lambda r, n: (n, r)), blk(lambda r, n: (n, r))],
        out_shape=[jax.ShapeDtypeStruct((S, d * HD), F32), jax.ShapeDtypeStruct((S, d * HD), F32)],
        compiler_params=_params(("parallel", "parallel")), name=name,
    )(qv, kvv, kvv, kvv, kvv, bias)


def _attn_combine(os, ls, *, name):
    T, HD = os[0].shape
    rb = _pick(T, 512)

    def body(o0, o1, o2, l0, l1, l2, o_ref, ob_ref, l_ref):
        la, lb, lc = l0[...], l1[...], l2[...]
        m = jnp.maximum(jnp.maximum(la, lb), lc)
        L = m + jnp.log(jnp.exp(la - m) + jnp.exp(lb - m) + jnp.exp(lc - m))
        o = jnp.exp(la - L) * o0[...] + jnp.exp(lb - L) * o1[...] + jnp.exp(lc - L) * o2[...]
        o_ref[...] = o
        ob_ref[...] = o.astype(BF16)
        l_ref[...] = L

    row = pl.BlockSpec((rb, HD), lambda i: (i, 0))
    return pl.pallas_call(
        body, grid=(T // rb,), in_specs=[row] * 6, out_specs=[row] * 3,
        out_shape=[jax.ShapeDtypeStruct((T, HD), F32), jax.ShapeDtypeStruct((T, HD), BF16),
                   jax.ShapeDtypeStruct((T, HD), F32)],
        compiler_params=_params(("parallel",)), name=name,
    )(*[a.reshape(T, HD) for a in os], *[a.reshape(T, HD) for a in ls])


def _attn_bwd(q, kv, bias, do, o, L, gi, *, name):
    T = q.shape[0]
    HD = kv.shape[1] // 2
    d = DILATED_GROUPS[gi][1]
    S = T // d
    NB = S // BLK
    H = HD // HEAD_DIM
    qv = q.reshape(S, d * 3 * HD)
    kvv = kv.reshape(S, d * 2 * HD)
    dov, ov, Lv = (a.reshape(S, d * HD) for a in (do, o, L))

    def body(q_ref, kp_ref, kc_ref, vp_ref, vc_ref, b_ref, do_ref, o_ref, L_ref,
             dq_ref, dk_ref, dv_ref, db_ref, ck_ref, cv_ref):
        r = pl.program_id(0)
        n = pl.program_id(1)

        @pl.when((r == 0) & (n == 0))
        def _():
            db_ref[...] = jnp.zeros_like(db_ref)

        @pl.when(n == 0)
        def _():
            ck_ref[...] = jnp.zeros_like(ck_ref)
            cv_ref[...] = jnp.zeros_like(cv_ref)

        @pl.when(n < NB)
        def _():
            col = lax.broadcasted_iota(jnp.int32, (BLK, 2 * BLK), 1)
            first = (n == 0) & (col < BLK)
            hm = _head_masks()
            for p in range(HD // LANES):
                sl = slice(p * LANES, (p + 1) * LANES)
                qp = q_ref[:, sl]
                kc = jnp.concatenate([kp_ref[:, sl], kc_ref[:, sl]], axis=0)
                vc = jnp.concatenate([vp_ref[:, sl], vc_ref[:, sl]], axis=0)
                dop = do_ref[:, sl]
                dob = dop.astype(BF16)
                prod = dop * o_ref[:, sl]
                Lp = L_ref[:, sl]
                dq_parts = []
                dkc = None
                dvc = None
                for hh in range(2):
                    qm = jnp.where(hm[hh], qp, jnp.zeros_like(qp))
                    dom = jnp.where(hm[hh], dob, jnp.zeros_like(dob))
                    s = lax.dot_general(qm, kc, (((1,), (1,)), ((), ())), preferred_element_type=F32)
                    s = jnp.where(first, NEG, s + b_ref[2 * p + hh])
                    lse = Lp[:, hh * HEAD_DIM:hh * HEAD_DIM + 1]
                    pr = jnp.exp(s - lse)
                    dp = lax.dot_general(dom, vc, (((1,), (1,)), ((), ())), preferred_element_type=F32)
                    delta = jnp.sum(jnp.where(hm[hh], prod, 0.0), axis=1, keepdims=True)
                    ds = pr * (dp - delta)
                    db_ref[2 * p + hh] += ds
                    dsb = ds.astype(BF16)
                    dq_parts.append(jnp.dot(dsb, kc, preferred_element_type=F32))
                    dkh = lax.dot_general(dsb, qm, (((0,), (0,)), ((), ())), preferred_element_type=F32)
                    dvh = lax.dot_general(pr.astype(BF16), dom, (((0,), (0,)), ((), ())),
                                          preferred_element_type=F32)
                    dkc = dkh if dkc is None else dkc + dkh
                    dvc = dvh if dvc is None else dvc + dvh
                dq = jnp.where(hm[0], dq_parts[0], dq_parts[1])
                dq_ref[:, sl] = (dq * (HEAD_DIM ** -0.5)).astype(BF16)
                dk_ref[:, sl] = ck_ref[:, sl] + dkc[:BLK]
                dv_ref[:, sl] = cv_ref[:, sl] + dvc[:BLK]
                ck_ref[:, sl] = dkc[BLK:]
                cv_ref[:, sl] = dvc[BLK:]

        @pl.when(n == NB)
        def _():
            dk_ref[...] = ck_ref[...]
            dv_ref[...] = cv_ref[...]

    blk = lambda f: pl.BlockSpec((BLK, HD), f)
    cur = lambda n: jnp.minimum(n, NB - 1)
    prev = lambda n: jnp.maximum(jnp.minimum(n, NB - 1) - 1, 0)
    lag = lambda n: jnp.maximum(n - 1, 0)
    return pl.pallas_call(
        body, grid=(d, NB + 1),
        in_specs=[blk(lambda r, n: (cur(n), r * 3 + gi)),
                  blk(lambda r, n: (prev(n), r * 2)), blk(lambda r, n: (cur(n), r * 2)),
                  blk(lambda r, n: (prev(n), r * 2 + 1)), blk(lambda r, n: (cur(n), r * 2 + 1)),
                  pl.BlockSpec((H, BLK, 2 * BLK), lambda r, n: (0, 0, 0)),
                  blk(lambda r, n: (cur(n), r)), blk(lambda r, n: (cur(n), r)),
                  blk(lambda r, n: (cur(n), r))],
        out_specs=[blk(lambda r, n: (cur(n), r)), blk(lambda r, n: (lag(n), r)),
                   blk(lambda r, n: (lag(n), r)),
                   pl.BlockSpec((H, BLK, 2 * BLK), lambda r, n: (0, 0, 0))],
        out_shape=[jax.ShapeDtypeStruct((S, d * HD), BF16), jax.ShapeDtypeStruct((S, d * HD), F32),
                   jax.ShapeDtypeStruct((S, d * HD), F32),
                   jax.ShapeDtypeStruct((H, BLK, 2 * BLK), F32)],
        scratch_shapes=[pltpu.VMEM((BLK, HD), F32), pltpu.VMEM((BLK, HD), F32)],
        compiler_params=_params(("arbitrary", "arbitrary")), name=name,
    )(qv, kvv, kvv, kvv, kvv, bias, dov, ov, Lv)


def _local_step(x, tgt, W):
    T, D = x.shape
    HD = W["kv_w"].shape[1] // 2
    H = HD // HEAD_DIM
    G = W["a_w_s"].shape[1]
    assert T % (DILATED_GROUPS[-1][1] * BLK) == 0

    tril = jnp.tril(jnp.ones((CHUNK, CHUNK), F32))
    bmap = jnp.asarray(_bucket_maps())
    bias = _band_bias(W["rel_table"], bmap, H, name="band_bias")

    saved = []
    xc, xcb = x, x.astype(BF16)
    kvb = None
    for i in range(DEPTH):
        s = {"x": xc, "xb": xcb}
        if i < N_A:
            ws_m = W["a_w_s"][i] * tril
            s["ws"] = ws_m.astype(BF16)
            s["wst"] = jnp.swapaxes(ws_m, 1, 2).astype(BF16)
            s["bst"] = W["a_b_s"][i].T
            s["zp"] = _mm(xcb, W["a_w_in"][i], name=f"a_in_{i}")
            s["y"] = _sgu_fwd(s["zp"], W["a_ln_g"][i], W["a_ln_b"][i], s["ws"], s["bst"], name=f"sgu_fwd_{i}")
            s["h"] = _mm(s["y"], W["a_w_out"][i], name=f"a_out_{i}")
        else:
            j = i - N_A
            if kvb is None:
                kvb = _mm(xcb, W["kv_w"], out_dtype=BF16, name="kv_proj")
            s["q"] = _mm(xcb, W["b_w_q"][j], out_dtype=BF16, scale=HEAD_DIM ** -0.5, name=f"q_proj_{j}")
            os, ls = [], []
            for gi in range(N_GROUPS):
                o_g, l_g = _attn_fwd(s["q"], kvb, bias[gi], gi, name=f"attn_fwd_{j}_{gi}")
                os.append(o_g)
                ls.append(l_g)
            s["o"], s["ob"], s["L"] = _attn_combine(os, ls, name=f"attn_mix_{j}")
            s["h"] = _mm(s["ob"], W["b_w_o"][j], name=f"o_proj_{j}")
        s["x1"], s["x1b"] = _add_ln_fwd(xc, s["h"], W["ln_g"][i, 0], W["ln_b"][i, 0], name=f"ln1_fwd_{i}")
        s["hh"] = _mm(s["x1b"], W["ffn_w_up"][i], name=f"ffn_up_{i}")
        s["cw"] = W["ffn_conv_w"][i]
        s["cb"] = W["ffn_conv_b"][i].reshape(1, -1)
        s["act"] = _convgate_fwd(s["hh"], s["cw"], s["cb"], name=f"convgate_fwd_{i}")
        s["f"] = _mm(s["act"], W["ffn_w_down"][i], name=f"ffn_down_{i}")
        xc, xcb = _add_ln_fwd(s["x1"], s["f"], W["ln_g"][i, 1], W["ln_b"][i, 1], name=f"ln2_fwd_{i}")
        saved.append(s)

    dy, lossv = _loss_grad(xc, tgt, name="loss_grad")
    loss = lossv[0, 0]

    gl = {k: [None] * DEPTH for k in ("ffn_w_up", "ffn_conv_w", "ffn_conv_b", "ffn_w_down", "ln_g", "ln_b")}
    ga = {k: [None] * N_A for k in ("a_w_in", "a_ln_g", "a_ln_b", "a_w_s", "a_b_s", "a_w_out")}
    gb = {k: [None] * (DEPTH - N_A) for k in ("b_w_q", "b_w_o")}
    dks, dvs, dbias = [], [], []
    grads = {}
    terms = [(1.0, dy)]
    for i in reversed(range(DEPTH)):
        s = saved[i]
        dp2, dp2b, dg2, db2 = _add_ln_bwd(s["x1"], s["f"], W["ln_g"][i, 1], terms, name=f"ln2_bwd_{i}")
        dact = _mm(dp2b, W["ffn_w_down"][i], tb=True, name=f"ffn_down_dx_{i}")
        gl["ffn_w_down"][i] = _mm(s["act"], dp2b, ta=True, name=f"ffn_down_dw_{i}")
        dha, dhg, dwa, dwg, dba, dbg = _convgate_bwd(s["hh"], dact, s["cw"], s["cb"], name=f"convgate_bwd_{i}")
        dhh = jnp.concatenate([dha, dhg], axis=1)
        gl["ffn_conv_w"][i] = jnp.concatenate([dwa, dwg], axis=1)
        gl["ffn_conv_b"][i] = jnp.concatenate([dba, dbg], axis=1)[0]
        dx1 = _mm(dhh, W["ffn_w_up"][i], tb=True, name=f"ffn_up_dx_{i}")
        gl["ffn_w_up"][i] = _mm(s["x1b"], dhh, ta=True, name=f"ffn_up_dw_{i}")
        dp1, dp1b, dg1, db1 = _add_ln_bwd(s["x"], s["h"], W["ln_g"][i, 0], [(ALPHA, dp2), (1.0, dx1)],
                                          name=f"ln1_bwd_{i}")
        gl["ln_g"][i] = jnp.concatenate([dg1, dg2], axis=0)
        gl["ln_b"][i] = jnp.concatenate([db1, db2], axis=0)
        terms = [(ALPHA, dp1)]
        if i < N_A:
            dyy = _mm(dp1b, W["a_w_out"][i], tb=True, name=f"a_out_dx_{i}")
            ga["a_w_out"][i] = _mm(s["y"], dp1b, ta=True, name=f"a_out_dw_{i}")
            dzp, dlg, dlb, dws, dbs = _sgu_bwd(s["zp"], dyy, W["a_ln_g"][i], W["a_ln_b"][i], s["ws"], s["wst"],
                                               s["bst"], name=f"sgu_bwd_{i}")
            ga["a_ln_g"][i], ga["a_ln_b"][i], ga["a_w_s"][i] = dlg[0], dlb[0], dws
            ga["a_b_s"][i] = dbs[:, :G].T
            terms.append((1.0, _mm(dzp, W["a_w_in"][i], tb=True, name=f"a_in_dx_{i}")))
            ga["a_w_in"][i] = _mm(s["xb"], dzp, ta=True, name=f"a_in_dw_{i}")
        else:
            j = i - N_A
            do = _mm(dp1b, W["b_w_o"][j], tb=True, name=f"o_proj_dx_{j}")
            gb["b_w_o"][j] = _mm(s["ob"], dp1b, ta=True, name=f"o_proj_dw_{j}")
            dqs, dbl = [], []
            for gi in range(N_GROUPS):
                dq_g, dk_g, dv_g, db_g = _attn_bwd(s["q"], kvb, bias[gi], do, s["o"], s["L"], gi,
                                                   name=f"attn_bwd_{j}_{gi}")
                dqs.append(dq_g.reshape(T, HD))
                dks.append((1.0, dk_g.reshape(T, HD)))
                dvs.append((1.0, dv_g.reshape(T, HD)))
                dbl.append(db_g)
            dbias.append(jnp.stack(dbl))
            dq = jnp.concatenate(dqs, axis=1)
            terms.append((1.0, _mm(dq, W["b_w_q"][j], tb=True, name=f"q_proj_dx_{j}")))
            gb["b_w_q"][j] = _mm(s["xb"], dq, ta=True, name=f"q_proj_dw_{j}")
            if i == N_A:
                dkv = jnp.concatenate([_lincomb(dks, BF16, name="dk_sum"), _lincomb(dvs, BF16, name="dv_sum")],
                                      axis=1)
                terms.append((1.0, _mm(dkv, W["kv_w"], tb=True, name="kv_proj_dx")))
                grads["kv_w"] = _mm(s["xb"], dkv, ta=True, name="kv_proj_dw")
                dbt = _lincomb([(1.0, a.reshape(-1, 2 * BLK)) for a in dbias], F32, name="dbias_sum")
                dtab = _band_bias_bwd(dbt.reshape(N_GROUPS, H, BLK, 2 * BLK), bmap, H, name="band_bias_bwd")
                grads["rel_table"] = jnp.transpose(dtab[:, :, :H], (1, 0, 2)).reshape(REL_BUCKETS, N_GROUPS * H)
    grad_x = _lincomb(terms, F32, name="grad_x")
    for dct in (gl, ga, gb):
        for k, v in dct.items():
            grads[k] = jnp.stack(v)
    return loss, grad_x, grads


def _my_index():
    return 4 * lax.axis_index("x") + 2 * lax.axis_index("y") + lax.axis_index("c")


HBM_SPEC = pl.BlockSpec(memory_space=pltpu.HBM)


def _all_gather(xs, *, name):
    R, C = xs.shape

    def body(x_ref, out_ref, send_sems, recv_sems, local_sem):
        x, y, c = lax.axis_index("x"), lax.axis_index("y"), lax.axis_index("c")
        me, sibling = (x, y, c), (x, y, 1 - c)
        chips = [(1 - x, y), (x, 1 - y), (1 - x, 1 - y)]

        def slot(px, py, pc):
            return out_ref.at[4 * px + 2 * py + pc]

        def copy(k, block, to, src=None):
            return pltpu.make_async_remote_copy(
                src_ref=slot(*block) if src is None else src, dst_ref=slot(*block),
                send_sem=send_sems.at[k], recv_sem=recv_sems.at[k],
                device_id=to, device_id_type=MESH)

        mine = pltpu.make_async_copy(x_ref, slot(*me), local_sem)
        mine.start()
        first = [copy(0, me, sibling, src=x_ref)]
        first += [copy(1 + j, me, (*chip, c), src=x_ref) for j, chip in enumerate(chips)]
        for cp in first:
            cp.start()
        passed = [copy(4 + j, (*chip, c), sibling) for j, chip in enumerate(chips)]
        for j, chip in enumerate(chips):
            copy(1 + j, (*chip, c), me).wait_recv()
            passed[j].start()
        copy(0, sibling, me).wait_recv()
        for j, chip in enumerate(chips):
            copy(4 + j, (*chip, 1 - c), me).wait_recv()
        for cp in first + passed:
            cp.wait_send()
        mine.wait()

    return pl.pallas_call(
        body, out_shape=jax.ShapeDtypeStruct((N_DEV, R, C), xs.dtype),
        in_specs=[HBM_SPEC], out_specs=HBM_SPEC,
        scratch_shapes=[pltpu.SemaphoreType.DMA((7,)), pltpu.SemaphoreType.DMA((7,)),
                        pltpu.SemaphoreType.DMA],
        name=name,
    )(xs)


def _all_to_all(p, *, name):
    _, R, C = p.shape

    def body(p_ref, out_ref, send_sems, recv_sems, local_sem):
        x, y, c = lax.axis_index("x"), lax.axis_index("y"), lax.axis_index("c")
        me = 4 * x + 2 * y + c
        mine = pltpu.make_async_copy(p_ref.at[me], out_ref.at[me], local_sem)
        mine.start()
        copies = []
        for k in range(1, N_DEV):
            px = 1 - x if k & 4 else x
            py = 1 - y if k & 2 else y
            pc = 1 - c if k & 1 else c
            peer = 4 * px + 2 * py + pc
            copies.append(pltpu.make_async_remote_copy(
                src_ref=p_ref.at[peer], dst_ref=out_ref.at[me],
                send_sem=send_sems.at[k - 1], recv_sem=recv_sems.at[k - 1],
                device_id=(px, py, pc), device_id_type=MESH))
        for cp in copies:
            cp.start()
        for cp in copies:
            cp.wait()
        mine.wait()

    return pl.pallas_call(
        body, out_shape=jax.ShapeDtypeStruct(p.shape, p.dtype),
        in_specs=[HBM_SPEC], out_specs=HBM_SPEC,
        scratch_shapes=[pltpu.SemaphoreType.DMA((7,)), pltpu.SemaphoreType.DMA((7,)),
                        pltpu.SemaphoreType.DMA],
        name=name,
    )(p)


def _sum_parts(parts, *, name):
    n, R, C = parts.shape
    rb = _pick(R, 512) if R % LANES == 0 else R

    def body(p_ref, o_ref):
        acc = p_ref[0].astype(F32)
        for k in range(1, n):
            acc = acc + p_ref[k].astype(F32)
        o_ref[...] = acc

    return pl.pallas_call(
        body, grid=(R // rb,), in_specs=[pl.BlockSpec((n, rb, C), lambda i: (0, i, 0))],
        out_specs=pl.BlockSpec((rb, C), lambda i: (i, 0)),
        out_shape=jax.ShapeDtypeStruct((R, C), F32),
        compiler_params=_params(("parallel",)), name=name,
    )(parts)


def _adamw(w, m, v, parts, *, name):
    n, R, C = parts.shape
    rb = _pick(R, 256) if R % LANES == 0 else R

    def body(w_ref, m_ref, v_ref, p_ref, g_ref, d_ref, nm_ref, nv_ref):
        g = p_ref[0].astype(F32)
        for k in range(1, n):
            g = g + p_ref[k].astype(F32)
        mn = ADAM_B1 * m_ref[...] + (1.0 - ADAM_B1) * g
        vn = ADAM_B2 * v_ref[...] + (1.0 - ADAM_B2) * jnp.square(g)
        m_hat = mn / (1.0 - ADAM_B1 ** ADAM_STEP)
        v_hat = vn / (1.0 - ADAM_B2 ** ADAM_STEP)
        g_ref[...] = g
        d_ref[...] = -ADAM_LR * (m_hat / (jnp.sqrt(v_hat) + ADAM_EPS) + ADAM_WD * w_ref[...])
        nm_ref[...] = mn
        nv_ref[...] = vn

    row = pl.BlockSpec((rb, C), lambda i: (i, 0))
    return pl.pallas_call(
        body, grid=(R // rb,), in_specs=[row, row, row, pl.BlockSpec((n, rb, C), lambda i: (0, i, 0))],
        out_specs=[row] * 4, out_shape=[jax.ShapeDtypeStruct((R, C), F32)] * 4,
        compiler_params=_params(("parallel",)), name=name,
    )(w, m, v, parts)


BIG = (("a_w_in", 2), ("a_w_out", 1), ("kv_w", 0), ("b_w_q", 2), ("b_w_o", 2), ("ffn_w_up", 2), ("ffn_w_down", 1))
SMALL_SHARDED = (("a_ln_g", 1), ("a_ln_b", 1), ("ffn_conv_w", 2), ("ln_g", 2), ("ln_b", 2))
REPLICATED = ("a_w_s", "a_b_s", "rel_table", "ffn_conv_b")
BIG_COLS = 1024
SMALL_COLS = LANES


def _pack_local(arrs, cols, dtype, row_mult, lead=0):
    lshape = arrs[0].shape[:lead]
    flat = jnp.concatenate([a.reshape(*lshape, -1).astype(dtype) for a in arrs], axis=-1)
    rows = -(-flat.shape[-1] // cols)
    rows = -(-rows // row_mult) * row_mult
    flat = jnp.pad(flat, [(0, 0)] * lead + [(0, rows * cols - flat.shape[-1])])
    return flat.reshape(*lshape, rows, cols)


def _unpack_local(packed, shapes):
    lead = packed.shape[:-2]
    flat = packed.reshape(*lead, -1)
    out, off = [], 0
    for shp in shapes:
        n = int(np.prod(shp))
        out.append(flat[..., off:off + n].reshape(*lead, *shp))
        off += n
    return out


def _merge_shards(stacked, axis):
    a = jnp.moveaxis(stacked, 0, axis)
    shp = list(a.shape)
    return a.reshape(shp[:axis] + [shp[axis] * shp[axis + 1]] + shp[axis + 2:])


def _split_shards(full, axis):
    shp = list(full.shape)
    a = full.reshape(shp[:axis] + [N_DEV, shp[axis] // N_DEV] + shp[axis + 1:])
    return jnp.moveaxis(a, axis, 0)


def kernel(x, a_w_in, a_ln_g, a_ln_b, a_w_s, a_b_s, a_w_out, kv_w, b_w_q, b_w_o, rel_table, ffn_w_up, ffn_conv_w, ffn_conv_b, ffn_w_down, ln_g, ln_b, loss_target, m_a_w_in, m_a_ln_g, m_a_ln_b, m_a_w_s, m_a_b_s, m_a_w_out, m_kv_w, m_b_w_q, m_b_w_o, m_rel_table, m_ffn_w_up, m_ffn_conv_w, m_ffn_conv_b, m_ffn_w_down, m_ln_g, m_ln_b, v_a_w_in, v_a_ln_g, v_a_ln_b, v_a_w_s, v_a_b_s, v_a_w_out, v_kv_w, v_b_w_q, v_b_w_o, v_rel_table, v_ffn_w_up, v_ffn_conv_w, v_ffn_conv_b, v_ffn_w_down, v_ln_g, v_ln_b):
    names = ["a_w_in", "a_ln_g", "a_ln_b", "a_w_s", "a_b_s", "a_w_out", "kv_w", "b_w_q", "b_w_o", "rel_table",
             "ffn_w_up", "ffn_conv_w", "ffn_conv_b", "ffn_w_down", "ln_g", "ln_b"]
    w = dict(zip(names, (a_w_in, a_ln_g, a_ln_b, a_w_s, a_b_s, a_w_out, kv_w, b_w_q, b_w_o, rel_table,
                         ffn_w_up, ffn_conv_w, ffn_conv_b, ffn_w_down, ln_g, ln_b)))
    m = dict(zip(names, (m_a_w_in, m_a_ln_g, m_a_ln_b, m_a_w_s, m_a_b_s, m_a_w_out, m_kv_w, m_b_w_q, m_b_w_o,
                         m_rel_table, m_ffn_w_up, m_ffn_conv_w, m_ffn_conv_b, m_ffn_w_down, m_ln_g, m_ln_b)))
    v = dict(zip(names, (v_a_w_in, v_a_ln_g, v_a_ln_b, v_a_w_s, v_a_b_s, v_a_w_out, v_kv_w, v_b_w_q, v_b_w_o,
                         v_rel_table, v_ffn_w_up, v_ffn_conv_w, v_ffn_conv_b, v_ffn_w_down, v_ln_g, v_ln_b)))
    big_names = [n for n, _ in BIG]
    small_names = [n for n, _ in SMALL_SHARDED]
    big_shapes = [w[n].shape for n in big_names]
    small_shapes = [w[n].shape for n in small_names]
    rep_shapes = [w[n].shape for n in REPLICATED]

    big_all = _all_gather(_pack_local([w[n] for n in big_names], BIG_COLS, BF16, LANES), name="gather_big")
    small_all = _all_gather(_pack_local([w[n] for n in small_names], SMALL_COLS, F32, 8), name="gather_small")
    W = {n: w[n] for n in REPLICATED}
    for (n, ax), st in zip(BIG, _unpack_local(big_all, big_shapes)):
        W[n] = _merge_shards(st, ax)
    for (n, ax), st in zip(SMALL_SHARDED, _unpack_local(small_all, small_shapes)):
        W[n] = _merge_shards(st, ax)

    loss, grad_x, grads = _local_step(x[0], loss_target[0], W)
    loss = lax.psum(loss, ("x", "y", "c"))

    send = _pack_local([_split_shards(grads[n], ax) for n, ax in BIG], BIG_COLS, BF16, LANES, lead=1)
    parts_big = _all_to_all(send, name="exchange_big")
    gb, db, mb, vb = _adamw(_pack_local([w[n] for n in big_names], BIG_COLS, F32, LANES),
                            _pack_local([m[n] for n in big_names], BIG_COLS, F32, LANES),
                            _pack_local([v[n] for n in big_names], BIG_COLS, F32, LANES),
                            parts_big, name="adamw_big")

    small_full = [_split_shards(grads[n], ax) for n, ax in SMALL_SHARDED]
    small_pack = _pack_local(small_full, SMALL_COLS, F32, 8, lead=1)
    srows = small_pack.shape[1]
    small_pack = small_pack.reshape(N_DEV * srows, SMALL_COLS)
    rep_pack = _pack_local([grads[n] for n in REPLICATED], SMALL_COLS, F32, 8)
    gsum = _sum_parts(_all_gather(jnp.concatenate([small_pack, rep_pack], axis=0), name="gather_small_grads"),
                      name="sum_small_grads")
    g_small = lax.dynamic_slice_in_dim(gsum, _my_index() * srows, srows, axis=0)
    g_rep = gsum[N_DEV * srows:]
    gs_in = jnp.concatenate([g_small, g_rep], axis=0)[None]
    pack_sr = lambda d: jnp.concatenate([_pack_local([d[n] for n in small_names], SMALL_COLS, F32, 8),
                                         _pack_local([d[n] for n in REPLICATED], SMALL_COLS, F32, 8)], axis=0)
    gs, ds, ms, vs = _adamw(pack_sr(w), pack_sr(m), pack_sr(v), gs_in, name="adamw_small")

    def unpack_all(big, small):
        out = dict(zip(big_names, _unpack_local(big, big_shapes)))
        out.update(zip(small_names, _unpack_local(small[:srows], small_shapes)))
        out.update(zip(REPLICATED, _unpack_local(small[srows:], rep_shapes)))
        return [out[n] for n in names]

    return (loss, grad_x[None], *unpack_all(gb, gs), *unpack_all(db, ds), *unpack_all(mb, ms),
            *unpack_all(vb, vs))
```

```python
import math

import numpy as np
import jax
import jax.numpy as jnp
from jax import lax
from jax.experimental import pallas as pl
from jax.experimental.pallas import tpu as pltpu

F32 = jnp.float32
BF16 = jnp.bfloat16
MESH = pl.DeviceIdType.MESH

N_DEV = 8
DEPTH = 4
N_A = 2
CHUNK = 128
BLK = 128
HEAD_DIM = 64
DILATED_GROUPS = ((128, 1), (512, 4), (2048, 16))
N_GROUPS = 3
REL_BUCKETS = 32
REL_MAX_DIST = 2048
ALPHA = (2 * DEPTH) ** 0.25
LN_EPS = 1e-5
NEG = -1e30
ADAM_LR = 0.001
ADAM_B1 = 0.9
ADAM_B2 = 0.999
ADAM_EPS = 1e-08
ADAM_WD = 0.01
ADAM_STEP = 10

LANES = 128
VMEM_LIMIT = 56 * 1024 * 1024
MM_TILE_CAP = 1408
INV_SQRT2 = 1.0 / math.sqrt(2.0)
INV_SQRT_2PI = 1.0 / math.sqrt(2.0 * math.pi)


def _pick(n, cap):
    best = None
    for t in range(LANES, min(n, cap) + 1, LANES):
        if n % t == 0:
            best = t
    return best if best is not None else n


def _params(sem):
    return pltpu.CompilerParams(dimension_semantics=sem, vmem_limit_bytes=VMEM_LIMIT)


def _gelu(x):
    return 0.5 * x * (1.0 + lax.erf(x * INV_SQRT2))


def _gelu_grad(x):
    return 0.5 * (1.0 + lax.erf(x * INV_SQRT2)) + x * jnp.exp(-0.5 * x * x) * INV_SQRT_2PI


def _mm(a, b, *, ta=False, tb=False, out_dtype=F32, scale=None, name):
    if ta:
        K, M = a.shape
    else:
        M, K = a.shape
    if tb:
        N, Kb = b.shape
    else:
        Kb, N = b.shape
    assert K == Kb, (a.shape, b.shape, ta, tb)
    tm, tn, tk = _pick(M, MM_TILE_CAP), _pick(N, MM_TILE_CAP), _pick(K, MM_TILE_CAP)
    nk = K // tk
    dn = (((0 if ta else 1,), (1 if tb else 0,)), ((), ()))

    def body(a_ref, b_ref, o_ref, acc_ref):
        k = pl.program_id(2)
        part = lax.dot_general(a_ref[...].astype(BF16), b_ref[...].astype(BF16), dn,
                               preferred_element_type=F32)

        @pl.when(k == 0)
        def _():
            acc_ref[...] = part

        @pl.when(k > 0)
        def _():
            acc_ref[...] += part

        @pl.when(k == nk - 1)
        def _():
            r = acc_ref[...]
            if scale is not None:
                r = r * scale
            o_ref[...] = r.astype(out_dtype)

    a_spec = (pl.BlockSpec((tk, tm), lambda i, j, k: (k, i)) if ta
              else pl.BlockSpec((tm, tk), lambda i, j, k: (i, k)))
    b_spec = (pl.BlockSpec((tn, tk), lambda i, j, k: (j, k)) if tb
              else pl.BlockSpec((tk, tn), lambda i, j, k: (k, j)))
    return pl.pallas_call(
        body, grid=(M // tm, N // tn, nk), in_specs=[a_spec, b_spec],
        out_specs=pl.BlockSpec((tm, tn), lambda i, j, k: (i, j)),
        out_shape=jax.ShapeDtypeStruct((M, N), out_dtype),
        scratch_shapes=[pltpu.VMEM((tm, tn), F32)],
        compiler_params=_params(("parallel", "parallel", "arbitrary")), name=name,
    )(a, b)


def _add_ln_fwd(x, h, g, b, *, name):
    T, D = x.shape
    rb = _pick(T, 512)

    def body(x_ref, h_ref, g_ref, b_ref, o_ref, ob_ref):
        pre = ALPHA * x_ref[...] + h_ref[...]
        mu = jnp.mean(pre, axis=1, keepdims=True)
        cen = pre - mu
        var = jnp.mean(cen * cen, axis=1, keepdims=True)
        y = cen * lax.rsqrt(var + LN_EPS) * g_ref[...] + b_ref[...]
        o_ref[...] = y
        ob_ref[...] = y.astype(BF16)

    row = pl.BlockSpec((rb, D), lambda i: (i, 0))
    vec = pl.BlockSpec((1, D), lambda i: (0, 0))
    return pl.pallas_call(
        body, grid=(T // rb,), in_specs=[row, row, vec, vec], out_specs=[row, row],
        out_shape=[jax.ShapeDtypeStruct((T, D), F32), jax.ShapeDtypeStruct((T, D), BF16)],
        compiler_params=_params(("parallel",)), name=name,
    )(x, h, g.reshape(1, D), b.reshape(1, D))


def _add_ln_bwd(x, h, g, terms, *, name):
    T, D = x.shape
    rb = _pick(T, 512)
    coefs = [c for c, _ in terms]
    nt = len(terms)

    def body(*refs):
        x_ref, h_ref, g_ref = refs[:3]
        t_refs = refs[3:3 + nt]
        dp_ref, dpb_ref, dg_ref, db_ref = refs[3 + nt:]
        dy = None
        for c, r in zip(coefs, t_refs):
            v = r[...] if c == 1.0 else c * r[...]
            dy = v if dy is None else dy + v
        pre = ALPHA * x_ref[...] + h_ref[...]
        mu = jnp.mean(pre, axis=1, keepdims=True)
        cen = pre - mu
        var = jnp.mean(cen * cen, axis=1, keepdims=True)
        rstd = lax.rsqrt(var + LN_EPS)
        xhat = cen * rstd
        dxh = dy * g_ref[...]
        m1 = jnp.mean(dxh, axis=1, keepdims=True)
        m2 = jnp.mean(dxh * xhat, axis=1, keepdims=True)
        dpre = rstd * (dxh - m1 - xhat * m2)
        dp_ref[...] = dpre
        dpb_ref[...] = dpre.astype(BF16)
        dg = jnp.sum(dy * xhat, axis=0, keepdims=True)
        db = jnp.sum(dy, axis=0, keepdims=True)

        @pl.when(pl.program_id(0) == 0)
        def _():
            dg_ref[...] = dg
            db_ref[...] = db

        @pl.when(pl.program_id(0) > 0)
        def _():
            dg_ref[...] += dg
            db_ref[...] += db

    row = pl.BlockSpec((rb, D), lambda i: (i, 0))
    vec = pl.BlockSpec((1, D), lambda i: (0, 0))
    return pl.pallas_call(
        body, grid=(T // rb,), in_specs=[row, row, vec] + [row] * nt,
        out_specs=[row, row, vec, vec],
        out_shape=[jax.ShapeDtypeStruct((T, D), F32), jax.ShapeDtypeStruct((T, D), BF16),
                   jax.ShapeDtypeStruct((1, D), F32), jax.ShapeDtypeStruct((1, D), F32)],
        compiler_params=_params(("arbitrary",)), name=name,
    )(x, h, g.reshape(1, D), *[a for _, a in terms])


def _lincomb(terms, out_dtype, *, name):
    R, C = terms[0][1].shape
    rb = _pick(R, 512)
    coefs = [c for c, _ in terms]
    nt = len(terms)

    def body(*refs):
        acc = None
        for c, r in zip(coefs, refs[:nt]):
            v = r[...].astype(F32)
            v = v if c == 1.0 else c * v
            acc = v if acc is None else acc + v
        refs[nt][...] = acc.astype(out_dtype)

    row = pl.BlockSpec((rb, C), lambda i: (i, 0))
    return pl.pallas_call(
        body, grid=(R // rb,), in_specs=[row] * nt, out_specs=row,
        out_shape=jax.ShapeDtypeStruct((R, C), out_dtype),
        compiler_params=_params(("parallel",)), name=name,
    )(*[a for _, a in terms])


def _loss_grad(y, tgt, *, name):
    T, D = y.shape
    rb = _pick(T, 512)

    def body(y_ref, t_ref, dy_ref, l_ref):
        err = y_ref[...] - t_ref[...]
        dy_ref[...] = err * (1.0 / D)
        part = jnp.sum(jnp.sum(err * err, axis=1, keepdims=True), axis=0, keepdims=True) * (0.5 / D)
        part = jnp.broadcast_to(part, (1, LANES))

        @pl.when(pl.program_id(0) == 0)
        def _():
            l_ref[...] = part

        @pl.when(pl.program_id(0) > 0)
        def _():
            l_ref[...] += part

    row = pl.BlockSpec((rb, D), lambda i: (i, 0))
    return pl.pallas_call(
        body, grid=(T // rb,), in_specs=[row, row],
        out_specs=[row, pl.BlockSpec((1, LANES), lambda i: (0, 0))],
        out_shape=[jax.ShapeDtypeStruct((T, D), F32), jax.ShapeDtypeStruct((1, LANES), F32)],
        compiler_params=_params(("arbitrary",)), name=name,
    )(y, tgt)


def _sgu_fwd(zp, ln_g, ln_b, ws, bst, *, name):
    T, E2 = zp.shape
    E = E2 // 2
    G = ws.shape[0]
    cg = E // G
    rb = 2 * CHUNK

    def body(z_ref, g_ref, b_ref, ws_ref, bs_ref, y_ref):
        u = _gelu(z_ref[:, :E])
        v = _gelu(z_ref[:, E:])
        mu = jnp.mean(v, axis=1, keepdims=True)
        cen = v - mu
        var = jnp.mean(cen * cen, axis=1, keepdims=True)
        vn = (cen * lax.rsqrt(var + LN_EPS) * g_ref[...] + b_ref[...]).astype(BF16)
        for ci in range(rb // CHUNK):
            rows = slice(ci * CHUNK, (ci + 1) * CHUNK)
            for gi in range(G):
                cols = slice(gi * cg, (gi + 1) * cg)
                sv = jnp.dot(ws_ref[gi], vn[rows, cols], preferred_element_type=F32)
                sv = sv + bs_ref[:, gi:gi + 1]
                y_ref[rows, cols] = (u[rows, cols] * sv).astype(BF16)

    return pl.pallas_call(
        body, grid=(T // rb,),
        in_specs=[pl.BlockSpec((rb, E2), lambda i: (i, 0)),
                  pl.BlockSpec((1, E), lambda i: (0, 0)), pl.BlockSpec((1, E), lambda i: (0, 0)),
                  pl.BlockSpec((G, CHUNK, CHUNK), lambda i: (0, 0, 0)),
                  pl.BlockSpec((CHUNK, G), lambda i: (0, 0))],
        out_specs=pl.BlockSpec((rb, E), lambda i: (i, 0)),
        out_shape=jax.ShapeDtypeStruct((T, E), BF16),
        compiler_params=_params(("parallel",)), name=name,
    )(zp, ln_g.reshape(1, E), ln_b.reshape(1, E), ws, bst)


def _sgu_bwd(zp, dy, ln_g, ln_b, ws, wst, bst, *, name):
    T, E2 = zp.shape
    E = E2 // 2
    G = ws.shape[0]
    cg = E // G
    rb = CHUNK
    nsteps = T // rb

    def body(z_ref, dy_ref, g_ref, b_ref, ws_ref, wst_ref, bs_ref,
             dz_ref, dg_ref, db_ref, dws_ref, dbs_ref, dsv_acc):
        step = pl.program_id(0)

        @pl.when(step == 0)
        def _():
            dg_ref[...] = jnp.zeros_like(dg_ref)
            db_ref[...] = jnp.zeros_like(db_ref)
            dws_ref[...] = jnp.zeros_like(dws_ref)
            dsv_acc[...] = jnp.zeros_like(dsv_acc)

        zu = z_ref[:, :E]
        zv = z_ref[:, E:]
        u = _gelu(zu)
        v = _gelu(zv)
        mu = jnp.mean(v, axis=1, keepdims=True)
        cen = v - mu
        var = jnp.mean(cen * cen, axis=1, keepdims=True)
        rstd = lax.rsqrt(var + LN_EPS)
        xhat = cen * rstd
        vn = (xhat * g_ref[...] + b_ref[...]).astype(BF16)
        dyv = dy_ref[...]
        dsv = dyv * u
        dsv_acc[...] += dsv
        dsvb = dsv.astype(BF16)
        tril = (lax.broadcasted_iota(jnp.int32, (CHUNK, CHUNK), 0)
                >= lax.broadcasted_iota(jnp.int32, (CHUNK, CHUNK), 1))
        du_parts = []
        dvn_parts = []
        for gi in range(G):
            cols = slice(gi * cg, (gi + 1) * cg)
            sv = jnp.dot(ws_ref[gi], vn[:, cols], preferred_element_type=F32) + bs_ref[:, gi:gi + 1]
            du_parts.append(dyv[:, cols] * sv)
            dvn_parts.append(jnp.dot(wst_ref[gi], dsvb[:, cols], preferred_element_type=F32))
            dw = lax.dot_general(dsvb[:, cols], vn[:, cols], (((1,), (1,)), ((), ())),
                                 preferred_element_type=F32)
            dws_ref[gi] += jnp.where(tril, dw, 0.0)
        du = jnp.concatenate(du_parts, axis=1)
        dvn = jnp.concatenate(dvn_parts, axis=1)
        dg_ref[...] += jnp.sum(dvn * xhat, axis=0, keepdims=True)
        db_ref[...] += jnp.sum(dvn, axis=0, keepdims=True)
        dxh = dvn * g_ref[...]
        m1 = jnp.mean(dxh, axis=1, keepdims=True)
        m2 = jnp.mean(dxh * xhat, axis=1, keepdims=True)
        dv = rstd * (dxh - m1 - xhat * m2)
        dz_ref[:, :E] = (du * _gelu_grad(zu)).astype(BF16)
        dz_ref[:, E:] = (dv * _gelu_grad(zv)).astype(BF16)

        @pl.when(step == nsteps - 1)
        def _():
            lane = lax.broadcasted_iota(jnp.int32, (CHUNK, LANES), 1)
            out = jnp.zeros((CHUNK, LANES), F32)
            for gi in range(G):
                s = jnp.sum(dsv_acc[:, gi * cg:(gi + 1) * cg], axis=1, keepdims=True)
                out = jnp.where(lane == gi, s, out)
            dbs_ref[...] = out

    vecE = pl.BlockSpec((1, E), lambda i: (0, 0))
    wspec = pl.BlockSpec((G, CHUNK, CHUNK), lambda i: (0, 0, 0))
    return pl.pallas_call(
        body, grid=(nsteps,),
        in_specs=[pl.BlockSpec((rb, E2), lambda i: (i, 0)), pl.BlockSpec((rb, E), lambda i: (i, 0)),
                  vecE, vecE, wspec, wspec, pl.BlockSpec((CHUNK, G), lambda i: (0, 0))],
        out_specs=[pl.BlockSpec((rb, E2), lambda i: (i, 0)), vecE, vecE, wspec,
                   pl.BlockSpec((CHUNK, LANES), lambda i: (0, 0))],
        out_shape=[jax.ShapeDtypeStruct((T, E2), BF16), jax.ShapeDtypeStruct((1, E), F32),
                   jax.ShapeDtypeStruct((1, E), F32), jax.ShapeDtypeStruct((G, CHUNK, CHUNK), F32),
                   jax.ShapeDtypeStruct((CHUNK, LANES), F32)],
        scratch_shapes=[pltpu.VMEM((CHUNK, E), F32)],
        compiler_params=_params(("arbitrary",)), name=name,
    )(zp, dy, ln_g.reshape(1, E), ln_b.reshape(1, E), ws, wst, bst)


def _shift_down(x, k, row):
    return jnp.where(row >= k, pltpu.roll(x, k, 0), 0.0)


def _shift_up(x, k, row, T):
    return jnp.where(row < T - k, pltpu.roll(x, T - k, 0), 0.0)


def _conv3(x, w_ref, b_ref, row):
    return (w_ref[0:1, :] * _shift_down(x, 2, row) + w_ref[1:2, :] * _shift_down(x, 1, row)
            + w_ref[2:3, :] * x + b_ref[...])


def _convgate_fwd(hh, cw, cb, *, name):
    T, F2 = hh.shape
    F = F2 // 2
    ns = F // LANES

    def body(a_ref, g_ref, wa_ref, wg_ref, ba_ref, bg_ref, o_ref):
        row = lax.broadcasted_iota(jnp.int32, (T, LANES), 0)
        ca = _conv3(a_ref[...], wa_ref, ba_ref, row)
        cgv = _conv3(g_ref[...], wg_ref, bg_ref, row)
        o_ref[...] = (_gelu(ca) * cgv).astype(BF16)

    sa = lambda r: pl.BlockSpec((r, LANES), lambda j: (0, j))
    sg = lambda r: pl.BlockSpec((r, LANES), lambda j: (0, j + ns))
    return pl.pallas_call(
        body, grid=(ns,), in_specs=[sa(T), sg(T), sa(3), sg(3), sa(1), sg(1)],
        out_specs=sa(T), out_shape=jax.ShapeDtypeStruct((T, F), BF16),
        compiler_params=_params(("parallel",)), name=name,
    )(hh, hh, cw, cw, cb, cb)


def _convgate_bwd(hh, dact, cw, cb, *, name):
    T, F2 = hh.shape
    F = F2 // 2
    ns = F // LANES

    def body(a_ref, g_ref, d_ref, wa_ref, wg_ref, ba_ref, bg_ref,
             da_ref, dg_ref, dwa_ref, dwg_ref, dba_ref, dbg_ref):
        row = lax.broadcasted_iota(jnp.int32, (T, LANES), 0)
        d = d_ref[...].astype(F32)
        ca = _conv3(a_ref[...], wa_ref, ba_ref, row)
        cgv = _conv3(g_ref[...], wg_ref, bg_ref, row)
        cdf = 0.5 * (1.0 + lax.erf(ca * INV_SQRT2))
        dca = d * cgv * (cdf + ca * jnp.exp(-0.5 * ca * ca) * INV_SQRT_2PI)
        dcg = d * (ca * cdf)
        for x_ref, w_ref, dc, dx_ref, dw_ref, db_ref in (
                (a_ref, wa_ref, dca, da_ref, dwa_ref, dba_ref),
                (g_ref, wg_ref, dcg, dg_ref, dwg_ref, dbg_ref)):
            x = x_ref[...]
            dx = (w_ref[2:3, :] * dc + w_ref[1:2, :] * _shift_up(dc, 1, row, T)
                  + w_ref[0:1, :] * _shift_up(dc, 2, row, T))
            dx_ref[...] = dx.astype(BF16)
            dw_ref[0:1, :] = jnp.sum(dc * _shift_down(x, 2, row), axis=0, keepdims=True)
            dw_ref[1:2, :] = jnp.sum(dc * _shift_down(x, 1, row), axis=0, keepdims=True)
            dw_ref[2:3, :] = jnp.sum(dc * x, axis=0, keepdims=True)
            db_ref[...] = jnp.sum(dc, axis=0, keepdims=True)

    sa = lambda r: pl.BlockSpec((r, LANES), lambda j: (0, j))
    sg = lambda r: pl.BlockSpec((r, LANES), lambda j: (0, j + ns))
    return pl.pallas_call(
        body, grid=(ns,), in_specs=[sa(T), sg(T), sa(T), sa(3), sg(3), sa(1), sg(1)],
        out_specs=[sa(T), sa(T), sa(3), sa(3), sa(1), sa(1)],
        out_shape=[jax.ShapeDtypeStruct((T, F), BF16), jax.ShapeDtypeStruct((T, F), BF16),
                   jax.ShapeDtypeStruct((3, F), F32), jax.ShapeDtypeStruct((3, F), F32),
                   jax.ShapeDtypeStruct((1, F), F32), jax.ShapeDtypeStruct((1, F), F32)],
        compiler_params=_params(("parallel",)), name=name,
    )(hh, hh, dact, cw, cw, cb, cb)


def _bucket_maps():
    iq = np.arange(BLK)[:, None]
    ik = np.arange(2 * BLK)[None, :]
    delta = iq + BLK - ik
    maps = []
    for win, dil in DILATED_GROUPS:
        n = np.clip(delta, 0, None) * dil
        max_exact = REL_BUCKETS // 2
        nf = np.maximum(n, 1).astype(np.float32)
        large = max_exact + (np.log(nf / np.float32(max_exact)) / np.float32(math.log(REL_MAX_DIST / max_exact))
                             * np.float32(REL_BUCKETS - max_exact)).astype(np.int32)
        large = np.minimum(large, REL_BUCKETS - 1)
        bucket = np.where(n < max_exact, n, large)
        valid = (delta >= 0) & (delta <= win // dil)
        maps.append(np.where(valid, bucket, -1).astype(np.int32))
    return np.stack(maps)


def _band_bias(rel_table, bmap, H, *, name):
    def body(t_ref, m_ref, o_ref):
        g = pl.program_id(0)
        bm = m_ref[0]
        for h in range(H):
            acc = jnp.full((BLK, 2 * BLK), NEG, F32)
            for b in range(REL_BUCKETS):
                acc = jnp.where(bm == b, t_ref[b, g * H + h], acc)
            o_ref[0, h] = acc

    return pl.pallas_call(
        body, grid=(N_GROUPS,),
        in_specs=[pl.BlockSpec(memory_space=pltpu.SMEM),
                  pl.BlockSpec((1, BLK, 2 * BLK), lambda g: (g, 0, 0))],
        out_specs=pl.BlockSpec((1, H, BLK, 2 * BLK), lambda g: (g, 0, 0, 0)),
        out_shape=jax.ShapeDtypeStruct((N_GROUPS, H, BLK, 2 * BLK), F32),
        compiler_params=_params(("parallel",)), name=name,
    )(rel_table, bmap)


def _band_bias_bwd(dbias, bmap, H, *, name):
    def body(d_ref, m_ref, o_ref):
        bm = m_ref[0]
        rowi = lax.broadcasted_iota(jnp.int32, (REL_BUCKETS, LANES), 0)
        lane = lax.broadcasted_iota(jnp.int32, (REL_BUCKETS, LANES), 1)
        out = jnp.zeros((REL_BUCKETS, LANES), F32)
        for h in range(H):
            dv = d_ref[0, h]
            for b in range(REL_BUCKETS):
                s = jnp.sum(jnp.sum(jnp.where(bm == b, dv, 0.0), axis=1, keepdims=True),
                            axis=0, keepdims=True)
                out = jnp.where((rowi == b) & (lane == h), s, out)
        o_ref[0] = out

    return pl.pallas_call(
        body, grid=(N_GROUPS,),
        in_specs=[pl.BlockSpec((1, H, BLK, 2 * BLK), lambda g: (g, 0, 0, 0)),
                  pl.BlockSpec((1, BLK, 2 * BLK), lambda g: (g, 0, 0))],
        out_specs=pl.BlockSpec((1, REL_BUCKETS, LANES), lambda g: (g, 0, 0)),
        out_shape=jax.ShapeDtypeStruct((N_GROUPS, REL_BUCKETS, LANES), F32),
        compiler_params=_params(("parallel",)), name=name,
    )(dbias, bmap)


def _head_masks():
    lane = lax.broadcasted_iota(jnp.int32, (BLK, LANES), 1)
    return (lane < HEAD_DIM, lane >= HEAD_DIM)


def _attn_fwd(q, kv, bias, gi, *, name):
    T = q.shape[0]
    HD = kv.shape[1] // 2
    d = DILATED_GROUPS[gi][1]
    S = T // d
    NB = S // BLK
    H = HD // HEAD_DIM
    qv = q.reshape(S, d * 3 * HD)
    kvv = kv.reshape(S, d * 2 * HD)

    def body(q_ref, kp_ref, kc_ref, vp_ref, vc_ref, b_ref, o_ref, l_ref):
        n = pl.program_id(1)
        col = lax.broadcasted_iota(jnp.int32, (BLK, 2 * BLK), 1)
        first = (n == 0) & (col < BLK)
        hm = _head_masks()
        for p in range(HD // LANES):
            sl = slice(p * LANES, (p + 1) * LANES)
            qp = q_ref[:, sl]
            kc = jnp.concatenate([kp_ref[:, sl], kc_ref[:, sl]], axis=0)
            vc = jnp.concatenate([vp_ref[:, sl], vc_ref[:, sl]], axis=0)
            outs = []
            lses = []
            for hh in range(2):
                qm = jnp.where(hm[hh], qp, jnp.zeros_like(qp))
                s = lax.dot_general(qm, kc, (((1,), (1,)), ((), ())), preferred_element_type=F32)
                s = jnp.where(first, NEG, s + b_ref[2 * p + hh])
                m = jnp.max(s, axis=1, keepdims=True)
                e = jnp.exp(s - m)
                den = jnp.sum(e, axis=1, keepdims=True)
                outs.append(jnp.dot((e / den).astype(BF16), vc, preferred_element_type=F32))
                lses.append(m + jnp.log(den))
            o_ref[:, sl] = jnp.where(hm[0], outs[0], outs[1])
            l_ref[:, sl] = jnp.where(hm[0], lses[0], lses[1])

    blk = lambda f: pl.BlockSpec((BLK, HD), f)
    prev = lambda n: jnp.maximum(n - 1, 0)
    return pl.pallas_call(
        body, grid=(d, NB),
        in_specs=[blk(lambda r, n: (n, r * 3 + gi)),
                  blk(lambda r, n: (prev(n), r * 2)), blk(lambda r, n: (n, r * 2)),
                  blk(lambda r, n: (prev(n), r * 2 + 1)), blk(lambda r, n: (n, r * 2 + 1)),
                  pl.BlockSpec((H, BLK, 2 * BLK), lambda r, n: (0, 0, 0))],
        out_specs=[blk(lambda r, n: (n, r)), blk(lambda r, n: (n, r))],
        out_shape=[jax.ShapeDtypeStruct((S, d * HD), F32), jax.ShapeDtypeStruct((S, d * HD), F32)],
        compiler_params=_params(("parallel", "parallel")), name=name,
    )(qv, kvv, kvv, kvv, kvv, bias)


def _attn_combine(os, ls, *, name):
    T, HD = os[0].shape
    rb = _pick(T, 512)

    def body(o0, o1, o2, l0, l1, l2, o_ref, ob_ref, l_ref):
        la, lb, lc = l0[...], l1[...], l2[...]
        m = jnp.maximum(jnp.maximum(la, lb), lc)
        L = m + jnp.log(jnp.exp(la - m) + jnp.exp(lb - m) + jnp.exp(lc - m))
        o = jnp.exp(la - L) * o0[...] + jnp.exp(lb - L) * o1[...] + jnp.exp(lc - L) * o2[...]
        o_ref[...] = o
        ob_ref[...] = o.astype(BF16)
        l_ref[...] = L

    row = pl.BlockSpec((rb, HD), lambda i: (i, 0))
    return pl.pallas_call(
        body, grid=(T // rb,), in_specs=[row] * 6, out_specs=[row] * 3,
        out_shape=[jax.ShapeDtypeStruct((T, HD), F32), jax.ShapeDtypeStruct((T, HD), BF16),
                   jax.ShapeDtypeStruct((T, HD), F32)],
        compiler_params=_params(("parallel",)), name=name,
    )(*[a.reshape(T, HD) for a in os], *[a.reshape(T, HD) for a in ls])


def _attn_bwd(q, kv, bias, do, o, L, gi, *, name):
    T = q.shape[0]
    HD = kv.shape[1] // 2
    d = DILATED_GROUPS[gi][1]
    S = T // d
    NB = S // BLK
    H = HD // HEAD_DIM
    qv = q.reshape(S, d * 3 * HD)
    kvv = kv.reshape(S, d * 2 * HD)
    dov, ov, Lv = (a.reshape(S, d * HD) for a in (do, o, L))

    def body(q_ref, kp_ref, kc_ref, vp_ref, vc_ref, b_ref, do_ref, o_ref, L_ref,
             dq_ref, dk_ref, dv_ref, db_ref, ck_ref, cv_ref):
        r = pl.program_id(0)
        n = pl.program_id(1)

        @pl.when((r == 0) & (n == 0))
        def _():
            db_ref[...] = jnp.zeros_like(db_ref)

        @pl.when(n == 0)
        def _():
            ck_ref[...] = jnp.zeros_like(ck_ref)
            cv_ref[...] = jnp.zeros_like(cv_ref)

        @pl.when(n < NB)
        def _():
            col = lax.broadcasted_iota(jnp.int32, (BLK, 2 * BLK), 1)
            first = (n == 0) & (col < BLK)
            hm = _head_masks()
            for p in range(HD // LANES):
                sl = slice(p * LANES, (p + 1) * LANES)
                qp = q_ref[:, sl]
                kc = jnp.concatenate([kp_ref[:, sl], kc_ref[:, sl]], axis=0)
                vc = jnp.concatenate([vp_ref[:, sl], vc_ref[:, sl]], axis=0)
                dop = do_ref[:, sl]
                dob = dop.astype(BF16)
                prod = dop * o_ref[:, sl]
                Lp = L_ref[:, sl]
                dq_parts = []
                dkc = None
                dvc = None
                for hh in range(2):
                    qm = jnp.where(hm[hh], qp, jnp.zeros_like(qp))
                    dom = jnp.where(hm[hh], dob, jnp.zeros_like(dob))
                    s = lax.dot_general(qm, kc, (((1,), (1,)), ((), ())), preferred_element_type=F32)
                    s = jnp.where(first, NEG, s + b_ref[2 * p + hh])
                    lse = Lp[:, hh * HEAD_DIM:hh * HEAD_DIM + 1]
                    pr = jnp.exp(s - lse)
                    dp = lax.dot_general(dom, vc, (((1,), (1,)), ((), ())), preferred_element_type=F32)
                    delta = jnp.sum(jnp.where(hm[hh], prod, 0.0), axis=1, keepdims=True)
                    ds = pr * (dp - delta)
                    db_ref[2 * p + hh] += ds
                    dsb = ds.astype(BF16)
                    dq_parts.append(jnp.dot(dsb, kc, preferred_element_type=F32))
                    dkh = lax.dot_general(dsb, qm, (((0,), (0,)), ((), ())), preferred_element_type=F32)
                    dvh = lax.dot_general(pr.astype(BF16), dom, (((0,), (0,)), ((), ())),
                                          preferred_element_type=F32)
                    dkc = dkh if dkc is None else dkc + dkh
                    dvc = dvh if dvc is None else dvc + dvh
                dq = jnp.where(hm[0], dq_parts[0], dq_parts[1])
                dq_ref[:, sl] = (dq * (HEAD_DIM ** -0.5)).astype(BF16)
                dk_ref[:, sl] = ck_ref[:, sl] + dkc[:BLK]
                dv_ref[:, sl] = cv_ref[:, sl] + dvc[:BLK]
                ck_ref[:, sl] = dkc[BLK:]
                cv_ref[:, sl] = dvc[BLK:]

        @pl.when(n == NB)
        def _():
            dk_ref[...] = ck_ref[...]
            dv_ref[...] = cv_ref[...]

    blk = lambda f: pl.BlockSpec((BLK, HD), f)
    cur = lambda n: jnp.minimum(n, NB - 1)
    prev = lambda n: jnp.maximum(jnp.minimum(n, NB - 1) - 1, 0)
    lag = lambda n: jnp.maximum(n - 1, 0)
    return pl.pallas_call(
        body, grid=(d, NB + 1),
        in_specs=[blk(lambda r, n: (cur(n), r * 3 + gi)),
                  blk(lambda r, n: (prev(n), r * 2)), blk(lambda r, n: (cur(n), r * 2)),
                  blk(lambda r, n: (prev(n), r * 2 + 1)), blk(lambda r, n: (cur(n), r * 2 + 1)),
                  pl.BlockSpec((H, BLK, 2 * BLK), lambda r, n: (0, 0, 0)),
                  blk(lambda r, n: (cur(n), r)), blk(lambda r, n: (cur(n), r)),
                  blk(lambda r, n: (cur(n), r))],
        out_specs=[blk(lambda r, n: (cur(n), r)), blk(lambda r, n: (lag(n), r)),
                   blk(lambda r, n: (lag(n), r)),
                   pl.BlockSpec((H, BLK, 2 * BLK), lambda r, n: (0, 0, 0))],
        out_shape=[jax.ShapeDtypeStruct((S, d * HD), BF16), jax.ShapeDtypeStruct((S, d * HD), F32),
                   jax.ShapeDtypeStruct((S, d * HD), F32),
                   jax.ShapeDtypeStruct((H, BLK, 2 * BLK), F32)],
        scratch_shapes=[pltpu.VMEM((BLK, HD), F32), pltpu.VMEM((BLK, HD), F32)],
        compiler_params=_params(("arbitrary", "arbitrary")), name=name,
    )(qv, kvv, kvv, kvv, kvv, bias, dov, ov, Lv)


def _local_step(x, tgt, W):
    T, D = x.shape
    HD = W["kv_w"].shape[1] // 2
    H = HD // HEAD_DIM
    G = W["a_w_s"].shape[1]
    assert T % (DILATED_GROUPS[-1][1] * BLK) == 0

    tril = jnp.tril(jnp.ones((CHUNK, CHUNK), F32))
    bmap = jnp.asarray(_bucket_maps())
    bias = _band_bias(W["rel_table"], bmap, H, name="band_bias")

    saved = []
    xc, xcb = x, x.astype(BF16)
    kvb = None
    for i in range(DEPTH):
        s = {"x": xc, "xb": xcb}
        if i < N_A:
            ws_m = W["a_w_s"][i] * tril
            s["ws"] = ws_m.astype(BF16)
            s["wst"] = jnp.swapaxes(ws_m, 1, 2).astype(BF16)
            s["bst"] = W["a_b_s"][i].T
            s["zp"] = _mm(xcb, W["a_w_in"][i], name=f"a_in_{i}")
            s["y"] = _sgu_fwd(s["zp"], W["a_ln_g"][i], W["a_ln_b"][i], s["ws"], s["bst"], name=f"sgu_fwd_{i}")
            s["h"] = _mm(s["y"], W["a_w_out"][i], name=f"a_out_{i}")
        else:
            j = i - N_A
            if kvb is None:
                kvb = _mm(xcb, W["kv_w"], out_dtype=BF16, name="kv_proj")
            s["q"] = _mm(xcb, W["b_w_q_t"][j], tb=True, out_dtype=BF16, scale=HEAD_DIM ** -0.5,
                         name=f"q_proj_{j}")
            os, ls = [], []
            for gi in range(N_GROUPS):
                o_g, l_g = _attn_fwd(s["q"], kvb, bias[gi], gi, name=f"attn_fwd_{j}_{gi}")
                os.append(o_g)
                ls.append(l_g)
            s["o"], s["ob"], s["L"] = _attn_combine(os, ls, name=f"attn_mix_{j}")
            s["h"] = _mm(s["ob"], W["b_w_o"][j], name=f"o_proj_{j}")
        s["x1"], s["x1b"] = _add_ln_fwd(xc, s["h"], W["ln_g"][i, 0], W["ln_b"][i, 0], name=f"ln1_fwd_{i}")
        s["hh"] = _mm(s["x1b"], W["ffn_w_up_t"][i], tb=True, name=f"ffn_up_{i}")
        s["cw"] = W["ffn_conv_w"][i]
        s["cb"] = W["ffn_conv_b"][i].reshape(1, -1)
        s["act"] = _convgate_fwd(s["hh"], s["cw"], s["cb"], name=f"convgate_fwd_{i}")
        s["f"] = _mm(s["act"], W["ffn_w_down"][i], name=f"ffn_down_{i}")
        xc, xcb = _add_ln_fwd(s["x1"], s["f"], W["ln_g"][i, 1], W["ln_b"][i, 1], name=f"ln2_fwd_{i}")
        saved.append(s)

    dy, lossv = _loss_grad(xc, tgt, name="loss_grad")
    loss = lossv[0, 0]

    gl = {k: [None] * DEPTH for k in ("ffn_w_up_t", "ffn_conv_w", "ffn_conv_b", "ffn_w_down", "ln_g", "ln_b")}
    ga = {k: [None] * N_A for k in ("a_w_in", "a_ln_g", "a_ln_b", "a_w_s", "a_b_s", "a_w_out")}
    gb = {k: [None] * (DEPTH - N_A) for k in ("b_w_q_t", "b_w_o")}
    mats = ("a_w_in", "a_w_out", "b_w_q_t", "b_w_o", "ffn_w_up_t", "ffn_w_down")
    dks, dvs, dbias = [], [], []
    grads = {}
    terms = [(1.0, dy)]
    for i in reversed(range(DEPTH)):
        s = saved[i]
        dp2, dp2b, dg2, db2 = _add_ln_bwd(s["x1"], s["f"], W["ln_g"][i, 1], terms, name=f"ln2_bwd_{i}")
        dact = _mm(dp2b, W["ffn_w_down"][i], tb=True, name=f"ffn_down_dx_{i}")
        gl["ffn_w_down"][i] = _mm(s["act"], dp2b, ta=True, out_dtype=BF16, name=f"ffn_down_dw_{i}")
        dha, dhg, dwa, dwg, dba, dbg = _convgate_bwd(s["hh"], dact, s["cw"], s["cb"], name=f"convgate_bwd_{i}")
        dhh = jnp.concatenate([dha, dhg], axis=1)
        gl["ffn_conv_w"][i] = jnp.concatenate([dwa, dwg], axis=1)
        gl["ffn_conv_b"][i] = jnp.concatenate([dba, dbg], axis=1)[0]
        dx1 = _mm(dhh, W["ffn_w_up_t"][i], name=f"ffn_up_dx_{i}")
        gl["ffn_w_up_t"][i] = _mm(dhh, s["x1b"], ta=True, out_dtype=BF16, name=f"ffn_up_dw_{i}")
        dp1, dp1b, dg1, db1 = _add_ln_bwd(s["x"], s["h"], W["ln_g"][i, 0], [(ALPHA, dp2), (1.0, dx1)],
                                          name=f"ln1_bwd_{i}")
        gl["ln_g"][i] = jnp.concatenate([dg1, dg2], axis=0)
        gl["ln_b"][i] = jnp.concatenate([db1, db2], axis=0)
        terms = [(ALPHA, dp1)]
        if i < N_A:
            dyy = _mm(dp1b, W["a_w_out"][i], tb=True, name=f"a_out_dx_{i}")
            ga["a_w_out"][i] = _mm(s["y"], dp1b, ta=True, out_dtype=BF16, name=f"a_out_dw_{i}")
            dzp, dlg, dlb, dws, dbs = _sgu_bwd(s["zp"], dyy, W["a_ln_g"][i], W["a_ln_b"][i], s["ws"], s["wst"],
                                               s["bst"], name=f"sgu_bwd_{i}")
            ga["a_ln_g"][i], ga["a_ln_b"][i], ga["a_w_s"][i] = dlg[0], dlb[0], dws
            ga["a_b_s"][i] = dbs[:, :G].T
            terms.append((1.0, _mm(dzp, W["a_w_in"][i], tb=True, name=f"a_in_dx_{i}")))
            ga["a_w_in"][i] = _mm(s["xb"], dzp, ta=True, out_dtype=BF16, name=f"a_in_dw_{i}")
        else:
            j = i - N_A
            do = _mm(dp1b, W["b_w_o"][j], tb=True, name=f"o_proj_dx_{j}")
            gb["b_w_o"][j] = _mm(s["ob"], dp1b, ta=True, out_dtype=BF16, name=f"o_proj_dw_{j}")
            dqs, dbl = [], []
            for gi in range(N_GROUPS):
                dq_g, dk_g, dv_g, db_g = _attn_bwd(s["q"], kvb, bias[gi], do, s["o"], s["L"], gi,
                                                   name=f"attn_bwd_{j}_{gi}")
                dqs.append(dq_g.reshape(T, HD))
                dks.append((1.0, dk_g.reshape(T, HD)))
                dvs.append((1.0, dv_g.reshape(T, HD)))
                dbl.append(db_g)
            dbias.append(jnp.stack(dbl))
            dq = jnp.concatenate(dqs, axis=1)
            terms.append((1.0, _mm(dq, W["b_w_q_t"][j], name=f"q_proj_dx_{j}")))
            gb["b_w_q_t"][j] = _mm(dq, s["xb"], ta=True, out_dtype=BF16, name=f"q_proj_dw_{j}")
            if i == N_A:
                dkv = jnp.concatenate([_lincomb(dks, BF16, name="dk_sum"), _lincomb(dvs, BF16, name="dv_sum")],
                                      axis=1)
                terms.append((1.0, _mm(dkv, W["kv_w"], tb=True, name="kv_proj_dx")))
                grads["kv_w"] = [_mm(s["xb"], dkv, ta=True, out_dtype=BF16, name="kv_proj_dw")]
                dbt = _lincomb([(1.0, a.reshape(-1, 2 * BLK)) for a in dbias], F32, name="dbias_sum")
                dtab = _band_bias_bwd(dbt.reshape(N_GROUPS, H, BLK, 2 * BLK), bmap, H, name="band_bias_bwd")
                grads["rel_table"] = jnp.transpose(dtab[:, :, :H], (1, 0, 2)).reshape(REL_BUCKETS, N_GROUPS * H)
    grad_x = _lincomb(terms, F32, name="grad_x")
    for dct in (gl, ga, gb):
        for k, v in dct.items():
            grads[k] = v if k in mats else jnp.stack(v)
    return loss, grad_x, grads


def _my_index():
    return 4 * lax.axis_index("x") + 2 * lax.axis_index("y") + lax.axis_index("c")


HBM_SPEC = pl.BlockSpec(memory_space=pltpu.HBM)


def _block(ref, k, n, axis):
    off = pl.multiple_of(k * n, n)
    return ref.at[pl.ds(off, n), :] if axis == 0 else ref.at[:, pl.ds(off, n)]


def _gather_mats(local, axis, *, name):
    L, a, b = local.shape
    n = a if axis == 0 else b
    full = (a * N_DEV, b) if axis == 0 else (a, b * N_DEV)

    def body(x_ref, *rest):
        outs = rest[:L]
        send_sems, recv_sems, local_sems = rest[L:]
        x, y, c = lax.axis_index("x"), lax.axis_index("y"), lax.axis_index("c")
        me, sibling = (x, y, c), (x, y, 1 - c)
        chips = [(1 - x, y), (x, 1 - y), (1 - x, 1 - y)]

        def slot(l, px, py, pc):
            return _block(outs[l], 4 * px + 2 * py + pc, n, axis)

        def copy(l, k, blk, to, src=None):
            return pltpu.make_async_remote_copy(
                src_ref=slot(l, *blk) if src is None else src, dst_ref=slot(l, *blk),
                send_sem=send_sems.at[7 * l + k], recv_sem=recv_sems.at[7 * l + k],
                device_id=to, device_id_type=MESH)

        mine, first, passed = [], [], []
        for l in range(L):
            mine.append(pltpu.make_async_copy(x_ref.at[l], slot(l, *me), local_sems.at[l]))
            mine[-1].start()
            first.append(copy(l, 0, me, sibling, src=x_ref.at[l]))
            first += [copy(l, 1 + j, me, (*chip, c), src=x_ref.at[l]) for j, chip in enumerate(chips)]
        for cp in first:
            cp.start()
        for l in range(L):
            for j, chip in enumerate(chips):
                copy(l, 1 + j, (*chip, c), me).wait_recv()
                passed.append(copy(l, 4 + j, (*chip, c), sibling))
                passed[-1].start()
        for l in range(L):
            copy(l, 0, sibling, me).wait_recv()
            for j, chip in enumerate(chips):
                copy(l, 4 + j, (*chip, 1 - c), me).wait_recv()
        for cp in first + passed:
            cp.wait_send()
        for cp in mine:
            cp.wait()

    return pl.pallas_call(
        body, out_shape=[jax.ShapeDtypeStruct(full, local.dtype)] * L,
        in_specs=[HBM_SPEC], out_specs=[HBM_SPEC] * L,
        scratch_shapes=[pltpu.SemaphoreType.DMA((7 * L,)), pltpu.SemaphoreType.DMA((7 * L,)),
                        pltpu.SemaphoreType.DMA((L,))],
        name=name,
    )(local)


def _exchange_mats(mats, axis, *, name):
    L = len(mats)
    A, B = mats[0].shape
    a, b = (A // N_DEV, B) if axis == 0 else (A, B // N_DEV)
    n = a if axis == 0 else b

    def body(*refs):
        g_refs = refs[:L]
        out_ref, send_sems, recv_sems, local_sems = refs[L:]
        x, y, c = lax.axis_index("x"), lax.axis_index("y"), lax.axis_index("c")
        me = 4 * x + 2 * y + c
        mine, copies = [], []
        for l in range(L):
            mine.append(pltpu.make_async_copy(_block(g_refs[l], me, n, axis), out_ref.at[me, l],
                                              local_sems.at[l]))
            mine[-1].start()
            for k in range(1, N_DEV):
                px = 1 - x if k & 4 else x
                py = 1 - y if k & 2 else y
                pc = 1 - c if k & 1 else c
                copies.append(pltpu.make_async_remote_copy(
                    src_ref=_block(g_refs[l], 4 * px + 2 * py + pc, n, axis), dst_ref=out_ref.at[me, l],
                    send_sem=send_sems.at[7 * l + k - 1], recv_sem=recv_sems.at[7 * l + k - 1],
                    device_id=(px, py, pc), device_id_type=MESH))
        for cp in copies:
            cp.start()
        for cp in copies:
            cp.wait()
        for cp in mine:
            cp.wait()

    return pl.pallas_call(
        body, out_shape=jax.ShapeDtypeStruct((N_DEV, L, a, b), mats[0].dtype),
        in_specs=[HBM_SPEC] * L, out_specs=HBM_SPEC,
        scratch_shapes=[pltpu.SemaphoreType.DMA((7 * L,)), pltpu.SemaphoreType.DMA((7 * L,)),
                        pltpu.SemaphoreType.DMA((L,))],
        name=name,
    )(*mats)


def _sum_parts(parts, *, name):
    n, R, C = parts.shape
    rb = _pick(R, 512) if R % LANES == 0 else R

    def body(p_ref, o_ref):
        acc = p_ref[0].astype(F32)
        for k in range(1, n):
            acc = acc + p_ref[k].astype(F32)
        o_ref[...] = acc

    return pl.pallas_call(
        body, grid=(R // rb,), in_specs=[pl.BlockSpec((n, rb, C), lambda i: (0, i, 0))],
        out_specs=pl.BlockSpec((rb, C), lambda i: (i, 0)),
        out_shape=jax.ShapeDtypeStruct((R, C), F32),
        compiler_params=_params(("parallel",)), name=name,
    )(parts)


def _adamw(w, m, v, parts, *, name):
    n, R, C = parts.shape
    rb = _pick(R, 256) if R % LANES == 0 else R

    def body(w_ref, m_ref, v_ref, p_ref, g_ref, d_ref, nm_ref, nv_ref):
        g = p_ref[0].astype(F32)
        for k in range(1, n):
            g = g + p_ref[k].astype(F32)
        mn = ADAM_B1 * m_ref[...] + (1.0 - ADAM_B1) * g
        vn = ADAM_B2 * v_ref[...] + (1.0 - ADAM_B2) * jnp.square(g)
        m_hat = mn / (1.0 - ADAM_B1 ** ADAM_STEP)
        v_hat = vn / (1.0 - ADAM_B2 ** ADAM_STEP)
        g_ref[...] = g
        d_ref[...] = -ADAM_LR * (m_hat / (jnp.sqrt(v_hat) + ADAM_EPS) + ADAM_WD * w_ref[...])
        nm_ref[...] = mn
        nv_ref[...] = vn

    row = pl.BlockSpec((rb, C), lambda i: (i, 0))
    return pl.pallas_call(
        body, grid=(R // rb,), in_specs=[row, row, row, pl.BlockSpec((n, rb, C), lambda i: (0, i, 0))],
        out_specs=[row] * 4, out_shape=[jax.ShapeDtypeStruct((R, C), F32)] * 4,
        compiler_params=_params(("parallel",)), name=name,
    )(w, m, v, parts)


BIG = (("a_w_in", "a_w_in", 1, False), ("a_w_out", "a_w_out", 0, False), ("kv_w", "kv_w", 0, False),
       ("b_w_q", "b_w_q_t", 0, True), ("b_w_o", "b_w_o", 1, False), ("ffn_w_up", "ffn_w_up_t", 0, True),
       ("ffn_w_down", "ffn_w_down", 0, False))
SMALL_SHARDED = (("a_ln_g", 1), ("a_ln_b", 1), ("ffn_conv_w", 2), ("ln_g", 2), ("ln_b", 2))
REPLICATED = ("a_w_s", "a_b_s", "rel_table", "ffn_conv_b")


def _pack_rows(arrs, lead=0):
    lshape = arrs[0].shape[:lead]
    p = jnp.concatenate([a.reshape(*lshape, -1, LANES) for a in arrs], axis=lead)
    pad = -p.shape[lead] % 8
    return jnp.pad(p, [(0, 0)] * lead + [(0, pad), (0, 0)])


def _unpack_rows(packed, shapes, lead=0):
    lshape = packed.shape[:lead]
    out, off = [], 0
    for shp in shapes:
        r = int(np.prod(shp)) // LANES
        out.append(lax.slice_in_dim(packed, off, off + r, axis=lead).reshape(*lshape, *shp))
        off += r
    return out


def _as_mats(a, transposed):
    a = a[None] if a.ndim == 2 else a
    return jnp.swapaxes(a, 1, 2) if transposed else a


def _merge_shards(stacked, axis):
    a = jnp.moveaxis(stacked, 0, axis)
    shp = list(a.shape)
    return a.reshape(shp[:axis] + [shp[axis] * shp[axis + 1]] + shp[axis + 2:])


def _split_shards(full, axis):
    shp = list(full.shape)
    a = full.reshape(shp[:axis] + [N_DEV, shp[axis] // N_DEV] + shp[axis + 1:])
    return jnp.moveaxis(a, axis, 0)


def kernel(x, a_w_in, a_ln_g, a_ln_b, a_w_s, a_b_s, a_w_out, kv_w, b_w_q, b_w_o, rel_table, ffn_w_up, ffn_conv_w, ffn_conv_b, ffn_w_down, ln_g, ln_b, loss_target, m_a_w_in, m_a_ln_g, m_a_ln_b, m_a_w_s, m_a_b_s, m_a_w_out, m_kv_w, m_b_w_q, m_b_w_o, m_rel_table, m_ffn_w_up, m_ffn_conv_w, m_ffn_conv_b, m_ffn_w_down, m_ln_g, m_ln_b, v_a_w_in, v_a_ln_g, v_a_ln_b, v_a_w_s, v_a_b_s, v_a_w_out, v_kv_w, v_b_w_q, v_b_w_o, v_rel_table, v_ffn_w_up, v_ffn_conv_w, v_ffn_conv_b, v_ffn_w_down, v_ln_g, v_ln_b):
    names = ["a_w_in", "a_ln_g", "a_ln_b", "a_w_s", "a_b_s", "a_w_out", "kv_w", "b_w_q", "b_w_o", "rel_table",
             "ffn_w_up", "ffn_conv_w", "ffn_conv_b", "ffn_w_down", "ln_g", "ln_b"]
    w = dict(zip(names, (a_w_in, a_ln_g, a_ln_b, a_w_s, a_b_s, a_w_out, kv_w, b_w_q, b_w_o, rel_table,
                         ffn_w_up, ffn_conv_w, ffn_conv_b, ffn_w_down, ln_g, ln_b)))
    m = dict(zip(names, (m_a_w_in, m_a_ln_g, m_a_ln_b, m_a_w_s, m_a_b_s, m_a_w_out, m_kv_w, m_b_w_q, m_b_w_o,
                         m_rel_table, m_ffn_w_up, m_ffn_conv_w, m_ffn_conv_b, m_ffn_w_down, m_ln_g, m_ln_b)))
    v = dict(zip(names, (v_a_w_in, v_a_ln_g, v_a_ln_b, v_a_w_s, v_a_b_s, v_a_w_out, v_kv_w, v_b_w_q, v_b_w_o,
                         v_rel_table, v_ffn_w_up, v_ffn_conv_w, v_ffn_conv_b, v_ffn_w_down, v_ln_g, v_ln_b)))
    small_names = [n for n, _ in SMALL_SHARDED]
    small_shapes = [w[n].shape for n in small_names]
    rep_shapes = [w[n].shape for n in REPLICATED]

    W = {n: w[n] for n in REPLICATED}
    for n, key, axis, tr in BIG:
        W[key] = _gather_mats(_as_mats(w[n], tr).astype(BF16), axis, name=f"gather_{n}")
    W["kv_w"] = W["kv_w"][0]
    small_all = _gather_mats(_pack_rows([w[n] for n in small_names])[None], 0, name="gather_small")[0]
    srows = small_all.shape[0] // N_DEV
    small_st = _unpack_rows(small_all.reshape(N_DEV, srows, LANES), small_shapes, lead=1)
    for (n, ax), st in zip(SMALL_SHARDED, small_st):
        W[n] = _merge_shards(st, ax)

    loss, grad_x, grads = _local_step(x[0], loss_target[0], W)
    loss = lax.psum(loss, ("x", "y", "c"))

    out = {}
    for n, key, axis, tr in BIG:
        parts = _exchange_mats(grads[key], axis, name=f"exchange_{n}")
        L, a, b = parts.shape[1:]
        parts = parts.reshape(N_DEV, L * a, b)
        if tr:
            g_t = _sum_parts(parts, name=f"sum_{n}").reshape(L, a, b)
            parts = jnp.swapaxes(g_t, 1, 2).reshape(1, L * b, a)
        shp = w[n].shape
        res = _adamw(w[n].reshape(-1, shp[-1]), m[n].reshape(-1, shp[-1]), v[n].reshape(-1, shp[-1]), parts,
                     name=f"adamw_{n}")
        out[n] = [r.reshape(shp) for r in res]

    small_pack = _pack_rows([_split_shards(grads[n], ax) for n, ax in SMALL_SHARDED], lead=1)
    rep_pack = _pack_rows([grads[n] for n in REPLICATED])
    mine = jnp.concatenate([small_pack.reshape(N_DEV * srows, LANES), rep_pack], axis=0)
    allp = _gather_mats(mine[None], 0, name="gather_small_grads")[0]
    gsum = _sum_parts(allp.reshape(N_DEV, mine.shape[0], LANES), name="sum_small_grads")
    g_small = lax.dynamic_slice_in_dim(gsum, _my_index() * srows, srows, axis=0)
    gs_in = jnp.concatenate([g_small, gsum[N_DEV * srows:]], axis=0)[None]
    pack_sr = lambda d: jnp.concatenate([_pack_rows([d[n] for n in small_names]),
                                         _pack_rows([d[n] for n in REPLICATED])], axis=0)
    res = _adamw(pack_sr(w), pack_sr(m), pack_sr(v), gs_in, name="adamw_small")
    for n, vals in zip(small_names, zip(*[_unpack_rows(r[:srows], small_shapes) for r in res])):
        out[n] = list(vals)
    for n, vals in zip(REPLICATED, zip(*[_unpack_rows(r[srows:], rep_shapes) for r in res])):
        out[n] = list(vals)

    return (loss, grad_x[None], *[out[n][0] for n in names], *[out[n][1] for n in names],
            *[out[n][2] for n in names], *[out[n][3] for n in names])
```

```python
import math

import numpy as np
import jax
import jax.numpy as jnp
from jax import lax
from jax.experimental import pallas as pl
from jax.experimental.pallas import tpu as pltpu

F32 = jnp.float32
BF16 = jnp.bfloat16
MESH = pl.DeviceIdType.MESH

N_DEV = 8
DEPTH = 4
N_A = 2
CHUNK = 128
BLK = 128
HEAD_DIM = 64
DILATED_GROUPS = ((128, 1), (512, 4), (2048, 16))
N_GROUPS = 3
REL_BUCKETS = 32
REL_MAX_DIST = 2048
ALPHA = (2 * DEPTH) ** 0.25
LN_EPS = 1e-5
NEG = -1e30
ADAM_LR = 0.001
ADAM_B1 = 0.9
ADAM_B2 = 0.999
ADAM_EPS = 1e-08
ADAM_WD = 0.01
ADAM_STEP = 10

LANES = 128
VMEM_LIMIT = 56 * 1024 * 1024
MM_TILE_CAP = 1408
INV_SQRT2 = 1.0 / math.sqrt(2.0)
INV_SQRT_2PI = 1.0 / math.sqrt(2.0 * math.pi)


def _pick(n, cap):
    best = None
    for t in range(LANES, min(n, cap) + 1, LANES):
        if n % t == 0:
            best = t
    return best if best is not None else n


def _params(sem):
    return pltpu.CompilerParams(dimension_semantics=sem, vmem_limit_bytes=VMEM_LIMIT)


def _gelu(x):
    return 0.5 * x * (1.0 + lax.erf(x * INV_SQRT2))


def _gelu_grad(x):
    return 0.5 * (1.0 + lax.erf(x * INV_SQRT2)) + x * jnp.exp(-0.5 * x * x) * INV_SQRT_2PI


def _mm(a, b, *, ta=False, tb=False, out_dtype=F32, scale=None, name):
    if ta:
        K, M = a.shape
    else:
        M, K = a.shape
    if tb:
        N, Kb = b.shape
    else:
        Kb, N = b.shape
    assert K == Kb, (a.shape, b.shape, ta, tb)
    tm, tn, tk = _pick(M, MM_TILE_CAP), _pick(N, MM_TILE_CAP), _pick(K, MM_TILE_CAP)
    nk = K // tk
    dn = (((0 if ta else 1,), (1 if tb else 0,)), ((), ()))

    def body(a_ref, b_ref, o_ref, acc_ref):
        k = pl.program_id(2)
        part = lax.dot_general(a_ref[...].astype(BF16), b_ref[...].astype(BF16), dn,
                               preferred_element_type=F32)

        @pl.when(k == 0)
        def _():
            acc_ref[...] = part

        @pl.when(k > 0)
        def _():
            acc_ref[...] += part

        @pl.when(k == nk - 1)
        def _():
            r = acc_ref[...]
            if scale is not None:
                r = r * scale
            o_ref[...] = r.astype(out_dtype)

    a_spec = (pl.BlockSpec((tk, tm), lambda i, j, k: (k, i)) if ta
              else pl.BlockSpec((tm, tk), lambda i, j, k: (i, k)))
    b_spec = (pl.BlockSpec((tn, tk), lambda i, j, k: (j, k)) if tb
              else pl.BlockSpec((tk, tn), lambda i, j, k: (k, j)))
    return pl.pallas_call(
        body, grid=(M // tm, N // tn, nk), in_specs=[a_spec, b_spec],
        out_specs=pl.BlockSpec((tm, tn), lambda i, j, k: (i, j)),
        out_shape=jax.ShapeDtypeStruct((M, N), out_dtype),
        scratch_shapes=[pltpu.VMEM((tm, tn), F32)],
        compiler_params=_params(("parallel", "parallel", "arbitrary")), name=name,
    )(a, b)


def _add_ln_fwd(x, h, g, b, *, name):
    T, D = x.shape
    rb = _pick(T, 512)

    def body(x_ref, h_ref, g_ref, b_ref, o_ref, ob_ref):
        pre = ALPHA * x_ref[...] + h_ref[...]
        mu = jnp.mean(pre, axis=1, keepdims=True)
        cen = pre - mu
        var = jnp.mean(cen * cen, axis=1, keepdims=True)
        y = cen * lax.rsqrt(var + LN_EPS) * g_ref[...] + b_ref[...]
        o_ref[...] = y
        ob_ref[...] = y.astype(BF16)

    row = pl.BlockSpec((rb, D), lambda i: (i, 0))
    vec = pl.BlockSpec((1, D), lambda i: (0, 0))
    return pl.pallas_call(
        body, grid=(T // rb,), in_specs=[row, row, vec, vec], out_specs=[row, row],
        out_shape=[jax.ShapeDtypeStruct((T, D), F32), jax.ShapeDtypeStruct((T, D), BF16)],
        compiler_params=_params(("parallel",)), name=name,
    )(x, h, g.reshape(1, D), b.reshape(1, D))


def _add_ln_bwd(x, h, g, terms, *, name):
    T, D = x.shape
    rb = _pick(T, 512)
    coefs = [c for c, _ in terms]
    nt = len(terms)

    def body(*refs):
        x_ref, h_ref, g_ref = refs[:3]
        t_refs = refs[3:3 + nt]
        dp_ref, dpb_ref, dg_ref, db_ref = refs[3 + nt:]
        dy = None
        for c, r in zip(coefs, t_refs):
            v = r[...] if c == 1.0 else c * r[...]
            dy = v if dy is None else dy + v
        pre = ALPHA * x_ref[...] + h_ref[...]
        mu = jnp.mean(pre, axis=1, keepdims=True)
        cen = pre - mu
        var = jnp.mean(cen * cen, axis=1, keepdims=True)
        rstd = lax.rsqrt(var + LN_EPS)
        xhat = cen * rstd
        dxh = dy * g_ref[...]
        m1 = jnp.mean(dxh, axis=1, keepdims=True)
        m2 = jnp.mean(dxh * xhat, axis=1, keepdims=True)
        dpre = rstd * (dxh - m1 - xhat * m2)
        dp_ref[...] = dpre
        dpb_ref[...] = dpre.astype(BF16)
        dg = jnp.sum(dy * xhat, axis=0, keepdims=True)
        db = jnp.sum(dy, axis=0, keepdims=True)

        @pl.when(pl.program_id(0) == 0)
        def _():
            dg_ref[...] = dg
            db_ref[...] = db

        @pl.when(pl.program_id(0) > 0)
        def _():
            dg_ref[...] += dg
            db_ref[...] += db

    row = pl.BlockSpec((rb, D), lambda i: (i, 0))
    vec = pl.BlockSpec((1, D), lambda i: (0, 0))
    return pl.pallas_call(
        body, grid=(T // rb,), in_specs=[row, row, vec] + [row] * nt,
        out_specs=[row, row, vec, vec],
        out_shape=[jax.ShapeDtypeStruct((T, D), F32), jax.ShapeDtypeStruct((T, D), BF16),
                   jax.ShapeDtypeStruct((1, D), F32), jax.ShapeDtypeStruct((1, D), F32)],
        compiler_params=_params(("arbitrary",)), name=name,
    )(x, h, g.reshape(1, D), *[a for _, a in terms])


def _lincomb(terms, out_dtype, *, name):
    R, C = terms[0][1].shape
    rb = _pick(R, 512)
    coefs = [c for c, _ in terms]
    nt = len(terms)

    def body(*refs):
        acc = None
        for c, r in zip(coefs, refs[:nt]):
            v = r[...].astype(F32)
            v = v if c == 1.0 else c * v
            acc = v if acc is None else acc + v
        refs[nt][...] = acc.astype(out_dtype)

    row = pl.BlockSpec((rb, C), lambda i: (i, 0))
    return pl.pallas_call(
        body, grid=(R // rb,), in_specs=[row] * nt, out_specs=row,
        out_shape=jax.ShapeDtypeStruct((R, C), out_dtype),
        compiler_params=_params(("parallel",)), name=name,
    )(*[a for _, a in terms])


def _loss_grad(y, tgt, *, name):
    T, D = y.shape
    rb = _pick(T, 512)

    def body(y_ref, t_ref, dy_ref, l_ref):
        err = y_ref[...] - t_ref[...]
        dy_ref[...] = err * (1.0 / D)
        part = jnp.sum(jnp.sum(err * err, axis=1, keepdims=True), axis=0, keepdims=True) * (0.5 / D)
        part = jnp.broadcast_to(part, (1, LANES))

        @pl.when(pl.program_id(0) == 0)
        def _():
            l_ref[...] = part

        @pl.when(pl.program_id(0) > 0)
        def _():
            l_ref[...] += part

    row = pl.BlockSpec((rb, D), lambda i: (i, 0))
    return pl.pallas_call(
        body, grid=(T // rb,), in_specs=[row, row],
        out_specs=[row, pl.BlockSpec((1, LANES), lambda i: (0, 0))],
        out_shape=[jax.ShapeDtypeStruct((T, D), F32), jax.ShapeDtypeStruct((1, LANES), F32)],
        compiler_params=_params(("arbitrary",)), name=name,
    )(y, tgt)


def _sgu_fwd(zp, ln_g, ln_b, ws, bst, *, name):
    T, E2 = zp.shape
    E = E2 // 2
    G = ws.shape[0]
    cg = E // G
    rb = 2 * CHUNK

    def body(z_ref, g_ref, b_ref, ws_ref, bs_ref, y_ref):
        u = _gelu(z_ref[:, :E])
        v = _gelu(z_ref[:, E:])
        mu = jnp.mean(v, axis=1, keepdims=True)
        cen = v - mu
        var = jnp.mean(cen * cen, axis=1, keepdims=True)
        vn = (cen * lax.rsqrt(var + LN_EPS) * g_ref[...] + b_ref[...]).astype(BF16)
        for ci in range(rb // CHUNK):
            rows = slice(ci * CHUNK, (ci + 1) * CHUNK)
            for gi in range(G):
                cols = slice(gi * cg, (gi + 1) * cg)
                sv = jnp.dot(ws_ref[gi], vn[rows, cols], preferred_element_type=F32)
                sv = sv + bs_ref[:, gi:gi + 1]
                y_ref[rows, cols] = (u[rows, cols] * sv).astype(BF16)

    return pl.pallas_call(
        body, grid=(T // rb,),
        in_specs=[pl.BlockSpec((rb, E2), lambda i: (i, 0)),
                  pl.BlockSpec((1, E), lambda i: (0, 0)), pl.BlockSpec((1, E), lambda i: (0, 0)),
                  pl.BlockSpec((G, CHUNK, CHUNK), lambda i: (0, 0, 0)),
                  pl.BlockSpec((CHUNK, G), lambda i: (0, 0))],
        out_specs=pl.BlockSpec((rb, E), lambda i: (i, 0)),
        out_shape=jax.ShapeDtypeStruct((T, E), BF16),
        compiler_params=_params(("parallel",)), name=name,
    )(zp, ln_g.reshape(1, E), ln_b.reshape(1, E), ws, bst)


def _sgu_bwd(zp, dy, ln_g, ln_b, ws, wst, bst, *, name):
    T, E2 = zp.shape
    E = E2 // 2
    G = ws.shape[0]
    cg = E // G
    rb = CHUNK
    nsteps = T // rb

    def body(z_ref, dy_ref, g_ref, b_ref, ws_ref, wst_ref, bs_ref,
             dz_ref, dg_ref, db_ref, dws_ref, dbs_ref, dsv_acc):
        step = pl.program_id(0)

        @pl.when(step == 0)
        def _():
            dg_ref[...] = jnp.zeros_like(dg_ref)
            db_ref[...] = jnp.zeros_like(db_ref)
            dws_ref[...] = jnp.zeros_like(dws_ref)
            dsv_acc[...] = jnp.zeros_like(dsv_acc)

        zu = z_ref[:, :E]
        zv = z_ref[:, E:]
        u = _gelu(zu)
        v = _gelu(zv)
        mu = jnp.mean(v, axis=1, keepdims=True)
        cen = v - mu
        var = jnp.mean(cen * cen, axis=1, keepdims=True)
        rstd = lax.rsqrt(var + LN_EPS)
        xhat = cen * rstd
        vn = (xhat * g_ref[...] + b_ref[...]).astype(BF16)
        dyv = dy_ref[...]
        dsv = dyv * u
        dsv_acc[...] += dsv
        dsvb = dsv.astype(BF16)
        tril = (lax.broadcasted_iota(jnp.int32, (CHUNK, CHUNK), 0)
                >= lax.broadcasted_iota(jnp.int32, (CHUNK, CHUNK), 1))
        du_parts = []
        dvn_parts = []
        for gi in range(G):
            cols = slice(gi * cg, (gi + 1) * cg)
            sv = jnp.dot(ws_ref[gi], vn[:, cols], preferred_element_type=F32) + bs_ref[:, gi:gi + 1]
            du_parts.append(dyv[:, cols] * sv)
            dvn_parts.append(jnp.dot(wst_ref[gi], dsvb[:, cols], preferred_element_type=F32))
            dw = lax.dot_general(dsvb[:, cols], vn[:, cols], (((1,), (1,)), ((), ())),
                                 preferred_element_type=F32)
            dws_ref[gi] += jnp.where(tril, dw, 0.0)
        du = jnp.concatenate(du_parts, axis=1)
        dvn = jnp.concatenate(dvn_parts, axis=1)
        dg_ref[...] += jnp.sum(dvn * xhat, axis=0, keepdims=True)
        db_ref[...] += jnp.sum(dvn, axis=0, keepdims=True)
        dxh = dvn * g_ref[...]
        m1 = jnp.mean(dxh, axis=1, keepdims=True)
        m2 = jnp.mean(dxh * xhat, axis=1, keepdims=True)
        dv = rstd * (dxh - m1 - xhat * m2)
        dz_ref[:, :E] = (du * _gelu_grad(zu)).astype(BF16)
        dz_ref[:, E:] = (dv * _gelu_grad(zv)).astype(BF16)

        @pl.when(step == nsteps - 1)
        def _():
            lane = lax.broadcasted_iota(jnp.int32, (CHUNK, LANES), 1)
            out = jnp.zeros((CHUNK, LANES), F32)
            for gi in range(G):
                s = jnp.sum(dsv_acc[:, gi * cg:(gi + 1) * cg], axis=1, keepdims=True)
                out = jnp.where(lane == gi, s, out)
            dbs_ref[...] = out

    vecE = pl.BlockSpec((1, E), lambda i: (0, 0))
    wspec = pl.BlockSpec((G, CHUNK, CHUNK), lambda i: (0, 0, 0))
    return pl.pallas_call(
        body, grid=(nsteps,),
        in_specs=[pl.BlockSpec((rb, E2), lambda i: (i, 0)), pl.BlockSpec((rb, E), lambda i: (i, 0)),
                  vecE, vecE, wspec, wspec, pl.BlockSpec((CHUNK, G), lambda i: (0, 0))],
        out_specs=[pl.BlockSpec((rb, E2), lambda i: (i, 0)), vecE, vecE, wspec,
                   pl.BlockSpec((CHUNK, LANES), lambda i: (0, 0))],
        out_shape=[jax.ShapeDtypeStruct((T, E2), BF16), jax.ShapeDtypeStruct((1, E), F32),
                   jax.ShapeDtypeStruct((1, E), F32), jax.ShapeDtypeStruct((G, CHUNK, CHUNK), F32),
                   jax.ShapeDtypeStruct((CHUNK, LANES), F32)],
        scratch_shapes=[pltpu.VMEM((CHUNK, E), F32)],
        compiler_params=_params(("arbitrary",)), name=name,
    )(zp, dy, ln_g.reshape(1, E), ln_b.reshape(1, E), ws, wst, bst)


def _shift_down(x, k, row):
    return jnp.where(row >= k, pltpu.roll(x, k, 0), 0.0)


def _shift_up(x, k, row, T):
    return jnp.where(row < T - k, pltpu.roll(x, T - k, 0), 0.0)


def _conv3(x, w_ref, b_ref, row):
    return (w_ref[0:1, :] * _shift_down(x, 2, row) + w_ref[1:2, :] * _shift_down(x, 1, row)
            + w_ref[2:3, :] * x + b_ref[...])


def _convgate_fwd(hh, cw, cb, *, name):
    T, F2 = hh.shape
    F = F2 // 2
    ns = F // LANES

    def body(a_ref, g_ref, wa_ref, wg_ref, ba_ref, bg_ref, o_ref):
        row = lax.broadcasted_iota(jnp.int32, (T, LANES), 0)
        ca = _conv3(a_ref[...], wa_ref, ba_ref, row)
        cgv = _conv3(g_ref[...], wg_ref, bg_ref, row)
        o_ref[...] = (_gelu(ca) * cgv).astype(BF16)

    sa = lambda r: pl.BlockSpec((r, LANES), lambda j: (0, j))
    sg = lambda r: pl.BlockSpec((r, LANES), lambda j: (0, j + ns))
    return pl.pallas_call(
        body, grid=(ns,), in_specs=[sa(T), sg(T), sa(3), sg(3), sa(1), sg(1)],
        out_specs=sa(T), out_shape=jax.ShapeDtypeStruct((T, F), BF16),
        compiler_params=_params(("parallel",)), name=name,
    )(hh, hh, cw, cw, cb, cb)


def _convgate_bwd(hh, dact, cw, cb, *, name):
    T, F2 = hh.shape
    F = F2 // 2
    ns = F // LANES

    def body(a_ref, g_ref, d_ref, wa_ref, wg_ref, ba_ref, bg_ref,
             da_ref, dg_ref, dwa_ref, dwg_ref, dba_ref, dbg_ref):
        row = lax.broadcasted_iota(jnp.int32, (T, LANES), 0)
        d = d_ref[...].astype(F32)
        ca = _conv3(a_ref[...], wa_ref, ba_ref, row)
        cgv = _conv3(g_ref[...], wg_ref, bg_ref, row)
        cdf = 0.5 * (1.0 + lax.erf(ca * INV_SQRT2))
        dca = d * cgv * (cdf + ca * jnp.exp(-0.5 * ca * ca) * INV_SQRT_2PI)
        dcg = d * (ca * cdf)
        for x_ref, w_ref, dc, dx_ref, dw_ref, db_ref in (
                (a_ref, wa_ref, dca, da_ref, dwa_ref, dba_ref),
                (g_ref, wg_ref, dcg, dg_ref, dwg_ref, dbg_ref)):
            x = x_ref[...]
            dx = (w_ref[2:3, :] * dc + w_ref[1:2, :] * _shift_up(dc, 1, row, T)
                  + w_ref[0:1, :] * _shift_up(dc, 2, row, T))
            dx_ref[...] = dx.astype(BF16)
            dw_ref[0:1, :] = jnp.sum(dc * _shift_down(x, 2, row), axis=0, keepdims=True)
            dw_ref[1:2, :] = jnp.sum(dc * _shift_down(x, 1, row), axis=0, keepdims=True)
            dw_ref[2:3, :] = jnp.sum(dc * x, axis=0, keepdims=True)
            db_ref[...] = jnp.sum(dc, axis=0, keepdims=True)

    sa = lambda r: pl.BlockSpec((r, LANES), lambda j: (0, j))
    sg = lambda r: pl.BlockSpec((r, LANES), lambda j: (0, j + ns))
    return pl.pallas_call(
        body, grid=(ns,), in_specs=[sa(T), sg(T), sa(T), sa(3), sg(3), sa(1), sg(1)],
        out_specs=[sa(T), sa(T), sa(3), sa(3), sa(1), sa(1)],
        out_shape=[jax.ShapeDtypeStruct((T, F), BF16), jax.ShapeDtypeStruct((T, F), BF16),
                   jax.ShapeDtypeStruct((3, F), F32), jax.ShapeDtypeStruct((3, F), F32),
                   jax.ShapeDtypeStruct((1, F), F32), jax.ShapeDtypeStruct((1, F), F32)],
        compiler_params=_params(("parallel",)), name=name,
    )(hh, hh, dact, cw, cw, cb, cb)


def _bucket_maps():
    iq = np.arange(BLK)[:, None]
    ik = np.arange(2 * BLK)[None, :]
    delta = iq + BLK - ik
    maps = []
    for win, dil in DILATED_GROUPS:
        n = np.clip(delta, 0, None) * dil
        max_exact = REL_BUCKETS // 2
        nf = np.maximum(n, 1).astype(np.float32)
        large = max_exact + (np.log(nf / np.float32(max_exact)) / np.float32(math.log(REL_MAX_DIST / max_exact))
                             * np.float32(REL_BUCKETS - max_exact)).astype(np.int32)
        large = np.minimum(large, REL_BUCKETS - 1)
        bucket = np.where(n < max_exact, n, large)
        valid = (delta >= 0) & (delta <= win // dil)
        maps.append(np.where(valid, bucket, -1).astype(np.int32))
    return np.stack(maps)


def _band_bias(rel_table, bmap, H, *, name):
    def body(t_ref, m_ref, o_ref):
        g = pl.program_id(0)
        bm = m_ref[0]
        for h in range(H):
            acc = jnp.full((BLK, 2 * BLK), NEG, F32)
            for b in range(REL_BUCKETS):
                acc = jnp.where(bm == b, t_ref[b, g * H + h], acc)
            o_ref[0, h] = acc

    return pl.pallas_call(
        body, grid=(N_GROUPS,),
        in_specs=[pl.BlockSpec(memory_space=pltpu.SMEM),
                  pl.BlockSpec((1, BLK, 2 * BLK), lambda g: (g, 0, 0))],
        out_specs=pl.BlockSpec((1, H, BLK, 2 * BLK), lambda g: (g, 0, 0, 0)),
        out_shape=jax.ShapeDtypeStruct((N_GROUPS, H, BLK, 2 * BLK), F32),
        compiler_params=_params(("parallel",)), name=name,
    )(rel_table, bmap)


def _band_bias_bwd(dbias, bmap, H, *, name):
    def body(d_ref, m_ref, o_ref):
        bm = m_ref[0]
        rowi = lax.broadcasted_iota(jnp.int32, (REL_BUCKETS, LANES), 0)
        lane = lax.broadcasted_iota(jnp.int32, (REL_BUCKETS, LANES), 1)
        out = jnp.zeros((REL_BUCKETS, LANES), F32)
        for h in range(H):
            dv = d_ref[0, h]
            for b in range(REL_BUCKETS):
                s = jnp.sum(jnp.sum(jnp.where(bm == b, dv, 0.0), axis=1, keepdims=True),
                            axis=0, keepdims=True)
                out = jnp.where((rowi == b) & (lane == h), s, out)
        o_ref[0] = out

    return pl.pallas_call(
        body, grid=(N_GROUPS,),
        in_specs=[pl.BlockSpec((1, H, BLK, 2 * BLK), lambda g: (g, 0, 0, 0)),
                  pl.BlockSpec((1, BLK, 2 * BLK), lambda g: (g, 0, 0))],
        out_specs=pl.BlockSpec((1, REL_BUCKETS, LANES), lambda g: (g, 0, 0)),
        out_shape=jax.ShapeDtypeStruct((N_GROUPS, REL_BUCKETS, LANES), F32),
        compiler_params=_params(("parallel",)), name=name,
    )(dbias, bmap)


def _head_masks():
    lane = lax.broadcasted_iota(jnp.int32, (BLK, LANES), 1)
    return (lane < HEAD_DIM, lane >= HEAD_DIM)


def _attn_fwd(q, kv, bias, gi, *, name):
    T = q.shape[0]
    HD = kv.shape[1] // 2
    d = DILATED_GROUPS[gi][1]
    S = T // d
    NB = S // BLK
    H = HD // HEAD_DIM
    qv, qcol = (q, gi) if d == 1 else (q[:, gi * HD:(gi + 1) * HD].reshape(S, d * HD), 0)
    kvv = kv.reshape(S, d * 2 * HD)

    def body(q_ref, kp_ref, kc_ref, vp_ref, vc_ref, b_ref, o_ref, l_ref):
        n = pl.program_id(1)
        col = lax.broadcasted_iota(jnp.int32, (BLK, 2 * BLK), 1)
        first = (n == 0) & (col < BLK)
        hm = _head_masks()
        for p in range(HD // LANES):
            sl = slice(p * LANES, (p + 1) * LANES)
            qp = q_ref[:, sl]
            kc = jnp.concatenate([kp_ref[:, sl], kc_ref[:, sl]], axis=0)
            vc = jnp.concatenate([vp_ref[:, sl], vc_ref[:, sl]], axis=0)
            outs = []
            lses = []
            for hh in range(2):
                qm = jnp.where(hm[hh], qp, jnp.zeros_like(qp))
                s = lax.dot_general(qm, kc, (((1,), (1,)), ((), ())), preferred_element_type=F32)
                s = jnp.where(first, NEG, s + b_ref[2 * p + hh])
                m = jnp.max(s, axis=1, keepdims=True)
                e = jnp.exp(s - m)
                den = jnp.sum(e, axis=1, keepdims=True)
                outs.append(jnp.dot((e / den).astype(BF16), vc, preferred_element_type=F32))
                lses.append(m + jnp.log(den))
            o_ref[:, sl] = jnp.where(hm[0], outs[0], outs[1])
            l_ref[:, sl] = jnp.where(hm[0], lses[0], lses[1])

    blk = lambda f: pl.BlockSpec((BLK, HD), f)
    prev = lambda n: jnp.maximum(n - 1, 0)
    return pl.pallas_call(
        body, grid=(d, NB),
        in_specs=[blk(lambda r, n: (n, r + qcol)),
                  blk(lambda r, n: (prev(n), r * 2)), blk(lambda r, n: (n, r * 2)),
                  blk(lambda r, n: (prev(n), r * 2 + 1)), blk(lambda r, n: (n, r * 2 + 1)),
                  pl.BlockSpec((H, BLK, 2 * BLK), lambda r, n: (0, 0, 0))],
        out_specs=[blk(lambda r, n: (n, r)), blk(lambda r, n: (n, r))],
        out_shape=[jax.ShapeDtypeStruct((S, d * HD), F32), jax.ShapeDtypeStruct((S, d * HD), F32)],
        compiler_params=_params(("parallel", "parallel")), name=name,
    )(qv, kvv, kvv, kvv, kvv, bias)


def _attn_combine(os, ls, *, name):
    T, HD = os[0].shape
    rb = _pick(T, 512)

    def body(o0, o1, o2, l0, l1, l2, o_ref, ob_ref, l_ref):
        la, lb, lc = l0[...], l1[...], l2[...]
        m = jnp.maximum(jnp.maximum(la, lb), lc)
        L = m + jnp.log(jnp.exp(la - m) + jnp.exp(lb - m) + jnp.exp(lc - m))
        o = jnp.exp(la - L) * o0[...] + jnp.exp(lb - L) * o1[...] + jnp.exp(lc - L) * o2[...]
        o_ref[...] = o
        ob_ref[...] = o.astype(BF16)
        l_ref[...] = L

    row = pl.BlockSpec((rb, HD), lambda i: (i, 0))
    return pl.pallas_call(
        body, grid=(T // rb,), in_specs=[row] * 6, out_specs=[row] * 3,
        out_shape=[jax.ShapeDtypeStruct((T, HD), F32), jax.ShapeDtypeStruct((T, HD), BF16),
                   jax.ShapeDtypeStruct((T, HD), F32)],
        compiler_params=_params(("parallel",)), name=name,
    )(*[a.reshape(T, HD) for a in os], *[a.reshape(T, HD) for a in ls])


def _attn_bwd(q, kv, bias, do, o, L, gi, *, name):
    T = q.shape[0]
    HD = kv.shape[1] // 2
    d = DILATED_GROUPS[gi][1]
    S = T // d
    NB = S // BLK
    H = HD // HEAD_DIM
    qv, qcol = (q, gi) if d == 1 else (q[:, gi * HD:(gi + 1) * HD].reshape(S, d * HD), 0)
    kvv = kv.reshape(S, d * 2 * HD)
    dov, ov, Lv = (a.reshape(S, d * HD) for a in (do, o, L))

    def body(q_ref, kp_ref, kc_ref, vp_ref, vc_ref, b_ref, do_ref, o_ref, L_ref,
             dq_ref, dk_ref, dv_ref, db_ref, ck_ref, cv_ref):
        r = pl.program_id(0)
        n = pl.program_id(1)

        @pl.when((r == 0) & (n == 0))
        def _():
            db_ref[...] = jnp.zeros_like(db_ref)

        @pl.when(n == 0)
        def _():
            ck_ref[...] = jnp.zeros_like(ck_ref)
            cv_ref[...] = jnp.zeros_like(cv_ref)

        @pl.when(n < NB)
        def _():
            col = lax.broadcasted_iota(jnp.int32, (BLK, 2 * BLK), 1)
            first = (n == 0) & (col < BLK)
            hm = _head_masks()
            for p in range(HD // LANES):
                sl = slice(p * LANES, (p + 1) * LANES)
                qp = q_ref[:, sl]
                kc = jnp.concatenate([kp_ref[:, sl], kc_ref[:, sl]], axis=0)
                vc = jnp.concatenate([vp_ref[:, sl], vc_ref[:, sl]], axis=0)
                dop = do_ref[:, sl]
                dob = dop.astype(BF16)
                prod = dop * o_ref[:, sl]
                Lp = L_ref[:, sl]
                dq_parts = []
                dkc = None
                dvc = None
                for hh in range(2):
                    qm = jnp.where(hm[hh], qp, jnp.zeros_like(qp))
                    dom = jnp.where(hm[hh], dob, jnp.zeros_like(dob))
                    s = lax.dot_general(qm, kc, (((1,), (1,)), ((), ())), preferred_element_type=F32)
                    s = jnp.where(first, NEG, s + b_ref[2 * p + hh])
                    lse = Lp[:, hh * HEAD_DIM:hh * HEAD_DIM + 1]
                    pr = jnp.exp(s - lse)
                    dp = lax.dot_general(dom, vc, (((1,), (1,)), ((), ())), preferred_element_type=F32)
                    delta = jnp.sum(jnp.where(hm[hh], prod, 0.0), axis=1, keepdims=True)
                    ds = pr * (dp - delta)
                    db_ref[2 * p + hh] += ds
                    dsb = ds.astype(BF16)
                    dq_parts.append(jnp.dot(dsb, kc, preferred_element_type=F32))
                    dkh = lax.dot_general(dsb, qm, (((0,), (0,)), ((), ())), preferred_element_type=F32)
                    dvh = lax.dot_general(pr.astype(BF16), dom, (((0,), (0,)), ((), ())),
                                          preferred_element_type=F32)
                    dkc = dkh if dkc is None else dkc + dkh
                    dvc = dvh if dvc is None else dvc + dvh
                dq = jnp.where(hm[0], dq_parts[0], dq_parts[1])
                dq_ref[:, sl] = (dq * (HEAD_DIM ** -0.5)).astype(BF16)
                dk_ref[:, sl] = ck_ref[:, sl] + dkc[:BLK]
                dv_ref[:, sl] = cv_ref[:, sl] + dvc[:BLK]
                ck_ref[:, sl] = dkc[BLK:]
                cv_ref[:, sl] = dvc[BLK:]

        @pl.when(n == NB)
        def _():
            dk_ref[...] = ck_ref[...]
            dv_ref[...] = cv_ref[...]

    blk = lambda f: pl.BlockSpec((BLK, HD), f)
    cur = lambda n: jnp.minimum(n, NB - 1)
    prev = lambda n: jnp.maximum(jnp.minimum(n, NB - 1) - 1, 0)
    lag = lambda n: jnp.maximum(n - 1, 0)
    return pl.pallas_call(
        body, grid=(d, NB + 1),
        in_specs=[blk(lambda r, n: (cur(n), r + qcol)),
                  blk(lambda r, n: (prev(n), r * 2)), blk(lambda r, n: (cur(n), r * 2)),
                  blk(lambda r, n: (prev(n), r * 2 + 1)), blk(lambda r, n: (cur(n), r * 2 + 1)),
                  pl.BlockSpec((H, BLK, 2 * BLK), lambda r, n: (0, 0, 0)),
                  blk(lambda r, n: (cur(n), r)), blk(lambda r, n: (cur(n), r)),
                  blk(lambda r, n: (cur(n), r))],
        out_specs=[blk(lambda r, n: (cur(n), r)), blk(lambda r, n: (lag(n), r)),
                   blk(lambda r, n: (lag(n), r)),
                   pl.BlockSpec((H, BLK, 2 * BLK), lambda r, n: (0, 0, 0))],
        out_shape=[jax.ShapeDtypeStruct((S, d * HD), BF16), jax.ShapeDtypeStruct((S, d * HD), F32),
                   jax.ShapeDtypeStruct((S, d * HD), F32),
                   jax.ShapeDtypeStruct((H, BLK, 2 * BLK), F32)],
        scratch_shapes=[pltpu.VMEM((BLK, HD), F32), pltpu.VMEM((BLK, HD), F32)],
        compiler_params=_params(("arbitrary", "arbitrary")), name=name,
    )(qv, kvv, kvv, kvv, kvv, bias, dov, ov, Lv)


class _Weights(dict):
    def __init__(self, base, fetch=None, emit=None):
        super().__init__(base)
        self._fetch, self._emit = fetch, emit

    def fetch(self, group, after):
        if self._fetch is not None:
            for (key, layer), mat in self._fetch(group, after).items():
                self[key][layer] = mat

    def emit(self, group, mats):
        return None if self._emit is None else self._emit(group, mats)


def _local_step(x, tgt, W):
    T, D = x.shape
    H = W["rel_table"].shape[1] // N_GROUPS
    HD = H * HEAD_DIM
    G = W["a_w_s"].shape[1]
    assert T % (DILATED_GROUPS[-1][1] * BLK) == 0

    tril = jnp.tril(jnp.ones((CHUNK, CHUNK), F32))
    bmap = jnp.asarray(_bucket_maps())
    bias = _band_bias(W["rel_table"], bmap, H, name="band_bias")

    saved = []
    xc, xcb = x, x.astype(BF16)
    kvb = None
    for i in range(DEPTH):
        s = {"x": xc, "xb": xcb}
        W.fetch(2 * i, xc)
        if i < N_A:
            ws_m = W["a_w_s"][i] * tril
            s["ws"] = ws_m.astype(BF16)
            s["wst"] = jnp.swapaxes(ws_m, 1, 2).astype(BF16)
            s["bst"] = W["a_b_s"][i].T
            s["zp"] = _mm(xcb, W["a_w_in"][i], name=f"a_in_{i}")
            s["y"] = _sgu_fwd(s["zp"], W["a_ln_g"][i], W["a_ln_b"][i], s["ws"], s["bst"], name=f"sgu_fwd_{i}")
            s["h"] = _mm(s["y"], W["a_w_out"][i], name=f"a_out_{i}")
        else:
            j = i - N_A
            if kvb is None:
                kvb = _mm(xcb, W["kv_w"][0], out_dtype=BF16, name="kv_proj")
            s["q"] = _mm(xcb, W["b_w_q_t"][j], tb=True, out_dtype=BF16, scale=HEAD_DIM ** -0.5,
                         name=f"q_proj_{j}")
            os, ls = [], []
            for gi in range(N_GROUPS):
                o_g, l_g = _attn_fwd(s["q"], kvb, bias[gi], gi, name=f"attn_fwd_{j}_{gi}")
                os.append(o_g)
                ls.append(l_g)
            s["o"], s["ob"], s["L"] = _attn_combine(os, ls, name=f"attn_mix_{j}")
            s["h"] = _mm(s["ob"], W["b_w_o"][j], name=f"o_proj_{j}")
        s["x1"], s["x1b"] = _add_ln_fwd(xc, s["h"], W["ln_g"][i, 0], W["ln_b"][i, 0], name=f"ln1_fwd_{i}")
        W.fetch(2 * i + 1, s["x1"])
        s["hh"] = _mm(s["x1b"], W["ffn_w_up_t"][i], tb=True, name=f"ffn_up_{i}")
        s["cw"] = W["ffn_conv_w"][i]
        s["cb"] = W["ffn_conv_b"][i].reshape(1, -1)
        s["act"] = _convgate_fwd(s["hh"], s["cw"], s["cb"], name=f"convgate_fwd_{i}")
        s["f"] = _mm(s["act"], W["ffn_w_down"][i], name=f"ffn_down_{i}")
        xc, xcb = _add_ln_fwd(s["x1"], s["f"], W["ln_g"][i, 1], W["ln_b"][i, 1], name=f"ln2_fwd_{i}")
        saved.append(s)

    dy, lossv = _loss_grad(xc, tgt, name="loss_grad")
    loss = lossv[0, 0]

    gl = {k: [None] * DEPTH for k in ("ffn_w_up_t", "ffn_conv_w", "ffn_conv_b", "ffn_w_down", "ln_g", "ln_b")}
    ga = {k: [None] * N_A for k in ("a_w_in", "a_ln_g", "a_ln_b", "a_w_s", "a_b_s", "a_w_out")}
    gb = {k: [None] * (DEPTH - N_A) for k in ("b_w_q_t", "b_w_o")}
    mats = ("a_w_in", "a_w_out", "b_w_q_t", "b_w_o", "ffn_w_up_t", "ffn_w_down")
    dks, dvs, dbias = [], [], []
    grads = {}
    terms = [(1.0, dy)]
    tok = None
    behind = lambda g: g if tok is None else g + tok
    for i in reversed(range(DEPTH)):
        s = saved[i]
        dp2, dp2b, dg2, db2 = _add_ln_bwd(s["x1"], s["f"], behind(W["ln_g"][i, 1]), terms, name=f"ln2_bwd_{i}")
        dact = _mm(dp2b, W["ffn_w_down"][i], tb=True, name=f"ffn_down_dx_{i}")
        gl["ffn_w_down"][i] = _mm(s["act"], dp2b, ta=True, out_dtype=BF16, name=f"ffn_down_dw_{i}")
        dha, dhg, dwa, dwg, dba, dbg = _convgate_bwd(s["hh"], dact, s["cw"], s["cb"], name=f"convgate_bwd_{i}")
        dhh = jnp.concatenate([dha, dhg], axis=1)
        gl["ffn_conv_w"][i] = jnp.concatenate([dwa, dwg], axis=1)
        gl["ffn_conv_b"][i] = jnp.concatenate([dba, dbg], axis=1)[0]
        dx1 = _mm(dhh, W["ffn_w_up_t"][i], name=f"ffn_up_dx_{i}")
        gl["ffn_w_up_t"][i] = _mm(dhh, s["x1b"], ta=True, out_dtype=BF16, name=f"ffn_up_dw_{i}")
        tok = W.emit(2 * i + 1, {("ffn_w_up_t", i): gl["ffn_w_up_t"][i], ("ffn_w_down", i): gl["ffn_w_down"][i]})
        dp1, dp1b, dg1, db1 = _add_ln_bwd(s["x"], s["h"], behind(W["ln_g"][i, 0]), [(ALPHA, dp2), (1.0, dx1)],
                                          name=f"ln1_bwd_{i}")
        gl["ln_g"][i] = jnp.concatenate([dg1, dg2], axis=0)
        gl["ln_b"][i] = jnp.concatenate([db1, db2], axis=0)
        terms = [(ALPHA, dp1)]
        if i < N_A:
            dyy = _mm(dp1b, W["a_w_out"][i], tb=True, name=f"a_out_dx_{i}")
            ga["a_w_out"][i] = _mm(s["y"], dp1b, ta=True, out_dtype=BF16, name=f"a_out_dw_{i}")
            dzp, dlg, dlb, dws, dbs = _sgu_bwd(s["zp"], dyy, W["a_ln_g"][i], W["a_ln_b"][i], s["ws"], s["wst"],
                                               s["bst"], name=f"sgu_bwd_{i}")
            ga["a_ln_g"][i], ga["a_ln_b"][i], ga["a_w_s"][i] = dlg[0], dlb[0], dws
            ga["a_b_s"][i] = dbs[:, :G].T
            terms.append((1.0, _mm(dzp, W["a_w_in"][i], tb=True, name=f"a_in_dx_{i}")))
            ga["a_w_in"][i] = _mm(s["xb"], dzp, ta=True, out_dtype=BF16, name=f"a_in_dw_{i}")
            tok = W.emit(2 * i, {("a_w_in", i): ga["a_w_in"][i], ("a_w_out", i): ga["a_w_out"][i]})
        else:
            j = i - N_A
            do = _mm(dp1b, W["b_w_o"][j], tb=True, name=f"o_proj_dx_{j}")
            gb["b_w_o"][j] = _mm(s["ob"], dp1b, ta=True, out_dtype=BF16, name=f"o_proj_dw_{j}")
            dqs, dbl = [], []
            for gi in range(N_GROUPS):
                dq_g, dk_g, dv_g, db_g = _attn_bwd(s["q"], kvb, bias[gi], do, s["o"], s["L"], gi,
                                                   name=f"attn_bwd_{j}_{gi}")
                dqs.append(dq_g.reshape(T, HD))
                dks.append((1.0, dk_g.reshape(T, HD)))
                dvs.append((1.0, dv_g.reshape(T, HD)))
                dbl.append(db_g)
            dbias.append(jnp.stack(dbl))
            dq = jnp.concatenate(dqs, axis=1)
            terms.append((1.0, _mm(dq, W["b_w_q_t"][j], name=f"q_proj_dx_{j}")))
            gb["b_w_q_t"][j] = _mm(dq, s["xb"], ta=True, out_dtype=BF16, name=f"q_proj_dw_{j}")
            out_b = {("b_w_q_t", j): gb["b_w_q_t"][j], ("b_w_o", j): gb["b_w_o"][j]}
            if i == N_A:
                dkv = jnp.concatenate([_lincomb(dks, BF16, name="dk_sum"), _lincomb(dvs, BF16, name="dv_sum")],
                                      axis=1)
                terms.append((1.0, _mm(dkv, W["kv_w"][0], tb=True, name="kv_proj_dx")))
                grads["kv_w"] = [_mm(s["xb"], dkv, ta=True, out_dtype=BF16, name="kv_proj_dw")]
                out_b[("kv_w", 0)] = grads["kv_w"][0]
                dbt = _lincomb([(1.0, a.reshape(-1, 2 * BLK)) for a in dbias], F32, name="dbias_sum")
                dtab = _band_bias_bwd(dbt.reshape(N_GROUPS, H, BLK, 2 * BLK), bmap, H, name="band_bias_bwd")
                grads["rel_table"] = jnp.transpose(dtab[:, :, :H], (1, 0, 2)).reshape(REL_BUCKETS, N_GROUPS * H)
            tok = W.emit(2 * i, out_b)
    grad_x = _lincomb(terms, F32, name="grad_x")
    for dct in (gl, ga, gb):
        for k, v in dct.items():
            grads[k] = v if k in mats else jnp.stack(v)
    return loss, grad_x, grads


def _my_index():
    return 4 * lax.axis_index("x") + 2 * lax.axis_index("y") + lax.axis_index("c")


HBM_SPEC = pl.BlockSpec(memory_space=pltpu.HBM)


def _block(ref, k, n, axis):
    off = pl.multiple_of(k * n, n)
    return ref.at[pl.ds(off, n), :] if axis == 0 else ref.at[:, pl.ds(off, n)]


def _gather_mats(local, axis, *, name):
    L, a, b = local.shape
    n = a if axis == 0 else b
    full = (a * N_DEV, b) if axis == 0 else (a, b * N_DEV)

    def body(x_ref, *rest):
        outs = rest[:L]
        send_sems, recv_sems, local_sems = rest[L:]
        x, y, c = lax.axis_index("x"), lax.axis_index("y"), lax.axis_index("c")
        me, sibling = (x, y, c), (x, y, 1 - c)
        chips = [(1 - x, y), (x, 1 - y), (1 - x, 1 - y)]

        def slot(l, px, py, pc):
            return _block(outs[l], 4 * px + 2 * py + pc, n, axis)

        def copy(l, k, blk, to, src=None):
            return pltpu.make_async_remote_copy(
                src_ref=slot(l, *blk) if src is None else src, dst_ref=slot(l, *blk),
                send_sem=send_sems.at[7 * l + k], recv_sem=recv_sems.at[7 * l + k],
                device_id=to, device_id_type=MESH)

        mine, first, passed = [], [], []
        for l in range(L):
            mine.append(pltpu.make_async_copy(x_ref.at[l], slot(l, *me), local_sems.at[l]))
            mine[-1].start()
            first.append(copy(l, 0, me, sibling, src=x_ref.at[l]))
            first += [copy(l, 1 + j, me, (*chip, c), src=x_ref.at[l]) for j, chip in enumerate(chips)]
        for cp in first:
            cp.start()
        for l in range(L):
            for j, chip in enumerate(chips):
                copy(l, 1 + j, (*chip, c), me).wait_recv()
                passed.append(copy(l, 4 + j, (*chip, c), sibling))
                passed[-1].start()
        for l in range(L):
            copy(l, 0, sibling, me).wait_recv()
            for j, chip in enumerate(chips):
                copy(l, 4 + j, (*chip, 1 - c), me).wait_recv()
        for cp in first + passed:
            cp.wait_send()
        for cp in mine:
            cp.wait()

    return pl.pallas_call(
        body, out_shape=[jax.ShapeDtypeStruct(full, local.dtype)] * L,
        in_specs=[HBM_SPEC], out_specs=[HBM_SPEC] * L,
        scratch_shapes=[pltpu.SemaphoreType.DMA((7 * L,)), pltpu.SemaphoreType.DMA((7 * L,)),
                        pltpu.SemaphoreType.DMA((L,))],
        name=name,
    )(local)


SEM_SPEC = pl.BlockSpec(memory_space=pltpu.SEMAPHORE)
FLOWING = pltpu.SideEffectType.DATAFLOW_SIDE_EFFECTING


def _peers(x, y, c):
    return [(1 - x if k & 4 else x, 1 - y if k & 2 else y, 1 - c if k & 1 else c) for k in range(1, N_DEV)]


def _ends(src_ref, land_ref, peer_index, me, n, axis, gather):
    if gather:
        return src_ref, _block(land_ref, me, n, axis)
    return _block(src_ref, peer_index, n, axis), land_ref.at[me]


def _send_start(groups, gather, *, name):
    flat = [(g, j, mat, axis) for g, items in enumerate(groups) for j, (mat, axis) in enumerate(items)]
    M, G = len(flat), len(groups)
    lands, ns = [], []
    for _, _, mat, axis in flat:
        A, B = mat.shape
        if gather:
            lands.append((A * N_DEV, B) if axis == 0 else (A, B * N_DEV))
            ns.append(A if axis == 0 else B)
        else:
            lands.append((N_DEV, A // N_DEV, B) if axis == 0 else (N_DEV, A, B // N_DEV))
            ns.append(A // N_DEV if axis == 0 else B // N_DEV)

    def body(*refs):
        src_refs, land_refs, sems = refs[:M], refs[M:2 * M], refs[2 * M:2 * M + 2 * G]
        token = refs[-1]
        x, y, c = lax.axis_index("x"), lax.axis_index("y"), lax.axis_index("c")
        me = 4 * x + 2 * y + c
        for i, (g, j, _, axis) in enumerate(flat):
            for k, (px, py, pc) in enumerate(_peers(x, y, c)):
                s, d = _ends(src_refs[i], land_refs[i], 4 * px + 2 * py + pc, me, ns[i], axis, gather)
                pltpu.make_async_remote_copy(
                    src_ref=s, dst_ref=d, send_sem=sems[2 * g].at[7 * j + k], recv_sem=sems[2 * g + 1].at[7 * j + k],
                    device_id=(px, py, pc), device_id_type=MESH).start()
        token[...] = jnp.zeros_like(token)

    sem_shapes = []
    for items in groups:
        sem_shapes += [pltpu.SemaphoreType.DMA((7 * len(items),))] * 2
    outs = pl.pallas_call(
        body, name=name,
        out_shape=(*sem_shapes, *[pltpu.HBM(m.shape, m.dtype) for _, _, m, _ in flat],
                   *[pltpu.HBM(shp, m.dtype) for shp, (_, _, m, _) in zip(lands, flat)],
                   jax.ShapeDtypeStruct((8, LANES), F32)),
        in_specs=[HBM_SPEC] * (2 * M),
        out_specs=(*[SEM_SPEC] * (2 * G), *[HBM_SPEC] * (2 * M), pl.BlockSpec(memory_space=pltpu.VMEM)),
        input_output_aliases={i: 2 * G + i for i in range(2 * M)},
        compiler_params=pltpu.CompilerParams(has_side_effects=FLOWING),
    )(*[pltpu.with_memory_space_constraint(m, pltpu.HBM) for _, _, m, _ in flat],
      *[pltpu.with_memory_space_constraint(lax.empty(shp, m.dtype), pltpu.HBM)
        for shp, (_, _, m, _) in zip(lands, flat)])
    handles = []
    for g in range(G):
        idx = [i for i, f in enumerate(flat) if f[0] == g]
        handles.append((outs[2 * g], outs[2 * g + 1], [outs[2 * G + i] for i in idx],
                        [outs[2 * G + M + i] for i in idx], [flat[i][3] for i in idx]))
    return handles, outs[-1]


def _send_wait(handle, gather, after, *, name):
    send_sems, recv_sems, mats, lands, axes = handle
    n_m = len(mats)
    ns = []
    for mat, land, axis in zip(mats, lands, axes):
        ns.append(mat.shape[axis] if gather else land.shape[1 + axis])

    def body(*refs):
        src_refs, land_refs = refs[:n_m], refs[n_m:2 * n_m]
        ssem, rsem = refs[2 * n_m], refs[2 * n_m + 1]
        local_sems = refs[-1]
        x, y, c = lax.axis_index("x"), lax.axis_index("y"), lax.axis_index("c")
        me = 4 * x + 2 * y + c
        mine = []
        for j in range(n_m):
            s, d = _ends(src_refs[j], land_refs[j], me, me, ns[j], axes[j], gather)
            mine.append(pltpu.make_async_copy(s, d, local_sems.at[j]))
            mine[-1].start()
        for j in range(n_m):
            for k, (px, py, pc) in enumerate(_peers(x, y, c)):
                s, d = _ends(src_refs[j], land_refs[j], 4 * px + 2 * py + pc, me, ns[j], axes[j], gather)
                cp = pltpu.make_async_remote_copy(
                    src_ref=s, dst_ref=d, send_sem=ssem.at[7 * j + k], recv_sem=rsem.at[7 * j + k],
                    device_id=(px, py, pc), device_id_type=MESH)
                cp.wait_send()
                cp.wait_recv()
        for cp in mine:
            cp.wait()

    outs = pl.pallas_call(
        body, name=name,
        out_shape=(*[pltpu.HBM(m.shape, m.dtype) for m in mats], *[pltpu.HBM(l.shape, l.dtype) for l in lands]),
        in_specs=[HBM_SPEC] * (2 * n_m) + [SEM_SPEC, SEM_SPEC, pl.BlockSpec(memory_space=pl.ANY)],
        out_specs=tuple([HBM_SPEC] * (2 * n_m)),
        input_output_aliases={i: i for i in range(2 * n_m)},
        scratch_shapes=[pltpu.SemaphoreType.DMA((n_m,))],
        compiler_params=pltpu.CompilerParams(has_side_effects=FLOWING),
    )(*mats, *lands, send_sems, recv_sems, after)
    return list(outs[n_m:])


def _sum_parts(parts, *, name):
    n, R, C = parts.shape
    rb = _pick(R, 512) if R % LANES == 0 else R

    def body(p_ref, o_ref):
        acc = p_ref[0].astype(F32)
        for k in range(1, n):
            acc = acc + p_ref[k].astype(F32)
        o_ref[...] = acc

    return pl.pallas_call(
        body, grid=(R // rb,), in_specs=[pl.BlockSpec((n, rb, C), lambda i: (0, i, 0))],
        out_specs=pl.BlockSpec((rb, C), lambda i: (i, 0)),
        out_shape=jax.ShapeDtypeStruct((R, C), F32),
        compiler_params=_params(("parallel",)), name=name,
    )(parts)


def _adamw(w, m, v, parts, *, name):
    L, R, C = w.shape
    n = parts[0].shape[0]
    cap = max(16, VMEM_LIMIT // 3 // (2 * L * n * C * parts[0].dtype.itemsize))
    rb = max([r for r in range(16, min(R, cap) + 1, 16) if R % r == 0], default=R)

    def body(w_ref, m_ref, v_ref, *rest):
        p_refs = rest[:L]
        g_ref, d_ref, nm_ref, nv_ref = rest[L:]
        for l in range(L):
            @pl.when(pl.program_id(0) == l)
            def _(p_ref=p_refs[l]):
                g = p_ref[0].astype(F32)
                for k in range(1, n):
                    g = g + p_ref[k].astype(F32)
                mn = ADAM_B1 * m_ref[...] + (1.0 - ADAM_B1) * g
                vn = ADAM_B2 * v_ref[...] + (1.0 - ADAM_B2) * jnp.square(g)
                m_hat = mn / (1.0 - ADAM_B1 ** ADAM_STEP)
                v_hat = vn / (1.0 - ADAM_B2 ** ADAM_STEP)
                g_ref[...] = g
                d_ref[...] = -ADAM_LR * (m_hat / (jnp.sqrt(v_hat) + ADAM_EPS) + ADAM_WD * w_ref[...])
                nm_ref[...] = mn
                nv_ref[...] = vn

    row = pl.BlockSpec((None, rb, C), lambda l, i: (l, i, 0))
    part = lambda k: pl.BlockSpec((n, rb, C), lambda l, i: (0, jnp.where(l == k, i, 0), 0))
    return pl.pallas_call(
        body, grid=(L, R // rb), in_specs=[row, row, row] + [part(k) for k in range(L)],
        out_specs=[row] * 4, out_shape=[jax.ShapeDtypeStruct((L, R, C), F32)] * 4,
        compiler_params=_params(("arbitrary", "arbitrary")), name=name,
    )(w, m, v, *parts)


BIG = (("a_w_in", "a_w_in", 1, False), ("a_w_out", "a_w_out", 0, False), ("kv_w", "kv_w", 0, False),
       ("b_w_q", "b_w_q_t", 0, True), ("b_w_o", "b_w_o", 1, False), ("ffn_w_up", "ffn_w_up_t", 0, True),
       ("ffn_w_down", "ffn_w_down", 0, False))
SMALL_SHARDED = (("a_ln_g", 1), ("a_ln_b", 1), ("ffn_conv_w", 2), ("ln_g", 2), ("ln_b", 2))
REPLICATED = ("a_w_s", "a_b_s", "rel_table", "ffn_conv_b")


def _pack_rows(arrs, lead=0):
    lshape = arrs[0].shape[:lead]
    p = jnp.concatenate([a.reshape(*lshape, -1, LANES) for a in arrs], axis=lead)
    pad = -p.shape[lead] % 8
    return jnp.pad(p, [(0, 0)] * lead + [(0, pad), (0, 0)])


def _unpack_rows(packed, shapes, lead=0):
    lshape = packed.shape[:lead]
    out, off = [], 0
    for shp in shapes:
        r = int(np.prod(shp)) // LANES
        out.append(lax.slice_in_dim(packed, off, off + r, axis=lead).reshape(*lshape, *shp))
        off += r
    return out


def _as_mats(a, transposed):
    a = a[None] if a.ndim == 2 else a
    return jnp.swapaxes(a, 1, 2) if transposed else a


def _merge_shards(stacked, axis):
    a = jnp.moveaxis(stacked, 0, axis)
    shp = list(a.shape)
    return a.reshape(shp[:axis] + [shp[axis] * shp[axis + 1]] + shp[axis + 2:])


def _split_shards(full, axis):
    shp = list(full.shape)
    a = full.reshape(shp[:axis] + [N_DEV, shp[axis] // N_DEV] + shp[axis + 1:])
    return jnp.moveaxis(a, axis, 0)


def kernel(x, a_w_in, a_ln_g, a_ln_b, a_w_s, a_b_s, a_w_out, kv_w, b_w_q, b_w_o, rel_table, ffn_w_up, ffn_conv_w, ffn_conv_b, ffn_w_down, ln_g, ln_b, loss_target, m_a_w_in, m_a_ln_g, m_a_ln_b, m_a_w_s, m_a_b_s, m_a_w_out, m_kv_w, m_b_w_q, m_b_w_o, m_rel_table, m_ffn_w_up, m_ffn_conv_w, m_ffn_conv_b, m_ffn_w_down, m_ln_g, m_ln_b, v_a_w_in, v_a_ln_g, v_a_ln_b, v_a_w_s, v_a_b_s, v_a_w_out, v_kv_w, v_b_w_q, v_b_w_o, v_rel_table, v_ffn_w_up, v_ffn_conv_w, v_ffn_conv_b, v_ffn_w_down, v_ln_g, v_ln_b):
    names = ["a_w_in", "a_ln_g", "a_ln_b", "a_w_s", "a_b_s", "a_w_out", "kv_w", "b_w_q", "b_w_o", "rel_table",
             "ffn_w_up", "ffn_conv_w", "ffn_conv_b", "ffn_w_down", "ln_g", "ln_b"]
    w = dict(zip(names, (a_w_in, a_ln_g, a_ln_b, a_w_s, a_b_s, a_w_out, kv_w, b_w_q, b_w_o, rel_table,
                         ffn_w_up, ffn_conv_w, ffn_conv_b, ffn_w_down, ln_g, ln_b)))
    m = dict(zip(names, (m_a_w_in, m_a_ln_g, m_a_ln_b, m_a_w_s, m_a_b_s, m_a_w_out, m_kv_w, m_b_w_q, m_b_w_o,
                         m_rel_table, m_ffn_w_up, m_ffn_conv_w, m_ffn_conv_b, m_ffn_w_down, m_ln_g, m_ln_b)))
    v = dict(zip(names, (v_a_w_in, v_a_ln_g, v_a_ln_b, v_a_w_s, v_a_b_s, v_a_w_out, v_kv_w, v_b_w_q, v_b_w_o,
                         v_rel_table, v_ffn_w_up, v_ffn_conv_w, v_ffn_conv_b, v_ffn_w_down, v_ln_g, v_ln_b)))
    small_names = [n for n, _ in SMALL_SHARDED]
    small_shapes = [w[n].shape for n in small_names]
    rep_shapes = [w[n].shape for n in REPLICATED]

    axis_of = {key: axis for _, key, axis, _ in BIG}
    src = {}
    for n, key, axis, tr in BIG:
        loc = _as_mats(w[n], tr).astype(BF16)
        for l in range(loc.shape[0]):
            src[(key, l)] = loc[l]
    order = []
    for i in range(DEPTH):
        mixer = ([("a_w_in", i), ("a_w_out", i)] if i < N_A
                 else ([("kv_w", 0)] if i == N_A else []) + [("b_w_q_t", i - N_A), ("b_w_o", i - N_A)])
        order += [mixer, [("ffn_w_up_t", i), ("ffn_w_down", i)]]
    small_src = _pack_rows([w[n] for n in small_names])
    srows = small_src.shape[0]
    handles, _ = _send_start([[(small_src, 0)]] + [[(src[kl], axis_of[kl[0]]) for kl in grp] for grp in order],
                             True, name="gather_start")
    small_all = _send_wait(handles[0], True, x, name="gather_wait_small")[0]
    small_st = _unpack_rows(small_all.reshape(N_DEV, srows, LANES), small_shapes, lead=1)
    base = {n: w[n] for n in REPLICATED}
    for (n, ax), st in zip(SMALL_SHARDED, small_st):
        base[n] = _merge_shards(st, ax)
    for n, key, _, tr in BIG:
        base[key] = [None] * (1 if w[n].ndim == 2 else w[n].shape[0])

    def fetch(group, after):
        mats = _send_wait(handles[1 + group], True, after, name=f"gather_wait_{group}")
        return dict(zip(order[group], mats))

    sent = {}

    def emit(group, mats):
        keys = list(mats)
        hs, token = _send_start([[(mats[kl], axis_of[kl[0]]) for kl in keys]], False, name=f"exchange_start_{group}")
        sent[group] = (keys, hs[0])
        return token[0, 0]

    loss, grad_x, grads = _local_step(x[0], loss_target[0], _Weights(base, fetch, emit))
    loss = lax.psum(loss, ("x", "y", "c"))

    landed = {}
    for group in sorted(sent, reverse=True):
        keys, h = sent[group]
        landed.update(zip(keys, _send_wait(h, False, grad_x, name=f"exchange_wait_{group}")))
    out = {}
    for n, key, axis, tr in BIG:
        shp = w[n].shape
        L = 1 if len(shp) == 2 else shp[0]
        parts = [landed[(key, l)] for l in range(L)]
        if tr:
            parts = [_sum_parts(p, name=f"sum_{n}_{l}").T[None] for l, p in enumerate(parts)]
        shp3 = (L, *shp[-2:])
        res = _adamw(w[n].reshape(shp3), m[n].reshape(shp3), v[n].reshape(shp3), parts, name=f"adamw_{n}")
        out[n] = [r.reshape(shp) for r in res]

    small_pack = _pack_rows([_split_shards(grads[n], ax) for n, ax in SMALL_SHARDED], lead=1)
    rep_pack = _pack_rows([grads[n] for n in REPLICATED])
    mine = jnp.concatenate([small_pack.reshape(N_DEV * srows, LANES), rep_pack], axis=0)
    allp = _gather_mats(mine[None], 0, name="gather_small_grads")[0]
    gsum = _sum_parts(allp.reshape(N_DEV, mine.shape[0], LANES), name="sum_small_grads")
    g_small = lax.dynamic_slice_in_dim(gsum, _my_index() * srows, srows, axis=0)
    gs_in = jnp.concatenate([g_small, gsum[N_DEV * srows:]], axis=0)[None]
    pack_sr = lambda d: jnp.concatenate([_pack_rows([d[n] for n in small_names]),
                                         _pack_rows([d[n] for n in REPLICATED])], axis=0)
    res = _adamw(pack_sr(w)[None], pack_sr(m)[None], pack_sr(v)[None], [gs_in], name="adamw_small")
    for n, vals in zip(small_names, zip(*[_unpack_rows(r[0, :srows], small_shapes) for r in res])):
        out[n] = list(vals)
    for n, vals in zip(REPLICATED, zip(*[_unpack_rows(r[0, srows:], rep_shapes) for r in res])):
        out[n] = list(vals)

    return (loss, grad_x[None], *[out[n][0] for n in names], *[out[n][1] for n in names],
            *[out[n][2] for n in names], *[out[n][3] for n in names])
```

```python
import math

import numpy as np
import jax
import jax.numpy as jnp
from jax import lax
from jax.experimental import pallas as pl
from jax.experimental.pallas import tpu as pltpu

F32 = jnp.float32
BF16 = jnp.bfloat16
MESH = pl.DeviceIdType.MESH

N_DEV = 8
DEPTH = 4
N_A = 2
CHUNK = 128
BLK = 128
HEAD_DIM = 64
DILATED_GROUPS = ((128, 1), (512, 4), (2048, 16))
N_GROUPS = 3
REL_BUCKETS = 32
REL_MAX_DIST = 2048
ALPHA = (2 * DEPTH) ** 0.25
LN_EPS = 1e-5
NEG = -1e30
ADAM_LR = 0.001
ADAM_B1 = 0.9
ADAM_B2 = 0.999
ADAM_EPS = 1e-08
ADAM_WD = 0.01
ADAM_STEP = 10

LANES = 128
VMEM_LIMIT = 56 * 1024 * 1024
MM_TILE_CAP = 1408
INV_SQRT2 = 1.0 / math.sqrt(2.0)
INV_SQRT_2PI = 1.0 / math.sqrt(2.0 * math.pi)


def _pick(n, cap):
    best = None
    for t in range(LANES, min(n, cap) + 1, LANES):
        if n % t == 0:
            best = t
    return best if best is not None else n


def _params(sem):
    return pltpu.CompilerParams(dimension_semantics=sem, vmem_limit_bytes=VMEM_LIMIT)


def _gelu(x):
    return 0.5 * x * (1.0 + lax.erf(x * INV_SQRT2))


def _gelu_grad(x):
    return 0.5 * (1.0 + lax.erf(x * INV_SQRT2)) + x * jnp.exp(-0.5 * x * x) * INV_SQRT_2PI


def _mm(a, b, *, ta=False, tb=False, out_dtype=F32, scale=None, name):
    if ta:
        K, M = a.shape
    else:
        M, K = a.shape
    if tb:
        N, Kb = b.shape
    else:
        Kb, N = b.shape
    assert K == Kb, (a.shape, b.shape, ta, tb)
    tm, tn, tk = _pick(M, MM_TILE_CAP), _pick(N, MM_TILE_CAP), _pick(K, MM_TILE_CAP)
    nk = K // tk
    dn = (((0 if ta else 1,), (1 if tb else 0,)), ((), ()))

    def body(a_ref, b_ref, o_ref, acc_ref):
        k = pl.program_id(2)
        part = lax.dot_general(a_ref[...].astype(BF16), b_ref[...].astype(BF16), dn,
                               preferred_element_type=F32)

        @pl.when(k == 0)
        def _():
            acc_ref[...] = part

        @pl.when(k > 0)
        def _():
            acc_ref[...] += part

        @pl.when(k == nk - 1)
        def _():
            r = acc_ref[...]
            if scale is not None:
                r = r * scale
            o_ref[...] = r.astype(out_dtype)

    a_spec = (pl.BlockSpec((tk, tm), lambda i, j, k: (k, i)) if ta
              else pl.BlockSpec((tm, tk), lambda i, j, k: (i, k)))
    b_spec = (pl.BlockSpec((tn, tk), lambda i, j, k: (j, k)) if tb
              else pl.BlockSpec((tk, tn), lambda i, j, k: (k, j)))
    return pl.pallas_call(
        body, grid=(M // tm, N // tn, nk), in_specs=[a_spec, b_spec],
        out_specs=pl.BlockSpec((tm, tn), lambda i, j, k: (i, j)),
        out_shape=jax.ShapeDtypeStruct((M, N), out_dtype),
        scratch_shapes=[pltpu.VMEM((tm, tn), F32)],
        compiler_params=_params(("parallel", "parallel", "arbitrary")), name=name,
    )(a, b)


def _add_ln_fwd(x, h, g, b, *, name):
    T, D = x.shape
    rb = _pick(T, 512)

    def body(x_ref, h_ref, g_ref, b_ref, o_ref, ob_ref):
        pre = ALPHA * x_ref[...] + h_ref[...]
        mu = jnp.mean(pre, axis=1, keepdims=True)
        cen = pre - mu
        var = jnp.mean(cen * cen, axis=1, keepdims=True)
        y = cen * lax.rsqrt(var + LN_EPS) * g_ref[...] + b_ref[...]
        o_ref[...] = y
        ob_ref[...] = y.astype(BF16)

    row = pl.BlockSpec((rb, D), lambda i: (i, 0))
    vec = pl.BlockSpec((1, D), lambda i: (0, 0))
    return pl.pallas_call(
        body, grid=(T // rb,), in_specs=[row, row, vec, vec], out_specs=[row, row],
        out_shape=[jax.ShapeDtypeStruct((T, D), F32), jax.ShapeDtypeStruct((T, D), BF16)],
        compiler_params=_params(("parallel",)), name=name,
    )(x, h, g.reshape(1, D), b.reshape(1, D))


def _add_ln_bwd(x, h, g, terms, *, name):
    T, D = x.shape
    rb = _pick(T, 512)
    coefs = [c for c, _ in terms]
    nt = len(terms)

    def body(*refs):
        x_ref, h_ref, g_ref = refs[:3]
        t_refs = refs[3:3 + nt]
        dp_ref, dpb_ref, dg_ref, db_ref = refs[3 + nt:]
        dy = None
        for c, r in zip(coefs, t_refs):
            v = r[...] if c == 1.0 else c * r[...]
            dy = v if dy is None else dy + v
        pre = ALPHA * x_ref[...] + h_ref[...]
        mu = jnp.mean(pre, axis=1, keepdims=True)
        cen = pre - mu
        var = jnp.mean(cen * cen, axis=1, keepdims=True)
        rstd = lax.rsqrt(var + LN_EPS)
        xhat = cen * rstd
        dxh = dy * g_ref[...]
        m1 = jnp.mean(dxh, axis=1, keepdims=True)
        m2 = jnp.mean(dxh * xhat, axis=1, keepdims=True)
        dpre = rstd * (dxh - m1 - xhat * m2)
        dp_ref[...] = dpre
        dpb_ref[...] = dpre.astype(BF16)
        dg = jnp.sum(dy * xhat, axis=0, keepdims=True)
        db = jnp.sum(dy, axis=0, keepdims=True)

        @pl.when(pl.program_id(0) == 0)
        def _():
            dg_ref[...] = dg
            db_ref[...] = db

        @pl.when(pl.program_id(0) > 0)
        def _():
            dg_ref[...] += dg
            db_ref[...] += db

    row = pl.BlockSpec((rb, D), lambda i: (i, 0))
    vec = pl.BlockSpec((1, D), lambda i: (0, 0))
    return pl.pallas_call(
        body, grid=(T // rb,), in_specs=[row, row, vec] + [row] * nt,
        out_specs=[row, row, vec, vec],
        out_shape=[jax.ShapeDtypeStruct((T, D), F32), jax.ShapeDtypeStruct((T, D), BF16),
                   jax.ShapeDtypeStruct((1, D), F32), jax.ShapeDtypeStruct((1, D), F32)],
        compiler_params=_params(("arbitrary",)), name=name,
    )(x, h, g.reshape(1, D), *[a for _, a in terms])


def _lincomb(terms, out_dtype, *, name):
    R, C = terms[0][1].shape
    rb = _pick(R, 512)
    coefs = [c for c, _ in terms]
    nt = len(terms)

    def body(*refs):
        acc = None
        for c, r in zip(coefs, refs[:nt]):
            v = r[...].astype(F32)
            v = v if c == 1.0 else c * v
            acc = v if acc is None else acc + v
        refs[nt][...] = acc.astype(out_dtype)

    row = pl.BlockSpec((rb, C), lambda i: (i, 0))
    return pl.pallas_call(
        body, grid=(R // rb,), in_specs=[row] * nt, out_specs=row,
        out_shape=jax.ShapeDtypeStruct((R, C), out_dtype),
        compiler_params=_params(("parallel",)), name=name,
    )(*[a for _, a in terms])


def _loss_grad(y, tgt, *, name):
    T, D = y.shape
    rb = _pick(T, 512)

    def body(y_ref, t_ref, dy_ref, l_ref):
        err = y_ref[...] - t_ref[...]
        dy_ref[...] = err * (1.0 / D)
        part = jnp.sum(jnp.sum(err * err, axis=1, keepdims=True), axis=0, keepdims=True) * (0.5 / D)
        part = jnp.broadcast_to(part, (1, LANES))

        @pl.when(pl.program_id(0) == 0)
        def _():
            l_ref[...] = part

        @pl.when(pl.program_id(0) > 0)
        def _():
            l_ref[...] += part

    row = pl.BlockSpec((rb, D), lambda i: (i, 0))
    return pl.pallas_call(
        body, grid=(T // rb,), in_specs=[row, row],
        out_specs=[row, pl.BlockSpec((1, LANES), lambda i: (0, 0))],
        out_shape=[jax.ShapeDtypeStruct((T, D), F32), jax.ShapeDtypeStruct((1, LANES), F32)],
        compiler_params=_params(("arbitrary",)), name=name,
    )(y, tgt)


def _sgu_fwd(zp, ln_g, ln_b, ws, bst, *, name):
    T, E2 = zp.shape
    E = E2 // 2
    G = ws.shape[0]
    cg = E // G
    rb = 2 * CHUNK

    def body(z_ref, g_ref, b_ref, ws_ref, bs_ref, y_ref):
        u = _gelu(z_ref[:, :E])
        v = _gelu(z_ref[:, E:])
        mu = jnp.mean(v, axis=1, keepdims=True)
        cen = v - mu
        var = jnp.mean(cen * cen, axis=1, keepdims=True)
        vn = (cen * lax.rsqrt(var + LN_EPS) * g_ref[...] + b_ref[...]).astype(BF16)
        for ci in range(rb // CHUNK):
            rows = slice(ci * CHUNK, (ci + 1) * CHUNK)
            for gi in range(G):
                cols = slice(gi * cg, (gi + 1) * cg)
                sv = jnp.dot(ws_ref[gi], vn[rows, cols], preferred_element_type=F32)
                sv = sv + bs_ref[:, gi:gi + 1]
                y_ref[rows, cols] = (u[rows, cols] * sv).astype(BF16)

    return pl.pallas_call(
        body, grid=(T // rb,),
        in_specs=[pl.BlockSpec((rb, E2), lambda i: (i, 0)),
                  pl.BlockSpec((1, E), lambda i: (0, 0)), pl.BlockSpec((1, E), lambda i: (0, 0)),
                  pl.BlockSpec((G, CHUNK, CHUNK), lambda i: (0, 0, 0)),
                  pl.BlockSpec((CHUNK, G), lambda i: (0, 0))],
        out_specs=pl.BlockSpec((rb, E), lambda i: (i, 0)),
        out_shape=jax.ShapeDtypeStruct((T, E), BF16),
        compiler_params=_params(("parallel",)), name=name,
    )(zp, ln_g.reshape(1, E), ln_b.reshape(1, E), ws, bst)


def _sgu_bwd(zp, dy, ln_g, ln_b, ws, wst, bst, *, name):
    T, E2 = zp.shape
    E = E2 // 2
    G = ws.shape[0]
    cg = E // G
    rb = CHUNK
    nsteps = T // rb

    def body(z_ref, dy_ref, g_ref, b_ref, ws_ref, wst_ref, bs_ref,
             dz_ref, dg_ref, db_ref, dws_ref, dbs_ref, dsv_acc):
        step = pl.program_id(0)

        @pl.when(step == 0)
        def _():
            dg_ref[...] = jnp.zeros_like(dg_ref)
            db_ref[...] = jnp.zeros_like(db_ref)
            dws_ref[...] = jnp.zeros_like(dws_ref)
            dsv_acc[...] = jnp.zeros_like(dsv_acc)

        zu = z_ref[:, :E]
        zv = z_ref[:, E:]
        u = _gelu(zu)
        v = _gelu(zv)
        mu = jnp.mean(v, axis=1, keepdims=True)
        cen = v - mu
        var = jnp.mean(cen * cen, axis=1, keepdims=True)
        rstd = lax.rsqrt(var + LN_EPS)
        xhat = cen * rstd
        vn = (xhat * g_ref[...] + b_ref[...]).astype(BF16)
        dyv = dy_ref[...]
        dsv = dyv * u
        dsv_acc[...] += dsv
        dsvb = dsv.astype(BF16)
        tril = (lax.broadcasted_iota(jnp.int32, (CHUNK, CHUNK), 0)
                >= lax.broadcasted_iota(jnp.int32, (CHUNK, CHUNK), 1))
        du_parts = []
        dvn_parts = []
        for gi in range(G):
            cols = slice(gi * cg, (gi + 1) * cg)
            sv = jnp.dot(ws_ref[gi], vn[:, cols], preferred_element_type=F32) + bs_ref[:, gi:gi + 1]
            du_parts.append(dyv[:, cols] * sv)
            dvn_parts.append(jnp.dot(wst_ref[gi], dsvb[:, cols], preferred_element_type=F32))
            dw = lax.dot_general(dsvb[:, cols], vn[:, cols], (((1,), (1,)), ((), ())),
                                 preferred_element_type=F32)
            dws_ref[gi] += jnp.where(tril, dw, 0.0)
        du = jnp.concatenate(du_parts, axis=1)
        dvn = jnp.concatenate(dvn_parts, axis=1)
        dg_ref[...] += jnp.sum(dvn * xhat, axis=0, keepdims=True)
        db_ref[...] += jnp.sum(dvn, axis=0, keepdims=True)
        dxh = dvn * g_ref[...]
        m1 = jnp.mean(dxh, axis=1, keepdims=True)
        m2 = jnp.mean(dxh * xhat, axis=1, keepdims=True)
        dv = rstd * (dxh - m1 - xhat * m2)
        dz_ref[:, :E] = (du * _gelu_grad(zu)).astype(BF16)
        dz_ref[:, E:] = (dv * _gelu_grad(zv)).astype(BF16)

        @pl.when(step == nsteps - 1)
        def _():
            lane = lax.broadcasted_iota(jnp.int32, (CHUNK, LANES), 1)
            out = jnp.zeros((CHUNK, LANES), F32)
            for gi in range(G):
                s = jnp.sum(dsv_acc[:, gi * cg:(gi + 1) * cg], axis=1, keepdims=True)
                out = jnp.where(lane == gi, s, out)
            dbs_ref[...] = out

    vecE = pl.BlockSpec((1, E), lambda i: (0, 0))
    wspec = pl.BlockSpec((G, CHUNK, CHUNK), lambda i: (0, 0, 0))
    return pl.pallas_call(
        body, grid=(nsteps,),
        in_specs=[pl.BlockSpec((rb, E2), lambda i: (i, 0)), pl.BlockSpec((rb, E), lambda i: (i, 0)),
                  vecE, vecE, wspec, wspec, pl.BlockSpec((CHUNK, G), lambda i: (0, 0))],
        out_specs=[pl.BlockSpec((rb, E2), lambda i: (i, 0)), vecE, vecE, wspec,
                   pl.BlockSpec((CHUNK, LANES), lambda i: (0, 0))],
        out_shape=[jax.ShapeDtypeStruct((T, E2), BF16), jax.ShapeDtypeStruct((1, E), F32),
                   jax.ShapeDtypeStruct((1, E), F32), jax.ShapeDtypeStruct((G, CHUNK, CHUNK), F32),
                   jax.ShapeDtypeStruct((CHUNK, LANES), F32)],
        scratch_shapes=[pltpu.VMEM((CHUNK, E), F32)],
        compiler_params=_params(("arbitrary",)), name=name,
    )(zp, dy, ln_g.reshape(1, E), ln_b.reshape(1, E), ws, wst, bst)


def _shift_down(x, k, row):
    return jnp.where(row >= k, pltpu.roll(x, k, 0), 0.0)


def _shift_up(x, k, row, T):
    return jnp.where(row < T - k, pltpu.roll(x, T - k, 0), 0.0)


def _conv3(x, w_ref, b_ref, row):
    return (w_ref[0:1, :] * _shift_down(x, 2, row) + w_ref[1:2, :] * _shift_down(x, 1, row)
            + w_ref[2:3, :] * x + b_ref[...])


def _convgate_fwd(hh, cw, cb, *, name):
    T, F2 = hh.shape
    F = F2 // 2
    ns = F // LANES

    def body(a_ref, g_ref, wa_ref, wg_ref, ba_ref, bg_ref, o_ref):
        row = lax.broadcasted_iota(jnp.int32, (T, LANES), 0)
        ca = _conv3(a_ref[...], wa_ref, ba_ref, row)
        cgv = _conv3(g_ref[...], wg_ref, bg_ref, row)
        o_ref[...] = (_gelu(ca) * cgv).astype(BF16)

    sa = lambda r: pl.BlockSpec((r, LANES), lambda j: (0, j))
    sg = lambda r: pl.BlockSpec((r, LANES), lambda j: (0, j + ns))
    return pl.pallas_call(
        body, grid=(ns,), in_specs=[sa(T), sg(T), sa(3), sg(3), sa(1), sg(1)],
        out_specs=sa(T), out_shape=jax.ShapeDtypeStruct((T, F), BF16),
        compiler_params=_params(("parallel",)), name=name,
    )(hh, hh, cw, cw, cb, cb)


def _convgate_bwd(hh, dact, cw, cb, *, name):
    T, F2 = hh.shape
    F = F2 // 2
    ns = F // LANES

    def body(a_ref, g_ref, d_ref, wa_ref, wg_ref, ba_ref, bg_ref,
             da_ref, dg_ref, dwa_ref, dwg_ref, dba_ref, dbg_ref):
        row = lax.broadcasted_iota(jnp.int32, (T, LANES), 0)
        d = d_ref[...].astype(F32)
        ca = _conv3(a_ref[...], wa_ref, ba_ref, row)
        cgv = _conv3(g_ref[...], wg_ref, bg_ref, row)
        cdf = 0.5 * (1.0 + lax.erf(ca * INV_SQRT2))
        dca = d * cgv * (cdf + ca * jnp.exp(-0.5 * ca * ca) * INV_SQRT_2PI)
        dcg = d * (ca * cdf)
        for x_ref, w_ref, dc, dx_ref, dw_ref, db_ref in (
                (a_ref, wa_ref, dca, da_ref, dwa_ref, dba_ref),
                (g_ref, wg_ref, dcg, dg_ref, dwg_ref, dbg_ref)):
            x = x_ref[...]
            dx = (w_ref[2:3, :] * dc + w_ref[1:2, :] * _shift_up(dc, 1, row, T)
                  + w_ref[0:1, :] * _shift_up(dc, 2, row, T))
            dx_ref[...] = dx.astype(BF16)
            dw_ref[0:1, :] = jnp.sum(dc * _shift_down(x, 2, row), axis=0, keepdims=True)
            dw_ref[1:2, :] = jnp.sum(dc * _shift_down(x, 1, row), axis=0, keepdims=True)
            dw_ref[2:3, :] = jnp.sum(dc * x, axis=0, keepdims=True)
            db_ref[...] = jnp.sum(dc, axis=0, keepdims=True)

    sa = lambda r: pl.BlockSpec((r, LANES), lambda j: (0, j))
    sg = lambda r: pl.BlockSpec((r, LANES), lambda j: (0, j + ns))
    return pl.pallas_call(
        body, grid=(ns,), in_specs=[sa(T), sg(T), sa(T), sa(3), sg(3), sa(1), sg(1)],
        out_specs=[sa(T), sa(T), sa(3), sa(3), sa(1), sa(1)],
        out_shape=[jax.ShapeDtypeStruct((T, F), BF16), jax.ShapeDtypeStruct((T, F), BF16),
                   jax.ShapeDtypeStruct((3, F), F32), jax.ShapeDtypeStruct((3, F), F32),
                   jax.ShapeDtypeStruct((1, F), F32), jax.ShapeDtypeStruct((1, F), F32)],
        compiler_params=_params(("parallel",)), name=name,
    )(hh, hh, dact, cw, cw, cb, cb)


def _bucket_maps():
    iq = np.arange(BLK)[:, None]
    ik = np.arange(2 * BLK)[None, :]
    delta = iq + BLK - ik
    maps = []
    for win, dil in DILATED_GROUPS:
        n = np.clip(delta, 0, None) * dil
        max_exact = REL_BUCKETS // 2
        nf = np.maximum(n, 1).astype(np.float32)
        large = max_exact + (np.log(nf / np.float32(max_exact)) / np.float32(math.log(REL_MAX_DIST / max_exact))
                             * np.float32(REL_BUCKETS - max_exact)).astype(np.int32)
        large = np.minimum(large, REL_BUCKETS - 1)
        bucket = np.where(n < max_exact, n, large)
        valid = (delta >= 0) & (delta <= win // dil)
        maps.append(np.where(valid, bucket, -1).astype(np.int32))
    return np.stack(maps)


def _band_bias(rel_table, bmap, H, *, name):
    def body(t_ref, m_ref, o_ref):
        g = pl.program_id(0)
        bm = m_ref[0]
        for h in range(H):
            acc = jnp.full((BLK, 2 * BLK), NEG, F32)
            for b in range(REL_BUCKETS):
                acc = jnp.where(bm == b, t_ref[b, g * H + h], acc)
            o_ref[0, h] = acc

    return pl.pallas_call(
        body, grid=(N_GROUPS,),
        in_specs=[pl.BlockSpec(memory_space=pltpu.SMEM),
                  pl.BlockSpec((1, BLK, 2 * BLK), lambda g: (g, 0, 0))],
        out_specs=pl.BlockSpec((1, H, BLK, 2 * BLK), lambda g: (g, 0, 0, 0)),
        out_shape=jax.ShapeDtypeStruct((N_GROUPS, H, BLK, 2 * BLK), F32),
        compiler_params=_params(("parallel",)), name=name,
    )(rel_table, bmap)


def _band_bias_bwd(dbias, bmap, H, *, name):
    def body(d_ref, m_ref, o_ref):
        bm = m_ref[0]
        rowi = lax.broadcasted_iota(jnp.int32, (REL_BUCKETS, LANES), 0)
        lane = lax.broadcasted_iota(jnp.int32, (REL_BUCKETS, LANES), 1)
        out = jnp.zeros((REL_BUCKETS, LANES), F32)
        for h in range(H):
            dv = d_ref[0, h]
            for b in range(REL_BUCKETS):
                s = jnp.sum(jnp.sum(jnp.where(bm == b, dv, 0.0), axis=1, keepdims=True),
                            axis=0, keepdims=True)
                out = jnp.where((rowi == b) & (lane == h), s, out)
        o_ref[0] = out

    return pl.pallas_call(
        body, grid=(N_GROUPS,),
        in_specs=[pl.BlockSpec((1, H, BLK, 2 * BLK), lambda g: (g, 0, 0, 0)),
                  pl.BlockSpec((1, BLK, 2 * BLK), lambda g: (g, 0, 0))],
        out_specs=pl.BlockSpec((1, REL_BUCKETS, LANES), lambda g: (g, 0, 0)),
        out_shape=jax.ShapeDtypeStruct((N_GROUPS, REL_BUCKETS, LANES), F32),
        compiler_params=_params(("parallel",)), name=name,
    )(dbias, bmap)


def _head_masks():
    lane = lax.broadcasted_iota(jnp.int32, (BLK, LANES), 1)
    return (lane < HEAD_DIM, lane >= HEAD_DIM)


def _attn_fwd(q, kv, bias, gi, *, name):
    T = q.shape[0]
    HD = kv.shape[1] // 2
    d = DILATED_GROUPS[gi][1]
    S = T // d
    NB = S // BLK
    H = HD // HEAD_DIM
    qv, qcol = (q, gi) if d == 1 else (q[:, gi * HD:(gi + 1) * HD].reshape(S, d * HD), 0)
    kvv = kv.reshape(S, d * 2 * HD)

    def body(q_ref, kp_ref, kc_ref, vp_ref, vc_ref, b_ref, o_ref, l_ref):
        n = pl.program_id(1)
        col = lax.broadcasted_iota(jnp.int32, (BLK, 2 * BLK), 1)
        first = (n == 0) & (col < BLK)
        hm = _head_masks()
        for p in range(HD // LANES):
            sl = slice(p * LANES, (p + 1) * LANES)
            qp = q_ref[:, sl]
            kc = jnp.concatenate([kp_ref[:, sl], kc_ref[:, sl]], axis=0)
            vc = jnp.concatenate([vp_ref[:, sl], vc_ref[:, sl]], axis=0)
            outs = []
            lses = []
            for hh in range(2):
                qm = jnp.where(hm[hh], qp, jnp.zeros_like(qp))
                s = lax.dot_general(qm, kc, (((1,), (1,)), ((), ())), preferred_element_type=F32)
                s = jnp.where(first, NEG, s + b_ref[2 * p + hh])
                m = jnp.max(s, axis=1, keepdims=True)
                e = jnp.exp(s - m)
                den = jnp.sum(e, axis=1, keepdims=True)
                outs.append(jnp.dot((e / den).astype(BF16), vc, preferred_element_type=F32))
                lses.append(m + jnp.log(den))
            o_ref[:, sl] = jnp.where(hm[0], outs[0], outs[1])
            l_ref[:, sl] = jnp.where(hm[0], lses[0], lses[1])

    blk = lambda f: pl.BlockSpec((BLK, HD), f)
    prev = lambda n: jnp.maximum(n - 1, 0)
    return pl.pallas_call(
        body, grid=(d, NB),
        in_specs=[blk(lambda r, n: (n, r + qcol)),
                  blk(lambda r, n: (prev(n), r * 2)), blk(lambda r, n: (n, r * 2)),
                  blk(lambda r, n: (prev(n), r * 2 + 1)), blk(lambda r, n: (n, r * 2 + 1)),
                  pl.BlockSpec((H, BLK, 2 * BLK), lambda r, n: (0, 0, 0))],
        out_specs=[blk(lambda r, n: (n, r)), blk(lambda r, n: (n, r))],
        out_shape=[jax.ShapeDtypeStruct((S, d * HD), F32), jax.ShapeDtypeStruct((S, d * HD), F32)],
        compiler_params=_params(("parallel", "parallel")), name=name,
    )(qv, kvv, kvv, kvv, kvv, bias)


def _attn_combine(os, ls, *, name):
    T, HD = os[0].shape
    rb = _pick(T, 512)

    def body(o0, o1, o2, l0, l1, l2, o_ref, ob_ref, l_ref):
        la, lb, lc = l0[...], l1[...], l2[...]
        m = jnp.maximum(jnp.maximum(la, lb), lc)
        L = m + jnp.log(jnp.exp(la - m) + jnp.exp(lb - m) + jnp.exp(lc - m))
        o = jnp.exp(la - L) * o0[...] + jnp.exp(lb - L) * o1[...] + jnp.exp(lc - L) * o2[...]
        o_ref[...] = o
        ob_ref[...] = o.astype(BF16)
        l_ref[...] = L

    row = pl.BlockSpec((rb, HD), lambda i: (i, 0))
    return pl.pallas_call(
        body, grid=(T // rb,), in_specs=[row] * 6, out_specs=[row] * 3,
        out_shape=[jax.ShapeDtypeStruct((T, HD), F32), jax.ShapeDtypeStruct((T, HD), BF16),
                   jax.ShapeDtypeStruct((T, HD), F32)],
        compiler_params=_params(("parallel",)), name=name,
    )(*[a.reshape(T, HD) for a in os], *[a.reshape(T, HD) for a in ls])


def _attn_bwd(q, kv, bias, do, o, L, gi, *, name):
    T = q.shape[0]
    HD = kv.shape[1] // 2
    d = DILATED_GROUPS[gi][1]
    S = T // d
    NB = S // BLK
    H = HD // HEAD_DIM
    qv, qcol = (q, gi) if d == 1 else (q[:, gi * HD:(gi + 1) * HD].reshape(S, d * HD), 0)
    kvv = kv.reshape(S, d * 2 * HD)
    dov, ov, Lv = (a.reshape(S, d * HD) for a in (do, o, L))

    def body(q_ref, kp_ref, kc_ref, vp_ref, vc_ref, b_ref, do_ref, o_ref, L_ref,
             dq_ref, dk_ref, dv_ref, db_ref, ck_ref, cv_ref):
        r = pl.program_id(0)
        n = pl.program_id(1)

        @pl.when((r == 0) & (n == 0))
        def _():
            db_ref[...] = jnp.zeros_like(db_ref)

        @pl.when(n == 0)
        def _():
            ck_ref[...] = jnp.zeros_like(ck_ref)
            cv_ref[...] = jnp.zeros_like(cv_ref)

        @pl.when(n < NB)
        def _():
            col = lax.broadcasted_iota(jnp.int32, (BLK, 2 * BLK), 1)
            first = (n == 0) & (col < BLK)
            hm = _head_masks()
            for p in range(HD // LANES):
                sl = slice(p * LANES, (p + 1) * LANES)
                qp = q_ref[:, sl]
                kc = jnp.concatenate([kp_ref[:, sl], kc_ref[:, sl]], axis=0)
                vc = jnp.concatenate([vp_ref[:, sl], vc_ref[:, sl]], axis=0)
                dop = do_ref[:, sl]
                dob = dop.astype(BF16)
                prod = dop * o_ref[:, sl]
                Lp = L_ref[:, sl]
                dq_parts = []
                dkc = None
                dvc = None
                for hh in range(2):
                    qm = jnp.where(hm[hh], qp, jnp.zeros_like(qp))
                    dom = jnp.where(hm[hh], dob, jnp.zeros_like(dob))
                    s = lax.dot_general(qm, kc, (((1,), (1,)), ((), ())), preferred_element_type=F32)
                    s = jnp.where(first, NEG, s + b_ref[2 * p + hh])
                    lse = Lp[:, hh * HEAD_DIM:hh * HEAD_DIM + 1]
                    pr = jnp.exp(s - lse)
                    dp = lax.dot_general(dom, vc, (((1,), (1,)), ((), ())), preferred_element_type=F32)
                    delta = jnp.sum(jnp.where(hm[hh], prod, 0.0), axis=1, keepdims=True)
                    ds = pr * (dp - delta)
                    db_ref[2 * p + hh] += ds
                    dsb = ds.astype(BF16)
                    dq_parts.append(jnp.dot(dsb, kc, preferred_element_type=F32))
                    dkh = lax.dot_general(dsb, qm, (((0,), (0,)), ((), ())), preferred_element_type=F32)
                    dvh = lax.dot_general(pr.astype(BF16), dom, (((0,), (0,)), ((), ())),
                                          preferred_element_type=F32)
                    dkc = dkh if dkc is None else dkc + dkh
                    dvc = dvh if dvc is None else dvc + dvh
                dq = jnp.where(hm[0], dq_parts[0], dq_parts[1])
                dq_ref[:, sl] = (dq * (HEAD_DIM ** -0.5)).astype(BF16)
                dk_ref[:, sl] = ck_ref[:, sl] + dkc[:BLK]
                dv_ref[:, sl] = cv_ref[:, sl] + dvc[:BLK]
                ck_ref[:, sl] = dkc[BLK:]
                cv_ref[:, sl] = dvc[BLK:]

        @pl.when(n == NB)
        def _():
            dk_ref[...] = ck_ref[...]
            dv_ref[...] = cv_ref[...]

    blk = lambda f: pl.BlockSpec((BLK, HD), f)
    cur = lambda n: jnp.minimum(n, NB - 1)
    prev = lambda n: jnp.maximum(jnp.minimum(n, NB - 1) - 1, 0)
    lag = lambda n: jnp.maximum(n - 1, 0)
    return pl.pallas_call(
        body, grid=(d, NB + 1),
        in_specs=[blk(lambda r, n: (cur(n), r + qcol)),
                  blk(lambda r, n: (prev(n), r * 2)), blk(lambda r, n: (cur(n), r * 2)),
                  blk(lambda r, n: (prev(n), r * 2 + 1)), blk(lambda r, n: (cur(n), r * 2 + 1)),
                  pl.BlockSpec((H, BLK, 2 * BLK), lambda r, n: (0, 0, 0)),
                  blk(lambda r, n: (cur(n), r)), blk(lambda r, n: (cur(n), r)),
                  blk(lambda r, n: (cur(n), r))],
        out_specs=[blk(lambda r, n: (cur(n), r)), blk(lambda r, n: (lag(n), r)),
                   blk(lambda r, n: (lag(n), r)),
                   pl.BlockSpec((H, BLK, 2 * BLK), lambda r, n: (0, 0, 0))],
        out_shape=[jax.ShapeDtypeStruct((S, d * HD), BF16), jax.ShapeDtypeStruct((S, d * HD), F32),
                   jax.ShapeDtypeStruct((S, d * HD), F32),
                   jax.ShapeDtypeStruct((H, BLK, 2 * BLK), F32)],
        scratch_shapes=[pltpu.VMEM((BLK, HD), F32), pltpu.VMEM((BLK, HD), F32)],
        compiler_params=_params(("arbitrary", "arbitrary")), name=name,
    )(qv, kvv, kvv, kvv, kvv, bias, dov, ov, Lv)


class _Weights(dict):
    def __init__(self, base, fetch=None, emit=None):
        super().__init__(base)
        self._fetch, self._emit = fetch, emit

    def fetch(self, group, after):
        if self._fetch is not None:
            for (key, layer), mat in self._fetch(group, after).items():
                self[key][layer] = mat

    def emit(self, group, mats):
        return None if self._emit is None else self._emit(group, mats)


def _local_step(x, tgt, W):
    T, D = x.shape
    H = W["rel_table"].shape[1] // N_GROUPS
    HD = H * HEAD_DIM
    G = W["a_w_s"].shape[1]
    assert T % (DILATED_GROUPS[-1][1] * BLK) == 0

    tril = jnp.tril(jnp.ones((CHUNK, CHUNK), F32))
    bmap = jnp.asarray(_bucket_maps())
    bias = _band_bias(W["rel_table"], bmap, H, name="band_bias")

    saved = []
    xc, xcb = x, x.astype(BF16)
    kvb = None
    for i in range(DEPTH):
        s = {"x": xc, "xb": xcb}
        W.fetch(3 * i, xc)
        if i < N_A:
            ws_m = W["a_w_s"][i] * tril
            s["ws"] = ws_m.astype(BF16)
            s["wst"] = jnp.swapaxes(ws_m, 1, 2).astype(BF16)
            s["bst"] = W["a_b_s"][i].T
            s["zp"] = _mm(xcb, W["a_w_in"][i], name=f"a_in_{i}")
            s["y"] = _sgu_fwd(s["zp"], W["a_ln_g"][i], W["a_ln_b"][i], s["ws"], s["bst"], name=f"sgu_fwd_{i}")
            W.fetch(3 * i + 1, s["zp"])
            s["h"] = _mm(s["y"], W["a_w_out"][i], name=f"a_out_{i}")
        else:
            j = i - N_A
            if kvb is None:
                kvb = _mm(xcb, W["kv_w"][0], out_dtype=BF16, name="kv_proj")
            s["q"] = _mm(xcb, W["b_w_q_t"][j], tb=True, out_dtype=BF16, scale=HEAD_DIM ** -0.5,
                         name=f"q_proj_{j}")
            os, ls = [], []
            for gi in range(N_GROUPS):
                o_g, l_g = _attn_fwd(s["q"], kvb, bias[gi], gi, name=f"attn_fwd_{j}_{gi}")
                os.append(o_g)
                ls.append(l_g)
            s["o"], s["ob"], s["L"] = _attn_combine(os, ls, name=f"attn_mix_{j}")
            W.fetch(3 * i + 1, s["q"])
            s["h"] = _mm(s["ob"], W["b_w_o"][j], name=f"o_proj_{j}")
        s["x1"], s["x1b"] = _add_ln_fwd(xc, s["h"], W["ln_g"][i, 0], W["ln_b"][i, 0], name=f"ln1_fwd_{i}")
        W.fetch(3 * i + 2, s["x1"])
        s["hh"] = _mm(s["x1b"], W["ffn_w_up_t"][i], tb=True, name=f"ffn_up_{i}")
        s["cw"] = W["ffn_conv_w"][i]
        s["cb"] = W["ffn_conv_b"][i].reshape(1, -1)
        s["act"] = _convgate_fwd(s["hh"], s["cw"], s["cb"], name=f"convgate_fwd_{i}")
        s["f"] = _mm(s["act"], W["ffn_w_down"][i], name=f"ffn_down_{i}")
        xc, xcb = _add_ln_fwd(s["x1"], s["f"], W["ln_g"][i, 1], W["ln_b"][i, 1], name=f"ln2_fwd_{i}")
        saved.append(s)

    dy, lossv = _loss_grad(xc, tgt, name="loss_grad")
    loss = lossv[0, 0]

    gl = {k: [None] * DEPTH for k in ("ffn_w_up_t", "ffn_conv_w", "ffn_conv_b", "ffn_w_down", "ln_g", "ln_b")}
    ga = {k: [None] * N_A for k in ("a_w_in", "a_ln_g", "a_ln_b", "a_w_s", "a_b_s", "a_w_out")}
    gb = {k: [None] * (DEPTH - N_A) for k in ("b_w_q_t", "b_w_o")}
    mats = ("a_w_in", "a_w_out", "b_w_q_t", "b_w_o", "ffn_w_up_t", "ffn_w_down")
    dks, dvs, dbias = [], [], []
    grads = {}
    terms = [(1.0, dy)]
    tok = None
    behind = lambda g: g if tok is None else g + tok
    for i in reversed(range(DEPTH)):
        s = saved[i]
        dp2, dp2b, dg2, db2 = _add_ln_bwd(s["x1"], s["f"], behind(W["ln_g"][i, 1]), terms, name=f"ln2_bwd_{i}")
        dact = _mm(dp2b, W["ffn_w_down"][i], tb=True, name=f"ffn_down_dx_{i}")
        gl["ffn_w_down"][i] = _mm(s["act"], dp2b, ta=True, out_dtype=BF16, name=f"ffn_down_dw_{i}")
        dha, dhg, dwa, dwg, dba, dbg = _convgate_bwd(s["hh"], dact, s["cw"], s["cb"], name=f"convgate_bwd_{i}")
        dhh = jnp.concatenate([dha, dhg], axis=1)
        gl["ffn_conv_w"][i] = jnp.concatenate([dwa, dwg], axis=1)
        gl["ffn_conv_b"][i] = jnp.concatenate([dba, dbg], axis=1)[0]
        dx1 = _mm(dhh, W["ffn_w_up_t"][i], name=f"ffn_up_dx_{i}")
        gl["ffn_w_up_t"][i] = _mm(dhh, s["x1b"], ta=True, out_dtype=BF16, name=f"ffn_up_dw_{i}")
        tok = W.emit(3 * i + 2, {("ffn_w_up_t", i): gl["ffn_w_up_t"][i], ("ffn_w_down", i): gl["ffn_w_down"][i]})
        dp1, dp1b, dg1, db1 = _add_ln_bwd(s["x"], s["h"], behind(W["ln_g"][i, 0]), [(ALPHA, dp2), (1.0, dx1)],
                                          name=f"ln1_bwd_{i}")
        gl["ln_g"][i] = jnp.concatenate([dg1, dg2], axis=0)
        gl["ln_b"][i] = jnp.concatenate([db1, db2], axis=0)
        terms = [(ALPHA, dp1)]
        if i < N_A:
            dyy = _mm(dp1b, W["a_w_out"][i], tb=True, name=f"a_out_dx_{i}")
            ga["a_w_out"][i] = _mm(s["y"], dp1b, ta=True, out_dtype=BF16, name=f"a_out_dw_{i}")
            tok = W.emit(3 * i + 1, {("a_w_out", i): ga["a_w_out"][i]})
            dzp, dlg, dlb, dws, dbs = _sgu_bwd(s["zp"], dyy, behind(W["a_ln_g"][i]), W["a_ln_b"][i], s["ws"],
                                               s["wst"], s["bst"], name=f"sgu_bwd_{i}")
            ga["a_ln_g"][i], ga["a_ln_b"][i], ga["a_w_s"][i] = dlg[0], dlb[0], dws
            ga["a_b_s"][i] = dbs[:, :G].T
            terms.append((1.0, _mm(dzp, W["a_w_in"][i], tb=True, name=f"a_in_dx_{i}")))
            ga["a_w_in"][i] = _mm(s["xb"], dzp, ta=True, out_dtype=BF16, name=f"a_in_dw_{i}")
            tok = W.emit(3 * i, {("a_w_in", i): ga["a_w_in"][i]})
        else:
            j = i - N_A
            do = _mm(dp1b, W["b_w_o"][j], tb=True, name=f"o_proj_dx_{j}")
            gb["b_w_o"][j] = _mm(s["ob"], dp1b, ta=True, out_dtype=BF16, name=f"o_proj_dw_{j}")
            tok = W.emit(3 * i + 1, {("b_w_o", j): gb["b_w_o"][j]})
            bias_b = behind(bias)
            dqs, dbl = [], []
            for gi in range(N_GROUPS):
                dq_g, dk_g, dv_g, db_g = _attn_bwd(s["q"], kvb, bias_b[gi], do, s["o"], s["L"], gi,
                                                   name=f"attn_bwd_{j}_{gi}")
                dqs.append(dq_g.reshape(T, HD))
                dks.append((1.0, dk_g.reshape(T, HD)))
                dvs.append((1.0, dv_g.reshape(T, HD)))
                dbl.append(db_g)
            dbias.append(jnp.stack(dbl))
            dq = jnp.concatenate(dqs, axis=1)
            terms.append((1.0, _mm(dq, W["b_w_q_t"][j], name=f"q_proj_dx_{j}")))
            gb["b_w_q_t"][j] = _mm(dq, s["xb"], ta=True, out_dtype=BF16, name=f"q_proj_dw_{j}")
            out_b = {("b_w_q_t", j): gb["b_w_q_t"][j]}
            if i == N_A:
                dkv = jnp.concatenate([_lincomb(dks, BF16, name="dk_sum"), _lincomb(dvs, BF16, name="dv_sum")],
                                      axis=1)
                terms.append((1.0, _mm(dkv, W["kv_w"][0], tb=True, name="kv_proj_dx")))
                grads["kv_w"] = [_mm(s["xb"], dkv, ta=True, out_dtype=BF16, name="kv_proj_dw")]
                out_b[("kv_w", 0)] = grads["kv_w"][0]
                dbt = _lincomb([(1.0, a.reshape(-1, 2 * BLK)) for a in dbias], F32, name="dbias_sum")
                dtab = _band_bias_bwd(dbt.reshape(N_GROUPS, H, BLK, 2 * BLK), bmap, H, name="band_bias_bwd")
                grads["rel_table"] = jnp.transpose(dtab[:, :, :H], (1, 0, 2)).reshape(REL_BUCKETS, N_GROUPS * H)
            tok = W.emit(3 * i, out_b)
    grad_x = _lincomb(terms, F32, name="grad_x")
    for dct in (gl, ga, gb):
        for k, v in dct.items():
            grads[k] = v if k in mats else jnp.stack(v)
    return loss, grad_x, grads


def _my_index():
    return 4 * lax.axis_index("x") + 2 * lax.axis_index("y") + lax.axis_index("c")


HBM_SPEC = pl.BlockSpec(memory_space=pltpu.HBM)


def _block(ref, k, n, axis):
    off = pl.multiple_of(k * n, n)
    return ref.at[pl.ds(off, n), :] if axis == 0 else ref.at[:, pl.ds(off, n)]


def _gather_mats(local, axis, *, name):
    L, a, b = local.shape
    n = a if axis == 0 else b
    full = (a * N_DEV, b) if axis == 0 else (a, b * N_DEV)

    def body(x_ref, *rest):
        outs = rest[:L]
        send_sems, recv_sems, local_sems = rest[L:]
        x, y, c = lax.axis_index("x"), lax.axis_index("y"), lax.axis_index("c")
        me, sibling = (x, y, c), (x, y, 1 - c)
        chips = [(1 - x, y), (x, 1 - y), (1 - x, 1 - y)]

        def slot(l, px, py, pc):
            return _block(outs[l], 4 * px + 2 * py + pc, n, axis)

        def copy(l, k, blk, to, src=None):
            return pltpu.make_async_remote_copy(
                src_ref=slot(l, *blk) if src is None else src, dst_ref=slot(l, *blk),
                send_sem=send_sems.at[7 * l + k], recv_sem=recv_sems.at[7 * l + k],
                device_id=to, device_id_type=MESH)

        mine, first, passed = [], [], []
        for l in range(L):
            mine.append(pltpu.make_async_copy(x_ref.at[l], slot(l, *me), local_sems.at[l]))
            mine[-1].start()
            first.append(copy(l, 0, me, sibling, src=x_ref.at[l]))
            first += [copy(l, 1 + j, me, (*chip, c), src=x_ref.at[l]) for j, chip in enumerate(chips)]
        for cp in first:
            cp.start()
        for l in range(L):
            for j, chip in enumerate(chips):
                copy(l, 1 + j, (*chip, c), me).wait_recv()
                passed.append(copy(l, 4 + j, (*chip, c), sibling))
                passed[-1].start()
        for l in range(L):
            copy(l, 0, sibling, me).wait_recv()
            for j, chip in enumerate(chips):
                copy(l, 4 + j, (*chip, 1 - c), me).wait_recv()
        for cp in first + passed:
            cp.wait_send()
        for cp in mine:
            cp.wait()

    return pl.pallas_call(
        body, out_shape=[jax.ShapeDtypeStruct(full, local.dtype)] * L,
        in_specs=[HBM_SPEC], out_specs=[HBM_SPEC] * L,
        scratch_shapes=[pltpu.SemaphoreType.DMA((7 * L,)), pltpu.SemaphoreType.DMA((7 * L,)),
                        pltpu.SemaphoreType.DMA((L,))],
        name=name,
    )(local)


SEM_SPEC = pl.BlockSpec(memory_space=pltpu.SEMAPHORE)
FLOWING = pltpu.SideEffectType.DATAFLOW_SIDE_EFFECTING


def _peers(x, y, c):
    return [(1 - x if k & 4 else x, 1 - y if k & 2 else y, 1 - c if k & 1 else c) for k in range(1, N_DEV)]


def _ends(src_ref, land_ref, peer_index, me, n, axis, gather):
    if gather:
        return src_ref, _block(land_ref, me, n, axis)
    return _block(src_ref, peer_index, n, axis), land_ref.at[me]


def _send_start(groups, gather, *, name):
    flat = [(g, j, mat, axis) for g, items in enumerate(groups) for j, (mat, axis) in enumerate(items)]
    M, G = len(flat), len(groups)
    lands, ns = [], []
    for _, _, mat, axis in flat:
        A, B = mat.shape
        if gather:
            lands.append((A * N_DEV, B) if axis == 0 else (A, B * N_DEV))
            ns.append(A if axis == 0 else B)
        else:
            lands.append((N_DEV, A // N_DEV, B) if axis == 0 else (N_DEV, A, B // N_DEV))
            ns.append(A // N_DEV if axis == 0 else B // N_DEV)

    def body(*refs):
        src_refs, land_refs, sems = refs[:M], refs[M:2 * M], refs[2 * M:2 * M + 3 * G]
        token = refs[-1]
        x, y, c = lax.axis_index("x"), lax.axis_index("y"), lax.axis_index("c")
        me = 4 * x + 2 * y + c
        for i, (g, j, _, axis) in enumerate(flat):
            for k, (px, py, pc) in enumerate(_peers(x, y, c)):
                s, d = _ends(src_refs[i], land_refs[i], 4 * px + 2 * py + pc, me, ns[i], axis, gather)
                pltpu.make_async_remote_copy(
                    src_ref=s, dst_ref=d, send_sem=sems[3 * g].at[7 * j + k], recv_sem=sems[3 * g + 1].at[7 * j + k],
                    device_id=(px, py, pc), device_id_type=MESH).start()
            s, d = _ends(src_refs[i], land_refs[i], me, me, ns[i], axis, gather)
            pltpu.make_async_copy(s, d, sems[3 * g + 2].at[j]).start()
        token[...] = jnp.zeros_like(token)

    sem_shapes = []
    for items in groups:
        sem_shapes += [pltpu.SemaphoreType.DMA((7 * len(items),))] * 2 + [pltpu.SemaphoreType.DMA((len(items),))]
    outs = pl.pallas_call(
        body, name=name,
        out_shape=(*sem_shapes, *[pltpu.HBM(m.shape, m.dtype) for _, _, m, _ in flat],
                   *[pltpu.HBM(shp, m.dtype) for shp, (_, _, m, _) in zip(lands, flat)],
                   jax.ShapeDtypeStruct((8, LANES), F32)),
        in_specs=[HBM_SPEC] * (2 * M),
        out_specs=(*[SEM_SPEC] * (3 * G), *[HBM_SPEC] * (2 * M), pl.BlockSpec(memory_space=pltpu.VMEM)),
        input_output_aliases={i: 3 * G + i for i in range(2 * M)},
        compiler_params=pltpu.CompilerParams(has_side_effects=FLOWING),
    )(*[pltpu.with_memory_space_constraint(m, pltpu.HBM) for _, _, m, _ in flat],
      *[pltpu.with_memory_space_constraint(lax.empty(shp, m.dtype), pltpu.HBM)
        for shp, (_, _, m, _) in zip(lands, flat)])
    handles = []
    for g in range(G):
        idx = [i for i, f in enumerate(flat) if f[0] == g]
        handles.append((outs[3 * g], outs[3 * g + 1], outs[3 * g + 2], [outs[3 * G + i] for i in idx],
                        [outs[3 * G + M + i] for i in idx], [flat[i][3] for i in idx]))
    return handles, outs[-1]


def _send_wait(handle, gather, after, *, name):
    send_sems, recv_sems, local_sems, mats, lands, axes = handle
    n_m = len(mats)
    ns = []
    for mat, land, axis in zip(mats, lands, axes):
        ns.append(mat.shape[axis] if gather else land.shape[1 + axis])

    def body(*refs):
        src_refs, land_refs = refs[:n_m], refs[n_m:2 * n_m]
        ssem, rsem, lsem = refs[2 * n_m:2 * n_m + 3]
        x, y, c = lax.axis_index("x"), lax.axis_index("y"), lax.axis_index("c")
        me = 4 * x + 2 * y + c
        for j in range(n_m):
            for k, (px, py, pc) in enumerate(_peers(x, y, c)):
                s, d = _ends(src_refs[j], land_refs[j], 4 * px + 2 * py + pc, me, ns[j], axes[j], gather)
                cp = pltpu.make_async_remote_copy(
                    src_ref=s, dst_ref=d, send_sem=ssem.at[7 * j + k], recv_sem=rsem.at[7 * j + k],
                    device_id=(px, py, pc), device_id_type=MESH)
                cp.wait_send()
                cp.wait_recv()
            s, d = _ends(src_refs[j], land_refs[j], me, me, ns[j], axes[j], gather)
            pltpu.make_async_copy(s, d, lsem.at[j]).wait()

    outs = pl.pallas_call(
        body, name=name,
        out_shape=(*[pltpu.HBM(m.shape, m.dtype) for m in mats], *[pltpu.HBM(l.shape, l.dtype) for l in lands]),
        in_specs=[HBM_SPEC] * (2 * n_m) + [SEM_SPEC] * 3 + [pl.BlockSpec(memory_space=pl.ANY)],
        out_specs=tuple([HBM_SPEC] * (2 * n_m)),
        input_output_aliases={i: i for i in range(2 * n_m)},
        compiler_params=pltpu.CompilerParams(has_side_effects=FLOWING),
    )(*mats, *lands, send_sems, recv_sems, local_sems, after)
    return list(outs[n_m:])


def _sum_parts(parts, *, name):
    n, R, C = parts.shape
    rb = _pick(R, 512) if R % LANES == 0 else R

    def body(p_ref, o_ref):
        acc = p_ref[0].astype(F32)
        for k in range(1, n):
            acc = acc + p_ref[k].astype(F32)
        o_ref[...] = acc

    return pl.pallas_call(
        body, grid=(R // rb,), in_specs=[pl.BlockSpec((n, rb, C), lambda i: (0, i, 0))],
        out_specs=pl.BlockSpec((rb, C), lambda i: (i, 0)),
        out_shape=jax.ShapeDtypeStruct((R, C), F32),
        compiler_params=_params(("parallel",)), name=name,
    )(parts)


def _adamw(w, m, v, parts, *, name):
    L, R, C = w.shape
    n = parts[0].shape[0]
    cap = max(16, VMEM_LIMIT // 3 // (2 * L * n * C * parts[0].dtype.itemsize))
    rb = max([r for r in range(16, min(R, cap) + 1, 16) if R % r == 0], default=R)

    def body(w_ref, m_ref, v_ref, *rest):
        p_refs = rest[:L]
        g_ref, d_ref, nm_ref, nv_ref = rest[L:]
        for l in range(L):
            @pl.when(pl.program_id(0) == l)
            def _(p_ref=p_refs[l]):
                g = p_ref[0].astype(F32)
                for k in range(1, n):
                    g = g + p_ref[k].astype(F32)
                mn = ADAM_B1 * m_ref[...] + (1.0 - ADAM_B1) * g
                vn = ADAM_B2 * v_ref[...] + (1.0 - ADAM_B2) * jnp.square(g)
                m_hat = mn / (1.0 - ADAM_B1 ** ADAM_STEP)
                v_hat = vn / (1.0 - ADAM_B2 ** ADAM_STEP)
                g_ref[...] = g
                d_ref[...] = -ADAM_LR * (m_hat / (jnp.sqrt(v_hat) + ADAM_EPS) + ADAM_WD * w_ref[...])
                nm_ref[...] = mn
                nv_ref[...] = vn

    row = pl.BlockSpec((None, rb, C), lambda l, i: (l, i, 0))
    part = lambda k: pl.BlockSpec((n, rb, C), lambda l, i: (0, jnp.where(l == k, i, 0), 0))
    return pl.pallas_call(
        body, grid=(L, R // rb), in_specs=[row, row, row] + [part(k) for k in range(L)],
        out_specs=[row] * 4, out_shape=[jax.ShapeDtypeStruct((L, R, C), F32)] * 4,
        compiler_params=_params(("arbitrary", "arbitrary")), name=name,
    )(w, m, v, *parts)


BIG = (("a_w_in", "a_w_in", 1, False), ("a_w_out", "a_w_out", 0, False), ("kv_w", "kv_w", 0, False),
       ("b_w_q", "b_w_q_t", 0, True), ("b_w_o", "b_w_o", 1, False), ("ffn_w_up", "ffn_w_up_t", 0, True),
       ("ffn_w_down", "ffn_w_down", 0, False))
SMALL_SHARDED = (("a_ln_g", 1), ("a_ln_b", 1), ("ffn_conv_w", 2), ("ln_g", 2), ("ln_b", 2))
REPLICATED = ("a_w_s", "a_b_s", "rel_table", "ffn_conv_b")


def _pack_rows(arrs, lead=0):
    lshape = arrs[0].shape[:lead]
    p = jnp.concatenate([a.reshape(*lshape, -1, LANES) for a in arrs], axis=lead)
    pad = -p.shape[lead] % 8
    return jnp.pad(p, [(0, 0)] * lead + [(0, pad), (0, 0)])


def _unpack_rows(packed, shapes, lead=0):
    lshape = packed.shape[:lead]
    out, off = [], 0
    for shp in shapes:
        r = int(np.prod(shp)) // LANES
        out.append(lax.slice_in_dim(packed, off, off + r, axis=lead).reshape(*lshape, *shp))
        off += r
    return out


def _as_mats(a, transposed):
    a = a[None] if a.ndim == 2 else a
    return jnp.swapaxes(a, 1, 2) if transposed else a


def _merge_shards(stacked, axis):
    a = jnp.moveaxis(stacked, 0, axis)
    shp = list(a.shape)
    return a.reshape(shp[:axis] + [shp[axis] * shp[axis + 1]] + shp[axis + 2:])


def _split_shards(full, axis):
    shp = list(full.shape)
    a = full.reshape(shp[:axis] + [N_DEV, shp[axis] // N_DEV] + shp[axis + 1:])
    return jnp.moveaxis(a, axis, 0)


def kernel(x, a_w_in, a_ln_g, a_ln_b, a_w_s, a_b_s, a_w_out, kv_w, b_w_q, b_w_o, rel_table, ffn_w_up, ffn_conv_w, ffn_conv_b, ffn_w_down, ln_g, ln_b, loss_target, m_a_w_in, m_a_ln_g, m_a_ln_b, m_a_w_s, m_a_b_s, m_a_w_out, m_kv_w, m_b_w_q, m_b_w_o, m_rel_table, m_ffn_w_up, m_ffn_conv_w, m_ffn_conv_b, m_ffn_w_down, m_ln_g, m_ln_b, v_a_w_in, v_a_ln_g, v_a_ln_b, v_a_w_s, v_a_b_s, v_a_w_out, v_kv_w, v_b_w_q, v_b_w_o, v_rel_table, v_ffn_w_up, v_ffn_conv_w, v_ffn_conv_b, v_ffn_w_down, v_ln_g, v_ln_b):
    names = ["a_w_in", "a_ln_g", "a_ln_b", "a_w_s", "a_b_s", "a_w_out", "kv_w", "b_w_q", "b_w_o", "rel_table",
             "ffn_w_up", "ffn_conv_w", "ffn_conv_b", "ffn_w_down", "ln_g", "ln_b"]
    w = dict(zip(names, (a_w_in, a_ln_g, a_ln_b, a_w_s, a_b_s, a_w_out, kv_w, b_w_q, b_w_o, rel_table,
                         ffn_w_up, ffn_conv_w, ffn_conv_b, ffn_w_down, ln_g, ln_b)))
    m = dict(zip(names, (m_a_w_in, m_a_ln_g, m_a_ln_b, m_a_w_s, m_a_b_s, m_a_w_out, m_kv_w, m_b_w_q, m_b_w_o,
                         m_rel_table, m_ffn_w_up, m_ffn_conv_w, m_ffn_conv_b, m_ffn_w_down, m_ln_g, m_ln_b)))
    v = dict(zip(names, (v_a_w_in, v_a_ln_g, v_a_ln_b, v_a_w_s, v_a_b_s, v_a_w_out, v_kv_w, v_b_w_q, v_b_w_o,
                         v_rel_table, v_ffn_w_up, v_ffn_conv_w, v_ffn_conv_b, v_ffn_w_down, v_ln_g, v_ln_b)))
    small_names = [n for n, _ in SMALL_SHARDED]
    small_shapes = [w[n].shape for n in small_names]
    rep_shapes = [w[n].shape for n in REPLICATED]

    axis_of = {key: axis for _, key, axis, _ in BIG}
    src = {}
    for n, key, axis, tr in BIG:
        loc = _as_mats(w[n], tr).astype(BF16)
        for l in range(loc.shape[0]):
            src[(key, l)] = loc[l]
    order = []
    for i in range(DEPTH):
        if i < N_A:
            order += [[("a_w_in", i)], [("a_w_out", i)]]
        else:
            order += [([("kv_w", 0)] if i == N_A else []) + [("b_w_q_t", i - N_A)], [("b_w_o", i - N_A)]]
        order.append([("ffn_w_up_t", i), ("ffn_w_down", i)])
    small_src = _pack_rows([w[n] for n in small_names])
    srows = small_src.shape[0]
    handles, _ = _send_start([[(small_src, 0)]] + [[(src[kl], axis_of[kl[0]]) for kl in grp] for grp in order],
                             True, name="gather_start")
    small_all = _send_wait(handles[0], True, x, name="gather_wait_small")[0]
    small_st = _unpack_rows(small_all.reshape(N_DEV, srows, LANES), small_shapes, lead=1)
    base = {n: w[n] for n in REPLICATED}
    for (n, ax), st in zip(SMALL_SHARDED, small_st):
        base[n] = _merge_shards(st, ax)
    for n, key, _, tr in BIG:
        base[key] = [None] * (1 if w[n].ndim == 2 else w[n].shape[0])

    def fetch(group, after):
        mats = _send_wait(handles[1 + group], True, after, name=f"gather_wait_{group}")
        return dict(zip(order[group], mats))

    sent = {}

    def emit(group, mats):
        keys = list(mats)
        hs, token = _send_start([[(mats[kl], axis_of[kl[0]]) for kl in keys]], False, name=f"exchange_start_{group}")
        sent[group] = (keys, hs[0])
        return token[0, 0]

    loss, grad_x, grads = _local_step(x[0], loss_target[0], _Weights(base, fetch, emit))
    loss = lax.psum(loss, ("x", "y", "c"))

    out = {}
    small_pack = _pack_rows([_split_shards(grads[n], ax) for n, ax in SMALL_SHARDED], lead=1)
    rep_pack = _pack_rows([grads[n] for n in REPLICATED])
    mine = jnp.concatenate([small_pack.reshape(N_DEV * srows, LANES), rep_pack], axis=0)
    allp = _gather_mats(mine[None], 0, name="gather_small_grads")[0]
    gsum = _sum_parts(allp.reshape(N_DEV, mine.shape[0], LANES), name="sum_small_grads")
    g_small = lax.dynamic_slice_in_dim(gsum, _my_index() * srows, srows, axis=0)
    gs_in = jnp.concatenate([g_small, gsum[N_DEV * srows:]], axis=0)[None]
    pack_sr = lambda d: jnp.concatenate([_pack_rows([d[n] for n in small_names]),
                                         _pack_rows([d[n] for n in REPLICATED])], axis=0)
    res = _adamw(pack_sr(w)[None], pack_sr(m)[None], pack_sr(v)[None], [gs_in], name="adamw_small")
    for n, vals in zip(small_names, zip(*[_unpack_rows(r[0, :srows], small_shapes) for r in res])):
        out[n] = list(vals)
    for n, vals in zip(REPLICATED, zip(*[_unpack_rows(r[0, srows:], rep_shapes) for r in res])):
        out[n] = list(vals)

    landed = {}
    for group in sorted(sent, reverse=True):
        keys, h = sent[group]
        landed.update(zip(keys, _send_wait(h, False, res[0], name=f"exchange_wait_{group}")))
    for n, key, axis, tr in BIG:
        shp = w[n].shape
        L = 1 if len(shp) == 2 else shp[0]
        parts = [landed[(key, l)] for l in range(L)]
        if tr:
            parts = [_sum_parts(p, name=f"sum_{n}_{l}").T[None] for l, p in enumerate(parts)]
        shp3 = (L, *shp[-2:])
        res = _adamw(w[n].reshape(shp3), m[n].reshape(shp3), v[n].reshape(shp3), parts, name=f"adamw_{n}")
        out[n] = [r.reshape(shp) for r in res]

    return (loss, grad_x[None], *[out[n][0] for n in names], *[out[n][1] for n in names],
            *[out[n][2] for n in names], *[out[n][3] for n in names])
```

```python
import math

import numpy as np
import jax
import jax.numpy as jnp
from jax import lax
from jax.experimental import pallas as pl
from jax.experimental.pallas import tpu as pltpu

F32 = jnp.float32
BF16 = jnp.bfloat16
MESH = pl.DeviceIdType.MESH

N_DEV = 8
DEPTH = 4
N_A = 2
CHUNK = 128
BLK = 128
HEAD_DIM = 64
DILATED_GROUPS = ((128, 1), (512, 4), (2048, 16))
N_GROUPS = 3
REL_BUCKETS = 32
REL_MAX_DIST = 2048
ALPHA = (2 * DEPTH) ** 0.25
LN_EPS = 1e-5
NEG = -1e30
ADAM_LR = 0.001
ADAM_B1 = 0.9
ADAM_B2 = 0.999
ADAM_EPS = 1e-08
ADAM_WD = 0.01
ADAM_STEP = 10

LANES = 128
VMEM_LIMIT = 56 * 1024 * 1024
MM_TILE_CAP = 1408
INV_SQRT2 = 1.0 / math.sqrt(2.0)
INV_SQRT_2PI = 1.0 / math.sqrt(2.0 * math.pi)


def _pick(n, cap):
    best = None
    for t in range(LANES, min(n, cap) + 1, LANES):
        if n % t == 0:
            best = t
    return best if best is not None else n


def _params(sem):
    return pltpu.CompilerParams(dimension_semantics=sem, vmem_limit_bytes=VMEM_LIMIT)


def _gelu(x):
    return 0.5 * x * (1.0 + lax.erf(x * INV_SQRT2))


def _gelu_grad(x):
    return 0.5 * (1.0 + lax.erf(x * INV_SQRT2)) + x * jnp.exp(-0.5 * x * x) * INV_SQRT_2PI


def _mm(a, b, *, ta=False, tb=False, out_dtype=F32, scale=None, name):
    halves = a.ndim == 3
    ash = (a.shape[1], 2 * a.shape[2]) if halves else a.shape
    if ta:
        K, M = ash
    else:
        M, K = ash
    if tb:
        N, Kb = b.shape
    else:
        Kb, N = b.shape
    assert K == Kb, (a.shape, b.shape, ta, tb)
    tm, tn, tk = _pick(M, MM_TILE_CAP), _pick(N, MM_TILE_CAP), _pick(K, MM_TILE_CAP)
    if halves and ta:
        tm = _pick(M // 2, MM_TILE_CAP)
    if halves and not ta:
        tk = _pick(K // 2, MM_TILE_CAP)
    nk = K // tk
    dn = (((0 if ta else 1,), (1 if tb else 0,)), ((), ()))

    def body(a_ref, b_ref, o_ref, acc_ref):
        k = pl.program_id(2)
        part = lax.dot_general(a_ref[...].astype(BF16), b_ref[...].astype(BF16), dn,
                               preferred_element_type=F32)

        @pl.when(k == 0)
        def _():
            acc_ref[...] = part

        @pl.when(k > 0)
        def _():
            acc_ref[...] += part

        @pl.when(k == nk - 1)
        def _():
            r = acc_ref[...]
            if scale is not None:
                r = r * scale
            o_ref[...] = r.astype(out_dtype)

    if halves and ta:
        nh = M // 2 // tm
        a_spec = pl.BlockSpec((None, tk, tm), lambda i, j, k: (i // nh, k, i % nh))
    elif halves:
        nh = K // 2 // tk
        a_spec = pl.BlockSpec((None, tm, tk), lambda i, j, k: (k // nh, i, k % nh))
    else:
        a_spec = (pl.BlockSpec((tk, tm), lambda i, j, k: (k, i)) if ta
                  else pl.BlockSpec((tm, tk), lambda i, j, k: (i, k)))
    b_spec = (pl.BlockSpec((tn, tk), lambda i, j, k: (j, k)) if tb
              else pl.BlockSpec((tk, tn), lambda i, j, k: (k, j)))
    return pl.pallas_call(
        body, grid=(M // tm, N // tn, nk), in_specs=[a_spec, b_spec],
        out_specs=pl.BlockSpec((tm, tn), lambda i, j, k: (i, j)),
        out_shape=jax.ShapeDtypeStruct((M, N), out_dtype),
        scratch_shapes=[pltpu.VMEM((tm, tn), F32)],
        compiler_params=_params(("parallel", "parallel", "arbitrary")), name=name,
    )(a, b)


def _add_ln_fwd(x, h, g, b, *, name):
    T, D = x.shape
    rb = _pick(T, 512)

    def body(x_ref, h_ref, g_ref, b_ref, o_ref, ob_ref):
        pre = ALPHA * x_ref[...] + h_ref[...]
        mu = jnp.mean(pre, axis=1, keepdims=True)
        cen = pre - mu
        var = jnp.mean(cen * cen, axis=1, keepdims=True)
        y = cen * lax.rsqrt(var + LN_EPS) * g_ref[...] + b_ref[...]
        o_ref[...] = y
        ob_ref[...] = y.astype(BF16)

    row = pl.BlockSpec((rb, D), lambda i: (i, 0))
    vec = pl.BlockSpec((1, D), lambda i: (0, 0))
    return pl.pallas_call(
        body, grid=(T // rb,), in_specs=[row, row, vec, vec], out_specs=[row, row],
        out_shape=[jax.ShapeDtypeStruct((T, D), F32), jax.ShapeDtypeStruct((T, D), BF16)],
        compiler_params=_params(("parallel",)), name=name,
    )(x, h, g.reshape(1, D), b.reshape(1, D))


def _add_ln_bwd(x, h, g, terms, *, name):
    T, D = x.shape
    rb = _pick(T, 512)
    coefs = [c for c, _ in terms]
    nt = len(terms)

    def body(*refs):
        x_ref, h_ref, g_ref = refs[:3]
        t_refs = refs[3:3 + nt]
        dp_ref, dpb_ref, dg_ref, db_ref = refs[3 + nt:]
        dy = None
        for c, r in zip(coefs, t_refs):
            v = r[...] if c == 1.0 else c * r[...]
            dy = v if dy is None else dy + v
        pre = ALPHA * x_ref[...] + h_ref[...]
        mu = jnp.mean(pre, axis=1, keepdims=True)
        cen = pre - mu
        var = jnp.mean(cen * cen, axis=1, keepdims=True)
        rstd = lax.rsqrt(var + LN_EPS)
        xhat = cen * rstd
        dxh = dy * g_ref[...]
        m1 = jnp.mean(dxh, axis=1, keepdims=True)
        m2 = jnp.mean(dxh * xhat, axis=1, keepdims=True)
        dpre = rstd * (dxh - m1 - xhat * m2)
        dp_ref[...] = dpre
        dpb_ref[...] = dpre.astype(BF16)
        dg = jnp.sum(dy * xhat, axis=0, keepdims=True)
        db = jnp.sum(dy, axis=0, keepdims=True)

        @pl.when(pl.program_id(0) == 0)
        def _():
            dg_ref[...] = dg
            db_ref[...] = db

        @pl.when(pl.program_id(0) > 0)
        def _():
            dg_ref[...] += dg
            db_ref[...] += db

    row = pl.BlockSpec((rb, D), lambda i: (i, 0))
    vec = pl.BlockSpec((1, D), lambda i: (0, 0))
    return pl.pallas_call(
        body, grid=(T // rb,), in_specs=[row, row, vec] + [row] * nt,
        out_specs=[row, row, vec, vec],
        out_shape=[jax.ShapeDtypeStruct((T, D), F32), jax.ShapeDtypeStruct((T, D), BF16),
                   jax.ShapeDtypeStruct((1, D), F32), jax.ShapeDtypeStruct((1, D), F32)],
        compiler_params=_params(("arbitrary",)), name=name,
    )(x, h, g.reshape(1, D), *[a for _, a in terms])


def _lincomb(terms, out_dtype, *, name):
    R, C = terms[0][1].shape
    rb = _pick(R, 512)
    coefs = [c for c, _ in terms]
    nt = len(terms)

    def body(*refs):
        acc = None
        for c, r in zip(coefs, refs[:nt]):
            v = r[...].astype(F32)
            v = v if c == 1.0 else c * v
            acc = v if acc is None else acc + v
        refs[nt][...] = acc.astype(out_dtype)

    row = pl.BlockSpec((rb, C), lambda i: (i, 0))
    return pl.pallas_call(
        body, grid=(R // rb,), in_specs=[row] * nt, out_specs=row,
        out_shape=jax.ShapeDtypeStruct((R, C), out_dtype),
        compiler_params=_params(("parallel",)), name=name,
    )(*[a for _, a in terms])


def _loss_grad(y, tgt, *, name):
    T, D = y.shape
    rb = _pick(T, 512)

    def body(y_ref, t_ref, dy_ref, l_ref):
        err = y_ref[...] - t_ref[...]
        dy_ref[...] = err * (1.0 / D)
        part = jnp.sum(jnp.sum(err * err, axis=1, keepdims=True), axis=0, keepdims=True) * (0.5 / D)
        part = jnp.broadcast_to(part, (1, LANES))

        @pl.when(pl.program_id(0) == 0)
        def _():
            l_ref[...] = part

        @pl.when(pl.program_id(0) > 0)
        def _():
            l_ref[...] += part

    row = pl.BlockSpec((rb, D), lambda i: (i, 0))
    return pl.pallas_call(
        body, grid=(T // rb,), in_specs=[row, row],
        out_specs=[row, pl.BlockSpec((1, LANES), lambda i: (0, 0))],
        out_shape=[jax.ShapeDtypeStruct((T, D), F32), jax.ShapeDtypeStruct((1, LANES), F32)],
        compiler_params=_params(("arbitrary",)), name=name,
    )(y, tgt)


def _sgu_fwd(zp, ln_g, ln_b, ws, bst, *, name):
    T, E2 = zp.shape
    E = E2 // 2
    G = ws.shape[0]
    cg = E // G
    rb = 2 * CHUNK

    def body(z_ref, g_ref, b_ref, ws_ref, bs_ref, y_ref):
        u = _gelu(z_ref[:, :E])
        v = _gelu(z_ref[:, E:])
        mu = jnp.mean(v, axis=1, keepdims=True)
        cen = v - mu
        var = jnp.mean(cen * cen, axis=1, keepdims=True)
        vn = (cen * lax.rsqrt(var + LN_EPS) * g_ref[...] + b_ref[...]).astype(BF16)
        for ci in range(rb // CHUNK):
            rows = slice(ci * CHUNK, (ci + 1) * CHUNK)
            for gi in range(G):
                cols = slice(gi * cg, (gi + 1) * cg)
                sv = jnp.dot(ws_ref[gi], vn[rows, cols], preferred_element_type=F32)
                sv = sv + bs_ref[:, gi:gi + 1]
                y_ref[rows, cols] = (u[rows, cols] * sv).astype(BF16)

    return pl.pallas_call(
        body, grid=(T // rb,),
        in_specs=[pl.BlockSpec((rb, E2), lambda i: (i, 0)),
                  pl.BlockSpec((1, E), lambda i: (0, 0)), pl.BlockSpec((1, E), lambda i: (0, 0)),
                  pl.BlockSpec((G, CHUNK, CHUNK), lambda i: (0, 0, 0)),
                  pl.BlockSpec((CHUNK, G), lambda i: (0, 0))],
        out_specs=pl.BlockSpec((rb, E), lambda i: (i, 0)),
        out_shape=jax.ShapeDtypeStruct((T, E), BF16),
        compiler_params=_params(("parallel",)), name=name,
    )(zp, ln_g.reshape(1, E), ln_b.reshape(1, E), ws, bst)


def _sgu_bwd(zp, dy, ln_g, ln_b, ws, wst, bst, *, name):
    T, E2 = zp.shape
    E = E2 // 2
    G = ws.shape[0]
    cg = E // G
    rb = CHUNK
    nsteps = T // rb

    def body(z_ref, dy_ref, g_ref, b_ref, ws_ref, wst_ref, bs_ref,
             dz_ref, dg_ref, db_ref, dws_ref, dbs_ref, dsv_acc):
        step = pl.program_id(0)

        @pl.when(step == 0)
        def _():
            dg_ref[...] = jnp.zeros_like(dg_ref)
            db_ref[...] = jnp.zeros_like(db_ref)
            dws_ref[...] = jnp.zeros_like(dws_ref)
            dsv_acc[...] = jnp.zeros_like(dsv_acc)

        zu = z_ref[:, :E]
        zv = z_ref[:, E:]
        u = _gelu(zu)
        v = _gelu(zv)
        mu = jnp.mean(v, axis=1, keepdims=True)
        cen = v - mu
        var = jnp.mean(cen * cen, axis=1, keepdims=True)
        rstd = lax.rsqrt(var + LN_EPS)
        xhat = cen * rstd
        vn = (xhat * g_ref[...] + b_ref[...]).astype(BF16)
        dyv = dy_ref[...]
        dsv = dyv * u
        dsv_acc[...] += dsv
        dsvb = dsv.astype(BF16)
        tril = (lax.broadcasted_iota(jnp.int32, (CHUNK, CHUNK), 0)
                >= lax.broadcasted_iota(jnp.int32, (CHUNK, CHUNK), 1))
        du_parts = []
        dvn_parts = []
        for gi in range(G):
            cols = slice(gi * cg, (gi + 1) * cg)
            sv = jnp.dot(ws_ref[gi], vn[:, cols], preferred_element_type=F32) + bs_ref[:, gi:gi + 1]
            du_parts.append(dyv[:, cols] * sv)
            dvn_parts.append(jnp.dot(wst_ref[gi], dsvb[:, cols], preferred_element_type=F32))
            dw = lax.dot_general(dsvb[:, cols], vn[:, cols], (((1,), (1,)), ((), ())),
                                 preferred_element_type=F32)
            dws_ref[gi] += jnp.where(tril, dw, 0.0)
        du = jnp.concatenate(du_parts, axis=1)
        dvn = jnp.concatenate(dvn_parts, axis=1)
        dg_ref[...] += jnp.sum(dvn * xhat, axis=0, keepdims=True)
        db_ref[...] += jnp.sum(dvn, axis=0, keepdims=True)
        dxh = dvn * g_ref[...]
        m1 = jnp.mean(dxh, axis=1, keepdims=True)
        m2 = jnp.mean(dxh * xhat, axis=1, keepdims=True)
        dv = rstd * (dxh - m1 - xhat * m2)
        dz_ref[:, :E] = (du * _gelu_grad(zu)).astype(BF16)
        dz_ref[:, E:] = (dv * _gelu_grad(zv)).astype(BF16)

        @pl.when(step == nsteps - 1)
        def _():
            lane = lax.broadcasted_iota(jnp.int32, (CHUNK, LANES), 1)
            out = jnp.zeros((CHUNK, LANES), F32)
            for gi in range(G):
                s = jnp.sum(dsv_acc[:, gi * cg:(gi + 1) * cg], axis=1, keepdims=True)
                out = jnp.where(lane == gi, s, out)
            dbs_ref[...] = out

    vecE = pl.BlockSpec((1, E), lambda i: (0, 0))
    wspec = pl.BlockSpec((G, CHUNK, CHUNK), lambda i: (0, 0, 0))
    return pl.pallas_call(
        body, grid=(nsteps,),
        in_specs=[pl.BlockSpec((rb, E2), lambda i: (i, 0)), pl.BlockSpec((rb, E), lambda i: (i, 0)),
                  vecE, vecE, wspec, wspec, pl.BlockSpec((CHUNK, G), lambda i: (0, 0))],
        out_specs=[pl.BlockSpec((rb, E2), lambda i: (i, 0)), vecE, vecE, wspec,
                   pl.BlockSpec((CHUNK, LANES), lambda i: (0, 0))],
        out_shape=[jax.ShapeDtypeStruct((T, E2), BF16), jax.ShapeDtypeStruct((1, E), F32),
                   jax.ShapeDtypeStruct((1, E), F32), jax.ShapeDtypeStruct((G, CHUNK, CHUNK), F32),
                   jax.ShapeDtypeStruct((CHUNK, LANES), F32)],
        scratch_shapes=[pltpu.VMEM((CHUNK, E), F32)],
        compiler_params=_params(("arbitrary",)), name=name,
    )(zp, dy, ln_g.reshape(1, E), ln_b.reshape(1, E), ws, wst, bst)


def _shift_down(x, k, row):
    return jnp.where(row >= k, pltpu.roll(x, k, 0), 0.0)


def _shift_up(x, k, row, T):
    return jnp.where(row < T - k, pltpu.roll(x, T - k, 0), 0.0)


def _conv3(x, w_ref, b_ref, row):
    return (w_ref[0:1, :] * _shift_down(x, 2, row) + w_ref[1:2, :] * _shift_down(x, 1, row)
            + w_ref[2:3, :] * x + b_ref[...])


def _convgate_fwd(hh, cw, cb, *, name):
    T, F2 = hh.shape
    F = F2 // 2
    ns = F // LANES

    def body(a_ref, g_ref, wa_ref, wg_ref, ba_ref, bg_ref, o_ref):
        row = lax.broadcasted_iota(jnp.int32, (T, LANES), 0)
        ca = _conv3(a_ref[...], wa_ref, ba_ref, row)
        cgv = _conv3(g_ref[...], wg_ref, bg_ref, row)
        o_ref[...] = (_gelu(ca) * cgv).astype(BF16)

    sa = lambda r: pl.BlockSpec((r, LANES), lambda j: (0, j))
    sg = lambda r: pl.BlockSpec((r, LANES), lambda j: (0, j + ns))
    return pl.pallas_call(
        body, grid=(ns,), in_specs=[sa(T), sg(T), sa(3), sg(3), sa(1), sg(1)],
        out_specs=sa(T), out_shape=jax.ShapeDtypeStruct((T, F), BF16),
        compiler_params=_params(("parallel",)), name=name,
    )(hh, hh, cw, cw, cb, cb)


def _convgate_bwd(hh, dact, cw, cb, *, name):
    T, F2 = hh.shape
    F = F2 // 2
    ns = F // LANES

    def body(a_ref, g_ref, d_ref, wa_ref, wg_ref, ba_ref, bg_ref,
             dh_ref, dwa_ref, dwg_ref, dba_ref, dbg_ref):
        da_ref, dg_ref = dh_ref.at[0], dh_ref.at[1]
        row = lax.broadcasted_iota(jnp.int32, (T, LANES), 0)
        d = d_ref[...].astype(F32)
        ca = _conv3(a_ref[...], wa_ref, ba_ref, row)
        cgv = _conv3(g_ref[...], wg_ref, bg_ref, row)
        cdf = 0.5 * (1.0 + lax.erf(ca * INV_SQRT2))
        dca = d * cgv * (cdf + ca * jnp.exp(-0.5 * ca * ca) * INV_SQRT_2PI)
        dcg = d * (ca * cdf)
        for x_ref, w_ref, dc, dx_ref, dw_ref, db_ref in (
                (a_ref, wa_ref, dca, da_ref, dwa_ref, dba_ref),
                (g_ref, wg_ref, dcg, dg_ref, dwg_ref, dbg_ref)):
            x = x_ref[...]
            dx = (w_ref[2:3, :] * dc + w_ref[1:2, :] * _shift_up(dc, 1, row, T)
                  + w_ref[0:1, :] * _shift_up(dc, 2, row, T))
            dx_ref[...] = dx.astype(BF16)
            dw_ref[0:1, :] = jnp.sum(dc * _shift_down(x, 2, row), axis=0, keepdims=True)
            dw_ref[1:2, :] = jnp.sum(dc * _shift_down(x, 1, row), axis=0, keepdims=True)
            dw_ref[2:3, :] = jnp.sum(dc * x, axis=0, keepdims=True)
            db_ref[...] = jnp.sum(dc, axis=0, keepdims=True)

    sa = lambda r: pl.BlockSpec((r, LANES), lambda j: (0, j))
    sg = lambda r: pl.BlockSpec((r, LANES), lambda j: (0, j + ns))
    return pl.pallas_call(
        body, grid=(ns,), in_specs=[sa(T), sg(T), sa(T), sa(3), sg(3), sa(1), sg(1)],
        out_specs=[pl.BlockSpec((2, T, LANES), lambda j: (0, 0, j)), sa(3), sa(3), sa(1), sa(1)],
        out_shape=[jax.ShapeDtypeStruct((2, T, F), BF16),
                   jax.ShapeDtypeStruct((3, F), F32), jax.ShapeDtypeStruct((3, F), F32),
                   jax.ShapeDtypeStruct((1, F), F32), jax.ShapeDtypeStruct((1, F), F32)],
        compiler_params=_params(("parallel",)), name=name,
    )(hh, hh, dact, cw, cw, cb, cb)


def _bucket_maps():
    iq = np.arange(BLK)[:, None]
    ik = np.arange(2 * BLK)[None, :]
    delta = iq + BLK - ik
    maps = []
    for win, dil in DILATED_GROUPS:
        n = np.clip(delta, 0, None) * dil
        max_exact = REL_BUCKETS // 2
        nf = np.maximum(n, 1).astype(np.float32)
        large = max_exact + (np.log(nf / np.float32(max_exact)) / np.float32(math.log(REL_MAX_DIST / max_exact))
                             * np.float32(REL_BUCKETS - max_exact)).astype(np.int32)
        large = np.minimum(large, REL_BUCKETS - 1)
        bucket = np.where(n < max_exact, n, large)
        valid = (delta >= 0) & (delta <= win // dil)
        maps.append(np.where(valid, bucket, -1).astype(np.int32))
    return np.stack(maps)


def _band_bias(rel_table, bmap, H, *, name):
    def body(t_ref, m_ref, o_ref):
        g = pl.program_id(0)
        bm = m_ref[0]
        for h in range(H):
            acc = jnp.full((BLK, 2 * BLK), NEG, F32)
            for b in range(REL_BUCKETS):
                acc = jnp.where(bm == b, t_ref[b, g * H + h], acc)
            o_ref[0, h] = acc

    return pl.pallas_call(
        body, grid=(N_GROUPS,),
        in_specs=[pl.BlockSpec(memory_space=pltpu.SMEM),
                  pl.BlockSpec((1, BLK, 2 * BLK), lambda g: (g, 0, 0))],
        out_specs=pl.BlockSpec((1, H, BLK, 2 * BLK), lambda g: (g, 0, 0, 0)),
        out_shape=jax.ShapeDtypeStruct((N_GROUPS, H, BLK, 2 * BLK), F32),
        compiler_params=_params(("parallel",)), name=name,
    )(rel_table, bmap)


def _band_bias_bwd(dbias, bmap, H, *, name):
    def body(d_ref, m_ref, o_ref):
        bm = m_ref[0]
        rowi = lax.broadcasted_iota(jnp.int32, (REL_BUCKETS, LANES), 0)
        lane = lax.broadcasted_iota(jnp.int32, (REL_BUCKETS, LANES), 1)
        out = jnp.zeros((REL_BUCKETS, LANES), F32)
        for h in range(H):
            dv = d_ref[0, h]
            for b in range(REL_BUCKETS):
                s = jnp.sum(jnp.sum(jnp.where(bm == b, dv, 0.0), axis=1, keepdims=True),
                            axis=0, keepdims=True)
                out = jnp.where((rowi == b) & (lane == h), s, out)
        o_ref[0] = out

    return pl.pallas_call(
        body, grid=(N_GROUPS,),
        in_specs=[pl.BlockSpec((1, H, BLK, 2 * BLK), lambda g: (g, 0, 0, 0)),
                  pl.BlockSpec((1, BLK, 2 * BLK), lambda g: (g, 0, 0))],
        out_specs=pl.BlockSpec((1, REL_BUCKETS, LANES), lambda g: (g, 0, 0)),
        out_shape=jax.ShapeDtypeStruct((N_GROUPS, REL_BUCKETS, LANES), F32),
        compiler_params=_params(("parallel",)), name=name,
    )(dbias, bmap)


def _head_masks():
    lane = lax.broadcasted_iota(jnp.int32, (BLK, LANES), 1)
    return (lane < HEAD_DIM, lane >= HEAD_DIM)


def _attn_fwd(q, kv, bias, gi, *, name):
    T = q.shape[0]
    HD = kv.shape[1] // 2
    d = DILATED_GROUPS[gi][1]
    S = T // d
    NB = S // BLK
    H = HD // HEAD_DIM
    qv, qcol = (q, gi) if d == 1 else (q[:, gi * HD:(gi + 1) * HD].reshape(S, d * HD), 0)
    kvv = kv.reshape(S, d * 2 * HD)

    def body(q_ref, kp_ref, kc_ref, vp_ref, vc_ref, b_ref, o_ref, l_ref):
        n = pl.program_id(1)
        col = lax.broadcasted_iota(jnp.int32, (BLK, 2 * BLK), 1)
        first = (n == 0) & (col < BLK)
        hm = _head_masks()
        for p in range(HD // LANES):
            sl = slice(p * LANES, (p + 1) * LANES)
            qp = q_ref[:, sl]
            kc = jnp.concatenate([kp_ref[:, sl], kc_ref[:, sl]], axis=0)
            vc = jnp.concatenate([vp_ref[:, sl], vc_ref[:, sl]], axis=0)
            outs = []
            lses = []
            for hh in range(2):
                qm = jnp.where(hm[hh], qp, jnp.zeros_like(qp))
                s = lax.dot_general(qm, kc, (((1,), (1,)), ((), ())), preferred_element_type=F32)
                s = jnp.where(first, NEG, s + b_ref[2 * p + hh])
                m = jnp.max(s, axis=1, keepdims=True)
                e = jnp.exp(s - m)
                den = jnp.sum(e, axis=1, keepdims=True)
                outs.append(jnp.dot((e / den).astype(BF16), vc, preferred_element_type=F32))
                lses.append(m + jnp.log(den))
            o_ref[:, sl] = jnp.where(hm[0], outs[0], outs[1])
            l_ref[:, sl] = jnp.where(hm[0], lses[0], lses[1])

    blk = lambda f: pl.BlockSpec((BLK, HD), f)
    prev = lambda n: jnp.maximum(n - 1, 0)
    return pl.pallas_call(
        body, grid=(d, NB),
        in_specs=[blk(lambda r, n: (n, r + qcol)),
                  blk(lambda r, n: (prev(n), r * 2)), blk(lambda r, n: (n, r * 2)),
                  blk(lambda r, n: (prev(n), r * 2 + 1)), blk(lambda r, n: (n, r * 2 + 1)),
                  pl.BlockSpec((H, BLK, 2 * BLK), lambda r, n: (0, 0, 0))],
        out_specs=[blk(lambda r, n: (n, r)), blk(lambda r, n: (n, r))],
        out_shape=[jax.ShapeDtypeStruct((S, d * HD), F32), jax.ShapeDtypeStruct((S, d * HD), F32)],
        compiler_params=_params(("parallel", "parallel")), name=name,
    )(qv, kvv, kvv, kvv, kvv, bias)


def _attn_combine(os, ls, *, name):
    T, HD = os[0].shape
    rb = _pick(T, 512)

    def body(o0, o1, o2, l0, l1, l2, o_ref, ob_ref, l_ref):
        la, lb, lc = l0[...], l1[...], l2[...]
        m = jnp.maximum(jnp.maximum(la, lb), lc)
        L = m + jnp.log(jnp.exp(la - m) + jnp.exp(lb - m) + jnp.exp(lc - m))
        o = jnp.exp(la - L) * o0[...] + jnp.exp(lb - L) * o1[...] + jnp.exp(lc - L) * o2[...]
        o_ref[...] = o
        ob_ref[...] = o.astype(BF16)
        l_ref[...] = L

    row = pl.BlockSpec((rb, HD), lambda i: (i, 0))
    return pl.pallas_call(
        body, grid=(T // rb,), in_specs=[row] * 6, out_specs=[row] * 3,
        out_shape=[jax.ShapeDtypeStruct((T, HD), F32), jax.ShapeDtypeStruct((T, HD), BF16),
                   jax.ShapeDtypeStruct((T, HD), F32)],
        compiler_params=_params(("parallel",)), name=name,
    )(*[a.reshape(T, HD) for a in os], *[a.reshape(T, HD) for a in ls])


def _attn_bwd(q, kv, bias, do, o, L, gi, *, name):
    T = q.shape[0]
    HD = kv.shape[1] // 2
    d = DILATED_GROUPS[gi][1]
    S = T // d
    NB = S // BLK
    H = HD // HEAD_DIM
    qv, qcol = (q, gi) if d == 1 else (q[:, gi * HD:(gi + 1) * HD].reshape(S, d * HD), 0)
    kvv = kv.reshape(S, d * 2 * HD)
    dov, ov, Lv = (a.reshape(S, d * HD) for a in (do, o, L))

    def body(q_ref, kp_ref, kc_ref, vp_ref, vc_ref, b_ref, do_ref, o_ref, L_ref,
             dq_ref, dk_ref, dv_ref, db_ref, ck_ref, cv_ref):
        r = pl.program_id(0)
        n = pl.program_id(1)

        @pl.when((r == 0) & (n == 0))
        def _():
            db_ref[...] = jnp.zeros_like(db_ref)

        @pl.when(n == 0)
        def _():
            ck_ref[...] = jnp.zeros_like(ck_ref)
            cv_ref[...] = jnp.zeros_like(cv_ref)

        @pl.when(n < NB)
        def _():
            col = lax.broadcasted_iota(jnp.int32, (BLK, 2 * BLK), 1)
            first = (n == 0) & (col < BLK)
            hm = _head_masks()
            for p in range(HD // LANES):
                sl = slice(p * LANES, (p + 1) * LANES)
                qp = q_ref[:, sl]
                kc = jnp.concatenate([kp_ref[:, sl], kc_ref[:, sl]], axis=0)
                vc = jnp.concatenate([vp_ref[:, sl], vc_ref[:, sl]], axis=0)
                dop = do_ref[:, sl]
                dob = dop.astype(BF16)
                prod = dop * o_ref[:, sl]
                Lp = L_ref[:, sl]
                dq_parts = []
                dkc = None
                dvc = None
                for hh in range(2):
                    qm = jnp.where(hm[hh], qp, jnp.zeros_like(qp))
                    dom = jnp.where(hm[hh], dob, jnp.zeros_like(dob))
                    s = lax.dot_general(qm, kc, (((1,), (1,)), ((), ())), preferred_element_type=F32)
                    s = jnp.where(first, NEG, s + b_ref[2 * p + hh])
                    lse = Lp[:, hh * HEAD_DIM:hh * HEAD_DIM + 1]
                    pr = jnp.exp(s - lse)
                    dp = lax.dot_general(dom, vc, (((1,), (1,)), ((), ())), preferred_element_type=F32)
                    delta = jnp.sum(jnp.where(hm[hh], prod, 0.0), axis=1, keepdims=True)
                    ds = pr * (dp - delta)
                    db_ref[2 * p + hh] += ds
                    dsb = ds.astype(BF16)
                    dq_parts.append(jnp.dot(dsb, kc, preferred_element_type=F32))
                    dkh = lax.dot_general(dsb, qm, (((0,), (0,)), ((), ())), preferred_element_type=F32)
                    dvh = lax.dot_general(pr.astype(BF16), dom, (((0,), (0,)), ((), ())),
                                          preferred_element_type=F32)
                    dkc = dkh if dkc is None else dkc + dkh
                    dvc = dvh if dvc is None else dvc + dvh
                dq = jnp.where(hm[0], dq_parts[0], dq_parts[1])
                dq_ref[:, sl] = (dq * (HEAD_DIM ** -0.5)).astype(BF16)
                dk_ref[:, sl] = ck_ref[:, sl] + dkc[:BLK]
                dv_ref[:, sl] = cv_ref[:, sl] + dvc[:BLK]
                ck_ref[:, sl] = dkc[BLK:]
                cv_ref[:, sl] = dvc[BLK:]

        @pl.when(n == NB)
        def _():
            dk_ref[...] = ck_ref[...]
            dv_ref[...] = cv_ref[...]

    blk = lambda f: pl.BlockSpec((BLK, HD), f)
    cur = lambda n: jnp.minimum(n, NB - 1)
    prev = lambda n: jnp.maximum(jnp.minimum(n, NB - 1) - 1, 0)
    lag = lambda n: jnp.maximum(n - 1, 0)
    return pl.pallas_call(
        body, grid=(d, NB + 1),
        in_specs=[blk(lambda r, n: (cur(n), r + qcol)),
                  blk(lambda r, n: (prev(n), r * 2)), blk(lambda r, n: (cur(n), r * 2)),
                  blk(lambda r, n: (prev(n), r * 2 + 1)), blk(lambda r, n: (cur(n), r * 2 + 1)),
                  pl.BlockSpec((H, BLK, 2 * BLK), lambda r, n: (0, 0, 0)),
                  blk(lambda r, n: (cur(n), r)), blk(lambda r, n: (cur(n), r)),
                  blk(lambda r, n: (cur(n), r))],
        out_specs=[blk(lambda r, n: (cur(n), r)), blk(lambda r, n: (lag(n), r)),
                   blk(lambda r, n: (lag(n), r)),
                   pl.BlockSpec((H, BLK, 2 * BLK), lambda r, n: (0, 0, 0))],
        out_shape=[jax.ShapeDtypeStruct((S, d * HD), BF16), jax.ShapeDtypeStruct((S, d * HD), F32),
                   jax.ShapeDtypeStruct((S, d * HD), F32),
                   jax.ShapeDtypeStruct((H, BLK, 2 * BLK), F32)],
        scratch_shapes=[pltpu.VMEM((BLK, HD), F32), pltpu.VMEM((BLK, HD), F32)],
        compiler_params=_params(("arbitrary", "arbitrary")), name=name,
    )(qv, kvv, kvv, kvv, kvv, bias, dov, ov, Lv)


class _Weights(dict):
    def __init__(self, base, fetch=None, emit=None):
        super().__init__(base)
        self._fetch, self._emit = fetch, emit

    def fetch(self, group, after):
        if self._fetch is not None:
            for (key, layer), mat in self._fetch(group, after).items():
                self[key][layer] = mat

    def emit(self, group, mats):
        return None if self._emit is None else self._emit(group, mats)


def _local_step(x, tgt, W):
    T, D = x.shape
    H = W["rel_table"].shape[1] // N_GROUPS
    HD = H * HEAD_DIM
    G = W["a_w_s"].shape[1]
    assert T % (DILATED_GROUPS[-1][1] * BLK) == 0

    tril = jnp.tril(jnp.ones((CHUNK, CHUNK), F32))
    bmap = jnp.asarray(_bucket_maps())
    bias = _band_bias(W["rel_table"], bmap, H, name="band_bias")

    saved = []
    xc, xcb = x, x.astype(BF16)
    kvb = None
    for i in range(DEPTH):
        s = {"x": xc, "xb": xcb}
        W.fetch(4 * i, xc)
        if i < N_A:
            ws_m = W["a_w_s"][i] * tril
            s["ws"] = ws_m.astype(BF16)
            s["wst"] = jnp.swapaxes(ws_m, 1, 2).astype(BF16)
            s["bst"] = W["a_b_s"][i].T
            s["zp"] = _mm(xcb, W["a_w_in"][i], name=f"a_in_{i}")
            s["y"] = _sgu_fwd(s["zp"], W["a_ln_g"][i], W["a_ln_b"][i], s["ws"], s["bst"], name=f"sgu_fwd_{i}")
            W.fetch(4 * i + 1, s["zp"])
            s["h"] = _mm(s["y"], W["a_w_out"][i], name=f"a_out_{i}")
        else:
            j = i - N_A
            if kvb is None:
                kvb = _mm(xcb, W["kv_w"][0], out_dtype=BF16, name="kv_proj")
            s["q"] = _mm(xcb, W["b_w_q_t"][j], tb=True, out_dtype=BF16, scale=HEAD_DIM ** -0.5,
                         name=f"q_proj_{j}")
            os, ls = [], []
            for gi in range(N_GROUPS):
                o_g, l_g = _attn_fwd(s["q"], kvb, bias[gi], gi, name=f"attn_fwd_{j}_{gi}")
                os.append(o_g)
                ls.append(l_g)
            s["o"], s["ob"], s["L"] = _attn_combine(os, ls, name=f"attn_mix_{j}")
            W.fetch(4 * i + 1, s["q"])
            s["h"] = _mm(s["ob"], W["b_w_o"][j], name=f"o_proj_{j}")
        s["x1"], s["x1b"] = _add_ln_fwd(xc, s["h"], W["ln_g"][i, 0], W["ln_b"][i, 0], name=f"ln1_fwd_{i}")
        W.fetch(4 * i + 2, s["x1"])
        s["hh"] = _mm(s["x1b"], W["ffn_w_up_t"][i], tb=True, name=f"ffn_up_{i}")
        s["cw"] = W["ffn_conv_w"][i]
        s["cb"] = W["ffn_conv_b"][i].reshape(1, -1)
        s["act"] = _convgate_fwd(s["hh"], s["cw"], s["cb"], name=f"convgate_fwd_{i}")
        W.fetch(4 * i + 3, s["hh"])
        s["f"] = _mm(s["act"], W["ffn_w_down"][i], name=f"ffn_down_{i}")
        xc, xcb = _add_ln_fwd(s["x1"], s["f"], W["ln_g"][i, 1], W["ln_b"][i, 1], name=f"ln2_fwd_{i}")
        saved.append(s)

    dy, lossv = _loss_grad(xc, tgt, name="loss_grad")
    loss = lossv[0, 0]

    gl = {k: [None] * DEPTH for k in ("ffn_w_up_t", "ffn_conv_w", "ffn_conv_b", "ffn_w_down", "ln_g", "ln_b")}
    ga = {k: [None] * N_A for k in ("a_w_in", "a_ln_g", "a_ln_b", "a_w_s", "a_b_s", "a_w_out")}
    gb = {k: [None] * (DEPTH - N_A) for k in ("b_w_q_t", "b_w_o")}
    mats = ("a_w_in", "a_w_out", "b_w_q_t", "b_w_o", "ffn_w_up_t", "ffn_w_down")
    dks, dvs, dbias = [], [], []
    grads = {}
    terms = [(1.0, dy)]
    tok = None
    behind = lambda g: g if tok is None else g + tok
    for i in reversed(range(DEPTH)):
        s = saved[i]
        dp2, dp2b, dg2, db2 = _add_ln_bwd(s["x1"], s["f"], behind(W["ln_g"][i, 1]), terms, name=f"ln2_bwd_{i}")
        dact = _mm(dp2b, W["ffn_w_down"][i], tb=True, name=f"ffn_down_dx_{i}")
        gl["ffn_w_down"][i] = _mm(s["act"], dp2b, ta=True, out_dtype=BF16, name=f"ffn_down_dw_{i}")
        dhh, dwa, dwg, dba, dbg = _convgate_bwd(s["hh"], dact, s["cw"], s["cb"], name=f"convgate_bwd_{i}")
        gl["ffn_conv_w"][i] = jnp.concatenate([dwa, dwg], axis=1)
        gl["ffn_conv_b"][i] = jnp.concatenate([dba, dbg], axis=1)[0]
        dx1 = _mm(dhh, W["ffn_w_up_t"][i], name=f"ffn_up_dx_{i}")
        gl["ffn_w_up_t"][i] = _mm(dhh, s["x1b"], ta=True, out_dtype=BF16, name=f"ffn_up_dw_{i}")
        tok = W.emit(3 * i + 2, {("ffn_w_up_t", i): gl["ffn_w_up_t"][i], ("ffn_w_down", i): gl["ffn_w_down"][i]})
        dp1, dp1b, dg1, db1 = _add_ln_bwd(s["x"], s["h"], behind(W["ln_g"][i, 0]), [(ALPHA, dp2), (1.0, dx1)],
                                          name=f"ln1_bwd_{i}")
        gl["ln_g"][i] = jnp.concatenate([dg1, dg2], axis=0)
        gl["ln_b"][i] = jnp.concatenate([db1, db2], axis=0)
        terms = [(ALPHA, dp1)]
        if i < N_A:
            dyy = _mm(dp1b, W["a_w_out"][i], tb=True, name=f"a_out_dx_{i}")
            ga["a_w_out"][i] = _mm(s["y"], dp1b, ta=True, out_dtype=BF16, name=f"a_out_dw_{i}")
            tok = W.emit(3 * i + 1, {("a_w_out", i): ga["a_w_out"][i]})
            dzp, dlg, dlb, dws, dbs = _sgu_bwd(s["zp"], dyy, behind(W["a_ln_g"][i]), W["a_ln_b"][i], s["ws"],
                                               s["wst"], s["bst"], name=f"sgu_bwd_{i}")
            ga["a_ln_g"][i], ga["a_ln_b"][i], ga["a_w_s"][i] = dlg[0], dlb[0], dws
            ga["a_b_s"][i] = dbs[:, :G].T
            terms.append((1.0, _mm(dzp, W["a_w_in"][i], tb=True, name=f"a_in_dx_{i}")))
            ga["a_w_in"][i] = _mm(s["xb"], dzp, ta=True, out_dtype=BF16, name=f"a_in_dw_{i}")
            tok = W.emit(3 * i, {("a_w_in", i): ga["a_w_in"][i]})
        else:
            j = i - N_A
            do = _mm(dp1b, W["b_w_o"][j], tb=True, name=f"o_proj_dx_{j}")
            gb["b_w_o"][j] = _mm(s["ob"], dp1b, ta=True, out_dtype=BF16, name=f"o_proj_dw_{j}")
            tok = W.emit(3 * i + 1, {("b_w_o", j): gb["b_w_o"][j]})
            bias_b = behind(bias)
            dqs, dbl = [], []
            for gi in range(N_GROUPS):
                dq_g, dk_g, dv_g, db_g = _attn_bwd(s["q"], kvb, bias_b[gi], do, s["o"], s["L"], gi,
                                                   name=f"attn_bwd_{j}_{gi}")
                dqs.append(dq_g.reshape(T, HD))
                dks.append((1.0, dk_g.reshape(T, HD)))
                dvs.append((1.0, dv_g.reshape(T, HD)))
                dbl.append(db_g)
            dbias.append(jnp.stack(dbl))
            dq = jnp.concatenate(dqs, axis=1)
            terms.append((1.0, _mm(dq, W["b_w_q_t"][j], name=f"q_proj_dx_{j}")))
            gb["b_w_q_t"][j] = _mm(dq, s["xb"], ta=True, out_dtype=BF16, name=f"q_proj_dw_{j}")
            out_b = {("b_w_q_t", j): gb["b_w_q_t"][j]}
            if i == N_A:
                dkv = jnp.concatenate([_lincomb(dks, BF16, name="dk_sum"), _lincomb(dvs, BF16, name="dv_sum")],
                                      axis=1)
                terms.append((1.0, _mm(dkv, W["kv_w"][0], tb=True, name="kv_proj_dx")))
                grads["kv_w"] = [_mm(s["xb"], dkv, ta=True, out_dtype=BF16, name="kv_proj_dw")]
                out_b[("kv_w", 0)] = grads["kv_w"][0]
                dbt = _lincomb([(1.0, a.reshape(-1, 2 * BLK)) for a in dbias], F32, name="dbias_sum")
                dtab = _band_bias_bwd(dbt.reshape(N_GROUPS, H, BLK, 2 * BLK), bmap, H, name="band_bias_bwd")
                grads["rel_table"] = jnp.transpose(dtab[:, :, :H], (1, 0, 2)).reshape(REL_BUCKETS, N_GROUPS * H)
            tok = W.emit(3 * i, out_b)
    grad_x = _lincomb(terms, F32, name="grad_x")
    for dct in (gl, ga, gb):
        for k, v in dct.items():
            grads[k] = v if k in mats else jnp.stack(v)
    return loss, grad_x, grads


def _my_index():
    return 4 * lax.axis_index("x") + 2 * lax.axis_index("y") + lax.axis_index("c")


HBM_SPEC = pl.BlockSpec(memory_space=pltpu.HBM)


def _block(ref, k, n, axis):
    off = pl.multiple_of(k * n, n)
    return ref.at[pl.ds(off, n), :] if axis == 0 else ref.at[:, pl.ds(off, n)]


def _gather_mats(local, axis, *, name):
    L, a, b = local.shape
    n = a if axis == 0 else b
    full = (a * N_DEV, b) if axis == 0 else (a, b * N_DEV)

    def body(x_ref, *rest):
        outs = rest[:L]
        send_sems, recv_sems, local_sems = rest[L:]
        x, y, c = lax.axis_index("x"), lax.axis_index("y"), lax.axis_index("c")
        me, sibling = (x, y, c), (x, y, 1 - c)
        chips = [(1 - x, y), (x, 1 - y), (1 - x, 1 - y)]

        def slot(l, px, py, pc):
            return _block(outs[l], 4 * px + 2 * py + pc, n, axis)

        def copy(l, k, blk, to, src=None):
            return pltpu.make_async_remote_copy(
                src_ref=slot(l, *blk) if src is None else src, dst_ref=slot(l, *blk),
                send_sem=send_sems.at[7 * l + k], recv_sem=recv_sems.at[7 * l + k],
                device_id=to, device_id_type=MESH)

        mine, first, passed = [], [], []
        for l in range(L):
            mine.append(pltpu.make_async_copy(x_ref.at[l], slot(l, *me), local_sems.at[l]))
            mine[-1].start()
            first.append(copy(l, 0, me, sibling, src=x_ref.at[l]))
            first += [copy(l, 1 + j, me, (*chip, c), src=x_ref.at[l]) for j, chip in enumerate(chips)]
        for cp in first:
            cp.start()
        for l in range(L):
            for j, chip in enumerate(chips):
                copy(l, 1 + j, (*chip, c), me).wait_recv()
                passed.append(copy(l, 4 + j, (*chip, c), sibling))
                passed[-1].start()
        for l in range(L):
            copy(l, 0, sibling, me).wait_recv()
            for j, chip in enumerate(chips):
                copy(l, 4 + j, (*chip, 1 - c), me).wait_recv()
        for cp in first + passed:
            cp.wait_send()
        for cp in mine:
            cp.wait()

    return pl.pallas_call(
        body, out_shape=[jax.ShapeDtypeStruct(full, local.dtype)] * L,
        in_specs=[HBM_SPEC], out_specs=[HBM_SPEC] * L,
        scratch_shapes=[pltpu.SemaphoreType.DMA((7 * L,)), pltpu.SemaphoreType.DMA((7 * L,)),
                        pltpu.SemaphoreType.DMA((L,))],
        name=name,
    )(local)


SEM_SPEC = pl.BlockSpec(memory_space=pltpu.SEMAPHORE)
FLOWING = pltpu.SideEffectType.DATAFLOW_SIDE_EFFECTING


def _peers(x, y, c):
    return [(1 - x if k & 4 else x, 1 - y if k & 2 else y, 1 - c if k & 1 else c) for k in range(1, N_DEV)]


def _ends(src_ref, land_ref, peer_index, me, n, axis, gather):
    if gather:
        return src_ref, _block(land_ref, me, n, axis)
    return _block(src_ref, peer_index, n, axis), land_ref.at[me]


def _send_start(groups, gather, *, name):
    flat = [(g, j, mat, axis) for g, items in enumerate(groups) for j, (mat, axis) in enumerate(items)]
    M, G = len(flat), len(groups)
    lands, ns = [], []
    for _, _, mat, axis in flat:
        A, B = mat.shape
        if gather:
            lands.append((A * N_DEV, B) if axis == 0 else (A, B * N_DEV))
            ns.append(A if axis == 0 else B)
        else:
            lands.append((N_DEV, A // N_DEV, B) if axis == 0 else (N_DEV, A, B // N_DEV))
            ns.append(A // N_DEV if axis == 0 else B // N_DEV)

    def body(*refs):
        src_refs, land_refs, sems = refs[:M], refs[M:2 * M], refs[2 * M:2 * M + 3 * G]
        token = refs[-1]
        x, y, c = lax.axis_index("x"), lax.axis_index("y"), lax.axis_index("c")
        me = 4 * x + 2 * y + c
        for i, (g, j, _, axis) in enumerate(flat):
            for k, (px, py, pc) in enumerate(_peers(x, y, c)):
                s, d = _ends(src_refs[i], land_refs[i], 4 * px + 2 * py + pc, me, ns[i], axis, gather)
                pltpu.make_async_remote_copy(
                    src_ref=s, dst_ref=d, send_sem=sems[3 * g].at[7 * j + k], recv_sem=sems[3 * g + 1].at[7 * j + k],
                    device_id=(px, py, pc), device_id_type=MESH).start()
            s, d = _ends(src_refs[i], land_refs[i], me, me, ns[i], axis, gather)
            pltpu.make_async_copy(s, d, sems[3 * g + 2].at[j]).start()
        token[...] = jnp.zeros_like(token)

    sem_shapes = []
    for items in groups:
        sem_shapes += [pltpu.SemaphoreType.DMA((7 * len(items),))] * 2 + [pltpu.SemaphoreType.DMA((len(items),))]
    outs = pl.pallas_call(
        body, name=name,
        out_shape=(*sem_shapes, *[pltpu.HBM(m.shape, m.dtype) for _, _, m, _ in flat],
                   *[pltpu.HBM(shp, m.dtype) for shp, (_, _, m, _) in zip(lands, flat)],
                   jax.ShapeDtypeStruct((8, LANES), F32)),
        in_specs=[HBM_SPEC] * (2 * M),
        out_specs=(*[SEM_SPEC] * (3 * G), *[HBM_SPEC] * (2 * M), pl.BlockSpec(memory_space=pltpu.VMEM)),
        input_output_aliases={i: 3 * G + i for i in range(2 * M)},
        compiler_params=pltpu.CompilerParams(has_side_effects=FLOWING),
    )(*[pltpu.with_memory_space_constraint(m, pltpu.HBM) for _, _, m, _ in flat],
      *[pltpu.with_memory_space_constraint(lax.empty(shp, m.dtype), pltpu.HBM)
        for shp, (_, _, m, _) in zip(lands, flat)])
    handles = []
    for g in range(G):
        idx = [i for i, f in enumerate(flat) if f[0] == g]
        handles.append((outs[3 * g], outs[3 * g + 1], outs[3 * g + 2], [outs[3 * G + i] for i in idx],
                        [outs[3 * G + M + i] for i in idx], [flat[i][3] for i in idx]))
    return handles, outs[-1]


def _send_wait(handle, gather, after, *, name):
    send_sems, recv_sems, local_sems, mats, lands, axes = handle
    n_m = len(mats)
    ns = []
    for mat, land, axis in zip(mats, lands, axes):
        ns.append(mat.shape[axis] if gather else land.shape[1 + axis])

    def body(*refs):
        src_refs, land_refs = refs[:n_m], refs[n_m:2 * n_m]
        ssem, rsem, lsem = refs[2 * n_m:2 * n_m + 3]
        x, y, c = lax.axis_index("x"), lax.axis_index("y"), lax.axis_index("c")
        me = 4 * x + 2 * y + c
        for j in range(n_m):
            for k, (px, py, pc) in enumerate(_peers(x, y, c)):
                s, d = _ends(src_refs[j], land_refs[j], 4 * px + 2 * py + pc, me, ns[j], axes[j], gather)
                cp = pltpu.make_async_remote_copy(
                    src_ref=s, dst_ref=d, send_sem=ssem.at[7 * j + k], recv_sem=rsem.at[7 * j + k],
                    device_id=(px, py, pc), device_id_type=MESH)
                cp.wait_send()
                cp.wait_recv()
            s, d = _ends(src_refs[j], land_refs[j], me, me, ns[j], axes[j], gather)
            pltpu.make_async_copy(s, d, lsem.at[j]).wait()

    outs = pl.pallas_call(
        body, name=name,
        out_shape=(*[pltpu.HBM(m.shape, m.dtype) for m in mats], *[pltpu.HBM(l.shape, l.dtype) for l in lands]),
        in_specs=[HBM_SPEC] * (2 * n_m) + [SEM_SPEC] * 3 + [pl.BlockSpec(memory_space=pl.ANY)],
        out_specs=tuple([HBM_SPEC] * (2 * n_m)),
        input_output_aliases={i: i for i in range(2 * n_m)},
        compiler_params=pltpu.CompilerParams(has_side_effects=FLOWING),
    )(*mats, *lands, send_sems, recv_sems, local_sems, after)
    return list(outs[n_m:])


def _sum_parts(parts, *, name):
    n, R, C = parts.shape
    rb = _pick(R, 512) if R % LANES == 0 else R

    def body(p_ref, o_ref):
        acc = p_ref[0].astype(F32)
        for k in range(1, n):
            acc = acc + p_ref[k].astype(F32)
        o_ref[...] = acc

    return pl.pallas_call(
        body, grid=(R // rb,), in_specs=[pl.BlockSpec((n, rb, C), lambda i: (0, i, 0))],
        out_specs=pl.BlockSpec((rb, C), lambda i: (i, 0)),
        out_shape=jax.ShapeDtypeStruct((R, C), F32),
        compiler_params=_params(("parallel",)), name=name,
    )(parts)


def _adamw(w, m, v, parts, *, name):
    L, R, C = w.shape
    n = parts[0].shape[0]
    cap = max(16, VMEM_LIMIT // 3 // (2 * L * n * C * parts[0].dtype.itemsize))
    rb = max([r for r in range(16, min(R, cap) + 1, 16) if R % r == 0], default=R)

    def body(w_ref, m_ref, v_ref, *rest):
        p_refs = rest[:L]
        g_ref, d_ref, nm_ref, nv_ref = rest[L:]
        for l in range(L):
            @pl.when(pl.program_id(0) == l)
            def _(p_ref=p_refs[l]):
                g = p_ref[0].astype(F32)
                for k in range(1, n):
                    g = g + p_ref[k].astype(F32)
                mn = ADAM_B1 * m_ref[...] + (1.0 - ADAM_B1) * g
                vn = ADAM_B2 * v_ref[...] + (1.0 - ADAM_B2) * jnp.square(g)
                m_hat = mn / (1.0 - ADAM_B1 ** ADAM_STEP)
                v_hat = vn / (1.0 - ADAM_B2 ** ADAM_STEP)
                g_ref[...] = g
                d_ref[...] = -ADAM_LR * (m_hat / (jnp.sqrt(v_hat) + ADAM_EPS) + ADAM_WD * w_ref[...])
                nm_ref[...] = mn
                nv_ref[...] = vn

    row = pl.BlockSpec((None, rb, C), lambda l, i: (l, i, 0))
    part = lambda k: pl.BlockSpec((n, rb, C), lambda l, i: (0, jnp.where(l == k, i, 0), 0))
    return pl.pallas_call(
        body, grid=(L, R // rb), in_specs=[row, row, row] + [part(k) for k in range(L)],
        out_specs=[row] * 4, out_shape=[jax.ShapeDtypeStruct((L, R, C), F32)] * 4,
        compiler_params=_params(("arbitrary", "arbitrary")), name=name,
    )(w, m, v, *parts)


BIG = (("a_w_in", "a_w_in", 1, False), ("a_w_out", "a_w_out", 0, False), ("kv_w", "kv_w", 0, False),
       ("b_w_q", "b_w_q_t", 0, True), ("b_w_o", "b_w_o", 1, False), ("ffn_w_up", "ffn_w_up_t", 0, True),
       ("ffn_w_down", "ffn_w_down", 0, False))
SMALL_SHARDED = (("a_ln_g", 1), ("a_ln_b", 1), ("ffn_conv_w", 2), ("ln_g", 2), ("ln_b", 2))
REPLICATED = ("a_w_s", "a_b_s", "rel_table", "ffn_conv_b")


def _pack_rows(arrs, lead=0):
    lshape = arrs[0].shape[:lead]
    p = jnp.concatenate([a.reshape(*lshape, -1, LANES) for a in arrs], axis=lead)
    pad = -p.shape[lead] % 8
    return jnp.pad(p, [(0, 0)] * lead + [(0, pad), (0, 0)])


def _unpack_rows(packed, shapes, lead=0):
    lshape = packed.shape[:lead]
    out, off = [], 0
    for shp in shapes:
        r = int(np.prod(shp)) // LANES
        out.append(lax.slice_in_dim(packed, off, off + r, axis=lead).reshape(*lshape, *shp))
        off += r
    return out


def _as_mats(a, transposed):
    a = a[None] if a.ndim == 2 else a
    return jnp.swapaxes(a, 1, 2) if transposed else a


def _merge_shards(stacked, axis):
    a = jnp.moveaxis(stacked, 0, axis)
    shp = list(a.shape)
    return a.reshape(shp[:axis] + [shp[axis] * shp[axis + 1]] + shp[axis + 2:])


def _split_shards(full, axis):
    shp = list(full.shape)
    a = full.reshape(shp[:axis] + [N_DEV, shp[axis] // N_DEV] + shp[axis + 1:])
    return jnp.moveaxis(a, axis, 0)


def kernel(x, a_w_in, a_ln_g, a_ln_b, a_w_s, a_b_s, a_w_out, kv_w, b_w_q, b_w_o, rel_table, ffn_w_up, ffn_conv_w, ffn_conv_b, ffn_w_down, ln_g, ln_b, loss_target, m_a_w_in, m_a_ln_g, m_a_ln_b, m_a_w_s, m_a_b_s, m_a_w_out, m_kv_w, m_b_w_q, m_b_w_o, m_rel_table, m_ffn_w_up, m_ffn_conv_w, m_ffn_conv_b, m_ffn_w_down, m_ln_g, m_ln_b, v_a_w_in, v_a_ln_g, v_a_ln_b, v_a_w_s, v_a_b_s, v_a_w_out, v_kv_w, v_b_w_q, v_b_w_o, v_rel_table, v_ffn_w_up, v_ffn_conv_w, v_ffn_conv_b, v_ffn_w_down, v_ln_g, v_ln_b):
    names = ["a_w_in", "a_ln_g", "a_ln_b", "a_w_s", "a_b_s", "a_w_out", "kv_w", "b_w_q", "b_w_o", "rel_table",
             "ffn_w_up", "ffn_conv_w", "ffn_conv_b", "ffn_w_down", "ln_g", "ln_b"]
    w = dict(zip(names, (a_w_in, a_ln_g, a_ln_b, a_w_s, a_b_s, a_w_out, kv_w, b_w_q, b_w_o, rel_table,
                         ffn_w_up, ffn_conv_w, ffn_conv_b, ffn_w_down, ln_g, ln_b)))
    m = dict(zip(names, (m_a_w_in, m_a_ln_g, m_a_ln_b, m_a_w_s, m_a_b_s, m_a_w_out, m_kv_w, m_b_w_q, m_b_w_o,
                         m_rel_table, m_ffn_w_up, m_ffn_conv_w, m_ffn_conv_b, m_ffn_w_down, m_ln_g, m_ln_b)))
    v = dict(zip(names, (v_a_w_in, v_a_ln_g, v_a_ln_b, v_a_w_s, v_a_b_s, v_a_w_out, v_kv_w, v_b_w_q, v_b_w_o,
                         v_rel_table, v_ffn_w_up, v_ffn_conv_w, v_ffn_conv_b, v_ffn_w_down, v_ln_g, v_ln_b)))
    small_names = [n for n, _ in SMALL_SHARDED]
    small_shapes = [w[n].shape for n in small_names]
    rep_shapes = [w[n].shape for n in REPLICATED]

    axis_of = {key: axis for _, key, axis, _ in BIG}
    src = {}
    for n, key, axis, tr in BIG:
        loc = _as_mats(w[n], tr).astype(BF16)
        for l in range(loc.shape[0]):
            src[(key, l)] = loc[l]
    order = []
    for i in range(DEPTH):
        if i < N_A:
            order += [[("a_w_in", i)], [("a_w_out", i)]]
        else:
            order += [([("kv_w", 0)] if i == N_A else []) + [("b_w_q_t", i - N_A)], [("b_w_o", i - N_A)]]
        order += [[("ffn_w_up_t", i)], [("ffn_w_down", i)]]
    small_src = _pack_rows([w[n] for n in small_names])
    srows = small_src.shape[0]
    handles, _ = _send_start([[(small_src, 0)]] + [[(src[kl], axis_of[kl[0]]) for kl in grp] for grp in order],
                             True, name="gather_start")
    small_all = _send_wait(handles[0], True, x, name="gather_wait_small")[0]
    small_st = _unpack_rows(small_all.reshape(N_DEV, srows, LANES), small_shapes, lead=1)
    base = {n: w[n] for n in REPLICATED}
    for (n, ax), st in zip(SMALL_SHARDED, small_st):
        base[n] = _merge_shards(st, ax)
    for n, key, _, tr in BIG:
        base[key] = [None] * (1 if w[n].ndim == 2 else w[n].shape[0])

    def fetch(group, after):
        mats = _send_wait(handles[1 + group], True, after, name=f"gather_wait_{group}")
        return dict(zip(order[group], mats))

    sent = {}

    def emit(group, mats):
        keys = list(mats)
        hs, token = _send_start([[(mats[kl], axis_of[kl[0]]) for kl in keys]], False, name=f"exchange_start_{group}")
        sent[group] = (keys, hs[0])
        return token[0, 0]

    loss, grad_x, grads = _local_step(x[0], loss_target[0], _Weights(base, fetch, emit))
    loss = lax.psum(loss, ("x", "y", "c"))

    out = {}
    small_pack = _pack_rows([_split_shards(grads[n], ax) for n, ax in SMALL_SHARDED], lead=1)
    rep_pack = _pack_rows([grads[n] for n in REPLICATED])
    mine = jnp.concatenate([small_pack.reshape(N_DEV * srows, LANES), rep_pack], axis=0)
    hs, _ = _send_start([[(mine, 0)]], True, name="small_grads_start")

    landed = {}
    for group in sorted(sent, reverse=True):
        keys, h = sent[group]
        landed.update(zip(keys, _send_wait(h, False, grad_x, name=f"exchange_wait_{group}")))
    last = grad_x
    for n, key, axis, tr in BIG:
        shp = w[n].shape
        parts = [landed[(key, l)] for l in range(1 if len(shp) == 2 else shp[0])]
        res = _adamw(_as_mats(w[n], tr), _as_mats(m[n], tr), _as_mats(v[n], tr), parts, name=f"adamw_{n}")
        out[n] = [(jnp.swapaxes(r, 1, 2) if tr else r).reshape(shp) for r in res]
        last = res[0]

    allp = _send_wait(hs[0], True, last, name="small_grads_wait")[0]
    gsum = _sum_parts(allp.reshape(N_DEV, mine.shape[0], LANES), name="sum_small_grads")
    g_small = lax.dynamic_slice_in_dim(gsum, _my_index() * srows, srows, axis=0)
    gs_in = jnp.concatenate([g_small, gsum[N_DEV * srows:]], axis=0)[None]
    pack_sr = lambda d: jnp.concatenate([_pack_rows([d[n] for n in small_names]),
                                         _pack_rows([d[n] for n in REPLICATED])], axis=0)
    res = _adamw(pack_sr(w)[None], pack_sr(m)[None], pack_sr(v)[None], [gs_in], name="adamw_small")
    for n, vals in zip(small_names, zip(*[_unpack_rows(r[0, :srows], small_shapes) for r in res])):
        out[n] = list(vals)
    for n, vals in zip(REPLICATED, zip(*[_unpack_rows(r[0, srows:], rep_shapes) for r in res])):
        out[n] = list(vals)

    return (loss, grad_x[None], *[out[n][0] for n in names], *[out[n][1] for n in names],
            *[out[n][2] for n in names], *[out[n][3] for n in names])
```

```python
import math

import numpy as np
import jax
import jax.numpy as jnp
from jax import lax
from jax.experimental import pallas as pl
from jax.experimental.pallas import tpu as pltpu

F32 = jnp.float32
BF16 = jnp.bfloat16
ACT = jnp.bfloat16
MESH = pl.DeviceIdType.MESH

N_DEV = 8
DEPTH = 4
N_A = 2
CHUNK = 128
BLK = 128
HEAD_DIM = 64
DILATED_GROUPS = ((128, 1), (512, 4), (2048, 16))
N_GROUPS = 3
REL_BUCKETS = 32
REL_MAX_DIST = 2048
ALPHA = (2 * DEPTH) ** 0.25
LN_EPS = 1e-5
NEG = -1e30
ADAM_LR = 0.001
ADAM_B1 = 0.9
ADAM_B2 = 0.999
ADAM_EPS = 1e-08
ADAM_WD = 0.01
ADAM_STEP = 10

LANES = 128
VMEM_LIMIT = 56 * 1024 * 1024
MM_TILE_CAP = 1408
INV_SQRT2 = 1.0 / math.sqrt(2.0)
INV_SQRT_2PI = 1.0 / math.sqrt(2.0 * math.pi)


def _pick(n, cap):
    best = None
    for t in range(LANES, min(n, cap) + 1, LANES):
        if n % t == 0:
            best = t
    return best if best is not None else n


def _params(sem):
    return pltpu.CompilerParams(dimension_semantics=sem, vmem_limit_bytes=VMEM_LIMIT)


def _gelu(x):
    return 0.5 * x * (1.0 + lax.erf(x * INV_SQRT2))


def _gelu_grad(x):
    return 0.5 * (1.0 + lax.erf(x * INV_SQRT2)) + x * jnp.exp(-0.5 * x * x) * INV_SQRT_2PI


def _mm(a, b, *, ta=False, tb=False, out_dtype=F32, scale=None, name):
    halves = isinstance(a, tuple)
    ash = (a[0].shape[0], 2 * a[0].shape[1]) if halves else a.shape
    if ta:
        K, M = ash
    else:
        M, K = ash
    if tb:
        N, Kb = b.shape
    else:
        Kb, N = b.shape
    assert K == Kb, (ash, b.shape, ta, tb)
    tm, tn, tk = _pick(M, MM_TILE_CAP), _pick(N, MM_TILE_CAP), _pick(K, MM_TILE_CAP)
    if halves and ta:
        tm = _pick(M // 2, MM_TILE_CAP)
    if halves and not ta:
        tk = _pick(K // 2, MM_TILE_CAP)
    nk = K // tk
    nh = (M // 2 // tm if ta else K // 2 // tk) if halves else 0
    dn = (((0 if ta else 1,), (1 if tb else 0,)), ((), ()))

    def body(*refs):
        a_refs, (b_ref, o_ref, acc_ref) = refs[:-3], refs[-3:]
        k = pl.program_id(2)

        def accumulate(a_ref):
            part = lax.dot_general(a_ref[...].astype(BF16), b_ref[...].astype(BF16), dn,
                                   preferred_element_type=F32)

            @pl.when(k == 0)
            def _():
                acc_ref[...] = part

            @pl.when(k > 0)
            def _():
                acc_ref[...] += part

        if halves:
            first = (pl.program_id(0) if ta else k) < nh
            pl.when(first)(lambda: accumulate(a_refs[0]))
            pl.when(jnp.logical_not(first))(lambda: accumulate(a_refs[1]))
        else:
            accumulate(a_refs[0])

        @pl.when(k == nk - 1)
        def _():
            r = acc_ref[...]
            if scale is not None:
                r = r * scale
            o_ref[...] = r.astype(out_dtype)

    if halves and ta:
        a_specs = [pl.BlockSpec((tk, tm), lambda i, j, k: (jnp.where(i < nh, k, 0), jnp.minimum(i, nh - 1))),
                   pl.BlockSpec((tk, tm), lambda i, j, k: (jnp.where(i >= nh, k, 0), jnp.maximum(i - nh, 0)))]
    elif halves:
        a_specs = [pl.BlockSpec((tm, tk), lambda i, j, k: (i, jnp.minimum(k, nh - 1))),
                   pl.BlockSpec((tm, tk), lambda i, j, k: (i, jnp.maximum(k - nh, 0)))]
    else:
        a_specs = [pl.BlockSpec((tk, tm), lambda i, j, k: (k, i)) if ta
                   else pl.BlockSpec((tm, tk), lambda i, j, k: (i, k))]
    b_spec = (pl.BlockSpec((tn, tk), lambda i, j, k: (j, k)) if tb
              else pl.BlockSpec((tk, tn), lambda i, j, k: (k, j)))
    return pl.pallas_call(
        body, grid=(M // tm, N // tn, nk), in_specs=[*a_specs, b_spec],
        out_specs=pl.BlockSpec((tm, tn), lambda i, j, k: (i, j)),
        out_shape=jax.ShapeDtypeStruct((M, N), out_dtype),
        scratch_shapes=[pltpu.VMEM((tm, tn), F32)],
        compiler_params=_params(("parallel", "parallel", "arbitrary")), name=name,
    )(*(a if halves else (a,)), b)


def _add_ln_fwd(x, h, g, b, *, name):
    T, D = x.shape
    rb = _pick(T, 512)

    def body(x_ref, h_ref, g_ref, b_ref, o_ref, ob_ref):
        pre = ALPHA * x_ref[...] + h_ref[...]
        mu = jnp.mean(pre, axis=1, keepdims=True)
        cen = pre - mu
        var = jnp.mean(cen * cen, axis=1, keepdims=True)
        y = cen * lax.rsqrt(var + LN_EPS) * g_ref[...] + b_ref[...]
        o_ref[...] = y
        ob_ref[...] = y.astype(BF16)

    row = pl.BlockSpec((rb, D), lambda i: (i, 0))
    vec = pl.BlockSpec((1, D), lambda i: (0, 0))
    return pl.pallas_call(
        body, grid=(T // rb,), in_specs=[row, row, vec, vec], out_specs=[row, row],
        out_shape=[jax.ShapeDtypeStruct((T, D), F32), jax.ShapeDtypeStruct((T, D), BF16)],
        compiler_params=_params(("parallel",)), name=name,
    )(x, h, g.reshape(1, D), b.reshape(1, D))


def _add_ln_bwd(x, h, g, terms, *, name):
    T, D = x.shape
    rb = _pick(T, 512)
    coefs = [c for c, _ in terms]
    nt = len(terms)

    def body(*refs):
        x_ref, h_ref, g_ref = refs[:3]
        t_refs = refs[3:3 + nt]
        dp_ref, dpb_ref, dg_ref, db_ref = refs[3 + nt:]
        dy = None
        for c, r in zip(coefs, t_refs):
            v = r[...] if c == 1.0 else c * r[...]
            dy = v if dy is None else dy + v
        pre = ALPHA * x_ref[...] + h_ref[...]
        mu = jnp.mean(pre, axis=1, keepdims=True)
        cen = pre - mu
        var = jnp.mean(cen * cen, axis=1, keepdims=True)
        rstd = lax.rsqrt(var + LN_EPS)
        xhat = cen * rstd
        dxh = dy * g_ref[...]
        m1 = jnp.mean(dxh, axis=1, keepdims=True)
        m2 = jnp.mean(dxh * xhat, axis=1, keepdims=True)
        dpre = rstd * (dxh - m1 - xhat * m2)
        dp_ref[...] = dpre
        dpb_ref[...] = dpre.astype(BF16)
        dg = jnp.sum(dy * xhat, axis=0, keepdims=True)
        db = jnp.sum(dy, axis=0, keepdims=True)

        @pl.when(pl.program_id(0) == 0)
        def _():
            dg_ref[...] = dg
            db_ref[...] = db

        @pl.when(pl.program_id(0) > 0)
        def _():
            dg_ref[...] += dg
            db_ref[...] += db

    row = pl.BlockSpec((rb, D), lambda i: (i, 0))
    vec = pl.BlockSpec((1, D), lambda i: (0, 0))
    return pl.pallas_call(
        body, grid=(T // rb,), in_specs=[row, row, vec] + [row] * nt,
        out_specs=[row, row, vec, vec],
        out_shape=[jax.ShapeDtypeStruct((T, D), F32), jax.ShapeDtypeStruct((T, D), BF16),
                   jax.ShapeDtypeStruct((1, D), F32), jax.ShapeDtypeStruct((1, D), F32)],
        compiler_params=_params(("arbitrary",)), name=name,
    )(x, h, g.reshape(1, D), *[a for _, a in terms])


def _lincomb(terms, out_dtype, *, name):
    R, C = terms[0][1].shape
    rb = _pick(R, 512)
    coefs = [c for c, _ in terms]
    nt = len(terms)

    def body(*refs):
        acc = None
        for c, r in zip(coefs, refs[:nt]):
            v = r[...].astype(F32)
            v = v if c == 1.0 else c * v
            acc = v if acc is None else acc + v
        refs[nt][...] = acc.astype(out_dtype)

    row = pl.BlockSpec((rb, C), lambda i: (i, 0))
    return pl.pallas_call(
        body, grid=(R // rb,), in_specs=[row] * nt, out_specs=row,
        out_shape=jax.ShapeDtypeStruct((R, C), out_dtype),
        compiler_params=_params(("parallel",)), name=name,
    )(*[a for _, a in terms])


def _loss_grad(y, tgt, *, name):
    T, D = y.shape
    rb = _pick(T, 512)

    def body(y_ref, t_ref, dy_ref, l_ref):
        err = y_ref[...] - t_ref[...]
        dy_ref[...] = err * (1.0 / D)
        part = jnp.sum(jnp.sum(err * err, axis=1, keepdims=True), axis=0, keepdims=True) * (0.5 / D)
        part = jnp.broadcast_to(part, (1, LANES))

        @pl.when(pl.program_id(0) == 0)
        def _():
            l_ref[...] = part

        @pl.when(pl.program_id(0) > 0)
        def _():
            l_ref[...] += part

    row = pl.BlockSpec((rb, D), lambda i: (i, 0))
    return pl.pallas_call(
        body, grid=(T // rb,), in_specs=[row, row],
        out_specs=[row, pl.BlockSpec((1, LANES), lambda i: (0, 0))],
        out_shape=[jax.ShapeDtypeStruct((T, D), F32), jax.ShapeDtypeStruct((1, LANES), F32)],
        compiler_params=_params(("arbitrary",)), name=name,
    )(y, tgt)


def _sgu_fwd(zp, ln_g, ln_b, ws, bst, *, name):
    T, E2 = zp.shape
    E = E2 // 2
    G = ws.shape[0]
    cg = E // G
    rb = 2 * CHUNK

    def body(z_ref, g_ref, b_ref, ws_ref, bs_ref, y_ref):
        u = _gelu(z_ref[:, :E].astype(F32))
        v = _gelu(z_ref[:, E:].astype(F32))
        mu = jnp.mean(v, axis=1, keepdims=True)
        cen = v - mu
        var = jnp.mean(cen * cen, axis=1, keepdims=True)
        vn = (cen * lax.rsqrt(var + LN_EPS) * g_ref[...] + b_ref[...]).astype(BF16)
        for ci in range(rb // CHUNK):
            rows = slice(ci * CHUNK, (ci + 1) * CHUNK)
            for gi in range(G):
                cols = slice(gi * cg, (gi + 1) * cg)
                sv = jnp.dot(ws_ref[gi], vn[rows, cols], preferred_element_type=F32)
                sv = sv + bs_ref[:, gi:gi + 1]
                y_ref[rows, cols] = (u[rows, cols] * sv).astype(BF16)

    return pl.pallas_call(
        body, grid=(T // rb,),
        in_specs=[pl.BlockSpec((rb, E2), lambda i: (i, 0)),
                  pl.BlockSpec((1, E), lambda i: (0, 0)), pl.BlockSpec((1, E), lambda i: (0, 0)),
                  pl.BlockSpec((G, CHUNK, CHUNK), lambda i: (0, 0, 0)),
                  pl.BlockSpec((CHUNK, G), lambda i: (0, 0))],
        out_specs=pl.BlockSpec((rb, E), lambda i: (i, 0)),
        out_shape=jax.ShapeDtypeStruct((T, E), BF16),
        compiler_params=_params(("parallel",)), name=name,
    )(zp, ln_g.reshape(1, E), ln_b.reshape(1, E), ws, bst)


def _sgu_bwd(zp, dy, ln_g, ln_b, ws, wst, bst, *, name):
    T, E2 = zp.shape
    E = E2 // 2
    G = ws.shape[0]
    cg = E // G
    rb = CHUNK
    nsteps = T // rb

    def body(z_ref, dy_ref, g_ref, b_ref, ws_ref, wst_ref, bs_ref,
             dz_ref, dg_ref, db_ref, dws_ref, dbs_ref, dsv_acc):
        step = pl.program_id(0)

        @pl.when(step == 0)
        def _():
            dg_ref[...] = jnp.zeros_like(dg_ref)
            db_ref[...] = jnp.zeros_like(db_ref)
            dws_ref[...] = jnp.zeros_like(dws_ref)
            dsv_acc[...] = jnp.zeros_like(dsv_acc)

        zu = z_ref[:, :E].astype(F32)
        zv = z_ref[:, E:].astype(F32)
        u = _gelu(zu)
        v = _gelu(zv)
        mu = jnp.mean(v, axis=1, keepdims=True)
        cen = v - mu
        var = jnp.mean(cen * cen, axis=1, keepdims=True)
        rstd = lax.rsqrt(var + LN_EPS)
        xhat = cen * rstd
        vn = (xhat * g_ref[...] + b_ref[...]).astype(BF16)
        dyv = dy_ref[...].astype(F32)
        dsv = dyv * u
        dsv_acc[...] += dsv
        dsvb = dsv.astype(BF16)
        tril = (lax.broadcasted_iota(jnp.int32, (CHUNK, CHUNK), 0)
                >= lax.broadcasted_iota(jnp.int32, (CHUNK, CHUNK), 1))
        du_parts = []
        dvn_parts = []
        for gi in range(G):
            cols = slice(gi * cg, (gi + 1) * cg)
            sv = jnp.dot(ws_ref[gi], vn[:, cols], preferred_element_type=F32) + bs_ref[:, gi:gi + 1]
            du_parts.append(dyv[:, cols] * sv)
            dvn_parts.append(jnp.dot(wst_ref[gi], dsvb[:, cols], preferred_element_type=F32))
            dw = lax.dot_general(dsvb[:, cols], vn[:, cols], (((1,), (1,)), ((), ())),
                                 preferred_element_type=F32)
            dws_ref[gi] += jnp.where(tril, dw, 0.0)
        du = jnp.concatenate(du_parts, axis=1)
        dvn = jnp.concatenate(dvn_parts, axis=1)
        dg_ref[...] += jnp.sum(dvn * xhat, axis=0, keepdims=True)
        db_ref[...] += jnp.sum(dvn, axis=0, keepdims=True)
        dxh = dvn * g_ref[...]
        m1 = jnp.mean(dxh, axis=1, keepdims=True)
        m2 = jnp.mean(dxh * xhat, axis=1, keepdims=True)
        dv = rstd * (dxh - m1 - xhat * m2)
        dz_ref[:, :E] = (du * _gelu_grad(zu)).astype(BF16)
        dz_ref[:, E:] = (dv * _gelu_grad(zv)).astype(BF16)

        @pl.when(step == nsteps - 1)
        def _():
            lane = lax.broadcasted_iota(jnp.int32, (CHUNK, LANES), 1)
            out = jnp.zeros((CHUNK, LANES), F32)
            for gi in range(G):
                s = jnp.sum(dsv_acc[:, gi * cg:(gi + 1) * cg], axis=1, keepdims=True)
                out = jnp.where(lane == gi, s, out)
            dbs_ref[...] = out

    vecE = pl.BlockSpec((1, E), lambda i: (0, 0))
    wspec = pl.BlockSpec((G, CHUNK, CHUNK), lambda i: (0, 0, 0))
    return pl.pallas_call(
        body, grid=(nsteps,),
        in_specs=[pl.BlockSpec((rb, E2), lambda i: (i, 0)), pl.BlockSpec((rb, E), lambda i: (i, 0)),
                  vecE, vecE, wspec, wspec, pl.BlockSpec((CHUNK, G), lambda i: (0, 0))],
        out_specs=[pl.BlockSpec((rb, E2), lambda i: (i, 0)), vecE, vecE, wspec,
                   pl.BlockSpec((CHUNK, LANES), lambda i: (0, 0))],
        out_shape=[jax.ShapeDtypeStruct((T, E2), BF16), jax.ShapeDtypeStruct((1, E), F32),
                   jax.ShapeDtypeStruct((1, E), F32), jax.ShapeDtypeStruct((G, CHUNK, CHUNK), F32),
                   jax.ShapeDtypeStruct((CHUNK, LANES), F32)],
        scratch_shapes=[pltpu.VMEM((CHUNK, E), F32)],
        compiler_params=_params(("arbitrary",)), name=name,
    )(zp, dy, ln_g.reshape(1, E), ln_b.reshape(1, E), ws, wst, bst)


def _shift_down(x, k, row):
    return jnp.where(row >= k, pltpu.roll(x, k, 0), 0.0)


def _shift_up(x, k, row, T):
    return jnp.where(row < T - k, pltpu.roll(x, T - k, 0), 0.0)


def _conv3(x, w_ref, b_ref, row):
    return (w_ref[0:1, :] * _shift_down(x, 2, row) + w_ref[1:2, :] * _shift_down(x, 1, row)
            + w_ref[2:3, :] * x + b_ref[...])


def _convgate_fwd(hh, cw, cb, *, name):
    T, F2 = hh.shape
    F = F2 // 2
    ns = F // LANES

    def body(a_ref, g_ref, wa_ref, wg_ref, ba_ref, bg_ref, o_ref):
        row = lax.broadcasted_iota(jnp.int32, (T, LANES), 0)
        ca = _conv3(a_ref[...].astype(F32), wa_ref, ba_ref, row)
        cgv = _conv3(g_ref[...].astype(F32), wg_ref, bg_ref, row)
        o_ref[...] = (_gelu(ca) * cgv).astype(BF16)

    sa = lambda r: pl.BlockSpec((r, LANES), lambda j: (0, j))
    sg = lambda r: pl.BlockSpec((r, LANES), lambda j: (0, j + ns))
    return pl.pallas_call(
        body, grid=(ns,), in_specs=[sa(T), sg(T), sa(3), sg(3), sa(1), sg(1)],
        out_specs=sa(T), out_shape=jax.ShapeDtypeStruct((T, F), BF16),
        compiler_params=_params(("parallel",)), name=name,
    )(hh, hh, cw, cw, cb, cb)


def _convgate_bwd(hh, dact, cw, cb, *, name):
    T, F2 = hh.shape
    F = F2 // 2
    ns = F // LANES

    def body(a_ref, g_ref, d_ref, wa_ref, wg_ref, ba_ref, bg_ref,
             da_ref, dg_ref, dwa_ref, dwg_ref, dba_ref, dbg_ref):
        row = lax.broadcasted_iota(jnp.int32, (T, LANES), 0)
        d = d_ref[...].astype(F32)
        ca = _conv3(a_ref[...].astype(F32), wa_ref, ba_ref, row)
        cgv = _conv3(g_ref[...].astype(F32), wg_ref, bg_ref, row)
        cdf = 0.5 * (1.0 + lax.erf(ca * INV_SQRT2))
        dca = d * cgv * (cdf + ca * jnp.exp(-0.5 * ca * ca) * INV_SQRT_2PI)
        dcg = d * (ca * cdf)
        for x_ref, w_ref, dc, dx_ref, dw_ref, db_ref in (
                (a_ref, wa_ref, dca, da_ref, dwa_ref, dba_ref),
                (g_ref, wg_ref, dcg, dg_ref, dwg_ref, dbg_ref)):
            x = x_ref[...].astype(F32)
            dx = (w_ref[2:3, :] * dc + w_ref[1:2, :] * _shift_up(dc, 1, row, T)
                  + w_ref[0:1, :] * _shift_up(dc, 2, row, T))
            dx_ref[...] = dx.astype(BF16)
            dw_ref[0:1, :] = jnp.sum(dc * _shift_down(x, 2, row), axis=0, keepdims=True)
            dw_ref[1:2, :] = jnp.sum(dc * _shift_down(x, 1, row), axis=0, keepdims=True)
            dw_ref[2:3, :] = jnp.sum(dc * x, axis=0, keepdims=True)
            db_ref[...] = jnp.sum(dc, axis=0, keepdims=True)

    sa = lambda r: pl.BlockSpec((r, LANES), lambda j: (0, j))
    sg = lambda r: pl.BlockSpec((r, LANES), lambda j: (0, j + ns))
    return pl.pallas_call(
        body, grid=(ns,), in_specs=[sa(T), sg(T), sa(T), sa(3), sg(3), sa(1), sg(1)],
        out_specs=[sa(T), sa(T), sa(3), sa(3), sa(1), sa(1)],
        out_shape=[jax.ShapeDtypeStruct((T, F), BF16), jax.ShapeDtypeStruct((T, F), BF16),
                   jax.ShapeDtypeStruct((3, F), F32), jax.ShapeDtypeStruct((3, F), F32),
                   jax.ShapeDtypeStruct((1, F), F32), jax.ShapeDtypeStruct((1, F), F32)],
        compiler_params=_params(("parallel",)), name=name,
    )(hh, hh, dact, cw, cw, cb, cb)


def _bucket_maps():
    iq = np.arange(BLK)[:, None]
    ik = np.arange(2 * BLK)[None, :]
    delta = iq + BLK - ik
    maps = []
    for win, dil in DILATED_GROUPS:
        n = np.clip(delta, 0, None) * dil
        max_exact = REL_BUCKETS // 2
        nf = np.maximum(n, 1).astype(np.float32)
        large = max_exact + (np.log(nf / np.float32(max_exact)) / np.float32(math.log(REL_MAX_DIST / max_exact))
                             * np.float32(REL_BUCKETS - max_exact)).astype(np.int32)
        large = np.minimum(large, REL_BUCKETS - 1)
        bucket = np.where(n < max_exact, n, large)
        valid = (delta >= 0) & (delta <= win // dil)
        maps.append(np.where(valid, bucket, -1).astype(np.int32))
    return np.stack(maps)


def _band_bias(rel_table, bmap, H, *, name):
    def body(t_ref, m_ref, o_ref):
        g = pl.program_id(0)
        bm = m_ref[0]
        for h in range(H):
            acc = jnp.full((BLK, 2 * BLK), NEG, F32)
            for b in range(REL_BUCKETS):
                acc = jnp.where(bm == b, t_ref[b, g * H + h], acc)
            o_ref[0, h] = acc

    return pl.pallas_call(
        body, grid=(N_GROUPS,),
        in_specs=[pl.BlockSpec(memory_space=pltpu.SMEM),
                  pl.BlockSpec((1, BLK, 2 * BLK), lambda g: (g, 0, 0))],
        out_specs=pl.BlockSpec((1, H, BLK, 2 * BLK), lambda g: (g, 0, 0, 0)),
        out_shape=jax.ShapeDtypeStruct((N_GROUPS, H, BLK, 2 * BLK), F32),
        compiler_params=_params(("parallel",)), name=name,
    )(rel_table, bmap)


def _band_bias_bwd(dbias, bmap, H, *, name):
    def body(d_ref, m_ref, o_ref):
        bm = m_ref[0]
        rowi = lax.broadcasted_iota(jnp.int32, (REL_BUCKETS, LANES), 0)
        lane = lax.broadcasted_iota(jnp.int32, (REL_BUCKETS, LANES), 1)
        out = jnp.zeros((REL_BUCKETS, LANES), F32)
        for h in range(H):
            dv = d_ref[0, h]
            for b in range(REL_BUCKETS):
                s = jnp.sum(jnp.sum(jnp.where(bm == b, dv, 0.0), axis=1, keepdims=True),
                            axis=0, keepdims=True)
                out = jnp.where((rowi == b) & (lane == h), s, out)
        o_ref[0] = out

    return pl.pallas_call(
        body, grid=(N_GROUPS,),
        in_specs=[pl.BlockSpec((1, H, BLK, 2 * BLK), lambda g: (g, 0, 0, 0)),
                  pl.BlockSpec((1, BLK, 2 * BLK), lambda g: (g, 0, 0))],
        out_specs=pl.BlockSpec((1, REL_BUCKETS, LANES), lambda g: (g, 0, 0)),
        out_shape=jax.ShapeDtypeStruct((N_GROUPS, REL_BUCKETS, LANES), F32),
        compiler_params=_params(("parallel",)), name=name,
    )(dbias, bmap)


def _head_masks():
    lane = lax.broadcasted_iota(jnp.int32, (BLK, LANES), 1)
    return (lane < HEAD_DIM, lane >= HEAD_DIM)


def _attn_fwd(q, kv, bias, gi, *, name):
    T = q.shape[0]
    HD = kv.shape[1] // 2
    d = DILATED_GROUPS[gi][1]
    S = T // d
    NB = S // BLK
    H = HD // HEAD_DIM
    qv, qcol = (q, gi) if d == 1 else (q[:, gi * HD:(gi + 1) * HD].reshape(S, d * HD), 0)
    kvv = kv.reshape(S, d * 2 * HD)

    def body(q_ref, kp_ref, kc_ref, vp_ref, vc_ref, b_ref, o_ref, l_ref):
        n = pl.program_id(1)
        col = lax.broadcasted_iota(jnp.int32, (BLK, 2 * BLK), 1)
        first = (n == 0) & (col < BLK)
        hm = _head_masks()
        for p in range(HD // LANES):
            sl = slice(p * LANES, (p + 1) * LANES)
            qp = q_ref[:, sl]
            kc = jnp.concatenate([kp_ref[:, sl], kc_ref[:, sl]], axis=0)
            vc = jnp.concatenate([vp_ref[:, sl], vc_ref[:, sl]], axis=0)
            outs = []
            lses = []
            for hh in range(2):
                qm = jnp.where(hm[hh], qp, jnp.zeros_like(qp))
                s = lax.dot_general(qm, kc, (((1,), (1,)), ((), ())), preferred_element_type=F32)
                s = jnp.where(first, NEG, s + b_ref[2 * p + hh])
                m = jnp.max(s, axis=1, keepdims=True)
                e = jnp.exp(s - m)
                den = jnp.sum(e, axis=1, keepdims=True)
                outs.append(jnp.dot((e / den).astype(BF16), vc, preferred_element_type=F32))
                lses.append(m + jnp.log(den))
            o_ref[:, sl] = jnp.where(hm[0], outs[0], outs[1])
            l_ref[:, sl] = jnp.where(hm[0], lses[0], lses[1])

    blk = lambda f: pl.BlockSpec((BLK, HD), f)
    prev = lambda n: jnp.maximum(n - 1, 0)
    return pl.pallas_call(
        body, grid=(d, NB),
        in_specs=[blk(lambda r, n: (n, r + qcol)),
                  blk(lambda r, n: (prev(n), r * 2)), blk(lambda r, n: (n, r * 2)),
                  blk(lambda r, n: (prev(n), r * 2 + 1)), blk(lambda r, n: (n, r * 2 + 1)),
                  pl.BlockSpec((H, BLK, 2 * BLK), lambda r, n: (0, 0, 0))],
        out_specs=[blk(lambda r, n: (n, r)), blk(lambda r, n: (n, r))],
        out_shape=[jax.ShapeDtypeStruct((S, d * HD), F32), jax.ShapeDtypeStruct((S, d * HD), F32)],
        compiler_params=_params(("parallel", "parallel")), name=name,
    )(qv, kvv, kvv, kvv, kvv, bias)


def _attn_combine(os, ls, *, name):
    T, HD = os[0].shape
    rb = _pick(T, 512)

    def body(o0, o1, o2, l0, l1, l2, o_ref, ob_ref, l_ref):
        la, lb, lc = l0[...], l1[...], l2[...]
        m = jnp.maximum(jnp.maximum(la, lb), lc)
        L = m + jnp.log(jnp.exp(la - m) + jnp.exp(lb - m) + jnp.exp(lc - m))
        o = jnp.exp(la - L) * o0[...] + jnp.exp(lb - L) * o1[...] + jnp.exp(lc - L) * o2[...]
        o_ref[...] = o
        ob_ref[...] = o.astype(BF16)
        l_ref[...] = L

    row = pl.BlockSpec((rb, HD), lambda i: (i, 0))
    return pl.pallas_call(
        body, grid=(T // rb,), in_specs=[row] * 6, out_specs=[row] * 3,
        out_shape=[jax.ShapeDtypeStruct((T, HD), F32), jax.ShapeDtypeStruct((T, HD), BF16),
                   jax.ShapeDtypeStruct((T, HD), F32)],
        compiler_params=_params(("parallel",)), name=name,
    )(*[a.reshape(T, HD) for a in os], *[a.reshape(T, HD) for a in ls])


def _attn_bwd(q, kv, bias, do, o, L, gi, *, name):
    T = q.shape[0]
    HD = kv.shape[1] // 2
    d = DILATED_GROUPS[gi][1]
    S = T // d
    NB = S // BLK
    H = HD // HEAD_DIM
    qv, qcol = (q, gi) if d == 1 else (q[:, gi * HD:(gi + 1) * HD].reshape(S, d * HD), 0)
    kvv = kv.reshape(S, d * 2 * HD)
    dov, ov, Lv = (a.reshape(S, d * HD) for a in (do, o, L))

    def body(q_ref, kp_ref, kc_ref, vp_ref, vc_ref, b_ref, do_ref, o_ref, L_ref,
             dq_ref, dk_ref, dv_ref, db_ref, ck_ref, cv_ref):
        r = pl.program_id(0)
        n = pl.program_id(1)

        @pl.when((r == 0) & (n == 0))
        def _():
            db_ref[...] = jnp.zeros_like(db_ref)

        @pl.when(n == 0)
        def _():
            ck_ref[...] = jnp.zeros_like(ck_ref)
            cv_ref[...] = jnp.zeros_like(cv_ref)

        @pl.when(n < NB)
        def _():
            col = lax.broadcasted_iota(jnp.int32, (BLK, 2 * BLK), 1)
            first = (n == 0) & (col < BLK)
            hm = _head_masks()
            for p in range(HD // LANES):
                sl = slice(p * LANES, (p + 1) * LANES)
                qp = q_ref[:, sl]
                kc = jnp.concatenate([kp_ref[:, sl], kc_ref[:, sl]], axis=0)
                vc = jnp.concatenate([vp_ref[:, sl], vc_ref[:, sl]], axis=0)
                dop = do_ref[:, sl]
                dob = dop.astype(BF16)
                prod = dop * o_ref[:, sl]
                Lp = L_ref[:, sl]
                dq_parts = []
                dkc = None
                dvc = None
                for hh in range(2):
                    qm = jnp.where(hm[hh], qp, jnp.zeros_like(qp))
                    dom = jnp.where(hm[hh], dob, jnp.zeros_like(dob))
                    s = lax.dot_general(qm, kc, (((1,), (1,)), ((), ())), preferred_element_type=F32)
                    s = jnp.where(first, NEG, s + b_ref[2 * p + hh])
                    lse = Lp[:, hh * HEAD_DIM:hh * HEAD_DIM + 1]
                    pr = jnp.exp(s - lse)
                    dp = lax.dot_general(dom, vc, (((1,), (1,)), ((), ())), preferred_element_type=F32)
                    delta = jnp.sum(jnp.where(hm[hh], prod, 0.0), axis=1, keepdims=True)
                    ds = pr * (dp - delta)
                    db_ref[2 * p + hh] += ds
                    dsb = ds.astype(BF16)
                    dq_parts.append(jnp.dot(dsb, kc, preferred_element_type=F32))
                    dkh = lax.dot_general(dsb, qm, (((0,), (0,)), ((), ())), preferred_element_type=F32)
                    dvh = lax.dot_general(pr.astype(BF16), dom, (((0,), (0,)), ((), ())),
                                          preferred_element_type=F32)
                    dkc = dkh if dkc is None else dkc + dkh
                    dvc = dvh if dvc is None else dvc + dvh
                dq = jnp.where(hm[0], dq_parts[0], dq_parts[1])
                dq_ref[:, sl] = (dq * (HEAD_DIM ** -0.5)).astype(BF16)
                dk_ref[:, sl] = ck_ref[:, sl] + dkc[:BLK]
                dv_ref[:, sl] = cv_ref[:, sl] + dvc[:BLK]
                ck_ref[:, sl] = dkc[BLK:]
                cv_ref[:, sl] = dvc[BLK:]

        @pl.when(n == NB)
        def _():
            dk_ref[...] = ck_ref[...]
            dv_ref[...] = cv_ref[...]

    blk = lambda f: pl.BlockSpec((BLK, HD), f)
    cur = lambda n: jnp.minimum(n, NB - 1)
    prev = lambda n: jnp.maximum(jnp.minimum(n, NB - 1) - 1, 0)
    lag = lambda n: jnp.maximum(n - 1, 0)
    return pl.pallas_call(
        body, grid=(d, NB + 1),
        in_specs=[blk(lambda r, n: (cur(n), r + qcol)),
                  blk(lambda r, n: (prev(n), r * 2)), blk(lambda r, n: (cur(n), r * 2)),
                  blk(lambda r, n: (prev(n), r * 2 + 1)), blk(lambda r, n: (cur(n), r * 2 + 1)),
                  pl.BlockSpec((H, BLK, 2 * BLK), lambda r, n: (0, 0, 0)),
                  blk(lambda r, n: (cur(n), r)), blk(lambda r, n: (cur(n), r)),
                  blk(lambda r, n: (cur(n), r))],
        out_specs=[blk(lambda r, n: (cur(n), r)), blk(lambda r, n: (lag(n), r)),
                   blk(lambda r, n: (lag(n), r)),
                   pl.BlockSpec((H, BLK, 2 * BLK), lambda r, n: (0, 0, 0))],
        out_shape=[jax.ShapeDtypeStruct((S, d * HD), BF16), jax.ShapeDtypeStruct((S, d * HD), F32),
                   jax.ShapeDtypeStruct((S, d * HD), F32),
                   jax.ShapeDtypeStruct((H, BLK, 2 * BLK), F32)],
        scratch_shapes=[pltpu.VMEM((BLK, HD), F32), pltpu.VMEM((BLK, HD), F32)],
        compiler_params=_params(("arbitrary", "arbitrary")), name=name,
    )(qv, kvv, kvv, kvv, kvv, bias, dov, ov, Lv)


class _Weights(dict):
    def __init__(self, base, fetch=None, emit=None):
        super().__init__(base)
        self._fetch, self._emit = fetch, emit

    def fetch(self, group, after):
        if self._fetch is not None:
            for (key, layer), mat in self._fetch(group, after).items():
                self[key][layer] = mat

    def emit(self, group, mats):
        return None if self._emit is None else self._emit(group, mats)


def _local_step(x, tgt, W):
    T, D = x.shape
    H = W["rel_table"].shape[1] // N_GROUPS
    HD = H * HEAD_DIM
    G = W["a_w_s"].shape[1]
    assert T % (DILATED_GROUPS[-1][1] * BLK) == 0

    tril = jnp.tril(jnp.ones((CHUNK, CHUNK), F32))
    bmap = jnp.asarray(_bucket_maps())
    bias = _band_bias(W["rel_table"], bmap, H, name="band_bias")

    saved = []
    xc, xcb = x, x.astype(BF16)
    kvb = None
    for i in range(DEPTH):
        s = {"x": xc, "xb": xcb}
        W.fetch(4 * i, xc)
        if i < N_A:
            ws_m = W["a_w_s"][i] * tril
            s["ws"] = ws_m.astype(BF16)
            s["wst"] = jnp.swapaxes(ws_m, 1, 2).astype(BF16)
            s["bst"] = W["a_b_s"][i].T
            s["zp"] = _mm(xcb, W["a_w_in"][i], out_dtype=ACT, name=f"a_in_{i}")
            s["y"] = _sgu_fwd(s["zp"], W["a_ln_g"][i], W["a_ln_b"][i], s["ws"], s["bst"], name=f"sgu_fwd_{i}")
            W.fetch(4 * i + 1, s["zp"])
            s["h"] = _mm(s["y"], W["a_w_out"][i], name=f"a_out_{i}")
        else:
            j = i - N_A
            if kvb is None:
                kvb = _mm(xcb, W["kv_w"][0], out_dtype=BF16, name="kv_proj")
            s["q"] = _mm(xcb, W["b_w_q_t"][j], tb=True, out_dtype=BF16, scale=HEAD_DIM ** -0.5,
                         name=f"q_proj_{j}")
            os, ls = [], []
            for gi in range(N_GROUPS):
                o_g, l_g = _attn_fwd(s["q"], kvb, bias[gi], gi, name=f"attn_fwd_{j}_{gi}")
                os.append(o_g)
                ls.append(l_g)
            s["o"], s["ob"], s["L"] = _attn_combine(os, ls, name=f"attn_mix_{j}")
            W.fetch(4 * i + 1, s["q"])
            s["h"] = _mm(s["ob"], W["b_w_o"][j], name=f"o_proj_{j}")
        s["x1"], s["x1b"] = _add_ln_fwd(xc, s["h"], W["ln_g"][i, 0], W["ln_b"][i, 0], name=f"ln1_fwd_{i}")
        W.fetch(4 * i + 2, s["x1"])
        s["hh"] = _mm(s["x1b"], W["ffn_w_up_t"][i], tb=True, out_dtype=ACT, name=f"ffn_up_{i}")
        s["cw"] = W["ffn_conv_w"][i]
        s["cb"] = W["ffn_conv_b"][i].reshape(1, -1)
        s["act"] = _convgate_fwd(s["hh"], s["cw"], s["cb"], name=f"convgate_fwd_{i}")
        W.fetch(4 * i + 3, s["hh"])
        s["f"] = _mm(s["act"], W["ffn_w_down"][i], name=f"ffn_down_{i}")
        xc, xcb = _add_ln_fwd(s["x1"], s["f"], W["ln_g"][i, 1], W["ln_b"][i, 1], name=f"ln2_fwd_{i}")
        saved.append(s)

    dy, lossv = _loss_grad(xc, tgt, name="loss_grad")
    loss = lossv[0, 0]

    gl = {k: [None] * DEPTH for k in ("ffn_w_up_t", "ffn_conv_w", "ffn_conv_b", "ffn_w_down", "ln_g", "ln_b")}
    ga = {k: [None] * N_A for k in ("a_w_in", "a_ln_g", "a_ln_b", "a_w_s", "a_b_s", "a_w_out")}
    gb = {k: [None] * (DEPTH - N_A) for k in ("b_w_q_t", "b_w_o")}
    mats = ("a_w_in", "a_w_out", "b_w_q_t", "b_w_o", "ffn_w_up_t", "ffn_w_down")
    dks, dvs, dbias = [], [], []
    grads = {}
    terms = [(1.0, dy)]
    tok = None
    behind = lambda g: g if tok is None else g + tok
    for i in reversed(range(DEPTH)):
        s = saved[i]
        dp2, dp2b, dg2, db2 = _add_ln_bwd(s["x1"], s["f"], behind(W["ln_g"][i, 1]), terms, name=f"ln2_bwd_{i}")
        dact = _mm(dp2b, W["ffn_w_down"][i], tb=True, out_dtype=ACT, name=f"ffn_down_dx_{i}")
        gl["ffn_w_down"][i] = _mm(s["act"], dp2b, ta=True, out_dtype=BF16, name=f"ffn_down_dw_{i}")
        dha, dhg, dwa, dwg, dba, dbg = _convgate_bwd(s["hh"], dact, s["cw"], s["cb"], name=f"convgate_bwd_{i}")
        dhh = (dha, dhg)
        gl["ffn_conv_w"][i] = jnp.concatenate([dwa, dwg], axis=1)
        gl["ffn_conv_b"][i] = jnp.concatenate([dba, dbg], axis=1)[0]
        dx1 = _mm(dhh, W["ffn_w_up_t"][i], name=f"ffn_up_dx_{i}")
        gl["ffn_w_up_t"][i] = _mm(dhh, s["x1b"], ta=True, out_dtype=BF16, name=f"ffn_up_dw_{i}")
        tok = W.emit(3 * i + 2, {("ffn_w_up_t", i): gl["ffn_w_up_t"][i], ("ffn_w_down", i): gl["ffn_w_down"][i]})
        dp1, dp1b, dg1, db1 = _add_ln_bwd(s["x"], s["h"], behind(W["ln_g"][i, 0]), [(ALPHA, dp2), (1.0, dx1)],
                                          name=f"ln1_bwd_{i}")
        gl["ln_g"][i] = jnp.concatenate([dg1, dg2], axis=0)
        gl["ln_b"][i] = jnp.concatenate([db1, db2], axis=0)
        terms = [(ALPHA, dp1)]
        if i < N_A:
            dyy = _mm(dp1b, W["a_w_out"][i], tb=True, out_dtype=ACT, name=f"a_out_dx_{i}")
            ga["a_w_out"][i] = _mm(s["y"], dp1b, ta=True, out_dtype=BF16, name=f"a_out_dw_{i}")
            tok = W.emit(3 * i + 1, {("a_w_out", i): ga["a_w_out"][i]})
            dzp, dlg, dlb, dws, dbs = _sgu_bwd(s["zp"], dyy, behind(W["a_ln_g"][i]), W["a_ln_b"][i], s["ws"],
                                               s["wst"], s["bst"], name=f"sgu_bwd_{i}")
            ga["a_ln_g"][i], ga["a_ln_b"][i], ga["a_w_s"][i] = dlg[0], dlb[0], dws
            ga["a_b_s"][i] = dbs[:, :G].T
            terms.append((1.0, _mm(dzp, W["a_w_in"][i], tb=True, name=f"a_in_dx_{i}")))
            ga["a_w_in"][i] = _mm(s["xb"], dzp, ta=True, out_dtype=BF16, name=f"a_in_dw_{i}")
            tok = W.emit(3 * i, {("a_w_in", i): ga["a_w_in"][i]})
        else:
            j = i - N_A
            do = _mm(dp1b, W["b_w_o"][j], tb=True, name=f"o_proj_dx_{j}")
            gb["b_w_o"][j] = _mm(s["ob"], dp1b, ta=True, out_dtype=BF16, name=f"o_proj_dw_{j}")
            tok = W.emit(3 * i + 1, {("b_w_o", j): gb["b_w_o"][j]})
            bias_b = behind(bias)
            dqs, dbl = [], []
            for gi in range(N_GROUPS):
                dq_g, dk_g, dv_g, db_g = _attn_bwd(s["q"], kvb, bias_b[gi], do, s["o"], s["L"], gi,
                                                   name=f"attn_bwd_{j}_{gi}")
                dqs.append(dq_g.reshape(T, HD))
                dks.append((1.0, dk_g.reshape(T, HD)))
                dvs.append((1.0, dv_g.reshape(T, HD)))
                dbl.append(db_g)
            dbias.append(jnp.stack(dbl))
            dq = jnp.concatenate(dqs, axis=1)
            terms.append((1.0, _mm(dq, W["b_w_q_t"][j], name=f"q_proj_dx_{j}")))
            gb["b_w_q_t"][j] = _mm(dq, s["xb"], ta=True, out_dtype=BF16, name=f"q_proj_dw_{j}")
            out_b = {("b_w_q_t", j): gb["b_w_q_t"][j]}
            if i == N_A:
                dkv = jnp.concatenate([_lincomb(dks, BF16, name="dk_sum"), _lincomb(dvs, BF16, name="dv_sum")],
                                      axis=1)
                terms.append((1.0, _mm(dkv, W["kv_w"][0], tb=True, name="kv_proj_dx")))
                grads["kv_w"] = [_mm(s["xb"], dkv, ta=True, out_dtype=BF16, name="kv_proj_dw")]
                out_b[("kv_w", 0)] = grads["kv_w"][0]
                dbt = _lincomb([(1.0, a.reshape(-1, 2 * BLK)) for a in dbias], F32, name="dbias_sum")
                dtab = _band_bias_bwd(dbt.reshape(N_GROUPS, H, BLK, 2 * BLK), bmap, H, name="band_bias_bwd")
                grads["rel_table"] = jnp.transpose(dtab[:, :, :H], (1, 0, 2)).reshape(REL_BUCKETS, N_GROUPS * H)
            tok = W.emit(3 * i, out_b)
    grad_x = _lincomb(terms, F32, name="grad_x")
    for dct in (gl, ga, gb):
        for k, v in dct.items():
            grads[k] = v if k in mats else jnp.stack(v)
    return loss, grad_x, grads


def _my_index():
    return 4 * lax.axis_index("x") + 2 * lax.axis_index("y") + lax.axis_index("c")


HBM_SPEC = pl.BlockSpec(memory_space=pltpu.HBM)


def _block(ref, k, n, axis):
    off = pl.multiple_of(k * n, n)
    return ref.at[pl.ds(off, n), :] if axis == 0 else ref.at[:, pl.ds(off, n)]


def _gather_mats(local, axis, *, name):
    L, a, b = local.shape
    n = a if axis == 0 else b
    full = (a * N_DEV, b) if axis == 0 else (a, b * N_DEV)

    def body(x_ref, *rest):
        outs = rest[:L]
        send_sems, recv_sems, local_sems = rest[L:]
        x, y, c = lax.axis_index("x"), lax.axis_index("y"), lax.axis_index("c")
        me, sibling = (x, y, c), (x, y, 1 - c)
        chips = [(1 - x, y), (x, 1 - y), (1 - x, 1 - y)]

        def slot(l, px, py, pc):
            return _block(outs[l], 4 * px + 2 * py + pc, n, axis)

        def copy(l, k, blk, to, src=None):
            return pltpu.make_async_remote_copy(
                src_ref=slot(l, *blk) if src is None else src, dst_ref=slot(l, *blk),
                send_sem=send_sems.at[7 * l + k], recv_sem=recv_sems.at[7 * l + k],
                device_id=to, device_id_type=MESH)

        mine, first, passed = [], [], []
        for l in range(L):
            mine.append(pltpu.make_async_copy(x_ref.at[l], slot(l, *me), local_sems.at[l]))
            mine[-1].start()
            first.append(copy(l, 0, me, sibling, src=x_ref.at[l]))
            first += [copy(l, 1 + j, me, (*chip, c), src=x_ref.at[l]) for j, chip in enumerate(chips)]
        for cp in first:
            cp.start()
        for l in range(L):
            for j, chip in enumerate(chips):
                copy(l, 1 + j, (*chip, c), me).wait_recv()
                passed.append(copy(l, 4 + j, (*chip, c), sibling))
                passed[-1].start()
        for l in range(L):
            copy(l, 0, sibling, me).wait_recv()
            for j, chip in enumerate(chips):
                copy(l, 4 + j, (*chip, 1 - c), me).wait_recv()
        for cp in first + passed:
            cp.wait_send()
        for cp in mine:
            cp.wait()

    return pl.pallas_call(
        body, out_shape=[jax.ShapeDtypeStruct(full, local.dtype)] * L,
        in_specs=[HBM_SPEC], out_specs=[HBM_SPEC] * L,
        scratch_shapes=[pltpu.SemaphoreType.DMA((7 * L,)), pltpu.SemaphoreType.DMA((7 * L,)),
                        pltpu.SemaphoreType.DMA((L,))],
        name=name,
    )(local)


SEM_SPEC = pl.BlockSpec(memory_space=pltpu.SEMAPHORE)
FLOWING = pltpu.SideEffectType.DATAFLOW_SIDE_EFFECTING


def _peers(x, y, c):
    return [(1 - x if k & 4 else x, 1 - y if k & 2 else y, 1 - c if k & 1 else c) for k in range(1, N_DEV)]


def _ends(src_ref, land_ref, peer_index, me, n, axis, gather):
    if gather:
        return src_ref, _block(land_ref, me, n, axis)
    return _block(src_ref, peer_index, n, axis), land_ref.at[me]


def _send_start(groups, gather, *, name):
    flat = [(g, j, mat, axis) for g, items in enumerate(groups) for j, (mat, axis) in enumerate(items)]
    M, G = len(flat), len(groups)
    lands, ns = [], []
    for _, _, mat, axis in flat:
        A, B = mat.shape
        if gather:
            lands.append((A * N_DEV, B) if axis == 0 else (A, B * N_DEV))
            ns.append(A if axis == 0 else B)
        else:
            lands.append((N_DEV, A // N_DEV, B) if axis == 0 else (N_DEV, A, B // N_DEV))
            ns.append(A // N_DEV if axis == 0 else B // N_DEV)

    def body(*refs):
        src_refs, land_refs, sems = refs[:M], refs[M:2 * M], refs[2 * M:2 * M + 3 * G]
        token = refs[-1]
        x, y, c = lax.axis_index("x"), lax.axis_index("y"), lax.axis_index("c")
        me = 4 * x + 2 * y + c
        for i, (g, j, _, axis) in enumerate(flat):
            for k, (px, py, pc) in enumerate(_peers(x, y, c)):
                s, d = _ends(src_refs[i], land_refs[i], 4 * px + 2 * py + pc, me, ns[i], axis, gather)
                pltpu.make_async_remote_copy(
                    src_ref=s, dst_ref=d, send_sem=sems[3 * g].at[7 * j + k], recv_sem=sems[3 * g + 1].at[7 * j + k],
                    device_id=(px, py, pc), device_id_type=MESH).start()
            s, d = _ends(src_refs[i], land_refs[i], me, me, ns[i], axis, gather)
            pltpu.make_async_copy(s, d, sems[3 * g + 2].at[j]).start()
        token[...] = jnp.zeros_like(token)

    sem_shapes = []
    for items in groups:
        sem_shapes += [pltpu.SemaphoreType.DMA((7 * len(items),))] * 2 + [pltpu.SemaphoreType.DMA((len(items),))]
    outs = pl.pallas_call(
        body, name=name,
        out_shape=(*sem_shapes, *[pltpu.HBM(m.shape, m.dtype) for _, _, m, _ in flat],
                   *[pltpu.HBM(shp, m.dtype) for shp, (_, _, m, _) in zip(lands, flat)],
                   jax.ShapeDtypeStruct((8, LANES), F32)),
        in_specs=[HBM_SPEC] * (2 * M),
        out_specs=(*[SEM_SPEC] * (3 * G), *[HBM_SPEC] * (2 * M), pl.BlockSpec(memory_space=pltpu.VMEM)),
        input_output_aliases={i: 3 * G + i for i in range(2 * M)},
        compiler_params=pltpu.CompilerParams(has_side_effects=FLOWING),
    )(*[pltpu.with_memory_space_constraint(m, pltpu.HBM) for _, _, m, _ in flat],
      *[pltpu.with_memory_space_constraint(lax.empty(shp, m.dtype), pltpu.HBM)
        for shp, (_, _, m, _) in zip(lands, flat)])
    handles = []
    for g in range(G):
        idx = [i for i, f in enumerate(flat) if f[0] == g]
        handles.append((outs[3 * g], outs[3 * g + 1], outs[3 * g + 2], [outs[3 * G + i] for i in idx],
                        [outs[3 * G + M + i] for i in idx], [flat[i][3] for i in idx]))
    return handles, outs[-1]


def _send_wait(handle, gather, after, *, name):
    send_sems, recv_sems, local_sems, mats, lands, axes = handle
    n_m = len(mats)
    ns = []
    for mat, land, axis in zip(mats, lands, axes):
        ns.append(mat.shape[axis] if gather else land.shape[1 + axis])

    def body(*refs):
        src_refs, land_refs = refs[:n_m], refs[n_m:2 * n_m]
        ssem, rsem, lsem = refs[2 * n_m:2 * n_m + 3]
        x, y, c = lax.axis_index("x"), lax.axis_index("y"), lax.axis_index("c")
        me = 4 * x + 2 * y + c
        for j in range(n_m):
            for k, (px, py, pc) in enumerate(_peers(x, y, c)):
                s, d = _ends(src_refs[j], land_refs[j], 4 * px + 2 * py + pc, me, ns[j], axes[j], gather)
                cp = pltpu.make_async_remote_copy(
                    src_ref=s, dst_ref=d, send_sem=ssem.at[7 * j + k], recv_sem=rsem.at[7 * j + k],
                    device_id=(px, py, pc), device_id_type=MESH)
                cp.wait_send()
                cp.wait_recv()
            s, d = _ends(src_refs[j], land_refs[j], me, me, ns[j], axes[j], gather)
            pltpu.make_async_copy(s, d, lsem.at[j]).wait()

    outs = pl.pallas_call(
        body, name=name,
        out_shape=(*[pltpu.HBM(m.shape, m.dtype) for m in mats], *[pltpu.HBM(l.shape, l.dtype) for l in lands]),
        in_specs=[HBM_SPEC] * (2 * n_m) + [SEM_SPEC] * 3 + [pl.BlockSpec(memory_space=pl.ANY)],
        out_specs=tuple([HBM_SPEC] * (2 * n_m)),
        input_output_aliases={i: i for i in range(2 * n_m)},
        compiler_params=pltpu.CompilerParams(has_side_effects=FLOWING),
    )(*mats, *lands, send_sems, recv_sems, local_sems, after)
    return list(outs[n_m:])


def _sum_parts(parts, *, name):
    n, R, C = parts.shape
    rb = _pick(R, 512) if R % LANES == 0 else R

    def body(p_ref, o_ref):
        acc = p_ref[0].astype(F32)
        for k in range(1, n):
            acc = acc + p_ref[k].astype(F32)
        o_ref[...] = acc

    return pl.pallas_call(
        body, grid=(R // rb,), in_specs=[pl.BlockSpec((n, rb, C), lambda i: (0, i, 0))],
        out_specs=pl.BlockSpec((rb, C), lambda i: (i, 0)),
        out_shape=jax.ShapeDtypeStruct((R, C), F32),
        compiler_params=_params(("parallel",)), name=name,
    )(parts)


def _adamw(w, m, v, parts, *, name):
    L, R, C = w.shape
    n = parts[0].shape[0]
    cap = max(16, VMEM_LIMIT // 3 // (2 * L * n * C * parts[0].dtype.itemsize))
    rb = max([r for r in range(16, min(R, cap) + 1, 16) if R % r == 0], default=R)

    def body(w_ref, m_ref, v_ref, *rest):
        p_refs = rest[:L]
        g_ref, d_ref, nm_ref, nv_ref = rest[L:]
        for l in range(L):
            @pl.when(pl.program_id(0) == l)
            def _(p_ref=p_refs[l]):
                g = p_ref[0].astype(F32)
                for k in range(1, n):
                    g = g + p_ref[k].astype(F32)
                mn = ADAM_B1 * m_ref[...] + (1.0 - ADAM_B1) * g
                vn = ADAM_B2 * v_ref[...] + (1.0 - ADAM_B2) * jnp.square(g)
                m_hat = mn / (1.0 - ADAM_B1 ** ADAM_STEP)
                v_hat = vn / (1.0 - ADAM_B2 ** ADAM_STEP)
                g_ref[...] = g
                d_ref[...] = -ADAM_LR * (m_hat / (jnp.sqrt(v_hat) + ADAM_EPS) + ADAM_WD * w_ref[...])
                nm_ref[...] = mn
                nv_ref[...] = vn

    row = pl.BlockSpec((None, rb, C), lambda l, i: (l, i, 0))
    part = lambda k: pl.BlockSpec((n, rb, C), lambda l, i: (0, jnp.where(l == k, i, 0), 0))
    return pl.pallas_call(
        body, grid=(L, R // rb), in_specs=[row, row, row] + [part(k) for k in range(L)],
        out_specs=[row] * 4, out_shape=[jax.ShapeDtypeStruct((L, R, C), F32)] * 4,
        compiler_params=_params(("arbitrary", "arbitrary")), name=name,
    )(w, m, v, *parts)


BIG = (("a_w_in", "a_w_in", 1, False), ("a_w_out", "a_w_out", 0, False), ("kv_w", "kv_w", 0, False),
       ("b_w_q", "b_w_q_t", 0, True), ("b_w_o", "b_w_o", 1, False), ("ffn_w_up", "ffn_w_up_t", 0, True),
       ("ffn_w_down", "ffn_w_down", 0, False))
SMALL_SHARDED = (("a_ln_g", 1), ("a_ln_b", 1), ("ffn_conv_w", 2), ("ln_g", 2), ("ln_b", 2))
REPLICATED = ("a_w_s", "a_b_s", "rel_table", "ffn_conv_b")


def _pack_rows(arrs, lead=0):
    lshape = arrs[0].shape[:lead]
    p = jnp.concatenate([a.reshape(*lshape, -1, LANES) for a in arrs], axis=lead)
    pad = -p.shape[lead] % 8
    return jnp.pad(p, [(0, 0)] * lead + [(0, pad), (0, 0)])


def _unpack_rows(packed, shapes, lead=0):
    lshape = packed.shape[:lead]
    out, off = [], 0
    for shp in shapes:
        r = int(np.prod(shp)) // LANES
        out.append(lax.slice_in_dim(packed, off, off + r, axis=lead).reshape(*lshape, *shp))
        off += r
    return out


def _as_mats(a, transposed):
    a = a[None] if a.ndim == 2 else a
    return jnp.swapaxes(a, 1, 2) if transposed else a


def _merge_shards(stacked, axis):
    a = jnp.moveaxis(stacked, 0, axis)
    shp = list(a.shape)
    return a.reshape(shp[:axis] + [shp[axis] * shp[axis + 1]] + shp[axis + 2:])


def _split_shards(full, axis):
    shp = list(full.shape)
    a = full.reshape(shp[:axis] + [N_DEV, shp[axis] // N_DEV] + shp[axis + 1:])
    return jnp.moveaxis(a, axis, 0)


def kernel(x, a_w_in, a_ln_g, a_ln_b, a_w_s, a_b_s, a_w_out, kv_w, b_w_q, b_w_o, rel_table, ffn_w_up, ffn_conv_w, ffn_conv_b, ffn_w_down, ln_g, ln_b, loss_target, m_a_w_in, m_a_ln_g, m_a_ln_b, m_a_w_s, m_a_b_s, m_a_w_out, m_kv_w, m_b_w_q, m_b_w_o, m_rel_table, m_ffn_w_up, m_ffn_conv_w, m_ffn_conv_b, m_ffn_w_down, m_ln_g, m_ln_b, v_a_w_in, v_a_ln_g, v_a_ln_b, v_a_w_s, v_a_b_s, v_a_w_out, v_kv_w, v_b_w_q, v_b_w_o, v_rel_table, v_ffn_w_up, v_ffn_conv_w, v_ffn_conv_b, v_ffn_w_down, v_ln_g, v_ln_b):
    names = ["a_w_in", "a_ln_g", "a_ln_b", "a_w_s", "a_b_s", "a_w_out", "kv_w", "b_w_q", "b_w_o", "rel_table",
             "ffn_w_up", "ffn_conv_w", "ffn_conv_b", "ffn_w_down", "ln_g", "ln_b"]
    w = dict(zip(names, (a_w_in, a_ln_g, a_ln_b, a_w_s, a_b_s, a_w_out, kv_w, b_w_q, b_w_o, rel_table,
                         ffn_w_up, ffn_conv_w, ffn_conv_b, ffn_w_down, ln_g, ln_b)))
    m = dict(zip(names, (m_a_w_in, m_a_ln_g, m_a_ln_b, m_a_w_s, m_a_b_s, m_a_w_out, m_kv_w, m_b_w_q, m_b_w_o,
                         m_rel_table, m_ffn_w_up, m_ffn_conv_w, m_ffn_conv_b, m_ffn_w_down, m_ln_g, m_ln_b)))
    v = dict(zip(names, (v_a_w_in, v_a_ln_g, v_a_ln_b, v_a_w_s, v_a_b_s, v_a_w_out, v_kv_w, v_b_w_q, v_b_w_o,
                         v_rel_table, v_ffn_w_up, v_ffn_conv_w, v_ffn_conv_b, v_ffn_w_down, v_ln_g, v_ln_b)))
    small_names = [n for n, _ in SMALL_SHARDED]
    small_shapes = [w[n].shape for n in small_names]
    rep_shapes = [w[n].shape for n in REPLICATED]

    axis_of = {key: axis for _, key, axis, _ in BIG}
    src = {}
    for n, key, axis, tr in BIG:
        loc = _as_mats(w[n], tr).astype(BF16)
        for l in range(loc.shape[0]):
            src[(key, l)] = loc[l]
    order = []
    for i in range(DEPTH):
        if i < N_A:
            order += [[("a_w_in", i)], [("a_w_out", i)]]
        else:
            order += [([("kv_w", 0)] if i == N_A else []) + [("b_w_q_t", i - N_A)], [("b_w_o", i - N_A)]]
        order += [[("ffn_w_up_t", i)], [("ffn_w_down", i)]]
    small_src = _pack_rows([w[n] for n in small_names])
    srows = small_src.shape[0]
    handles, _ = _send_start([[(small_src, 0)]] + [[(src[kl], axis_of[kl[0]]) for kl in grp] for grp in order],
                             True, name="gather_start")
    small_all = _send_wait(handles[0], True, x, name="gather_wait_small")[0]
    small_st = _unpack_rows(small_all.reshape(N_DEV, srows, LANES), small_shapes, lead=1)
    base = {n: w[n] for n in REPLICATED}
    for (n, ax), st in zip(SMALL_SHARDED, small_st):
        base[n] = _merge_shards(st, ax)
    for n, key, _, tr in BIG:
        base[key] = [None] * (1 if w[n].ndim == 2 else w[n].shape[0])

    def fetch(group, after):
        mats = _send_wait(handles[1 + group], True, after, name=f"gather_wait_{group}")
        return dict(zip(order[group], mats))

    sent = {}

    def emit(group, mats):
        keys = list(mats)
        hs, token = _send_start([[(mats[kl], axis_of[kl[0]]) for kl in keys]], False, name=f"exchange_start_{group}")
        sent[group] = (keys, hs[0])
        return token[0, 0]

    loss, grad_x, grads = _local_step(x[0], loss_target[0], _Weights(base, fetch, emit))
    loss = lax.psum(loss, ("x", "y", "c"))

    out = {}
    small_pack = _pack_rows([_split_shards(grads[n], ax) for n, ax in SMALL_SHARDED], lead=1)
    rep_pack = _pack_rows([grads[n] for n in REPLICATED])
    mine = jnp.concatenate([small_pack.reshape(N_DEV * srows, LANES), rep_pack], axis=0)
    hs, _ = _send_start([[(mine, 0)]], True, name="small_grads_start")

    landed = {}
    for group in sorted(sent, reverse=True):
        keys, h = sent[group]
        landed.update(zip(keys, _send_wait(h, False, grad_x, name=f"exchange_wait_{group}")))
    last = grad_x
    for n, key, axis, tr in BIG:
        shp = w[n].shape
        parts = [landed[(key, l)] for l in range(1 if len(shp) == 2 else shp[0])]
        res = _adamw(_as_mats(w[n], tr), _as_mats(m[n], tr), _as_mats(v[n], tr), parts, name=f"adamw_{n}")
        out[n] = [(jnp.swapaxes(r, 1, 2) if tr else r).reshape(shp) for r in res]
        last = res[0]

    allp = _send_wait(hs[0], True, last, name="small_grads_wait")[0]
    gsum = _sum_parts(allp.reshape(N_DEV, mine.shape[0], LANES), name="sum_small_grads")
    g_small = lax.dynamic_slice_in_dim(gsum, _my_index() * srows, srows, axis=0)
    gs_in = jnp.concatenate([g_small, gsum[N_DEV * srows:]], axis=0)[None]
    pack_sr = lambda d: jnp.concatenate([_pack_rows([d[n] for n in small_names]),
                                         _pack_rows([d[n] for n in REPLICATED])], axis=0)
    res = _adamw(pack_sr(w)[None], pack_sr(m)[None], pack_sr(v)[None], [gs_in], name="adamw_small")
    for n, vals in zip(small_names, zip(*[_unpack_rows(r[0, :srows], small_shapes) for r in res])):
        out[n] = list(vals)
    for n, vals in zip(REPLICATED, zip(*[_unpack_rows(r[0, srows:], rep_shapes) for r in res])):
        out[n] = list(vals)

    return (loss, grad_x[None], *[out[n][0] for n in names], *[out[n][1] for n in names],
            *[out[n][2] for n in names], *[out[n][3] for n in names])
```

```python
import math

import numpy as np
import jax
import jax.numpy as jnp
from jax import lax
from jax.experimental import pallas as pl
from jax.experimental.pallas import tpu as pltpu

F32 = jnp.float32
BF16 = jnp.bfloat16
ACT = jnp.bfloat16
MESH = pl.DeviceIdType.MESH

N_DEV = 8
DEPTH = 4
N_A = 2
CHUNK = 128
BLK = 128
HEAD_DIM = 64
DILATED_GROUPS = ((128, 1), (512, 4), (2048, 16))
N_GROUPS = 3
REL_BUCKETS = 32
REL_MAX_DIST = 2048
ALPHA = (2 * DEPTH) ** 0.25
LN_EPS = 1e-5
NEG = -1e30
ADAM_LR = 0.001
ADAM_B1 = 0.9
ADAM_B2 = 0.999
ADAM_EPS = 1e-08
ADAM_WD = 0.01
ADAM_STEP = 10

LANES = 128
VMEM_LIMIT = 56 * 1024 * 1024
MM_TILE_CAP = 1408
INV_SQRT2 = 1.0 / math.sqrt(2.0)
INV_SQRT_2PI = 1.0 / math.sqrt(2.0 * math.pi)


def _pick(n, cap):
    best = None
    for t in range(LANES, min(n, cap) + 1, LANES):
        if n % t == 0:
            best = t
    return best if best is not None else n


def _params(sem):
    return pltpu.CompilerParams(dimension_semantics=sem, vmem_limit_bytes=VMEM_LIMIT)


def _gelu(x):
    return 0.5 * x * (1.0 + lax.erf(x * INV_SQRT2))


def _gelu_grad(x):
    return 0.5 * (1.0 + lax.erf(x * INV_SQRT2)) + x * jnp.exp(-0.5 * x * x) * INV_SQRT_2PI


def _mm(a, b, *, ta=False, tb=False, out_dtype=F32, scale=None, name):
    halves = isinstance(a, tuple)
    parts = 1 if halves or a.ndim == 2 else a.shape[0]
    ash = (a[0].shape[0], 2 * a[0].shape[1]) if halves else (a.shape if parts == 1
                                                               else (a.shape[1], parts * a.shape[2]))
    if ta:
        K, M = ash
    else:
        M, K = ash
    if tb:
        N, Kb = b.shape
    else:
        Kb, N = b.shape
    assert K == Kb, (ash, b.shape, ta, tb)
    tm, tn, tk = _pick(M, MM_TILE_CAP), _pick(N, MM_TILE_CAP), _pick(K, MM_TILE_CAP)
    split = 2 if halves else parts
    if split > 1 and ta:
        tm = _pick(M // split, MM_TILE_CAP)
    if split > 1 and not ta:
        tk = _pick(K // split, MM_TILE_CAP)
    nk = K // tk
    nh = (M // split // tm if ta else K // split // tk) if split > 1 else 0
    dn = (((0 if ta else 1,), (1 if tb else 0,)), ((), ()))

    def body(*refs):
        a_refs, (b_ref, o_ref, acc_ref) = refs[:-3], refs[-3:]
        k = pl.program_id(2)

        def accumulate(a_ref):
            part = lax.dot_general(a_ref[...].astype(BF16), b_ref[...].astype(BF16), dn,
                                   preferred_element_type=F32)

            @pl.when(k == 0)
            def _():
                acc_ref[...] = part

            @pl.when(k > 0)
            def _():
                acc_ref[...] += part

        if halves:
            first = (pl.program_id(0) if ta else k) < nh
            pl.when(first)(lambda: accumulate(a_refs[0]))
            pl.when(jnp.logical_not(first))(lambda: accumulate(a_refs[1]))
        else:
            accumulate(a_refs[0])

        @pl.when(k == nk - 1)
        def _():
            r = acc_ref[...]
            if scale is not None:
                r = r * scale
            o_ref[...] = r.astype(out_dtype)

    if halves and ta:
        a_specs = [pl.BlockSpec((tk, tm), lambda i, j, k: (jnp.where(i < nh, k, 0), jnp.minimum(i, nh - 1))),
                   pl.BlockSpec((tk, tm), lambda i, j, k: (jnp.where(i >= nh, k, 0), jnp.maximum(i - nh, 0)))]
    elif halves:
        a_specs = [pl.BlockSpec((tm, tk), lambda i, j, k: (i, jnp.minimum(k, nh - 1))),
                   pl.BlockSpec((tm, tk), lambda i, j, k: (i, jnp.maximum(k - nh, 0)))]
    elif parts > 1:
        a_specs = [pl.BlockSpec((None, tk, tm), lambda i, j, k: (i // nh, k, i % nh)) if ta
                   else pl.BlockSpec((None, tm, tk), lambda i, j, k: (k // nh, i, k % nh))]
    else:
        a_specs = [pl.BlockSpec((tk, tm), lambda i, j, k: (k, i)) if ta
                   else pl.BlockSpec((tm, tk), lambda i, j, k: (i, k))]
    b_spec = (pl.BlockSpec((tn, tk), lambda i, j, k: (j, k)) if tb
              else pl.BlockSpec((tk, tn), lambda i, j, k: (k, j)))
    return pl.pallas_call(
        body, grid=(M // tm, N // tn, nk), in_specs=[*a_specs, b_spec],
        out_specs=pl.BlockSpec((tm, tn), lambda i, j, k: (i, j)),
        out_shape=jax.ShapeDtypeStruct((M, N), out_dtype),
        scratch_shapes=[pltpu.VMEM((tm, tn), F32)],
        compiler_params=_params(("parallel", "parallel", "arbitrary")), name=name,
    )(*(a if halves else (a,)), b)


def _add_ln_fwd(x, h, g, b, *, name):
    T, D = x.shape
    rb = _pick(T, 512)

    def body(x_ref, h_ref, g_ref, b_ref, o_ref, ob_ref):
        pre = ALPHA * x_ref[...] + h_ref[...]
        mu = jnp.mean(pre, axis=1, keepdims=True)
        cen = pre - mu
        var = jnp.mean(cen * cen, axis=1, keepdims=True)
        y = cen * lax.rsqrt(var + LN_EPS) * g_ref[...] + b_ref[...]
        o_ref[...] = y
        ob_ref[...] = y.astype(BF16)

    row = pl.BlockSpec((rb, D), lambda i: (i, 0))
    vec = pl.BlockSpec((1, D), lambda i: (0, 0))
    return pl.pallas_call(
        body, grid=(T // rb,), in_specs=[row, row, vec, vec], out_specs=[row, row],
        out_shape=[jax.ShapeDtypeStruct((T, D), F32), jax.ShapeDtypeStruct((T, D), BF16)],
        compiler_params=_params(("parallel",)), name=name,
    )(x, h, g.reshape(1, D), b.reshape(1, D))


def _add_ln_bwd(x, h, g, terms, *, name):
    T, D = x.shape
    rb = _pick(T, 512)
    coefs = [c for c, _ in terms]
    nt = len(terms)

    def body(*refs):
        x_ref, h_ref, g_ref = refs[:3]
        t_refs = refs[3:3 + nt]
        dp_ref, dpb_ref, dg_ref, db_ref = refs[3 + nt:]
        dy = None
        for c, r in zip(coefs, t_refs):
            v = r[...] if c == 1.0 else c * r[...]
            dy = v if dy is None else dy + v
        pre = ALPHA * x_ref[...] + h_ref[...]
        mu = jnp.mean(pre, axis=1, keepdims=True)
        cen = pre - mu
        var = jnp.mean(cen * cen, axis=1, keepdims=True)
        rstd = lax.rsqrt(var + LN_EPS)
        xhat = cen * rstd
        dxh = dy * g_ref[...]
        m1 = jnp.mean(dxh, axis=1, keepdims=True)
        m2 = jnp.mean(dxh * xhat, axis=1, keepdims=True)
        dpre = rstd * (dxh - m1 - xhat * m2)
        dp_ref[...] = dpre
        dpb_ref[...] = dpre.astype(BF16)
        dg = jnp.sum(dy * xhat, axis=0, keepdims=True)
        db = jnp.sum(dy, axis=0, keepdims=True)

        @pl.when(pl.program_id(0) == 0)
        def _():
            dg_ref[...] = dg
            db_ref[...] = db

        @pl.when(pl.program_id(0) > 0)
        def _():
            dg_ref[...] += dg
            db_ref[...] += db

    row = pl.BlockSpec((rb, D), lambda i: (i, 0))
    vec = pl.BlockSpec((1, D), lambda i: (0, 0))
    return pl.pallas_call(
        body, grid=(T // rb,), in_specs=[row, row, vec] + [row] * nt,
        out_specs=[row, row, vec, vec],
        out_shape=[jax.ShapeDtypeStruct((T, D), F32), jax.ShapeDtypeStruct((T, D), BF16),
                   jax.ShapeDtypeStruct((1, D), F32), jax.ShapeDtypeStruct((1, D), F32)],
        compiler_params=_params(("arbitrary",)), name=name,
    )(x, h, g.reshape(1, D), *[a for _, a in terms])


def _lincomb(terms, out_dtype, *, name):
    R, C = terms[0][1].shape
    rb = _pick(R, 512)
    coefs = [c for c, _ in terms]
    nt = len(terms)

    def body(*refs):
        acc = None
        for c, r in zip(coefs, refs[:nt]):
            v = r[...].astype(F32)
            v = v if c == 1.0 else c * v
            acc = v if acc is None else acc + v
        refs[nt][...] = acc.astype(out_dtype)

    row = pl.BlockSpec((rb, C), lambda i: (i, 0))
    return pl.pallas_call(
        body, grid=(R // rb,), in_specs=[row] * nt, out_specs=row,
        out_shape=jax.ShapeDtypeStruct((R, C), out_dtype),
        compiler_params=_params(("parallel",)), name=name,
    )(*[a for _, a in terms])


def _loss_grad(y, tgt, *, name):
    T, D = y.shape
    rb = _pick(T, 512)

    def body(y_ref, t_ref, dy_ref, l_ref):
        err = y_ref[...] - t_ref[...]
        dy_ref[...] = err * (1.0 / D)
        part = jnp.sum(jnp.sum(err * err, axis=1, keepdims=True), axis=0, keepdims=True) * (0.5 / D)
        part = jnp.broadcast_to(part, (1, LANES))

        @pl.when(pl.program_id(0) == 0)
        def _():
            l_ref[...] = part

        @pl.when(pl.program_id(0) > 0)
        def _():
            l_ref[...] += part

    row = pl.BlockSpec((rb, D), lambda i: (i, 0))
    return pl.pallas_call(
        body, grid=(T // rb,), in_specs=[row, row],
        out_specs=[row, pl.BlockSpec((1, LANES), lambda i: (0, 0))],
        out_shape=[jax.ShapeDtypeStruct((T, D), F32), jax.ShapeDtypeStruct((1, LANES), F32)],
        compiler_params=_params(("arbitrary",)), name=name,
    )(y, tgt)


def _sgu_fwd(zp, ln_g, ln_b, ws, bst, *, name):
    T, E2 = zp.shape
    E = E2 // 2
    G = ws.shape[0]
    cg = E // G
    rb = 2 * CHUNK

    def body(z_ref, g_ref, b_ref, ws_ref, bs_ref, y_ref):
        u = _gelu(z_ref[:, :E].astype(F32))
        v = _gelu(z_ref[:, E:].astype(F32))
        mu = jnp.mean(v, axis=1, keepdims=True)
        cen = v - mu
        var = jnp.mean(cen * cen, axis=1, keepdims=True)
        vn = (cen * lax.rsqrt(var + LN_EPS) * g_ref[...] + b_ref[...]).astype(BF16)
        for ci in range(rb // CHUNK):
            rows = slice(ci * CHUNK, (ci + 1) * CHUNK)
            for gi in range(G):
                cols = slice(gi * cg, (gi + 1) * cg)
                sv = jnp.dot(ws_ref[gi], vn[rows, cols], preferred_element_type=F32)
                sv = sv + bs_ref[:, gi:gi + 1]
                y_ref[rows, cols] = (u[rows, cols] * sv).astype(BF16)

    return pl.pallas_call(
        body, grid=(T // rb,),
        in_specs=[pl.BlockSpec((rb, E2), lambda i: (i, 0)),
                  pl.BlockSpec((1, E), lambda i: (0, 0)), pl.BlockSpec((1, E), lambda i: (0, 0)),
                  pl.BlockSpec((G, CHUNK, CHUNK), lambda i: (0, 0, 0)),
                  pl.BlockSpec((CHUNK, G), lambda i: (0, 0))],
        out_specs=pl.BlockSpec((rb, E), lambda i: (i, 0)),
        out_shape=jax.ShapeDtypeStruct((T, E), BF16),
        compiler_params=_params(("parallel",)), name=name,
    )(zp, ln_g.reshape(1, E), ln_b.reshape(1, E), ws, bst)


def _sgu_bwd(zp, dy, ln_g, ln_b, ws, wst, bst, *, name):
    T, E2 = zp.shape
    E = E2 // 2
    G = ws.shape[0]
    cg = E // G
    rb = CHUNK
    nsteps = T // rb

    def body(z_ref, dy_ref, g_ref, b_ref, ws_ref, wst_ref, bs_ref,
             dz_ref, dg_ref, db_ref, dws_ref, dbs_ref, dsv_acc):
        step = pl.program_id(0)

        @pl.when(step == 0)
        def _():
            dg_ref[...] = jnp.zeros_like(dg_ref)
            db_ref[...] = jnp.zeros_like(db_ref)
            dws_ref[...] = jnp.zeros_like(dws_ref)
            dsv_acc[...] = jnp.zeros_like(dsv_acc)

        zu = z_ref[:, :E].astype(F32)
        zv = z_ref[:, E:].astype(F32)
        u = _gelu(zu)
        v = _gelu(zv)
        mu = jnp.mean(v, axis=1, keepdims=True)
        cen = v - mu
        var = jnp.mean(cen * cen, axis=1, keepdims=True)
        rstd = lax.rsqrt(var + LN_EPS)
        xhat = cen * rstd
        vn = (xhat * g_ref[...] + b_ref[...]).astype(BF16)
        dyv = dy_ref[...].astype(F32)
        dsv = dyv * u
        dsv_acc[...] += dsv
        dsvb = dsv.astype(BF16)
        tril = (lax.broadcasted_iota(jnp.int32, (CHUNK, CHUNK), 0)
                >= lax.broadcasted_iota(jnp.int32, (CHUNK, CHUNK), 1))
        du_parts = []
        dvn_parts = []
        for gi in range(G):
            cols = slice(gi * cg, (gi + 1) * cg)
            sv = jnp.dot(ws_ref[gi], vn[:, cols], preferred_element_type=F32) + bs_ref[:, gi:gi + 1]
            du_parts.append(dyv[:, cols] * sv)
            dvn_parts.append(jnp.dot(wst_ref[gi], dsvb[:, cols], preferred_element_type=F32))
            dw = lax.dot_general(dsvb[:, cols], vn[:, cols], (((1,), (1,)), ((), ())),
                                 preferred_element_type=F32)
            dws_ref[gi] += jnp.where(tril, dw, 0.0)
        du = jnp.concatenate(du_parts, axis=1)
        dvn = jnp.concatenate(dvn_parts, axis=1)
        dg_ref[...] += jnp.sum(dvn * xhat, axis=0, keepdims=True)
        db_ref[...] += jnp.sum(dvn, axis=0, keepdims=True)
        dxh = dvn * g_ref[...]
        m1 = jnp.mean(dxh, axis=1, keepdims=True)
        m2 = jnp.mean(dxh * xhat, axis=1, keepdims=True)
        dv = rstd * (dxh - m1 - xhat * m2)
        dz_ref[:, :E] = (du * _gelu_grad(zu)).astype(BF16)
        dz_ref[:, E:] = (dv * _gelu_grad(zv)).astype(BF16)

        @pl.when(step == nsteps - 1)
        def _():
            lane = lax.broadcasted_iota(jnp.int32, (CHUNK, LANES), 1)
            out = jnp.zeros((CHUNK, LANES), F32)
            for gi in range(G):
                s = jnp.sum(dsv_acc[:, gi * cg:(gi + 1) * cg], axis=1, keepdims=True)
                out = jnp.where(lane == gi, s, out)
            dbs_ref[...] = out

    vecE = pl.BlockSpec((1, E), lambda i: (0, 0))
    wspec = pl.BlockSpec((G, CHUNK, CHUNK), lambda i: (0, 0, 0))
    return pl.pallas_call(
        body, grid=(nsteps,),
        in_specs=[pl.BlockSpec((rb, E2), lambda i: (i, 0)), pl.BlockSpec((rb, E), lambda i: (i, 0)),
                  vecE, vecE, wspec, wspec, pl.BlockSpec((CHUNK, G), lambda i: (0, 0))],
        out_specs=[pl.BlockSpec((rb, E2), lambda i: (i, 0)), vecE, vecE, wspec,
                   pl.BlockSpec((CHUNK, LANES), lambda i: (0, 0))],
        out_shape=[jax.ShapeDtypeStruct((T, E2), BF16), jax.ShapeDtypeStruct((1, E), F32),
                   jax.ShapeDtypeStruct((1, E), F32), jax.ShapeDtypeStruct((G, CHUNK, CHUNK), F32),
                   jax.ShapeDtypeStruct((CHUNK, LANES), F32)],
        scratch_shapes=[pltpu.VMEM((CHUNK, E), F32)],
        compiler_params=_params(("arbitrary",)), name=name,
    )(zp, dy, ln_g.reshape(1, E), ln_b.reshape(1, E), ws, wst, bst)


def _shift_down(x, k, row):
    return jnp.where(row >= k, pltpu.roll(x, k, 0), 0.0)


def _shift_up(x, k, row, T):
    return jnp.where(row < T - k, pltpu.roll(x, T - k, 0), 0.0)


def _conv3(x, w_ref, b_ref, row):
    return (w_ref[0:1, :] * _shift_down(x, 2, row) + w_ref[1:2, :] * _shift_down(x, 1, row)
            + w_ref[2:3, :] * x + b_ref[...])


def _convgate_fwd(hh, cw, cb, *, name):
    T, F2 = hh.shape
    F = F2 // 2
    ns = F // LANES

    def body(a_ref, g_ref, wa_ref, wg_ref, ba_ref, bg_ref, o_ref):
        row = lax.broadcasted_iota(jnp.int32, (T, LANES), 0)
        ca = _conv3(a_ref[...].astype(F32), wa_ref, ba_ref, row)
        cgv = _conv3(g_ref[...].astype(F32), wg_ref, bg_ref, row)
        o_ref[...] = (_gelu(ca) * cgv).astype(BF16)

    sa = lambda r: pl.BlockSpec((r, LANES), lambda j: (0, j))
    sg = lambda r: pl.BlockSpec((r, LANES), lambda j: (0, j + ns))
    return pl.pallas_call(
        body, grid=(ns,), in_specs=[sa(T), sg(T), sa(3), sg(3), sa(1), sg(1)],
        out_specs=sa(T), out_shape=jax.ShapeDtypeStruct((T, F), BF16),
        compiler_params=_params(("parallel",)), name=name,
    )(hh, hh, cw, cw, cb, cb)


def _convgate_bwd(hh, dact, cw, cb, *, name):
    T, F2 = hh.shape
    F = F2 // 2
    ns = F // LANES

    def body(a_ref, g_ref, d_ref, wa_ref, wg_ref, ba_ref, bg_ref,
             da_ref, dg_ref, dwa_ref, dwg_ref, dba_ref, dbg_ref):
        row = lax.broadcasted_iota(jnp.int32, (T, LANES), 0)
        d = d_ref[...].astype(F32)
        ca = _conv3(a_ref[...].astype(F32), wa_ref, ba_ref, row)
        cgv = _conv3(g_ref[...].astype(F32), wg_ref, bg_ref, row)
        cdf = 0.5 * (1.0 + lax.erf(ca * INV_SQRT2))
        dca = d * cgv * (cdf + ca * jnp.exp(-0.5 * ca * ca) * INV_SQRT_2PI)
        dcg = d * (ca * cdf)
        for x_ref, w_ref, dc, dx_ref, dw_ref, db_ref in (
                (a_ref, wa_ref, dca, da_ref, dwa_ref, dba_ref),
                (g_ref, wg_ref, dcg, dg_ref, dwg_ref, dbg_ref)):
            x = x_ref[...].astype(F32)
            dx = (w_ref[2:3, :] * dc + w_ref[1:2, :] * _shift_up(dc, 1, row, T)
                  + w_ref[0:1, :] * _shift_up(dc, 2, row, T))
            dx_ref[...] = dx.astype(BF16)
            dw_ref[0:1, :] = jnp.sum(dc * _shift_down(x, 2, row), axis=0, keepdims=True)
            dw_ref[1:2, :] = jnp.sum(dc * _shift_down(x, 1, row), axis=0, keepdims=True)
            dw_ref[2:3, :] = jnp.sum(dc * x, axis=0, keepdims=True)
            db_ref[...] = jnp.sum(dc, axis=0, keepdims=True)

    sa = lambda r: pl.BlockSpec((r, LANES), lambda j: (0, j))
    sg = lambda r: pl.BlockSpec((r, LANES), lambda j: (0, j + ns))
    return pl.pallas_call(
        body, grid=(ns,), in_specs=[sa(T), sg(T), sa(T), sa(3), sg(3), sa(1), sg(1)],
        out_specs=[sa(T), sa(T), sa(3), sa(3), sa(1), sa(1)],
        out_shape=[jax.ShapeDtypeStruct((T, F), BF16), jax.ShapeDtypeStruct((T, F), BF16),
                   jax.ShapeDtypeStruct((3, F), F32), jax.ShapeDtypeStruct((3, F), F32),
                   jax.ShapeDtypeStruct((1, F), F32), jax.ShapeDtypeStruct((1, F), F32)],
        compiler_params=_params(("parallel",)), name=name,
    )(hh, hh, dact, cw, cw, cb, cb)


def _bucket_maps():
    iq = np.arange(BLK)[:, None]
    ik = np.arange(2 * BLK)[None, :]
    delta = iq + BLK - ik
    maps = []
    for win, dil in DILATED_GROUPS:
        n = np.clip(delta, 0, None) * dil
        max_exact = REL_BUCKETS // 2
        nf = np.maximum(n, 1).astype(np.float32)
        large = max_exact + (np.log(nf / np.float32(max_exact)) / np.float32(math.log(REL_MAX_DIST / max_exact))
                             * np.float32(REL_BUCKETS - max_exact)).astype(np.int32)
        large = np.minimum(large, REL_BUCKETS - 1)
        bucket = np.where(n < max_exact, n, large)
        valid = (delta >= 0) & (delta <= win // dil)
        maps.append(np.where(valid, bucket, -1).astype(np.int32))
    return np.stack(maps)


def _band_bias(rel_table, bmap, H, *, name):
    def body(t_ref, m_ref, o_ref):
        g = pl.program_id(0)
        bm = m_ref[0]
        for h in range(H):
            acc = jnp.full((BLK, 2 * BLK), NEG, F32)
            for b in range(REL_BUCKETS):
                acc = jnp.where(bm == b, t_ref[b, g * H + h], acc)
            o_ref[0, h] = acc

    return pl.pallas_call(
        body, grid=(N_GROUPS,),
        in_specs=[pl.BlockSpec(memory_space=pltpu.SMEM),
                  pl.BlockSpec((1, BLK, 2 * BLK), lambda g: (g, 0, 0))],
        out_specs=pl.BlockSpec((1, H, BLK, 2 * BLK), lambda g: (g, 0, 0, 0)),
        out_shape=jax.ShapeDtypeStruct((N_GROUPS, H, BLK, 2 * BLK), F32),
        compiler_params=_params(("parallel",)), name=name,
    )(rel_table, bmap)


def _band_bias_bwd(dbias, bmap, H, *, name):
    def body(d_ref, m_ref, o_ref):
        bm = m_ref[0]
        rowi = lax.broadcasted_iota(jnp.int32, (REL_BUCKETS, LANES), 0)
        lane = lax.broadcasted_iota(jnp.int32, (REL_BUCKETS, LANES), 1)
        out = jnp.zeros((REL_BUCKETS, LANES), F32)
        for h in range(H):
            dv = d_ref[0, h]
            for b in range(REL_BUCKETS):
                s = jnp.sum(jnp.sum(jnp.where(bm == b, dv, 0.0), axis=1, keepdims=True),
                            axis=0, keepdims=True)
                out = jnp.where((rowi == b) & (lane == h), s, out)
        o_ref[0] = out

    return pl.pallas_call(
        body, grid=(N_GROUPS,),
        in_specs=[pl.BlockSpec((1, H, BLK, 2 * BLK), lambda g: (g, 0, 0, 0)),
                  pl.BlockSpec((1, BLK, 2 * BLK), lambda g: (g, 0, 0))],
        out_specs=pl.BlockSpec((1, REL_BUCKETS, LANES), lambda g: (g, 0, 0)),
        out_shape=jax.ShapeDtypeStruct((N_GROUPS, REL_BUCKETS, LANES), F32),
        compiler_params=_params(("parallel",)), name=name,
    )(dbias, bmap)


def _head_masks():
    lane = lax.broadcasted_iota(jnp.int32, (BLK, LANES), 1)
    return (lane < HEAD_DIM, lane >= HEAD_DIM)


def _attn_fwd(q, kv, bias, gi, *, name):
    T = q.shape[0]
    HD = kv.shape[1] // 2
    d = DILATED_GROUPS[gi][1]
    S = T // d
    NB = S // BLK
    H = HD // HEAD_DIM
    qv, qcol = (q, gi) if d == 1 else (q[:, gi * HD:(gi + 1) * HD].reshape(S, d * HD), 0)
    kvv = kv.reshape(S, d * 2 * HD)

    def body(q_ref, kp_ref, kc_ref, vp_ref, vc_ref, b_ref, o_ref, l_ref):
        n = pl.program_id(1)
        col = lax.broadcasted_iota(jnp.int32, (BLK, 2 * BLK), 1)
        first = (n == 0) & (col < BLK)
        hm = _head_masks()
        for p in range(HD // LANES):
            sl = slice(p * LANES, (p + 1) * LANES)
            qp = q_ref[:, sl]
            kc = jnp.concatenate([kp_ref[:, sl], kc_ref[:, sl]], axis=0)
            vc = jnp.concatenate([vp_ref[:, sl], vc_ref[:, sl]], axis=0)
            outs = []
            lses = []
            for hh in range(2):
                qm = jnp.where(hm[hh], qp, jnp.zeros_like(qp))
                s = lax.dot_general(qm, kc, (((1,), (1,)), ((), ())), preferred_element_type=F32)
                s = jnp.where(first, NEG, s + b_ref[2 * p + hh])
                m = jnp.max(s, axis=1, keepdims=True)
                e = jnp.exp(s - m)
                den = jnp.sum(e, axis=1, keepdims=True)
                outs.append(jnp.dot((e / den).astype(BF16), vc, preferred_element_type=F32))
                lses.append(m + jnp.log(den))
            o_ref[:, sl] = jnp.where(hm[0], outs[0], outs[1])
            l_ref[:, sl] = jnp.where(hm[0], lses[0], lses[1])

    blk = lambda f: pl.BlockSpec((BLK, HD), f)
    prev = lambda n: jnp.maximum(n - 1, 0)
    return pl.pallas_call(
        body, grid=(d, NB),
        in_specs=[blk(lambda r, n: (n, r + qcol)),
                  blk(lambda r, n: (prev(n), r * 2)), blk(lambda r, n: (n, r * 2)),
                  blk(lambda r, n: (prev(n), r * 2 + 1)), blk(lambda r, n: (n, r * 2 + 1)),
                  pl.BlockSpec((H, BLK, 2 * BLK), lambda r, n: (0, 0, 0))],
        out_specs=[blk(lambda r, n: (n, r)), blk(lambda r, n: (n, r))],
        out_shape=[jax.ShapeDtypeStruct((S, d * HD), F32), jax.ShapeDtypeStruct((S, d * HD), F32)],
        compiler_params=_params(("parallel", "parallel")), name=name,
    )(qv, kvv, kvv, kvv, kvv, bias)


def _attn_combine(os, ls, *, name):
    T, HD = os[0].shape
    rb = _pick(T, 512)

    def body(o0, o1, o2, l0, l1, l2, o_ref, ob_ref, l_ref):
        la, lb, lc = l0[...], l1[...], l2[...]
        m = jnp.maximum(jnp.maximum(la, lb), lc)
        L = m + jnp.log(jnp.exp(la - m) + jnp.exp(lb - m) + jnp.exp(lc - m))
        o = jnp.exp(la - L) * o0[...] + jnp.exp(lb - L) * o1[...] + jnp.exp(lc - L) * o2[...]
        o_ref[...] = o
        ob_ref[...] = o.astype(BF16)
        l_ref[...] = L

    row = pl.BlockSpec((rb, HD), lambda i: (i, 0))
    return pl.pallas_call(
        body, grid=(T // rb,), in_specs=[row] * 6, out_specs=[row] * 3,
        out_shape=[jax.ShapeDtypeStruct((T, HD), F32), jax.ShapeDtypeStruct((T, HD), BF16),
                   jax.ShapeDtypeStruct((T, HD), F32)],
        compiler_params=_params(("parallel",)), name=name,
    )(*[a.reshape(T, HD) for a in os], *[a.reshape(T, HD) for a in ls])


def _attn_bwd(q, kv, bias, do, o, L, gi, *, name):
    T = q.shape[0]
    HD = kv.shape[1] // 2
    d = DILATED_GROUPS[gi][1]
    S = T // d
    NB = S // BLK
    H = HD // HEAD_DIM
    qv, qcol = (q, gi) if d == 1 else (q[:, gi * HD:(gi + 1) * HD].reshape(S, d * HD), 0)
    kvv = kv.reshape(S, d * 2 * HD)
    dov, ov, Lv = (a.reshape(S, d * HD) for a in (do, o, L))

    def body(q_ref, kp_ref, kc_ref, vp_ref, vc_ref, b_ref, do_ref, o_ref, L_ref,
             dq_ref, dk_ref, dv_ref, db_ref, ck_ref, cv_ref):
        r = pl.program_id(0)
        n = pl.program_id(1)

        @pl.when((r == 0) & (n == 0))
        def _():
            db_ref[...] = jnp.zeros_like(db_ref)

        @pl.when(n == 0)
        def _():
            ck_ref[...] = jnp.zeros_like(ck_ref)
            cv_ref[...] = jnp.zeros_like(cv_ref)

        @pl.when(n < NB)
        def _():
            col = lax.broadcasted_iota(jnp.int32, (BLK, 2 * BLK), 1)
            first = (n == 0) & (col < BLK)
            hm = _head_masks()
            for p in range(HD // LANES):
                sl = slice(p * LANES, (p + 1) * LANES)
                qp = q_ref[:, sl]
                kc = jnp.concatenate([kp_ref[:, sl], kc_ref[:, sl]], axis=0)
                vc = jnp.concatenate([vp_ref[:, sl], vc_ref[:, sl]], axis=0)
                dop = do_ref[:, sl]
                dob = dop.astype(BF16)
                prod = dop * o_ref[:, sl]
                Lp = L_ref[:, sl]
                dq_parts = []
                dkc = None
                dvc = None
                for hh in range(2):
                    qm = jnp.where(hm[hh], qp, jnp.zeros_like(qp))
                    dom = jnp.where(hm[hh], dob, jnp.zeros_like(dob))
                    s = lax.dot_general(qm, kc, (((1,), (1,)), ((), ())), preferred_element_type=F32)
                    s = jnp.where(first, NEG, s + b_ref[2 * p + hh])
                    lse = Lp[:, hh * HEAD_DIM:hh * HEAD_DIM + 1]
                    pr = jnp.exp(s - lse)
                    dp = lax.dot_general(dom, vc, (((1,), (1,)), ((), ())), preferred_element_type=F32)
                    delta = jnp.sum(jnp.where(hm[hh], prod, 0.0), axis=1, keepdims=True)
                    ds = pr * (dp - delta)
                    db_ref[2 * p + hh] += ds
                    dsb = ds.astype(BF16)
                    dq_parts.append(jnp.dot(dsb, kc, preferred_element_type=F32))
                    dkh = lax.dot_general(dsb, qm, (((0,), (0,)), ((), ())), preferred_element_type=F32)
                    dvh = lax.dot_general(pr.astype(BF16), dom, (((0,), (0,)), ((), ())),
                                          preferred_element_type=F32)
                    dkc = dkh if dkc is None else dkc + dkh
                    dvc = dvh if dvc is None else dvc + dvh
                dq = jnp.where(hm[0], dq_parts[0], dq_parts[1])
                dq_ref[:, sl] = (dq * (HEAD_DIM ** -0.5)).astype(BF16)
                dk_ref[:, sl] = ck_ref[:, sl] + dkc[:BLK]
                dv_ref[:, sl] = cv_ref[:, sl] + dvc[:BLK]
                ck_ref[:, sl] = dkc[BLK:]
                cv_ref[:, sl] = dvc[BLK:]

        @pl.when(n == NB)
        def _():
            dk_ref[...] = ck_ref[...]
            dv_ref[...] = cv_ref[...]

    blk = lambda f: pl.BlockSpec((BLK, HD), f)
    cur = lambda n: jnp.minimum(n, NB - 1)
    prev = lambda n: jnp.maximum(jnp.minimum(n, NB - 1) - 1, 0)
    lag = lambda n: jnp.maximum(n - 1, 0)
    return pl.pallas_call(
        body, grid=(d, NB + 1),
        in_specs=[blk(lambda r, n: (cur(n), r + qcol)),
                  blk(lambda r, n: (prev(n), r * 2)), blk(lambda r, n: (cur(n), r * 2)),
                  blk(lambda r, n: (prev(n), r * 2 + 1)), blk(lambda r, n: (cur(n), r * 2 + 1)),
                  pl.BlockSpec((H, BLK, 2 * BLK), lambda r, n: (0, 0, 0)),
                  blk(lambda r, n: (cur(n), r)), blk(lambda r, n: (cur(n), r)),
                  blk(lambda r, n: (cur(n), r))],
        out_specs=[blk(lambda r, n: (cur(n), r)), blk(lambda r, n: (lag(n), r)),
                   blk(lambda r, n: (lag(n), r)),
                   pl.BlockSpec((H, BLK, 2 * BLK), lambda r, n: (0, 0, 0))],
        out_shape=[jax.ShapeDtypeStruct((S, d * HD), BF16), jax.ShapeDtypeStruct((S, d * HD), F32),
                   jax.ShapeDtypeStruct((S, d * HD), F32),
                   jax.ShapeDtypeStruct((H, BLK, 2 * BLK), F32)],
        scratch_shapes=[pltpu.VMEM((BLK, HD), F32), pltpu.VMEM((BLK, HD), F32)],
        compiler_params=_params(("arbitrary", "arbitrary")), name=name,
    )(qv, kvv, kvv, kvv, kvv, bias, dov, ov, Lv)


SUPER = DILATED_GROUPS[-1][1] * BLK


def _band_rows(it, d):
    r, j = it % d, it // d
    if d == 1:
        at = lambda blk: pl.ds(pl.multiple_of(blk * BLK, BLK), BLK)
    else:
        at = lambda blk: pl.ds(r + d * BLK * blk, BLK, stride=d)
    return at(j), at(jnp.maximum(j - 1, 0)), j > 0, j


def _last_rows(it, d):
    m = SUPER // (d * BLK)
    if d == 1:
        return pl.ds((m - 1) * BLK, BLK)
    return pl.ds(it % d + d * BLK * (m - 1), BLK, stride=d)


def _attn_fwd_all(q, kv, bias, *, name):
    T = q.shape[0]
    HD = kv.shape[1] // 2
    PP = HD // LANES
    NS = T // SUPER

    def body(q0, q1, q2, kp_ref, kc_ref, vp_ref, vc_ref, b_ref, o_ref, ob_ref, l_ref, og, lg):
        n = pl.program_id(1)
        col = lax.broadcasted_iota(jnp.int32, (BLK, 2 * BLK), 1)
        hm = _head_masks()
        for g, (q_ref, (_, d)) in enumerate(zip((q0, q1, q2), DILATED_GROUPS)):
            def step(it, carry, g=g, q_ref=q_ref, d=d):
                cur, prv, inside, j = _band_rows(it, d)
                last = _last_rows(it, d)
                qp = q_ref[cur, :].astype(BF16)
                kprev = jnp.where(inside, kc_ref[prv, :], kp_ref[last, :]).astype(BF16)
                vprev = jnp.where(inside, vc_ref[prv, :], vp_ref[last, :]).astype(BF16)
                kc = jnp.concatenate([kprev, kc_ref[cur, :].astype(BF16)], axis=0)
                vc = jnp.concatenate([vprev, vc_ref[cur, :].astype(BF16)], axis=0)
                first = (n == 0) & (j == 0) & (col < BLK)
                outs, lses = [], []
                for hh in range(2):
                    qm = jnp.where(hm[hh], qp, jnp.zeros_like(qp))
                    s = lax.dot_general(qm, kc, (((1,), (1,)), ((), ())), preferred_element_type=F32)
                    s = jnp.where(first, NEG, s + b_ref[g, hh])
                    mx = jnp.max(s, axis=1, keepdims=True)
                    e = jnp.exp(s - mx)
                    den = jnp.sum(e, axis=1, keepdims=True)
                    outs.append(jnp.dot((e / den).astype(BF16), vc, preferred_element_type=F32))
                    lses.append(mx + jnp.log(den))
                og.at[g][cur, :] = jnp.where(hm[0], outs[0], outs[1])
                lg.at[g][cur, :] = jnp.where(hm[0], lses[0], lses[1])
                return carry

            lax.fori_loop(0, SUPER // BLK, step, 0)
        la, lb, lc = lg[0], lg[1], lg[2]
        mx = jnp.maximum(jnp.maximum(la, lb), lc)
        L = mx + jnp.log(jnp.exp(la - mx) + jnp.exp(lb - mx) + jnp.exp(lc - mx))
        o = jnp.exp(la - L) * og[0] + jnp.exp(lb - L) * og[1] + jnp.exp(lc - L) * og[2]
        o_ref[...] = o
        ob_ref[...] = o.astype(BF16)
        l_ref[...] = L

    blk = lambda f: pl.BlockSpec((SUPER, LANES), f)
    prev = lambda n: jnp.maximum(n - 1, 0)
    qspec = lambda g: blk(lambda p, n: (n, g * PP + p))
    return pl.pallas_call(
        body, grid=(PP, NS),
        in_specs=[qspec(0), qspec(1), qspec(2),
                  blk(lambda p, n: (prev(n), p)), blk(lambda p, n: (n, p)),
                  blk(lambda p, n: (prev(n), PP + p)), blk(lambda p, n: (n, PP + p)),
                  pl.BlockSpec((N_GROUPS, 2, BLK, 2 * BLK), lambda p, n: (0, p, 0, 0))],
        out_specs=[blk(lambda p, n: (n, p))] * 3,
        out_shape=[jax.ShapeDtypeStruct((T, HD), F32), jax.ShapeDtypeStruct((T, HD), BF16),
                   jax.ShapeDtypeStruct((T, HD), F32)],
        scratch_shapes=[pltpu.VMEM((N_GROUPS, SUPER, LANES), F32), pltpu.VMEM((N_GROUPS, SUPER, LANES), F32)],
        compiler_params=_params(("parallel", "parallel")), name=name,
    )(q, q, q, kv, kv, kv, kv, bias)


def _attn_bwd_all(q, kv, bias, do, o, L, *, name):
    T = q.shape[0]
    HD = kv.shape[1] // 2
    PP = HD // LANES
    H = HD // HEAD_DIM
    NS = T // SUPER

    def body(q0, q1, q2, kp_ref, kc_ref, vp_ref, vc_ref, b_ref, do_ref, o_ref, L_ref,
             dq_ref, dk_ref, dv_ref, db_ref, ck_ref, cv_ref):
        n = pl.program_id(1)

        @pl.when(n == 0)
        def _():
            db_ref[...] = jnp.zeros_like(db_ref)
            ck_ref[...] = jnp.zeros_like(ck_ref)
            cv_ref[...] = jnp.zeros_like(cv_ref)

        dk_ref[...] = ck_ref[...]
        dv_ref[...] = cv_ref[...]
        ck_ref[...] = jnp.zeros_like(ck_ref)
        cv_ref[...] = jnp.zeros_like(cv_ref)

        @pl.when(n < NS)
        def _():
            col = lax.broadcasted_iota(jnp.int32, (BLK, 2 * BLK), 1)
            hm = _head_masks()
            for g, (q_ref, (_, d)) in enumerate(zip((q0, q1, q2), DILATED_GROUPS)):
                def step(it, carry, g=g, q_ref=q_ref, d=d):
                    cur, prv, inside, j = _band_rows(it, d)
                    last = _last_rows(it, d)
                    qp = q_ref[cur, :].astype(BF16)
                    kprev = jnp.where(inside, kc_ref[prv, :], kp_ref[last, :]).astype(BF16)
                    vprev = jnp.where(inside, vc_ref[prv, :], vp_ref[last, :]).astype(BF16)
                    kc = jnp.concatenate([kprev, kc_ref[cur, :].astype(BF16)], axis=0)
                    vc = jnp.concatenate([vprev, vc_ref[cur, :].astype(BF16)], axis=0)
                    first = (n == 0) & (j == 0) & (col < BLK)
                    dop = do_ref[cur, :]
                    dob = dop.astype(BF16)
                    prod = dop * o_ref[cur, :]
                    Lp = L_ref[cur, :]
                    dq_parts = []
                    dkc = None
                    dvc = None
                    for hh in range(2):
                        qm = jnp.where(hm[hh], qp, jnp.zeros_like(qp))
                        dom = jnp.where(hm[hh], dob, jnp.zeros_like(dob))
                        s = lax.dot_general(qm, kc, (((1,), (1,)), ((), ())), preferred_element_type=F32)
                        s = jnp.where(first, NEG, s + b_ref[g, hh])
                        lse = Lp[:, hh * HEAD_DIM:hh * HEAD_DIM + 1]
                        pr = jnp.exp(s - lse)
                        dp = lax.dot_general(dom, vc, (((1,), (1,)), ((), ())), preferred_element_type=F32)
                        delta = jnp.sum(jnp.where(hm[hh], prod, 0.0), axis=1, keepdims=True)
                        ds = pr * (dp - delta)
                        db_ref[g, hh] += ds
                        dsb = ds.astype(BF16)
                        dq_parts.append(jnp.dot(dsb, kc, preferred_element_type=F32))
                        dkh = lax.dot_general(dsb, qm, (((0,), (0,)), ((), ())), preferred_element_type=F32)
                        dvh = lax.dot_general(pr.astype(BF16), dom, (((0,), (0,)), ((), ())),
                                              preferred_element_type=F32)
                        dkc = dkh if dkc is None else dkc + dkh
                        dvc = dvh if dvc is None else dvc + dvh
                    dq_ref.at[g][cur, :] = jnp.where(hm[0], dq_parts[0], dq_parts[1]) * (HEAD_DIM ** -0.5)
                    ck_ref[cur, :] += dkc[BLK:]
                    cv_ref[cur, :] += dvc[BLK:]
                    ck_ref[prv, :] += jnp.where(inside, dkc[:BLK], 0.0)
                    cv_ref[prv, :] += jnp.where(inside, dvc[:BLK], 0.0)
                    dk_ref[last, :] += jnp.where(inside, 0.0, dkc[:BLK])
                    dv_ref[last, :] += jnp.where(inside, 0.0, dvc[:BLK])
                    return carry

                lax.fori_loop(0, SUPER // BLK, step, 0)

    blk = lambda f: pl.BlockSpec((SUPER, LANES), f)
    cur = lambda n: jnp.minimum(n, NS - 1)
    prev = lambda n: jnp.maximum(jnp.minimum(n, NS - 1) - 1, 0)
    lag = lambda n: jnp.maximum(n - 1, 0)
    qspec = lambda g: blk(lambda p, n: (cur(n), g * PP + p))
    bspec = pl.BlockSpec((N_GROUPS, 2, BLK, 2 * BLK), lambda p, n: (0, p, 0, 0))
    return pl.pallas_call(
        body, grid=(PP, NS + 1),
        in_specs=[qspec(0), qspec(1), qspec(2),
                  blk(lambda p, n: (prev(n), p)), blk(lambda p, n: (cur(n), p)),
                  blk(lambda p, n: (prev(n), PP + p)), blk(lambda p, n: (cur(n), PP + p)),
                  bspec, blk(lambda p, n: (cur(n), p)), blk(lambda p, n: (cur(n), p)),
                  blk(lambda p, n: (cur(n), p))],
        out_specs=[pl.BlockSpec((N_GROUPS, SUPER, LANES), lambda p, n: (0, cur(n), p)),
                   blk(lambda p, n: (lag(n), p)), blk(lambda p, n: (lag(n), p)), bspec],
        out_shape=[jax.ShapeDtypeStruct((N_GROUPS, T, HD), F32), jax.ShapeDtypeStruct((T, HD), F32),
                   jax.ShapeDtypeStruct((T, HD), F32),
                   jax.ShapeDtypeStruct((N_GROUPS, H, BLK, 2 * BLK), F32)],
        scratch_shapes=[pltpu.VMEM((SUPER, LANES), F32), pltpu.VMEM((SUPER, LANES), F32)],
        compiler_params=_params(("arbitrary", "arbitrary")), name=name,
    )(q, q, q, kv, kv, kv, kv, bias, do, o, L)


class _Weights(dict):
    def __init__(self, base, fetch=None, emit=None):
        super().__init__(base)
        self._fetch, self._emit = fetch, emit

    def fetch(self, group, after):
        if self._fetch is not None:
            for (key, layer), mat in self._fetch(group, after).items():
                self[key][layer] = mat

    def emit(self, group, mats):
        return None if self._emit is None else self._emit(group, mats)


def _local_step(x, tgt, W):
    T, D = x.shape
    H = W["rel_table"].shape[1] // N_GROUPS
    HD = H * HEAD_DIM
    G = W["a_w_s"].shape[1]
    assert T % (DILATED_GROUPS[-1][1] * BLK) == 0

    tril = jnp.tril(jnp.ones((CHUNK, CHUNK), F32))
    bmap = jnp.asarray(_bucket_maps())
    bias = _band_bias(W["rel_table"], bmap, H, name="band_bias")

    saved = []
    xc, xcb = x, x.astype(BF16)
    kvb = None
    for i in range(DEPTH):
        s = {"x": xc, "xb": xcb}
        W.fetch(4 * i, xc)
        if i < N_A:
            ws_m = W["a_w_s"][i] * tril
            s["ws"] = ws_m.astype(BF16)
            s["wst"] = jnp.swapaxes(ws_m, 1, 2).astype(BF16)
            s["bst"] = W["a_b_s"][i].T
            s["zp"] = _mm(xcb, W["a_w_in"][i], out_dtype=ACT, name=f"a_in_{i}")
            s["y"] = _sgu_fwd(s["zp"], W["a_ln_g"][i], W["a_ln_b"][i], s["ws"], s["bst"], name=f"sgu_fwd_{i}")
            W.fetch(4 * i + 1, s["zp"])
            s["h"] = _mm(s["y"], W["a_w_out"][i], name=f"a_out_{i}")
        else:
            j = i - N_A
            if kvb is None:
                kvb = _mm(xcb, W["kv_w"][0], name="kv_proj")
            s["q"] = _mm(xcb, W["b_w_q_t"][j], tb=True, scale=HEAD_DIM ** -0.5, name=f"q_proj_{j}")
            s["o"], s["ob"], s["L"] = _attn_fwd_all(s["q"], kvb, bias, name=f"attn_fwd_{j}")
            W.fetch(4 * i + 1, s["q"])
            s["h"] = _mm(s["ob"], W["b_w_o"][j], name=f"o_proj_{j}")
        s["x1"], s["x1b"] = _add_ln_fwd(xc, s["h"], W["ln_g"][i, 0], W["ln_b"][i, 0], name=f"ln1_fwd_{i}")
        W.fetch(4 * i + 2, s["x1"])
        s["hh"] = _mm(s["x1b"], W["ffn_w_up_t"][i], tb=True, out_dtype=ACT, name=f"ffn_up_{i}")
        s["cw"] = W["ffn_conv_w"][i]
        s["cb"] = W["ffn_conv_b"][i].reshape(1, -1)
        s["act"] = _convgate_fwd(s["hh"], s["cw"], s["cb"], name=f"convgate_fwd_{i}")
        W.fetch(4 * i + 3, s["hh"])
        s["f"] = _mm(s["act"], W["ffn_w_down"][i], name=f"ffn_down_{i}")
        xc, xcb = _add_ln_fwd(s["x1"], s["f"], W["ln_g"][i, 1], W["ln_b"][i, 1], name=f"ln2_fwd_{i}")
        saved.append(s)

    dy, lossv = _loss_grad(xc, tgt, name="loss_grad")
    loss = lossv[0, 0]

    gl = {k: [None] * DEPTH for k in ("ffn_w_up_t", "ffn_conv_w", "ffn_conv_b", "ffn_w_down", "ln_g", "ln_b")}
    ga = {k: [None] * N_A for k in ("a_w_in", "a_ln_g", "a_ln_b", "a_w_s", "a_b_s", "a_w_out")}
    gb = {k: [None] * (DEPTH - N_A) for k in ("b_w_q_t", "b_w_o")}
    mats = ("a_w_in", "a_w_out", "b_w_q_t", "b_w_o", "ffn_w_up_t", "ffn_w_down")
    dks, dvs, dbias = [], [], []
    grads = {}
    terms = [(1.0, dy)]
    tok = None
    behind = lambda g: g if tok is None else g + tok
    for i in reversed(range(DEPTH)):
        s = saved[i]
        dp2, dp2b, dg2, db2 = _add_ln_bwd(s["x1"], s["f"], behind(W["ln_g"][i, 1]), terms, name=f"ln2_bwd_{i}")
        dact = _mm(dp2b, W["ffn_w_down"][i], tb=True, out_dtype=ACT, name=f"ffn_down_dx_{i}")
        gl["ffn_w_down"][i] = _mm(s["act"], dp2b, ta=True, out_dtype=BF16, name=f"ffn_down_dw_{i}")
        dha, dhg, dwa, dwg, dba, dbg = _convgate_bwd(s["hh"], dact, s["cw"], s["cb"], name=f"convgate_bwd_{i}")
        dhh = (dha, dhg)
        gl["ffn_conv_w"][i] = jnp.concatenate([dwa, dwg], axis=1)
        gl["ffn_conv_b"][i] = jnp.concatenate([dba, dbg], axis=1)[0]
        dx1 = _mm(dhh, W["ffn_w_up_t"][i], name=f"ffn_up_dx_{i}")
        gl["ffn_w_up_t"][i] = _mm(dhh, s["x1b"], ta=True, out_dtype=BF16, name=f"ffn_up_dw_{i}")
        tok = W.emit(3 * i + 2, {("ffn_w_up_t", i): gl["ffn_w_up_t"][i], ("ffn_w_down", i): gl["ffn_w_down"][i]})
        dp1, dp1b, dg1, db1 = _add_ln_bwd(s["x"], s["h"], behind(W["ln_g"][i, 0]), [(ALPHA, dp2), (1.0, dx1)],
                                          name=f"ln1_bwd_{i}")
        gl["ln_g"][i] = jnp.concatenate([dg1, dg2], axis=0)
        gl["ln_b"][i] = jnp.concatenate([db1, db2], axis=0)
        terms = [(ALPHA, dp1)]
        if i < N_A:
            dyy = _mm(dp1b, W["a_w_out"][i], tb=True, out_dtype=ACT, name=f"a_out_dx_{i}")
            ga["a_w_out"][i] = _mm(s["y"], dp1b, ta=True, out_dtype=BF16, name=f"a_out_dw_{i}")
            tok = W.emit(3 * i + 1, {("a_w_out", i): ga["a_w_out"][i]})
            dzp, dlg, dlb, dws, dbs = _sgu_bwd(s["zp"], dyy, behind(W["a_ln_g"][i]), W["a_ln_b"][i], s["ws"],
                                               s["wst"], s["bst"], name=f"sgu_bwd_{i}")
            ga["a_ln_g"][i], ga["a_ln_b"][i], ga["a_w_s"][i] = dlg[0], dlb[0], dws
            ga["a_b_s"][i] = dbs[:, :G].T
            terms.append((1.0, _mm(dzp, W["a_w_in"][i], tb=True, name=f"a_in_dx_{i}")))
            ga["a_w_in"][i] = _mm(s["xb"], dzp, ta=True, out_dtype=BF16, name=f"a_in_dw_{i}")
            tok = W.emit(3 * i, {("a_w_in", i): ga["a_w_in"][i]})
        else:
            j = i - N_A
            do = _mm(dp1b, W["b_w_o"][j], tb=True, name=f"o_proj_dx_{j}")
            gb["b_w_o"][j] = _mm(s["ob"], dp1b, ta=True, out_dtype=BF16, name=f"o_proj_dw_{j}")
            tok = W.emit(3 * i + 1, {("b_w_o", j): gb["b_w_o"][j]})
            bias_b = behind(bias)
            dq, dk_j, dv_j, db_j = _attn_bwd_all(s["q"], kvb, bias_b, do, s["o"], s["L"], name=f"attn_bwd_{j}")
            dks.append((1.0, dk_j))
            dvs.append((1.0, dv_j))
            dbias.append(db_j)
            terms.append((1.0, _mm(dq, W["b_w_q_t"][j], name=f"q_proj_dx_{j}")))
            gb["b_w_q_t"][j] = _mm(dq, s["xb"], ta=True, out_dtype=BF16, name=f"q_proj_dw_{j}")
            out_b = {("b_w_q_t", j): gb["b_w_q_t"][j]}
            if i == N_A:
                dkv = jnp.concatenate([_lincomb(dks, BF16, name="dk_sum"), _lincomb(dvs, BF16, name="dv_sum")],
                                      axis=1)
                terms.append((1.0, _mm(dkv, W["kv_w"][0], tb=True, name="kv_proj_dx")))
                grads["kv_w"] = [_mm(s["xb"], dkv, ta=True, out_dtype=BF16, name="kv_proj_dw")]
                out_b[("kv_w", 0)] = grads["kv_w"][0]
                dbt = _lincomb([(1.0, a.reshape(-1, 2 * BLK)) for a in dbias], F32, name="dbias_sum")
                dtab = _band_bias_bwd(dbt.reshape(N_GROUPS, H, BLK, 2 * BLK), bmap, H, name="band_bias_bwd")
                grads["rel_table"] = jnp.transpose(dtab[:, :, :H], (1, 0, 2)).reshape(REL_BUCKETS, N_GROUPS * H)
            tok = W.emit(3 * i, out_b)
    grad_x = _lincomb(terms, F32, name="grad_x")
    for dct in (gl, ga, gb):
        for k, v in dct.items():
            grads[k] = v if k in mats else jnp.stack(v)
    return loss, grad_x, grads


def _my_index():
    return 4 * lax.axis_index("x") + 2 * lax.axis_index("y") + lax.axis_index("c")


HBM_SPEC = pl.BlockSpec(memory_space=pltpu.HBM)


def _block(ref, k, n, axis):
    off = pl.multiple_of(k * n, n)
    return ref.at[pl.ds(off, n), :] if axis == 0 else ref.at[:, pl.ds(off, n)]


def _gather_mats(local, axis, *, name):
    L, a, b = local.shape
    n = a if axis == 0 else b
    full = (a * N_DEV, b) if axis == 0 else (a, b * N_DEV)

    def body(x_ref, *rest):
        outs = rest[:L]
        send_sems, recv_sems, local_sems = rest[L:]
        x, y, c = lax.axis_index("x"), lax.axis_index("y"), lax.axis_index("c")
        me, sibling = (x, y, c), (x, y, 1 - c)
        chips = [(1 - x, y), (x, 1 - y), (1 - x, 1 - y)]

        def slot(l, px, py, pc):
            return _block(outs[l], 4 * px + 2 * py + pc, n, axis)

        def copy(l, k, blk, to, src=None):
            return pltpu.make_async_remote_copy(
                src_ref=slot(l, *blk) if src is None else src, dst_ref=slot(l, *blk),
                send_sem=send_sems.at[7 * l + k], recv_sem=recv_sems.at[7 * l + k],
                device_id=to, device_id_type=MESH)

        mine, first, passed = [], [], []
        for l in range(L):
            mine.append(pltpu.make_async_copy(x_ref.at[l], slot(l, *me), local_sems.at[l]))
            mine[-1].start()
            first.append(copy(l, 0, me, sibling, src=x_ref.at[l]))
            first += [copy(l, 1 + j, me, (*chip, c), src=x_ref.at[l]) for j, chip in enumerate(chips)]
        for cp in first:
            cp.start()
        for l in range(L):
            for j, chip in enumerate(chips):
                copy(l, 1 + j, (*chip, c), me).wait_recv()
                passed.append(copy(l, 4 + j, (*chip, c), sibling))
                passed[-1].start()
        for l in range(L):
            copy(l, 0, sibling, me).wait_recv()
            for j, chip in enumerate(chips):
                copy(l, 4 + j, (*chip, 1 - c), me).wait_recv()
        for cp in first + passed:
            cp.wait_send()
        for cp in mine:
            cp.wait()

    return pl.pallas_call(
        body, out_shape=[jax.ShapeDtypeStruct(full, local.dtype)] * L,
        in_specs=[HBM_SPEC], out_specs=[HBM_SPEC] * L,
        scratch_shapes=[pltpu.SemaphoreType.DMA((7 * L,)), pltpu.SemaphoreType.DMA((7 * L,)),
                        pltpu.SemaphoreType.DMA((L,))],
        name=name,
    )(local)


SEM_SPEC = pl.BlockSpec(memory_space=pltpu.SEMAPHORE)
FLOWING = pltpu.SideEffectType.DATAFLOW_SIDE_EFFECTING


def _peers(x, y, c):
    return [(1 - x if k & 4 else x, 1 - y if k & 2 else y, 1 - c if k & 1 else c) for k in range(1, N_DEV)]


def _ends(src_ref, land_ref, peer_index, me, n, axis, gather):
    if gather:
        return src_ref, _block(land_ref, me, n, axis)
    return _block(src_ref, peer_index, n, axis), land_ref.at[me]


def _send_start(groups, gather, *, name):
    flat = [(g, j, mat, axis) for g, items in enumerate(groups) for j, (mat, axis) in enumerate(items)]
    M, G = len(flat), len(groups)
    lands, ns = [], []
    for _, _, mat, axis in flat:
        A, B = mat.shape
        if gather:
            lands.append((A * N_DEV, B) if axis == 0 else (A, B * N_DEV))
            ns.append(A if axis == 0 else B)
        else:
            lands.append((N_DEV, A // N_DEV, B) if axis == 0 else (N_DEV, A, B // N_DEV))
            ns.append(A // N_DEV if axis == 0 else B // N_DEV)

    def body(*refs):
        src_refs, land_refs, sems = refs[:M], refs[M:2 * M], refs[2 * M:2 * M + 3 * G]
        token = refs[-1]
        x, y, c = lax.axis_index("x"), lax.axis_index("y"), lax.axis_index("c")
        me = 4 * x + 2 * y + c
        for i, (g, j, _, axis) in enumerate(flat):
            for k, (px, py, pc) in enumerate(_peers(x, y, c)):
                s, d = _ends(src_refs[i], land_refs[i], 4 * px + 2 * py + pc, me, ns[i], axis, gather)
                pltpu.make_async_remote_copy(
                    src_ref=s, dst_ref=d, send_sem=sems[3 * g].at[7 * j + k], recv_sem=sems[3 * g + 1].at[7 * j + k],
                    device_id=(px, py, pc), device_id_type=MESH).start()
            s, d = _ends(src_refs[i], land_refs[i], me, me, ns[i], axis, gather)
            pltpu.make_async_copy(s, d, sems[3 * g + 2].at[j]).start()
        token[...] = jnp.zeros_like(token)

    sem_shapes = []
    for items in groups:
        sem_shapes += [pltpu.SemaphoreType.DMA((7 * len(items),))] * 2 + [pltpu.SemaphoreType.DMA((len(items),))]
    outs = pl.pallas_call(
        body, name=name,
        out_shape=(*sem_shapes, *[pltpu.HBM(m.shape, m.dtype) for _, _, m, _ in flat],
                   *[pltpu.HBM(shp, m.dtype) for shp, (_, _, m, _) in zip(lands, flat)],
                   jax.ShapeDtypeStruct((8, LANES), F32)),
        in_specs=[HBM_SPEC] * (2 * M),
        out_specs=(*[SEM_SPEC] * (3 * G), *[HBM_SPEC] * (2 * M), pl.BlockSpec(memory_space=pltpu.VMEM)),
        input_output_aliases={i: 3 * G + i for i in range(2 * M)},
        compiler_params=pltpu.CompilerParams(has_side_effects=FLOWING),
    )(*[pltpu.with_memory_space_constraint(m, pltpu.HBM) for _, _, m, _ in flat],
      *[pltpu.with_memory_space_constraint(lax.empty(shp, m.dtype), pltpu.HBM)
        for shp, (_, _, m, _) in zip(lands, flat)])
    handles = []
    for g in range(G):
        idx = [i for i, f in enumerate(flat) if f[0] == g]
        handles.append((outs[3 * g], outs[3 * g + 1], outs[3 * g + 2], [outs[3 * G + i] for i in idx],
                        [outs[3 * G + M + i] for i in idx], [flat[i][3] for i in idx]))
    return handles, outs[-1]


def _send_wait(handle, gather, after, *, name):
    send_sems, recv_sems, local_sems, mats, lands, axes = handle
    n_m = len(mats)
    ns = []
    for mat, land, axis in zip(mats, lands, axes):
        ns.append(mat.shape[axis] if gather else land.shape[1 + axis])

    def body(*refs):
        src_refs, land_refs = refs[:n_m], refs[n_m:2 * n_m]
        ssem, rsem, lsem = refs[2 * n_m:2 * n_m + 3]
        x, y, c = lax.axis_index("x"), lax.axis_index("y"), lax.axis_index("c")
        me = 4 * x + 2 * y + c
        for j in range(n_m):
            for k, (px, py, pc) in enumerate(_peers(x, y, c)):
                s, d = _ends(src_refs[j], land_refs[j], 4 * px + 2 * py + pc, me, ns[j], axes[j], gather)
                cp = pltpu.make_async_remote_copy(
                    src_ref=s, dst_ref=d, send_sem=ssem.at[7 * j + k], recv_sem=rsem.at[7 * j + k],
                    device_id=(px, py, pc), device_id_type=MESH)
                cp.wait_send()
                cp.wait_recv()
            s, d = _ends(src_refs[j], land_refs[j], me, me, ns[j], axes[j], gather)
            pltpu.make_async_copy(s, d, lsem.at[j]).wait()

    outs = pl.pallas_call(
        body, name=name,
        out_shape=(*[pltpu.HBM(m.shape, m.dtype) for m in mats], *[pltpu.HBM(l.shape, l.dtype) for l in lands]),
        in_specs=[HBM_SPEC] * (2 * n_m) + [SEM_SPEC] * 3 + [pl.BlockSpec(memory_space=pl.ANY)],
        out_specs=tuple([HBM_SPEC] * (2 * n_m)),
        input_output_aliases={i: i for i in range(2 * n_m)},
        compiler_params=pltpu.CompilerParams(has_side_effects=FLOWING),
    )(*mats, *lands, send_sems, recv_sems, local_sems, after)
    return list(outs[n_m:])


def _sum_parts(parts, *, name):
    n, R, C = parts.shape
    rb = _pick(R, 512) if R % LANES == 0 else R

    def body(p_ref, o_ref):
        acc = p_ref[0].astype(F32)
        for k in range(1, n):
            acc = acc + p_ref[k].astype(F32)
        o_ref[...] = acc

    return pl.pallas_call(
        body, grid=(R // rb,), in_specs=[pl.BlockSpec((n, rb, C), lambda i: (0, i, 0))],
        out_specs=pl.BlockSpec((rb, C), lambda i: (i, 0)),
        out_shape=jax.ShapeDtypeStruct((R, C), F32),
        compiler_params=_params(("parallel",)), name=name,
    )(parts)


def _adamw(w, m, v, parts, *, name):
    L, R, C = w.shape
    n = parts[0].shape[0]
    cap = max(16, VMEM_LIMIT // 3 // (2 * L * n * C * parts[0].dtype.itemsize))
    rb = max([r for r in range(16, min(R, cap) + 1, 16) if R % r == 0], default=R)

    def body(w_ref, m_ref, v_ref, *rest):
        p_refs = rest[:L]
        g_ref, d_ref, nm_ref, nv_ref = rest[L:]
        for l in range(L):
            @pl.when(pl.program_id(0) == l)
            def _(p_ref=p_refs[l]):
                g = p_ref[0].astype(F32)
                for k in range(1, n):
                    g = g + p_ref[k].astype(F32)
                mn = ADAM_B1 * m_ref[...] + (1.0 - ADAM_B1) * g
                vn = ADAM_B2 * v_ref[...] + (1.0 - ADAM_B2) * jnp.square(g)
                m_hat = mn / (1.0 - ADAM_B1 ** ADAM_STEP)
                v_hat = vn / (1.0 - ADAM_B2 ** ADAM_STEP)
                g_ref[...] = g
                d_ref[...] = -ADAM_LR * (m_hat / (jnp.sqrt(v_hat) + ADAM_EPS) + ADAM_WD * w_ref[...])
                nm_ref[...] = mn
                nv_ref[...] = vn

    row = pl.BlockSpec((None, rb, C), lambda l, i: (l, i, 0))
    part = lambda k: pl.BlockSpec((n, rb, C), lambda l, i: (0, jnp.where(l == k, i, 0), 0))
    return pl.pallas_call(
        body, grid=(L, R // rb), in_specs=[row, row, row] + [part(k) for k in range(L)],
        out_specs=[row] * 4, out_shape=[jax.ShapeDtypeStruct((L, R, C), F32)] * 4,
        compiler_params=_params(("arbitrary", "arbitrary")), name=name,
    )(w, m, v, *parts)


BIG = (("a_w_in", "a_w_in", 1, False), ("a_w_out", "a_w_out", 0, False), ("kv_w", "kv_w", 0, False),
       ("b_w_q", "b_w_q_t", 0, True), ("b_w_o", "b_w_o", 1, False), ("ffn_w_up", "ffn_w_up_t", 0, True),
       ("ffn_w_down", "ffn_w_down", 0, False))
SMALL_SHARDED = (("a_ln_g", 1), ("a_ln_b", 1), ("ffn_conv_w", 2), ("ln_g", 2), ("ln_b", 2))
REPLICATED = ("a_w_s", "a_b_s", "rel_table", "ffn_conv_b")


def _pack_rows(arrs, lead=0):
    lshape = arrs[0].shape[:lead]
    p = jnp.concatenate([a.reshape(*lshape, -1, LANES) for a in arrs], axis=lead)
    pad = -p.shape[lead] % 8
    return jnp.pad(p, [(0, 0)] * lead + [(0, pad), (0, 0)])


def _unpack_rows(packed, shapes, lead=0):
    lshape = packed.shape[:lead]
    out, off = [], 0
    for shp in shapes:
        r = int(np.prod(shp)) // LANES
        out.append(lax.slice_in_dim(packed, off, off + r, axis=lead).reshape(*lshape, *shp))
        off += r
    return out


def _as_mats(a, transposed):
    a = a[None] if a.ndim == 2 else a
    return jnp.swapaxes(a, 1, 2) if transposed else a


def _merge_shards(stacked, axis):
    a = jnp.moveaxis(stacked, 0, axis)
    shp = list(a.shape)
    return a.reshape(shp[:axis] + [shp[axis] * shp[axis + 1]] + shp[axis + 2:])


def _split_shards(full, axis):
    shp = list(full.shape)
    a = full.reshape(shp[:axis] + [N_DEV, shp[axis] // N_DEV] + shp[axis + 1:])
    return jnp.moveaxis(a, axis, 0)


def kernel(x, a_w_in, a_ln_g, a_ln_b, a_w_s, a_b_s, a_w_out, kv_w, b_w_q, b_w_o, rel_table, ffn_w_up, ffn_conv_w, ffn_conv_b, ffn_w_down, ln_g, ln_b, loss_target, m_a_w_in, m_a_ln_g, m_a_ln_b, m_a_w_s, m_a_b_s, m_a_w_out, m_kv_w, m_b_w_q, m_b_w_o, m_rel_table, m_ffn_w_up, m_ffn_conv_w, m_ffn_conv_b, m_ffn_w_down, m_ln_g, m_ln_b, v_a_w_in, v_a_ln_g, v_a_ln_b, v_a_w_s, v_a_b_s, v_a_w_out, v_kv_w, v_b_w_q, v_b_w_o, v_rel_table, v_ffn_w_up, v_ffn_conv_w, v_ffn_conv_b, v_ffn_w_down, v_ln_g, v_ln_b):
    names = ["a_w_in", "a_ln_g", "a_ln_b", "a_w_s", "a_b_s", "a_w_out", "kv_w", "b_w_q", "b_w_o", "rel_table",
             "ffn_w_up", "ffn_conv_w", "ffn_conv_b", "ffn_w_down", "ln_g", "ln_b"]
    w = dict(zip(names, (a_w_in, a_ln_g, a_ln_b, a_w_s, a_b_s, a_w_out, kv_w, b_w_q, b_w_o, rel_table,
                         ffn_w_up, ffn_conv_w, ffn_conv_b, ffn_w_down, ln_g, ln_b)))
    m = dict(zip(names, (m_a_w_in, m_a_ln_g, m_a_ln_b, m_a_w_s, m_a_b_s, m_a_w_out, m_kv_w, m_b_w_q, m_b_w_o,
                         m_rel_table, m_ffn_w_up, m_ffn_conv_w, m_ffn_conv_b, m_ffn_w_down, m_ln_g, m_ln_b)))
    v = dict(zip(names, (v_a_w_in, v_a_ln_g, v_a_ln_b, v_a_w_s, v_a_b_s, v_a_w_out, v_kv_w, v_b_w_q, v_b_w_o,
                         v_rel_table, v_ffn_w_up, v_ffn_conv_w, v_ffn_conv_b, v_ffn_w_down, v_ln_g, v_ln_b)))
    small_names = [n for n, _ in SMALL_SHARDED]
    small_shapes = [w[n].shape for n in small_names]
    rep_shapes = [w[n].shape for n in REPLICATED]

    axis_of = {key: axis for _, key, axis, _ in BIG}
    src = {}
    for n, key, axis, tr in BIG:
        loc = _as_mats(w[n], tr).astype(BF16)
        for l in range(loc.shape[0]):
            src[(key, l)] = loc[l]
    order = []
    for i in range(DEPTH):
        if i < N_A:
            order += [[("a_w_in", i)], [("a_w_out", i)]]
        else:
            order += [([("kv_w", 0)] if i == N_A else []) + [("b_w_q_t", i - N_A)], [("b_w_o", i - N_A)]]
        order += [[("ffn_w_up_t", i)], [("ffn_w_down", i)]]
    small_src = _pack_rows([w[n] for n in small_names])
    srows = small_src.shape[0]
    handles, _ = _send_start([[(small_src, 0)]] + [[(src[kl], axis_of[kl[0]]) for kl in grp] for grp in order],
                             True, name="gather_start")
    small_all = _send_wait(handles[0], True, x, name="gather_wait_small")[0]
    small_st = _unpack_rows(small_all.reshape(N_DEV, srows, LANES), small_shapes, lead=1)
    base = {n: w[n] for n in REPLICATED}
    for (n, ax), st in zip(SMALL_SHARDED, small_st):
        base[n] = _merge_shards(st, ax)
    for n, key, _, tr in BIG:
        base[key] = [None] * (1 if w[n].ndim == 2 else w[n].shape[0])

    def fetch(group, after):
        mats = _send_wait(handles[1 + group], True, after, name=f"gather_wait_{group}")
        return dict(zip(order[group], mats))

    sent = {}

    def emit(group, mats):
        keys = list(mats)
        hs, token = _send_start([[(mats[kl], axis_of[kl[0]]) for kl in keys]], False, name=f"exchange_start_{group}")
        sent[group] = (keys, hs[0])
        return token[0, 0]

    loss, grad_x, grads = _local_step(x[0], loss_target[0], _Weights(base, fetch, emit))
    loss = lax.psum(loss, ("x", "y", "c"))

    out = {}
    small_pack = _pack_rows([_split_shards(grads[n], ax) for n, ax in SMALL_SHARDED], lead=1)
    rep_pack = _pack_rows([grads[n] for n in REPLICATED])
    mine = jnp.concatenate([small_pack.reshape(N_DEV * srows, LANES), rep_pack], axis=0)
    hs, _ = _send_start([[(mine, 0)]], True, name="small_grads_start")

    landed = {}
    for group in sorted(sent, reverse=True):
        keys, h = sent[group]
        landed.update(zip(keys, _send_wait(h, False, grad_x, name=f"exchange_wait_{group}")))
    last = grad_x
    for n, key, axis, tr in BIG:
        shp = w[n].shape
        parts = [landed[(key, l)] for l in range(1 if len(shp) == 2 else shp[0])]
        res = _adamw(_as_mats(w[n], tr), _as_mats(m[n], tr), _as_mats(v[n], tr), parts, name=f"adamw_{n}")
        out[n] = [(jnp.swapaxes(r, 1, 2) if tr else r).reshape(shp) for r in res]
        last = res[0]

    allp = _send_wait(hs[0], True, last, name="small_grads_wait")[0]
    gsum = _sum_parts(allp.reshape(N_DEV, mine.shape[0], LANES), name="sum_small_grads")
    g_small = lax.dynamic_slice_in_dim(gsum, _my_index() * srows, srows, axis=0)
    gs_in = jnp.concatenate([g_small, gsum[N_DEV * srows:]], axis=0)[None]
    pack_sr = lambda d: jnp.concatenate([_pack_rows([d[n] for n in small_names]),
                                         _pack_rows([d[n] for n in REPLICATED])], axis=0)
    res = _adamw(pack_sr(w)[None], pack_sr(m)[None], pack_sr(v)[None], [gs_in], name="adamw_small")
    for n, vals in zip(small_names, zip(*[_unpack_rows(r[0, :srows], small_shapes) for r in res])):
        out[n] = list(vals)
    for n, vals in zip(REPLICATED, zip(*[_unpack_rows(r[0, srows:], rep_shapes) for r in res])):
        out[n] = list(vals)

    return (loss, grad_x[None], *[out[n][0] for n in names], *[out[n][1] for n in names],
            *[out[n][2] for n in names], *[out[n][3] for n in names])
```

```python
import math

import numpy as np
import jax
import jax.numpy as jnp
from jax import lax
from jax.experimental import pallas as pl
from jax.experimental.pallas import tpu as pltpu

F32 = jnp.float32
BF16 = jnp.bfloat16
ACT = jnp.bfloat16
MESH = pl.DeviceIdType.MESH

N_DEV = 8
DEPTH = 4
N_A = 2
CHUNK = 128
BLK = 128
HEAD_DIM = 64
DILATED_GROUPS = ((128, 1), (512, 4), (2048, 16))
N_GROUPS = 3
REL_BUCKETS = 32
REL_MAX_DIST = 2048
ALPHA = (2 * DEPTH) ** 0.25
LN_EPS = 1e-5
NEG = -1e30
ADAM_LR = 0.001
ADAM_B1 = 0.9
ADAM_B2 = 0.999
ADAM_EPS = 1e-08
ADAM_WD = 0.01
ADAM_STEP = 10

LANES = 128
VMEM_LIMIT = 56 * 1024 * 1024
MM_TILE_CAP = 1408
INV_SQRT2 = 1.0 / math.sqrt(2.0)
INV_SQRT_2PI = 1.0 / math.sqrt(2.0 * math.pi)


def _pick(n, cap):
    best = None
    for t in range(LANES, min(n, cap) + 1, LANES):
        if n % t == 0:
            best = t
    return best if best is not None else n


def _params(sem):
    return pltpu.CompilerParams(dimension_semantics=sem, vmem_limit_bytes=VMEM_LIMIT)


def _gelu(x):
    return 0.5 * x * (1.0 + lax.erf(x * INV_SQRT2))


def _gelu_grad(x):
    return 0.5 * (1.0 + lax.erf(x * INV_SQRT2)) + x * jnp.exp(-0.5 * x * x) * INV_SQRT_2PI


def _mm(a, b, *, ta=False, tb=False, out_dtype=F32, scale=None, name):
    halves = isinstance(a, tuple)
    parts = 1 if halves or a.ndim == 2 else a.shape[0]
    ash = (a[0].shape[0], 2 * a[0].shape[1]) if halves else (a.shape if parts == 1
                                                               else (a.shape[1], parts * a.shape[2]))
    if ta:
        K, M = ash
    else:
        M, K = ash
    if tb:
        N, Kb = b.shape
    else:
        Kb, N = b.shape
    assert K == Kb, (ash, b.shape, ta, tb)
    tm, tn, tk = _pick(M, MM_TILE_CAP), _pick(N, MM_TILE_CAP), _pick(K, MM_TILE_CAP)
    split = 2 if halves else parts
    if split > 1 and ta:
        tm = _pick(M // split, MM_TILE_CAP)
    if split > 1 and not ta:
        tk = _pick(K // split, MM_TILE_CAP)
    nk = K // tk
    nh = (M // split // tm if ta else K // split // tk) if split > 1 else 0
    dn = (((0 if ta else 1,), (1 if tb else 0,)), ((), ()))

    def body(*refs):
        a_refs, (b_ref, o_ref, acc_ref) = refs[:-3], refs[-3:]
        k = pl.program_id(2)

        def accumulate(a_ref):
            part = lax.dot_general(a_ref[...].astype(BF16), b_ref[...].astype(BF16), dn,
                                   preferred_element_type=F32)

            @pl.when(k == 0)
            def _():
                acc_ref[...] = part

            @pl.when(k > 0)
            def _():
                acc_ref[...] += part

        if halves:
            first = (pl.program_id(0) if ta else k) < nh
            pl.when(first)(lambda: accumulate(a_refs[0]))
            pl.when(jnp.logical_not(first))(lambda: accumulate(a_refs[1]))
        else:
            accumulate(a_refs[0])

        @pl.when(k == nk - 1)
        def _():
            r = acc_ref[...]
            if scale is not None:
                r = r * scale
            o_ref[...] = r.astype(out_dtype)

    if halves and ta:
        a_specs = [pl.BlockSpec((tk, tm), lambda i, j, k: (jnp.where(i < nh, k, 0), jnp.minimum(i, nh - 1))),
                   pl.BlockSpec((tk, tm), lambda i, j, k: (jnp.where(i >= nh, k, 0), jnp.maximum(i - nh, 0)))]
    elif halves:
        a_specs = [pl.BlockSpec((tm, tk), lambda i, j, k: (i, jnp.minimum(k, nh - 1))),
                   pl.BlockSpec((tm, tk), lambda i, j, k: (i, jnp.maximum(k - nh, 0)))]
    elif parts > 1:
        a_specs = [pl.BlockSpec((None, tk, tm), lambda i, j, k: (i // nh, k, i % nh)) if ta
                   else pl.BlockSpec((None, tm, tk), lambda i, j, k: (k // nh, i, k % nh))]
    else:
        a_specs = [pl.BlockSpec((tk, tm), lambda i, j, k: (k, i)) if ta
                   else pl.BlockSpec((tm, tk), lambda i, j, k: (i, k))]
    b_spec = (pl.BlockSpec((tn, tk), lambda i, j, k: (j, k)) if tb
              else pl.BlockSpec((tk, tn), lambda i, j, k: (k, j)))
    return pl.pallas_call(
        body, grid=(M // tm, N // tn, nk), in_specs=[*a_specs, b_spec],
        out_specs=pl.BlockSpec((tm, tn), lambda i, j, k: (i, j)),
        out_shape=jax.ShapeDtypeStruct((M, N), out_dtype),
        scratch_shapes=[pltpu.VMEM((tm, tn), F32)],
        compiler_params=_params(("parallel", "parallel", "arbitrary")), name=name,
    )(*(a if halves else (a,)), b)


def _add_ln_fwd(x, h, g, b, *, name):
    T, D = x.shape
    rb = _pick(T, 512)

    def body(x_ref, h_ref, g_ref, b_ref, o_ref, ob_ref):
        pre = ALPHA * x_ref[...] + h_ref[...]
        mu = jnp.mean(pre, axis=1, keepdims=True)
        cen = pre - mu
        var = jnp.mean(cen * cen, axis=1, keepdims=True)
        y = cen * lax.rsqrt(var + LN_EPS) * g_ref[...] + b_ref[...]
        o_ref[...] = y
        ob_ref[...] = y.astype(BF16)

    row = pl.BlockSpec((rb, D), lambda i: (i, 0))
    vec = pl.BlockSpec((1, D), lambda i: (0, 0))
    return pl.pallas_call(
        body, grid=(T // rb,), in_specs=[row, row, vec, vec], out_specs=[row, row],
        out_shape=[jax.ShapeDtypeStruct((T, D), F32), jax.ShapeDtypeStruct((T, D), BF16)],
        compiler_params=_params(("parallel",)), name=name,
    )(x, h, g.reshape(1, D), b.reshape(1, D))


def _add_ln_bwd(x, h, g, terms, *, name):
    T, D = x.shape
    rb = _pick(T, 512)
    coefs = [c for c, _ in terms]
    nt = len(terms)

    def body(*refs):
        x_ref, h_ref, g_ref = refs[:3]
        t_refs = refs[3:3 + nt]
        dp_ref, dpb_ref, dg_ref, db_ref = refs[3 + nt:]
        dy = None
        for c, r in zip(coefs, t_refs):
            v = r[...] if c == 1.0 else c * r[...]
            dy = v if dy is None else dy + v
        pre = ALPHA * x_ref[...] + h_ref[...]
        mu = jnp.mean(pre, axis=1, keepdims=True)
        cen = pre - mu
        var = jnp.mean(cen * cen, axis=1, keepdims=True)
        rstd = lax.rsqrt(var + LN_EPS)
        xhat = cen * rstd
        dxh = dy * g_ref[...]
        m1 = jnp.mean(dxh, axis=1, keepdims=True)
        m2 = jnp.mean(dxh * xhat, axis=1, keepdims=True)
        dpre = rstd * (dxh - m1 - xhat * m2)
        dp_ref[...] = dpre
        dpb_ref[...] = dpre.astype(BF16)
        dg = jnp.sum(dy * xhat, axis=0, keepdims=True)
        db = jnp.sum(dy, axis=0, keepdims=True)

        @pl.when(pl.program_id(0) == 0)
        def _():
            dg_ref[...] = dg
            db_ref[...] = db

        @pl.when(pl.program_id(0) > 0)
        def _():
            dg_ref[...] += dg
            db_ref[...] += db

    row = pl.BlockSpec((rb, D), lambda i: (i, 0))
    vec = pl.BlockSpec((1, D), lambda i: (0, 0))
    return pl.pallas_call(
        body, grid=(T // rb,), in_specs=[row, row, vec] + [row] * nt,
        out_specs=[row, row, vec, vec],
        out_shape=[jax.ShapeDtypeStruct((T, D), F32), jax.ShapeDtypeStruct((T, D), BF16),
                   jax.ShapeDtypeStruct((1, D), F32), jax.ShapeDtypeStruct((1, D), F32)],
        compiler_params=_params(("arbitrary",)), name=name,
    )(x, h, g.reshape(1, D), *[a for _, a in terms])


def _lincomb(terms, out_dtype, *, name):
    R, C = terms[0][1].shape
    rb = _pick(R, 512)
    coefs = [c for c, _ in terms]
    nt = len(terms)

    def body(*refs):
        acc = None
        for c, r in zip(coefs, refs[:nt]):
            v = r[...].astype(F32)
            v = v if c == 1.0 else c * v
            acc = v if acc is None else acc + v
        refs[nt][...] = acc.astype(out_dtype)

    row = pl.BlockSpec((rb, C), lambda i: (i, 0))
    return pl.pallas_call(
        body, grid=(R // rb,), in_specs=[row] * nt, out_specs=row,
        out_shape=jax.ShapeDtypeStruct((R, C), out_dtype),
        compiler_params=_params(("parallel",)), name=name,
    )(*[a for _, a in terms])


def _loss_grad(y, tgt, *, name):
    T, D = y.shape
    rb = _pick(T, 512)

    def body(y_ref, t_ref, dy_ref, l_ref):
        err = y_ref[...] - t_ref[...]
        dy_ref[...] = err * (1.0 / D)
        part = jnp.sum(jnp.sum(err * err, axis=1, keepdims=True), axis=0, keepdims=True) * (0.5 / D)
        part = jnp.broadcast_to(part, (1, LANES))

        @pl.when(pl.program_id(0) == 0)
        def _():
            l_ref[...] = part

        @pl.when(pl.program_id(0) > 0)
        def _():
            l_ref[...] += part

    row = pl.BlockSpec((rb, D), lambda i: (i, 0))
    return pl.pallas_call(
        body, grid=(T // rb,), in_specs=[row, row],
        out_specs=[row, pl.BlockSpec((1, LANES), lambda i: (0, 0))],
        out_shape=[jax.ShapeDtypeStruct((T, D), F32), jax.ShapeDtypeStruct((1, LANES), F32)],
        compiler_params=_params(("arbitrary",)), name=name,
    )(y, tgt)


def _sgu_fwd(zp, ln_g, ln_b, ws, bst, *, name):
    T, E2 = zp.shape
    E = E2 // 2
    G = ws.shape[0]
    cg = E // G
    rb = 2 * CHUNK

    def body(z_ref, g_ref, b_ref, ws_ref, bs_ref, y_ref):
        u = _gelu(z_ref[:, :E].astype(F32))
        v = _gelu(z_ref[:, E:].astype(F32))
        mu = jnp.mean(v, axis=1, keepdims=True)
        cen = v - mu
        var = jnp.mean(cen * cen, axis=1, keepdims=True)
        vn = (cen * lax.rsqrt(var + LN_EPS) * g_ref[...] + b_ref[...]).astype(BF16)
        for ci in range(rb // CHUNK):
            rows = slice(ci * CHUNK, (ci + 1) * CHUNK)
            for gi in range(G):
                cols = slice(gi * cg, (gi + 1) * cg)
                sv = jnp.dot(ws_ref[gi], vn[rows, cols], preferred_element_type=F32)
                sv = sv + bs_ref[:, gi:gi + 1]
                y_ref[rows, cols] = (u[rows, cols] * sv).astype(BF16)

    return pl.pallas_call(
        body, grid=(T // rb,),
        in_specs=[pl.BlockSpec((rb, E2), lambda i: (i, 0)),
                  pl.BlockSpec((1, E), lambda i: (0, 0)), pl.BlockSpec((1, E), lambda i: (0, 0)),
                  pl.BlockSpec((G, CHUNK, CHUNK), lambda i: (0, 0, 0)),
                  pl.BlockSpec((CHUNK, G), lambda i: (0, 0))],
        out_specs=pl.BlockSpec((rb, E), lambda i: (i, 0)),
        out_shape=jax.ShapeDtypeStruct((T, E), BF16),
        compiler_params=_params(("parallel",)), name=name,
    )(zp, ln_g.reshape(1, E), ln_b.reshape(1, E), ws, bst)


def _sgu_bwd(zp, dy, ln_g, ln_b, ws, wst, bst, *, name):
    T, E2 = zp.shape
    E = E2 // 2
    G = ws.shape[0]
    cg = E // G
    rb = CHUNK
    nsteps = T // rb

    def body(z_ref, dy_ref, g_ref, b_ref, ws_ref, wst_ref, bs_ref,
             dz_ref, dg_ref, db_ref, dws_ref, dbs_ref, dsv_acc):
        step = pl.program_id(0)

        @pl.when(step == 0)
        def _():
            dg_ref[...] = jnp.zeros_like(dg_ref)
            db_ref[...] = jnp.zeros_like(db_ref)
            dws_ref[...] = jnp.zeros_like(dws_ref)
            dsv_acc[...] = jnp.zeros_like(dsv_acc)

        zu = z_ref[:, :E].astype(F32)
        zv = z_ref[:, E:].astype(F32)
        u = _gelu(zu)
        v = _gelu(zv)
        mu = jnp.mean(v, axis=1, keepdims=True)
        cen = v - mu
        var = jnp.mean(cen * cen, axis=1, keepdims=True)
        rstd = lax.rsqrt(var + LN_EPS)
        xhat = cen * rstd
        vn = (xhat * g_ref[...] + b_ref[...]).astype(BF16)
        dyv = dy_ref[...].astype(F32)
        dsv = dyv * u
        dsv_acc[...] += dsv
        dsvb = dsv.astype(BF16)
        tril = (lax.broadcasted_iota(jnp.int32, (CHUNK, CHUNK), 0)
                >= lax.broadcasted_iota(jnp.int32, (CHUNK, CHUNK), 1))
        du_parts = []
        dvn_parts = []
        for gi in range(G):
            cols = slice(gi * cg, (gi + 1) * cg)
            sv = jnp.dot(ws_ref[gi], vn[:, cols], preferred_element_type=F32) + bs_ref[:, gi:gi + 1]
            du_parts.append(dyv[:, cols] * sv)
            dvn_parts.append(jnp.dot(wst_ref[gi], dsvb[:, cols], preferred_element_type=F32))
            dw = lax.dot_general(dsvb[:, cols], vn[:, cols], (((1,), (1,)), ((), ())),
                                 preferred_element_type=F32)
            dws_ref[gi] += jnp.where(tril, dw, 0.0)
        du = jnp.concatenate(du_parts, axis=1)
        dvn = jnp.concatenate(dvn_parts, axis=1)
        dg_ref[...] += jnp.sum(dvn * xhat, axis=0, keepdims=True)
        db_ref[...] += jnp.sum(dvn, axis=0, keepdims=True)
        dxh = dvn * g_ref[...]
        m1 = jnp.mean(dxh, axis=1, keepdims=True)
        m2 = jnp.mean(dxh * xhat, axis=1, keepdims=True)
        dv = rstd * (dxh - m1 - xhat * m2)
        dz_ref[:, :E] = (du * _gelu_grad(zu)).astype(BF16)
        dz_ref[:, E:] = (dv * _gelu_grad(zv)).astype(BF16)

        @pl.when(step == nsteps - 1)
        def _():
            lane = lax.broadcasted_iota(jnp.int32, (CHUNK, LANES), 1)
            out = jnp.zeros((CHUNK, LANES), F32)
            for gi in range(G):
                s = jnp.sum(dsv_acc[:, gi * cg:(gi + 1) * cg], axis=1, keepdims=True)
                out = jnp.where(lane == gi, s, out)
            dbs_ref[...] = out

    vecE = pl.BlockSpec((1, E), lambda i: (0, 0))
    wspec = pl.BlockSpec((G, CHUNK, CHUNK), lambda i: (0, 0, 0))
    return pl.pallas_call(
        body, grid=(nsteps,),
        in_specs=[pl.BlockSpec((rb, E2), lambda i: (i, 0)), pl.BlockSpec((rb, E), lambda i: (i, 0)),
                  vecE, vecE, wspec, wspec, pl.BlockSpec((CHUNK, G), lambda i: (0, 0))],
        out_specs=[pl.BlockSpec((rb, E2), lambda i: (i, 0)), vecE, vecE, wspec,
                   pl.BlockSpec((CHUNK, LANES), lambda i: (0, 0))],
        out_shape=[jax.ShapeDtypeStruct((T, E2), BF16), jax.ShapeDtypeStruct((1, E), F32),
                   jax.ShapeDtypeStruct((1, E), F32), jax.ShapeDtypeStruct((G, CHUNK, CHUNK), F32),
                   jax.ShapeDtypeStruct((CHUNK, LANES), F32)],
        scratch_shapes=[pltpu.VMEM((CHUNK, E), F32)],
        compiler_params=_params(("arbitrary",)), name=name,
    )(zp, dy, ln_g.reshape(1, E), ln_b.reshape(1, E), ws, wst, bst)


def _shift_down(x, k, row):
    return jnp.where(row >= k, pltpu.roll(x, k, 0), 0.0)


def _shift_up(x, k, row, T):
    return jnp.where(row < T - k, pltpu.roll(x, T - k, 0), 0.0)


def _conv3(x, w_ref, b_ref, row):
    return (w_ref[0:1, :] * _shift_down(x, 2, row) + w_ref[1:2, :] * _shift_down(x, 1, row)
            + w_ref[2:3, :] * x + b_ref[...])


def _convgate_fwd(hh, cw, cb, *, name):
    T, F2 = hh.shape
    F = F2 // 2
    ns = F // LANES

    def body(a_ref, g_ref, wa_ref, wg_ref, ba_ref, bg_ref, o_ref):
        row = lax.broadcasted_iota(jnp.int32, (T, LANES), 0)
        ca = _conv3(a_ref[...].astype(F32), wa_ref, ba_ref, row)
        cgv = _conv3(g_ref[...].astype(F32), wg_ref, bg_ref, row)
        o_ref[...] = (_gelu(ca) * cgv).astype(BF16)

    sa = lambda r: pl.BlockSpec((r, LANES), lambda j: (0, j))
    sg = lambda r: pl.BlockSpec((r, LANES), lambda j: (0, j + ns))
    return pl.pallas_call(
        body, grid=(ns,), in_specs=[sa(T), sg(T), sa(3), sg(3), sa(1), sg(1)],
        out_specs=sa(T), out_shape=jax.ShapeDtypeStruct((T, F), BF16),
        compiler_params=_params(("parallel",)), name=name,
    )(hh, hh, cw, cw, cb, cb)


def _convgate_bwd(hh, dact, cw, cb, *, name):
    T, F2 = hh.shape
    F = F2 // 2
    ns = F // LANES

    def body(a_ref, g_ref, d_ref, wa_ref, wg_ref, ba_ref, bg_ref,
             da_ref, dg_ref, dwa_ref, dwg_ref, dba_ref, dbg_ref):
        row = lax.broadcasted_iota(jnp.int32, (T, LANES), 0)
        d = d_ref[...].astype(F32)
        ca = _conv3(a_ref[...].astype(F32), wa_ref, ba_ref, row)
        cgv = _conv3(g_ref[...].astype(F32), wg_ref, bg_ref, row)
        cdf = 0.5 * (1.0 + lax.erf(ca * INV_SQRT2))
        dca = d * cgv * (cdf + ca * jnp.exp(-0.5 * ca * ca) * INV_SQRT_2PI)
        dcg = d * (ca * cdf)
        for x_ref, w_ref, dc, dx_ref, dw_ref, db_ref in (
                (a_ref, wa_ref, dca, da_ref, dwa_ref, dba_ref),
                (g_ref, wg_ref, dcg, dg_ref, dwg_ref, dbg_ref)):
            x = x_ref[...].astype(F32)
            dx = (w_ref[2:3, :] * dc + w_ref[1:2, :] * _shift_up(dc, 1, row, T)
                  + w_ref[0:1, :] * _shift_up(dc, 2, row, T))
            dx_ref[...] = dx.astype(BF16)
            dw_ref[0:1, :] = jnp.sum(dc * _shift_down(x, 2, row), axis=0, keepdims=True)
            dw_ref[1:2, :] = jnp.sum(dc * _shift_down(x, 1, row), axis=0, keepdims=True)
            dw_ref[2:3, :] = jnp.sum(dc * x, axis=0, keepdims=True)
            db_ref[...] = jnp.sum(dc, axis=0, keepdims=True)

    sa = lambda r: pl.BlockSpec((r, LANES), lambda j: (0, j))
    sg = lambda r: pl.BlockSpec((r, LANES), lambda j: (0, j + ns))
    return pl.pallas_call(
        body, grid=(ns,), in_specs=[sa(T), sg(T), sa(T), sa(3), sg(3), sa(1), sg(1)],
        out_specs=[sa(T), sa(T), sa(3), sa(3), sa(1), sa(1)],
        out_shape=[jax.ShapeDtypeStruct((T, F), BF16), jax.ShapeDtypeStruct((T, F), BF16),
                   jax.ShapeDtypeStruct((3, F), F32), jax.ShapeDtypeStruct((3, F), F32),
                   jax.ShapeDtypeStruct((1, F), F32), jax.ShapeDtypeStruct((1, F), F32)],
        compiler_params=_params(("parallel",)), name=name,
    )(hh, hh, dact, cw, cw, cb, cb)


def _bucket_maps():
    iq = np.arange(BLK)[:, None]
    ik = np.arange(2 * BLK)[None, :]
    delta = iq + BLK - ik
    maps = []
    for win, dil in DILATED_GROUPS:
        n = np.clip(delta, 0, None) * dil
        max_exact = REL_BUCKETS // 2
        nf = np.maximum(n, 1).astype(np.float32)
        large = max_exact + (np.log(nf / np.float32(max_exact)) / np.float32(math.log(REL_MAX_DIST / max_exact))
                             * np.float32(REL_BUCKETS - max_exact)).astype(np.int32)
        large = np.minimum(large, REL_BUCKETS - 1)
        bucket = np.where(n < max_exact, n, large)
        valid = (delta >= 0) & (delta <= win // dil)
        maps.append(np.where(valid, bucket, -1).astype(np.int32))
    return np.stack(maps)


def _band_bias(rel_table, bmap, H, *, name):
    def body(t_ref, m_ref, o_ref):
        g = pl.program_id(0)
        bm = m_ref[0]
        for h in range(H):
            acc = jnp.full((BLK, 2 * BLK), NEG, F32)
            for b in range(REL_BUCKETS):
                acc = jnp.where(bm == b, t_ref[b, g * H + h], acc)
            o_ref[0, h] = acc

    return pl.pallas_call(
        body, grid=(N_GROUPS,),
        in_specs=[pl.BlockSpec(memory_space=pltpu.SMEM),
                  pl.BlockSpec((1, BLK, 2 * BLK), lambda g: (g, 0, 0))],
        out_specs=pl.BlockSpec((1, H, BLK, 2 * BLK), lambda g: (g, 0, 0, 0)),
        out_shape=jax.ShapeDtypeStruct((N_GROUPS, H, BLK, 2 * BLK), F32),
        compiler_params=_params(("parallel",)), name=name,
    )(rel_table, bmap)


def _band_bias_bwd(dbias, bmap, H, *, name):
    def body(d_ref, m_ref, o_ref):
        bm = m_ref[0]
        rowi = lax.broadcasted_iota(jnp.int32, (REL_BUCKETS, LANES), 0)
        lane = lax.broadcasted_iota(jnp.int32, (REL_BUCKETS, LANES), 1)
        out = jnp.zeros((REL_BUCKETS, LANES), F32)
        for h in range(H):
            dv = d_ref[0, h]
            for b in range(REL_BUCKETS):
                s = jnp.sum(jnp.sum(jnp.where(bm == b, dv, 0.0), axis=1, keepdims=True),
                            axis=0, keepdims=True)
                out = jnp.where((rowi == b) & (lane == h), s, out)
        o_ref[0] = out

    return pl.pallas_call(
        body, grid=(N_GROUPS,),
        in_specs=[pl.BlockSpec((1, H, BLK, 2 * BLK), lambda g: (g, 0, 0, 0)),
                  pl.BlockSpec((1, BLK, 2 * BLK), lambda g: (g, 0, 0))],
        out_specs=pl.BlockSpec((1, REL_BUCKETS, LANES), lambda g: (g, 0, 0)),
        out_shape=jax.ShapeDtypeStruct((N_GROUPS, REL_BUCKETS, LANES), F32),
        compiler_params=_params(("parallel",)), name=name,
    )(dbias, bmap)


def _head_masks():
    lane = lax.broadcasted_iota(jnp.int32, (BLK, LANES), 1)
    return (lane < HEAD_DIM, lane >= HEAD_DIM)


def _attn_fwd(q, kv, bias, gi, *, name):
    T = q.shape[0]
    HD = kv.shape[1] // 2
    d = DILATED_GROUPS[gi][1]
    S = T // d
    NB = S // BLK
    H = HD // HEAD_DIM
    qv, qcol = (q, gi) if d == 1 else (q[:, gi * HD:(gi + 1) * HD].reshape(S, d * HD), 0)
    kvv = kv.reshape(S, d * 2 * HD)

    def body(q_ref, kp_ref, kc_ref, vp_ref, vc_ref, b_ref, o_ref, l_ref):
        n = pl.program_id(1)
        col = lax.broadcasted_iota(jnp.int32, (BLK, 2 * BLK), 1)
        first = (n == 0) & (col < BLK)
        hm = _head_masks()
        for p in range(HD // LANES):
            sl = slice(p * LANES, (p + 1) * LANES)
            qp = q_ref[:, sl]
            kc = jnp.concatenate([kp_ref[:, sl], kc_ref[:, sl]], axis=0)
            vc = jnp.concatenate([vp_ref[:, sl], vc_ref[:, sl]], axis=0)
            outs = []
            lses = []
            for hh in range(2):
                qm = jnp.where(hm[hh], qp, jnp.zeros_like(qp))
                s = lax.dot_general(qm, kc, (((1,), (1,)), ((), ())), preferred_element_type=F32)
                s = jnp.where(first, NEG, s + b_ref[2 * p + hh])
                m = jnp.max(s, axis=1, keepdims=True)
                e = jnp.exp(s - m)
                den = jnp.sum(e, axis=1, keepdims=True)
                outs.append(jnp.dot((e / den).astype(BF16), vc, preferred_element_type=F32))
                lses.append(m + jnp.log(den))
            o_ref[:, sl] = jnp.where(hm[0], outs[0], outs[1])
            l_ref[:, sl] = jnp.where(hm[0], lses[0], lses[1])

    blk = lambda f: pl.BlockSpec((BLK, HD), f)
    prev = lambda n: jnp.maximum(n - 1, 0)
    return pl.pallas_call(
        body, grid=(d, NB),
        in_specs=[blk(lambda r, n: (n, r + qcol)),
                  blk(lambda r, n: (prev(n), r * 2)), blk(lambda r, n: (n, r * 2)),
                  blk(lambda r, n: (prev(n), r * 2 + 1)), blk(lambda r, n: (n, r * 2 + 1)),
                  pl.BlockSpec((H, BLK, 2 * BLK), lambda r, n: (0, 0, 0))],
        out_specs=[blk(lambda r, n: (n, r)), blk(lambda r, n: (n, r))],
        out_shape=[jax.ShapeDtypeStruct((S, d * HD), F32), jax.ShapeDtypeStruct((S, d * HD), F32)],
        compiler_params=_params(("parallel", "parallel")), name=name,
    )(qv, kvv, kvv, kvv, kvv, bias)


def _attn_combine(os, ls, *, name):
    T, HD = os[0].shape
    rb = _pick(T, 512)

    def body(o0, o1, o2, l0, l1, l2, o_ref, ob_ref, l_ref):
        la, lb, lc = l0[...], l1[...], l2[...]
        m = jnp.maximum(jnp.maximum(la, lb), lc)
        L = m + jnp.log(jnp.exp(la - m) + jnp.exp(lb - m) + jnp.exp(lc - m))
        o = jnp.exp(la - L) * o0[...] + jnp.exp(lb - L) * o1[...] + jnp.exp(lc - L) * o2[...]
        o_ref[...] = o
        ob_ref[...] = o.astype(BF16)
        l_ref[...] = L

    row = pl.BlockSpec((rb, HD), lambda i: (i, 0))
    return pl.pallas_call(
        body, grid=(T // rb,), in_specs=[row] * 6, out_specs=[row] * 3,
        out_shape=[jax.ShapeDtypeStruct((T, HD), F32), jax.ShapeDtypeStruct((T, HD), BF16),
                   jax.ShapeDtypeStruct((T, HD), F32)],
        compiler_params=_params(("parallel",)), name=name,
    )(*[a.reshape(T, HD) for a in os], *[a.reshape(T, HD) for a in ls])


def _attn_bwd(q, kv, bias, do, o, L, gi, *, name):
    T = q.shape[0]
    HD = kv.shape[1] // 2
    d = DILATED_GROUPS[gi][1]
    S = T // d
    NB = S // BLK
    H = HD // HEAD_DIM
    qv, qcol = (q, gi) if d == 1 else (q[:, gi * HD:(gi + 1) * HD].reshape(S, d * HD), 0)
    kvv = kv.reshape(S, d * 2 * HD)
    dov, ov, Lv = (a.reshape(S, d * HD) for a in (do, o, L))

    def body(q_ref, kp_ref, kc_ref, vp_ref, vc_ref, b_ref, do_ref, o_ref, L_ref,
             dq_ref, dk_ref, dv_ref, db_ref, ck_ref, cv_ref):
        r = pl.program_id(0)
        n = pl.program_id(1)

        @pl.when((r == 0) & (n == 0))
        def _():
            db_ref[...] = jnp.zeros_like(db_ref)

        @pl.when(n == 0)
        def _():
            ck_ref[...] = jnp.zeros_like(ck_ref)
            cv_ref[...] = jnp.zeros_like(cv_ref)

        @pl.when(n < NB)
        def _():
            col = lax.broadcasted_iota(jnp.int32, (BLK, 2 * BLK), 1)
            first = (n == 0) & (col < BLK)
            hm = _head_masks()
            for p in range(HD // LANES):
                sl = slice(p * LANES, (p + 1) * LANES)
                qp = q_ref[:, sl]
                kc = jnp.concatenate([kp_ref[:, sl], kc_ref[:, sl]], axis=0)
                vc = jnp.concatenate([vp_ref[:, sl], vc_ref[:, sl]], axis=0)
                dop = do_ref[:, sl]
                dob = dop.astype(BF16)
                prod = dop * o_ref[:, sl]
                Lp = L_ref[:, sl]
                dq_parts = []
                dkc = None
                dvc = None
                for hh in range(2):
                    qm = jnp.where(hm[hh], qp, jnp.zeros_like(qp))
                    dom = jnp.where(hm[hh], dob, jnp.zeros_like(dob))
                    s = lax.dot_general(qm, kc, (((1,), (1,)), ((), ())), preferred_element_type=F32)
                    s = jnp.where(first, NEG, s + b_ref[2 * p + hh])
                    lse = Lp[:, hh * HEAD_DIM:hh * HEAD_DIM + 1]
                    pr = jnp.exp(s - lse)
                    dp = lax.dot_general(dom, vc, (((1,), (1,)), ((), ())), preferred_element_type=F32)
                    delta = jnp.sum(jnp.where(hm[hh], prod, 0.0), axis=1, keepdims=True)
                    ds = pr * (dp - delta)
                    db_ref[2 * p + hh] += ds
                    dsb = ds.astype(BF16)
                    dq_parts.append(jnp.dot(dsb, kc, preferred_element_type=F32))
                    dkh = lax.dot_general(dsb, qm, (((0,), (0,)), ((), ())), preferred_element_type=F32)
                    dvh = lax.dot_general(pr.astype(BF16), dom, (((0,), (0,)), ((), ())),
                                          preferred_element_type=F32)
                    dkc = dkh if dkc is None else dkc + dkh
                    dvc = dvh if dvc is None else dvc + dvh
                dq = jnp.where(hm[0], dq_parts[0], dq_parts[1])
                dq_ref[:, sl] = (dq * (HEAD_DIM ** -0.5)).astype(BF16)
                dk_ref[:, sl] = ck_ref[:, sl] + dkc[:BLK]
                dv_ref[:, sl] = cv_ref[:, sl] + dvc[:BLK]
                ck_ref[:, sl] = dkc[BLK:]
                cv_ref[:, sl] = dvc[BLK:]

        @pl.when(n == NB)
        def _():
            dk_ref[...] = ck_ref[...]
            dv_ref[...] = cv_ref[...]

    blk = lambda f: pl.BlockSpec((BLK, HD), f)
    cur = lambda n: jnp.minimum(n, NB - 1)
    prev = lambda n: jnp.maximum(jnp.minimum(n, NB - 1) - 1, 0)
    lag = lambda n: jnp.maximum(n - 1, 0)
    return pl.pallas_call(
        body, grid=(d, NB + 1),
        in_specs=[blk(lambda r, n: (cur(n), r + qcol)),
                  blk(lambda r, n: (prev(n), r * 2)), blk(lambda r, n: (cur(n), r * 2)),
                  blk(lambda r, n: (prev(n), r * 2 + 1)), blk(lambda r, n: (cur(n), r * 2 + 1)),
                  pl.BlockSpec((H, BLK, 2 * BLK), lambda r, n: (0, 0, 0)),
                  blk(lambda r, n: (cur(n), r)), blk(lambda r, n: (cur(n), r)),
                  blk(lambda r, n: (cur(n), r))],
        out_specs=[blk(lambda r, n: (cur(n), r)), blk(lambda r, n: (lag(n), r)),
                   blk(lambda r, n: (lag(n), r)),
                   pl.BlockSpec((H, BLK, 2 * BLK), lambda r, n: (0, 0, 0))],
        out_shape=[jax.ShapeDtypeStruct((S, d * HD), BF16), jax.ShapeDtypeStruct((S, d * HD), F32),
                   jax.ShapeDtypeStruct((S, d * HD), F32),
                   jax.ShapeDtypeStruct((H, BLK, 2 * BLK), F32)],
        scratch_shapes=[pltpu.VMEM((BLK, HD), F32), pltpu.VMEM((BLK, HD), F32)],
        compiler_params=_params(("arbitrary", "arbitrary")), name=name,
    )(qv, kvv, kvv, kvv, kvv, bias, dov, ov, Lv)


SUPER = DILATED_GROUPS[-1][1] * BLK


def _band_rows(it, d):
    r, j = it % d, it // d
    if d == 1:
        at = lambda blk: pl.ds(pl.multiple_of(blk * BLK, BLK), BLK)
    else:
        at = lambda blk: pl.ds(r + d * BLK * blk, BLK, stride=d)
    return at(j), at(jnp.maximum(j - 1, 0))


def _band_loops(step, d):
    n_it = SUPER // BLK

    def run(lo, hi, inside):
        if hi > lo:
            def body(it, carry):
                step(it, inside)
                return carry
            lax.fori_loop(lo, hi, body, 0, unroll=max(u for u in (4, 3, 2, 1) if (hi - lo) % u == 0))

    run(0, d, False)
    run(d, n_it, True)


def _last_rows(it, d):
    m = SUPER // (d * BLK)
    if d == 1:
        return pl.ds((m - 1) * BLK, BLK)
    return pl.ds(it % d + d * BLK * (m - 1), BLK, stride=d)


def _attn_fwd_all(q, kv, bias, *, name):
    T = q.shape[0]
    HD = kv.shape[1] // 2
    PP = HD // LANES
    NS = T // SUPER

    def body(q0, q1, q2, kp_ref, kc_ref, vp_ref, vc_ref, b_ref, o_ref, ob_ref, l_ref, og, lg):
        n = pl.program_id(1)
        col = lax.broadcasted_iota(jnp.int32, (BLK, 2 * BLK), 1)
        hm = _head_masks()
        for g, (q_ref, (_, d)) in enumerate(zip((q0, q1, q2), DILATED_GROUPS)):
            def step(it, inside, g=g, q_ref=q_ref, d=d):
                cur, prv = _band_rows(it, d)
                qp = q_ref[cur, :].astype(BF16)
                if inside:
                    kprev, vprev = kc_ref[prv, :], vc_ref[prv, :]
                else:
                    last = _last_rows(it, d)
                    kprev, vprev = kp_ref[last, :], vp_ref[last, :]
                kc = jnp.concatenate([kprev.astype(BF16), kc_ref[cur, :].astype(BF16)], axis=0)
                vc = jnp.concatenate([vprev.astype(BF16), vc_ref[cur, :].astype(BF16)], axis=0)
                first = (n == 0) & (col < BLK)
                outs, lses = [], []
                for hh in range(2):
                    qm = jnp.where(hm[hh], qp, jnp.zeros_like(qp))
                    s = lax.dot_general(qm, kc, (((1,), (1,)), ((), ())), preferred_element_type=F32)
                    s = s + b_ref[g, hh]
                    if not inside:
                        s = jnp.where(first, NEG, s)
                    mx = jnp.max(s, axis=1, keepdims=True)
                    e = jnp.exp(s - mx)
                    den = jnp.sum(e, axis=1, keepdims=True)
                    outs.append(jnp.dot((e / den).astype(BF16), vc, preferred_element_type=F32))
                    lses.append(mx + jnp.log(den))
                og.at[g][cur, :] = jnp.where(hm[0], outs[0], outs[1])
                lg.at[g][cur, :] = jnp.where(hm[0], lses[0], lses[1])

            _band_loops(step, d)
        la, lb, lc = lg[0], lg[1], lg[2]
        mx = jnp.maximum(jnp.maximum(la, lb), lc)
        L = mx + jnp.log(jnp.exp(la - mx) + jnp.exp(lb - mx) + jnp.exp(lc - mx))
        o = jnp.exp(la - L) * og[0] + jnp.exp(lb - L) * og[1] + jnp.exp(lc - L) * og[2]
        o_ref[...] = o
        ob_ref[...] = o.astype(BF16)
        l_ref[...] = L

    blk = lambda f: pl.BlockSpec((SUPER, LANES), f)
    prev = lambda n: jnp.maximum(n - 1, 0)
    qspec = lambda g: blk(lambda p, n: (n, g * PP + p))
    return pl.pallas_call(
        body, grid=(PP, NS),
        in_specs=[qspec(0), qspec(1), qspec(2),
                  blk(lambda p, n: (prev(n), p)), blk(lambda p, n: (n, p)),
                  blk(lambda p, n: (prev(n), PP + p)), blk(lambda p, n: (n, PP + p)),
                  pl.BlockSpec((N_GROUPS, 2, BLK, 2 * BLK), lambda p, n: (0, p, 0, 0))],
        out_specs=[blk(lambda p, n: (n, p))] * 3,
        out_shape=[jax.ShapeDtypeStruct((T, HD), F32), jax.ShapeDtypeStruct((T, HD), BF16),
                   jax.ShapeDtypeStruct((T, HD), F32)],
        scratch_shapes=[pltpu.VMEM((N_GROUPS, SUPER, LANES), F32), pltpu.VMEM((N_GROUPS, SUPER, LANES), F32)],
        compiler_params=_params(("parallel", "parallel")), name=name,
    )(q, q, q, kv, kv, kv, kv, bias)


def _attn_bwd_all(q, kv, bias, do, o, L, *, name):
    T = q.shape[0]
    HD = kv.shape[1] // 2
    PP = HD // LANES
    H = HD // HEAD_DIM
    NS = T // SUPER

    def body(q0, q1, q2, kp_ref, kc_ref, vp_ref, vc_ref, b_ref, do_ref, o_ref, L_ref,
             dq_ref, dk_ref, dv_ref, db_ref, ck_ref, cv_ref):
        n = pl.program_id(1)

        @pl.when(n == 0)
        def _():
            db_ref[...] = jnp.zeros_like(db_ref)
            ck_ref[...] = jnp.zeros_like(ck_ref)
            cv_ref[...] = jnp.zeros_like(cv_ref)

        dk_ref[...] = ck_ref[...]
        dv_ref[...] = cv_ref[...]
        ck_ref[...] = jnp.zeros_like(ck_ref)
        cv_ref[...] = jnp.zeros_like(cv_ref)

        @pl.when(n < NS)
        def _():
            col = lax.broadcasted_iota(jnp.int32, (BLK, 2 * BLK), 1)
            hm = _head_masks()
            for g, (q_ref, (_, d)) in enumerate(zip((q0, q1, q2), DILATED_GROUPS)):
                def step(it, inside, g=g, q_ref=q_ref, d=d):
                    cur, prv = _band_rows(it, d)
                    last = _last_rows(it, d)
                    qp = q_ref[cur, :].astype(BF16)
                    if inside:
                        kprev, vprev = kc_ref[prv, :], vc_ref[prv, :]
                    else:
                        kprev, vprev = kp_ref[last, :], vp_ref[last, :]
                    kc = jnp.concatenate([kprev.astype(BF16), kc_ref[cur, :].astype(BF16)], axis=0)
                    vc = jnp.concatenate([vprev.astype(BF16), vc_ref[cur, :].astype(BF16)], axis=0)
                    first = (n == 0) & (col < BLK)
                    dop = do_ref[cur, :]
                    dob = dop.astype(BF16)
                    prod = dop * o_ref[cur, :]
                    Lp = L_ref[cur, :]
                    dq_parts = []
                    dkc = None
                    dvc = None
                    for hh in range(2):
                        qm = jnp.where(hm[hh], qp, jnp.zeros_like(qp))
                        dom = jnp.where(hm[hh], dob, jnp.zeros_like(dob))
                        s = lax.dot_general(qm, kc, (((1,), (1,)), ((), ())), preferred_element_type=F32)
                        s = s + b_ref[g, hh]
                        if not inside:
                            s = jnp.where(first, NEG, s)
                        lse = Lp[:, hh * HEAD_DIM:hh * HEAD_DIM + 1]
                        pr = jnp.exp(s - lse)
                        dp = lax.dot_general(dom, vc, (((1,), (1,)), ((), ())), preferred_element_type=F32)
                        delta = jnp.sum(jnp.where(hm[hh], prod, 0.0), axis=1, keepdims=True)
                        ds = pr * (dp - delta)
                        db_ref[g, hh] += ds
                        dsb = ds.astype(BF16)
                        dq_parts.append(jnp.dot(dsb, kc, preferred_element_type=F32))
                        dkh = lax.dot_general(dsb, qm, (((0,), (0,)), ((), ())), preferred_element_type=F32)
                        dvh = lax.dot_general(pr.astype(BF16), dom, (((0,), (0,)), ((), ())),
                                              preferred_element_type=F32)
                        dkc = dkh if dkc is None else dkc + dkh
                        dvc = dvh if dvc is None else dvc + dvh
                    dq_ref.at[g][cur, :] = jnp.where(hm[0], dq_parts[0], dq_parts[1]) * (HEAD_DIM ** -0.5)
                    ck_ref[cur, :] += dkc[BLK:]
                    cv_ref[cur, :] += dvc[BLK:]
                    if inside:
                        ck_ref[prv, :] += dkc[:BLK]
                        cv_ref[prv, :] += dvc[:BLK]
                    else:
                        dk_ref[last, :] += dkc[:BLK]
                        dv_ref[last, :] += dvc[:BLK]

                _band_loops(step, d)

    blk = lambda f: pl.BlockSpec((SUPER, LANES), f)
    cur = lambda n: jnp.minimum(n, NS - 1)
    prev = lambda n: jnp.maximum(jnp.minimum(n, NS - 1) - 1, 0)
    lag = lambda n: jnp.maximum(n - 1, 0)
    qspec = lambda g: blk(lambda p, n: (cur(n), g * PP + p))
    bspec = pl.BlockSpec((N_GROUPS, 2, BLK, 2 * BLK), lambda p, n: (0, p, 0, 0))
    return pl.pallas_call(
        body, grid=(PP, NS + 1),
        in_specs=[qspec(0), qspec(1), qspec(2),
                  blk(lambda p, n: (prev(n), p)), blk(lambda p, n: (cur(n), p)),
                  blk(lambda p, n: (prev(n), PP + p)), blk(lambda p, n: (cur(n), PP + p)),
                  bspec, blk(lambda p, n: (cur(n), p)), blk(lambda p, n: (cur(n), p)),
                  blk(lambda p, n: (cur(n), p))],
        out_specs=[pl.BlockSpec((N_GROUPS, SUPER, LANES), lambda p, n: (0, cur(n), p)),
                   blk(lambda p, n: (lag(n), p)), blk(lambda p, n: (lag(n), p)), bspec],
        out_shape=[jax.ShapeDtypeStruct((N_GROUPS, T, HD), F32), jax.ShapeDtypeStruct((T, HD), F32),
                   jax.ShapeDtypeStruct((T, HD), F32),
                   jax.ShapeDtypeStruct((N_GROUPS, H, BLK, 2 * BLK), F32)],
        scratch_shapes=[pltpu.VMEM((SUPER, LANES), F32), pltpu.VMEM((SUPER, LANES), F32)],
        compiler_params=_params(("arbitrary", "arbitrary")), name=name,
    )(q, q, q, kv, kv, kv, kv, bias, do, o, L)


class _Weights(dict):
    def __init__(self, base, fetch=None, emit=None):
        super().__init__(base)
        self._fetch, self._emit = fetch, emit

    def fetch(self, group, after):
        if self._fetch is not None:
            for (key, layer), mat in self._fetch(group, after).items():
                self[key][layer] = mat

    def emit(self, group, mats):
        return None if self._emit is None else self._emit(group, mats)


def _local_step(x, tgt, W):
    T, D = x.shape
    H = W["rel_table"].shape[1] // N_GROUPS
    HD = H * HEAD_DIM
    G = W["a_w_s"].shape[1]
    assert T % (DILATED_GROUPS[-1][1] * BLK) == 0

    tril = jnp.tril(jnp.ones((CHUNK, CHUNK), F32))
    bmap = jnp.asarray(_bucket_maps())
    bias = _band_bias(W["rel_table"], bmap, H, name="band_bias")

    saved = []
    xc, xcb = x, x.astype(BF16)
    kvb = None
    for i in range(DEPTH):
        s = {"x": xc, "xb": xcb}
        W.fetch(4 * i, xc)
        if i < N_A:
            ws_m = W["a_w_s"][i] * tril
            s["ws"] = ws_m.astype(BF16)
            s["wst"] = jnp.swapaxes(ws_m, 1, 2).astype(BF16)
            s["bst"] = W["a_b_s"][i].T
            s["zp"] = _mm(xcb, W["a_w_in"][i], out_dtype=ACT, name=f"a_in_{i}")
            s["y"] = _sgu_fwd(s["zp"], W["a_ln_g"][i], W["a_ln_b"][i], s["ws"], s["bst"], name=f"sgu_fwd_{i}")
            W.fetch(4 * i + 1, s["zp"])
            s["h"] = _mm(s["y"], W["a_w_out"][i], name=f"a_out_{i}")
        else:
            j = i - N_A
            if kvb is None:
                kvb = _mm(xcb, W["kv_w"][0], name="kv_proj")
            s["q"] = _mm(xcb, W["b_w_q_t"][j], tb=True, scale=HEAD_DIM ** -0.5, name=f"q_proj_{j}")
            s["o"], s["ob"], s["L"] = _attn_fwd_all(s["q"], kvb, bias, name=f"attn_fwd_{j}")
            W.fetch(4 * i + 1, s["q"])
            s["h"] = _mm(s["ob"], W["b_w_o"][j], name=f"o_proj_{j}")
        s["x1"], s["x1b"] = _add_ln_fwd(xc, s["h"], W["ln_g"][i, 0], W["ln_b"][i, 0], name=f"ln1_fwd_{i}")
        W.fetch(4 * i + 2, s["x1"])
        s["hh"] = _mm(s["x1b"], W["ffn_w_up_t"][i], tb=True, out_dtype=ACT, name=f"ffn_up_{i}")
        s["cw"] = W["ffn_conv_w"][i]
        s["cb"] = W["ffn_conv_b"][i].reshape(1, -1)
        s["act"] = _convgate_fwd(s["hh"], s["cw"], s["cb"], name=f"convgate_fwd_{i}")
        W.fetch(4 * i + 3, s["hh"])
        s["f"] = _mm(s["act"], W["ffn_w_down"][i], name=f"ffn_down_{i}")
        xc, xcb = _add_ln_fwd(s["x1"], s["f"], W["ln_g"][i, 1], W["ln_b"][i, 1], name=f"ln2_fwd_{i}")
        saved.append(s)

    dy, lossv = _loss_grad(xc, tgt, name="loss_grad")
    loss = lossv[0, 0]

    gl = {k: [None] * DEPTH for k in ("ffn_w_up_t", "ffn_conv_w", "ffn_conv_b", "ffn_w_down", "ln_g", "ln_b")}
    ga = {k: [None] * N_A for k in ("a_w_in", "a_ln_g", "a_ln_b", "a_w_s", "a_b_s", "a_w_out")}
    gb = {k: [None] * (DEPTH - N_A) for k in ("b_w_q_t", "b_w_o")}
    mats = ("a_w_in", "a_w_out", "b_w_q_t", "b_w_o", "ffn_w_up_t", "ffn_w_down")
    dks, dvs, dbias = [], [], []
    grads = {}
    terms = [(1.0, dy)]
    tok = None
    behind = lambda g: g if tok is None else g + tok
    for i in reversed(range(DEPTH)):
        s = saved[i]
        dp2, dp2b, dg2, db2 = _add_ln_bwd(s["x1"], s["f"], behind(W["ln_g"][i, 1]), terms, name=f"ln2_bwd_{i}")
        dact = _mm(dp2b, W["ffn_w_down"][i], tb=True, out_dtype=ACT, name=f"ffn_down_dx_{i}")
        gl["ffn_w_down"][i] = _mm(s["act"], dp2b, ta=True, out_dtype=BF16, name=f"ffn_down_dw_{i}")
        dha, dhg, dwa, dwg, dba, dbg = _convgate_bwd(s["hh"], dact, s["cw"], s["cb"], name=f"convgate_bwd_{i}")
        dhh = (dha, dhg)
        gl["ffn_conv_w"][i] = jnp.concatenate([dwa, dwg], axis=1)
        gl["ffn_conv_b"][i] = jnp.concatenate([dba, dbg], axis=1)[0]
        dx1 = _mm(dhh, W["ffn_w_up_t"][i], name=f"ffn_up_dx_{i}")
        gl["ffn_w_up_t"][i] = _mm(dhh, s["x1b"], ta=True, out_dtype=BF16, name=f"ffn_up_dw_{i}")
        tok = W.emit(3 * i + 2, {("ffn_w_up_t", i): gl["ffn_w_up_t"][i], ("ffn_w_down", i): gl["ffn_w_down"][i]})
        dp1, dp1b, dg1, db1 = _add_ln_bwd(s["x"], s["h"], behind(W["ln_g"][i, 0]), [(ALPHA, dp2), (1.0, dx1)],
                                          name=f"ln1_bwd_{i}")
        gl["ln_g"][i] = jnp.concatenate([dg1, dg2], axis=0)
        gl["ln_b"][i] = jnp.concatenate([db1, db2], axis=0)
        terms = [(ALPHA, dp1)]
        if i < N_A:
            dyy = _mm(dp1b, W["a_w_out"][i], tb=True, out_dtype=ACT, name=f"a_out_dx_{i}")
            ga["a_w_out"][i] = _mm(s["y"], dp1b, ta=True, out_dtype=BF16, name=f"a_out_dw_{i}")
            tok = W.emit(3 * i + 1, {("a_w_out", i): ga["a_w_out"][i]})
            dzp, dlg, dlb, dws, dbs = _sgu_bwd(s["zp"], dyy, behind(W["a_ln_g"][i]), W["a_ln_b"][i], s["ws"],
                                               s["wst"], s["bst"], name=f"sgu_bwd_{i}")
            ga["a_ln_g"][i], ga["a_ln_b"][i], ga["a_w_s"][i] = dlg[0], dlb[0], dws
            ga["a_b_s"][i] = dbs[:, :G].T
            terms.append((1.0, _mm(dzp, W["a_w_in"][i], tb=True, name=f"a_in_dx_{i}")))
            ga["a_w_in"][i] = _mm(s["xb"], dzp, ta=True, out_dtype=BF16, name=f"a_in_dw_{i}")
            tok = W.emit(3 * i, {("a_w_in", i): ga["a_w_in"][i]})
        else:
            j = i - N_A
            do = _mm(dp1b, W["b_w_o"][j], tb=True, name=f"o_proj_dx_{j}")
            gb["b_w_o"][j] = _mm(s["ob"], dp1b, ta=True, out_dtype=BF16, name=f"o_proj_dw_{j}")
            tok = W.emit(3 * i + 1, {("b_w_o", j): gb["b_w_o"][j]})
            bias_b = behind(bias)
            dq, dk_j, dv_j, db_j = _attn_bwd_all(s["q"], kvb, bias_b, do, s["o"], s["L"], name=f"attn_bwd_{j}")
            dks.append((1.0, dk_j))
            dvs.append((1.0, dv_j))
            dbias.append(db_j)
            terms.append((1.0, _mm(dq, W["b_w_q_t"][j], name=f"q_proj_dx_{j}")))
            gb["b_w_q_t"][j] = _mm(dq, s["xb"], ta=True, out_dtype=BF16, name=f"q_proj_dw_{j}")
            out_b = {("b_w_q_t", j): gb["b_w_q_t"][j]}
            if i == N_A:
                dkv = jnp.concatenate([_lincomb(dks, BF16, name="dk_sum"), _lincomb(dvs, BF16, name="dv_sum")],
                                      axis=1)
                terms.append((1.0, _mm(dkv, W["kv_w"][0], tb=True, name="kv_proj_dx")))
                grads["kv_w"] = [_mm(s["xb"], dkv, ta=True, out_dtype=BF16, name="kv_proj_dw")]
                out_b[("kv_w", 0)] = grads["kv_w"][0]
                dbt = _lincomb([(1.0, a.reshape(-1, 2 * BLK)) for a in dbias], F32, name="dbias_sum")
                dtab = _band_bias_bwd(dbt.reshape(N_GROUPS, H, BLK, 2 * BLK), bmap, H, name="band_bias_bwd")
                grads["rel_table"] = jnp.transpose(dtab[:, :, :H], (1, 0, 2)).reshape(REL_BUCKETS, N_GROUPS * H)
            tok = W.emit(3 * i, out_b)
    grad_x = _lincomb(terms, F32, name="grad_x")
    for dct in (gl, ga, gb):
        for k, v in dct.items():
            grads[k] = v if k in mats else jnp.stack(v)
    return loss, grad_x, grads


def _my_index():
    return 4 * lax.axis_index("x") + 2 * lax.axis_index("y") + lax.axis_index("c")


HBM_SPEC = pl.BlockSpec(memory_space=pltpu.HBM)


def _block(ref, k, n, axis):
    off = pl.multiple_of(k * n, n)
    return ref.at[pl.ds(off, n), :] if axis == 0 else ref.at[:, pl.ds(off, n)]


def _gather_mats(local, axis, *, name):
    L, a, b = local.shape
    n = a if axis == 0 else b
    full = (a * N_DEV, b) if axis == 0 else (a, b * N_DEV)

    def body(x_ref, *rest):
        outs = rest[:L]
        send_sems, recv_sems, local_sems = rest[L:]
        x, y, c = lax.axis_index("x"), lax.axis_index("y"), lax.axis_index("c")
        me, sibling = (x, y, c), (x, y, 1 - c)
        chips = [(1 - x, y), (x, 1 - y), (1 - x, 1 - y)]

        def slot(l, px, py, pc):
            return _block(outs[l], 4 * px + 2 * py + pc, n, axis)

        def copy(l, k, blk, to, src=None):
            return pltpu.make_async_remote_copy(
                src_ref=slot(l, *blk) if src is None else src, dst_ref=slot(l, *blk),
                send_sem=send_sems.at[7 * l + k], recv_sem=recv_sems.at[7 * l + k],
                device_id=to, device_id_type=MESH)

        mine, first, passed = [], [], []
        for l in range(L):
            mine.append(pltpu.make_async_copy(x_ref.at[l], slot(l, *me), local_sems.at[l]))
            mine[-1].start()
            first.append(copy(l, 0, me, sibling, src=x_ref.at[l]))
            first += [copy(l, 1 + j, me, (*chip, c), src=x_ref.at[l]) for j, chip in enumerate(chips)]
        for cp in first:
            cp.start()
        for l in range(L):
            for j, chip in enumerate(chips):
                copy(l, 1 + j, (*chip, c), me).wait_recv()
                passed.append(copy(l, 4 + j, (*chip, c), sibling))
                passed[-1].start()
        for l in range(L):
            copy(l, 0, sibling, me).wait_recv()
            for j, chip in enumerate(chips):
                copy(l, 4 + j, (*chip, 1 - c), me).wait_recv()
        for cp in first + passed:
            cp.wait_send()
        for cp in mine:
            cp.wait()

    return pl.pallas_call(
        body, out_shape=[jax.ShapeDtypeStruct(full, local.dtype)] * L,
        in_specs=[HBM_SPEC], out_specs=[HBM_SPEC] * L,
        scratch_shapes=[pltpu.SemaphoreType.DMA((7 * L,)), pltpu.SemaphoreType.DMA((7 * L,)),
                        pltpu.SemaphoreType.DMA((L,))],
        name=name,
    )(local)


SEM_SPEC = pl.BlockSpec(memory_space=pltpu.SEMAPHORE)
FLOWING = pltpu.SideEffectType.DATAFLOW_SIDE_EFFECTING


def _peers(x, y, c):
    return [(1 - x if k & 4 else x, 1 - y if k & 2 else y, 1 - c if k & 1 else c) for k in range(1, N_DEV)]


def _ends(src_ref, land_ref, peer_index, me, n, axis, gather):
    if gather:
        return src_ref, _block(land_ref, me, n, axis)
    return _block(src_ref, peer_index, n, axis), land_ref.at[me]


def _send_start(groups, gather, *, name):
    flat = [(g, j, mat, axis) for g, items in enumerate(groups) for j, (mat, axis) in enumerate(items)]
    M, G = len(flat), len(groups)
    lands, ns = [], []
    for _, _, mat, axis in flat:
        A, B = mat.shape
        if gather:
            lands.append((A * N_DEV, B) if axis == 0 else (A, B * N_DEV))
            ns.append(A if axis == 0 else B)
        else:
            lands.append((N_DEV, A // N_DEV, B) if axis == 0 else (N_DEV, A, B // N_DEV))
            ns.append(A // N_DEV if axis == 0 else B // N_DEV)

    def body(*refs):
        src_refs, land_refs, sems = refs[:M], refs[M:2 * M], refs[2 * M:2 * M + 3 * G]
        token = refs[-1]
        x, y, c = lax.axis_index("x"), lax.axis_index("y"), lax.axis_index("c")
        me = 4 * x + 2 * y + c
        for i, (g, j, _, axis) in enumerate(flat):
            for k, (px, py, pc) in enumerate(_peers(x, y, c)):
                s, d = _ends(src_refs[i], land_refs[i], 4 * px + 2 * py + pc, me, ns[i], axis, gather)
                pltpu.make_async_remote_copy(
                    src_ref=s, dst_ref=d, send_sem=sems[3 * g].at[7 * j + k], recv_sem=sems[3 * g + 1].at[7 * j + k],
                    device_id=(px, py, pc), device_id_type=MESH).start()
            s, d = _ends(src_refs[i], land_refs[i], me, me, ns[i], axis, gather)
            pltpu.make_async_copy(s, d, sems[3 * g + 2].at[j]).start()
        token[...] = jnp.zeros_like(token)

    sem_shapes = []
    for items in groups:
        sem_shapes += [pltpu.SemaphoreType.DMA((7 * len(items),))] * 2 + [pltpu.SemaphoreType.DMA((len(items),))]
    outs = pl.pallas_call(
        body, name=name,
        out_shape=(*sem_shapes, *[pltpu.HBM(m.shape, m.dtype) for _, _, m, _ in flat],
                   *[pltpu.HBM(shp, m.dtype) for shp, (_, _, m, _) in zip(lands, flat)],
                   jax.ShapeDtypeStruct((8, LANES), F32)),
        in_specs=[HBM_SPEC] * (2 * M),
        out_specs=(*[SEM_SPEC] * (3 * G), *[HBM_SPEC] * (2 * M), pl.BlockSpec(memory_space=pltpu.VMEM)),
        input_output_aliases={i: 3 * G + i for i in range(2 * M)},
        compiler_params=pltpu.CompilerParams(has_side_effects=FLOWING),
    )(*[pltpu.with_memory_space_constraint(m, pltpu.HBM) for _, _, m, _ in flat],
      *[pltpu.with_memory_space_constraint(lax.empty(shp, m.dtype), pltpu.HBM)
        for shp, (_, _, m, _) in zip(lands, flat)])
    handles = []
    for g in range(G):
        idx = [i for i, f in enumerate(flat) if f[0] == g]
        handles.append((outs[3 * g], outs[3 * g + 1], outs[3 * g + 2], [outs[3 * G + i] for i in idx],
                        [outs[3 * G + M + i] for i in idx], [flat[i][3] for i in idx]))
    return handles, outs[-1]


def _send_wait(handle, gather, after, *, name):
    send_sems, recv_sems, local_sems, mats, lands, axes = handle
    n_m = len(mats)
    ns = []
    for mat, land, axis in zip(mats, lands, axes):
        ns.append(mat.shape[axis] if gather else land.shape[1 + axis])

    def body(*refs):
        src_refs, land_refs = refs[:n_m], refs[n_m:2 * n_m]
        ssem, rsem, lsem = refs[2 * n_m:2 * n_m + 3]
        x, y, c = lax.axis_index("x"), lax.axis_index("y"), lax.axis_index("c")
        me = 4 * x + 2 * y + c
        for j in range(n_m):
            for k, (px, py, pc) in enumerate(_peers(x, y, c)):
                s, d = _ends(src_refs[j], land_refs[j], 4 * px + 2 * py + pc, me, ns[j], axes[j], gather)
                cp = pltpu.make_async_remote_copy(
                    src_ref=s, dst_ref=d, send_sem=ssem.at[7 * j + k], recv_sem=rsem.at[7 * j + k],
                    device_id=(px, py, pc), device_id_type=MESH)
                cp.wait_send()
                cp.wait_recv()
            s, d = _ends(src_refs[j], land_refs[j], me, me, ns[j], axes[j], gather)
            pltpu.make_async_copy(s, d, lsem.at[j]).wait()

    outs = pl.pallas_call(
        body, name=name,
        out_shape=(*[pltpu.HBM(m.shape, m.dtype) for m in mats], *[pltpu.HBM(l.shape, l.dtype) for l in lands]),
        in_specs=[HBM_SPEC] * (2 * n_m) + [SEM_SPEC] * 3 + [pl.BlockSpec(memory_space=pl.ANY)],
        out_specs=tuple([HBM_SPEC] * (2 * n_m)),
        input_output_aliases={i: i for i in range(2 * n_m)},
        compiler_params=pltpu.CompilerParams(has_side_effects=FLOWING),
    )(*mats, *lands, send_sems, recv_sems, local_sems, after)
    return list(outs[n_m:])


def _sum_parts(parts, *, name):
    n, R, C = parts.shape
    rb = _pick(R, 512) if R % LANES == 0 else R

    def body(p_ref, o_ref):
        acc = p_ref[0].astype(F32)
        for k in range(1, n):
            acc = acc + p_ref[k].astype(F32)
        o_ref[...] = acc

    return pl.pallas_call(
        body, grid=(R // rb,), in_specs=[pl.BlockSpec((n, rb, C), lambda i: (0, i, 0))],
        out_specs=pl.BlockSpec((rb, C), lambda i: (i, 0)),
        out_shape=jax.ShapeDtypeStruct((R, C), F32),
        compiler_params=_params(("parallel",)), name=name,
    )(parts)


def _adamw(w, m, v, parts, *, name):
    L, R, C = w.shape
    n = parts[0].shape[0]
    cap = max(16, VMEM_LIMIT // 3 // (2 * L * n * C * parts[0].dtype.itemsize))
    rb = max([r for r in range(16, min(R, cap) + 1, 16) if R % r == 0], default=R)

    def body(w_ref, m_ref, v_ref, *rest):
        p_refs = rest[:L]
        g_ref, d_ref, nm_ref, nv_ref = rest[L:]
        for l in range(L):
            @pl.when(pl.program_id(0) == l)
            def _(p_ref=p_refs[l]):
                g = p_ref[0].astype(F32)
                for k in range(1, n):
                    g = g + p_ref[k].astype(F32)
                mn = ADAM_B1 * m_ref[...] + (1.0 - ADAM_B1) * g
                vn = ADAM_B2 * v_ref[...] + (1.0 - ADAM_B2) * jnp.square(g)
                m_hat = mn / (1.0 - ADAM_B1 ** ADAM_STEP)
                v_hat = vn / (1.0 - ADAM_B2 ** ADAM_STEP)
                g_ref[...] = g
                d_ref[...] = -ADAM_LR * (m_hat / (jnp.sqrt(v_hat) + ADAM_EPS) + ADAM_WD * w_ref[...])
                nm_ref[...] = mn
                nv_ref[...] = vn

    row = pl.BlockSpec((None, rb, C), lambda l, i: (l, i, 0))
    part = lambda k: pl.BlockSpec((n, rb, C), lambda l, i: (0, jnp.where(l == k, i, 0), 0))
    return pl.pallas_call(
        body, grid=(L, R // rb), in_specs=[row, row, row] + [part(k) for k in range(L)],
        out_specs=[row] * 4, out_shape=[jax.ShapeDtypeStruct((L, R, C), F32)] * 4,
        compiler_params=_params(("arbitrary", "arbitrary")), name=name,
    )(w, m, v, *parts)


BIG = (("a_w_in", "a_w_in", 1, False), ("a_w_out", "a_w_out", 0, False), ("kv_w", "kv_w", 0, False),
       ("b_w_q", "b_w_q_t", 0, True), ("b_w_o", "b_w_o", 1, False), ("ffn_w_up", "ffn_w_up_t", 0, True),
       ("ffn_w_down", "ffn_w_down", 0, False))
SMALL_SHARDED = (("a_ln_g", 1), ("a_ln_b", 1), ("ffn_conv_w", 2), ("ln_g", 2), ("ln_b", 2))
REPLICATED = ("a_w_s", "a_b_s", "rel_table", "ffn_conv_b")


def _pack_rows(arrs, lead=0):
    lshape = arrs[0].shape[:lead]
    p = jnp.concatenate([a.reshape(*lshape, -1, LANES) for a in arrs], axis=lead)
    pad = -p.shape[lead] % 8
    return jnp.pad(p, [(0, 0)] * lead + [(0, pad), (0, 0)])


def _unpack_rows(packed, shapes, lead=0):
    lshape = packed.shape[:lead]
    out, off = [], 0
    for shp in shapes:
        r = int(np.prod(shp)) // LANES
        out.append(lax.slice_in_dim(packed, off, off + r, axis=lead).reshape(*lshape, *shp))
        off += r
    return out


def _as_mats(a, transposed):
    a = a[None] if a.ndim == 2 else a
    return jnp.swapaxes(a, 1, 2) if transposed else a


def _merge_shards(stacked, axis):
    a = jnp.moveaxis(stacked, 0, axis)
    shp = list(a.shape)
    return a.reshape(shp[:axis] + [shp[axis] * shp[axis + 1]] + shp[axis + 2:])


def _split_shards(full, axis):
    shp = list(full.shape)
    a = full.reshape(shp[:axis] + [N_DEV, shp[axis] // N_DEV] + shp[axis + 1:])
    return jnp.moveaxis(a, axis, 0)


def kernel(x, a_w_in, a_ln_g, a_ln_b, a_w_s, a_b_s, a_w_out, kv_w, b_w_q, b_w_o, rel_table, ffn_w_up, ffn_conv_w, ffn_conv_b, ffn_w_down, ln_g, ln_b, loss_target, m_a_w_in, m_a_ln_g, m_a_ln_b, m_a_w_s, m_a_b_s, m_a_w_out, m_kv_w, m_b_w_q, m_b_w_o, m_rel_table, m_ffn_w_up, m_ffn_conv_w, m_ffn_conv_b, m_ffn_w_down, m_ln_g, m_ln_b, v_a_w_in, v_a_ln_g, v_a_ln_b, v_a_w_s, v_a_b_s, v_a_w_out, v_kv_w, v_b_w_q, v_b_w_o, v_rel_table, v_ffn_w_up, v_ffn_conv_w, v_ffn_conv_b, v_ffn_w_down, v_ln_g, v_ln_b):
    names = ["a_w_in", "a_ln_g", "a_ln_b", "a_w_s", "a_b_s", "a_w_out", "kv_w", "b_w_q", "b_w_o", "rel_table",
             "ffn_w_up", "ffn_conv_w", "ffn_conv_b", "ffn_w_down", "ln_g", "ln_b"]
    w = dict(zip(names, (a_w_in, a_ln_g, a_ln_b, a_w_s, a_b_s, a_w_out, kv_w, b_w_q, b_w_o, rel_table,
                         ffn_w_up, ffn_conv_w, ffn_conv_b, ffn_w_down, ln_g, ln_b)))
    m = dict(zip(names, (m_a_w_in, m_a_ln_g, m_a_ln_b, m_a_w_s, m_a_b_s, m_a_w_out, m_kv_w, m_b_w_q, m_b_w_o,
                         m_rel_table, m_ffn_w_up, m_ffn_conv_w, m_ffn_conv_b, m_ffn_w_down, m_ln_g, m_ln_b)))
    v = dict(zip(names, (v_a_w_in, v_a_ln_g, v_a_ln_b, v_a_w_s, v_a_b_s, v_a_w_out, v_kv_w, v_b_w_q, v_b_w_o,
                         v_rel_table, v_ffn_w_up, v_ffn_conv_w, v_ffn_conv_b, v_ffn_w_down, v_ln_g, v_ln_b)))
    small_names = [n for n, _ in SMALL_SHARDED]
    small_shapes = [w[n].shape for n in small_names]
    rep_shapes = [w[n].shape for n in REPLICATED]

    axis_of = {key: axis for _, key, axis, _ in BIG}
    src = {}
    for n, key, axis, tr in BIG:
        loc = _as_mats(w[n], tr).astype(BF16)
        for l in range(loc.shape[0]):
            src[(key, l)] = loc[l]
    order = []
    for i in range(DEPTH):
        if i < N_A:
            order += [[("a_w_in", i)], [("a_w_out", i)]]
        else:
            order += [([("kv_w", 0)] if i == N_A else []) + [("b_w_q_t", i - N_A)], [("b_w_o", i - N_A)]]
        order += [[("ffn_w_up_t", i)], [("ffn_w_down", i)]]
    small_src = _pack_rows([w[n] for n in small_names])
    srows = small_src.shape[0]
    handles, _ = _send_start([[(small_src, 0)]] + [[(src[kl], axis_of[kl[0]]) for kl in grp] for grp in order],
                             True, name="gather_start")
    small_all = _send_wait(handles[0], True, x, name="gather_wait_small")[0]
    small_st = _unpack_rows(small_all.reshape(N_DEV, srows, LANES), small_shapes, lead=1)
    base = {n: w[n] for n in REPLICATED}
    for (n, ax), st in zip(SMALL_SHARDED, small_st):
        base[n] = _merge_shards(st, ax)
    for n, key, _, tr in BIG:
        base[key] = [None] * (1 if w[n].ndim == 2 else w[n].shape[0])

    def fetch(group, after):
        mats = _send_wait(handles[1 + group], True, after, name=f"gather_wait_{group}")
        return dict(zip(order[group], mats))

    sent = {}

    def emit(group, mats):
        keys = list(mats)
        hs, token = _send_start([[(mats[kl], axis_of[kl[0]]) for kl in keys]], False, name=f"exchange_start_{group}")
        sent[group] = (keys, hs[0])
        return token[0, 0]

    loss, grad_x, grads = _local_step(x[0], loss_target[0], _Weights(base, fetch, emit))
    loss = lax.psum(loss, ("x", "y", "c"))

    out = {}
    small_pack = _pack_rows([_split_shards(grads[n], ax) for n, ax in SMALL_SHARDED], lead=1)
    rep_pack = _pack_rows([grads[n] for n in REPLICATED])
    mine = jnp.concatenate([small_pack.reshape(N_DEV * srows, LANES), rep_pack], axis=0)
    hs, _ = _send_start([[(mine, 0)]], True, name="small_grads_start")

    landed = {}
    for group in sorted(sent, reverse=True):
        keys, h = sent[group]
        landed.update(zip(keys, _send_wait(h, False, grad_x, name=f"exchange_wait_{group}")))
    last = grad_x
    for n, key, axis, tr in BIG:
        shp = w[n].shape
        parts = [landed[(key, l)] for l in range(1 if len(shp) == 2 else shp[0])]
        res = _adamw(_as_mats(w[n], tr), _as_mats(m[n], tr), _as_mats(v[n], tr), parts, name=f"adamw_{n}")
        out[n] = [(jnp.swapaxes(r, 1, 2) if tr else r).reshape(shp) for r in res]
        last = res[0]

    allp = _send_wait(hs[0], True, last, name="small_grads_wait")[0]
    gsum = _sum_parts(allp.reshape(N_DEV, mine.shape[0], LANES), name="sum_small_grads")
    g_small = lax.dynamic_slice_in_dim(gsum, _my_index() * srows, srows, axis=0)
    gs_in = jnp.concatenate([g_small, gsum[N_DEV * srows:]], axis=0)[None]
    pack_sr = lambda d: jnp.concatenate([_pack_rows([d[n] for n in small_names]),
                                         _pack_rows([d[n] for n in REPLICATED])], axis=0)
    res = _adamw(pack_sr(w)[None], pack_sr(m)[None], pack_sr(v)[None], [gs_in], name="adamw_small")
    for n, vals in zip(small_names, zip(*[_unpack_rows(r[0, :srows], small_shapes) for r in res])):
        out[n] = list(vals)
    for n, vals in zip(REPLICATED, zip(*[_unpack_rows(r[0, srows:], rep_shapes) for r in res])):
        out[n] = list(vals)

    return (loss, grad_x[None], *[out[n][0] for n in names], *[out[n][1] for n in names],
            *[out[n][2] for n in names], *[out[n][3] for n in names])
```

```python
import math

import numpy as np
import jax
import jax.numpy as jnp
from jax import lax
from jax.experimental import pallas as pl
from jax.experimental.pallas import tpu as pltpu

F32 = jnp.float32
BF16 = jnp.bfloat16
ACT = jnp.bfloat16
MESH = pl.DeviceIdType.MESH

N_DEV = 8
DEPTH = 4
N_A = 2
CHUNK = 128
BLK = 128
HEAD_DIM = 64
DILATED_GROUPS = ((128, 1), (512, 4), (2048, 16))
N_GROUPS = 3
REL_BUCKETS = 32
REL_MAX_DIST = 2048
ALPHA = (2 * DEPTH) ** 0.25
LN_EPS = 1e-5
NEG = -1e30
ADAM_LR = 0.001
ADAM_B1 = 0.9
ADAM_B2 = 0.999
ADAM_EPS = 1e-08
ADAM_WD = 0.01
ADAM_STEP = 10

LANES = 128
VMEM_LIMIT = 56 * 1024 * 1024
MM_TILE_CAP = 1408
INV_SQRT2 = 1.0 / math.sqrt(2.0)
INV_SQRT_2PI = 1.0 / math.sqrt(2.0 * math.pi)


def _pick(n, cap):
    best = None
    for t in range(LANES, min(n, cap) + 1, LANES):
        if n % t == 0:
            best = t
    return best if best is not None else n


def _params(sem):
    return pltpu.CompilerParams(dimension_semantics=sem, vmem_limit_bytes=VMEM_LIMIT)


def _ordered(body, in_specs, args, after):
    if after is None:
        return body, list(in_specs), tuple(args)
    return (lambda _, *refs: body(*refs)), [pl.BlockSpec(memory_space=pl.ANY), *in_specs], (after, *args)


def _gelu(x):
    return 0.5 * x * (1.0 + lax.erf(x * INV_SQRT2))


def _gelu_grad(x):
    return 0.5 * (1.0 + lax.erf(x * INV_SQRT2)) + x * jnp.exp(-0.5 * x * x) * INV_SQRT_2PI


def _mm(a, b, *, ta=False, tb=False, out_dtype=F32, scale=None, after=None, name):
    halves = isinstance(a, tuple)
    parts = 1 if halves or a.ndim == 2 else a.shape[0]
    ash = (a[0].shape[0], 2 * a[0].shape[1]) if halves else (a.shape if parts == 1
                                                               else (a.shape[1], parts * a.shape[2]))
    if ta:
        K, M = ash
    else:
        M, K = ash
    if tb:
        N, Kb = b.shape
    else:
        Kb, N = b.shape
    assert K == Kb, (ash, b.shape, ta, tb)
    tm, tn, tk = _pick(M, MM_TILE_CAP), _pick(N, MM_TILE_CAP), _pick(K, MM_TILE_CAP)
    split = 2 if halves else parts
    if split > 1 and ta:
        tm = _pick(M // split, MM_TILE_CAP)
    if split > 1 and not ta:
        tk = _pick(K // split, MM_TILE_CAP)
    nk = K // tk
    nh = (M // split // tm if ta else K // split // tk) if split > 1 else 0
    dn = (((0 if ta else 1,), (1 if tb else 0,)), ((), ()))

    def body(*refs):
        a_refs, (b_ref, o_ref, acc_ref) = refs[:-3], refs[-3:]
        k = pl.program_id(2)

        def accumulate(a_ref):
            part = lax.dot_general(a_ref[...].astype(BF16), b_ref[...].astype(BF16), dn,
                                   preferred_element_type=F32)

            @pl.when(k == 0)
            def _():
                acc_ref[...] = part

            @pl.when(k > 0)
            def _():
                acc_ref[...] += part

        if halves:
            first = (pl.program_id(0) if ta else k) < nh
            pl.when(first)(lambda: accumulate(a_refs[0]))
            pl.when(jnp.logical_not(first))(lambda: accumulate(a_refs[1]))
        else:
            accumulate(a_refs[0])

        @pl.when(k == nk - 1)
        def _():
            r = acc_ref[...]
            if scale is not None:
                r = r * scale
            o_ref[...] = r.astype(out_dtype)

    if halves and ta:
        a_specs = [pl.BlockSpec((tk, tm), lambda i, j, k: (jnp.where(i < nh, k, 0), jnp.minimum(i, nh - 1))),
                   pl.BlockSpec((tk, tm), lambda i, j, k: (jnp.where(i >= nh, k, 0), jnp.maximum(i - nh, 0)))]
    elif halves:
        a_specs = [pl.BlockSpec((tm, tk), lambda i, j, k: (i, jnp.minimum(k, nh - 1))),
                   pl.BlockSpec((tm, tk), lambda i, j, k: (i, jnp.maximum(k - nh, 0)))]
    elif parts > 1:
        a_specs = [pl.BlockSpec((None, tk, tm), lambda i, j, k: (i // nh, k, i % nh)) if ta
                   else pl.BlockSpec((None, tm, tk), lambda i, j, k: (k // nh, i, k % nh))]
    else:
        a_specs = [pl.BlockSpec((tk, tm), lambda i, j, k: (k, i)) if ta
                   else pl.BlockSpec((tm, tk), lambda i, j, k: (i, k))]
    b_spec = (pl.BlockSpec((tn, tk), lambda i, j, k: (j, k)) if tb
              else pl.BlockSpec((tk, tn), lambda i, j, k: (k, j)))
    body, in_specs, args = _ordered(body, [*a_specs, b_spec], (*(a if halves else (a,)), b), after)
    return pl.pallas_call(
        body, grid=(M // tm, N // tn, nk), in_specs=in_specs,
        out_specs=pl.BlockSpec((tm, tn), lambda i, j, k: (i, j)),
        out_shape=jax.ShapeDtypeStruct((M, N), out_dtype),
        scratch_shapes=[pltpu.VMEM((tm, tn), F32)],
        compiler_params=_params(("parallel", "parallel", "arbitrary")), name=name,
    )(*args)


def _add_ln_fwd(x, h, g, b, *, name):
    T, D = x.shape
    rb = _pick(T, 512)

    def body(x_ref, h_ref, g_ref, b_ref, o_ref, ob_ref):
        pre = ALPHA * x_ref[...] + h_ref[...]
        mu = jnp.mean(pre, axis=1, keepdims=True)
        cen = pre - mu
        var = jnp.mean(cen * cen, axis=1, keepdims=True)
        y = cen * lax.rsqrt(var + LN_EPS) * g_ref[...] + b_ref[...]
        o_ref[...] = y
        ob_ref[...] = y.astype(BF16)

    row = pl.BlockSpec((rb, D), lambda i: (i, 0))
    vec = pl.BlockSpec((1, D), lambda i: (0, 0))
    return pl.pallas_call(
        body, grid=(T // rb,), in_specs=[row, row, vec, vec], out_specs=[row, row],
        out_shape=[jax.ShapeDtypeStruct((T, D), F32), jax.ShapeDtypeStruct((T, D), BF16)],
        compiler_params=_params(("parallel",)), name=name,
    )(x, h, g.reshape(1, D), b.reshape(1, D))


def _add_ln_bwd(x, h, g, terms, *, after=None, name):
    T, D = x.shape
    rb = _pick(T, 512)
    coefs = [c for c, _ in terms]
    nt = len(terms)

    def body(*refs):
        x_ref, h_ref, g_ref = refs[:3]
        t_refs = refs[3:3 + nt]
        dp_ref, dpb_ref, dg_ref, db_ref = refs[3 + nt:]
        dy = None
        for c, r in zip(coefs, t_refs):
            v = r[...] if c == 1.0 else c * r[...]
            dy = v if dy is None else dy + v
        pre = ALPHA * x_ref[...] + h_ref[...]
        mu = jnp.mean(pre, axis=1, keepdims=True)
        cen = pre - mu
        var = jnp.mean(cen * cen, axis=1, keepdims=True)
        rstd = lax.rsqrt(var + LN_EPS)
        xhat = cen * rstd
        dxh = dy * g_ref[...]
        m1 = jnp.mean(dxh, axis=1, keepdims=True)
        m2 = jnp.mean(dxh * xhat, axis=1, keepdims=True)
        dpre = rstd * (dxh - m1 - xhat * m2)
        dp_ref[...] = dpre
        dpb_ref[...] = dpre.astype(BF16)
        dg = jnp.sum(dy * xhat, axis=0, keepdims=True)
        db = jnp.sum(dy, axis=0, keepdims=True)

        @pl.when(pl.program_id(0) == 0)
        def _():
            dg_ref[...] = dg
            db_ref[...] = db

        @pl.when(pl.program_id(0) > 0)
        def _():
            dg_ref[...] += dg
            db_ref[...] += db

    row = pl.BlockSpec((rb, D), lambda i: (i, 0))
    vec = pl.BlockSpec((1, D), lambda i: (0, 0))
    body, in_specs, args = _ordered(body, [row, row, vec] + [row] * nt,
                                    (x, h, g.reshape(1, D), *[a for _, a in terms]), after)
    return pl.pallas_call(
        body, grid=(T // rb,), in_specs=in_specs,
        out_specs=[row, row, vec, vec],
        out_shape=[jax.ShapeDtypeStruct((T, D), F32), jax.ShapeDtypeStruct((T, D), BF16),
                   jax.ShapeDtypeStruct((1, D), F32), jax.ShapeDtypeStruct((1, D), F32)],
        compiler_params=_params(("arbitrary",)), name=name,
    )(*args)


def _lincomb(terms, out_dtype, *, name):
    R, C = terms[0][1].shape
    rb = _pick(R, 512)
    coefs = [c for c, _ in terms]
    nt = len(terms)

    def body(*refs):
        acc = None
        for c, r in zip(coefs, refs[:nt]):
            v = r[...].astype(F32)
            v = v if c == 1.0 else c * v
            acc = v if acc is None else acc + v
        refs[nt][...] = acc.astype(out_dtype)

    row = pl.BlockSpec((rb, C), lambda i: (i, 0))
    return pl.pallas_call(
        body, grid=(R // rb,), in_specs=[row] * nt, out_specs=row,
        out_shape=jax.ShapeDtypeStruct((R, C), out_dtype),
        compiler_params=_params(("parallel",)), name=name,
    )(*[a for _, a in terms])


def _loss_grad(y, tgt, *, name):
    T, D = y.shape
    rb = _pick(T, 512)

    def body(y_ref, t_ref, dy_ref, l_ref):
        err = y_ref[...] - t_ref[...]
        dy_ref[...] = err * (1.0 / D)
        part = jnp.sum(jnp.sum(err * err, axis=1, keepdims=True), axis=0, keepdims=True) * (0.5 / D)
        part = jnp.broadcast_to(part, (1, LANES))

        @pl.when(pl.program_id(0) == 0)
        def _():
            l_ref[...] = part

        @pl.when(pl.program_id(0) > 0)
        def _():
            l_ref[...] += part

    row = pl.BlockSpec((rb, D), lambda i: (i, 0))
    return pl.pallas_call(
        body, grid=(T // rb,), in_specs=[row, row],
        out_specs=[row, pl.BlockSpec((1, LANES), lambda i: (0, 0))],
        out_shape=[jax.ShapeDtypeStruct((T, D), F32), jax.ShapeDtypeStruct((1, LANES), F32)],
        compiler_params=_params(("arbitrary",)), name=name,
    )(y, tgt)


def _sgu_fwd(zp, ln_g, ln_b, ws, bst, *, name):
    T, E2 = zp.shape
    E = E2 // 2
    G = ws.shape[0]
    cg = E // G
    rb = 2 * CHUNK

    def body(z_ref, g_ref, b_ref, ws_ref, bs_ref, y_ref):
        u = _gelu(z_ref[:, :E].astype(F32))
        v = _gelu(z_ref[:, E:].astype(F32))
        mu = jnp.mean(v, axis=1, keepdims=True)
        cen = v - mu
        var = jnp.mean(cen * cen, axis=1, keepdims=True)
        vn = (cen * lax.rsqrt(var + LN_EPS) * g_ref[...] + b_ref[...]).astype(BF16)
        for ci in range(rb // CHUNK):
            rows = slice(ci * CHUNK, (ci + 1) * CHUNK)
            for gi in range(G):
                cols = slice(gi * cg, (gi + 1) * cg)
                sv = jnp.dot(ws_ref[gi], vn[rows, cols], preferred_element_type=F32)
                sv = sv + bs_ref[:, gi:gi + 1]
                y_ref[rows, cols] = (u[rows, cols] * sv).astype(BF16)

    return pl.pallas_call(
        body, grid=(T // rb,),
        in_specs=[pl.BlockSpec((rb, E2), lambda i: (i, 0)),
                  pl.BlockSpec((1, E), lambda i: (0, 0)), pl.BlockSpec((1, E), lambda i: (0, 0)),
                  pl.BlockSpec((G, CHUNK, CHUNK), lambda i: (0, 0, 0)),
                  pl.BlockSpec((CHUNK, G), lambda i: (0, 0))],
        out_specs=pl.BlockSpec((rb, E), lambda i: (i, 0)),
        out_shape=jax.ShapeDtypeStruct((T, E), BF16),
        compiler_params=_params(("parallel",)), name=name,
    )(zp, ln_g.reshape(1, E), ln_b.reshape(1, E), ws, bst)


def _sgu_bwd(zp, dy, ln_g, ln_b, ws, wst, bst, *, after=None, name):
    T, E2 = zp.shape
    E = E2 // 2
    G = ws.shape[0]
    cg = E // G
    rb = CHUNK
    nsteps = T // rb

    def body(z_ref, dy_ref, g_ref, b_ref, ws_ref, wst_ref, bs_ref,
             dz_ref, dg_ref, db_ref, dws_ref, dbs_ref, dsv_acc):
        step = pl.program_id(0)

        @pl.when(step == 0)
        def _():
            dg_ref[...] = jnp.zeros_like(dg_ref)
            db_ref[...] = jnp.zeros_like(db_ref)
            dws_ref[...] = jnp.zeros_like(dws_ref)
            dsv_acc[...] = jnp.zeros_like(dsv_acc)

        zu = z_ref[:, :E].astype(F32)
        zv = z_ref[:, E:].astype(F32)
        u = _gelu(zu)
        v = _gelu(zv)
        mu = jnp.mean(v, axis=1, keepdims=True)
        cen = v - mu
        var = jnp.mean(cen * cen, axis=1, keepdims=True)
        rstd = lax.rsqrt(var + LN_EPS)
        xhat = cen * rstd
        vn = (xhat * g_ref[...] + b_ref[...]).astype(BF16)
        dyv = dy_ref[...].astype(F32)
        dsv = dyv * u
        dsv_acc[...] += dsv
        dsvb = dsv.astype(BF16)
        tril = (lax.broadcasted_iota(jnp.int32, (CHUNK, CHUNK), 0)
                >= lax.broadcasted_iota(jnp.int32, (CHUNK, CHUNK), 1))
        du_parts = []
        dvn_parts = []
        for gi in range(G):
            cols = slice(gi * cg, (gi + 1) * cg)
            sv = jnp.dot(ws_ref[gi], vn[:, cols], preferred_element_type=F32) + bs_ref[:, gi:gi + 1]
            du_parts.append(dyv[:, cols] * sv)
            dvn_parts.append(jnp.dot(wst_ref[gi], dsvb[:, cols], preferred_element_type=F32))
            dw = lax.dot_general(dsvb[:, cols], vn[:, cols], (((1,), (1,)), ((), ())),
                                 preferred_element_type=F32)
            dws_ref[gi] += jnp.where(tril, dw, 0.0)
        du = jnp.concatenate(du_parts, axis=1)
        dvn = jnp.concatenate(dvn_parts, axis=1)
        dg_ref[...] += jnp.sum(dvn * xhat, axis=0, keepdims=True)
        db_ref[...] += jnp.sum(dvn, axis=0, keepdims=True)
        dxh = dvn * g_ref[...]
        m1 = jnp.mean(dxh, axis=1, keepdims=True)
        m2 = jnp.mean(dxh * xhat, axis=1, keepdims=True)
        dv = rstd * (dxh - m1 - xhat * m2)
        dz_ref[:, :E] = (du * _gelu_grad(zu)).astype(BF16)
        dz_ref[:, E:] = (dv * _gelu_grad(zv)).astype(BF16)

        @pl.when(step == nsteps - 1)
        def _():
            lane = lax.broadcasted_iota(jnp.int32, (CHUNK, LANES), 1)
            out = jnp.zeros((CHUNK, LANES), F32)
            for gi in range(G):
                s = jnp.sum(dsv_acc[:, gi * cg:(gi + 1) * cg], axis=1, keepdims=True)
                out = jnp.where(lane == gi, s, out)
            dbs_ref[...] = out

    vecE = pl.BlockSpec((1, E), lambda i: (0, 0))
    wspec = pl.BlockSpec((G, CHUNK, CHUNK), lambda i: (0, 0, 0))
    body, in_specs, args = _ordered(
        body, [pl.BlockSpec((rb, E2), lambda i: (i, 0)), pl.BlockSpec((rb, E), lambda i: (i, 0)),
               vecE, vecE, wspec, wspec, pl.BlockSpec((CHUNK, G), lambda i: (0, 0))],
        (zp, dy, ln_g.reshape(1, E), ln_b.reshape(1, E), ws, wst, bst), after)
    return pl.pallas_call(
        body, grid=(nsteps,), in_specs=in_specs,
        out_specs=[pl.BlockSpec((rb, E2), lambda i: (i, 0)), vecE, vecE, wspec,
                   pl.BlockSpec((CHUNK, LANES), lambda i: (0, 0))],
        out_shape=[jax.ShapeDtypeStruct((T, E2), BF16), jax.ShapeDtypeStruct((1, E), F32),
                   jax.ShapeDtypeStruct((1, E), F32), jax.ShapeDtypeStruct((G, CHUNK, CHUNK), F32),
                   jax.ShapeDtypeStruct((CHUNK, LANES), F32)],
        scratch_shapes=[pltpu.VMEM((CHUNK, E), F32)],
        compiler_params=_params(("arbitrary",)), name=name,
    )(*args)


def _shift_down(x, k, row):
    return jnp.where(row >= k, pltpu.roll(x, k, 0), 0.0)


def _shift_up(x, k, row, T):
    return jnp.where(row < T - k, pltpu.roll(x, T - k, 0), 0.0)


def _conv3(x, w_ref, b_ref, row):
    return (w_ref[0:1, :] * _shift_down(x, 2, row) + w_ref[1:2, :] * _shift_down(x, 1, row)
            + w_ref[2:3, :] * x + b_ref[...])


def _convgate_fwd(hh, cw, cb, *, name):
    T, F2 = hh.shape
    F = F2 // 2
    ns = F // LANES

    def body(a_ref, g_ref, wa_ref, wg_ref, ba_ref, bg_ref, o_ref):
        row = lax.broadcasted_iota(jnp.int32, (T, LANES), 0)
        ca = _conv3(a_ref[...].astype(F32), wa_ref, ba_ref, row)
        cgv = _conv3(g_ref[...].astype(F32), wg_ref, bg_ref, row)
        o_ref[...] = (_gelu(ca) * cgv).astype(BF16)

    sa = lambda r: pl.BlockSpec((r, LANES), lambda j: (0, j))
    sg = lambda r: pl.BlockSpec((r, LANES), lambda j: (0, j + ns))
    return pl.pallas_call(
        body, grid=(ns,), in_specs=[sa(T), sg(T), sa(3), sg(3), sa(1), sg(1)],
        out_specs=sa(T), out_shape=jax.ShapeDtypeStruct((T, F), BF16),
        compiler_params=_params(("parallel",)), name=name,
    )(hh, hh, cw, cw, cb, cb)


def _convgate_bwd(hh, dact, cw, cb, *, name):
    T, F2 = hh.shape
    F = F2 // 2
    ns = F // LANES

    def body(a_ref, g_ref, d_ref, wa_ref, wg_ref, ba_ref, bg_ref,
             da_ref, dg_ref, dwa_ref, dwg_ref, dba_ref, dbg_ref):
        row = lax.broadcasted_iota(jnp.int32, (T, LANES), 0)
        d = d_ref[...].astype(F32)
        ca = _conv3(a_ref[...].astype(F32), wa_ref, ba_ref, row)
        cgv = _conv3(g_ref[...].astype(F32), wg_ref, bg_ref, row)
        cdf = 0.5 * (1.0 + lax.erf(ca * INV_SQRT2))
        dca = d * cgv * (cdf + ca * jnp.exp(-0.5 * ca * ca) * INV_SQRT_2PI)
        dcg = d * (ca * cdf)
        for x_ref, w_ref, dc, dx_ref, dw_ref, db_ref in (
                (a_ref, wa_ref, dca, da_ref, dwa_ref, dba_ref),
                (g_ref, wg_ref, dcg, dg_ref, dwg_ref, dbg_ref)):
            x = x_ref[...].astype(F32)
            dx = (w_ref[2:3, :] * dc + w_ref[1:2, :] * _shift_up(dc, 1, row, T)
                  + w_ref[0:1, :] * _shift_up(dc, 2, row, T))
            dx_ref[...] = dx.astype(BF16)
            dw_ref[0:1, :] = jnp.sum(dc * _shift_down(x, 2, row), axis=0, keepdims=True)
            dw_ref[1:2, :] = jnp.sum(dc * _shift_down(x, 1, row), axis=0, keepdims=True)
            dw_ref[2:3, :] = jnp.sum(dc * x, axis=0, keepdims=True)
            db_ref[...] = jnp.sum(dc, axis=0, keepdims=True)

    sa = lambda r: pl.BlockSpec((r, LANES), lambda j: (0, j))
    sg = lambda r: pl.BlockSpec((r, LANES), lambda j: (0, j + ns))
    return pl.pallas_call(
        body, grid=(ns,), in_specs=[sa(T), sg(T), sa(T), sa(3), sg(3), sa(1), sg(1)],
        out_specs=[sa(T), sa(T), sa(3), sa(3), sa(1), sa(1)],
        out_shape=[jax.ShapeDtypeStruct((T, F), BF16), jax.ShapeDtypeStruct((T, F), BF16),
                   jax.ShapeDtypeStruct((3, F), F32), jax.ShapeDtypeStruct((3, F), F32),
                   jax.ShapeDtypeStruct((1, F), F32), jax.ShapeDtypeStruct((1, F), F32)],
        compiler_params=_params(("parallel",)), name=name,
    )(hh, hh, dact, cw, cw, cb, cb)


def _bucket_maps():
    iq = np.arange(BLK)[:, None]
    ik = np.arange(2 * BLK)[None, :]
    delta = iq + BLK - ik
    maps = []
    for win, dil in DILATED_GROUPS:
        n = np.clip(delta, 0, None) * dil
        max_exact = REL_BUCKETS // 2
        nf = np.maximum(n, 1).astype(np.float32)
        large = max_exact + (np.log(nf / np.float32(max_exact)) / np.float32(math.log(REL_MAX_DIST / max_exact))
                             * np.float32(REL_BUCKETS - max_exact)).astype(np.int32)
        large = np.minimum(large, REL_BUCKETS - 1)
        bucket = np.where(n < max_exact, n, large)
        valid = (delta >= 0) & (delta <= win // dil)
        maps.append(np.where(valid, bucket, -1).astype(np.int32))
    return np.stack(maps)


def _band_bias(rel_table, bmap, H, *, name):
    def body(t_ref, m_ref, o_ref):
        g = pl.program_id(0)
        bm = m_ref[0]
        for h in range(H):
            acc = jnp.full((BLK, 2 * BLK), NEG, F32)
            for b in range(REL_BUCKETS):
                acc = jnp.where(bm == b, t_ref[b, g * H + h], acc)
            o_ref[0, h] = acc

    return pl.pallas_call(
        body, grid=(N_GROUPS,),
        in_specs=[pl.BlockSpec(memory_space=pltpu.SMEM),
                  pl.BlockSpec((1, BLK, 2 * BLK), lambda g: (g, 0, 0))],
        out_specs=pl.BlockSpec((1, H, BLK, 2 * BLK), lambda g: (g, 0, 0, 0)),
        out_shape=jax.ShapeDtypeStruct((N_GROUPS, H, BLK, 2 * BLK), F32),
        compiler_params=_params(("parallel",)), name=name,
    )(rel_table, bmap)


def _band_bias_bwd(dbias, bmap, H, *, name):
    def body(d_ref, m_ref, o_ref):
        bm = m_ref[0]
        rowi = lax.broadcasted_iota(jnp.int32, (REL_BUCKETS, LANES), 0)
        lane = lax.broadcasted_iota(jnp.int32, (REL_BUCKETS, LANES), 1)
        out = jnp.zeros((REL_BUCKETS, LANES), F32)
        for h in range(H):
            dv = d_ref[0, h]
            for b in range(REL_BUCKETS):
                s = jnp.sum(jnp.sum(jnp.where(bm == b, dv, 0.0), axis=1, keepdims=True),
                            axis=0, keepdims=True)
                out = jnp.where((rowi == b) & (lane == h), s, out)
        o_ref[0] = out

    return pl.pallas_call(
        body, grid=(N_GROUPS,),
        in_specs=[pl.BlockSpec((1, H, BLK, 2 * BLK), lambda g: (g, 0, 0, 0)),
                  pl.BlockSpec((1, BLK, 2 * BLK), lambda g: (g, 0, 0))],
        out_specs=pl.BlockSpec((1, REL_BUCKETS, LANES), lambda g: (g, 0, 0)),
        out_shape=jax.ShapeDtypeStruct((N_GROUPS, REL_BUCKETS, LANES), F32),
        compiler_params=_params(("parallel",)), name=name,
    )(dbias, bmap)


def _head_masks():
    lane = lax.broadcasted_iota(jnp.int32, (BLK, LANES), 1)
    return (lane < HEAD_DIM, lane >= HEAD_DIM)


def _attn_fwd(q, kv, bias, gi, *, name):
    T = q.shape[0]
    HD = kv.shape[1] // 2
    d = DILATED_GROUPS[gi][1]
    S = T // d
    NB = S // BLK
    H = HD // HEAD_DIM
    qv, qcol = (q, gi) if d == 1 else (q[:, gi * HD:(gi + 1) * HD].reshape(S, d * HD), 0)
    kvv = kv.reshape(S, d * 2 * HD)

    def body(q_ref, kp_ref, kc_ref, vp_ref, vc_ref, b_ref, o_ref, l_ref):
        n = pl.program_id(1)
        col = lax.broadcasted_iota(jnp.int32, (BLK, 2 * BLK), 1)
        first = (n == 0) & (col < BLK)
        hm = _head_masks()
        for p in range(HD // LANES):
            sl = slice(p * LANES, (p + 1) * LANES)
            qp = q_ref[:, sl]
            kc = jnp.concatenate([kp_ref[:, sl], kc_ref[:, sl]], axis=0)
            vc = jnp.concatenate([vp_ref[:, sl], vc_ref[:, sl]], axis=0)
            outs = []
            lses = []
            for hh in range(2):
                qm = jnp.where(hm[hh], qp, jnp.zeros_like(qp))
                s = lax.dot_general(qm, kc, (((1,), (1,)), ((), ())), preferred_element_type=F32)
                s = jnp.where(first, NEG, s + b_ref[2 * p + hh])
                m = jnp.max(s, axis=1, keepdims=True)
                e = jnp.exp(s - m)
                den = jnp.sum(e, axis=1, keepdims=True)
                outs.append(jnp.dot((e / den).astype(BF16), vc, preferred_element_type=F32))
                lses.append(m + jnp.log(den))
            o_ref[:, sl] = jnp.where(hm[0], outs[0], outs[1])
            l_ref[:, sl] = jnp.where(hm[0], lses[0], lses[1])

    blk = lambda f: pl.BlockSpec((BLK, HD), f)
    prev = lambda n: jnp.maximum(n - 1, 0)
    return pl.pallas_call(
        body, grid=(d, NB),
        in_specs=[blk(lambda r, n: (n, r + qcol)),
                  blk(lambda r, n: (prev(n), r * 2)), blk(lambda r, n: (n, r * 2)),
                  blk(lambda r, n: (prev(n), r * 2 + 1)), blk(lambda r, n: (n, r * 2 + 1)),
                  pl.BlockSpec((H, BLK, 2 * BLK), lambda r, n: (0, 0, 0))],
        out_specs=[blk(lambda r, n: (n, r)), blk(lambda r, n: (n, r))],
        out_shape=[jax.ShapeDtypeStruct((S, d * HD), F32), jax.ShapeDtypeStruct((S, d * HD), F32)],
        compiler_params=_params(("parallel", "parallel")), name=name,
    )(qv, kvv, kvv, kvv, kvv, bias)


def _attn_combine(os, ls, *, name):
    T, HD = os[0].shape
    rb = _pick(T, 512)

    def body(o0, o1, o2, l0, l1, l2, o_ref, ob_ref, l_ref):
        la, lb, lc = l0[...], l1[...], l2[...]
        m = jnp.maximum(jnp.maximum(la, lb), lc)
        L = m + jnp.log(jnp.exp(la - m) + jnp.exp(lb - m) + jnp.exp(lc - m))
        o = jnp.exp(la - L) * o0[...] + jnp.exp(lb - L) * o1[...] + jnp.exp(lc - L) * o2[...]
        o_ref[...] = o
        ob_ref[...] = o.astype(BF16)
        l_ref[...] = L

    row = pl.BlockSpec((rb, HD), lambda i: (i, 0))
    return pl.pallas_call(
        body, grid=(T // rb,), in_specs=[row] * 6, out_specs=[row] * 3,
        out_shape=[jax.ShapeDtypeStruct((T, HD), F32), jax.ShapeDtypeStruct((T, HD), BF16),
                   jax.ShapeDtypeStruct((T, HD), F32)],
        compiler_params=_params(("parallel",)), name=name,
    )(*[a.reshape(T, HD) for a in os], *[a.reshape(T, HD) for a in ls])


def _attn_bwd(q, kv, bias, do, o, L, gi, *, name):
    T = q.shape[0]
    HD = kv.shape[1] // 2
    d = DILATED_GROUPS[gi][1]
    S = T // d
    NB = S // BLK
    H = HD // HEAD_DIM
    qv, qcol = (q, gi) if d == 1 else (q[:, gi * HD:(gi + 1) * HD].reshape(S, d * HD), 0)
    kvv = kv.reshape(S, d * 2 * HD)
    dov, ov, Lv = (a.reshape(S, d * HD) for a in (do, o, L))

    def body(q_ref, kp_ref, kc_ref, vp_ref, vc_ref, b_ref, do_ref, o_ref, L_ref,
             dq_ref, dk_ref, dv_ref, db_ref, ck_ref, cv_ref):
        r = pl.program_id(0)
        n = pl.program_id(1)

        @pl.when((r == 0) & (n == 0))
        def _():
            db_ref[...] = jnp.zeros_like(db_ref)

        @pl.when(n == 0)
        def _():
            ck_ref[...] = jnp.zeros_like(ck_ref)
            cv_ref[...] = jnp.zeros_like(cv_ref)

        @pl.when(n < NB)
        def _():
            col = lax.broadcasted_iota(jnp.int32, (BLK, 2 * BLK), 1)
            first = (n == 0) & (col < BLK)
            hm = _head_masks()
            for p in range(HD // LANES):
                sl = slice(p * LANES, (p + 1) * LANES)
                qp = q_ref[:, sl]
                kc = jnp.concatenate([kp_ref[:, sl], kc_ref[:, sl]], axis=0)
                vc = jnp.concatenate([vp_ref[:, sl], vc_ref[:, sl]], axis=0)
                dop = do_ref[:, sl]
                dob = dop.astype(BF16)
                prod = dop * o_ref[:, sl]
                Lp = L_ref[:, sl]
                dq_parts = []
                dkc = None
                dvc = None
                for hh in range(2):
                    qm = jnp.where(hm[hh], qp, jnp.zeros_like(qp))
                    dom = jnp.where(hm[hh], dob, jnp.zeros_like(dob))
                    s = lax.dot_general(qm, kc, (((1,), (1,)), ((), ())), preferred_element_type=F32)
                    s = jnp.where(first, NEG, s + b_ref[2 * p + hh])
                    lse = Lp[:, hh * HEAD_DIM:hh * HEAD_DIM + 1]
                    pr = jnp.exp(s - lse)
                    dp = lax.dot_general(dom, vc, (((1,), (1,)), ((), ())), preferred_element_type=F32)
                    delta = jnp.sum(jnp.where(hm[hh], prod, 0.0), axis=1, keepdims=True)
                    ds = pr * (dp - delta)
                    db_ref[2 * p + hh] += ds
                    dsb = ds.astype(BF16)
                    dq_parts.append(jnp.dot(dsb, kc, preferred_element_type=F32))
                    dkh = lax.dot_general(dsb, qm, (((0,), (0,)), ((), ())), preferred_element_type=F32)
                    dvh = lax.dot_general(pr.astype(BF16), dom, (((0,), (0,)), ((), ())),
                                          preferred_element_type=F32)
                    dkc = dkh if dkc is None else dkc + dkh
                    dvc = dvh if dvc is None else dvc + dvh
                dq = jnp.where(hm[0], dq_parts[0], dq_parts[1])
                dq_ref[:, sl] = (dq * (HEAD_DIM ** -0.5)).astype(BF16)
                dk_ref[:, sl] = ck_ref[:, sl] + dkc[:BLK]
                dv_ref[:, sl] = cv_ref[:, sl] + dvc[:BLK]
                ck_ref[:, sl] = dkc[BLK:]
                cv_ref[:, sl] = dvc[BLK:]

        @pl.when(n == NB)
        def _():
            dk_ref[...] = ck_ref[...]
            dv_ref[...] = cv_ref[...]

    blk = lambda f: pl.BlockSpec((BLK, HD), f)
    cur = lambda n: jnp.minimum(n, NB - 1)
    prev = lambda n: jnp.maximum(jnp.minimum(n, NB - 1) - 1, 0)
    lag = lambda n: jnp.maximum(n - 1, 0)
    return pl.pallas_call(
        body, grid=(d, NB + 1),
        in_specs=[blk(lambda r, n: (cur(n), r + qcol)),
                  blk(lambda r, n: (prev(n), r * 2)), blk(lambda r, n: (cur(n), r * 2)),
                  blk(lambda r, n: (prev(n), r * 2 + 1)), blk(lambda r, n: (cur(n), r * 2 + 1)),
                  pl.BlockSpec((H, BLK, 2 * BLK), lambda r, n: (0, 0, 0)),
                  blk(lambda r, n: (cur(n), r)), blk(lambda r, n: (cur(n), r)),
                  blk(lambda r, n: (cur(n), r))],
        out_specs=[blk(lambda r, n: (cur(n), r)), blk(lambda r, n: (lag(n), r)),
                   blk(lambda r, n: (lag(n), r)),
                   pl.BlockSpec((H, BLK, 2 * BLK), lambda r, n: (0, 0, 0))],
        out_shape=[jax.ShapeDtypeStruct((S, d * HD), BF16), jax.ShapeDtypeStruct((S, d * HD), F32),
                   jax.ShapeDtypeStruct((S, d * HD), F32),
                   jax.ShapeDtypeStruct((H, BLK, 2 * BLK), F32)],
        scratch_shapes=[pltpu.VMEM((BLK, HD), F32), pltpu.VMEM((BLK, HD), F32)],
        compiler_params=_params(("arbitrary", "arbitrary")), name=name,
    )(qv, kvv, kvv, kvv, kvv, bias, dov, ov, Lv)


SUPER = DILATED_GROUPS[-1][1] * BLK


def _band_rows(it, d):
    r, j = it % d, it // d
    if d == 1:
        at = lambda blk: pl.ds(pl.multiple_of(blk * BLK, BLK), BLK)
    else:
        at = lambda blk: pl.ds(r + d * BLK * blk, BLK, stride=d)
    return at(j), at(jnp.maximum(j - 1, 0))


def _band_loops(step, d):
    n_it = SUPER // BLK

    def run(lo, hi, inside):
        if hi > lo:
            def body(it, carry):
                step(it, inside)
                return carry
            lax.fori_loop(lo, hi, body, 0, unroll=max(u for u in (4, 3, 2, 1) if (hi - lo) % u == 0))

    run(0, d, False)
    run(d, n_it, True)


def _last_rows(it, d):
    m = SUPER // (d * BLK)
    if d == 1:
        return pl.ds((m - 1) * BLK, BLK)
    return pl.ds(it % d + d * BLK * (m - 1), BLK, stride=d)


def _attn_fwd_all(q, kv, bias, *, name):
    T = q.shape[0]
    HD = kv.shape[1] // 2
    PP = HD // LANES
    NS = T // SUPER

    def body(q0, q1, q2, kp_ref, kc_ref, vp_ref, vc_ref, b_ref, o_ref, ob_ref, l_ref, og, lg):
        n = pl.program_id(1)
        col = lax.broadcasted_iota(jnp.int32, (BLK, 2 * BLK), 1)
        hm = _head_masks()
        for g, (q_ref, (_, d)) in enumerate(zip((q0, q1, q2), DILATED_GROUPS)):
            def step(it, inside, g=g, q_ref=q_ref, d=d):
                cur, prv = _band_rows(it, d)
                qp = q_ref[cur, :].astype(BF16)
                if inside:
                    kprev, vprev = kc_ref[prv, :], vc_ref[prv, :]
                else:
                    last = _last_rows(it, d)
                    kprev, vprev = kp_ref[last, :], vp_ref[last, :]
                kc = jnp.concatenate([kprev.astype(BF16), kc_ref[cur, :].astype(BF16)], axis=0)
                vc = jnp.concatenate([vprev.astype(BF16), vc_ref[cur, :].astype(BF16)], axis=0)
                first = (n == 0) & (col < BLK)
                outs, lses = [], []
                for hh in range(2):
                    qm = jnp.where(hm[hh], qp, jnp.zeros_like(qp))
                    s = lax.dot_general(qm, kc, (((1,), (1,)), ((), ())), preferred_element_type=F32)
                    s = s + b_ref[g, hh]
                    if not inside:
                        s = jnp.where(first, NEG, s)
                    mx = jnp.max(s, axis=1, keepdims=True)
                    e = jnp.exp(s - mx)
                    den = jnp.sum(e, axis=1, keepdims=True)
                    outs.append(jnp.dot((e / den).astype(BF16), vc, preferred_element_type=F32))
                    lses.append(mx + jnp.log(den))
                og.at[g][cur, :] = jnp.where(hm[0], outs[0], outs[1])
                lg.at[g][cur, :] = jnp.where(hm[0], lses[0], lses[1])

            _band_loops(step, d)
        la, lb, lc = lg[0], lg[1], lg[2]
        mx = jnp.maximum(jnp.maximum(la, lb), lc)
        L = mx + jnp.log(jnp.exp(la - mx) + jnp.exp(lb - mx) + jnp.exp(lc - mx))
        o = jnp.exp(la - L) * og[0] + jnp.exp(lb - L) * og[1] + jnp.exp(lc - L) * og[2]
        o_ref[...] = o
        ob_ref[...] = o.astype(BF16)
        l_ref[...] = L

    blk = lambda f: pl.BlockSpec((SUPER, LANES), f)
    prev = lambda n: jnp.maximum(n - 1, 0)
    qspec = lambda g: blk(lambda p, n: (n, g * PP + p))
    return pl.pallas_call(
        body, grid=(PP, NS),
        in_specs=[qspec(0), qspec(1), qspec(2),
                  blk(lambda p, n: (prev(n), p)), blk(lambda p, n: (n, p)),
                  blk(lambda p, n: (prev(n), PP + p)), blk(lambda p, n: (n, PP + p)),
                  pl.BlockSpec((N_GROUPS, 2, BLK, 2 * BLK), lambda p, n: (0, p, 0, 0))],
        out_specs=[blk(lambda p, n: (n, p))] * 3,
        out_shape=[jax.ShapeDtypeStruct((T, HD), F32), jax.ShapeDtypeStruct((T, HD), BF16),
                   jax.ShapeDtypeStruct((T, HD), F32)],
        scratch_shapes=[pltpu.VMEM((N_GROUPS, SUPER, LANES), F32), pltpu.VMEM((N_GROUPS, SUPER, LANES), F32)],
        compiler_params=_params(("parallel", "parallel")), name=name,
    )(q, q, q, kv, kv, kv, kv, bias)


def _attn_bwd_all(q, kv, bias, do, o, L, *, after=None, name):
    T = q.shape[0]
    HD = kv.shape[1] // 2
    PP = HD // LANES
    H = HD // HEAD_DIM
    NS = T // SUPER

    def body(q0, q1, q2, kp_ref, kc_ref, vp_ref, vc_ref, b_ref, do_ref, o_ref, L_ref,
             dq_ref, dk_ref, dv_ref, db_ref, ck_ref, cv_ref):
        n = pl.program_id(1)

        @pl.when(n == 0)
        def _():
            db_ref[...] = jnp.zeros_like(db_ref)
            ck_ref[...] = jnp.zeros_like(ck_ref)
            cv_ref[...] = jnp.zeros_like(cv_ref)

        dk_ref[...] = ck_ref[...]
        dv_ref[...] = cv_ref[...]
        ck_ref[...] = jnp.zeros_like(ck_ref)
        cv_ref[...] = jnp.zeros_like(cv_ref)

        @pl.when(n < NS)
        def _():
            col = lax.broadcasted_iota(jnp.int32, (BLK, 2 * BLK), 1)
            hm = _head_masks()
            for g, (q_ref, (_, d)) in enumerate(zip((q0, q1, q2), DILATED_GROUPS)):
                def step(it, inside, g=g, q_ref=q_ref, d=d):
                    cur, prv = _band_rows(it, d)
                    last = _last_rows(it, d)
                    qp = q_ref[cur, :].astype(BF16)
                    if inside:
                        kprev, vprev = kc_ref[prv, :], vc_ref[prv, :]
                    else:
                        kprev, vprev = kp_ref[last, :], vp_ref[last, :]
                    kc = jnp.concatenate([kprev.astype(BF16), kc_ref[cur, :].astype(BF16)], axis=0)
                    vc = jnp.concatenate([vprev.astype(BF16), vc_ref[cur, :].astype(BF16)], axis=0)
                    first = (n == 0) & (col < BLK)
                    dop = do_ref[cur, :]
                    dob = dop.astype(BF16)
                    prod = dop * o_ref[cur, :]
                    Lp = L_ref[cur, :]
                    dq_parts = []
                    dkc = None
                    dvc = None
                    for hh in range(2):
                        qm = jnp.where(hm[hh], qp, jnp.zeros_like(qp))
                        dom = jnp.where(hm[hh], dob, jnp.zeros_like(dob))
                        s = lax.dot_general(qm, kc, (((1,), (1,)), ((), ())), preferred_element_type=F32)
                        s = s + b_ref[g, hh]
                        if not inside:
                            s = jnp.where(first, NEG, s)
                        lse = Lp[:, hh * HEAD_DIM:hh * HEAD_DIM + 1]
                        pr = jnp.exp(s - lse)
                        dp = lax.dot_general(dom, vc, (((1,), (1,)), ((), ())), preferred_element_type=F32)
                        delta = jnp.sum(jnp.where(hm[hh], prod, 0.0), axis=1, keepdims=True)
                        ds = pr * (dp - delta)
                        db_ref[g, hh] += ds
                        dsb = ds.astype(BF16)
                        dq_parts.append(jnp.dot(dsb, kc, preferred_element_type=F32))
                        dkh = lax.dot_general(dsb, qm, (((0,), (0,)), ((), ())), preferred_element_type=F32)
                        dvh = lax.dot_general(pr.astype(BF16), dom, (((0,), (0,)), ((), ())),
                                              preferred_element_type=F32)
                        dkc = dkh if dkc is None else dkc + dkh
                        dvc = dvh if dvc is None else dvc + dvh
                    dq_ref.at[g][cur, :] = jnp.where(hm[0], dq_parts[0], dq_parts[1]) * (HEAD_DIM ** -0.5)
                    ck_ref[cur, :] += dkc[BLK:]
                    cv_ref[cur, :] += dvc[BLK:]
                    if inside:
                        ck_ref[prv, :] += dkc[:BLK]
                        cv_ref[prv, :] += dvc[:BLK]
                    else:
                        dk_ref[last, :] += dkc[:BLK]
                        dv_ref[last, :] += dvc[:BLK]

                _band_loops(step, d)

    blk = lambda f: pl.BlockSpec((SUPER, LANES), f)
    cur = lambda n: jnp.minimum(n, NS - 1)
    prev = lambda n: jnp.maximum(jnp.minimum(n, NS - 1) - 1, 0)
    lag = lambda n: jnp.maximum(n - 1, 0)
    qspec = lambda g: blk(lambda p, n: (cur(n), g * PP + p))
    bspec = pl.BlockSpec((N_GROUPS, 2, BLK, 2 * BLK), lambda p, n: (0, p, 0, 0))
    body, in_specs, args = _ordered(
        body, [qspec(0), qspec(1), qspec(2),
               blk(lambda p, n: (prev(n), p)), blk(lambda p, n: (cur(n), p)),
               blk(lambda p, n: (prev(n), PP + p)), blk(lambda p, n: (cur(n), PP + p)),
               bspec, blk(lambda p, n: (cur(n), p)), blk(lambda p, n: (cur(n), p)),
               blk(lambda p, n: (cur(n), p))],
        (q, q, q, kv, kv, kv, kv, bias, do, o, L), after)
    return pl.pallas_call(
        body, grid=(PP, NS + 1), in_specs=in_specs,
        out_specs=[pl.BlockSpec((N_GROUPS, SUPER, LANES), lambda p, n: (0, cur(n), p)),
                   blk(lambda p, n: (lag(n), p)), blk(lambda p, n: (lag(n), p)), bspec],
        out_shape=[jax.ShapeDtypeStruct((N_GROUPS, T, HD), F32), jax.ShapeDtypeStruct((T, HD), F32),
                   jax.ShapeDtypeStruct((T, HD), F32),
                   jax.ShapeDtypeStruct((N_GROUPS, H, BLK, 2 * BLK), F32)],
        scratch_shapes=[pltpu.VMEM((SUPER, LANES), F32), pltpu.VMEM((SUPER, LANES), F32)],
        compiler_params=_params(("arbitrary", "arbitrary")), name=name,
    )(*args)


class _Weights(dict):
    def __init__(self, base, fetch=None, emit=None, emit_small=None):
        super().__init__(base)
        self._fetch, self._emit, self._emit_small = fetch, emit, emit_small

    def fetch(self, group, after):
        if self._fetch is not None:
            for (key, layer), mat in self._fetch(group, after).items():
                self[key][layer] = mat

    def emit(self, group, mats):
        return None if self._emit is None else self._emit(group, mats)

    def emit_small(self, grads):
        return None if self._emit_small is None else self._emit_small(grads)


def _local_step(x, tgt, W):
    T, D = x.shape
    H = W["rel_table"].shape[1] // N_GROUPS
    HD = H * HEAD_DIM
    G = W["a_w_s"].shape[1]
    assert T % (DILATED_GROUPS[-1][1] * BLK) == 0

    tril = jnp.tril(jnp.ones((CHUNK, CHUNK), F32))
    bmap = jnp.asarray(_bucket_maps())
    bias = _band_bias(W["rel_table"], bmap, H, name="band_bias")

    saved = []
    xc, xcb = x, x.astype(BF16)
    kvb = None
    for i in range(DEPTH):
        s = {"x": xc, "xb": xcb}
        W.fetch(4 * i, xc)
        if i < N_A:
            ws_m = W["a_w_s"][i] * tril
            s["ws"] = ws_m.astype(BF16)
            s["wst"] = jnp.swapaxes(ws_m, 1, 2).astype(BF16)
            s["bst"] = W["a_b_s"][i].T
            s["zp"] = _mm(xcb, W["a_w_in"][i], out_dtype=ACT, name=f"a_in_{i}")
            s["y"] = _sgu_fwd(s["zp"], W["a_ln_g"][i], W["a_ln_b"][i], s["ws"], s["bst"], name=f"sgu_fwd_{i}")
            W.fetch(4 * i + 1, s["zp"])
            s["h"] = _mm(s["y"], W["a_w_out"][i], name=f"a_out_{i}")
        else:
            j = i - N_A
            if kvb is None:
                kvb = _mm(xcb, W["kv_w"][0], name="kv_proj")
            s["q"] = _mm(xcb, W["b_w_q_t"][j], tb=True, scale=HEAD_DIM ** -0.5, name=f"q_proj_{j}")
            s["o"], s["ob"], s["L"] = _attn_fwd_all(s["q"], kvb, bias, name=f"attn_fwd_{j}")
            W.fetch(4 * i + 1, s["q"])
            s["h"] = _mm(s["ob"], W["b_w_o"][j], name=f"o_proj_{j}")
        s["x1"], s["x1b"] = _add_ln_fwd(xc, s["h"], W["ln_g"][i, 0], W["ln_b"][i, 0], name=f"ln1_fwd_{i}")
        W.fetch(4 * i + 2, s["x1"])
        s["hh"] = _mm(s["x1b"], W["ffn_w_up_t"][i], tb=True, out_dtype=ACT, name=f"ffn_up_{i}")
        s["cw"] = W["ffn_conv_w"][i]
        s["cb"] = W["ffn_conv_b"][i].reshape(1, -1)
        s["act"] = _convgate_fwd(s["hh"], s["cw"], s["cb"], name=f"convgate_fwd_{i}")
        W.fetch(4 * i + 3, s["hh"])
        s["f"] = _mm(s["act"], W["ffn_w_down"][i], name=f"ffn_down_{i}")
        xc, xcb = _add_ln_fwd(s["x1"], s["f"], W["ln_g"][i, 1], W["ln_b"][i, 1], name=f"ln2_fwd_{i}")
        saved.append(s)

    dy, lossv = _loss_grad(xc, tgt, name="loss_grad")
    loss = lossv[0, 0]

    gl = {k: [None] * DEPTH for k in ("ffn_w_up_t", "ffn_conv_w", "ffn_conv_b", "ffn_w_down", "ln_g", "ln_b")}
    ga = {k: [None] * N_A for k in ("a_w_in", "a_ln_g", "a_ln_b", "a_w_s", "a_b_s", "a_w_out")}
    gb = {k: [None] * (DEPTH - N_A) for k in ("b_w_q_t", "b_w_o")}
    mats = ("a_w_in", "a_w_out", "b_w_q_t", "b_w_o", "ffn_w_up_t", "ffn_w_down")
    dks, dvs, dbias = [], [], []
    grads = {}
    terms = [(1.0, dy)]
    tok = None
    small_keys = ("ffn_conv_w", "ffn_conv_b", "ln_g", "ln_b", "a_ln_g", "a_ln_b", "a_w_s", "a_b_s")
    for i in reversed(range(DEPTH)):
        s = saved[i]
        dp2, dp2b, dg2, db2 = _add_ln_bwd(s["x1"], s["f"], W["ln_g"][i, 1], terms, after=tok, name=f"ln2_bwd_{i}")
        dact = _mm(dp2b, W["ffn_w_down"][i], tb=True, out_dtype=ACT, name=f"ffn_down_dx_{i}")
        gl["ffn_w_down"][i] = _mm(s["act"], dp2b, ta=True, out_dtype=BF16, name=f"ffn_down_dw_{i}")
        dha, dhg, dwa, dwg, dba, dbg = _convgate_bwd(s["hh"], dact, s["cw"], s["cb"], name=f"convgate_bwd_{i}")
        dhh = (dha, dhg)
        gl["ffn_conv_w"][i] = jnp.concatenate([dwa, dwg], axis=1)
        gl["ffn_conv_b"][i] = jnp.concatenate([dba, dbg], axis=1)[0]
        dx1 = _mm(dhh, W["ffn_w_up_t"][i], name=f"ffn_up_dx_{i}")
        gl["ffn_w_up_t"][i] = _mm(dhh, s["x1b"], ta=True, out_dtype=BF16, name=f"ffn_up_dw_{i}")
        tok = W.emit(3 * i + 2, {("ffn_w_up_t", i): gl["ffn_w_up_t"][i], ("ffn_w_down", i): gl["ffn_w_down"][i]})
        dp1, dp1b, dg1, db1 = _add_ln_bwd(s["x"], s["h"], W["ln_g"][i, 0], [(ALPHA, dp2), (1.0, dx1)],
                                          after=tok, name=f"ln1_bwd_{i}")
        gl["ln_g"][i] = jnp.concatenate([dg1, dg2], axis=0)
        gl["ln_b"][i] = jnp.concatenate([db1, db2], axis=0)
        terms = [(ALPHA, dp1)]
        if i < N_A:
            dyy = _mm(dp1b, W["a_w_out"][i], tb=True, out_dtype=ACT, name=f"a_out_dx_{i}")
            ga["a_w_out"][i] = _mm(s["y"], dp1b, ta=True, out_dtype=BF16, name=f"a_out_dw_{i}")
            tok = W.emit(3 * i + 1, {("a_w_out", i): ga["a_w_out"][i]})
            dzp, dlg, dlb, dws, dbs = _sgu_bwd(s["zp"], dyy, W["a_ln_g"][i], W["a_ln_b"][i], s["ws"],
                                               s["wst"], s["bst"], after=tok, name=f"sgu_bwd_{i}")
            ga["a_ln_g"][i], ga["a_ln_b"][i], ga["a_w_s"][i] = dlg[0], dlb[0], dws
            ga["a_b_s"][i] = dbs[:, :G].T
            if i == 0:
                for dct in (gl, ga):
                    grads.update({k: jnp.stack(v) for k, v in dct.items() if k in small_keys})
                tok = W.emit_small(grads)
            ga["a_w_in"][i] = _mm(s["xb"], dzp, ta=True, out_dtype=BF16, after=tok, name=f"a_in_dw_{i}")
            tok = W.emit(3 * i, {("a_w_in", i): ga["a_w_in"][i]})
            terms.append((1.0, _mm(dzp, W["a_w_in"][i], tb=True, after=tok, name=f"a_in_dx_{i}")))
        else:
            j = i - N_A
            do = _mm(dp1b, W["b_w_o"][j], tb=True, name=f"o_proj_dx_{j}")
            gb["b_w_o"][j] = _mm(s["ob"], dp1b, ta=True, out_dtype=BF16, name=f"o_proj_dw_{j}")
            tok = W.emit(3 * i + 1, {("b_w_o", j): gb["b_w_o"][j]})
            dq, dk_j, dv_j, db_j = _attn_bwd_all(s["q"], kvb, bias, do, s["o"], s["L"], after=tok,
                                                 name=f"attn_bwd_{j}")
            dks.append((1.0, dk_j))
            dvs.append((1.0, dv_j))
            dbias.append(db_j)
            terms.append((1.0, _mm(dq, W["b_w_q_t"][j], name=f"q_proj_dx_{j}")))
            gb["b_w_q_t"][j] = _mm(dq, s["xb"], ta=True, out_dtype=BF16, name=f"q_proj_dw_{j}")
            out_b = {("b_w_q_t", j): gb["b_w_q_t"][j]}
            if i == N_A:
                dkv = jnp.concatenate([_lincomb(dks, BF16, name="dk_sum"), _lincomb(dvs, BF16, name="dv_sum")],
                                      axis=1)
                terms.append((1.0, _mm(dkv, W["kv_w"][0], tb=True, name="kv_proj_dx")))
                grads["kv_w"] = [_mm(s["xb"], dkv, ta=True, out_dtype=BF16, name="kv_proj_dw")]
                out_b[("kv_w", 0)] = grads["kv_w"][0]
                dbt = _lincomb([(1.0, a.reshape(-1, 2 * BLK)) for a in dbias], F32, name="dbias_sum")
                dtab = _band_bias_bwd(dbt.reshape(N_GROUPS, H, BLK, 2 * BLK), bmap, H, name="band_bias_bwd")
                grads["rel_table"] = jnp.transpose(dtab[:, :, :H], (1, 0, 2)).reshape(REL_BUCKETS, N_GROUPS * H)
            tok = W.emit(3 * i, out_b)
    grad_x = _lincomb(terms, F32, name="grad_x")
    for dct in (gl, ga, gb):
        grads.update({k: v for k, v in dct.items() if k in mats})
    return loss, grad_x, grads


def _my_index():
    return 4 * lax.axis_index("x") + 2 * lax.axis_index("y") + lax.axis_index("c")


HBM_SPEC = pl.BlockSpec(memory_space=pltpu.HBM)


def _block(ref, k, n, axis):
    off = pl.multiple_of(k * n, n)
    return ref.at[pl.ds(off, n), :] if axis == 0 else ref.at[:, pl.ds(off, n)]


def _gather_mats(local, axis, *, name):
    L, a, b = local.shape
    n = a if axis == 0 else b
    full = (a * N_DEV, b) if axis == 0 else (a, b * N_DEV)

    def body(x_ref, *rest):
        outs = rest[:L]
        send_sems, recv_sems, local_sems = rest[L:]
        x, y, c = lax.axis_index("x"), lax.axis_index("y"), lax.axis_index("c")
        me, sibling = (x, y, c), (x, y, 1 - c)
        chips = [(1 - x, y), (x, 1 - y), (1 - x, 1 - y)]

        def slot(l, px, py, pc):
            return _block(outs[l], 4 * px + 2 * py + pc, n, axis)

        def copy(l, k, blk, to, src=None):
            return pltpu.make_async_remote_copy(
                src_ref=slot(l, *blk) if src is None else src, dst_ref=slot(l, *blk),
                send_sem=send_sems.at[7 * l + k], recv_sem=recv_sems.at[7 * l + k],
                device_id=to, device_id_type=MESH)

        mine, first, passed = [], [], []
        for l in range(L):
            mine.append(pltpu.make_async_copy(x_ref.at[l], slot(l, *me), local_sems.at[l]))
            mine[-1].start()
            first.append(copy(l, 0, me, sibling, src=x_ref.at[l]))
            first += [copy(l, 1 + j, me, (*chip, c), src=x_ref.at[l]) for j, chip in enumerate(chips)]
        for cp in first:
            cp.start()
        for l in range(L):
            for j, chip in enumerate(chips):
                copy(l, 1 + j, (*chip, c), me).wait_recv()
                passed.append(copy(l, 4 + j, (*chip, c), sibling))
                passed[-1].start()
        for l in range(L):
            copy(l, 0, sibling, me).wait_recv()
            for j, chip in enumerate(chips):
                copy(l, 4 + j, (*chip, 1 - c), me).wait_recv()
        for cp in first + passed:
            cp.wait_send()
        for cp in mine:
            cp.wait()

    return pl.pallas_call(
        body, out_shape=[jax.ShapeDtypeStruct(full, local.dtype)] * L,
        in_specs=[HBM_SPEC], out_specs=[HBM_SPEC] * L,
        scratch_shapes=[pltpu.SemaphoreType.DMA((7 * L,)), pltpu.SemaphoreType.DMA((7 * L,)),
                        pltpu.SemaphoreType.DMA((L,))],
        name=name,
    )(local)


SEM_SPEC = pl.BlockSpec(memory_space=pltpu.SEMAPHORE)
FLOWING = pltpu.SideEffectType.DATAFLOW_SIDE_EFFECTING


def _peers(x, y, c):
    return [(1 - x if k & 4 else x, 1 - y if k & 2 else y, 1 - c if k & 1 else c) for k in range(1, N_DEV)]


def _ends(src_ref, land_ref, peer_index, me, n, axis, gather):
    if gather:
        return src_ref, _block(land_ref, me, n, axis)
    return _block(src_ref, peer_index, n, axis), land_ref.at[me]


def _send_start(groups, gather, *, name):
    flat = [(g, j, mat, axis) for g, items in enumerate(groups) for j, (mat, axis) in enumerate(items)]
    M, G = len(flat), len(groups)
    lands, ns = [], []
    for _, _, mat, axis in flat:
        A, B = mat.shape
        if gather:
            lands.append((A * N_DEV, B) if axis == 0 else (A, B * N_DEV))
            ns.append(A if axis == 0 else B)
        else:
            lands.append((N_DEV, A // N_DEV, B) if axis == 0 else (N_DEV, A, B // N_DEV))
            ns.append(A // N_DEV if axis == 0 else B // N_DEV)

    def body(*refs):
        src_refs, land_refs, sems = refs[:M], refs[M:2 * M], refs[2 * M:2 * M + 3 * G]
        token = refs[-1]
        x, y, c = lax.axis_index("x"), lax.axis_index("y"), lax.axis_index("c")
        me = 4 * x + 2 * y + c
        for i, (g, j, _, axis) in enumerate(flat):
            for k, (px, py, pc) in enumerate(_peers(x, y, c)):
                s, d = _ends(src_refs[i], land_refs[i], 4 * px + 2 * py + pc, me, ns[i], axis, gather)
                pltpu.make_async_remote_copy(
                    src_ref=s, dst_ref=d, send_sem=sems[3 * g].at[7 * j + k], recv_sem=sems[3 * g + 1].at[7 * j + k],
                    device_id=(px, py, pc), device_id_type=MESH).start()
            s, d = _ends(src_refs[i], land_refs[i], me, me, ns[i], axis, gather)
            pltpu.make_async_copy(s, d, sems[3 * g + 2].at[j]).start()
        token[...] = jnp.zeros_like(token)

    sem_shapes = []
    for items in groups:
        sem_shapes += [pltpu.SemaphoreType.DMA((7 * len(items),))] * 2 + [pltpu.SemaphoreType.DMA((len(items),))]
    outs = pl.pallas_call(
        body, name=name,
        out_shape=(*sem_shapes, *[pltpu.HBM(m.shape, m.dtype) for _, _, m, _ in flat],
                   *[pltpu.HBM(shp, m.dtype) for shp, (_, _, m, _) in zip(lands, flat)],
                   jax.ShapeDtypeStruct((8, LANES), F32)),
        in_specs=[HBM_SPEC] * (2 * M),
        out_specs=(*[SEM_SPEC] * (3 * G), *[HBM_SPEC] * (2 * M), pl.BlockSpec(memory_space=pltpu.VMEM)),
        input_output_aliases={i: 3 * G + i for i in range(2 * M)},
        compiler_params=pltpu.CompilerParams(has_side_effects=FLOWING),
    )(*[pltpu.with_memory_space_constraint(m, pltpu.HBM) for _, _, m, _ in flat],
      *[pltpu.with_memory_space_constraint(lax.empty(shp, m.dtype), pltpu.HBM)
        for shp, (_, _, m, _) in zip(lands, flat)])
    handles = []
    for g in range(G):
        idx = [i for i, f in enumerate(flat) if f[0] == g]
        handles.append((outs[3 * g], outs[3 * g + 1], outs[3 * g + 2], [outs[3 * G + i] for i in idx],
                        [outs[3 * G + M + i] for i in idx], [flat[i][3] for i in idx]))
    return handles, outs[-1]


def _send_wait(handle, gather, after, *, name):
    send_sems, recv_sems, local_sems, mats, lands, axes = handle
    n_m = len(mats)
    ns = []
    for mat, land, axis in zip(mats, lands, axes):
        ns.append(mat.shape[axis] if gather else land.shape[1 + axis])

    def body(*refs):
        src_refs, land_refs = refs[:n_m], refs[n_m:2 * n_m]
        ssem, rsem, lsem = refs[2 * n_m:2 * n_m + 3]
        x, y, c = lax.axis_index("x"), lax.axis_index("y"), lax.axis_index("c")
        me = 4 * x + 2 * y + c
        for j in range(n_m):
            for k, (px, py, pc) in enumerate(_peers(x, y, c)):
                s, d = _ends(src_refs[j], land_refs[j], 4 * px + 2 * py + pc, me, ns[j], axes[j], gather)
                cp = pltpu.make_async_remote_copy(
                    src_ref=s, dst_ref=d, send_sem=ssem.at[7 * j + k], recv_sem=rsem.at[7 * j + k],
                    device_id=(px, py, pc), device_id_type=MESH)
                cp.wait_send()
                cp.wait_recv()
            s, d = _ends(src_refs[j], land_refs[j], me, me, ns[j], axes[j], gather)
            pltpu.make_async_copy(s, d, lsem.at[j]).wait()

    outs = pl.pallas_call(
        body, name=name,
        out_shape=(*[pltpu.HBM(m.shape, m.dtype) for m in mats], *[pltpu.HBM(l.shape, l.dtype) for l in lands]),
        in_specs=[HBM_SPEC] * (2 * n_m) + [SEM_SPEC] * 3 + [pl.BlockSpec(memory_space=pl.ANY)],
        out_specs=tuple([HBM_SPEC] * (2 * n_m)),
        input_output_aliases={i: i for i in range(2 * n_m)},
        compiler_params=pltpu.CompilerParams(has_side_effects=FLOWING),
    )(*mats, *lands, send_sems, recv_sems, local_sems, after)
    return list(outs[n_m:])


def _sum_parts(parts, *, name):
    n, R, C = parts.shape
    rb = _pick(R, 512) if R % LANES == 0 else R

    def body(p_ref, o_ref):
        acc = p_ref[0].astype(F32)
        for k in range(1, n):
            acc = acc + p_ref[k].astype(F32)
        o_ref[...] = acc

    return pl.pallas_call(
        body, grid=(R // rb,), in_specs=[pl.BlockSpec((n, rb, C), lambda i: (0, i, 0))],
        out_specs=pl.BlockSpec((rb, C), lambda i: (i, 0)),
        out_shape=jax.ShapeDtypeStruct((R, C), F32),
        compiler_params=_params(("parallel",)), name=name,
    )(parts)


def _adamw(w, m, v, parts, *, name):
    L, R, C = w.shape
    n = parts[0].shape[0]
    cap = max(16, VMEM_LIMIT // 3 // (2 * L * n * C * parts[0].dtype.itemsize))
    rb = max([r for r in range(16, min(R, cap) + 1, 16) if R % r == 0], default=R)

    def body(w_ref, m_ref, v_ref, *rest):
        p_refs = rest[:L]
        g_ref, d_ref, nm_ref, nv_ref = rest[L:]
        for l in range(L):
            @pl.when(pl.program_id(0) == l)
            def _(p_ref=p_refs[l]):
                g = p_ref[0].astype(F32)
                for k in range(1, n):
                    g = g + p_ref[k].astype(F32)
                mn = ADAM_B1 * m_ref[...] + (1.0 - ADAM_B1) * g
                vn = ADAM_B2 * v_ref[...] + (1.0 - ADAM_B2) * jnp.square(g)
                m_hat = mn / (1.0 - ADAM_B1 ** ADAM_STEP)
                v_hat = vn / (1.0 - ADAM_B2 ** ADAM_STEP)
                g_ref[...] = g
                d_ref[...] = -ADAM_LR * (m_hat / (jnp.sqrt(v_hat) + ADAM_EPS) + ADAM_WD * w_ref[...])
                nm_ref[...] = mn
                nv_ref[...] = vn

    row = pl.BlockSpec((None, rb, C), lambda l, i: (l, i, 0))
    part = lambda k: pl.BlockSpec((n, rb, C), lambda l, i: (0, jnp.where(l == k, i, 0), 0))
    return pl.pallas_call(
        body, grid=(L, R // rb), in_specs=[row, row, row] + [part(k) for k in range(L)],
        out_specs=[row] * 4, out_shape=[jax.ShapeDtypeStruct((L, R, C), F32)] * 4,
        compiler_params=_params(("arbitrary", "arbitrary")), name=name,
    )(w, m, v, *parts)


BIG = (("a_w_in", "a_w_in", 1, False), ("a_w_out", "a_w_out", 0, False), ("kv_w", "kv_w", 0, False),
       ("b_w_q", "b_w_q_t", 0, True), ("b_w_o", "b_w_o", 1, False), ("ffn_w_up", "ffn_w_up_t", 0, True),
       ("ffn_w_down", "ffn_w_down", 0, False))
SMALL_SHARDED = (("a_ln_g", 1), ("a_ln_b", 1), ("ffn_conv_w", 2), ("ln_g", 2), ("ln_b", 2))
REPLICATED = ("a_w_s", "a_b_s", "rel_table", "ffn_conv_b")


def _pack_rows(arrs, lead=0):
    lshape = arrs[0].shape[:lead]
    p = jnp.concatenate([a.reshape(*lshape, -1, LANES) for a in arrs], axis=lead)
    pad = -p.shape[lead] % 8
    return jnp.pad(p, [(0, 0)] * lead + [(0, pad), (0, 0)])


def _unpack_rows(packed, shapes, lead=0):
    lshape = packed.shape[:lead]
    out, off = [], 0
    for shp in shapes:
        r = int(np.prod(shp)) // LANES
        out.append(lax.slice_in_dim(packed, off, off + r, axis=lead).reshape(*lshape, *shp))
        off += r
    return out


def _as_mats(a, transposed):
    a = a[None] if a.ndim == 2 else a
    return jnp.swapaxes(a, 1, 2) if transposed else a


def _merge_shards(stacked, axis):
    a = jnp.moveaxis(stacked, 0, axis)
    shp = list(a.shape)
    return a.reshape(shp[:axis] + [shp[axis] * shp[axis + 1]] + shp[axis + 2:])


def _split_shards(full, axis):
    shp = list(full.shape)
    a = full.reshape(shp[:axis] + [N_DEV, shp[axis] // N_DEV] + shp[axis + 1:])
    return jnp.moveaxis(a, axis, 0)


def kernel(x, a_w_in, a_ln_g, a_ln_b, a_w_s, a_b_s, a_w_out, kv_w, b_w_q, b_w_o, rel_table, ffn_w_up, ffn_conv_w, ffn_conv_b, ffn_w_down, ln_g, ln_b, loss_target, m_a_w_in, m_a_ln_g, m_a_ln_b, m_a_w_s, m_a_b_s, m_a_w_out, m_kv_w, m_b_w_q, m_b_w_o, m_rel_table, m_ffn_w_up, m_ffn_conv_w, m_ffn_conv_b, m_ffn_w_down, m_ln_g, m_ln_b, v_a_w_in, v_a_ln_g, v_a_ln_b, v_a_w_s, v_a_b_s, v_a_w_out, v_kv_w, v_b_w_q, v_b_w_o, v_rel_table, v_ffn_w_up, v_ffn_conv_w, v_ffn_conv_b, v_ffn_w_down, v_ln_g, v_ln_b):
    names = ["a_w_in", "a_ln_g", "a_ln_b", "a_w_s", "a_b_s", "a_w_out", "kv_w", "b_w_q", "b_w_o", "rel_table",
             "ffn_w_up", "ffn_conv_w", "ffn_conv_b", "ffn_w_down", "ln_g", "ln_b"]
    w = dict(zip(names, (a_w_in, a_ln_g, a_ln_b, a_w_s, a_b_s, a_w_out, kv_w, b_w_q, b_w_o, rel_table,
                         ffn_w_up, ffn_conv_w, ffn_conv_b, ffn_w_down, ln_g, ln_b)))
    m = dict(zip(names, (m_a_w_in, m_a_ln_g, m_a_ln_b, m_a_w_s, m_a_b_s, m_a_w_out, m_kv_w, m_b_w_q, m_b_w_o,
                         m_rel_table, m_ffn_w_up, m_ffn_conv_w, m_ffn_conv_b, m_ffn_w_down, m_ln_g, m_ln_b)))
    v = dict(zip(names, (v_a_w_in, v_a_ln_g, v_a_ln_b, v_a_w_s, v_a_b_s, v_a_w_out, v_kv_w, v_b_w_q, v_b_w_o,
                         v_rel_table, v_ffn_w_up, v_ffn_conv_w, v_ffn_conv_b, v_ffn_w_down, v_ln_g, v_ln_b)))
    small_names = [n for n, _ in SMALL_SHARDED]
    small_shapes = [w[n].shape for n in small_names]
    rep_shapes = [w[n].shape for n in REPLICATED]

    axis_of = {key: axis for _, key, axis, _ in BIG}
    src = {}
    for n, key, axis, tr in BIG:
        loc = _as_mats(w[n], tr).astype(BF16)
        for l in range(loc.shape[0]):
            src[(key, l)] = loc[l]
    order = []
    for i in range(DEPTH):
        if i < N_A:
            order += [[("a_w_in", i)], [("a_w_out", i)]]
        else:
            order += [([("kv_w", 0)] if i == N_A else []) + [("b_w_q_t", i - N_A)], [("b_w_o", i - N_A)]]
        order += [[("ffn_w_up_t", i)], [("ffn_w_down", i)]]
    small_src = _pack_rows([w[n] for n in small_names])
    srows = small_src.shape[0]
    handles, _ = _send_start([[(small_src, 0)]] + [[(src[kl], axis_of[kl[0]]) for kl in grp] for grp in order],
                             True, name="gather_start")
    small_all = _send_wait(handles[0], True, x, name="gather_wait_small")[0]
    small_st = _unpack_rows(small_all.reshape(N_DEV, srows, LANES), small_shapes, lead=1)
    base = {n: w[n] for n in REPLICATED}
    for (n, ax), st in zip(SMALL_SHARDED, small_st):
        base[n] = _merge_shards(st, ax)
    for n, key, _, tr in BIG:
        base[key] = [None] * (1 if w[n].ndim == 2 else w[n].shape[0])

    def fetch(group, after):
        mats = _send_wait(handles[1 + group], True, after, name=f"gather_wait_{group}")
        return dict(zip(order[group], mats))

    sent = {}

    def emit(group, mats):
        keys = list(mats)
        hs, token = _send_start([[(mats[kl], axis_of[kl[0]]) for kl in keys]], False, name=f"exchange_start_{group}")
        sent[group] = (keys, hs[0])
        return token

    small_sent = []

    def emit_small(grads):
        small_pack = _pack_rows([_split_shards(grads[n], ax) for n, ax in SMALL_SHARDED], lead=1)
        rest = _pack_rows([grads[n] for n in REPLICATED[1:]])
        mine = jnp.concatenate([small_pack.reshape(N_DEV * srows, LANES), rest], axis=0)
        gating = grads[REPLICATED[0]].reshape(-1, LANES).astype(BF16)
        hs, token = _send_start([[(mine, 0), (gating, 0)]], True, name="small_grads_start")
        small_sent.append(hs[0])
        return token

    loss, grad_x, grads = _local_step(x[0], loss_target[0], _Weights(base, fetch, emit, emit_small))
    loss = lax.psum(loss, ("x", "y", "c"))
    out = {}

    landed = {}
    for group in sorted(sent, reverse=True):
        keys, h = sent[group]
        landed.update(zip(keys, _send_wait(h, False, grad_x, name=f"exchange_wait_{group}")))
    last = grad_x
    for n, key, axis, tr in BIG:
        shp = w[n].shape
        parts = [landed[(key, l)] for l in range(1 if len(shp) == 2 else shp[0])]
        res = _adamw(_as_mats(w[n], tr), _as_mats(m[n], tr), _as_mats(v[n], tr), parts, name=f"adamw_{n}")
        out[n] = [(jnp.swapaxes(r, 1, 2) if tr else r).reshape(shp) for r in res]
        last = res[0]

    allp, allg = _send_wait(small_sent[0], True, last, name="small_grads_wait")
    gsum = _sum_parts(allp.reshape(N_DEV, -1, LANES), name="sum_small_grads")
    gating = _sum_parts(allg.reshape(N_DEV, -1, LANES), name="sum_gating_grads")
    g_small = lax.dynamic_slice_in_dim(gsum, _my_index() * srows, srows, axis=0)
    pack_sr = lambda d: jnp.concatenate([_pack_rows([d[n] for n in small_names]),
                                         _pack_rows([d[n] for n in REPLICATED])], axis=0)
    n_rest = sum(int(np.prod(s)) for s in rep_shapes[1:]) // LANES
    gs_in = jnp.concatenate([g_small, gating, gsum[N_DEV * srows:N_DEV * srows + n_rest]], axis=0)
    gs_in = jnp.pad(gs_in, ((0, pack_sr(w).shape[0] - gs_in.shape[0]), (0, 0)))[None]
    res = _adamw(pack_sr(w)[None], pack_sr(m)[None], pack_sr(v)[None], [gs_in], name="adamw_small")
    for n, vals in zip(small_names, zip(*[_unpack_rows(r[0, :srows], small_shapes) for r in res])):
        out[n] = list(vals)
    for n, vals in zip(REPLICATED, zip(*[_unpack_rows(r[0, srows:], rep_shapes) for r in res])):
        out[n] = list(vals)

    return (loss, grad_x[None], *[out[n][0] for n in names], *[out[n][1] for n in names],
            *[out[n][2] for n in names], *[out[n][3] for n in names])
```

```python
import math

import numpy as np
import jax
import jax.numpy as jnp
from jax import lax
from jax.experimental import pallas as pl
from jax.experimental.pallas import tpu as pltpu

F32 = jnp.float32
BF16 = jnp.bfloat16
ACT = jnp.bfloat16
MESH = pl.DeviceIdType.MESH

N_DEV = 8
DEPTH = 4
N_A = 2
CHUNK = 128
BLK = 128
HEAD_DIM = 64
DILATED_GROUPS = ((128, 1), (512, 4), (2048, 16))
N_GROUPS = 3
REL_BUCKETS = 32
REL_MAX_DIST = 2048
ALPHA = (2 * DEPTH) ** 0.25
LN_EPS = 1e-5
NEG = -1e30
ADAM_LR = 0.001
ADAM_B1 = 0.9
ADAM_B2 = 0.999
ADAM_EPS = 1e-08
ADAM_WD = 0.01
ADAM_STEP = 10

LANES = 128
VMEM_LIMIT = 56 * 1024 * 1024
MM_TILE_CAP = 1408
INV_SQRT2 = 1.0 / math.sqrt(2.0)
INV_SQRT_2PI = 1.0 / math.sqrt(2.0 * math.pi)


def _pick(n, cap):
    best = None
    for t in range(LANES, min(n, cap) + 1, LANES):
        if n % t == 0:
            best = t
    return best if best is not None else n


def _params(sem):
    return pltpu.CompilerParams(dimension_semantics=sem, vmem_limit_bytes=VMEM_LIMIT)


def _ordered(body, in_specs, args, after):
    if after is None:
        return body, list(in_specs), tuple(args)
    return (lambda _, *refs: body(*refs)), [pl.BlockSpec(memory_space=pl.ANY), *in_specs], (after, *args)


def _gelu(x):
    return 0.5 * x * (1.0 + lax.erf(x * INV_SQRT2))


def _gelu_grad(x):
    return 0.5 * (1.0 + lax.erf(x * INV_SQRT2)) + x * jnp.exp(-0.5 * x * x) * INV_SQRT_2PI


def _mm(a, b, *, ta=False, tb=False, out_dtype=F32, scale=None, after=None, name):
    halves = isinstance(a, tuple)
    parts = 1 if halves or a.ndim == 2 else a.shape[0]
    ash = (a[0].shape[0], 2 * a[0].shape[1]) if halves else (a.shape if parts == 1
                                                               else (a.shape[1], parts * a.shape[2]))
    if ta:
        K, M = ash
    else:
        M, K = ash
    if tb:
        N, Kb = b.shape
    else:
        Kb, N = b.shape
    assert K == Kb, (ash, b.shape, ta, tb)
    tm, tn, tk = _pick(M, MM_TILE_CAP), _pick(N, MM_TILE_CAP), _pick(K, MM_TILE_CAP)
    split = 2 if halves else parts
    if split > 1 and ta:
        tm = _pick(M // split, MM_TILE_CAP)
    if split > 1 and not ta:
        tk = _pick(K // split, MM_TILE_CAP)
    nk = K // tk
    nh = (M // split // tm if ta else K // split // tk) if split > 1 else 0
    dn = (((0 if ta else 1,), (1 if tb else 0,)), ((), ()))

    def body(*refs):
        a_refs, (b_ref, o_ref, acc_ref) = refs[:-3], refs[-3:]
        k = pl.program_id(2)

        def accumulate(a_ref):
            part = lax.dot_general(a_ref[...].astype(BF16), b_ref[...].astype(BF16), dn,
                                   preferred_element_type=F32)

            @pl.when(k == 0)
            def _():
                acc_ref[...] = part

            @pl.when(k > 0)
            def _():
                acc_ref[...] += part

        if halves:
            first = (pl.program_id(0) if ta else k) < nh
            pl.when(first)(lambda: accumulate(a_refs[0]))
            pl.when(jnp.logical_not(first))(lambda: accumulate(a_refs[1]))
        else:
            accumulate(a_refs[0])

        @pl.when(k == nk - 1)
        def _():
            r = acc_ref[...]
            if scale is not None:
                r = r * scale
            o_ref[...] = r.astype(out_dtype)

    if halves and ta:
        a_specs = [pl.BlockSpec((tk, tm), lambda i, j, k: (jnp.where(i < nh, k, 0), jnp.minimum(i, nh - 1))),
                   pl.BlockSpec((tk, tm), lambda i, j, k: (jnp.where(i >= nh, k, 0), jnp.maximum(i - nh, 0)))]
    elif halves:
        a_specs = [pl.BlockSpec((tm, tk), lambda i, j, k: (i, jnp.minimum(k, nh - 1))),
                   pl.BlockSpec((tm, tk), lambda i, j, k: (i, jnp.maximum(k - nh, 0)))]
    elif parts > 1:
        a_specs = [pl.BlockSpec((None, tk, tm), lambda i, j, k: (i // nh, k, i % nh)) if ta
                   else pl.BlockSpec((None, tm, tk), lambda i, j, k: (k // nh, i, k % nh))]
    else:
        a_specs = [pl.BlockSpec((tk, tm), lambda i, j, k: (k, i)) if ta
                   else pl.BlockSpec((tm, tk), lambda i, j, k: (i, k))]
    b_spec = (pl.BlockSpec((tn, tk), lambda i, j, k: (j, k)) if tb
              else pl.BlockSpec((tk, tn), lambda i, j, k: (k, j)))
    body, in_specs, args = _ordered(body, [*a_specs, b_spec], (*(a if halves else (a,)), b), after)
    return pl.pallas_call(
        body, grid=(M // tm, N // tn, nk), in_specs=in_specs,
        out_specs=pl.BlockSpec((tm, tn), lambda i, j, k: (i, j)),
        out_shape=jax.ShapeDtypeStruct((M, N), out_dtype),
        scratch_shapes=[pltpu.VMEM((tm, tn), F32)],
        compiler_params=_params(("parallel", "parallel", "arbitrary")), name=name,
    )(*args)


def _add_ln_fwd(x, h, g, b, *, name):
    T, D = x.shape
    rb = _pick(T, 512)

    def body(x_ref, h_ref, g_ref, b_ref, o_ref, ob_ref):
        pre = ALPHA * x_ref[...] + h_ref[...].astype(F32)
        mu = jnp.mean(pre, axis=1, keepdims=True)
        cen = pre - mu
        var = jnp.mean(cen * cen, axis=1, keepdims=True)
        y = cen * lax.rsqrt(var + LN_EPS) * g_ref[...] + b_ref[...]
        o_ref[...] = y
        ob_ref[...] = y.astype(BF16)

    row = pl.BlockSpec((rb, D), lambda i: (i, 0))
    vec = pl.BlockSpec((1, D), lambda i: (0, 0))
    return pl.pallas_call(
        body, grid=(T // rb,), in_specs=[row, row, vec, vec], out_specs=[row, row],
        out_shape=[jax.ShapeDtypeStruct((T, D), F32), jax.ShapeDtypeStruct((T, D), BF16)],
        compiler_params=_params(("parallel",)), name=name,
    )(x, h, g.reshape(1, D), b.reshape(1, D))


def _add_ln_bwd(x, h, g, terms, *, after=None, name):
    T, D = x.shape
    rb = _pick(T, 512)
    coefs = [c for c, _ in terms]
    nt = len(terms)

    def body(*refs):
        x_ref, h_ref, g_ref = refs[:3]
        t_refs = refs[3:3 + nt]
        dp_ref, dpb_ref, dg_ref, db_ref = refs[3 + nt:]
        dy = None
        for c, r in zip(coefs, t_refs):
            v = r[...].astype(F32) if c == 1.0 else c * r[...].astype(F32)
            dy = v if dy is None else dy + v
        pre = ALPHA * x_ref[...] + h_ref[...].astype(F32)
        mu = jnp.mean(pre, axis=1, keepdims=True)
        cen = pre - mu
        var = jnp.mean(cen * cen, axis=1, keepdims=True)
        rstd = lax.rsqrt(var + LN_EPS)
        xhat = cen * rstd
        dxh = dy * g_ref[...]
        m1 = jnp.mean(dxh, axis=1, keepdims=True)
        m2 = jnp.mean(dxh * xhat, axis=1, keepdims=True)
        dpre = rstd * (dxh - m1 - xhat * m2)
        dp_ref[...] = dpre
        dpb_ref[...] = dpre.astype(BF16)
        dg = jnp.sum(dy * xhat, axis=0, keepdims=True)
        db = jnp.sum(dy, axis=0, keepdims=True)

        @pl.when(pl.program_id(0) == 0)
        def _():
            dg_ref[...] = dg
            db_ref[...] = db

        @pl.when(pl.program_id(0) > 0)
        def _():
            dg_ref[...] += dg
            db_ref[...] += db

    row = pl.BlockSpec((rb, D), lambda i: (i, 0))
    vec = pl.BlockSpec((1, D), lambda i: (0, 0))
    body, in_specs, args = _ordered(body, [row, row, vec] + [row] * nt,
                                    (x, h, g.reshape(1, D), *[a for _, a in terms]), after)
    return pl.pallas_call(
        body, grid=(T // rb,), in_specs=in_specs,
        out_specs=[row, row, vec, vec],
        out_shape=[jax.ShapeDtypeStruct((T, D), F32), jax.ShapeDtypeStruct((T, D), BF16),
                   jax.ShapeDtypeStruct((1, D), F32), jax.ShapeDtypeStruct((1, D), F32)],
        compiler_params=_params(("arbitrary",)), name=name,
    )(*args)


def _lincomb(terms, out_dtype, *, name):
    R, C = terms[0][1].shape
    rb = _pick(R, 512)
    coefs = [c for c, _ in terms]
    nt = len(terms)

    def body(*refs):
        acc = None
        for c, r in zip(coefs, refs[:nt]):
            v = r[...].astype(F32)
            v = v if c == 1.0 else c * v
            acc = v if acc is None else acc + v
        refs[nt][...] = acc.astype(out_dtype)

    row = pl.BlockSpec((rb, C), lambda i: (i, 0))
    return pl.pallas_call(
        body, grid=(R // rb,), in_specs=[row] * nt, out_specs=row,
        out_shape=jax.ShapeDtypeStruct((R, C), out_dtype),
        compiler_params=_params(("parallel",)), name=name,
    )(*[a for _, a in terms])


def _loss_grad(y, tgt, *, name):
    T, D = y.shape
    rb = _pick(T, 512)

    def body(y_ref, t_ref, dy_ref, l_ref):
        err = y_ref[...] - t_ref[...]
        dy_ref[...] = err * (1.0 / D)
        part = jnp.sum(jnp.sum(err * err, axis=1, keepdims=True), axis=0, keepdims=True) * (0.5 / D)
        part = jnp.broadcast_to(part, (1, LANES))

        @pl.when(pl.program_id(0) == 0)
        def _():
            l_ref[...] = part

        @pl.when(pl.program_id(0) > 0)
        def _():
            l_ref[...] += part

    row = pl.BlockSpec((rb, D), lambda i: (i, 0))
    return pl.pallas_call(
        body, grid=(T // rb,), in_specs=[row, row],
        out_specs=[row, pl.BlockSpec((1, LANES), lambda i: (0, 0))],
        out_shape=[jax.ShapeDtypeStruct((T, D), F32), jax.ShapeDtypeStruct((1, LANES), F32)],
        compiler_params=_params(("arbitrary",)), name=name,
    )(y, tgt)


def _sgu_fwd(zp, ln_g, ln_b, ws, bst, *, name):
    T, E2 = zp.shape
    E = E2 // 2
    G = ws.shape[0]
    cg = E // G
    rb = 2 * CHUNK

    def body(z_ref, g_ref, b_ref, ws_ref, bs_ref, y_ref):
        u = _gelu(z_ref[:, :E].astype(F32))
        v = _gelu(z_ref[:, E:].astype(F32))
        mu = jnp.mean(v, axis=1, keepdims=True)
        cen = v - mu
        var = jnp.mean(cen * cen, axis=1, keepdims=True)
        vn = (cen * lax.rsqrt(var + LN_EPS) * g_ref[...] + b_ref[...]).astype(BF16)
        for ci in range(rb // CHUNK):
            rows = slice(ci * CHUNK, (ci + 1) * CHUNK)
            for gi in range(G):
                cols = slice(gi * cg, (gi + 1) * cg)
                sv = jnp.dot(ws_ref[gi], vn[rows, cols], preferred_element_type=F32)
                sv = sv + bs_ref[:, gi:gi + 1]
                y_ref[rows, cols] = (u[rows, cols] * sv).astype(BF16)

    return pl.pallas_call(
        body, grid=(T // rb,),
        in_specs=[pl.BlockSpec((rb, E2), lambda i: (i, 0)),
                  pl.BlockSpec((1, E), lambda i: (0, 0)), pl.BlockSpec((1, E), lambda i: (0, 0)),
                  pl.BlockSpec((G, CHUNK, CHUNK), lambda i: (0, 0, 0)),
                  pl.BlockSpec((CHUNK, G), lambda i: (0, 0))],
        out_specs=pl.BlockSpec((rb, E), lambda i: (i, 0)),
        out_shape=jax.ShapeDtypeStruct((T, E), BF16),
        compiler_params=_params(("parallel",)), name=name,
    )(zp, ln_g.reshape(1, E), ln_b.reshape(1, E), ws, bst)


def _sgu_bwd(zp, dy, ln_g, ln_b, ws, wst, bst, *, after=None, name):
    T, E2 = zp.shape
    E = E2 // 2
    G = ws.shape[0]
    cg = E // G
    rb = CHUNK
    nsteps = T // rb

    def body(z_ref, dy_ref, g_ref, b_ref, ws_ref, wst_ref, bs_ref,
             dz_ref, dg_ref, db_ref, dws_ref, dbs_ref, dsv_acc):
        step = pl.program_id(0)

        @pl.when(step == 0)
        def _():
            dg_ref[...] = jnp.zeros_like(dg_ref)
            db_ref[...] = jnp.zeros_like(db_ref)
            dws_ref[...] = jnp.zeros_like(dws_ref)
            dsv_acc[...] = jnp.zeros_like(dsv_acc)

        zu = z_ref[:, :E].astype(F32)
        zv = z_ref[:, E:].astype(F32)
        u = _gelu(zu)
        v = _gelu(zv)
        mu = jnp.mean(v, axis=1, keepdims=True)
        cen = v - mu
        var = jnp.mean(cen * cen, axis=1, keepdims=True)
        rstd = lax.rsqrt(var + LN_EPS)
        xhat = cen * rstd
        vn = (xhat * g_ref[...] + b_ref[...]).astype(BF16)
        dyv = dy_ref[...].astype(F32)
        dsv = dyv * u
        dsv_acc[...] += dsv
        dsvb = dsv.astype(BF16)
        tril = (lax.broadcasted_iota(jnp.int32, (CHUNK, CHUNK), 0)
                >= lax.broadcasted_iota(jnp.int32, (CHUNK, CHUNK), 1))
        du_parts = []
        dvn_parts = []
        for gi in range(G):
            cols = slice(gi * cg, (gi + 1) * cg)
            sv = jnp.dot(ws_ref[gi], vn[:, cols], preferred_element_type=F32) + bs_ref[:, gi:gi + 1]
            du_parts.append(dyv[:, cols] * sv)
            dvn_parts.append(jnp.dot(wst_ref[gi], dsvb[:, cols], preferred_element_type=F32))
            dw = lax.dot_general(dsvb[:, cols], vn[:, cols], (((1,), (1,)), ((), ())),
                                 preferred_element_type=F32)
            dws_ref[gi] += jnp.where(tril, dw, 0.0)
        du = jnp.concatenate(du_parts, axis=1)
        dvn = jnp.concatenate(dvn_parts, axis=1)
        dg_ref[...] += jnp.sum(dvn * xhat, axis=0, keepdims=True)
        db_ref[...] += jnp.sum(dvn, axis=0, keepdims=True)
        dxh = dvn * g_ref[...]
        m1 = jnp.mean(dxh, axis=1, keepdims=True)
        m2 = jnp.mean(dxh * xhat, axis=1, keepdims=True)
        dv = rstd * (dxh - m1 - xhat * m2)
        dz_ref[:, :E] = (du * _gelu_grad(zu)).astype(BF16)
        dz_ref[:, E:] = (dv * _gelu_grad(zv)).astype(BF16)

        @pl.when(step == nsteps - 1)
        def _():
            lane = lax.broadcasted_iota(jnp.int32, (CHUNK, LANES), 1)
            out = jnp.zeros((CHUNK, LANES), F32)
            for gi in range(G):
                s = jnp.sum(dsv_acc[:, gi * cg:(gi + 1) * cg], axis=1, keepdims=True)
                out = jnp.where(lane == gi, s, out)
            dbs_ref[...] = out

    vecE = pl.BlockSpec((1, E), lambda i: (0, 0))
    wspec = pl.BlockSpec((G, CHUNK, CHUNK), lambda i: (0, 0, 0))
    body, in_specs, args = _ordered(
        body, [pl.BlockSpec((rb, E2), lambda i: (i, 0)), pl.BlockSpec((rb, E), lambda i: (i, 0)),
               vecE, vecE, wspec, wspec, pl.BlockSpec((CHUNK, G), lambda i: (0, 0))],
        (zp, dy, ln_g.reshape(1, E), ln_b.reshape(1, E), ws, wst, bst), after)
    return pl.pallas_call(
        body, grid=(nsteps,), in_specs=in_specs,
        out_specs=[pl.BlockSpec((rb, E2), lambda i: (i, 0)), vecE, vecE, wspec,
                   pl.BlockSpec((CHUNK, LANES), lambda i: (0, 0))],
        out_shape=[jax.ShapeDtypeStruct((T, E2), BF16), jax.ShapeDtypeStruct((1, E), F32),
                   jax.ShapeDtypeStruct((1, E), F32), jax.ShapeDtypeStruct((G, CHUNK, CHUNK), F32),
                   jax.ShapeDtypeStruct((CHUNK, LANES), F32)],
        scratch_shapes=[pltpu.VMEM((CHUNK, E), F32)],
        compiler_params=_params(("arbitrary",)), name=name,
    )(*args)


def _shift_down(x, k, row):
    return jnp.where(row >= k, pltpu.roll(x, k, 0), 0.0)


def _shift_up(x, k, row, T):
    return jnp.where(row < T - k, pltpu.roll(x, T - k, 0), 0.0)


def _conv3(x, w_ref, b_ref, row):
    return (w_ref[0:1, :] * _shift_down(x, 2, row) + w_ref[1:2, :] * _shift_down(x, 1, row)
            + w_ref[2:3, :] * x + b_ref[...])


def _convgate_fwd(hh, cw, cb, *, name):
    T, F2 = hh.shape
    F = F2 // 2
    ns = F // LANES

    def body(a_ref, g_ref, wa_ref, wg_ref, ba_ref, bg_ref, o_ref):
        row = lax.broadcasted_iota(jnp.int32, (T, LANES), 0)
        ca = _conv3(a_ref[...].astype(F32), wa_ref, ba_ref, row)
        cgv = _conv3(g_ref[...].astype(F32), wg_ref, bg_ref, row)
        o_ref[...] = (_gelu(ca) * cgv).astype(BF16)

    sa = lambda r: pl.BlockSpec((r, LANES), lambda j: (0, j))
    sg = lambda r: pl.BlockSpec((r, LANES), lambda j: (0, j + ns))
    return pl.pallas_call(
        body, grid=(ns,), in_specs=[sa(T), sg(T), sa(3), sg(3), sa(1), sg(1)],
        out_specs=sa(T), out_shape=jax.ShapeDtypeStruct((T, F), BF16),
        compiler_params=_params(("parallel",)), name=name,
    )(hh, hh, cw, cw, cb, cb)


def _convgate_bwd(hh, dact, cw, cb, *, name):
    T, F2 = hh.shape
    F = F2 // 2
    ns = F // LANES

    def body(a_ref, g_ref, d_ref, wa_ref, wg_ref, ba_ref, bg_ref,
             da_ref, dg_ref, dwa_ref, dwg_ref, dba_ref, dbg_ref):
        row = lax.broadcasted_iota(jnp.int32, (T, LANES), 0)
        d = d_ref[...].astype(F32)
        ca = _conv3(a_ref[...].astype(F32), wa_ref, ba_ref, row)
        cgv = _conv3(g_ref[...].astype(F32), wg_ref, bg_ref, row)
        cdf = 0.5 * (1.0 + lax.erf(ca * INV_SQRT2))
        dca = d * cgv * (cdf + ca * jnp.exp(-0.5 * ca * ca) * INV_SQRT_2PI)
        dcg = d * (ca * cdf)
        for x_ref, w_ref, dc, dx_ref, dw_ref, db_ref in (
                (a_ref, wa_ref, dca, da_ref, dwa_ref, dba_ref),
                (g_ref, wg_ref, dcg, dg_ref, dwg_ref, dbg_ref)):
            x = x_ref[...].astype(F32)
            dx = (w_ref[2:3, :] * dc + w_ref[1:2, :] * _shift_up(dc, 1, row, T)
                  + w_ref[0:1, :] * _shift_up(dc, 2, row, T))
            dx_ref[...] = dx.astype(BF16)
            dw_ref[0:1, :] = jnp.sum(dc * _shift_down(x, 2, row), axis=0, keepdims=True)
            dw_ref[1:2, :] = jnp.sum(dc * _shift_down(x, 1, row), axis=0, keepdims=True)
            dw_ref[2:3, :] = jnp.sum(dc * x, axis=0, keepdims=True)
            db_ref[...] = jnp.sum(dc, axis=0, keepdims=True)

    sa = lambda r: pl.BlockSpec((r, LANES), lambda j: (0, j))
    sg = lambda r: pl.BlockSpec((r, LANES), lambda j: (0, j + ns))
    return pl.pallas_call(
        body, grid=(ns,), in_specs=[sa(T), sg(T), sa(T), sa(3), sg(3), sa(1), sg(1)],
        out_specs=[sa(T), sa(T), sa(3), sa(3), sa(1), sa(1)],
        out_shape=[jax.ShapeDtypeStruct((T, F), BF16), jax.ShapeDtypeStruct((T, F), BF16),
                   jax.ShapeDtypeStruct((3, F), F32), jax.ShapeDtypeStruct((3, F), F32),
                   jax.ShapeDtypeStruct((1, F), F32), jax.ShapeDtypeStruct((1, F), F32)],
        compiler_params=_params(("parallel",)), name=name,
    )(hh, hh, dact, cw, cw, cb, cb)


def _bucket_maps():
    iq = np.arange(BLK)[:, None]
    ik = np.arange(2 * BLK)[None, :]
    delta = iq + BLK - ik
    maps = []
    for win, dil in DILATED_GROUPS:
        n = np.clip(delta, 0, None) * dil
        max_exact = REL_BUCKETS // 2
        nf = np.maximum(n, 1).astype(np.float32)
        large = max_exact + (np.log(nf / np.float32(max_exact)) / np.float32(math.log(REL_MAX_DIST / max_exact))
                             * np.float32(REL_BUCKETS - max_exact)).astype(np.int32)
        large = np.minimum(large, REL_BUCKETS - 1)
        bucket = np.where(n < max_exact, n, large)
        valid = (delta >= 0) & (delta <= win // dil)
        maps.append(np.where(valid, bucket, -1).astype(np.int32))
    return np.stack(maps)


def _band_bias(rel_table, bmap, H, *, name):
    def body(t_ref, m_ref, o_ref):
        g = pl.program_id(0)
        bm = m_ref[0]
        for h in range(H):
            acc = jnp.full((BLK, 2 * BLK), NEG, F32)
            for b in range(REL_BUCKETS):
                acc = jnp.where(bm == b, t_ref[b, g * H + h], acc)
            o_ref[0, h] = acc

    return pl.pallas_call(
        body, grid=(N_GROUPS,),
        in_specs=[pl.BlockSpec(memory_space=pltpu.SMEM),
                  pl.BlockSpec((1, BLK, 2 * BLK), lambda g: (g, 0, 0))],
        out_specs=pl.BlockSpec((1, H, BLK, 2 * BLK), lambda g: (g, 0, 0, 0)),
        out_shape=jax.ShapeDtypeStruct((N_GROUPS, H, BLK, 2 * BLK), F32),
        compiler_params=_params(("parallel",)), name=name,
    )(rel_table, bmap)


def _band_bias_bwd(dbias, bmap, H, *, name):
    def body(d_ref, m_ref, o_ref):
        bm = m_ref[0]
        rowi = lax.broadcasted_iota(jnp.int32, (REL_BUCKETS, LANES), 0)
        lane = lax.broadcasted_iota(jnp.int32, (REL_BUCKETS, LANES), 1)
        out = jnp.zeros((REL_BUCKETS, LANES), F32)
        for h in range(H):
            dv = d_ref[0, h]
            for b in range(REL_BUCKETS):
                s = jnp.sum(jnp.sum(jnp.where(bm == b, dv, 0.0), axis=1, keepdims=True),
                            axis=0, keepdims=True)
                out = jnp.where((rowi == b) & (lane == h), s, out)
        o_ref[0] = out

    return pl.pallas_call(
        body, grid=(N_GROUPS,),
        in_specs=[pl.BlockSpec((1, H, BLK, 2 * BLK), lambda g: (g, 0, 0, 0)),
                  pl.BlockSpec((1, BLK, 2 * BLK), lambda g: (g, 0, 0))],
        out_specs=pl.BlockSpec((1, REL_BUCKETS, LANES), lambda g: (g, 0, 0)),
        out_shape=jax.ShapeDtypeStruct((N_GROUPS, REL_BUCKETS, LANES), F32),
        compiler_params=_params(("parallel",)), name=name,
    )(dbias, bmap)


def _head_masks():
    lane = lax.broadcasted_iota(jnp.int32, (BLK, LANES), 1)
    return (lane < HEAD_DIM, lane >= HEAD_DIM)


def _attn_fwd(q, kv, bias, gi, *, name):
    T = q.shape[0]
    HD = kv.shape[1] // 2
    d = DILATED_GROUPS[gi][1]
    S = T // d
    NB = S // BLK
    H = HD // HEAD_DIM
    qv, qcol = (q, gi) if d == 1 else (q[:, gi * HD:(gi + 1) * HD].reshape(S, d * HD), 0)
    kvv = kv.reshape(S, d * 2 * HD)

    def body(q_ref, kp_ref, kc_ref, vp_ref, vc_ref, b_ref, o_ref, l_ref):
        n = pl.program_id(1)
        col = lax.broadcasted_iota(jnp.int32, (BLK, 2 * BLK), 1)
        first = (n == 0) & (col < BLK)
        hm = _head_masks()
        for p in range(HD // LANES):
            sl = slice(p * LANES, (p + 1) * LANES)
            qp = q_ref[:, sl]
            kc = jnp.concatenate([kp_ref[:, sl], kc_ref[:, sl]], axis=0)
            vc = jnp.concatenate([vp_ref[:, sl], vc_ref[:, sl]], axis=0)
            outs = []
            lses = []
            for hh in range(2):
                qm = jnp.where(hm[hh], qp, jnp.zeros_like(qp))
                s = lax.dot_general(qm, kc, (((1,), (1,)), ((), ())), preferred_element_type=F32)
                s = jnp.where(first, NEG, s + b_ref[2 * p + hh])
                m = jnp.max(s, axis=1, keepdims=True)
                e = jnp.exp(s - m)
                den = jnp.sum(e, axis=1, keepdims=True)
                outs.append(jnp.dot((e / den).astype(BF16), vc, preferred_element_type=F32))
                lses.append(m + jnp.log(den))
            o_ref[:, sl] = jnp.where(hm[0], outs[0], outs[1])
            l_ref[:, sl] = jnp.where(hm[0], lses[0], lses[1])

    blk = lambda f: pl.BlockSpec((BLK, HD), f)
    prev = lambda n: jnp.maximum(n - 1, 0)
    return pl.pallas_call(
        body, grid=(d, NB),
        in_specs=[blk(lambda r, n: (n, r + qcol)),
                  blk(lambda r, n: (prev(n), r * 2)), blk(lambda r, n: (n, r * 2)),
                  blk(lambda r, n: (prev(n), r * 2 + 1)), blk(lambda r, n: (n, r * 2 + 1)),
                  pl.BlockSpec((H, BLK, 2 * BLK), lambda r, n: (0, 0, 0))],
        out_specs=[blk(lambda r, n: (n, r)), blk(lambda r, n: (n, r))],
        out_shape=[jax.ShapeDtypeStruct((S, d * HD), F32), jax.ShapeDtypeStruct((S, d * HD), F32)],
        compiler_params=_params(("parallel", "parallel")), name=name,
    )(qv, kvv, kvv, kvv, kvv, bias)


def _attn_combine(os, ls, *, name):
    T, HD = os[0].shape
    rb = _pick(T, 512)

    def body(o0, o1, o2, l0, l1, l2, o_ref, ob_ref, l_ref):
        la, lb, lc = l0[...], l1[...], l2[...]
        m = jnp.maximum(jnp.maximum(la, lb), lc)
        L = m + jnp.log(jnp.exp(la - m) + jnp.exp(lb - m) + jnp.exp(lc - m))
        o = jnp.exp(la - L) * o0[...] + jnp.exp(lb - L) * o1[...] + jnp.exp(lc - L) * o2[...]
        o_ref[...] = o
        ob_ref[...] = o.astype(BF16)
        l_ref[...] = L

    row = pl.BlockSpec((rb, HD), lambda i: (i, 0))
    return pl.pallas_call(
        body, grid=(T // rb,), in_specs=[row] * 6, out_specs=[row] * 3,
        out_shape=[jax.ShapeDtypeStruct((T, HD), F32), jax.ShapeDtypeStruct((T, HD), BF16),
                   jax.ShapeDtypeStruct((T, HD), F32)],
        compiler_params=_params(("parallel",)), name=name,
    )(*[a.reshape(T, HD) for a in os], *[a.reshape(T, HD) for a in ls])


def _attn_bwd(q, kv, bias, do, o, L, gi, *, name):
    T = q.shape[0]
    HD = kv.shape[1] // 2
    d = DILATED_GROUPS[gi][1]
    S = T // d
    NB = S // BLK
    H = HD // HEAD_DIM
    qv, qcol = (q, gi) if d == 1 else (q[:, gi * HD:(gi + 1) * HD].reshape(S, d * HD), 0)
    kvv = kv.reshape(S, d * 2 * HD)
    dov, ov, Lv = (a.reshape(S, d * HD) for a in (do, o, L))

    def body(q_ref, kp_ref, kc_ref, vp_ref, vc_ref, b_ref, do_ref, o_ref, L_ref,
             dq_ref, dk_ref, dv_ref, db_ref, ck_ref, cv_ref):
        r = pl.program_id(0)
        n = pl.program_id(1)

        @pl.when((r == 0) & (n == 0))
        def _():
            db_ref[...] = jnp.zeros_like(db_ref)

        @pl.when(n == 0)
        def _():
            ck_ref[...] = jnp.zeros_like(ck_ref)
            cv_ref[...] = jnp.zeros_like(cv_ref)

        @pl.when(n < NB)
        def _():
            col = lax.broadcasted_iota(jnp.int32, (BLK, 2 * BLK), 1)
            first = (n == 0) & (col < BLK)
            hm = _head_masks()
            for p in range(HD // LANES):
                sl = slice(p * LANES, (p + 1) * LANES)
                qp = q_ref[:, sl]
                kc = jnp.concatenate([kp_ref[:, sl], kc_ref[:, sl]], axis=0)
                vc = jnp.concatenate([vp_ref[:, sl], vc_ref[:, sl]], axis=0)
                dop = do_ref[:, sl]
                dob = dop.astype(BF16)
                prod = dop * o_ref[:, sl]
                Lp = L_ref[:, sl]
                dq_parts = []
                dkc = None
                dvc = None
                for hh in range(2):
                    qm = jnp.where(hm[hh], qp, jnp.zeros_like(qp))
                    dom = jnp.where(hm[hh], dob, jnp.zeros_like(dob))
                    s = lax.dot_general(qm, kc, (((1,), (1,)), ((), ())), preferred_element_type=F32)
                    s = jnp.where(first, NEG, s + b_ref[2 * p + hh])
                    lse = Lp[:, hh * HEAD_DIM:hh * HEAD_DIM + 1]
                    pr = jnp.exp(s - lse)
                    dp = lax.dot_general(dom, vc, (((1,), (1,)), ((), ())), preferred_element_type=F32)
                    delta = jnp.sum(jnp.where(hm[hh], prod, 0.0), axis=1, keepdims=True)
                    ds = pr * (dp - delta)
                    db_ref[2 * p + hh] += ds
                    dsb = ds.astype(BF16)
                    dq_parts.append(jnp.dot(dsb, kc, preferred_element_type=F32))
                    dkh = lax.dot_general(dsb, qm, (((0,), (0,)), ((), ())), preferred_element_type=F32)
                    dvh = lax.dot_general(pr.astype(BF16), dom, (((0,), (0,)), ((), ())),
                                          preferred_element_type=F32)
                    dkc = dkh if dkc is None else dkc + dkh
                    dvc = dvh if dvc is None else dvc + dvh
                dq = jnp.where(hm[0], dq_parts[0], dq_parts[1])
                dq_ref[:, sl] = (dq * (HEAD_DIM ** -0.5)).astype(BF16)
                dk_ref[:, sl] = ck_ref[:, sl] + dkc[:BLK]
                dv_ref[:, sl] = cv_ref[:, sl] + dvc[:BLK]
                ck_ref[:, sl] = dkc[BLK:]
                cv_ref[:, sl] = dvc[BLK:]

        @pl.when(n == NB)
        def _():
            dk_ref[...] = ck_ref[...]
            dv_ref[...] = cv_ref[...]

    blk = lambda f: pl.BlockSpec((BLK, HD), f)
    cur = lambda n: jnp.minimum(n, NB - 1)
    prev = lambda n: jnp.maximum(jnp.minimum(n, NB - 1) - 1, 0)
    lag = lambda n: jnp.maximum(n - 1, 0)
    return pl.pallas_call(
        body, grid=(d, NB + 1),
        in_specs=[blk(lambda r, n: (cur(n), r + qcol)),
                  blk(lambda r, n: (prev(n), r * 2)), blk(lambda r, n: (cur(n), r * 2)),
                  blk(lambda r, n: (prev(n), r * 2 + 1)), blk(lambda r, n: (cur(n), r * 2 + 1)),
                  pl.BlockSpec((H, BLK, 2 * BLK), lambda r, n: (0, 0, 0)),
                  blk(lambda r, n: (cur(n), r)), blk(lambda r, n: (cur(n), r)),
                  blk(lambda r, n: (cur(n), r))],
        out_specs=[blk(lambda r, n: (cur(n), r)), blk(lambda r, n: (lag(n), r)),
                   blk(lambda r, n: (lag(n), r)),
                   pl.BlockSpec((H, BLK, 2 * BLK), lambda r, n: (0, 0, 0))],
        out_shape=[jax.ShapeDtypeStruct((S, d * HD), BF16), jax.ShapeDtypeStruct((S, d * HD), F32),
                   jax.ShapeDtypeStruct((S, d * HD), F32),
                   jax.ShapeDtypeStruct((H, BLK, 2 * BLK), F32)],
        scratch_shapes=[pltpu.VMEM((BLK, HD), F32), pltpu.VMEM((BLK, HD), F32)],
        compiler_params=_params(("arbitrary", "arbitrary")), name=name,
    )(qv, kvv, kvv, kvv, kvv, bias, dov, ov, Lv)


SUPER = DILATED_GROUPS[-1][1] * BLK


def _band_rows(it, d):
    r, j = it % d, it // d
    if d == 1:
        at = lambda blk: pl.ds(pl.multiple_of(blk * BLK, BLK), BLK)
    else:
        at = lambda blk: pl.ds(r + d * BLK * blk, BLK, stride=d)
    return at(j), at(jnp.maximum(j - 1, 0))


def _band_loops(step, d, unroll):
    n_it = SUPER // BLK

    def run(lo, hi, inside):
        if hi > lo:
            def body(it, carry):
                step(it, inside)
                return carry
            lax.fori_loop(lo, hi, body, 0, unroll=max(u for u in range(1, unroll + 1) if (hi - lo) % u == 0))

    run(0, d, False)
    run(d, n_it, True)


def _last_rows(it, d):
    m = SUPER // (d * BLK)
    if d == 1:
        return pl.ds((m - 1) * BLK, BLK)
    return pl.ds(it % d + d * BLK * (m - 1), BLK, stride=d)


def _attn_fwd_all(q, kv, bias, *, name):
    T = q.shape[0]
    HD = kv.shape[1] // 2
    PP = HD // LANES
    NS = T // SUPER

    def body(q0, q1, q2, kp_ref, kc_ref, vp_ref, vc_ref, b_ref, o_ref, ob_ref, l_ref, og, lg):
        n = pl.program_id(1)
        col = lax.broadcasted_iota(jnp.int32, (BLK, 2 * BLK), 1)
        hm = _head_masks()
        for g, (q_ref, (_, d)) in enumerate(zip((q0, q1, q2), DILATED_GROUPS)):
            def step(it, inside, g=g, q_ref=q_ref, d=d):
                cur, prv = _band_rows(it, d)
                qp = q_ref[cur, :].astype(BF16)
                if inside:
                    kprev, vprev = kc_ref[prv, :], vc_ref[prv, :]
                else:
                    last = _last_rows(it, d)
                    kprev, vprev = kp_ref[last, :], vp_ref[last, :]
                kc = jnp.concatenate([kprev.astype(BF16), kc_ref[cur, :].astype(BF16)], axis=0)
                vc = jnp.concatenate([vprev.astype(BF16), vc_ref[cur, :].astype(BF16)], axis=0)
                first = (n == 0) & (col < BLK)
                outs, lses = [], []
                for hh in range(2):
                    qm = jnp.where(hm[hh], qp, jnp.zeros_like(qp))
                    s = lax.dot_general(qm, kc, (((1,), (1,)), ((), ())), preferred_element_type=F32)
                    s = s + b_ref[g, hh]
                    if not inside:
                        s = jnp.where(first, NEG, s)
                    mx = jnp.max(s, axis=1, keepdims=True)
                    e = jnp.exp(s - mx)
                    den = jnp.sum(e, axis=1, keepdims=True)
                    outs.append(jnp.dot((e / den).astype(BF16), vc, preferred_element_type=F32))
                    lses.append(mx + jnp.log(den))
                og.at[g][cur, :] = jnp.where(hm[0], outs[0], outs[1])
                lg.at[g][cur, :] = jnp.where(hm[0], lses[0], lses[1])

            _band_loops(step, d, 8)
        la, lb, lc = lg[0], lg[1], lg[2]
        mx = jnp.maximum(jnp.maximum(la, lb), lc)
        L = mx + jnp.log(jnp.exp(la - mx) + jnp.exp(lb - mx) + jnp.exp(lc - mx))
        o = jnp.exp(la - L) * og[0] + jnp.exp(lb - L) * og[1] + jnp.exp(lc - L) * og[2]
        o_ref[...] = o
        ob_ref[...] = o.astype(BF16)
        l_ref[...] = L

    blk = lambda f: pl.BlockSpec((SUPER, LANES), f)
    prev = lambda n: jnp.maximum(n - 1, 0)
    qspec = lambda g: blk(lambda p, n: (n, g * PP + p))
    return pl.pallas_call(
        body, grid=(PP, NS),
        in_specs=[qspec(0), qspec(1), qspec(2),
                  blk(lambda p, n: (prev(n), p)), blk(lambda p, n: (n, p)),
                  blk(lambda p, n: (prev(n), PP + p)), blk(lambda p, n: (n, PP + p)),
                  pl.BlockSpec((N_GROUPS, 2, BLK, 2 * BLK), lambda p, n: (0, p, 0, 0))],
        out_specs=[blk(lambda p, n: (n, p))] * 3,
        out_shape=[jax.ShapeDtypeStruct((T, HD), F32), jax.ShapeDtypeStruct((T, HD), BF16),
                   jax.ShapeDtypeStruct((T, HD), F32)],
        scratch_shapes=[pltpu.VMEM((N_GROUPS, SUPER, LANES), F32), pltpu.VMEM((N_GROUPS, SUPER, LANES), F32)],
        compiler_params=_params(("parallel", "parallel")), name=name,
    )(q, q, q, kv, kv, kv, kv, bias)


def _attn_bwd_all(q, kv, bias, do, o, L, *, after=None, name):
    T = q.shape[0]
    HD = kv.shape[1] // 2
    PP = HD // LANES
    H = HD // HEAD_DIM
    NS = T // SUPER

    def body(q0, q1, q2, kp_ref, kc_ref, vp_ref, vc_ref, b_ref, do_ref, o_ref, L_ref,
             dq_ref, dk_ref, dv_ref, db_ref, ck_ref, cv_ref):
        n = pl.program_id(1)

        @pl.when(n == 0)
        def _():
            db_ref[...] = jnp.zeros_like(db_ref)
            ck_ref[...] = jnp.zeros_like(ck_ref)
            cv_ref[...] = jnp.zeros_like(cv_ref)

        dk_ref[...] = ck_ref[...]
        dv_ref[...] = cv_ref[...]
        ck_ref[...] = jnp.zeros_like(ck_ref)
        cv_ref[...] = jnp.zeros_like(cv_ref)

        @pl.when(n < NS)
        def _():
            col = lax.broadcasted_iota(jnp.int32, (BLK, 2 * BLK), 1)
            hm = _head_masks()
            for g, (q_ref, (_, d)) in enumerate(zip((q0, q1, q2), DILATED_GROUPS)):
                def step(it, inside, g=g, q_ref=q_ref, d=d):
                    cur, prv = _band_rows(it, d)
                    last = _last_rows(it, d)
                    qp = q_ref[cur, :].astype(BF16)
                    if inside:
                        kprev, vprev = kc_ref[prv, :], vc_ref[prv, :]
                    else:
                        kprev, vprev = kp_ref[last, :], vp_ref[last, :]
                    kc = jnp.concatenate([kprev.astype(BF16), kc_ref[cur, :].astype(BF16)], axis=0)
                    vc = jnp.concatenate([vprev.astype(BF16), vc_ref[cur, :].astype(BF16)], axis=0)
                    first = (n == 0) & (col < BLK)
                    dop = do_ref[cur, :]
                    dob = dop.astype(BF16)
                    prod = dop * o_ref[cur, :]
                    Lp = L_ref[cur, :]
                    dq_parts = []
                    dkc = None
                    dvc = None
                    for hh in range(2):
                        qm = jnp.where(hm[hh], qp, jnp.zeros_like(qp))
                        dom = jnp.where(hm[hh], dob, jnp.zeros_like(dob))
                        s = lax.dot_general(qm, kc, (((1,), (1,)), ((), ())), preferred_element_type=F32)
                        s = s + b_ref[g, hh]
                        if not inside:
                            s = jnp.where(first, NEG, s)
                        lse = Lp[:, hh * HEAD_DIM:hh * HEAD_DIM + 1]
                        pr = jnp.exp(s - lse)
                        dp = lax.dot_general(dom, vc, (((1,), (1,)), ((), ())), preferred_element_type=F32)
                        delta = jnp.sum(jnp.where(hm[hh], prod, 0.0), axis=1, keepdims=True)
                        ds = pr * (dp - delta)
                        db_ref[g, hh] += ds
                        dsb = ds.astype(BF16)
                        dq_parts.append(jnp.dot(dsb, kc, preferred_element_type=F32))
                        dkh = lax.dot_general(dsb, qm, (((0,), (0,)), ((), ())), preferred_element_type=F32)
                        dvh = lax.dot_general(pr.astype(BF16), dom, (((0,), (0,)), ((), ())),
                                              preferred_element_type=F32)
                        dkc = dkh if dkc is None else dkc + dkh
                        dvc = dvh if dvc is None else dvc + dvh
                    dq_ref.at[g][cur, :] = jnp.where(hm[0], dq_parts[0], dq_parts[1]) * (HEAD_DIM ** -0.5)
                    ck_ref[cur, :] += dkc[BLK:]
                    cv_ref[cur, :] += dvc[BLK:]
                    if inside:
                        ck_ref[prv, :] += dkc[:BLK]
                        cv_ref[prv, :] += dvc[:BLK]
                    else:
                        dk_ref[last, :] += dkc[:BLK]
                        dv_ref[last, :] += dvc[:BLK]

                _band_loops(step, d, 4)

    blk = lambda f: pl.BlockSpec((SUPER, LANES), f)
    cur = lambda n: jnp.minimum(n, NS - 1)
    prev = lambda n: jnp.maximum(jnp.minimum(n, NS - 1) - 1, 0)
    lag = lambda n: jnp.maximum(n - 1, 0)
    qspec = lambda g: blk(lambda p, n: (cur(n), g * PP + p))
    bspec = pl.BlockSpec((N_GROUPS, 2, BLK, 2 * BLK), lambda p, n: (0, p, 0, 0))
    body, in_specs, args = _ordered(
        body, [qspec(0), qspec(1), qspec(2),
               blk(lambda p, n: (prev(n), p)), blk(lambda p, n: (cur(n), p)),
               blk(lambda p, n: (prev(n), PP + p)), blk(lambda p, n: (cur(n), PP + p)),
               bspec, blk(lambda p, n: (cur(n), p)), blk(lambda p, n: (cur(n), p)),
               blk(lambda p, n: (cur(n), p))],
        (q, q, q, kv, kv, kv, kv, bias, do, o, L), after)
    return pl.pallas_call(
        body, grid=(PP, NS + 1), in_specs=in_specs,
        out_specs=[pl.BlockSpec((N_GROUPS, SUPER, LANES), lambda p, n: (0, cur(n), p)),
                   blk(lambda p, n: (lag(n), p)), blk(lambda p, n: (lag(n), p)), bspec],
        out_shape=[jax.ShapeDtypeStruct((N_GROUPS, T, HD), F32), jax.ShapeDtypeStruct((T, HD), F32),
                   jax.ShapeDtypeStruct((T, HD), F32),
                   jax.ShapeDtypeStruct((N_GROUPS, H, BLK, 2 * BLK), F32)],
        scratch_shapes=[pltpu.VMEM((SUPER, LANES), F32), pltpu.VMEM((SUPER, LANES), F32)],
        compiler_params=_params(("arbitrary", "arbitrary")), name=name,
    )(*args)


class _Weights(dict):
    def __init__(self, base, fetch=None, emit=None, emit_small=None):
        super().__init__(base)
        self._fetch, self._emit, self._emit_small = fetch, emit, emit_small

    def fetch(self, group, after):
        if self._fetch is not None:
            for (key, layer), mat in self._fetch(group, after).items():
                self[key][layer] = mat

    def emit(self, group, mats):
        return None if self._emit is None else self._emit(group, mats)

    def emit_small(self, grads):
        return None if self._emit_small is None else self._emit_small(grads)


def _local_step(x, tgt, W):
    T, D = x.shape
    H = W["rel_table"].shape[1] // N_GROUPS
    HD = H * HEAD_DIM
    G = W["a_w_s"].shape[1]
    assert T % (DILATED_GROUPS[-1][1] * BLK) == 0

    tril = jnp.tril(jnp.ones((CHUNK, CHUNK), F32))
    bmap = jnp.asarray(_bucket_maps())
    bias = _band_bias(W["rel_table"], bmap, H, name="band_bias")

    saved = []
    xc, xcb = x, x.astype(BF16)
    kvb = None
    for i in range(DEPTH):
        s = {"x": xc, "xb": xcb}
        W.fetch(4 * i, xc)
        if i < N_A:
            ws_m = W["a_w_s"][i] * tril
            s["ws"] = ws_m.astype(BF16)
            s["wst"] = jnp.swapaxes(ws_m, 1, 2).astype(BF16)
            s["bst"] = W["a_b_s"][i].T
            s["zp"] = _mm(xcb, W["a_w_in"][i], out_dtype=ACT, name=f"a_in_{i}")
            s["y"] = _sgu_fwd(s["zp"], W["a_ln_g"][i], W["a_ln_b"][i], s["ws"], s["bst"], name=f"sgu_fwd_{i}")
            W.fetch(4 * i + 1, s["zp"])
            s["h"] = _mm(s["y"], W["a_w_out"][i], out_dtype=ACT, name=f"a_out_{i}")
        else:
            j = i - N_A
            if kvb is None:
                kvb = _mm(xcb, W["kv_w"][0], name="kv_proj")
            s["q"] = _mm(xcb, W["b_w_q_t"][j], tb=True, scale=HEAD_DIM ** -0.5, name=f"q_proj_{j}")
            s["o"], s["ob"], s["L"] = _attn_fwd_all(s["q"], kvb, bias, name=f"attn_fwd_{j}")
            W.fetch(4 * i + 1, s["q"])
            s["h"] = _mm(s["ob"], W["b_w_o"][j], out_dtype=ACT, name=f"o_proj_{j}")
        s["x1"], s["x1b"] = _add_ln_fwd(xc, s["h"], W["ln_g"][i, 0], W["ln_b"][i, 0], name=f"ln1_fwd_{i}")
        W.fetch(4 * i + 2, s["x1"])
        s["hh"] = _mm(s["x1b"], W["ffn_w_up_t"][i], tb=True, out_dtype=ACT, name=f"ffn_up_{i}")
        s["cw"] = W["ffn_conv_w"][i]
        s["cb"] = W["ffn_conv_b"][i].reshape(1, -1)
        s["act"] = _convgate_fwd(s["hh"], s["cw"], s["cb"], name=f"convgate_fwd_{i}")
        W.fetch(4 * i + 3, s["hh"])
        s["f"] = _mm(s["act"], W["ffn_w_down"][i], out_dtype=ACT, name=f"ffn_down_{i}")
        xc, xcb = _add_ln_fwd(s["x1"], s["f"], W["ln_g"][i, 1], W["ln_b"][i, 1], name=f"ln2_fwd_{i}")
        saved.append(s)

    dy, lossv = _loss_grad(xc, tgt, name="loss_grad")
    loss = lossv[0, 0]

    gl = {k: [None] * DEPTH for k in ("ffn_w_up_t", "ffn_conv_w", "ffn_conv_b", "ffn_w_down", "ln_g", "ln_b")}
    ga = {k: [None] * N_A for k in ("a_w_in", "a_ln_g", "a_ln_b", "a_w_s", "a_b_s", "a_w_out")}
    gb = {k: [None] * (DEPTH - N_A) for k in ("b_w_q_t", "b_w_o")}
    mats = ("a_w_in", "a_w_out", "b_w_q_t", "b_w_o", "ffn_w_up_t", "ffn_w_down")
    dks, dvs, dbias = [], [], []
    grads = {}
    terms = [(1.0, dy)]
    tok = None
    small_keys = ("ffn_conv_w", "ffn_conv_b", "ln_g", "ln_b", "a_ln_g", "a_ln_b", "a_w_s", "a_b_s")
    for i in reversed(range(DEPTH)):
        s = saved[i]
        dp2, dp2b, dg2, db2 = _add_ln_bwd(s["x1"], s["f"], W["ln_g"][i, 1], terms, after=tok, name=f"ln2_bwd_{i}")
        dact = _mm(dp2b, W["ffn_w_down"][i], tb=True, out_dtype=ACT, name=f"ffn_down_dx_{i}")
        gl["ffn_w_down"][i] = _mm(s["act"], dp2b, ta=True, out_dtype=BF16, name=f"ffn_down_dw_{i}")
        dha, dhg, dwa, dwg, dba, dbg = _convgate_bwd(s["hh"], dact, s["cw"], s["cb"], name=f"convgate_bwd_{i}")
        dhh = (dha, dhg)
        gl["ffn_conv_w"][i] = jnp.concatenate([dwa, dwg], axis=1)
        gl["ffn_conv_b"][i] = jnp.concatenate([dba, dbg], axis=1)[0]
        dx1 = _mm(dhh, W["ffn_w_up_t"][i], out_dtype=ACT, name=f"ffn_up_dx_{i}")
        gl["ffn_w_up_t"][i] = _mm(dhh, s["x1b"], ta=True, out_dtype=BF16, name=f"ffn_up_dw_{i}")
        tok = W.emit(3 * i + 2, {("ffn_w_up_t", i): gl["ffn_w_up_t"][i], ("ffn_w_down", i): gl["ffn_w_down"][i]})
        dp1, dp1b, dg1, db1 = _add_ln_bwd(s["x"], s["h"], W["ln_g"][i, 0], [(ALPHA, dp2), (1.0, dx1)],
                                          after=tok, name=f"ln1_bwd_{i}")
        gl["ln_g"][i] = jnp.concatenate([dg1, dg2], axis=0)
        gl["ln_b"][i] = jnp.concatenate([db1, db2], axis=0)
        terms = [(ALPHA, dp1)]
        if i < N_A:
            dyy = _mm(dp1b, W["a_w_out"][i], tb=True, out_dtype=ACT, name=f"a_out_dx_{i}")
            ga["a_w_out"][i] = _mm(s["y"], dp1b, ta=True, out_dtype=BF16, name=f"a_out_dw_{i}")
            tok = W.emit(3 * i + 1, {("a_w_out", i): ga["a_w_out"][i]})
            dzp, dlg, dlb, dws, dbs = _sgu_bwd(s["zp"], dyy, W["a_ln_g"][i], W["a_ln_b"][i], s["ws"],
                                               s["wst"], s["bst"], after=tok, name=f"sgu_bwd_{i}")
            ga["a_ln_g"][i], ga["a_ln_b"][i], ga["a_w_s"][i] = dlg[0], dlb[0], dws
            ga["a_b_s"][i] = dbs[:, :G].T
            if i == 0:
                for dct in (gl, ga):
                    grads.update({k: jnp.stack(v) for k, v in dct.items() if k in small_keys})
                tok = W.emit_small(grads)
            ga["a_w_in"][i] = _mm(s["xb"], dzp, ta=True, out_dtype=BF16, after=tok, name=f"a_in_dw_{i}")
            tok = W.emit(3 * i, {("a_w_in", i): ga["a_w_in"][i]})
            terms.append((1.0, _mm(dzp, W["a_w_in"][i], tb=True, out_dtype=ACT, after=tok, name=f"a_in_dx_{i}")))
        else:
            j = i - N_A
            do = _mm(dp1b, W["b_w_o"][j], tb=True, name=f"o_proj_dx_{j}")
            gb["b_w_o"][j] = _mm(s["ob"], dp1b, ta=True, out_dtype=BF16, name=f"o_proj_dw_{j}")
            tok = W.emit(3 * i + 1, {("b_w_o", j): gb["b_w_o"][j]})
            dq, dk_j, dv_j, db_j = _attn_bwd_all(s["q"], kvb, bias, do, s["o"], s["L"], after=tok,
                                                 name=f"attn_bwd_{j}")
            dks.append((1.0, dk_j))
            dvs.append((1.0, dv_j))
            dbias.append(db_j)
            terms.append((1.0, _mm(dq, W["b_w_q_t"][j], out_dtype=ACT, name=f"q_proj_dx_{j}")))
            gb["b_w_q_t"][j] = _mm(dq, s["xb"], ta=True, out_dtype=BF16, name=f"q_proj_dw_{j}")
            out_b = {("b_w_q_t", j): gb["b_w_q_t"][j]}
            if i == N_A:
                dkv = jnp.concatenate([_lincomb(dks, BF16, name="dk_sum"), _lincomb(dvs, BF16, name="dv_sum")],
                                      axis=1)
                terms.append((1.0, _mm(dkv, W["kv_w"][0], tb=True, out_dtype=ACT, name="kv_proj_dx")))
                grads["kv_w"] = [_mm(s["xb"], dkv, ta=True, out_dtype=BF16, name="kv_proj_dw")]
                out_b[("kv_w", 0)] = grads["kv_w"][0]
                dbt = _lincomb([(1.0, a.reshape(-1, 2 * BLK)) for a in dbias], F32, name="dbias_sum")
                dtab = _band_bias_bwd(dbt.reshape(N_GROUPS, H, BLK, 2 * BLK), bmap, H, name="band_bias_bwd")
                grads["rel_table"] = jnp.transpose(dtab[:, :, :H], (1, 0, 2)).reshape(REL_BUCKETS, N_GROUPS * H)
            tok = W.emit(3 * i, out_b)
    grad_x = _lincomb(terms, F32, name="grad_x")
    for dct in (gl, ga, gb):
        grads.update({k: v for k, v in dct.items() if k in mats})
    return loss, grad_x, grads


def _my_index():
    return 4 * lax.axis_index("x") + 2 * lax.axis_index("y") + lax.axis_index("c")


HBM_SPEC = pl.BlockSpec(memory_space=pltpu.HBM)


def _block(ref, k, n, axis):
    off = pl.multiple_of(k * n, n)
    return ref.at[pl.ds(off, n), :] if axis == 0 else ref.at[:, pl.ds(off, n)]


def _gather_mats(local, axis, *, name):
    L, a, b = local.shape
    n = a if axis == 0 else b
    full = (a * N_DEV, b) if axis == 0 else (a, b * N_DEV)

    def body(x_ref, *rest):
        outs = rest[:L]
        send_sems, recv_sems, local_sems = rest[L:]
        x, y, c = lax.axis_index("x"), lax.axis_index("y"), lax.axis_index("c")
        me, sibling = (x, y, c), (x, y, 1 - c)
        chips = [(1 - x, y), (x, 1 - y), (1 - x, 1 - y)]

        def slot(l, px, py, pc):
            return _block(outs[l], 4 * px + 2 * py + pc, n, axis)

        def copy(l, k, blk, to, src=None):
            return pltpu.make_async_remote_copy(
                src_ref=slot(l, *blk) if src is None else src, dst_ref=slot(l, *blk),
                send_sem=send_sems.at[7 * l + k], recv_sem=recv_sems.at[7 * l + k],
                device_id=to, device_id_type=MESH)

        mine, first, passed = [], [], []
        for l in range(L):
            mine.append(pltpu.make_async_copy(x_ref.at[l], slot(l, *me), local_sems.at[l]))
            mine[-1].start()
            first.append(copy(l, 0, me, sibling, src=x_ref.at[l]))
            first += [copy(l, 1 + j, me, (*chip, c), src=x_ref.at[l]) for j, chip in enumerate(chips)]
        for cp in first:
            cp.start()
        for l in range(L):
            for j, chip in enumerate(chips):
                copy(l, 1 + j, (*chip, c), me).wait_recv()
                passed.append(copy(l, 4 + j, (*chip, c), sibling))
                passed[-1].start()
        for l in range(L):
            copy(l, 0, sibling, me).wait_recv()
            for j, chip in enumerate(chips):
                copy(l, 4 + j, (*chip, 1 - c), me).wait_recv()
        for cp in first + passed:
            cp.wait_send()
        for cp in mine:
            cp.wait()

    return pl.pallas_call(
        body, out_shape=[jax.ShapeDtypeStruct(full, local.dtype)] * L,
        in_specs=[HBM_SPEC], out_specs=[HBM_SPEC] * L,
        scratch_shapes=[pltpu.SemaphoreType.DMA((7 * L,)), pltpu.SemaphoreType.DMA((7 * L,)),
                        pltpu.SemaphoreType.DMA((L,))],
        name=name,
    )(local)


SEM_SPEC = pl.BlockSpec(memory_space=pltpu.SEMAPHORE)
FLOWING = pltpu.SideEffectType.DATAFLOW_SIDE_EFFECTING


def _peers(x, y, c):
    return [(1 - x if k & 4 else x, 1 - y if k & 2 else y, 1 - c if k & 1 else c) for k in range(1, N_DEV)]


def _ends(src_ref, land_ref, peer_index, me, n, axis, gather):
    if gather:
        return src_ref, _block(land_ref, me, n, axis)
    return _block(src_ref, peer_index, n, axis), land_ref.at[me]


def _send_start(groups, gather, *, name):
    flat = [(g, j, mat, axis) for g, items in enumerate(groups) for j, (mat, axis) in enumerate(items)]
    M, G = len(flat), len(groups)
    lands, ns = [], []
    for _, _, mat, axis in flat:
        A, B = mat.shape
        if gather:
            lands.append((A * N_DEV, B) if axis == 0 else (A, B * N_DEV))
            ns.append(A if axis == 0 else B)
        else:
            lands.append((N_DEV, A // N_DEV, B) if axis == 0 else (N_DEV, A, B // N_DEV))
            ns.append(A // N_DEV if axis == 0 else B // N_DEV)

    def body(*refs):
        src_refs, land_refs, sems = refs[:M], refs[M:2 * M], refs[2 * M:2 * M + 3 * G]
        token = refs[-1]
        x, y, c = lax.axis_index("x"), lax.axis_index("y"), lax.axis_index("c")
        me = 4 * x + 2 * y + c
        for i, (g, j, _, axis) in enumerate(flat):
            for k, (px, py, pc) in enumerate(_peers(x, y, c)):
                s, d = _ends(src_refs[i], land_refs[i], 4 * px + 2 * py + pc, me, ns[i], axis, gather)
                pltpu.make_async_remote_copy(
                    src_ref=s, dst_ref=d, send_sem=sems[3 * g].at[7 * j + k], recv_sem=sems[3 * g + 1].at[7 * j + k],
                    device_id=(px, py, pc), device_id_type=MESH).start()
            s, d = _ends(src_refs[i], land_refs[i], me, me, ns[i], axis, gather)
            pltpu.make_async_copy(s, d, sems[3 * g + 2].at[j]).start()
        token[...] = jnp.zeros_like(token)

    sem_shapes = []
    for items in groups:
        sem_shapes += [pltpu.SemaphoreType.DMA((7 * len(items),))] * 2 + [pltpu.SemaphoreType.DMA((len(items),))]
    outs = pl.pallas_call(
        body, name=name,
        out_shape=(*sem_shapes, *[pltpu.HBM(m.shape, m.dtype) for _, _, m, _ in flat],
                   *[pltpu.HBM(shp, m.dtype) for shp, (_, _, m, _) in zip(lands, flat)],
                   jax.ShapeDtypeStruct((8, LANES), F32)),
        in_specs=[HBM_SPEC] * (2 * M),
        out_specs=(*[SEM_SPEC] * (3 * G), *[HBM_SPEC] * (2 * M), pl.BlockSpec(memory_space=pltpu.VMEM)),
        input_output_aliases={i: 3 * G + i for i in range(2 * M)},
        compiler_params=pltpu.CompilerParams(has_side_effects=FLOWING),
    )(*[pltpu.with_memory_space_constraint(m, pltpu.HBM) for _, _, m, _ in flat],
      *[pltpu.with_memory_space_constraint(lax.empty(shp, m.dtype), pltpu.HBM)
        for shp, (_, _, m, _) in zip(lands, flat)])
    handles = []
    for g in range(G):
        idx = [i for i, f in enumerate(flat) if f[0] == g]
        handles.append((outs[3 * g], outs[3 * g + 1], outs[3 * g + 2], [outs[3 * G + i] for i in idx],
                        [outs[3 * G + M + i] for i in idx], [flat[i][3] for i in idx]))
    return handles, outs[-1]


def _send_wait(handle, gather, after, *, name):
    send_sems, recv_sems, local_sems, mats, lands, axes = handle
    n_m = len(mats)
    ns = []
    for mat, land, axis in zip(mats, lands, axes):
        ns.append(mat.shape[axis] if gather else land.shape[1 + axis])

    def body(*refs):
        src_refs, land_refs = refs[:n_m], refs[n_m:2 * n_m]
        ssem, rsem, lsem = refs[2 * n_m:2 * n_m + 3]
        x, y, c = lax.axis_index("x"), lax.axis_index("y"), lax.axis_index("c")
        me = 4 * x + 2 * y + c
        for j in range(n_m):
            for k, (px, py, pc) in enumerate(_peers(x, y, c)):
                s, d = _ends(src_refs[j], land_refs[j], 4 * px + 2 * py + pc, me, ns[j], axes[j], gather)
                cp = pltpu.make_async_remote_copy(
                    src_ref=s, dst_ref=d, send_sem=ssem.at[7 * j + k], recv_sem=rsem.at[7 * j + k],
                    device_id=(px, py, pc), device_id_type=MESH)
                cp.wait_send()
                cp.wait_recv()
            s, d = _ends(src_refs[j], land_refs[j], me, me, ns[j], axes[j], gather)
            pltpu.make_async_copy(s, d, lsem.at[j]).wait()

    outs = pl.pallas_call(
        body, name=name,
        out_shape=(*[pltpu.HBM(m.shape, m.dtype) for m in mats], *[pltpu.HBM(l.shape, l.dtype) for l in lands]),
        in_specs=[HBM_SPEC] * (2 * n_m) + [SEM_SPEC] * 3 + [pl.BlockSpec(memory_space=pl.ANY)],
        out_specs=tuple([HBM_SPEC] * (2 * n_m)),
        input_output_aliases={i: i for i in range(2 * n_m)},
        compiler_params=pltpu.CompilerParams(has_side_effects=FLOWING),
    )(*mats, *lands, send_sems, recv_sems, local_sems, after)
    return list(outs[n_m:])


def _sum_parts(parts, *, name):
    n, R, C = parts.shape
    rb = _pick(R, 512) if R % LANES == 0 else R

    def body(p_ref, o_ref):
        acc = p_ref[0].astype(F32)
        for k in range(1, n):
            acc = acc + p_ref[k].astype(F32)
        o_ref[...] = acc

    return pl.pallas_call(
        body, grid=(R // rb,), in_specs=[pl.BlockSpec((n, rb, C), lambda i: (0, i, 0))],
        out_specs=pl.BlockSpec((rb, C), lambda i: (i, 0)),
        out_shape=jax.ShapeDtypeStruct((R, C), F32),
        compiler_params=_params(("parallel",)), name=name,
    )(parts)


def _adamw(w, m, v, parts, *, name):
    L, R, C = w.shape
    n = parts[0].shape[0]
    cap = max(16, VMEM_LIMIT // 3 // (2 * L * n * C * parts[0].dtype.itemsize))
    rb = max([r for r in range(16, min(R, cap) + 1, 16) if R % r == 0], default=R)

    def body(w_ref, m_ref, v_ref, *rest):
        p_refs = rest[:L]
        g_ref, d_ref, nm_ref, nv_ref = rest[L:]
        for l in range(L):
            @pl.when(pl.program_id(0) == l)
            def _(p_ref=p_refs[l]):
                g = p_ref[0].astype(F32)
                for k in range(1, n):
                    g = g + p_ref[k].astype(F32)
                mn = ADAM_B1 * m_ref[...] + (1.0 - ADAM_B1) * g
                vn = ADAM_B2 * v_ref[...] + (1.0 - ADAM_B2) * jnp.square(g)
                m_hat = mn / (1.0 - ADAM_B1 ** ADAM_STEP)
                v_hat = vn / (1.0 - ADAM_B2 ** ADAM_STEP)
                g_ref[...] = g
                d_ref[...] = -ADAM_LR * (m_hat / (jnp.sqrt(v_hat) + ADAM_EPS) + ADAM_WD * w_ref[...])
                nm_ref[...] = mn
                nv_ref[...] = vn

    row = pl.BlockSpec((None, rb, C), lambda l, i: (l, i, 0))
    part = lambda k: pl.BlockSpec((n, rb, C), lambda l, i: (0, jnp.where(l == k, i, 0), 0))
    return pl.pallas_call(
        body, grid=(L, R // rb), in_specs=[row, row, row] + [part(k) for k in range(L)],
        out_specs=[row] * 4, out_shape=[jax.ShapeDtypeStruct((L, R, C), F32)] * 4,
        compiler_params=_params(("arbitrary", "arbitrary")), name=name,
    )(w, m, v, *parts)


BIG = (("a_w_in", "a_w_in", 1, False), ("a_w_out", "a_w_out", 0, False), ("kv_w", "kv_w", 0, False),
       ("b_w_q", "b_w_q_t", 0, True), ("b_w_o", "b_w_o", 1, False), ("ffn_w_up", "ffn_w_up_t", 0, True),
       ("ffn_w_down", "ffn_w_down", 0, False))
SMALL_SHARDED = (("a_ln_g", 1), ("a_ln_b", 1), ("ffn_conv_w", 2), ("ln_g", 2), ("ln_b", 2))
REPLICATED = ("a_w_s", "a_b_s", "rel_table", "ffn_conv_b")


def _pack_rows(arrs, lead=0):
    lshape = arrs[0].shape[:lead]
    p = jnp.concatenate([a.reshape(*lshape, -1, LANES) for a in arrs], axis=lead)
    pad = -p.shape[lead] % 8
    return jnp.pad(p, [(0, 0)] * lead + [(0, pad), (0, 0)])


def _unpack_rows(packed, shapes, lead=0):
    lshape = packed.shape[:lead]
    out, off = [], 0
    for shp in shapes:
        r = int(np.prod(shp)) // LANES
        out.append(lax.slice_in_dim(packed, off, off + r, axis=lead).reshape(*lshape, *shp))
        off += r
    return out


def _as_mats(a, transposed):
    a = a[None] if a.ndim == 2 else a
    return jnp.swapaxes(a, 1, 2) if transposed else a


def _merge_shards(stacked, axis):
    a = jnp.moveaxis(stacked, 0, axis)
    shp = list(a.shape)
    return a.reshape(shp[:axis] + [shp[axis] * shp[axis + 1]] + shp[axis + 2:])


def _split_shards(full, axis):
    shp = list(full.shape)
    a = full.reshape(shp[:axis] + [N_DEV, shp[axis] // N_DEV] + shp[axis + 1:])
    return jnp.moveaxis(a, axis, 0)


def kernel(x, a_w_in, a_ln_g, a_ln_b, a_w_s, a_b_s, a_w_out, kv_w, b_w_q, b_w_o, rel_table, ffn_w_up, ffn_conv_w, ffn_conv_b, ffn_w_down, ln_g, ln_b, loss_target, m_a_w_in, m_a_ln_g, m_a_ln_b, m_a_w_s, m_a_b_s, m_a_w_out, m_kv_w, m_b_w_q, m_b_w_o, m_rel_table, m_ffn_w_up, m_ffn_conv_w, m_ffn_conv_b, m_ffn_w_down, m_ln_g, m_ln_b, v_a_w_in, v_a_ln_g, v_a_ln_b, v_a_w_s, v_a_b_s, v_a_w_out, v_kv_w, v_b_w_q, v_b_w_o, v_rel_table, v_ffn_w_up, v_ffn_conv_w, v_ffn_conv_b, v_ffn_w_down, v_ln_g, v_ln_b):
    names = ["a_w_in", "a_ln_g", "a_ln_b", "a_w_s", "a_b_s", "a_w_out", "kv_w", "b_w_q", "b_w_o", "rel_table",
             "ffn_w_up", "ffn_conv_w", "ffn_conv_b", "ffn_w_down", "ln_g", "ln_b"]
    w = dict(zip(names, (a_w_in, a_ln_g, a_ln_b, a_w_s, a_b_s, a_w_out, kv_w, b_w_q, b_w_o, rel_table,
                         ffn_w_up, ffn_conv_w, ffn_conv_b, ffn_w_down, ln_g, ln_b)))
    m = dict(zip(names, (m_a_w_in, m_a_ln_g, m_a_ln_b, m_a_w_s, m_a_b_s, m_a_w_out, m_kv_w, m_b_w_q, m_b_w_o,
                         m_rel_table, m_ffn_w_up, m_ffn_conv_w, m_ffn_conv_b, m_ffn_w_down, m_ln_g, m_ln_b)))
    v = dict(zip(names, (v_a_w_in, v_a_ln_g, v_a_ln_b, v_a_w_s, v_a_b_s, v_a_w_out, v_kv_w, v_b_w_q, v_b_w_o,
                         v_rel_table, v_ffn_w_up, v_ffn_conv_w, v_ffn_conv_b, v_ffn_w_down, v_ln_g, v_ln_b)))
    small_names = [n for n, _ in SMALL_SHARDED]
    small_shapes = [w[n].shape for n in small_names]
    rep_shapes = [w[n].shape for n in REPLICATED]

    axis_of = {key: axis for _, key, axis, _ in BIG}
    src = {}
    for n, key, axis, tr in BIG:
        loc = _as_mats(w[n], tr).astype(BF16)
        for l in range(loc.shape[0]):
            src[(key, l)] = loc[l]
    order = []
    for i in range(DEPTH):
        if i < N_A:
            order += [[("a_w_in", i)], [("a_w_out", i)]]
        else:
            order += [([("kv_w", 0)] if i == N_A else []) + [("b_w_q_t", i - N_A)], [("b_w_o", i - N_A)]]
        order += [[("ffn_w_up_t", i)], [("ffn_w_down", i)]]
    small_src = _pack_rows([w[n] for n in small_names])
    srows = small_src.shape[0]
    handles, _ = _send_start([[(small_src, 0)]] + [[(src[kl], axis_of[kl[0]]) for kl in grp] for grp in order],
                             True, name="gather_start")
    small_all = _send_wait(handles[0], True, x, name="gather_wait_small")[0]
    small_st = _unpack_rows(small_all.reshape(N_DEV, srows, LANES), small_shapes, lead=1)
    base = {n: w[n] for n in REPLICATED}
    for (n, ax), st in zip(SMALL_SHARDED, small_st):
        base[n] = _merge_shards(st, ax)
    for n, key, _, tr in BIG:
        base[key] = [None] * (1 if w[n].ndim == 2 else w[n].shape[0])

    def fetch(group, after):
        mats = _send_wait(handles[1 + group], True, after, name=f"gather_wait_{group}")
        return dict(zip(order[group], mats))

    sent = {}

    def emit(group, mats):
        keys = list(mats)
        hs, token = _send_start([[(mats[kl], axis_of[kl[0]]) for kl in keys]], False, name=f"exchange_start_{group}")
        sent[group] = (keys, hs[0])
        return token

    small_sent = []

    def emit_small(grads):
        small_pack = _pack_rows([_split_shards(grads[n], ax) for n, ax in SMALL_SHARDED], lead=1)
        rest = _pack_rows([grads[n] for n in REPLICATED[1:]])
        mine = jnp.concatenate([small_pack.reshape(N_DEV * srows, LANES), rest], axis=0)
        gating = grads[REPLICATED[0]].reshape(-1, LANES).astype(BF16)
        hs, token = _send_start([[(mine, 0), (gating, 0)]], True, name="small_grads_start")
        small_sent.append(hs[0])
        return token

    loss, grad_x, grads = _local_step(x[0], loss_target[0], _Weights(base, fetch, emit, emit_small))
    loss = lax.psum(loss, ("x", "y", "c"))
    out = {}

    landed = {}
    for group in sorted(sent, reverse=True):
        keys, h = sent[group]
        landed.update(zip(keys, _send_wait(h, False, grad_x, name=f"exchange_wait_{group}")))
    last = grad_x
    for n, key, axis, tr in BIG:
        shp = w[n].shape
        parts = [landed[(key, l)] for l in range(1 if len(shp) == 2 else shp[0])]
        res = _adamw(_as_mats(w[n], tr), _as_mats(m[n], tr), _as_mats(v[n], tr), parts, name=f"adamw_{n}")
        out[n] = [(jnp.swapaxes(r, 1, 2) if tr else r).reshape(shp) for r in res]
        last = res[0]

    allp, allg = _send_wait(small_sent[0], True, last, name="small_grads_wait")
    gsum = _sum_parts(allp.reshape(N_DEV, -1, LANES), name="sum_small_grads")
    gating = _sum_parts(allg.reshape(N_DEV, -1, LANES), name="sum_gating_grads")
    g_small = lax.dynamic_slice_in_dim(gsum, _my_index() * srows, srows, axis=0)
    pack_sr = lambda d: jnp.concatenate([_pack_rows([d[n] for n in small_names]),
                                         _pack_rows([d[n] for n in REPLICATED])], axis=0)
    n_rest = sum(int(np.prod(s)) for s in rep_shapes[1:]) // LANES
    gs_in = jnp.concatenate([g_small, gating, gsum[N_DEV * srows:N_DEV * srows + n_rest]], axis=0)
    gs_in = jnp.pad(gs_in, ((0, pack_sr(w).shape[0] - gs_in.shape[0]), (0, 0)))[None]
    res = _adamw(pack_sr(w)[None], pack_sr(m)[None], pack_sr(v)[None], [gs_in], name="adamw_small")
    for n, vals in zip(small_names, zip(*[_unpack_rows(r[0, :srows], small_shapes) for r in res])):
        out[n] = list(vals)
    for n, vals in zip(REPLICATED, zip(*[_unpack_rows(r[0, srows:], rep_shapes) for r in res])):
        out[n] = list(vals)

    return (loss, grad_x[None], *[out[n][0] for n in names], *[out[n][1] for n in names],
            *[out[n][2] for n in names], *[out[n][3] for n in names])
```

```python
import math

import numpy as np
import jax
import jax.numpy as jnp
from jax import lax
from jax.experimental import pallas as pl
from jax.experimental.pallas import tpu as pltpu

F32 = jnp.float32
BF16 = jnp.bfloat16
ACT = jnp.bfloat16
MESH = pl.DeviceIdType.MESH

N_DEV = 8
DEPTH = 4
N_A = 2
CHUNK = 128
BLK = 128
HEAD_DIM = 64
DILATED_GROUPS = ((128, 1), (512, 4), (2048, 16))
N_GROUPS = 3
REL_BUCKETS = 32
REL_MAX_DIST = 2048
ALPHA = (2 * DEPTH) ** 0.25
LN_EPS = 1e-5
NEG = -1e30
ADAM_LR = 0.001
ADAM_B1 = 0.9
ADAM_B2 = 0.999
ADAM_EPS = 1e-08
ADAM_WD = 0.01
ADAM_STEP = 10

LANES = 128
VMEM_LIMIT = 56 * 1024 * 1024
MM_TILE_CAP = 1408
INV_SQRT2 = 1.0 / math.sqrt(2.0)
INV_SQRT_2PI = 1.0 / math.sqrt(2.0 * math.pi)


def _pick(n, cap):
    best = None
    for t in range(LANES, min(n, cap) + 1, LANES):
        if n % t == 0:
            best = t
    return best if best is not None else n


def _params(sem):
    return pltpu.CompilerParams(dimension_semantics=sem, vmem_limit_bytes=VMEM_LIMIT)


def _ordered(body, in_specs, args, after):
    if after is None:
        return body, list(in_specs), tuple(args)
    return (lambda _, *refs: body(*refs)), [pl.BlockSpec(memory_space=pl.ANY), *in_specs], (after, *args)


def _gelu(x):
    return 0.5 * x * (1.0 + lax.erf(x * INV_SQRT2))


def _gelu_grad(x):
    return 0.5 * (1.0 + lax.erf(x * INV_SQRT2)) + x * jnp.exp(-0.5 * x * x) * INV_SQRT_2PI


def _mm(a, b, *, ta=False, tb=False, out_dtype=F32, scale=None, after=None, name):
    halves = isinstance(a, tuple)
    parts = 1 if halves or a.ndim == 2 else a.shape[0]
    ash = (a[0].shape[0], 2 * a[0].shape[1]) if halves else (a.shape if parts == 1
                                                               else (a.shape[1], parts * a.shape[2]))
    if ta:
        K, M = ash
    else:
        M, K = ash
    if tb:
        N, Kb = b.shape
    else:
        Kb, N = b.shape
    assert K == Kb, (ash, b.shape, ta, tb)
    tm, tn, tk = _pick(M, MM_TILE_CAP), _pick(N, MM_TILE_CAP), _pick(K, MM_TILE_CAP)
    split = 2 if halves else parts
    if split > 1 and ta:
        tm = _pick(M // split, MM_TILE_CAP)
    if split > 1 and not ta:
        tk = _pick(K // split, MM_TILE_CAP)
    nk = K // tk
    nh = (M // split // tm if ta else K // split // tk) if split > 1 else 0
    dn = (((0 if ta else 1,), (1 if tb else 0,)), ((), ()))

    def body(*refs):
        a_refs, (b_ref, o_ref, acc_ref) = refs[:-3], refs[-3:]
        k = pl.program_id(2)

        def accumulate(a_ref):
            part = lax.dot_general(a_ref[...].astype(BF16), b_ref[...].astype(BF16), dn,
                                   preferred_element_type=F32)

            @pl.when(k == 0)
            def _():
                acc_ref[...] = part

            @pl.when(k > 0)
            def _():
                acc_ref[...] += part

        if halves:
            first = (pl.program_id(0) if ta else k) < nh
            pl.when(first)(lambda: accumulate(a_refs[0]))
            pl.when(jnp.logical_not(first))(lambda: accumulate(a_refs[1]))
        else:
            accumulate(a_refs[0])

        @pl.when(k == nk - 1)
        def _():
            r = acc_ref[...]
            if scale is not None:
                r = r * scale
            o_ref[...] = r.astype(out_dtype)

    if halves and ta:
        a_specs = [pl.BlockSpec((tk, tm), lambda i, j, k: (jnp.where(i < nh, k, 0), jnp.minimum(i, nh - 1))),
                   pl.BlockSpec((tk, tm), lambda i, j, k: (jnp.where(i >= nh, k, 0), jnp.maximum(i - nh, 0)))]
    elif halves:
        a_specs = [pl.BlockSpec((tm, tk), lambda i, j, k: (i, jnp.minimum(k, nh - 1))),
                   pl.BlockSpec((tm, tk), lambda i, j, k: (i, jnp.maximum(k - nh, 0)))]
    elif parts > 1:
        a_specs = [pl.BlockSpec((None, tk, tm), lambda i, j, k: (i // nh, k, i % nh)) if ta
                   else pl.BlockSpec((None, tm, tk), lambda i, j, k: (k // nh, i, k % nh))]
    else:
        a_specs = [pl.BlockSpec((tk, tm), lambda i, j, k: (k, i)) if ta
                   else pl.BlockSpec((tm, tk), lambda i, j, k: (i, k))]
    b_spec = (pl.BlockSpec((tn, tk), lambda i, j, k: (j, k)) if tb
              else pl.BlockSpec((tk, tn), lambda i, j, k: (k, j)))
    body, in_specs, args = _ordered(body, [*a_specs, b_spec], (*(a if halves else (a,)), b), after)
    return pl.pallas_call(
        body, grid=(M // tm, N // tn, nk), in_specs=in_specs,
        out_specs=pl.BlockSpec((tm, tn), lambda i, j, k: (i, j)),
        out_shape=jax.ShapeDtypeStruct((M, N), out_dtype),
        scratch_shapes=[pltpu.VMEM((tm, tn), F32)],
        compiler_params=_params(("parallel", "parallel", "arbitrary")), name=name,
    )(*args)


def _add_ln_fwd(x, h, g, b, *, name):
    T, D = x.shape
    rb = _pick(T, 512)

    def body(x_ref, h_ref, g_ref, b_ref, o_ref, ob_ref):
        pre = ALPHA * x_ref[...] + h_ref[...].astype(F32)
        mu = jnp.mean(pre, axis=1, keepdims=True)
        cen = pre - mu
        var = jnp.mean(cen * cen, axis=1, keepdims=True)
        y = cen * lax.rsqrt(var + LN_EPS) * g_ref[...] + b_ref[...]
        o_ref[...] = y
        ob_ref[...] = y.astype(BF16)

    row = pl.BlockSpec((rb, D), lambda i: (i, 0))
    vec = pl.BlockSpec((1, D), lambda i: (0, 0))
    return pl.pallas_call(
        body, grid=(T // rb,), in_specs=[row, row, vec, vec], out_specs=[row, row],
        out_shape=[jax.ShapeDtypeStruct((T, D), F32), jax.ShapeDtypeStruct((T, D), BF16)],
        compiler_params=_params(("parallel",)), name=name,
    )(x, h, g.reshape(1, D), b.reshape(1, D))


def _add_ln_bwd(x, h, g, terms, *, after=None, name):
    T, D = x.shape
    rb = _pick(T, 512)
    coefs = [c for c, _ in terms]
    nt = len(terms)

    def body(*refs):
        x_ref, h_ref, g_ref = refs[:3]
        t_refs = refs[3:3 + nt]
        dp_ref, dpb_ref, dg_ref, db_ref = refs[3 + nt:]
        dy = None
        for c, r in zip(coefs, t_refs):
            v = r[...].astype(F32) if c == 1.0 else c * r[...].astype(F32)
            dy = v if dy is None else dy + v
        pre = ALPHA * x_ref[...] + h_ref[...].astype(F32)
        mu = jnp.mean(pre, axis=1, keepdims=True)
        cen = pre - mu
        var = jnp.mean(cen * cen, axis=1, keepdims=True)
        rstd = lax.rsqrt(var + LN_EPS)
        xhat = cen * rstd
        dxh = dy * g_ref[...]
        m1 = jnp.mean(dxh, axis=1, keepdims=True)
        m2 = jnp.mean(dxh * xhat, axis=1, keepdims=True)
        dpre = rstd * (dxh - m1 - xhat * m2)
        dp_ref[...] = dpre
        dpb_ref[...] = dpre.astype(BF16)
        dg = jnp.sum(dy * xhat, axis=0, keepdims=True)
        db = jnp.sum(dy, axis=0, keepdims=True)

        @pl.when(pl.program_id(0) == 0)
        def _():
            dg_ref[...] = dg
            db_ref[...] = db

        @pl.when(pl.program_id(0) > 0)
        def _():
            dg_ref[...] += dg
            db_ref[...] += db

    row = pl.BlockSpec((rb, D), lambda i: (i, 0))
    vec = pl.BlockSpec((1, D), lambda i: (0, 0))
    body, in_specs, args = _ordered(body, [row, row, vec] + [row] * nt,
                                    (x, h, g.reshape(1, D), *[a for _, a in terms]), after)
    return pl.pallas_call(
        body, grid=(T // rb,), in_specs=in_specs,
        out_specs=[row, row, vec, vec],
        out_shape=[jax.ShapeDtypeStruct((T, D), F32), jax.ShapeDtypeStruct((T, D), BF16),
                   jax.ShapeDtypeStruct((1, D), F32), jax.ShapeDtypeStruct((1, D), F32)],
        compiler_params=_params(("arbitrary",)), name=name,
    )(*args)


def _lincomb(terms, out_dtype, *, name):
    R, C = terms[0][1].shape
    rb = _pick(R, 512)
    coefs = [c for c, _ in terms]
    nt = len(terms)

    def body(*refs):
        acc = None
        for c, r in zip(coefs, refs[:nt]):
            v = r[...].astype(F32)
            v = v if c == 1.0 else c * v
            acc = v if acc is None else acc + v
        refs[nt][...] = acc.astype(out_dtype)

    row = pl.BlockSpec((rb, C), lambda i: (i, 0))
    return pl.pallas_call(
        body, grid=(R // rb,), in_specs=[row] * nt, out_specs=row,
        out_shape=jax.ShapeDtypeStruct((R, C), out_dtype),
        compiler_params=_params(("parallel",)), name=name,
    )(*[a for _, a in terms])


def _loss_grad(y, tgt, *, name):
    T, D = y.shape
    rb = _pick(T, 512)

    def body(y_ref, t_ref, dy_ref, l_ref):
        err = y_ref[...] - t_ref[...]
        dy_ref[...] = err * (1.0 / D)
        part = jnp.sum(jnp.sum(err * err, axis=1, keepdims=True), axis=0, keepdims=True) * (0.5 / D)
        part = jnp.broadcast_to(part, (1, LANES))

        @pl.when(pl.program_id(0) == 0)
        def _():
            l_ref[...] = part

        @pl.when(pl.program_id(0) > 0)
        def _():
            l_ref[...] += part

    row = pl.BlockSpec((rb, D), lambda i: (i, 0))
    return pl.pallas_call(
        body, grid=(T // rb,), in_specs=[row, row],
        out_specs=[row, pl.BlockSpec((1, LANES), lambda i: (0, 0))],
        out_shape=[jax.ShapeDtypeStruct((T, D), F32), jax.ShapeDtypeStruct((1, LANES), F32)],
        compiler_params=_params(("arbitrary",)), name=name,
    )(y, tgt)


def _sgu_fwd(zp, ln_g, ln_b, ws, bst, *, name):
    T, E2 = zp.shape
    E = E2 // 2
    G = ws.shape[0]
    cg = E // G
    rb = 2 * CHUNK

    def body(z_ref, g_ref, b_ref, ws_ref, bs_ref, y_ref):
        u = _gelu(z_ref[:, :E].astype(F32))
        v = _gelu(z_ref[:, E:].astype(F32))
        mu = jnp.mean(v, axis=1, keepdims=True)
        cen = v - mu
        var = jnp.mean(cen * cen, axis=1, keepdims=True)
        vn = (cen * lax.rsqrt(var + LN_EPS) * g_ref[...] + b_ref[...]).astype(BF16)
        for ci in range(rb // CHUNK):
            rows = slice(ci * CHUNK, (ci + 1) * CHUNK)
            for gi in range(G):
                cols = slice(gi * cg, (gi + 1) * cg)
                sv = jnp.dot(ws_ref[gi], vn[rows, cols], preferred_element_type=F32)
                sv = sv + bs_ref[:, gi:gi + 1]
                y_ref[rows, cols] = (u[rows, cols] * sv).astype(BF16)

    return pl.pallas_call(
        body, grid=(T // rb,),
        in_specs=[pl.BlockSpec((rb, E2), lambda i: (i, 0)),
                  pl.BlockSpec((1, E), lambda i: (0, 0)), pl.BlockSpec((1, E), lambda i: (0, 0)),
                  pl.BlockSpec((G, CHUNK, CHUNK), lambda i: (0, 0, 0)),
                  pl.BlockSpec((CHUNK, G), lambda i: (0, 0))],
        out_specs=pl.BlockSpec((rb, E), lambda i: (i, 0)),
        out_shape=jax.ShapeDtypeStruct((T, E), BF16),
        compiler_params=_params(("parallel",)), name=name,
    )(zp, ln_g.reshape(1, E), ln_b.reshape(1, E), ws, bst)


def _sgu_bwd(zp, dy, ln_g, ln_b, ws, wst, bst, *, after=None, name):
    T, E2 = zp.shape
    E = E2 // 2
    G = ws.shape[0]
    cg = E // G
    rb = CHUNK
    nsteps = T // rb

    def body(z_ref, dy_ref, g_ref, b_ref, ws_ref, wst_ref, bs_ref,
             dz_ref, dg_ref, db_ref, dws_ref, dbs_ref, dsv_acc):
        step = pl.program_id(0)

        @pl.when(step == 0)
        def _():
            dg_ref[...] = jnp.zeros_like(dg_ref)
            db_ref[...] = jnp.zeros_like(db_ref)
            dws_ref[...] = jnp.zeros_like(dws_ref)
            dsv_acc[...] = jnp.zeros_like(dsv_acc)

        zu = z_ref[:, :E].astype(F32)
        zv = z_ref[:, E:].astype(F32)
        u = _gelu(zu)
        v = _gelu(zv)
        mu = jnp.mean(v, axis=1, keepdims=True)
        cen = v - mu
        var = jnp.mean(cen * cen, axis=1, keepdims=True)
        rstd = lax.rsqrt(var + LN_EPS)
        xhat = cen * rstd
        vn = (xhat * g_ref[...] + b_ref[...]).astype(BF16)
        dyv = dy_ref[...].astype(F32)
        dsv = dyv * u
        dsv_acc[...] += dsv
        dsvb = dsv.astype(BF16)
        tril = (lax.broadcasted_iota(jnp.int32, (CHUNK, CHUNK), 0)
                >= lax.broadcasted_iota(jnp.int32, (CHUNK, CHUNK), 1))
        du_parts = []
        dvn_parts = []
        for gi in range(G):
            cols = slice(gi * cg, (gi + 1) * cg)
            sv = jnp.dot(ws_ref[gi], vn[:, cols], preferred_element_type=F32) + bs_ref[:, gi:gi + 1]
            du_parts.append(dyv[:, cols] * sv)
            dvn_parts.append(jnp.dot(wst_ref[gi], dsvb[:, cols], preferred_element_type=F32))
            dw = lax.dot_general(dsvb[:, cols], vn[:, cols], (((1,), (1,)), ((), ())),
                                 preferred_element_type=F32)
            dws_ref[gi] += jnp.where(tril, dw, 0.0)
        du = jnp.concatenate(du_parts, axis=1)
        dvn = jnp.concatenate(dvn_parts, axis=1)
        dg_ref[...] += jnp.sum(dvn * xhat, axis=0, keepdims=True)
        db_ref[...] += jnp.sum(dvn, axis=0, keepdims=True)
        dxh = dvn * g_ref[...]
        m1 = jnp.mean(dxh, axis=1, keepdims=True)
        m2 = jnp.mean(dxh * xhat, axis=1, keepdims=True)
        dv = rstd * (dxh - m1 - xhat * m2)
        dz_ref[:, :E] = (du * _gelu_grad(zu)).astype(BF16)
        dz_ref[:, E:] = (dv * _gelu_grad(zv)).astype(BF16)

        @pl.when(step == nsteps - 1)
        def _():
            lane = lax.broadcasted_iota(jnp.int32, (CHUNK, LANES), 1)
            out = jnp.zeros((CHUNK, LANES), F32)
            for gi in range(G):
                s = jnp.sum(dsv_acc[:, gi * cg:(gi + 1) * cg], axis=1, keepdims=True)
                out = jnp.where(lane == gi, s, out)
            dbs_ref[...] = out

    vecE = pl.BlockSpec((1, E), lambda i: (0, 0))
    wspec = pl.BlockSpec((G, CHUNK, CHUNK), lambda i: (0, 0, 0))
    body, in_specs, args = _ordered(
        body, [pl.BlockSpec((rb, E2), lambda i: (i, 0)), pl.BlockSpec((rb, E), lambda i: (i, 0)),
               vecE, vecE, wspec, wspec, pl.BlockSpec((CHUNK, G), lambda i: (0, 0))],
        (zp, dy, ln_g.reshape(1, E), ln_b.reshape(1, E), ws, wst, bst), after)
    return pl.pallas_call(
        body, grid=(nsteps,), in_specs=in_specs,
        out_specs=[pl.BlockSpec((rb, E2), lambda i: (i, 0)), vecE, vecE, wspec,
                   pl.BlockSpec((CHUNK, LANES), lambda i: (0, 0))],
        out_shape=[jax.ShapeDtypeStruct((T, E2), BF16), jax.ShapeDtypeStruct((1, E), F32),
                   jax.ShapeDtypeStruct((1, E), F32), jax.ShapeDtypeStruct((G, CHUNK, CHUNK), F32),
                   jax.ShapeDtypeStruct((CHUNK, LANES), F32)],
        scratch_shapes=[pltpu.VMEM((CHUNK, E), F32)],
        compiler_params=_params(("arbitrary",)), name=name,
    )(*args)


def _shift_down(x, k, row):
    return jnp.where(row >= k, pltpu.roll(x, k, 0), 0.0)


def _shift_up(x, k, row, T):
    return jnp.where(row < T - k, pltpu.roll(x, T - k, 0), 0.0)


def _conv3(x, w_ref, b_ref, row):
    return (w_ref[0:1, :] * _shift_down(x, 2, row) + w_ref[1:2, :] * _shift_down(x, 1, row)
            + w_ref[2:3, :] * x + b_ref[...])


def _convgate_fwd(hh, cw, cb, *, name):
    T, F2 = hh.shape
    F = F2 // 2
    ns = F // LANES

    def body(a_ref, g_ref, wa_ref, wg_ref, ba_ref, bg_ref, o_ref, ca_ref, cg_ref):
        row = lax.broadcasted_iota(jnp.int32, (T, LANES), 0)
        ca = _conv3(a_ref[...].astype(F32), wa_ref, ba_ref, row)
        cgv = _conv3(g_ref[...].astype(F32), wg_ref, bg_ref, row)
        o_ref[...] = (_gelu(ca) * cgv).astype(BF16)
        ca_ref[...] = ca.astype(ACT)
        cg_ref[...] = cgv.astype(ACT)

    sa = lambda r: pl.BlockSpec((r, LANES), lambda j: (0, j))
    sg = lambda r: pl.BlockSpec((r, LANES), lambda j: (0, j + ns))
    return pl.pallas_call(
        body, grid=(ns,), in_specs=[sa(T), sg(T), sa(3), sg(3), sa(1), sg(1)],
        out_specs=[sa(T)] * 3,
        out_shape=[jax.ShapeDtypeStruct((T, F), BF16), jax.ShapeDtypeStruct((T, F), ACT),
                   jax.ShapeDtypeStruct((T, F), ACT)],
        compiler_params=_params(("parallel",)), name=name,
    )(hh, hh, cw, cw, cb, cb)


def _convgate_bwd(hh, hca, hcg, dact, cw, *, name):
    T, F2 = hh.shape
    F = F2 // 2
    ns = F // LANES

    def body(a_ref, g_ref, ca_ref, cg_ref, d_ref, wa_ref, wg_ref,
             da_ref, dg_ref, dwa_ref, dwg_ref, dba_ref, dbg_ref):
        row = lax.broadcasted_iota(jnp.int32, (T, LANES), 0)
        d = d_ref[...].astype(F32)
        ca = ca_ref[...].astype(F32)
        cgv = cg_ref[...].astype(F32)
        cdf = 0.5 * (1.0 + lax.erf(ca * INV_SQRT2))
        dca = d * cgv * (cdf + ca * jnp.exp(-0.5 * ca * ca) * INV_SQRT_2PI)
        dcg = d * (ca * cdf)
        for x_ref, w_ref, dc, dx_ref, dw_ref, db_ref in (
                (a_ref, wa_ref, dca, da_ref, dwa_ref, dba_ref),
                (g_ref, wg_ref, dcg, dg_ref, dwg_ref, dbg_ref)):
            x = x_ref[...].astype(F32)
            up1, up2 = _shift_up(dc, 1, row, T), _shift_up(dc, 2, row, T)
            dx_ref[...] = (w_ref[2:3, :] * dc + w_ref[1:2, :] * up1 + w_ref[0:1, :] * up2).astype(BF16)
            dw_ref[0:1, :] = jnp.sum(up2 * x, axis=0, keepdims=True)
            dw_ref[1:2, :] = jnp.sum(up1 * x, axis=0, keepdims=True)
            dw_ref[2:3, :] = jnp.sum(dc * x, axis=0, keepdims=True)
            db_ref[...] = jnp.sum(dc, axis=0, keepdims=True)

    sa = lambda r: pl.BlockSpec((r, LANES), lambda j: (0, j))
    sg = lambda r: pl.BlockSpec((r, LANES), lambda j: (0, j + ns))
    return pl.pallas_call(
        body, grid=(ns,), in_specs=[sa(T), sg(T), sa(T), sa(T), sa(T), sa(3), sg(3)],
        out_specs=[sa(T), sa(T), sa(3), sa(3), sa(1), sa(1)],
        out_shape=[jax.ShapeDtypeStruct((T, F), BF16), jax.ShapeDtypeStruct((T, F), BF16),
                   jax.ShapeDtypeStruct((3, F), F32), jax.ShapeDtypeStruct((3, F), F32),
                   jax.ShapeDtypeStruct((1, F), F32), jax.ShapeDtypeStruct((1, F), F32)],
        compiler_params=_params(("parallel",)), name=name,
    )(hh, hh, hca, hcg, dact, cw, cw)


def _bucket_maps():
    iq = np.arange(BLK)[:, None]
    ik = np.arange(2 * BLK)[None, :]
    delta = iq + BLK - ik
    maps = []
    for win, dil in DILATED_GROUPS:
        n = np.clip(delta, 0, None) * dil
        max_exact = REL_BUCKETS // 2
        nf = np.maximum(n, 1).astype(np.float32)
        large = max_exact + (np.log(nf / np.float32(max_exact)) / np.float32(math.log(REL_MAX_DIST / max_exact))
                             * np.float32(REL_BUCKETS - max_exact)).astype(np.int32)
        large = np.minimum(large, REL_BUCKETS - 1)
        bucket = np.where(n < max_exact, n, large)
        valid = (delta >= 0) & (delta <= win // dil)
        maps.append(np.where(valid, bucket, -1).astype(np.int32))
    return np.stack(maps)


def _band_bias(rel_table, bmap, H, *, name):
    def body(t_ref, m_ref, o_ref):
        g = pl.program_id(0)
        bm = m_ref[0]
        for h in range(H):
            acc = jnp.full((BLK, 2 * BLK), NEG, F32)
            for b in range(REL_BUCKETS):
                acc = jnp.where(bm == b, t_ref[b, g * H + h], acc)
            o_ref[0, h] = acc

    return pl.pallas_call(
        body, grid=(N_GROUPS,),
        in_specs=[pl.BlockSpec(memory_space=pltpu.SMEM),
                  pl.BlockSpec((1, BLK, 2 * BLK), lambda g: (g, 0, 0))],
        out_specs=pl.BlockSpec((1, H, BLK, 2 * BLK), lambda g: (g, 0, 0, 0)),
        out_shape=jax.ShapeDtypeStruct((N_GROUPS, H, BLK, 2 * BLK), F32),
        compiler_params=_params(("parallel",)), name=name,
    )(rel_table, bmap)


def _band_bias_bwd(dbias, bmap, H, *, name):
    def body(d_ref, m_ref, o_ref):
        bm = m_ref[0]
        rowi = lax.broadcasted_iota(jnp.int32, (REL_BUCKETS, LANES), 0)
        lane = lax.broadcasted_iota(jnp.int32, (REL_BUCKETS, LANES), 1)
        out = jnp.zeros((REL_BUCKETS, LANES), F32)
        for h in range(H):
            dv = d_ref[0, h]
            for b in range(REL_BUCKETS):
                s = jnp.sum(jnp.sum(jnp.where(bm == b, dv, 0.0), axis=1, keepdims=True),
                            axis=0, keepdims=True)
                out = jnp.where((rowi == b) & (lane == h), s, out)
        o_ref[0] = out

    return pl.pallas_call(
        body, grid=(N_GROUPS,),
        in_specs=[pl.BlockSpec((1, H, BLK, 2 * BLK), lambda g: (g, 0, 0, 0)),
                  pl.BlockSpec((1, BLK, 2 * BLK), lambda g: (g, 0, 0))],
        out_specs=pl.BlockSpec((1, REL_BUCKETS, LANES), lambda g: (g, 0, 0)),
        out_shape=jax.ShapeDtypeStruct((N_GROUPS, REL_BUCKETS, LANES), F32),
        compiler_params=_params(("parallel",)), name=name,
    )(dbias, bmap)


def _head_masks():
    lane = lax.broadcasted_iota(jnp.int32, (BLK, LANES), 1)
    return (lane < HEAD_DIM, lane >= HEAD_DIM)


def _attn_fwd(q, kv, bias, gi, *, name):
    T = q.shape[0]
    HD = kv.shape[1] // 2
    d = DILATED_GROUPS[gi][1]
    S = T // d
    NB = S // BLK
    H = HD // HEAD_DIM
    qv, qcol = (q, gi) if d == 1 else (q[:, gi * HD:(gi + 1) * HD].reshape(S, d * HD), 0)
    kvv = kv.reshape(S, d * 2 * HD)

    def body(q_ref, kp_ref, kc_ref, vp_ref, vc_ref, b_ref, o_ref, l_ref):
        n = pl.program_id(1)
        col = lax.broadcasted_iota(jnp.int32, (BLK, 2 * BLK), 1)
        first = (n == 0) & (col < BLK)
        hm = _head_masks()
        for p in range(HD // LANES):
            sl = slice(p * LANES, (p + 1) * LANES)
            qp = q_ref[:, sl]
            kc = jnp.concatenate([kp_ref[:, sl], kc_ref[:, sl]], axis=0)
            vc = jnp.concatenate([vp_ref[:, sl], vc_ref[:, sl]], axis=0)
            outs = []
            lses = []
            for hh in range(2):
                qm = jnp.where(hm[hh], qp, jnp.zeros_like(qp))
                s = lax.dot_general(qm, kc, (((1,), (1,)), ((), ())), preferred_element_type=F32)
                s = jnp.where(first, NEG, s + b_ref[2 * p + hh])
                m = jnp.max(s, axis=1, keepdims=True)
                e = jnp.exp(s - m)
                den = jnp.sum(e, axis=1, keepdims=True)
                outs.append(jnp.dot((e / den).astype(BF16), vc, preferred_element_type=F32))
                lses.append(m + jnp.log(den))
            o_ref[:, sl] = jnp.where(hm[0], outs[0], outs[1])
            l_ref[:, sl] = jnp.where(hm[0], lses[0], lses[1])

    blk = lambda f: pl.BlockSpec((BLK, HD), f)
    prev = lambda n: jnp.maximum(n - 1, 0)
    return pl.pallas_call(
        body, grid=(d, NB),
        in_specs=[blk(lambda r, n: (n, r + qcol)),
                  blk(lambda r, n: (prev(n), r * 2)), blk(lambda r, n: (n, r * 2)),
                  blk(lambda r, n: (prev(n), r * 2 + 1)), blk(lambda r, n: (n, r * 2 + 1)),
                  pl.BlockSpec((H, BLK, 2 * BLK), lambda r, n: (0, 0, 0))],
        out_specs=[blk(lambda r, n: (n, r)), blk(lambda r, n: (n, r))],
        out_shape=[jax.ShapeDtypeStruct((S, d * HD), F32), jax.ShapeDtypeStruct((S, d * HD), F32)],
        compiler_params=_params(("parallel", "parallel")), name=name,
    )(qv, kvv, kvv, kvv, kvv, bias)


def _attn_combine(os, ls, *, name):
    T, HD = os[0].shape
    rb = _pick(T, 512)

    def body(o0, o1, o2, l0, l1, l2, o_ref, ob_ref, l_ref):
        la, lb, lc = l0[...], l1[...], l2[...]
        m = jnp.maximum(jnp.maximum(la, lb), lc)
        L = m + jnp.log(jnp.exp(la - m) + jnp.exp(lb - m) + jnp.exp(lc - m))
        o = jnp.exp(la - L) * o0[...] + jnp.exp(lb - L) * o1[...] + jnp.exp(lc - L) * o2[...]
        o_ref[...] = o
        ob_ref[...] = o.astype(BF16)
        l_ref[...] = L

    row = pl.BlockSpec((rb, HD), lambda i: (i, 0))
    return pl.pallas_call(
        body, grid=(T // rb,), in_specs=[row] * 6, out_specs=[row] * 3,
        out_shape=[jax.ShapeDtypeStruct((T, HD), F32), jax.ShapeDtypeStruct((T, HD), BF16),
                   jax.ShapeDtypeStruct((T, HD), F32)],
        compiler_params=_params(("parallel",)), name=name,
    )(*[a.reshape(T, HD) for a in os], *[a.reshape(T, HD) for a in ls])


def _attn_bwd(q, kv, bias, do, o, L, gi, *, name):
    T = q.shape[0]
    HD = kv.shape[1] // 2
    d = DILATED_GROUPS[gi][1]
    S = T // d
    NB = S // BLK
    H = HD // HEAD_DIM
    qv, qcol = (q, gi) if d == 1 else (q[:, gi * HD:(gi + 1) * HD].reshape(S, d * HD), 0)
    kvv = kv.reshape(S, d * 2 * HD)
    dov, ov, Lv = (a.reshape(S, d * HD) for a in (do, o, L))

    def body(q_ref, kp_ref, kc_ref, vp_ref, vc_ref, b_ref, do_ref, o_ref, L_ref,
             dq_ref, dk_ref, dv_ref, db_ref, ck_ref, cv_ref):
        r = pl.program_id(0)
        n = pl.program_id(1)

        @pl.when((r == 0) & (n == 0))
        def _():
            db_ref[...] = jnp.zeros_like(db_ref)

        @pl.when(n == 0)
        def _():
            ck_ref[...] = jnp.zeros_like(ck_ref)
            cv_ref[...] = jnp.zeros_like(cv_ref)

        @pl.when(n < NB)
        def _():
            col = lax.broadcasted_iota(jnp.int32, (BLK, 2 * BLK), 1)
            first = (n == 0) & (col < BLK)
            hm = _head_masks()
            for p in range(HD // LANES):
                sl = slice(p * LANES, (p + 1) * LANES)
                qp = q_ref[:, sl]
                kc = jnp.concatenate([kp_ref[:, sl], kc_ref[:, sl]], axis=0)
                vc = jnp.concatenate([vp_ref[:, sl], vc_ref[:, sl]], axis=0)
                dop = do_ref[:, sl]
                dob = dop.astype(BF16)
                prod = dop * o_ref[:, sl]
                Lp = L_ref[:, sl]
                dq_parts = []
                dkc = None
                dvc = None
                for hh in range(2):
                    qm = jnp.where(hm[hh], qp, jnp.zeros_like(qp))
                    dom = jnp.where(hm[hh], dob, jnp.zeros_like(dob))
                    s = lax.dot_general(qm, kc, (((1,), (1,)), ((), ())), preferred_element_type=F32)
                    s = jnp.where(first, NEG, s + b_ref[2 * p + hh])
                    lse = Lp[:, hh * HEAD_DIM:hh * HEAD_DIM + 1]
                    pr = jnp.exp(s - lse)
                    dp = lax.dot_general(dom, vc, (((1,), (1,)), ((), ())), preferred_element_type=F32)
                    delta = jnp.sum(jnp.where(hm[hh], prod, 0.0), axis=1, keepdims=True)
                    ds = pr * (dp - delta)
                    db_ref[2 * p + hh] += ds
                    dsb = ds.astype(BF16)
                    dq_parts.append(jnp.dot(dsb, kc, preferred_element_type=F32))
                    dkh = lax.dot_general(dsb, qm, (((0,), (0,)), ((), ())), preferred_element_type=F32)
                    dvh = lax.dot_general(pr.astype(BF16), dom, (((0,), (0,)), ((), ())),
                                          preferred_element_type=F32)
                    dkc = dkh if dkc is None else dkc + dkh
                    dvc = dvh if dvc is None else dvc + dvh
                dq = jnp.where(hm[0], dq_parts[0], dq_parts[1])
                dq_ref[:, sl] = (dq * (HEAD_DIM ** -0.5)).astype(BF16)
                dk_ref[:, sl] = ck_ref[:, sl] + dkc[:BLK]
                dv_ref[:, sl] = cv_ref[:, sl] + dvc[:BLK]
                ck_ref[:, sl] = dkc[BLK:]
                cv_ref[:, sl] = dvc[BLK:]

        @pl.when(n == NB)
        def _():
            dk_ref[...] = ck_ref[...]
            dv_ref[...] = cv_ref[...]

    blk = lambda f: pl.BlockSpec((BLK, HD), f)
    cur = lambda n: jnp.minimum(n, NB - 1)
    prev = lambda n: jnp.maximum(jnp.minimum(n, NB - 1) - 1, 0)
    lag = lambda n: jnp.maximum(n - 1, 0)
    return pl.pallas_call(
        body, grid=(d, NB + 1),
        in_specs=[blk(lambda r, n: (cur(n), r + qcol)),
                  blk(lambda r, n: (prev(n), r * 2)), blk(lambda r, n: (cur(n), r * 2)),
                  blk(lambda r, n: (prev(n), r * 2 + 1)), blk(lambda r, n: (cur(n), r * 2 + 1)),
                  pl.BlockSpec((H, BLK, 2 * BLK), lambda r, n: (0, 0, 0)),
                  blk(lambda r, n: (cur(n), r)), blk(lambda r, n: (cur(n), r)),
                  blk(lambda r, n: (cur(n), r))],
        out_specs=[blk(lambda r, n: (cur(n), r)), blk(lambda r, n: (lag(n), r)),
                   blk(lambda r, n: (lag(n), r)),
                   pl.BlockSpec((H, BLK, 2 * BLK), lambda r, n: (0, 0, 0))],
        out_shape=[jax.ShapeDtypeStruct((S, d * HD), BF16), jax.ShapeDtypeStruct((S, d * HD), F32),
                   jax.ShapeDtypeStruct((S, d * HD), F32),
                   jax.ShapeDtypeStruct((H, BLK, 2 * BLK), F32)],
        scratch_shapes=[pltpu.VMEM((BLK, HD), F32), pltpu.VMEM((BLK, HD), F32)],
        compiler_params=_params(("arbitrary", "arbitrary")), name=name,
    )(qv, kvv, kvv, kvv, kvv, bias, dov, ov, Lv)


SUPER = DILATED_GROUPS[-1][1] * BLK


def _band_rows(it, d):
    r, j = it % d, it // d
    if d == 1:
        at = lambda blk: pl.ds(pl.multiple_of(blk * BLK, BLK), BLK)
    else:
        at = lambda blk: pl.ds(r + d * BLK * blk, BLK, stride=d)
    return at(j), at(jnp.maximum(j - 1, 0))


def _stack_heads(x, hm):
    zero = jnp.zeros_like(x)
    return jnp.concatenate([jnp.where(hm[0], x, zero), jnp.where(hm[1], x, zero)], axis=0)


def _band_loops(step, d, unroll):
    n_it = SUPER // BLK

    def run(lo, hi, inside):
        if hi > lo:
            def body(it, carry):
                step(it, inside)
                return carry
            lax.fori_loop(lo, hi, body, 0, unroll=max(u for u in range(1, unroll + 1) if (hi - lo) % u == 0))

    run(0, d, False)
    run(d, n_it, True)


def _last_rows(it, d):
    m = SUPER // (d * BLK)
    if d == 1:
        return pl.ds((m - 1) * BLK, BLK)
    return pl.ds(it % d + d * BLK * (m - 1), BLK, stride=d)


def _attn_fwd_all(q, kv, bias, *, name):
    T = q.shape[0]
    HD = kv.shape[1] // 2
    PP = HD // LANES
    NS = T // SUPER

    def body(q0, q1, q2, kp_ref, kc_ref, vp_ref, vc_ref, b_ref, o_ref, ob_ref, l_ref, og, lg):
        n = pl.program_id(1)
        col = lax.broadcasted_iota(jnp.int32, (2 * BLK, 2 * BLK), 1)
        hm = _head_masks()
        for g, (q_ref, (_, d)) in enumerate(zip((q0, q1, q2), DILATED_GROUPS)):
            def step(it, inside, g=g, q_ref=q_ref, d=d):
                cur, prv = _band_rows(it, d)
                qp = q_ref[cur, :].astype(BF16)
                if inside:
                    kprev, vprev = kc_ref[prv, :], vc_ref[prv, :]
                else:
                    last = _last_rows(it, d)
                    kprev, vprev = kp_ref[last, :], vp_ref[last, :]
                kc = jnp.concatenate([kprev.astype(BF16), kc_ref[cur, :].astype(BF16)], axis=0)
                vc = jnp.concatenate([vprev.astype(BF16), vc_ref[cur, :].astype(BF16)], axis=0)
                s = lax.dot_general(_stack_heads(qp, hm), kc, (((1,), (1,)), ((), ())),
                                    preferred_element_type=F32)
                s = s + b_ref[g].reshape(2 * BLK, 2 * BLK)
                if not inside:
                    s = jnp.where((n == 0) & (col < BLK), NEG, s)
                mx = jnp.max(s, axis=1, keepdims=True)
                e = jnp.exp(s - mx)
                den = jnp.sum(e, axis=1, keepdims=True)
                out = jnp.dot((e / den).astype(BF16), vc, preferred_element_type=F32)
                lse = mx + jnp.log(den)
                og.at[g][cur, :] = jnp.where(hm[0], out[:BLK], out[BLK:])
                lg.at[g][cur, :] = jnp.where(hm[0], lse[:BLK], lse[BLK:])

            _band_loops(step, d, 8)
        la, lb, lc = lg[0], lg[1], lg[2]
        mx = jnp.maximum(jnp.maximum(la, lb), lc)
        L = mx + jnp.log(jnp.exp(la - mx) + jnp.exp(lb - mx) + jnp.exp(lc - mx))
        o = jnp.exp(la - L) * og[0] + jnp.exp(lb - L) * og[1] + jnp.exp(lc - L) * og[2]
        o_ref[...] = o
        ob_ref[...] = o.astype(BF16)
        l_ref[...] = L

    blk = lambda f: pl.BlockSpec((SUPER, LANES), f)
    prev = lambda n: jnp.maximum(n - 1, 0)
    qspec = lambda g: blk(lambda p, n: (n, g * PP + p))
    return pl.pallas_call(
        body, grid=(PP, NS),
        in_specs=[qspec(0), qspec(1), qspec(2),
                  blk(lambda p, n: (prev(n), p)), blk(lambda p, n: (n, p)),
                  blk(lambda p, n: (prev(n), PP + p)), blk(lambda p, n: (n, PP + p)),
                  pl.BlockSpec((N_GROUPS, 2, BLK, 2 * BLK), lambda p, n: (0, p, 0, 0))],
        out_specs=[blk(lambda p, n: (n, p))] * 3,
        out_shape=[jax.ShapeDtypeStruct((T, HD), F32), jax.ShapeDtypeStruct((T, HD), BF16),
                   jax.ShapeDtypeStruct((T, HD), F32)],
        scratch_shapes=[pltpu.VMEM((N_GROUPS, SUPER, LANES), F32), pltpu.VMEM((N_GROUPS, SUPER, LANES), F32)],
        compiler_params=_params(("parallel", "parallel")), name=name,
    )(q, q, q, kv, kv, kv, kv, bias)


def _attn_bwd_all(q, kv, bias, do, o, L, *, after=None, name):
    T = q.shape[0]
    HD = kv.shape[1] // 2
    PP = HD // LANES
    H = HD // HEAD_DIM
    NS = T // SUPER

    def body(q0, q1, q2, kp_ref, kc_ref, vp_ref, vc_ref, b_ref, do_ref, o_ref, L_ref,
             dq_ref, dk_ref, dv_ref, db_ref, ck_ref, cv_ref):
        n = pl.program_id(1)

        @pl.when(n == 0)
        def _():
            db_ref[...] = jnp.zeros_like(db_ref)
            ck_ref[...] = jnp.zeros_like(ck_ref)
            cv_ref[...] = jnp.zeros_like(cv_ref)

        dk_ref[...] = ck_ref[...]
        dv_ref[...] = cv_ref[...]
        ck_ref[...] = jnp.zeros_like(ck_ref)
        cv_ref[...] = jnp.zeros_like(cv_ref)

        @pl.when(n < NS)
        def _():
            col = lax.broadcasted_iota(jnp.int32, (2 * BLK, 2 * BLK), 1)
            hm = _head_masks()
            for g, (q_ref, (_, d)) in enumerate(zip((q0, q1, q2), DILATED_GROUPS)):
                def step(it, inside, g=g, q_ref=q_ref, d=d):
                    cur, prv = _band_rows(it, d)
                    last = _last_rows(it, d)
                    qp = q_ref[cur, :].astype(BF16)
                    if inside:
                        kprev, vprev = kc_ref[prv, :], vc_ref[prv, :]
                    else:
                        kprev, vprev = kp_ref[last, :], vp_ref[last, :]
                    kc = jnp.concatenate([kprev.astype(BF16), kc_ref[cur, :].astype(BF16)], axis=0)
                    vc = jnp.concatenate([vprev.astype(BF16), vc_ref[cur, :].astype(BF16)], axis=0)
                    dop = do_ref[cur, :]
                    prod = dop * o_ref[cur, :]
                    Lp = L_ref[cur, :]
                    qs = _stack_heads(qp, hm)
                    dos = _stack_heads(dop.astype(BF16), hm)
                    lse = jnp.concatenate([Lp[:, 0:1], Lp[:, HEAD_DIM:HEAD_DIM + 1]], axis=0)
                    delta = jnp.concatenate([jnp.sum(jnp.where(hm[0], prod, 0.0), axis=1, keepdims=True),
                                             jnp.sum(jnp.where(hm[1], prod, 0.0), axis=1, keepdims=True)], axis=0)
                    s = lax.dot_general(qs, kc, (((1,), (1,)), ((), ())), preferred_element_type=F32)
                    s = s + b_ref[g].reshape(2 * BLK, 2 * BLK)
                    if not inside:
                        s = jnp.where((n == 0) & (col < BLK), NEG, s)
                    pr = jnp.exp(s - lse)
                    dp = lax.dot_general(dos, vc, (((1,), (1,)), ((), ())), preferred_element_type=F32)
                    ds = pr * (dp - delta)
                    db_ref[g] += ds.reshape(2, BLK, 2 * BLK)
                    dsb = ds.astype(BF16)
                    dqs = jnp.dot(dsb, kc, preferred_element_type=F32)
                    dkc = lax.dot_general(dsb, qs, (((0,), (0,)), ((), ())), preferred_element_type=F32)
                    dvc = lax.dot_general(pr.astype(BF16), dos, (((0,), (0,)), ((), ())),
                                          preferred_element_type=F32)
                    dq_ref.at[g][cur, :] = jnp.where(hm[0], dqs[:BLK], dqs[BLK:]) * (HEAD_DIM ** -0.5)
                    ck_ref[cur, :] += dkc[BLK:]
                    cv_ref[cur, :] += dvc[BLK:]
                    if inside:
                        ck_ref[prv, :] += dkc[:BLK]
                        cv_ref[prv, :] += dvc[:BLK]
                    else:
                        dk_ref[last, :] += dkc[:BLK]
                        dv_ref[last, :] += dvc[:BLK]

                _band_loops(step, d, 4)

    blk = lambda f: pl.BlockSpec((SUPER, LANES), f)
    cur = lambda n: jnp.minimum(n, NS - 1)
    prev = lambda n: jnp.maximum(jnp.minimum(n, NS - 1) - 1, 0)
    lag = lambda n: jnp.maximum(n - 1, 0)
    qspec = lambda g: blk(lambda p, n: (cur(n), g * PP + p))
    bspec = pl.BlockSpec((N_GROUPS, 2, BLK, 2 * BLK), lambda p, n: (0, p, 0, 0))
    body, in_specs, args = _ordered(
        body, [qspec(0), qspec(1), qspec(2),
               blk(lambda p, n: (prev(n), p)), blk(lambda p, n: (cur(n), p)),
               blk(lambda p, n: (prev(n), PP + p)), blk(lambda p, n: (cur(n), PP + p)),
               bspec, blk(lambda p, n: (cur(n), p)), blk(lambda p, n: (cur(n), p)),
               blk(lambda p, n: (cur(n), p))],
        (q, q, q, kv, kv, kv, kv, bias, do, o, L), after)
    return pl.pallas_call(
        body, grid=(PP, NS + 1), in_specs=in_specs,
        out_specs=[pl.BlockSpec((N_GROUPS, SUPER, LANES), lambda p, n: (0, cur(n), p)),
                   blk(lambda p, n: (lag(n), p)), blk(lambda p, n: (lag(n), p)), bspec],
        out_shape=[jax.ShapeDtypeStruct((N_GROUPS, T, HD), F32), jax.ShapeDtypeStruct((T, HD), F32),
                   jax.ShapeDtypeStruct((T, HD), F32),
                   jax.ShapeDtypeStruct((N_GROUPS, H, BLK, 2 * BLK), F32)],
        scratch_shapes=[pltpu.VMEM((SUPER, LANES), F32), pltpu.VMEM((SUPER, LANES), F32)],
        compiler_params=_params(("arbitrary", "arbitrary")), name=name,
    )(*args)


class _Weights(dict):
    def __init__(self, base, fetch=None, emit=None, emit_small=None):
        super().__init__(base)
        self._fetch, self._emit, self._emit_small = fetch, emit, emit_small

    def fetch(self, group, after):
        if self._fetch is not None:
            for (key, layer), mat in self._fetch(group, after).items():
                self[key][layer] = mat

    def emit(self, group, mats):
        return None if self._emit is None else self._emit(group, mats)

    def emit_small(self, grads):
        return None if self._emit_small is None else self._emit_small(grads)


def _local_step(x, tgt, W):
    T, D = x.shape
    H = W["rel_table"].shape[1] // N_GROUPS
    HD = H * HEAD_DIM
    G = W["a_w_s"].shape[1]
    assert T % (DILATED_GROUPS[-1][1] * BLK) == 0

    tril = jnp.tril(jnp.ones((CHUNK, CHUNK), F32))
    bmap = jnp.asarray(_bucket_maps())
    bias = _band_bias(W["rel_table"], bmap, H, name="band_bias")

    saved = []
    xc, xcb = x, x.astype(BF16)
    kvb = None
    for i in range(DEPTH):
        s = {"x": xc, "xb": xcb}
        W.fetch(4 * i, xc)
        if i < N_A:
            ws_m = W["a_w_s"][i] * tril
            s["ws"] = ws_m.astype(BF16)
            s["wst"] = jnp.swapaxes(ws_m, 1, 2).astype(BF16)
            s["bst"] = W["a_b_s"][i].T
            s["zp"] = _mm(xcb, W["a_w_in"][i], out_dtype=ACT, name=f"a_in_{i}")
            s["y"] = _sgu_fwd(s["zp"], W["a_ln_g"][i], W["a_ln_b"][i], s["ws"], s["bst"], name=f"sgu_fwd_{i}")
            W.fetch(4 * i + 1, s["zp"])
            s["h"] = _mm(s["y"], W["a_w_out"][i], out_dtype=ACT, name=f"a_out_{i}")
        else:
            j = i - N_A
            if kvb is None:
                kvb = _mm(xcb, W["kv_w"][0], name="kv_proj")
            s["q"] = _mm(xcb, W["b_w_q_t"][j], tb=True, scale=HEAD_DIM ** -0.5, name=f"q_proj_{j}")
            s["o"], s["ob"], s["L"] = _attn_fwd_all(s["q"], kvb, bias, name=f"attn_fwd_{j}")
            W.fetch(4 * i + 1, s["q"])
            s["h"] = _mm(s["ob"], W["b_w_o"][j], out_dtype=ACT, name=f"o_proj_{j}")
        s["x1"], s["x1b"] = _add_ln_fwd(xc, s["h"], W["ln_g"][i, 0], W["ln_b"][i, 0], name=f"ln1_fwd_{i}")
        W.fetch(4 * i + 2, s["x1"])
        s["hh"] = _mm(s["x1b"], W["ffn_w_up_t"][i], tb=True, out_dtype=ACT, name=f"ffn_up_{i}")
        s["cw"] = W["ffn_conv_w"][i]
        s["cb"] = W["ffn_conv_b"][i].reshape(1, -1)
        s["act"], s["hca"], s["hcg"] = _convgate_fwd(s["hh"], s["cw"], s["cb"], name=f"convgate_fwd_{i}")
        W.fetch(4 * i + 3, s["hh"])
        s["f"] = _mm(s["act"], W["ffn_w_down"][i], out_dtype=ACT, name=f"ffn_down_{i}")
        xc, xcb = _add_ln_fwd(s["x1"], s["f"], W["ln_g"][i, 1], W["ln_b"][i, 1], name=f"ln2_fwd_{i}")
        saved.append(s)

    dy, lossv = _loss_grad(xc, tgt, name="loss_grad")
    loss = lossv[0, 0]

    gl = {k: [None] * DEPTH for k in ("ffn_w_up_t", "ffn_conv_w", "ffn_conv_b", "ffn_w_down", "ln_g", "ln_b")}
    ga = {k: [None] * N_A for k in ("a_w_in", "a_ln_g", "a_ln_b", "a_w_s", "a_b_s", "a_w_out")}
    gb = {k: [None] * (DEPTH - N_A) for k in ("b_w_q_t", "b_w_o")}
    mats = ("a_w_in", "a_w_out", "b_w_q_t", "b_w_o", "ffn_w_up_t", "ffn_w_down")
    dks, dvs, dbias = [], [], []
    grads = {}
    terms = [(1.0, dy)]
    tok = None
    small_keys = ("ffn_conv_w", "ffn_conv_b", "ln_g", "ln_b", "a_ln_g", "a_ln_b", "a_w_s", "a_b_s")
    for i in reversed(range(DEPTH)):
        s = saved[i]
        dp2, dp2b, dg2, db2 = _add_ln_bwd(s["x1"], s["f"], W["ln_g"][i, 1], terms, after=tok, name=f"ln2_bwd_{i}")
        dact = _mm(dp2b, W["ffn_w_down"][i], tb=True, out_dtype=ACT, name=f"ffn_down_dx_{i}")
        gl["ffn_w_down"][i] = _mm(s["act"], dp2b, ta=True, out_dtype=BF16, name=f"ffn_down_dw_{i}")
        dha, dhg, dwa, dwg, dba, dbg = _convgate_bwd(s["hh"], s["hca"], s["hcg"], dact, s["cw"],
                                                     name=f"convgate_bwd_{i}")
        dhh = (dha, dhg)
        gl["ffn_conv_w"][i] = jnp.concatenate([dwa, dwg], axis=1)
        gl["ffn_conv_b"][i] = jnp.concatenate([dba, dbg], axis=1)[0]
        dx1 = _mm(dhh, W["ffn_w_up_t"][i], out_dtype=ACT, name=f"ffn_up_dx_{i}")
        gl["ffn_w_up_t"][i] = _mm(dhh, s["x1b"], ta=True, out_dtype=BF16, name=f"ffn_up_dw_{i}")
        tok = W.emit(3 * i + 2, {("ffn_w_up_t", i): gl["ffn_w_up_t"][i], ("ffn_w_down", i): gl["ffn_w_down"][i]})
        dp1, dp1b, dg1, db1 = _add_ln_bwd(s["x"], s["h"], W["ln_g"][i, 0], [(ALPHA, dp2), (1.0, dx1)],
                                          after=tok, name=f"ln1_bwd_{i}")
        gl["ln_g"][i] = jnp.concatenate([dg1, dg2], axis=0)
        gl["ln_b"][i] = jnp.concatenate([db1, db2], axis=0)
        terms = [(ALPHA, dp1)]
        if i < N_A:
            dyy = _mm(dp1b, W["a_w_out"][i], tb=True, out_dtype=ACT, name=f"a_out_dx_{i}")
            ga["a_w_out"][i] = _mm(s["y"], dp1b, ta=True, out_dtype=BF16, name=f"a_out_dw_{i}")
            tok = W.emit(3 * i + 1, {("a_w_out", i): ga["a_w_out"][i]})
            dzp, dlg, dlb, dws, dbs = _sgu_bwd(s["zp"], dyy, W["a_ln_g"][i], W["a_ln_b"][i], s["ws"],
                                               s["wst"], s["bst"], after=tok, name=f"sgu_bwd_{i}")
            ga["a_ln_g"][i], ga["a_ln_b"][i], ga["a_w_s"][i] = dlg[0], dlb[0], dws
            ga["a_b_s"][i] = dbs[:, :G].T
            if i == 0:
                for dct in (gl, ga):
                    grads.update({k: jnp.stack(v) for k, v in dct.items() if k in small_keys})
                tok = W.emit_small(grads)
            ga["a_w_in"][i] = _mm(s["xb"], dzp, ta=True, out_dtype=BF16, after=tok, name=f"a_in_dw_{i}")
            tok = W.emit(3 * i, {("a_w_in", i): ga["a_w_in"][i]})
            terms.append((1.0, _mm(dzp, W["a_w_in"][i], tb=True, out_dtype=ACT, after=tok, name=f"a_in_dx_{i}")))
        else:
            j = i - N_A
            do = _mm(dp1b, W["b_w_o"][j], tb=True, name=f"o_proj_dx_{j}")
            gb["b_w_o"][j] = _mm(s["ob"], dp1b, ta=True, out_dtype=BF16, name=f"o_proj_dw_{j}")
            tok = W.emit(3 * i + 1, {("b_w_o", j): gb["b_w_o"][j]})
            dq, dk_j, dv_j, db_j = _attn_bwd_all(s["q"], kvb, bias, do, s["o"], s["L"], after=tok,
                                                 name=f"attn_bwd_{j}")
            dks.append((1.0, dk_j))
            dvs.append((1.0, dv_j))
            dbias.append(db_j)
            terms.append((1.0, _mm(dq, W["b_w_q_t"][j], out_dtype=ACT, name=f"q_proj_dx_{j}")))
            gb["b_w_q_t"][j] = _mm(dq, s["xb"], ta=True, out_dtype=BF16, name=f"q_proj_dw_{j}")
            out_b = {("b_w_q_t", j): gb["b_w_q_t"][j]}
            if i == N_A:
                dkv = jnp.concatenate([_lincomb(dks, BF16, name="dk_sum"), _lincomb(dvs, BF16, name="dv_sum")],
                                      axis=1)
                terms.append((1.0, _mm(dkv, W["kv_w"][0], tb=True, out_dtype=ACT, name="kv_proj_dx")))
                grads["kv_w"] = [_mm(s["xb"], dkv, ta=True, out_dtype=BF16, name="kv_proj_dw")]
                out_b[("kv_w", 0)] = grads["kv_w"][0]
                dbt = _lincomb([(1.0, a.reshape(-1, 2 * BLK)) for a in dbias], F32, name="dbias_sum")
                dtab = _band_bias_bwd(dbt.reshape(N_GROUPS, H, BLK, 2 * BLK), bmap, H, name="band_bias_bwd")
                grads["rel_table"] = jnp.transpose(dtab[:, :, :H], (1, 0, 2)).reshape(REL_BUCKETS, N_GROUPS * H)
            tok = W.emit(3 * i, out_b)
    grad_x = _lincomb(terms, F32, name="grad_x")
    for dct in (gl, ga, gb):
        grads.update({k: v for k, v in dct.items() if k in mats})
    return loss, grad_x, grads


def _my_index():
    return 4 * lax.axis_index("x") + 2 * lax.axis_index("y") + lax.axis_index("c")


HBM_SPEC = pl.BlockSpec(memory_space=pltpu.HBM)


def _block(ref, k, n, axis):
    off = pl.multiple_of(k * n, n)
    return ref.at[pl.ds(off, n), :] if axis == 0 else ref.at[:, pl.ds(off, n)]


def _gather_mats(local, axis, *, name):
    L, a, b = local.shape
    n = a if axis == 0 else b
    full = (a * N_DEV, b) if axis == 0 else (a, b * N_DEV)

    def body(x_ref, *rest):
        outs = rest[:L]
        send_sems, recv_sems, local_sems = rest[L:]
        x, y, c = lax.axis_index("x"), lax.axis_index("y"), lax.axis_index("c")
        me, sibling = (x, y, c), (x, y, 1 - c)
        chips = [(1 - x, y), (x, 1 - y), (1 - x, 1 - y)]

        def slot(l, px, py, pc):
            return _block(outs[l], 4 * px + 2 * py + pc, n, axis)

        def copy(l, k, blk, to, src=None):
            return pltpu.make_async_remote_copy(
                src_ref=slot(l, *blk) if src is None else src, dst_ref=slot(l, *blk),
                send_sem=send_sems.at[7 * l + k], recv_sem=recv_sems.at[7 * l + k],
                device_id=to, device_id_type=MESH)

        mine, first, passed = [], [], []
        for l in range(L):
            mine.append(pltpu.make_async_copy(x_ref.at[l], slot(l, *me), local_sems.at[l]))
            mine[-1].start()
            first.append(copy(l, 0, me, sibling, src=x_ref.at[l]))
            first += [copy(l, 1 + j, me, (*chip, c), src=x_ref.at[l]) for j, chip in enumerate(chips)]
        for cp in first:
            cp.start()
        for l in range(L):
            for j, chip in enumerate(chips):
                copy(l, 1 + j, (*chip, c), me).wait_recv()
                passed.append(copy(l, 4 + j, (*chip, c), sibling))
                passed[-1].start()
        for l in range(L):
            copy(l, 0, sibling, me).wait_recv()
            for j, chip in enumerate(chips):
                copy(l, 4 + j, (*chip, 1 - c), me).wait_recv()
        for cp in first + passed:
            cp.wait_send()
        for cp in mine:
            cp.wait()

    return pl.pallas_call(
        body, out_shape=[jax.ShapeDtypeStruct(full, local.dtype)] * L,
        in_specs=[HBM_SPEC], out_specs=[HBM_SPEC] * L,
        scratch_shapes=[pltpu.SemaphoreType.DMA((7 * L,)), pltpu.SemaphoreType.DMA((7 * L,)),
                        pltpu.SemaphoreType.DMA((L,))],
        name=name,
    )(local)


SEM_SPEC = pl.BlockSpec(memory_space=pltpu.SEMAPHORE)
FLOWING = pltpu.SideEffectType.DATAFLOW_SIDE_EFFECTING


def _peers(x, y, c):
    return [(1 - x if k & 4 else x, 1 - y if k & 2 else y, 1 - c if k & 1 else c) for k in range(1, N_DEV)]


def _ends(src_ref, land_ref, peer_index, me, n, axis, gather):
    if gather:
        return src_ref, _block(land_ref, me, n, axis)
    return _block(src_ref, peer_index, n, axis), land_ref.at[me]


def _send_start(groups, gather, *, name):
    flat = [(g, j, mat, axis) for g, items in enumerate(groups) for j, (mat, axis) in enumerate(items)]
    M, G = len(flat), len(groups)
    lands, ns = [], []
    for _, _, mat, axis in flat:
        A, B = mat.shape
        if gather:
            lands.append((A * N_DEV, B) if axis == 0 else (A, B * N_DEV))
            ns.append(A if axis == 0 else B)
        else:
            lands.append((N_DEV, A // N_DEV, B) if axis == 0 else (N_DEV, A, B // N_DEV))
            ns.append(A // N_DEV if axis == 0 else B // N_DEV)

    def body(*refs):
        src_refs, land_refs, sems = refs[:M], refs[M:2 * M], refs[2 * M:2 * M + 3 * G]
        token = refs[-1]
        x, y, c = lax.axis_index("x"), lax.axis_index("y"), lax.axis_index("c")
        me = 4 * x + 2 * y + c
        for i, (g, j, _, axis) in enumerate(flat):
            for k, (px, py, pc) in enumerate(_peers(x, y, c)):
                s, d = _ends(src_refs[i], land_refs[i], 4 * px + 2 * py + pc, me, ns[i], axis, gather)
                pltpu.make_async_remote_copy(
                    src_ref=s, dst_ref=d, send_sem=sems[3 * g].at[7 * j + k], recv_sem=sems[3 * g + 1].at[7 * j + k],
                    device_id=(px, py, pc), device_id_type=MESH).start()
            s, d = _ends(src_refs[i], land_refs[i], me, me, ns[i], axis, gather)
            pltpu.make_async_copy(s, d, sems[3 * g + 2].at[j]).start()
        token[...] = jnp.zeros_like(token)

    sem_shapes = []
    for items in groups:
        sem_shapes += [pltpu.SemaphoreType.DMA((7 * len(items),))] * 2 + [pltpu.SemaphoreType.DMA((len(items),))]
    outs = pl.pallas_call(
        body, name=name,
        out_shape=(*sem_shapes, *[pltpu.HBM(m.shape, m.dtype) for _, _, m, _ in flat],
                   *[pltpu.HBM(shp, m.dtype) for shp, (_, _, m, _) in zip(lands, flat)],
                   jax.ShapeDtypeStruct((8, LANES), F32)),
        in_specs=[HBM_SPEC] * (2 * M),
        out_specs=(*[SEM_SPEC] * (3 * G), *[HBM_SPEC] * (2 * M), pl.BlockSpec(memory_space=pltpu.VMEM)),
        input_output_aliases={i: 3 * G + i for i in range(2 * M)},
        compiler_params=pltpu.CompilerParams(has_side_effects=FLOWING),
    )(*[pltpu.with_memory_space_constraint(m, pltpu.HBM) for _, _, m, _ in flat],
      *[pltpu.with_memory_space_constraint(lax.empty(shp, m.dtype), pltpu.HBM)
        for shp, (_, _, m, _) in zip(lands, flat)])
    handles = []
    for g in range(G):
        idx = [i for i, f in enumerate(flat) if f[0] == g]
        handles.append((outs[3 * g], outs[3 * g + 1], outs[3 * g + 2], [outs[3 * G + i] for i in idx],
                        [outs[3 * G + M + i] for i in idx], [flat[i][3] for i in idx]))
    return handles, outs[-1]


def _send_wait(handle, gather, after, *, name):
    send_sems, recv_sems, local_sems, mats, lands, axes = handle
    n_m = len(mats)
    ns = []
    for mat, land, axis in zip(mats, lands, axes):
        ns.append(mat.shape[axis] if gather else land.shape[1 + axis])

    def body(*refs):
        src_refs, land_refs = refs[:n_m], refs[n_m:2 * n_m]
        ssem, rsem, lsem = refs[2 * n_m:2 * n_m + 3]
        x, y, c = lax.axis_index("x"), lax.axis_index("y"), lax.axis_index("c")
        me = 4 * x + 2 * y + c
        for j in range(n_m):
            for k, (px, py, pc) in enumerate(_peers(x, y, c)):
                s, d = _ends(src_refs[j], land_refs[j], 4 * px + 2 * py + pc, me, ns[j], axes[j], gather)
                cp = pltpu.make_async_remote_copy(
                    src_ref=s, dst_ref=d, send_sem=ssem.at[7 * j + k], recv_sem=rsem.at[7 * j + k],
                    device_id=(px, py, pc), device_id_type=MESH)
                cp.wait_send()
                cp.wait_recv()
            s, d = _ends(src_refs[j], land_refs[j], me, me, ns[j], axes[j], gather)
            pltpu.make_async_copy(s, d, lsem.at[j]).wait()

    outs = pl.pallas_call(
        body, name=name,
        out_shape=(*[pltpu.HBM(m.shape, m.dtype) for m in mats], *[pltpu.HBM(l.shape, l.dtype) for l in lands]),
        in_specs=[HBM_SPEC] * (2 * n_m) + [SEM_SPEC] * 3 + [pl.BlockSpec(memory_space=pl.ANY)],
        out_specs=tuple([HBM_SPEC] * (2 * n_m)),
        input_output_aliases={i: i for i in range(2 * n_m)},
        compiler_params=pltpu.CompilerParams(has_side_effects=FLOWING),
    )(*mats, *lands, send_sems, recv_sems, local_sems, after)
    return list(outs[n_m:])


def _sum_parts(parts, *, name):
    n, R, C = parts.shape
    rb = _pick(R, 512) if R % LANES == 0 else R

    def body(p_ref, o_ref):
        acc = p_ref[0].astype(F32)
        for k in range(1, n):
            acc = acc + p_ref[k].astype(F32)
        o_ref[...] = acc

    return pl.pallas_call(
        body, grid=(R // rb,), in_specs=[pl.BlockSpec((n, rb, C), lambda i: (0, i, 0))],
        out_specs=pl.BlockSpec((rb, C), lambda i: (i, 0)),
        out_shape=jax.ShapeDtypeStruct((R, C), F32),
        compiler_params=_params(("parallel",)), name=name,
    )(parts)


def _adamw(w, m, v, parts, *, name):
    L, R, C = w.shape
    n = parts[0].shape[0]
    cap = max(16, VMEM_LIMIT // 3 // (2 * L * n * C * parts[0].dtype.itemsize))
    rb = max([r for r in range(16, min(R, cap) + 1, 16) if R % r == 0], default=R)

    def body(w_ref, m_ref, v_ref, *rest):
        p_refs = rest[:L]
        g_ref, d_ref, nm_ref, nv_ref = rest[L:]
        for l in range(L):
            @pl.when(pl.program_id(0) == l)
            def _(p_ref=p_refs[l]):
                g = p_ref[0].astype(F32)
                for k in range(1, n):
                    g = g + p_ref[k].astype(F32)
                mn = ADAM_B1 * m_ref[...] + (1.0 - ADAM_B1) * g
                vn = ADAM_B2 * v_ref[...] + (1.0 - ADAM_B2) * jnp.square(g)
                m_hat = mn / (1.0 - ADAM_B1 ** ADAM_STEP)
                v_hat = vn / (1.0 - ADAM_B2 ** ADAM_STEP)
                g_ref[...] = g
                d_ref[...] = -ADAM_LR * (m_hat / (jnp.sqrt(v_hat) + ADAM_EPS) + ADAM_WD * w_ref[...])
                nm_ref[...] = mn
                nv_ref[...] = vn

    row = pl.BlockSpec((None, rb, C), lambda l, i: (l, i, 0))
    part = lambda k: pl.BlockSpec((n, rb, C), lambda l, i: (0, jnp.where(l == k, i, 0), 0))
    return pl.pallas_call(
        body, grid=(L, R // rb), in_specs=[row, row, row] + [part(k) for k in range(L)],
        out_specs=[row] * 4, out_shape=[jax.ShapeDtypeStruct((L, R, C), F32)] * 4,
        compiler_params=_params(("arbitrary", "arbitrary")), name=name,
    )(w, m, v, *parts)


BIG = (("a_w_in", "a_w_in", 1, False), ("a_w_out", "a_w_out", 0, False), ("kv_w", "kv_w", 0, False),
       ("b_w_q", "b_w_q_t", 0, True), ("b_w_o", "b_w_o", 1, False), ("ffn_w_up", "ffn_w_up_t", 0, True),
       ("ffn_w_down", "ffn_w_down", 0, False))
SMALL_SHARDED = (("a_ln_g", 1), ("a_ln_b", 1), ("ffn_conv_w", 2), ("ln_g", 2), ("ln_b", 2))
REPLICATED = ("a_w_s", "a_b_s", "rel_table", "ffn_conv_b")


def _pack_rows(arrs, lead=0):
    lshape = arrs[0].shape[:lead]
    p = jnp.concatenate([a.reshape(*lshape, -1, LANES) for a in arrs], axis=lead)
    pad = -p.shape[lead] % 8
    return jnp.pad(p, [(0, 0)] * lead + [(0, pad), (0, 0)])


def _unpack_rows(packed, shapes, lead=0):
    lshape = packed.shape[:lead]
    out, off = [], 0
    for shp in shapes:
        r = int(np.prod(shp)) // LANES
        out.append(lax.slice_in_dim(packed, off, off + r, axis=lead).reshape(*lshape, *shp))
        off += r
    return out


def _as_mats(a, transposed):
    a = a[None] if a.ndim == 2 else a
    return jnp.swapaxes(a, 1, 2) if transposed else a


def _merge_shards(stacked, axis):
    a = jnp.moveaxis(stacked, 0, axis)
    shp = list(a.shape)
    return a.reshape(shp[:axis] + [shp[axis] * shp[axis + 1]] + shp[axis + 2:])


def _split_shards(full, axis):
    shp = list(full.shape)
    a = full.reshape(shp[:axis] + [N_DEV, shp[axis] // N_DEV] + shp[axis + 1:])
    return jnp.moveaxis(a, axis, 0)


def kernel(x, a_w_in, a_ln_g, a_ln_b, a_w_s, a_b_s, a_w_out, kv_w, b_w_q, b_w_o, rel_table, ffn_w_up, ffn_conv_w, ffn_conv_b, ffn_w_down, ln_g, ln_b, loss_target, m_a_w_in, m_a_ln_g, m_a_ln_b, m_a_w_s, m_a_b_s, m_a_w_out, m_kv_w, m_b_w_q, m_b_w_o, m_rel_table, m_ffn_w_up, m_ffn_conv_w, m_ffn_conv_b, m_ffn_w_down, m_ln_g, m_ln_b, v_a_w_in, v_a_ln_g, v_a_ln_b, v_a_w_s, v_a_b_s, v_a_w_out, v_kv_w, v_b_w_q, v_b_w_o, v_rel_table, v_ffn_w_up, v_ffn_conv_w, v_ffn_conv_b, v_ffn_w_down, v_ln_g, v_ln_b):
    names = ["a_w_in", "a_ln_g", "a_ln_b", "a_w_s", "a_b_s", "a_w_out", "kv_w", "b_w_q", "b_w_o", "rel_table",
             "ffn_w_up", "ffn_conv_w", "ffn_conv_b", "ffn_w_down", "ln_g", "ln_b"]
    w = dict(zip(names, (a_w_in, a_ln_g, a_ln_b, a_w_s, a_b_s, a_w_out, kv_w, b_w_q, b_w_o, rel_table,
                         ffn_w_up, ffn_conv_w, ffn_conv_b, ffn_w_down, ln_g, ln_b)))
    m = dict(zip(names, (m_a_w_in, m_a_ln_g, m_a_ln_b, m_a_w_s, m_a_b_s, m_a_w_out, m_kv_w, m_b_w_q, m_b_w_o,
                         m_rel_table, m_ffn_w_up, m_ffn_conv_w, m_ffn_conv_b, m_ffn_w_down, m_ln_g, m_ln_b)))
    v = dict(zip(names, (v_a_w_in, v_a_ln_g, v_a_ln_b, v_a_w_s, v_a_b_s, v_a_w_out, v_kv_w, v_b_w_q, v_b_w_o,
                         v_rel_table, v_ffn_w_up, v_ffn_conv_w, v_ffn_conv_b, v_ffn_w_down, v_ln_g, v_ln_b)))
    small_names = [n for n, _ in SMALL_SHARDED]
    small_shapes = [w[n].shape for n in small_names]
    rep_shapes = [w[n].shape for n in REPLICATED]

    axis_of = {key: axis for _, key, axis, _ in BIG}
    src = {}
    for n, key, axis, tr in BIG:
        loc = _as_mats(w[n], tr).astype(BF16)
        for l in range(loc.shape[0]):
            src[(key, l)] = loc[l]
    order = []
    for i in range(DEPTH):
        if i < N_A:
            order += [[("a_w_in", i)], [("a_w_out", i)]]
        else:
            order += [([("kv_w", 0)] if i == N_A else []) + [("b_w_q_t", i - N_A)], [("b_w_o", i - N_A)]]
        order += [[("ffn_w_up_t", i)], [("ffn_w_down", i)]]
    small_src = _pack_rows([w[n] for n in small_names])
    srows = small_src.shape[0]
    handles, _ = _send_start([[(small_src, 0)]] + [[(src[kl], axis_of[kl[0]]) for kl in grp] for grp in order],
                             True, name="gather_start")
    small_all = _send_wait(handles[0], True, x, name="gather_wait_small")[0]
    small_st = _unpack_rows(small_all.reshape(N_DEV, srows, LANES), small_shapes, lead=1)
    base = {n: w[n] for n in REPLICATED}
    for (n, ax), st in zip(SMALL_SHARDED, small_st):
        base[n] = _merge_shards(st, ax)
    for n, key, _, tr in BIG:
        base[key] = [None] * (1 if w[n].ndim == 2 else w[n].shape[0])

    def fetch(group, after):
        mats = _send_wait(handles[1 + group], True, after, name=f"gather_wait_{group}")
        return dict(zip(order[group], mats))

    sent = {}

    def emit(group, mats):
        keys = list(mats)
        hs, token = _send_start([[(mats[kl], axis_of[kl[0]]) for kl in keys]], False, name=f"exchange_start_{group}")
        sent[group] = (keys, hs[0])
        return token

    small_sent = []

    def emit_small(grads):
        small_pack = _pack_rows([_split_shards(grads[n], ax) for n, ax in SMALL_SHARDED], lead=1)
        rest = _pack_rows([grads[n] for n in REPLICATED[1:]])
        mine = jnp.concatenate([small_pack.reshape(N_DEV * srows, LANES), rest], axis=0)
        gating = grads[REPLICATED[0]].reshape(-1, LANES).astype(BF16)
        hs, token = _send_start([[(mine, 0), (gating, 0)]], True, name="small_grads_start")
        small_sent.append(hs[0])
        return token

    loss, grad_x, grads = _local_step(x[0], loss_target[0], _Weights(base, fetch, emit, emit_small))
    loss = lax.psum(loss, ("x", "y", "c"))
    out = {}

    landed = {}
    for group in sorted(sent, reverse=True):
        keys, h = sent[group]
        landed.update(zip(keys, _send_wait(h, False, grad_x, name=f"exchange_wait_{group}")))
    last = grad_x
    for n, key, axis, tr in BIG:
        shp = w[n].shape
        parts = [landed[(key, l)] for l in range(1 if len(shp) == 2 else shp[0])]
        res = _adamw(_as_mats(w[n], tr), _as_mats(m[n], tr), _as_mats(v[n], tr), parts, name=f"adamw_{n}")
        out[n] = [(jnp.swapaxes(r, 1, 2) if tr else r).reshape(shp) for r in res]
        last = res[0]

    allp, allg = _send_wait(small_sent[0], True, last, name="small_grads_wait")
    gsum = _sum_parts(allp.reshape(N_DEV, -1, LANES), name="sum_small_grads")
    gating = _sum_parts(allg.reshape(N_DEV, -1, LANES), name="sum_gating_grads")
    g_small = lax.dynamic_slice_in_dim(gsum, _my_index() * srows, srows, axis=0)
    pack_sr = lambda d: jnp.concatenate([_pack_rows([d[n] for n in small_names]),
                                         _pack_rows([d[n] for n in REPLICATED])], axis=0)
    n_rest = sum(int(np.prod(s)) for s in rep_shapes[1:]) // LANES
    gs_in = jnp.concatenate([g_small, gating, gsum[N_DEV * srows:N_DEV * srows + n_rest]], axis=0)
    gs_in = jnp.pad(gs_in, ((0, pack_sr(w).shape[0] - gs_in.shape[0]), (0, 0)))[None]
    res = _adamw(pack_sr(w)[None], pack_sr(m)[None], pack_sr(v)[None], [gs_in], name="adamw_small")
    for n, vals in zip(small_names, zip(*[_unpack_rows(r[0, :srows], small_shapes) for r in res])):
        out[n] = list(vals)
    for n, vals in zip(REPLICATED, zip(*[_unpack_rows(r[0, srows:], rep_shapes) for r in res])):
        out[n] = list(vals)

    return (loss, grad_x[None], *[out[n][0] for n in names], *[out[n][1] for n in names],
            *[out[n][2] for n in names], *[out[n][3] for n in names])
```

```python
import math

import numpy as np
import jax
import jax.numpy as jnp
from jax import lax
from jax.experimental import pallas as pl
from jax.experimental.pallas import tpu as pltpu

F32 = jnp.float32
BF16 = jnp.bfloat16
ACT = jnp.bfloat16
MESH = pl.DeviceIdType.MESH

N_DEV = 8
DEPTH = 4
N_A = 2
CHUNK = 128
BLK = 128
HEAD_DIM = 64
DILATED_GROUPS = ((128, 1), (512, 4), (2048, 16))
N_GROUPS = 3
REL_BUCKETS = 32
REL_MAX_DIST = 2048
ALPHA = (2 * DEPTH) ** 0.25
LN_EPS = 1e-5
NEG = -1e30
ADAM_LR = 0.001
ADAM_B1 = 0.9
ADAM_B2 = 0.999
ADAM_EPS = 1e-08
ADAM_WD = 0.01
ADAM_STEP = 10

LANES = 128
VMEM_LIMIT = 56 * 1024 * 1024
MM_TILE_CAP = 1408
MM_VMEM_BUDGET = 46 * 1024 * 1024
INV_SQRT2 = 1.0 / math.sqrt(2.0)
INV_SQRT_2PI = 1.0 / math.sqrt(2.0 * math.pi)


def _pick(n, cap):
    best = None
    for t in range(LANES, min(n, cap) + 1, LANES):
        if n % t == 0:
            best = t
    return best if best is not None else n


def _params(sem):
    return pltpu.CompilerParams(dimension_semantics=sem, vmem_limit_bytes=VMEM_LIMIT)


def _ordered(body, in_specs, args, after):
    if after is None:
        return body, list(in_specs), tuple(args)
    return (lambda _, *refs: body(*refs)), [pl.BlockSpec(memory_space=pl.ANY), *in_specs], (after, *args)


def _gelu(x):
    return 0.5 * x * (1.0 + lax.erf(x * INV_SQRT2))


def _gelu_grad(x):
    return 0.5 * (1.0 + lax.erf(x * INV_SQRT2)) + x * jnp.exp(-0.5 * x * x) * INV_SQRT_2PI


def _mm(a, b, *, ta=False, tb=False, out_dtype=F32, scale=None, after=None, name):
    halves = isinstance(a, tuple)
    parts = 1 if halves or a.ndim == 2 else a.shape[0]
    ash = (a[0].shape[0], 2 * a[0].shape[1]) if halves else (a.shape if parts == 1
                                                               else (a.shape[1], parts * a.shape[2]))
    if ta:
        K, M = ash
    else:
        M, K = ash
    if tb:
        N, Kb = b.shape
    else:
        Kb, N = b.shape
    assert K == Kb, (ash, b.shape, ta, tb)
    split = 2 if halves else parts
    tm = _pick(M // split if ta else M, MM_TILE_CAP)
    tn = _pick(N, MM_TILE_CAP)
    kspan = K if ta or split == 1 else K // split
    abytes = (a[0] if halves else a).dtype.itemsize * (2 if halves else 1)
    fixed = 2 * tm * tn * jnp.dtype(out_dtype).itemsize + tm * tn * 4
    fits = [t for t in range(LANES, kspan + 1, LANES)
            if kspan % t == 0 and 2 * t * (tm * abytes + tn * b.dtype.itemsize) + fixed <= MM_VMEM_BUDGET]
    tk = max(fits) if fits else _pick(kspan, MM_TILE_CAP)
    nk = K // tk
    nh = (M // split // tm if ta else K // split // tk) if split > 1 else 0
    dn = (((0 if ta else 1,), (1 if tb else 0,)), ((), ()))

    def body(*refs):
        n_tail = 3 if nk > 1 else 2
        a_refs, b_ref, o_ref = refs[:-n_tail], refs[-n_tail], refs[-n_tail + 1]
        k = pl.program_id(2)

        def finish(r):
            if scale is not None:
                r = r * scale
            o_ref[...] = r.astype(out_dtype)

        def accumulate(a_ref):
            part = lax.dot_general(a_ref[...].astype(BF16), b_ref[...].astype(BF16), dn,
                                   preferred_element_type=F32)
            if nk == 1:
                finish(part)
                return
            acc_ref = refs[-1]

            @pl.when(k == 0)
            def _():
                acc_ref[...] = part

            @pl.when((k > 0) & (k < nk - 1))
            def _():
                acc_ref[...] += part

            @pl.when(k == nk - 1)
            def _():
                finish(acc_ref[...] + part)

        if halves:
            first = (pl.program_id(0) if ta else k) < nh
            pl.when(first)(lambda: accumulate(a_refs[0]))
            pl.when(jnp.logical_not(first))(lambda: accumulate(a_refs[1]))
        else:
            accumulate(a_refs[0])

    if halves and ta:
        a_specs = [pl.BlockSpec((tk, tm), lambda i, j, k: (jnp.where(i < nh, k, 0), jnp.minimum(i, nh - 1))),
                   pl.BlockSpec((tk, tm), lambda i, j, k: (jnp.where(i >= nh, k, 0), jnp.maximum(i - nh, 0)))]
    elif halves:
        a_specs = [pl.BlockSpec((tm, tk), lambda i, j, k: (i, jnp.minimum(k, nh - 1))),
                   pl.BlockSpec((tm, tk), lambda i, j, k: (i, jnp.maximum(k - nh, 0)))]
    elif parts > 1:
        a_specs = [pl.BlockSpec((None, tk, tm), lambda i, j, k: (i // nh, k, i % nh)) if ta
                   else pl.BlockSpec((None, tm, tk), lambda i, j, k: (k // nh, i, k % nh))]
    else:
        a_specs = [pl.BlockSpec((tk, tm), lambda i, j, k: (k, i)) if ta
                   else pl.BlockSpec((tm, tk), lambda i, j, k: (i, k))]
    b_spec = (pl.BlockSpec((tn, tk), lambda i, j, k: (j, k)) if tb
              else pl.BlockSpec((tk, tn), lambda i, j, k: (k, j)))
    body, in_specs, args = _ordered(body, [*a_specs, b_spec], (*(a if halves else (a,)), b), after)
    return pl.pallas_call(
        body, grid=(M // tm, N // tn, nk), in_specs=in_specs,
        out_specs=pl.BlockSpec((tm, tn), lambda i, j, k: (i, j)),
        out_shape=jax.ShapeDtypeStruct((M, N), out_dtype),
        scratch_shapes=[pltpu.VMEM((tm, tn), F32)] if nk > 1 else [],
        compiler_params=_params(("parallel", "parallel", "arbitrary")), name=name,
    )(*args)


def _add_ln_fwd(x, h, g, b, *, name):
    T, D = x.shape
    rb = _pick(T, 512)

    def body(x_ref, h_ref, g_ref, b_ref, o_ref, ob_ref):
        pre = ALPHA * x_ref[...] + h_ref[...].astype(F32)
        mu = jnp.mean(pre, axis=1, keepdims=True)
        cen = pre - mu
        var = jnp.mean(cen * cen, axis=1, keepdims=True)
        y = cen * lax.rsqrt(var + LN_EPS) * g_ref[...] + b_ref[...]
        o_ref[...] = y
        ob_ref[...] = y.astype(BF16)

    row = pl.BlockSpec((rb, D), lambda i: (i, 0))
    vec = pl.BlockSpec((1, D), lambda i: (0, 0))
    return pl.pallas_call(
        body, grid=(T // rb,), in_specs=[row, row, vec, vec], out_specs=[row, row],
        out_shape=[jax.ShapeDtypeStruct((T, D), F32), jax.ShapeDtypeStruct((T, D), BF16)],
        compiler_params=_params(("parallel",)), name=name,
    )(x, h, g.reshape(1, D), b.reshape(1, D))


def _add_ln_bwd(x, h, g, terms, *, after=None, name):
    T, D = x.shape
    rb = _pick(T, 512)
    coefs = [c for c, _ in terms]
    nt = len(terms)

    def body(*refs):
        x_ref, h_ref, g_ref = refs[:3]
        t_refs = refs[3:3 + nt]
        dp_ref, dpb_ref, dg_ref, db_ref = refs[3 + nt:]
        dy = None
        for c, r in zip(coefs, t_refs):
            v = r[...].astype(F32) if c == 1.0 else c * r[...].astype(F32)
            dy = v if dy is None else dy + v
        pre = ALPHA * x_ref[...] + h_ref[...].astype(F32)
        mu = jnp.mean(pre, axis=1, keepdims=True)
        cen = pre - mu
        var = jnp.mean(cen * cen, axis=1, keepdims=True)
        rstd = lax.rsqrt(var + LN_EPS)
        xhat = cen * rstd
        dxh = dy * g_ref[...]
        m1 = jnp.mean(dxh, axis=1, keepdims=True)
        m2 = jnp.mean(dxh * xhat, axis=1, keepdims=True)
        dpre = rstd * (dxh - m1 - xhat * m2)
        dp_ref[...] = dpre
        dpb_ref[...] = dpre.astype(BF16)
        dg = jnp.sum(dy * xhat, axis=0, keepdims=True)
        db = jnp.sum(dy, axis=0, keepdims=True)

        @pl.when(pl.program_id(0) == 0)
        def _():
            dg_ref[...] = dg
            db_ref[...] = db

        @pl.when(pl.program_id(0) > 0)
        def _():
            dg_ref[...] += dg
            db_ref[...] += db

    row = pl.BlockSpec((rb, D), lambda i: (i, 0))
    vec = pl.BlockSpec((1, D), lambda i: (0, 0))
    body, in_specs, args = _ordered(body, [row, row, vec] + [row] * nt,
                                    (x, h, g.reshape(1, D), *[a for _, a in terms]), after)
    return pl.pallas_call(
        body, grid=(T // rb,), in_specs=in_specs,
        out_specs=[row, row, vec, vec],
        out_shape=[jax.ShapeDtypeStruct((T, D), F32), jax.ShapeDtypeStruct((T, D), BF16),
                   jax.ShapeDtypeStruct((1, D), F32), jax.ShapeDtypeStruct((1, D), F32)],
        compiler_params=_params(("arbitrary",)), name=name,
    )(*args)


def _lincomb(terms, out_dtype, *, name):
    R, C = terms[0][1].shape
    rb = _pick(R, 512)
    coefs = [c for c, _ in terms]
    nt = len(terms)

    def body(*refs):
        acc = None
        for c, r in zip(coefs, refs[:nt]):
            v = r[...].astype(F32)
            v = v if c == 1.0 else c * v
            acc = v if acc is None else acc + v
        refs[nt][...] = acc.astype(out_dtype)

    row = pl.BlockSpec((rb, C), lambda i: (i, 0))
    return pl.pallas_call(
        body, grid=(R // rb,), in_specs=[row] * nt, out_specs=row,
        out_shape=jax.ShapeDtypeStruct((R, C), out_dtype),
        compiler_params=_params(("parallel",)), name=name,
    )(*[a for _, a in terms])


def _loss_grad(y, tgt, *, name):
    T, D = y.shape
    rb = _pick(T, 512)

    def body(y_ref, t_ref, dy_ref, l_ref):
        err = y_ref[...] - t_ref[...]
        dy_ref[...] = err * (1.0 / D)
        part = jnp.sum(jnp.sum(err * err, axis=1, keepdims=True), axis=0, keepdims=True) * (0.5 / D)
        part = jnp.broadcast_to(part, (1, LANES))

        @pl.when(pl.program_id(0) == 0)
        def _():
            l_ref[...] = part

        @pl.when(pl.program_id(0) > 0)
        def _():
            l_ref[...] += part

    row = pl.BlockSpec((rb, D), lambda i: (i, 0))
    return pl.pallas_call(
        body, grid=(T // rb,), in_specs=[row, row],
        out_specs=[row, pl.BlockSpec((1, LANES), lambda i: (0, 0))],
        out_shape=[jax.ShapeDtypeStruct((T, D), F32), jax.ShapeDtypeStruct((1, LANES), F32)],
        compiler_params=_params(("arbitrary",)), name=name,
    )(y, tgt)


def _sgu_fwd(zp, ln_g, ln_b, ws, bst, *, name):
    T, E2 = zp.shape
    E = E2 // 2
    G = ws.shape[0]
    cg = E // G
    rb = 2 * CHUNK

    def body(z_ref, g_ref, b_ref, ws_ref, bs_ref, y_ref):
        u = _gelu(z_ref[:, :E].astype(F32))
        v = _gelu(z_ref[:, E:].astype(F32))
        mu = jnp.mean(v, axis=1, keepdims=True)
        cen = v - mu
        var = jnp.mean(cen * cen, axis=1, keepdims=True)
        vn = (cen * lax.rsqrt(var + LN_EPS) * g_ref[...] + b_ref[...]).astype(BF16)
        for ci in range(rb // CHUNK):
            rows = slice(ci * CHUNK, (ci + 1) * CHUNK)
            for gi in range(G):
                cols = slice(gi * cg, (gi + 1) * cg)
                sv = jnp.dot(ws_ref[gi], vn[rows, cols], preferred_element_type=F32)
                sv = sv + bs_ref[:, gi:gi + 1]
                y_ref[rows, cols] = (u[rows, cols] * sv).astype(BF16)

    return pl.pallas_call(
        body, grid=(T // rb,),
        in_specs=[pl.BlockSpec((rb, E2), lambda i: (i, 0)),
                  pl.BlockSpec((1, E), lambda i: (0, 0)), pl.BlockSpec((1, E), lambda i: (0, 0)),
                  pl.BlockSpec((G, CHUNK, CHUNK), lambda i: (0, 0, 0)),
                  pl.BlockSpec((CHUNK, G), lambda i: (0, 0))],
        out_specs=pl.BlockSpec((rb, E), lambda i: (i, 0)),
        out_shape=jax.ShapeDtypeStruct((T, E), BF16),
        compiler_params=_params(("parallel",)), name=name,
    )(zp, ln_g.reshape(1, E), ln_b.reshape(1, E), ws, bst)


def _sgu_bwd(zp, dy, ln_g, ln_b, ws, wst, bst, *, after=None, name):
    T, E2 = zp.shape
    E = E2 // 2
    G = ws.shape[0]
    cg = E // G
    rb = CHUNK
    nsteps = T // rb

    def body(z_ref, dy_ref, g_ref, b_ref, ws_ref, wst_ref, bs_ref,
             dz_ref, dg_ref, db_ref, dws_ref, dbs_ref, dsv_acc):
        step = pl.program_id(0)

        @pl.when(step == 0)
        def _():
            dg_ref[...] = jnp.zeros_like(dg_ref)
            db_ref[...] = jnp.zeros_like(db_ref)
            dws_ref[...] = jnp.zeros_like(dws_ref)
            dsv_acc[...] = jnp.zeros_like(dsv_acc)

        zu = z_ref[:, :E].astype(F32)
        zv = z_ref[:, E:].astype(F32)
        u = _gelu(zu)
        v = _gelu(zv)
        mu = jnp.mean(v, axis=1, keepdims=True)
        cen = v - mu
        var = jnp.mean(cen * cen, axis=1, keepdims=True)
        rstd = lax.rsqrt(var + LN_EPS)
        xhat = cen * rstd
        vn = (xhat * g_ref[...] + b_ref[...]).astype(BF16)
        dyv = dy_ref[...].astype(F32)
        dsv = dyv * u
        dsv_acc[...] += dsv
        dsvb = dsv.astype(BF16)
        tril = (lax.broadcasted_iota(jnp.int32, (CHUNK, CHUNK), 0)
                >= lax.broadcasted_iota(jnp.int32, (CHUNK, CHUNK), 1))
        du_parts = []
        dvn_parts = []
        for gi in range(G):
            cols = slice(gi * cg, (gi + 1) * cg)
            sv = jnp.dot(ws_ref[gi], vn[:, cols], preferred_element_type=F32) + bs_ref[:, gi:gi + 1]
            du_parts.append(dyv[:, cols] * sv)
            dvn_parts.append(jnp.dot(wst_ref[gi], dsvb[:, cols], preferred_element_type=F32))
            dw = lax.dot_general(dsvb[:, cols], vn[:, cols], (((1,), (1,)), ((), ())),
                                 preferred_element_type=F32)
            dws_ref[gi] += jnp.where(tril, dw, 0.0)
        du = jnp.concatenate(du_parts, axis=1)
        dvn = jnp.concatenate(dvn_parts, axis=1)
        dg_ref[...] += jnp.sum(dvn * xhat, axis=0, keepdims=True)
        db_ref[...] += jnp.sum(dvn, axis=0, keepdims=True)
        dxh = dvn * g_ref[...]
        m1 = jnp.mean(dxh, axis=1, keepdims=True)
        m2 = jnp.mean(dxh * xhat, axis=1, keepdims=True)
        dv = rstd * (dxh - m1 - xhat * m2)
        dz_ref[:, :E] = (du * _gelu_grad(zu)).astype(BF16)
        dz_ref[:, E:] = (dv * _gelu_grad(zv)).astype(BF16)

        @pl.when(step == nsteps - 1)
        def _():
            lane = lax.broadcasted_iota(jnp.int32, (CHUNK, LANES), 1)
            out = jnp.zeros((CHUNK, LANES), F32)
            for gi in range(G):
                s = jnp.sum(dsv_acc[:, gi * cg:(gi + 1) * cg], axis=1, keepdims=True)
                out = jnp.where(lane == gi, s, out)
            dbs_ref[...] = out

    vecE = pl.BlockSpec((1, E), lambda i: (0, 0))
    wspec = pl.BlockSpec((G, CHUNK, CHUNK), lambda i: (0, 0, 0))
    body, in_specs, args = _ordered(
        body, [pl.BlockSpec((rb, E2), lambda i: (i, 0)), pl.BlockSpec((rb, E), lambda i: (i, 0)),
               vecE, vecE, wspec, wspec, pl.BlockSpec((CHUNK, G), lambda i: (0, 0))],
        (zp, dy, ln_g.reshape(1, E), ln_b.reshape(1, E), ws, wst, bst), after)
    return pl.pallas_call(
        body, grid=(nsteps,), in_specs=in_specs,
        out_specs=[pl.BlockSpec((rb, E2), lambda i: (i, 0)), vecE, vecE, wspec,
                   pl.BlockSpec((CHUNK, LANES), lambda i: (0, 0))],
        out_shape=[jax.ShapeDtypeStruct((T, E2), BF16), jax.ShapeDtypeStruct((1, E), F32),
                   jax.ShapeDtypeStruct((1, E), F32), jax.ShapeDtypeStruct((G, CHUNK, CHUNK), F32),
                   jax.ShapeDtypeStruct((CHUNK, LANES), F32)],
        scratch_shapes=[pltpu.VMEM((CHUNK, E), F32)],
        compiler_params=_params(("arbitrary",)), name=name,
    )(*args)


def _shift_down(x, k, row):
    return jnp.where(row >= k, pltpu.roll(x, k, 0), 0.0)


def _shift_up(x, k, row, T):
    return jnp.where(row < T - k, pltpu.roll(x, T - k, 0), 0.0)


def _conv3(x, w_ref, b_ref, row):
    return (w_ref[0:1, :] * _shift_down(x, 2, row) + w_ref[1:2, :] * _shift_down(x, 1, row)
            + w_ref[2:3, :] * x + b_ref[...])


def _convgate_fwd(hh, cw, cb, *, name):
    T, F2 = hh.shape
    F = F2 // 2
    ns = F // LANES

    def body(a_ref, g_ref, wa_ref, wg_ref, ba_ref, bg_ref, o_ref, ca_ref, cg_ref):
        row = lax.broadcasted_iota(jnp.int32, (T, LANES), 0)
        ca = _conv3(a_ref[...].astype(F32), wa_ref, ba_ref, row)
        cgv = _conv3(g_ref[...].astype(F32), wg_ref, bg_ref, row)
        o_ref[...] = (_gelu(ca) * cgv).astype(BF16)
        ca_ref[...] = ca.astype(ACT)
        cg_ref[...] = cgv.astype(ACT)

    sa = lambda r: pl.BlockSpec((r, LANES), lambda j: (0, j))
    sg = lambda r: pl.BlockSpec((r, LANES), lambda j: (0, j + ns))
    return pl.pallas_call(
        body, grid=(ns,), in_specs=[sa(T), sg(T), sa(3), sg(3), sa(1), sg(1)],
        out_specs=[sa(T)] * 3,
        out_shape=[jax.ShapeDtypeStruct((T, F), BF16), jax.ShapeDtypeStruct((T, F), ACT),
                   jax.ShapeDtypeStruct((T, F), ACT)],
        compiler_params=_params(("parallel",)), name=name,
    )(hh, hh, cw, cw, cb, cb)


def _convgate_bwd(hh, hca, hcg, dact, cw, *, name):
    T, F2 = hh.shape
    F = F2 // 2
    ns = F // LANES

    def body(a_ref, g_ref, ca_ref, cg_ref, d_ref, wa_ref, wg_ref,
             da_ref, dg_ref, dwa_ref, dwg_ref, dba_ref, dbg_ref):
        row = lax.broadcasted_iota(jnp.int32, (T, LANES), 0)
        d = d_ref[...].astype(F32)
        ca = ca_ref[...].astype(F32)
        cgv = cg_ref[...].astype(F32)
        cdf = 0.5 * (1.0 + lax.erf(ca * INV_SQRT2))
        dca = d * cgv * (cdf + ca * jnp.exp(-0.5 * ca * ca) * INV_SQRT_2PI)
        dcg = d * (ca * cdf)
        for x_ref, w_ref, dc, dx_ref, dw_ref, db_ref in (
                (a_ref, wa_ref, dca, da_ref, dwa_ref, dba_ref),
                (g_ref, wg_ref, dcg, dg_ref, dwg_ref, dbg_ref)):
            x = x_ref[...].astype(F32)
            up1, up2 = _shift_up(dc, 1, row, T), _shift_up(dc, 2, row, T)
            dx_ref[...] = (w_ref[2:3, :] * dc + w_ref[1:2, :] * up1 + w_ref[0:1, :] * up2).astype(BF16)
            dw_ref[0:1, :] = jnp.sum(up2 * x, axis=0, keepdims=True)
            dw_ref[1:2, :] = jnp.sum(up1 * x, axis=0, keepdims=True)
            dw_ref[2:3, :] = jnp.sum(dc * x, axis=0, keepdims=True)
            db_ref[...] = jnp.sum(dc, axis=0, keepdims=True)

    sa = lambda r: pl.BlockSpec((r, LANES), lambda j: (0, j))
    sg = lambda r: pl.BlockSpec((r, LANES), lambda j: (0, j + ns))
    return pl.pallas_call(
        body, grid=(ns,), in_specs=[sa(T), sg(T), sa(T), sa(T), sa(T), sa(3), sg(3)],
        out_specs=[sa(T), sa(T), sa(3), sa(3), sa(1), sa(1)],
        out_shape=[jax.ShapeDtypeStruct((T, F), BF16), jax.ShapeDtypeStruct((T, F), BF16),
                   jax.ShapeDtypeStruct((3, F), F32), jax.ShapeDtypeStruct((3, F), F32),
                   jax.ShapeDtypeStruct((1, F), F32), jax.ShapeDtypeStruct((1, F), F32)],
        compiler_params=_params(("parallel",)), name=name,
    )(hh, hh, hca, hcg, dact, cw, cw)


def _bucket_maps():
    iq = np.arange(BLK)[:, None]
    ik = np.arange(2 * BLK)[None, :]
    delta = iq + BLK - ik
    maps = []
    for win, dil in DILATED_GROUPS:
        n = np.clip(delta, 0, None) * dil
        max_exact = REL_BUCKETS // 2
        nf = np.maximum(n, 1).astype(np.float32)
        large = max_exact + (np.log(nf / np.float32(max_exact)) / np.float32(math.log(REL_MAX_DIST / max_exact))
                             * np.float32(REL_BUCKETS - max_exact)).astype(np.int32)
        large = np.minimum(large, REL_BUCKETS - 1)
        bucket = np.where(n < max_exact, n, large)
        valid = (delta >= 0) & (delta <= win // dil)
        maps.append(np.where(valid, bucket, -1).astype(np.int32))
    return np.stack(maps)


def _band_bias(rel_table, bmap, H, *, name):
    def body(t_ref, m_ref, o_ref):
        g = pl.program_id(0)
        bm = m_ref[0]
        for h in range(H):
            acc = jnp.full((BLK, 2 * BLK), NEG, F32)
            for b in range(REL_BUCKETS):
                acc = jnp.where(bm == b, t_ref[b, g * H + h], acc)
            o_ref[0, h] = acc

    return pl.pallas_call(
        body, grid=(N_GROUPS,),
        in_specs=[pl.BlockSpec(memory_space=pltpu.SMEM),
                  pl.BlockSpec((1, BLK, 2 * BLK), lambda g: (g, 0, 0))],
        out_specs=pl.BlockSpec((1, H, BLK, 2 * BLK), lambda g: (g, 0, 0, 0)),
        out_shape=jax.ShapeDtypeStruct((N_GROUPS, H, BLK, 2 * BLK), F32),
        compiler_params=_params(("parallel",)), name=name,
    )(rel_table, bmap)


def _band_bias_bwd(dbias, bmap, H, *, name):
    def body(d_ref, m_ref, o_ref):
        bm = m_ref[0]
        rowi = lax.broadcasted_iota(jnp.int32, (REL_BUCKETS, LANES), 0)
        lane = lax.broadcasted_iota(jnp.int32, (REL_BUCKETS, LANES), 1)
        out = jnp.zeros((REL_BUCKETS, LANES), F32)
        for h in range(H):
            dv = d_ref[0, h]
            for b in range(REL_BUCKETS):
                s = jnp.sum(jnp.sum(jnp.where(bm == b, dv, 0.0), axis=1, keepdims=True),
                            axis=0, keepdims=True)
                out = jnp.where((rowi == b) & (lane == h), s, out)
        o_ref[0] = out

    return pl.pallas_call(
        body, grid=(N_GROUPS,),
        in_specs=[pl.BlockSpec((1, H, BLK, 2 * BLK), lambda g: (g, 0, 0, 0)),
                  pl.BlockSpec((1, BLK, 2 * BLK), lambda g: (g, 0, 0))],
        out_specs=pl.BlockSpec((1, REL_BUCKETS, LANES), lambda g: (g, 0, 0)),
        out_shape=jax.ShapeDtypeStruct((N_GROUPS, REL_BUCKETS, LANES), F32),
        compiler_params=_params(("parallel",)), name=name,
    )(dbias, bmap)


def _head_masks():
    lane = lax.broadcasted_iota(jnp.int32, (BLK, LANES), 1)
    return (lane < HEAD_DIM, lane >= HEAD_DIM)


def _attn_fwd(q, kv, bias, gi, *, name):
    T = q.shape[0]
    HD = kv.shape[1] // 2
    d = DILATED_GROUPS[gi][1]
    S = T // d
    NB = S // BLK
    H = HD // HEAD_DIM
    qv, qcol = (q, gi) if d == 1 else (q[:, gi * HD:(gi + 1) * HD].reshape(S, d * HD), 0)
    kvv = kv.reshape(S, d * 2 * HD)

    def body(q_ref, kp_ref, kc_ref, vp_ref, vc_ref, b_ref, o_ref, l_ref):
        n = pl.program_id(1)
        col = lax.broadcasted_iota(jnp.int32, (BLK, 2 * BLK), 1)
        first = (n == 0) & (col < BLK)
        hm = _head_masks()
        for p in range(HD // LANES):
            sl = slice(p * LANES, (p + 1) * LANES)
            qp = q_ref[:, sl]
            kc = jnp.concatenate([kp_ref[:, sl], kc_ref[:, sl]], axis=0)
            vc = jnp.concatenate([vp_ref[:, sl], vc_ref[:, sl]], axis=0)
            outs = []
            lses = []
            for hh in range(2):
                qm = jnp.where(hm[hh], qp, jnp.zeros_like(qp))
                s = lax.dot_general(qm, kc, (((1,), (1,)), ((), ())), preferred_element_type=F32)
                s = jnp.where(first, NEG, s + b_ref[2 * p + hh])
                m = jnp.max(s, axis=1, keepdims=True)
                e = jnp.exp(s - m)
                den = jnp.sum(e, axis=1, keepdims=True)
                outs.append(jnp.dot((e / den).astype(BF16), vc, preferred_element_type=F32))
                lses.append(m + jnp.log(den))
            o_ref[:, sl] = jnp.where(hm[0], outs[0], outs[1])
            l_ref[:, sl] = jnp.where(hm[0], lses[0], lses[1])

    blk = lambda f: pl.BlockSpec((BLK, HD), f)
    prev = lambda n: jnp.maximum(n - 1, 0)
    return pl.pallas_call(
        body, grid=(d, NB),
        in_specs=[blk(lambda r, n: (n, r + qcol)),
                  blk(lambda r, n: (prev(n), r * 2)), blk(lambda r, n: (n, r * 2)),
                  blk(lambda r, n: (prev(n), r * 2 + 1)), blk(lambda r, n: (n, r * 2 + 1)),
                  pl.BlockSpec((H, BLK, 2 * BLK), lambda r, n: (0, 0, 0))],
        out_specs=[blk(lambda r, n: (n, r)), blk(lambda r, n: (n, r))],
        out_shape=[jax.ShapeDtypeStruct((S, d * HD), F32), jax.ShapeDtypeStruct((S, d * HD), F32)],
        compiler_params=_params(("parallel", "parallel")), name=name,
    )(qv, kvv, kvv, kvv, kvv, bias)


def _attn_combine(os, ls, *, name):
    T, HD = os[0].shape
    rb = _pick(T, 512)

    def body(o0, o1, o2, l0, l1, l2, o_ref, ob_ref, l_ref):
        la, lb, lc = l0[...], l1[...], l2[...]
        m = jnp.maximum(jnp.maximum(la, lb), lc)
        L = m + jnp.log(jnp.exp(la - m) + jnp.exp(lb - m) + jnp.exp(lc - m))
        o = jnp.exp(la - L) * o0[...] + jnp.exp(lb - L) * o1[...] + jnp.exp(lc - L) * o2[...]
        o_ref[...] = o
        ob_ref[...] = o.astype(BF16)
        l_ref[...] = L

    row = pl.BlockSpec((rb, HD), lambda i: (i, 0))
    return pl.pallas_call(
        body, grid=(T // rb,), in_specs=[row] * 6, out_specs=[row] * 3,
        out_shape=[jax.ShapeDtypeStruct((T, HD), F32), jax.ShapeDtypeStruct((T, HD), BF16),
                   jax.ShapeDtypeStruct((T, HD), F32)],
        compiler_params=_params(("parallel",)), name=name,
    )(*[a.reshape(T, HD) for a in os], *[a.reshape(T, HD) for a in ls])


def _attn_bwd(q, kv, bias, do, o, L, gi, *, name):
    T = q.shape[0]
    HD = kv.shape[1] // 2
    d = DILATED_GROUPS[gi][1]
    S = T // d
    NB = S // BLK
    H = HD // HEAD_DIM
    qv, qcol = (q, gi) if d == 1 else (q[:, gi * HD:(gi + 1) * HD].reshape(S, d * HD), 0)
    kvv = kv.reshape(S, d * 2 * HD)
    dov, ov, Lv = (a.reshape(S, d * HD) for a in (do, o, L))

    def body(q_ref, kp_ref, kc_ref, vp_ref, vc_ref, b_ref, do_ref, o_ref, L_ref,
             dq_ref, dk_ref, dv_ref, db_ref, ck_ref, cv_ref):
        r = pl.program_id(0)
        n = pl.program_id(1)

        @pl.when((r == 0) & (n == 0))
        def _():
            db_ref[...] = jnp.zeros_like(db_ref)

        @pl.when(n == 0)
        def _():
            ck_ref[...] = jnp.zeros_like(ck_ref)
            cv_ref[...] = jnp.zeros_like(cv_ref)

        @pl.when(n < NB)
        def _():
            col = lax.broadcasted_iota(jnp.int32, (BLK, 2 * BLK), 1)
            first = (n == 0) & (col < BLK)
            hm = _head_masks()
            for p in range(HD // LANES):
                sl = slice(p * LANES, (p + 1) * LANES)
                qp = q_ref[:, sl]
                kc = jnp.concatenate([kp_ref[:, sl], kc_ref[:, sl]], axis=0)
                vc = jnp.concatenate([vp_ref[:, sl], vc_ref[:, sl]], axis=0)
                dop = do_ref[:, sl]
                dob = dop.astype(BF16)
                prod = dop * o_ref[:, sl]
                Lp = L_ref[:, sl]
                dq_parts = []
                dkc = None
                dvc = None
                for hh in range(2):
                    qm = jnp.where(hm[hh], qp, jnp.zeros_like(qp))
                    dom = jnp.where(hm[hh], dob, jnp.zeros_like(dob))
                    s = lax.dot_general(qm, kc, (((1,), (1,)), ((), ())), preferred_element_type=F32)
                    s = jnp.where(first, NEG, s + b_ref[2 * p + hh])
                    lse = Lp[:, hh * HEAD_DIM:hh * HEAD_DIM + 1]
                    pr = jnp.exp(s - lse)
                    dp = lax.dot_general(dom, vc, (((1,), (1,)), ((), ())), preferred_element_type=F32)
                    delta = jnp.sum(jnp.where(hm[hh], prod, 0.0), axis=1, keepdims=True)
                    ds = pr * (dp - delta)
                    db_ref[2 * p + hh] += ds
                    dsb = ds.astype(BF16)
                    dq_parts.append(jnp.dot(dsb, kc, preferred_element_type=F32))
                    dkh = lax.dot_general(dsb, qm, (((0,), (0,)), ((), ())), preferred_element_type=F32)
                    dvh = lax.dot_general(pr.astype(BF16), dom, (((0,), (0,)), ((), ())),
                                          preferred_element_type=F32)
                    dkc = dkh if dkc is None else dkc + dkh
                    dvc = dvh if dvc is None else dvc + dvh
                dq = jnp.where(hm[0], dq_parts[0], dq_parts[1])
                dq_ref[:, sl] = (dq * (HEAD_DIM ** -0.5)).astype(BF16)
                dk_ref[:, sl] = ck_ref[:, sl] + dkc[:BLK]
                dv_ref[:, sl] = cv_ref[:, sl] + dvc[:BLK]
                ck_ref[:, sl] = dkc[BLK:]
                cv_ref[:, sl] = dvc[BLK:]

        @pl.when(n == NB)
        def _():
            dk_ref[...] = ck_ref[...]
            dv_ref[...] = cv_ref[...]

    blk = lambda f: pl.BlockSpec((BLK, HD), f)
    cur = lambda n: jnp.minimum(n, NB - 1)
    prev = lambda n: jnp.maximum(jnp.minimum(n, NB - 1) - 1, 0)
    lag = lambda n: jnp.maximum(n - 1, 0)
    return pl.pallas_call(
        body, grid=(d, NB + 1),
        in_specs=[blk(lambda r, n: (cur(n), r + qcol)),
                  blk(lambda r, n: (prev(n), r * 2)), blk(lambda r, n: (cur(n), r * 2)),
                  blk(lambda r, n: (prev(n), r * 2 + 1)), blk(lambda r, n: (cur(n), r * 2 + 1)),
                  pl.BlockSpec((H, BLK, 2 * BLK), lambda r, n: (0, 0, 0)),
                  blk(lambda r, n: (cur(n), r)), blk(lambda r, n: (cur(n), r)),
                  blk(lambda r, n: (cur(n), r))],
        out_specs=[blk(lambda r, n: (cur(n), r)), blk(lambda r, n: (lag(n), r)),
                   blk(lambda r, n: (lag(n), r)),
                   pl.BlockSpec((H, BLK, 2 * BLK), lambda r, n: (0, 0, 0))],
        out_shape=[jax.ShapeDtypeStruct((S, d * HD), BF16), jax.ShapeDtypeStruct((S, d * HD), F32),
                   jax.ShapeDtypeStruct((S, d * HD), F32),
                   jax.ShapeDtypeStruct((H, BLK, 2 * BLK), F32)],
        scratch_shapes=[pltpu.VMEM((BLK, HD), F32), pltpu.VMEM((BLK, HD), F32)],
        compiler_params=_params(("arbitrary", "arbitrary")), name=name,
    )(qv, kvv, kvv, kvv, kvv, bias, dov, ov, Lv)


SUPER = DILATED_GROUPS[-1][1] * BLK


def _band_rows(it, d):
    r, j = it % d, it // d
    if d == 1:
        at = lambda blk: pl.ds(pl.multiple_of(blk * BLK, BLK), BLK)
    else:
        at = lambda blk: pl.ds(r + d * BLK * blk, BLK, stride=d)
    return at(j), at(jnp.maximum(j - 1, 0))


def _stack_heads(x, hm):
    zero = jnp.zeros_like(x)
    return jnp.concatenate([jnp.where(hm[0], x, zero), jnp.where(hm[1], x, zero)], axis=0)


def _band_loops(step, d, unroll):
    n_it = SUPER // BLK

    def run(lo, hi, inside):
        if hi > lo:
            def body(it, carry):
                step(it, inside)
                return carry
            lax.fori_loop(lo, hi, body, 0, unroll=max(u for u in range(1, unroll + 1) if (hi - lo) % u == 0))

    run(0, d, False)
    run(d, n_it, True)


def _last_rows(it, d):
    m = SUPER // (d * BLK)
    if d == 1:
        return pl.ds((m - 1) * BLK, BLK)
    return pl.ds(it % d + d * BLK * (m - 1), BLK, stride=d)


def _attn_fwd_all(q, kv, bias, *, name):
    T = q.shape[0]
    HD = kv.shape[1] // 2
    PP = HD // LANES
    NS = T // SUPER

    def body(q0, q1, q2, kp_ref, kc_ref, vp_ref, vc_ref, b_ref, o_ref, ob_ref, l_ref, og, lg):
        n = pl.program_id(1)
        col = lax.broadcasted_iota(jnp.int32, (2 * BLK, 2 * BLK), 1)
        hm = _head_masks()
        for g, (q_ref, (_, d)) in enumerate(zip((q0, q1, q2), DILATED_GROUPS)):
            def step(it, inside, g=g, q_ref=q_ref, d=d):
                cur, prv = _band_rows(it, d)
                qp = q_ref[cur, :].astype(BF16)
                if inside:
                    kprev, vprev = kc_ref[prv, :], vc_ref[prv, :]
                else:
                    last = _last_rows(it, d)
                    kprev, vprev = kp_ref[last, :], vp_ref[last, :]
                kc = jnp.concatenate([kprev.astype(BF16), kc_ref[cur, :].astype(BF16)], axis=0)
                vc = jnp.concatenate([vprev.astype(BF16), vc_ref[cur, :].astype(BF16)], axis=0)
                s = lax.dot_general(_stack_heads(qp, hm), kc, (((1,), (1,)), ((), ())),
                                    preferred_element_type=F32)
                s = s + b_ref[g].reshape(2 * BLK, 2 * BLK)
                if not inside:
                    s = jnp.where((n == 0) & (col < BLK), NEG, s)
                mx = jnp.max(s, axis=1, keepdims=True)
                e = jnp.exp(s - mx)
                den = jnp.sum(e, axis=1, keepdims=True)
                out = jnp.dot((e / den).astype(BF16), vc, preferred_element_type=F32)
                lse = mx + jnp.log(den)
                og.at[g][cur, :] = jnp.where(hm[0], out[:BLK], out[BLK:])
                lg.at[g][cur, :] = jnp.where(hm[0], lse[:BLK], lse[BLK:])

            _band_loops(step, d, 8)
        la, lb, lc = lg[0], lg[1], lg[2]
        mx = jnp.maximum(jnp.maximum(la, lb), lc)
        L = mx + jnp.log(jnp.exp(la - mx) + jnp.exp(lb - mx) + jnp.exp(lc - mx))
        o = jnp.exp(la - L) * og[0] + jnp.exp(lb - L) * og[1] + jnp.exp(lc - L) * og[2]
        o_ref[...] = o
        ob_ref[...] = o.astype(BF16)
        l_ref[...] = L

    blk = lambda f: pl.BlockSpec((SUPER, LANES), f)
    prev = lambda n: jnp.maximum(n - 1, 0)
    qspec = lambda g: blk(lambda p, n: (n, g * PP + p))
    return pl.pallas_call(
        body, grid=(PP, NS),
        in_specs=[qspec(0), qspec(1), qspec(2),
                  blk(lambda p, n: (prev(n), p)), blk(lambda p, n: (n, p)),
                  blk(lambda p, n: (prev(n), PP + p)), blk(lambda p, n: (n, PP + p)),
                  pl.BlockSpec((N_GROUPS, 2, BLK, 2 * BLK), lambda p, n: (0, p, 0, 0))],
        out_specs=[blk(lambda p, n: (n, p))] * 3,
        out_shape=[jax.ShapeDtypeStruct((T, HD), F32), jax.ShapeDtypeStruct((T, HD), BF16),
                   jax.ShapeDtypeStruct((T, HD), F32)],
        scratch_shapes=[pltpu.VMEM((N_GROUPS, SUPER, LANES), F32), pltpu.VMEM((N_GROUPS, SUPER, LANES), F32)],
        compiler_params=_params(("parallel", "parallel")), name=name,
    )(q, q, q, kv, kv, kv, kv, bias)


def _attn_bwd_all(q, kv, bias, do, o, L, *, after=None, name):
    T = q.shape[0]
    HD = kv.shape[1] // 2
    PP = HD // LANES
    H = HD // HEAD_DIM
    NS = T // SUPER

    def body(q0, q1, q2, kp_ref, kc_ref, vp_ref, vc_ref, b_ref, do_ref, o_ref, L_ref,
             dq_ref, dk_ref, dv_ref, db_ref, ck_ref, cv_ref):
        n = pl.program_id(1)

        @pl.when(n == 0)
        def _():
            db_ref[...] = jnp.zeros_like(db_ref)
            ck_ref[...] = jnp.zeros_like(ck_ref)
            cv_ref[...] = jnp.zeros_like(cv_ref)

        dk_ref[...] = ck_ref[...]
        dv_ref[...] = cv_ref[...]
        ck_ref[...] = jnp.zeros_like(ck_ref)
        cv_ref[...] = jnp.zeros_like(cv_ref)

        @pl.when(n < NS)
        def _():
            col = lax.broadcasted_iota(jnp.int32, (2 * BLK, 2 * BLK), 1)
            hm = _head_masks()
            for g, (q_ref, (_, d)) in enumerate(zip((q0, q1, q2), DILATED_GROUPS)):
                def step(it, inside, g=g, q_ref=q_ref, d=d):
                    cur, prv = _band_rows(it, d)
                    last = _last_rows(it, d)
                    qp = q_ref[cur, :].astype(BF16)
                    if inside:
                        kprev, vprev = kc_ref[prv, :], vc_ref[prv, :]
                    else:
                        kprev, vprev = kp_ref[last, :], vp_ref[last, :]
                    kc = jnp.concatenate([kprev.astype(BF16), kc_ref[cur, :].astype(BF16)], axis=0)
                    vc = jnp.concatenate([vprev.astype(BF16), vc_ref[cur, :].astype(BF16)], axis=0)
                    dop = do_ref[cur, :]
                    prod = dop * o_ref[cur, :]
                    Lp = L_ref[cur, :]
                    qs = _stack_heads(qp, hm)
                    dos = _stack_heads(dop.astype(BF16), hm)
                    lse = jnp.concatenate([Lp[:, 0:1], Lp[:, HEAD_DIM:HEAD_DIM + 1]], axis=0)
                    delta = jnp.concatenate([jnp.sum(jnp.where(hm[0], prod, 0.0), axis=1, keepdims=True),
                                             jnp.sum(jnp.where(hm[1], prod, 0.0), axis=1, keepdims=True)], axis=0)
                    s = lax.dot_general(qs, kc, (((1,), (1,)), ((), ())), preferred_element_type=F32)
                    s = s + b_ref[g].reshape(2 * BLK, 2 * BLK)
                    if not inside:
                        s = jnp.where((n == 0) & (col < BLK), NEG, s)
                    pr = jnp.exp(s - lse)
                    dp = lax.dot_general(dos, vc, (((1,), (1,)), ((), ())), preferred_element_type=F32)
                    ds = pr * (dp - delta)
                    db_ref[g] += ds.reshape(2, BLK, 2 * BLK)
                    dsb = ds.astype(BF16)
                    dqs = jnp.dot(dsb, kc, preferred_element_type=F32)
                    dkc = lax.dot_general(dsb, qs, (((0,), (0,)), ((), ())), preferred_element_type=F32)
                    dvc = lax.dot_general(pr.astype(BF16), dos, (((0,), (0,)), ((), ())),
                                          preferred_element_type=F32)
                    dq_ref.at[g][cur, :] = jnp.where(hm[0], dqs[:BLK], dqs[BLK:]) * (HEAD_DIM ** -0.5)
                    ck_ref[cur, :] += dkc[BLK:]
                    cv_ref[cur, :] += dvc[BLK:]
                    if inside:
                        ck_ref[prv, :] += dkc[:BLK]
                        cv_ref[prv, :] += dvc[:BLK]
                    else:
                        dk_ref[last, :] += dkc[:BLK]
                        dv_ref[last, :] += dvc[:BLK]

                _band_loops(step, d, 4)

    blk = lambda f: pl.BlockSpec((SUPER, LANES), f)
    cur = lambda n: jnp.minimum(n, NS - 1)
    prev = lambda n: jnp.maximum(jnp.minimum(n, NS - 1) - 1, 0)
    lag = lambda n: jnp.maximum(n - 1, 0)
    qspec = lambda g: blk(lambda p, n: (cur(n), g * PP + p))
    bspec = pl.BlockSpec((N_GROUPS, 2, BLK, 2 * BLK), lambda p, n: (0, p, 0, 0))
    body, in_specs, args = _ordered(
        body, [qspec(0), qspec(1), qspec(2),
               blk(lambda p, n: (prev(n), p)), blk(lambda p, n: (cur(n), p)),
               blk(lambda p, n: (prev(n), PP + p)), blk(lambda p, n: (cur(n), PP + p)),
               bspec, blk(lambda p, n: (cur(n), p)), blk(lambda p, n: (cur(n), p)),
               blk(lambda p, n: (cur(n), p))],
        (q, q, q, kv, kv, kv, kv, bias, do, o, L), after)
    return pl.pallas_call(
        body, grid=(PP, NS + 1), in_specs=in_specs,
        out_specs=[pl.BlockSpec((N_GROUPS, SUPER, LANES), lambda p, n: (0, cur(n), p)),
                   blk(lambda p, n: (lag(n), p)), blk(lambda p, n: (lag(n), p)), bspec],
        out_shape=[jax.ShapeDtypeStruct((N_GROUPS, T, HD), F32), jax.ShapeDtypeStruct((T, HD), F32),
                   jax.ShapeDtypeStruct((T, HD), F32),
                   jax.ShapeDtypeStruct((N_GROUPS, H, BLK, 2 * BLK), F32)],
        scratch_shapes=[pltpu.VMEM((SUPER, LANES), F32), pltpu.VMEM((SUPER, LANES), F32)],
        compiler_params=_params(("arbitrary", "arbitrary")), name=name,
    )(*args)


class _Weights(dict):
    def __init__(self, base, fetch=None, emit=None, emit_small=None):
        super().__init__(base)
        self._fetch, self._emit, self._emit_small = fetch, emit, emit_small

    def fetch(self, group, after):
        if self._fetch is not None:
            for (key, layer), mat in self._fetch(group, after).items():
                self[key][layer] = mat

    def emit(self, group, mats):
        return None if self._emit is None else self._emit(group, mats)

    def emit_small(self, grads):
        return None if self._emit_small is None else self._emit_small(grads)


def _local_step(x, tgt, W):
    T, D = x.shape
    H = W["rel_table"].shape[1] // N_GROUPS
    HD = H * HEAD_DIM
    G = W["a_w_s"].shape[1]
    assert T % (DILATED_GROUPS[-1][1] * BLK) == 0

    tril = jnp.tril(jnp.ones((CHUNK, CHUNK), F32))
    bmap = jnp.asarray(_bucket_maps())
    bias = _band_bias(W["rel_table"], bmap, H, name="band_bias")

    saved = []
    xc, xcb = x, x.astype(BF16)
    kvb = None
    for i in range(DEPTH):
        s = {"x": xc, "xb": xcb}
        W.fetch(4 * i, xc)
        if i < N_A:
            ws_m = W["a_w_s"][i] * tril
            s["ws"] = ws_m.astype(BF16)
            s["wst"] = jnp.swapaxes(ws_m, 1, 2).astype(BF16)
            s["bst"] = W["a_b_s"][i].T
            s["zp"] = _mm(xcb, W["a_w_in"][i], out_dtype=ACT, name=f"a_in_{i}")
            s["y"] = _sgu_fwd(s["zp"], W["a_ln_g"][i], W["a_ln_b"][i], s["ws"], s["bst"], name=f"sgu_fwd_{i}")
            W.fetch(4 * i + 1, s["zp"])
            s["h"] = _mm(s["y"], W["a_w_out"][i], out_dtype=ACT, name=f"a_out_{i}")
        else:
            j = i - N_A
            if kvb is None:
                kvb = _mm(xcb, W["kv_w"][0], name="kv_proj")
            s["q"] = _mm(xcb, W["b_w_q_t"][j], tb=True, scale=HEAD_DIM ** -0.5, name=f"q_proj_{j}")
            s["o"], s["ob"], s["L"] = _attn_fwd_all(s["q"], kvb, bias, name=f"attn_fwd_{j}")
            W.fetch(4 * i + 1, s["q"])
            s["h"] = _mm(s["ob"], W["b_w_o"][j], out_dtype=ACT, name=f"o_proj_{j}")
        s["x1"], s["x1b"] = _add_ln_fwd(xc, s["h"], W["ln_g"][i, 0], W["ln_b"][i, 0], name=f"ln1_fwd_{i}")
        W.fetch(4 * i + 2, s["x1"])
        s["hh"] = _mm(s["x1b"], W["ffn_w_up_t"][i], tb=True, out_dtype=ACT, name=f"ffn_up_{i}")
        s["cw"] = W["ffn_conv_w"][i]
        s["cb"] = W["ffn_conv_b"][i].reshape(1, -1)
        s["act"], s["hca"], s["hcg"] = _convgate_fwd(s["hh"], s["cw"], s["cb"], name=f"convgate_fwd_{i}")
        W.fetch(4 * i + 3, s["hh"])
        s["f"] = _mm(s["act"], W["ffn_w_down"][i], out_dtype=ACT, name=f"ffn_down_{i}")
        xc, xcb = _add_ln_fwd(s["x1"], s["f"], W["ln_g"][i, 1], W["ln_b"][i, 1], name=f"ln2_fwd_{i}")
        saved.append(s)

    dy, lossv = _loss_grad(xc, tgt, name="loss_grad")
    loss = lossv[0, 0]

    gl = {k: [None] * DEPTH for k in ("ffn_w_up_t", "ffn_conv_w", "ffn_conv_b", "ffn_w_down", "ln_g", "ln_b")}
    ga = {k: [None] * N_A for k in ("a_w_in", "a_ln_g", "a_ln_b", "a_w_s", "a_b_s", "a_w_out")}
    gb = {k: [None] * (DEPTH - N_A) for k in ("b_w_q_t", "b_w_o")}
    mats = ("a_w_in", "a_w_out", "b_w_q_t", "b_w_o", "ffn_w_up_t", "ffn_w_down")
    dks, dvs, dbias = [], [], []
    grads = {}
    terms = [(1.0, dy)]
    tok = None
    small_keys = ("ffn_conv_w", "ffn_conv_b", "ln_g", "ln_b", "a_ln_g", "a_ln_b", "a_w_s", "a_b_s")
    for i in reversed(range(DEPTH)):
        s = saved[i]
        dp2, dp2b, dg2, db2 = _add_ln_bwd(s["x1"], s["f"], W["ln_g"][i, 1], terms, after=tok, name=f"ln2_bwd_{i}")
        dact = _mm(dp2b, W["ffn_w_down"][i], tb=True, out_dtype=ACT, name=f"ffn_down_dx_{i}")
        gl["ffn_w_down"][i] = _mm(s["act"], dp2b, ta=True, out_dtype=BF16, name=f"ffn_down_dw_{i}")
        dha, dhg, dwa, dwg, dba, dbg = _convgate_bwd(s["hh"], s["hca"], s["hcg"], dact, s["cw"],
                                                     name=f"convgate_bwd_{i}")
        dhh = (dha, dhg)
        gl["ffn_conv_w"][i] = jnp.concatenate([dwa, dwg], axis=1)
        gl["ffn_conv_b"][i] = jnp.concatenate([dba, dbg], axis=1)[0]
        dx1 = _mm(dhh, W["ffn_w_up_t"][i], out_dtype=ACT, name=f"ffn_up_dx_{i}")
        gl["ffn_w_up_t"][i] = _mm(dhh, s["x1b"], ta=True, out_dtype=BF16, name=f"ffn_up_dw_{i}")
        tok = W.emit(3 * i + 2, {("ffn_w_up_t", i): gl["ffn_w_up_t"][i], ("ffn_w_down", i): gl["ffn_w_down"][i]})
        dp1, dp1b, dg1, db1 = _add_ln_bwd(s["x"], s["h"], W["ln_g"][i, 0], [(ALPHA, dp2), (1.0, dx1)],
                                          after=tok, name=f"ln1_bwd_{i}")
        gl["ln_g"][i] = jnp.concatenate([dg1, dg2], axis=0)
        gl["ln_b"][i] = jnp.concatenate([db1, db2], axis=0)
        terms = [(ALPHA, dp1)]
        if i < N_A:
            dyy = _mm(dp1b, W["a_w_out"][i], tb=True, out_dtype=ACT, name=f"a_out_dx_{i}")
            ga["a_w_out"][i] = _mm(s["y"], dp1b, ta=True, out_dtype=BF16, name=f"a_out_dw_{i}")
            tok = W.emit(3 * i + 1, {("a_w_out", i): ga["a_w_out"][i]})
            dzp, dlg, dlb, dws, dbs = _sgu_bwd(s["zp"], dyy, W["a_ln_g"][i], W["a_ln_b"][i], s["ws"],
                                               s["wst"], s["bst"], after=tok, name=f"sgu_bwd_{i}")
            ga["a_ln_g"][i], ga["a_ln_b"][i], ga["a_w_s"][i] = dlg[0], dlb[0], dws
            ga["a_b_s"][i] = dbs[:, :G].T
            if i == 0:
                for dct in (gl, ga):
                    grads.update({k: jnp.stack(v) for k, v in dct.items() if k in small_keys})
                tok = W.emit_small(grads)
            ga["a_w_in"][i] = _mm(s["xb"], dzp, ta=True, out_dtype=BF16, after=tok, name=f"a_in_dw_{i}")
            tok = W.emit(3 * i, {("a_w_in", i): ga["a_w_in"][i]})
            terms.append((1.0, _mm(dzp, W["a_w_in"][i], tb=True, out_dtype=ACT, after=tok, name=f"a_in_dx_{i}")))
        else:
            j = i - N_A
            do = _mm(dp1b, W["b_w_o"][j], tb=True, name=f"o_proj_dx_{j}")
            gb["b_w_o"][j] = _mm(s["ob"], dp1b, ta=True, out_dtype=BF16, name=f"o_proj_dw_{j}")
            tok = W.emit(3 * i + 1, {("b_w_o", j): gb["b_w_o"][j]})
            dq, dk_j, dv_j, db_j = _attn_bwd_all(s["q"], kvb, bias, do, s["o"], s["L"], after=tok,
                                                 name=f"attn_bwd_{j}")
            dks.append((1.0, dk_j))
            dvs.append((1.0, dv_j))
            dbias.append(db_j)
            terms.append((1.0, _mm(dq, W["b_w_q_t"][j], out_dtype=ACT, name=f"q_proj_dx_{j}")))
            gb["b_w_q_t"][j] = _mm(dq, s["xb"], ta=True, out_dtype=BF16, name=f"q_proj_dw_{j}")
            out_b = {("b_w_q_t", j): gb["b_w_q_t"][j]}
            if i == N_A:
                dkv = jnp.concatenate([_lincomb(dks, BF16, name="dk_sum"), _lincomb(dvs, BF16, name="dv_sum")],
                                      axis=1)
                terms.append((1.0, _mm(dkv, W["kv_w"][0], tb=True, out_dtype=ACT, name="kv_proj_dx")))
                grads["kv_w"] = [_mm(s["xb"], dkv, ta=True, out_dtype=BF16, name="kv_proj_dw")]
                out_b[("kv_w", 0)] = grads["kv_w"][0]
                dbt = _lincomb([(1.0, a.reshape(-1, 2 * BLK)) for a in dbias], F32, name="dbias_sum")
                dtab = _band_bias_bwd(dbt.reshape(N_GROUPS, H, BLK, 2 * BLK), bmap, H, name="band_bias_bwd")
                grads["rel_table"] = jnp.transpose(dtab[:, :, :H], (1, 0, 2)).reshape(REL_BUCKETS, N_GROUPS * H)
            tok = W.emit(3 * i, out_b)
    grad_x = _lincomb(terms, F32, name="grad_x")
    for dct in (gl, ga, gb):
        grads.update({k: v for k, v in dct.items() if k in mats})
    return loss, grad_x, grads


def _my_index():
    return 4 * lax.axis_index("x") + 2 * lax.axis_index("y") + lax.axis_index("c")


HBM_SPEC = pl.BlockSpec(memory_space=pltpu.HBM)


def _block(ref, k, n, axis):
    off = pl.multiple_of(k * n, n)
    return ref.at[pl.ds(off, n), :] if axis == 0 else ref.at[:, pl.ds(off, n)]


def _gather_mats(local, axis, *, name):
    L, a, b = local.shape
    n = a if axis == 0 else b
    full = (a * N_DEV, b) if axis == 0 else (a, b * N_DEV)

    def body(x_ref, *rest):
        outs = rest[:L]
        send_sems, recv_sems, local_sems = rest[L:]
        x, y, c = lax.axis_index("x"), lax.axis_index("y"), lax.axis_index("c")
        me, sibling = (x, y, c), (x, y, 1 - c)
        chips = [(1 - x, y), (x, 1 - y), (1 - x, 1 - y)]

        def slot(l, px, py, pc):
            return _block(outs[l], 4 * px + 2 * py + pc, n, axis)

        def copy(l, k, blk, to, src=None):
            return pltpu.make_async_remote_copy(
                src_ref=slot(l, *blk) if src is None else src, dst_ref=slot(l, *blk),
                send_sem=send_sems.at[7 * l + k], recv_sem=recv_sems.at[7 * l + k],
                device_id=to, device_id_type=MESH)

        mine, first, passed = [], [], []
        for l in range(L):
            mine.append(pltpu.make_async_copy(x_ref.at[l], slot(l, *me), local_sems.at[l]))
            mine[-1].start()
            first.append(copy(l, 0, me, sibling, src=x_ref.at[l]))
            first += [copy(l, 1 + j, me, (*chip, c), src=x_ref.at[l]) for j, chip in enumerate(chips)]
        for cp in first:
            cp.start()
        for l in range(L):
            for j, chip in enumerate(chips):
                copy(l, 1 + j, (*chip, c), me).wait_recv()
                passed.append(copy(l, 4 + j, (*chip, c), sibling))
                passed[-1].start()
        for l in range(L):
            copy(l, 0, sibling, me).wait_recv()
            for j, chip in enumerate(chips):
                copy(l, 4 + j, (*chip, 1 - c), me).wait_recv()
        for cp in first + passed:
            cp.wait_send()
        for cp in mine:
            cp.wait()

    return pl.pallas_call(
        body, out_shape=[jax.ShapeDtypeStruct(full, local.dtype)] * L,
        in_specs=[HBM_SPEC], out_specs=[HBM_SPEC] * L,
        scratch_shapes=[pltpu.SemaphoreType.DMA((7 * L,)), pltpu.SemaphoreType.DMA((7 * L,)),
                        pltpu.SemaphoreType.DMA((L,))],
        name=name,
    )(local)


SEM_SPEC = pl.BlockSpec(memory_space=pltpu.SEMAPHORE)
FLOWING = pltpu.SideEffectType.DATAFLOW_SIDE_EFFECTING


def _peers(x, y, c):
    return [(1 - x if k & 4 else x, 1 - y if k & 2 else y, 1 - c if k & 1 else c) for k in range(1, N_DEV)]


def _ends(src_ref, land_ref, peer_index, me, n, axis, gather):
    if gather:
        return src_ref, _block(land_ref, me, n, axis)
    return _block(src_ref, peer_index, n, axis), land_ref.at[me]


def _send_start(groups, gather, *, name):
    flat = [(g, j, mat, axis) for g, items in enumerate(groups) for j, (mat, axis) in enumerate(items)]
    M, G = len(flat), len(groups)
    lands, ns = [], []
    for _, _, mat, axis in flat:
        A, B = mat.shape
        if gather:
            lands.append((A * N_DEV, B) if axis == 0 else (A, B * N_DEV))
            ns.append(A if axis == 0 else B)
        else:
            lands.append((N_DEV, A // N_DEV, B) if axis == 0 else (N_DEV, A, B // N_DEV))
            ns.append(A // N_DEV if axis == 0 else B // N_DEV)

    def body(*refs):
        src_refs, land_refs, sems = refs[:M], refs[M:2 * M], refs[2 * M:2 * M + 3 * G]
        token = refs[-1]
        x, y, c = lax.axis_index("x"), lax.axis_index("y"), lax.axis_index("c")
        me = 4 * x + 2 * y + c
        for i, (g, j, _, axis) in enumerate(flat):
            for k, (px, py, pc) in enumerate(_peers(x, y, c)):
                s, d = _ends(src_refs[i], land_refs[i], 4 * px + 2 * py + pc, me, ns[i], axis, gather)
                pltpu.make_async_remote_copy(
                    src_ref=s, dst_ref=d, send_sem=sems[3 * g].at[7 * j + k], recv_sem=sems[3 * g + 1].at[7 * j + k],
                    device_id=(px, py, pc), device_id_type=MESH).start()
            s, d = _ends(src_refs[i], land_refs[i], me, me, ns[i], axis, gather)
            pltpu.make_async_copy(s, d, sems[3 * g + 2].at[j]).start()
        token[...] = jnp.zeros_like(token)

    sem_shapes = []
    for items in groups:
        sem_shapes += [pltpu.SemaphoreType.DMA((7 * len(items),))] * 2 + [pltpu.SemaphoreType.DMA((len(items),))]
    outs = pl.pallas_call(
        body, name=name,
        out_shape=(*sem_shapes, *[pltpu.HBM(m.shape, m.dtype) for _, _, m, _ in flat],
                   *[pltpu.HBM(shp, m.dtype) for shp, (_, _, m, _) in zip(lands, flat)],
                   jax.ShapeDtypeStruct((8, LANES), F32)),
        in_specs=[HBM_SPEC] * (2 * M),
        out_specs=(*[SEM_SPEC] * (3 * G), *[HBM_SPEC] * (2 * M), pl.BlockSpec(memory_space=pltpu.VMEM)),
        input_output_aliases={i: 3 * G + i for i in range(2 * M)},
        compiler_params=pltpu.CompilerParams(has_side_effects=FLOWING),
    )(*[pltpu.with_memory_space_constraint(m, pltpu.HBM) for _, _, m, _ in flat],
      *[pltpu.with_memory_space_constraint(lax.empty(shp, m.dtype), pltpu.HBM)
        for shp, (_, _, m, _) in zip(lands, flat)])
    handles = []
    for g in range(G):
        idx = [i for i, f in enumerate(flat) if f[0] == g]
        handles.append((outs[3 * g], outs[3 * g + 1], outs[3 * g + 2], [outs[3 * G + i] for i in idx],
                        [outs[3 * G + M + i] for i in idx], [flat[i][3] for i in idx]))
    return handles, outs[-1]


def _send_wait(handle, gather, after, *, name):
    send_sems, recv_sems, local_sems, mats, lands, axes = handle
    n_m = len(mats)
    ns = []
    for mat, land, axis in zip(mats, lands, axes):
        ns.append(mat.shape[axis] if gather else land.shape[1 + axis])

    def body(*refs):
        src_refs, land_refs = refs[:n_m], refs[n_m:2 * n_m]
        ssem, rsem, lsem = refs[2 * n_m:2 * n_m + 3]
        x, y, c = lax.axis_index("x"), lax.axis_index("y"), lax.axis_index("c")
        me = 4 * x + 2 * y + c
        for j in range(n_m):
            for k, (px, py, pc) in enumerate(_peers(x, y, c)):
                s, d = _ends(src_refs[j], land_refs[j], 4 * px + 2 * py + pc, me, ns[j], axes[j], gather)
                cp = pltpu.make_async_remote_copy(
                    src_ref=s, dst_ref=d, send_sem=ssem.at[7 * j + k], recv_sem=rsem.at[7 * j + k],
                    device_id=(px, py, pc), device_id_type=MESH)
                cp.wait_send()
                cp.wait_recv()
            s, d = _ends(src_refs[j], land_refs[j], me, me, ns[j], axes[j], gather)
            pltpu.make_async_copy(s, d, lsem.at[j]).wait()

    outs = pl.pallas_call(
        body, name=name,
        out_shape=(*[pltpu.HBM(m.shape, m.dtype) for m in mats], *[pltpu.HBM(l.shape, l.dtype) for l in lands]),
        in_specs=[HBM_SPEC] * (2 * n_m) + [SEM_SPEC] * 3 + [pl.BlockSpec(memory_space=pl.ANY)],
        out_specs=tuple([HBM_SPEC] * (2 * n_m)),
        input_output_aliases={i: i for i in range(2 * n_m)},
        compiler_params=pltpu.CompilerParams(has_side_effects=FLOWING),
    )(*mats, *lands, send_sems, recv_sems, local_sems, after)
    return list(outs[n_m:])


def _sum_parts(parts, *, name):
    n, R, C = parts.shape
    rb = _pick(R, 512) if R % LANES == 0 else R

    def body(p_ref, o_ref):
        acc = p_ref[0].astype(F32)
        for k in range(1, n):
            acc = acc + p_ref[k].astype(F32)
        o_ref[...] = acc

    return pl.pallas_call(
        body, grid=(R // rb,), in_specs=[pl.BlockSpec((n, rb, C), lambda i: (0, i, 0))],
        out_specs=pl.BlockSpec((rb, C), lambda i: (i, 0)),
        out_shape=jax.ShapeDtypeStruct((R, C), F32),
        compiler_params=_params(("parallel",)), name=name,
    )(parts)


def _adamw(w, m, v, parts, *, name):
    L, R, C = w.shape
    n = parts[0].shape[0]
    cap = max(16, VMEM_LIMIT // 3 // (2 * L * n * C * parts[0].dtype.itemsize))
    rb = max([r for r in range(16, min(R, cap) + 1, 16) if R % r == 0], default=R)

    def body(w_ref, m_ref, v_ref, *rest):
        p_refs = rest[:L]
        g_ref, d_ref, nm_ref, nv_ref = rest[L:]
        for l in range(L):
            @pl.when(pl.program_id(0) == l)
            def _(p_ref=p_refs[l]):
                g = p_ref[0].astype(F32)
                for k in range(1, n):
                    g = g + p_ref[k].astype(F32)
                mn = ADAM_B1 * m_ref[...] + (1.0 - ADAM_B1) * g
                vn = ADAM_B2 * v_ref[...] + (1.0 - ADAM_B2) * jnp.square(g)
                m_hat = mn / (1.0 - ADAM_B1 ** ADAM_STEP)
                v_hat = vn / (1.0 - ADAM_B2 ** ADAM_STEP)
                g_ref[...] = g
                d_ref[...] = -ADAM_LR * (m_hat / (jnp.sqrt(v_hat) + ADAM_EPS) + ADAM_WD * w_ref[...])
                nm_ref[...] = mn
                nv_ref[...] = vn

    row = pl.BlockSpec((None, rb, C), lambda l, i: (l, i, 0))
    part = lambda k: pl.BlockSpec((n, rb, C), lambda l, i: (0, jnp.where(l == k, i, 0), 0))
    return pl.pallas_call(
        body, grid=(L, R // rb), in_specs=[row, row, row] + [part(k) for k in range(L)],
        out_specs=[row] * 4, out_shape=[jax.ShapeDtypeStruct((L, R, C), F32)] * 4,
        compiler_params=_params(("arbitrary", "arbitrary")), name=name,
    )(w, m, v, *parts)


BIG = (("a_w_in", "a_w_in", 1, False), ("a_w_out", "a_w_out", 0, False), ("kv_w", "kv_w", 0, False),
       ("b_w_q", "b_w_q_t", 0, True), ("b_w_o", "b_w_o", 1, False), ("ffn_w_up", "ffn_w_up_t", 0, True),
       ("ffn_w_down", "ffn_w_down", 0, False))
SMALL_SHARDED = (("a_ln_g", 1), ("a_ln_b", 1), ("ffn_conv_w", 2), ("ln_g", 2), ("ln_b", 2))
REPLICATED = ("a_w_s", "a_b_s", "rel_table", "ffn_conv_b")


def _pack_rows(arrs, lead=0):
    lshape = arrs[0].shape[:lead]
    p = jnp.concatenate([a.reshape(*lshape, -1, LANES) for a in arrs], axis=lead)
    pad = -p.shape[lead] % 8
    return jnp.pad(p, [(0, 0)] * lead + [(0, pad), (0, 0)])


def _unpack_rows(packed, shapes, lead=0):
    lshape = packed.shape[:lead]
    out, off = [], 0
    for shp in shapes:
        r = int(np.prod(shp)) // LANES
        out.append(lax.slice_in_dim(packed, off, off + r, axis=lead).reshape(*lshape, *shp))
        off += r
    return out


def _as_mats(a, transposed):
    a = a[None] if a.ndim == 2 else a
    return jnp.swapaxes(a, 1, 2) if transposed else a


def _merge_shards(stacked, axis):
    a = jnp.moveaxis(stacked, 0, axis)
    shp = list(a.shape)
    return a.reshape(shp[:axis] + [shp[axis] * shp[axis + 1]] + shp[axis + 2:])


def _split_shards(full, axis):
    shp = list(full.shape)
    a = full.reshape(shp[:axis] + [N_DEV, shp[axis] // N_DEV] + shp[axis + 1:])
    return jnp.moveaxis(a, axis, 0)


def kernel(x, a_w_in, a_ln_g, a_ln_b, a_w_s, a_b_s, a_w_out, kv_w, b_w_q, b_w_o, rel_table, ffn_w_up, ffn_conv_w, ffn_conv_b, ffn_w_down, ln_g, ln_b, loss_target, m_a_w_in, m_a_ln_g, m_a_ln_b, m_a_w_s, m_a_b_s, m_a_w_out, m_kv_w, m_b_w_q, m_b_w_o, m_rel_table, m_ffn_w_up, m_ffn_conv_w, m_ffn_conv_b, m_ffn_w_down, m_ln_g, m_ln_b, v_a_w_in, v_a_ln_g, v_a_ln_b, v_a_w_s, v_a_b_s, v_a_w_out, v_kv_w, v_b_w_q, v_b_w_o, v_rel_table, v_ffn_w_up, v_ffn_conv_w, v_ffn_conv_b, v_ffn_w_down, v_ln_g, v_ln_b):
    names = ["a_w_in", "a_ln_g", "a_ln_b", "a_w_s", "a_b_s", "a_w_out", "kv_w", "b_w_q", "b_w_o", "rel_table",
             "ffn_w_up", "ffn_conv_w", "ffn_conv_b", "ffn_w_down", "ln_g", "ln_b"]
    w = dict(zip(names, (a_w_in, a_ln_g, a_ln_b, a_w_s, a_b_s, a_w_out, kv_w, b_w_q, b_w_o, rel_table,
                         ffn_w_up, ffn_conv_w, ffn_conv_b, ffn_w_down, ln_g, ln_b)))
    m = dict(zip(names, (m_a_w_in, m_a_ln_g, m_a_ln_b, m_a_w_s, m_a_b_s, m_a_w_out, m_kv_w, m_b_w_q, m_b_w_o,
                         m_rel_table, m_ffn_w_up, m_ffn_conv_w, m_ffn_conv_b, m_ffn_w_down, m_ln_g, m_ln_b)))
    v = dict(zip(names, (v_a_w_in, v_a_ln_g, v_a_ln_b, v_a_w_s, v_a_b_s, v_a_w_out, v_kv_w, v_b_w_q, v_b_w_o,
                         v_rel_table, v_ffn_w_up, v_ffn_conv_w, v_ffn_conv_b, v_ffn_w_down, v_ln_g, v_ln_b)))
    small_names = [n for n, _ in SMALL_SHARDED]
    small_shapes = [w[n].shape for n in small_names]
    rep_shapes = [w[n].shape for n in REPLICATED]

    axis_of = {key: axis for _, key, axis, _ in BIG}
    src = {}
    for n, key, axis, tr in BIG:
        loc = _as_mats(w[n], tr).astype(BF16)
        for l in range(loc.shape[0]):
            src[(key, l)] = loc[l]
    order = []
    for i in range(DEPTH):
        if i < N_A:
            order += [[("a_w_in", i)], [("a_w_out", i)]]
        else:
            order += [([("kv_w", 0)] if i == N_A else []) + [("b_w_q_t", i - N_A)], [("b_w_o", i - N_A)]]
        order += [[("ffn_w_up_t", i)], [("ffn_w_down", i)]]
    small_src = _pack_rows([w[n] for n in small_names])
    srows = small_src.shape[0]
    handles, _ = _send_start([[(small_src, 0)]] + [[(src[kl], axis_of[kl[0]]) for kl in grp] for grp in order],
                             True, name="gather_start")
    small_all = _send_wait(handles[0], True, x, name="gather_wait_small")[0]
    small_st = _unpack_rows(small_all.reshape(N_DEV, srows, LANES), small_shapes, lead=1)
    base = {n: w[n] for n in REPLICATED}
    for (n, ax), st in zip(SMALL_SHARDED, small_st):
        base[n] = _merge_shards(st, ax)
    for n, key, _, tr in BIG:
        base[key] = [None] * (1 if w[n].ndim == 2 else w[n].shape[0])

    def fetch(group, after):
        mats = _send_wait(handles[1 + group], True, after, name=f"gather_wait_{group}")
        return dict(zip(order[group], mats))

    sent = {}

    def emit(group, mats):
        keys = list(mats)
        hs, token = _send_start([[(mats[kl], axis_of[kl[0]]) for kl in keys]], False, name=f"exchange_start_{group}")
        sent[group] = (keys, hs[0])
        return token

    small_sent = []

    def emit_small(grads):
        small_pack = _pack_rows([_split_shards(grads[n], ax) for n, ax in SMALL_SHARDED], lead=1)
        rest = _pack_rows([grads[n] for n in REPLICATED[1:]])
        mine = jnp.concatenate([small_pack.reshape(N_DEV * srows, LANES), rest], axis=0)
        gating = grads[REPLICATED[0]].reshape(-1, LANES).astype(BF16)
        hs, token = _send_start([[(mine, 0), (gating, 0)]], True, name="small_grads_start")
        small_sent.append(hs[0])
        return token

    loss, grad_x, grads = _local_step(x[0], loss_target[0], _Weights(base, fetch, emit, emit_small))
    loss = lax.psum(loss, ("x", "y", "c"))
    out = {}

    landed = {}
    for group in sorted(sent, reverse=True):
        keys, h = sent[group]
        landed.update(zip(keys, _send_wait(h, False, grad_x, name=f"exchange_wait_{group}")))
    last = grad_x
    for n, key, axis, tr in BIG:
        shp = w[n].shape
        parts = [landed[(key, l)] for l in range(1 if len(shp) == 2 else shp[0])]
        res = _adamw(_as_mats(w[n], tr), _as_mats(m[n], tr), _as_mats(v[n], tr), parts, name=f"adamw_{n}")
        out[n] = [(jnp.swapaxes(r, 1, 2) if tr else r).reshape(shp) for r in res]
        last = res[0]

    allp, allg = _send_wait(small_sent[0], True, last, name="small_grads_wait")
    gsum = _sum_parts(allp.reshape(N_DEV, -1, LANES), name="sum_small_grads")
    gating = _sum_parts(allg.reshape(N_DEV, -1, LANES), name="sum_gating_grads")
    g_small = lax.dynamic_slice_in_dim(gsum, _my_index() * srows, srows, axis=0)
    pack_sr = lambda d: jnp.concatenate([_pack_rows([d[n] for n in small_names]),
                                         _pack_rows([d[n] for n in REPLICATED])], axis=0)
    n_rest = sum(int(np.prod(s)) for s in rep_shapes[1:]) // LANES
    gs_in = jnp.concatenate([g_small, gating, gsum[N_DEV * srows:N_DEV * srows + n_rest]], axis=0)
    gs_in = jnp.pad(gs_in, ((0, pack_sr(w).shape[0] - gs_in.shape[0]), (0, 0)))[None]
    res = _adamw(pack_sr(w)[None], pack_sr(m)[None], pack_sr(v)[None], [gs_in], name="adamw_small")
    for n, vals in zip(small_names, zip(*[_unpack_rows(r[0, :srows], small_shapes) for r in res])):
        out[n] = list(vals)
    for n, vals in zip(REPLICATED, zip(*[_unpack_rows(r[0, srows:], rep_shapes) for r in res])):
        out[n] = list(vals)

    return (loss, grad_x[None], *[out[n][0] for n in names], *[out[n][1] for n in names],
            *[out[n][2] for n in names], *[out[n][3] for n in names])
```

```python
import math

import numpy as np
import jax
import jax.numpy as jnp
from jax import lax
from jax.experimental import pallas as pl
from jax.experimental.pallas import tpu as pltpu

F32 = jnp.float32
BF16 = jnp.bfloat16
ACT = jnp.bfloat16
MESH = pl.DeviceIdType.MESH

N_DEV = 8
DEPTH = 4
N_A = 2
CHUNK = 128
BLK = 128
HEAD_DIM = 64
DILATED_GROUPS = ((128, 1), (512, 4), (2048, 16))
N_GROUPS = 3
REL_BUCKETS = 32
REL_MAX_DIST = 2048
ALPHA = (2 * DEPTH) ** 0.25
LN_EPS = 1e-5
NEG = -1e30
ADAM_LR = 0.001
ADAM_B1 = 0.9
ADAM_B2 = 0.999
ADAM_EPS = 1e-08
ADAM_WD = 0.01
ADAM_STEP = 10

LANES = 128
VMEM_LIMIT = 56 * 1024 * 1024
MM_TILE_CAP = 1408
MM_VMEM_BUDGET = 46 * 1024 * 1024
INV_SQRT2 = 1.0 / math.sqrt(2.0)
INV_SQRT_2PI = 1.0 / math.sqrt(2.0 * math.pi)


def _pick(n, cap):
    best = None
    for t in range(LANES, min(n, cap) + 1, LANES):
        if n % t == 0:
            best = t
    return best if best is not None else n


def _params(sem):
    return pltpu.CompilerParams(dimension_semantics=sem, vmem_limit_bytes=VMEM_LIMIT)


def _ordered(body, in_specs, args, after):
    if after is None:
        return body, list(in_specs), tuple(args)
    return (lambda _, *refs: body(*refs)), [pl.BlockSpec(memory_space=pl.ANY), *in_specs], (after, *args)


def _gelu(x):
    return 0.5 * x * (1.0 + lax.erf(x * INV_SQRT2))


def _gelu_grad(x):
    return 0.5 * (1.0 + lax.erf(x * INV_SQRT2)) + x * jnp.exp(-0.5 * x * x) * INV_SQRT_2PI


def _mm(a, b, *, ta=False, tb=False, out_dtype=F32, scale=None, after=None, name):
    halves = isinstance(a, tuple)
    parts = 1 if halves or a.ndim == 2 else a.shape[0]
    ash = (a[0].shape[0], 2 * a[0].shape[1]) if halves else (a.shape if parts == 1
                                                               else (a.shape[1], parts * a.shape[2]))
    if ta:
        K, M = ash
    else:
        M, K = ash
    if tb:
        N, Kb = b.shape
    else:
        Kb, N = b.shape
    assert K == Kb, (ash, b.shape, ta, tb)
    split = 2 if halves else parts
    tm = _pick(M // split if ta else M, MM_TILE_CAP)
    tn = _pick(N, MM_TILE_CAP)
    kspan = K if ta or split == 1 else K // split
    abytes = (a[0] if halves else a).dtype.itemsize * (2 if halves else 1)
    fixed = 2 * tm * tn * jnp.dtype(out_dtype).itemsize + tm * tn * 4
    fits = [t for t in range(LANES, kspan + 1, LANES)
            if kspan % t == 0 and 2 * t * (tm * abytes + tn * b.dtype.itemsize) + fixed <= MM_VMEM_BUDGET]
    tk = max(fits) if fits else _pick(kspan, MM_TILE_CAP)
    nk = K // tk
    nh = (M // split // tm if ta else K // split // tk) if split > 1 else 0
    dn = (((0 if ta else 1,), (1 if tb else 0,)), ((), ()))

    def body(*refs):
        n_tail = 3 if nk > 1 else 2
        a_refs, b_ref, o_ref = refs[:-n_tail], refs[-n_tail], refs[-n_tail + 1]
        k = pl.program_id(2)

        def finish(r):
            if scale is not None:
                r = r * scale
            o_ref[...] = r.astype(out_dtype)

        def accumulate(a_ref):
            part = lax.dot_general(a_ref[...].astype(BF16), b_ref[...].astype(BF16), dn,
                                   preferred_element_type=F32)
            if nk == 1:
                finish(part)
                return
            acc_ref = refs[-1]

            @pl.when(k == 0)
            def _():
                acc_ref[...] = part

            @pl.when((k > 0) & (k < nk - 1))
            def _():
                acc_ref[...] += part

            @pl.when(k == nk - 1)
            def _():
                finish(acc_ref[...] + part)

        if halves:
            first = (pl.program_id(0) if ta else k) < nh
            pl.when(first)(lambda: accumulate(a_refs[0]))
            pl.when(jnp.logical_not(first))(lambda: accumulate(a_refs[1]))
        else:
            accumulate(a_refs[0])

    if halves and ta:
        a_specs = [pl.BlockSpec((tk, tm), lambda i, j, k: (jnp.where(i < nh, k, 0), jnp.minimum(i, nh - 1))),
                   pl.BlockSpec((tk, tm), lambda i, j, k: (jnp.where(i >= nh, k, 0), jnp.maximum(i - nh, 0)))]
    elif halves:
        a_specs = [pl.BlockSpec((tm, tk), lambda i, j, k: (i, jnp.minimum(k, nh - 1))),
                   pl.BlockSpec((tm, tk), lambda i, j, k: (i, jnp.maximum(k - nh, 0)))]
    elif parts > 1:
        a_specs = [pl.BlockSpec((None, tk, tm), lambda i, j, k: (i // nh, k, i % nh)) if ta
                   else pl.BlockSpec((None, tm, tk), lambda i, j, k: (k // nh, i, k % nh))]
    else:
        a_specs = [pl.BlockSpec((tk, tm), lambda i, j, k: (k, i)) if ta
                   else pl.BlockSpec((tm, tk), lambda i, j, k: (i, k))]
    b_spec = (pl.BlockSpec((tn, tk), lambda i, j, k: (j, k)) if tb
              else pl.BlockSpec((tk, tn), lambda i, j, k: (k, j)))
    body, in_specs, args = _ordered(body, [*a_specs, b_spec], (*(a if halves else (a,)), b), after)
    return pl.pallas_call(
        body, grid=(M // tm, N // tn, nk), in_specs=in_specs,
        out_specs=pl.BlockSpec((tm, tn), lambda i, j, k: (i, j)),
        out_shape=jax.ShapeDtypeStruct((M, N), out_dtype),
        scratch_shapes=[pltpu.VMEM((tm, tn), F32)] if nk > 1 else [],
        compiler_params=_params(("parallel", "parallel", "arbitrary")), name=name,
    )(*args)


def _add_ln_fwd(x, h, g, b, *, wide, name):
    T, D = x.shape
    rb = _pick(T, 512)

    def body(x_ref, h_ref, g_ref, b_ref, *o_refs):
        pre = ALPHA * x_ref[...].astype(F32) + h_ref[...].astype(F32)
        mu = jnp.mean(pre, axis=1, keepdims=True)
        cen = pre - mu
        var = jnp.mean(cen * cen, axis=1, keepdims=True)
        y = cen * lax.rsqrt(var + LN_EPS) * g_ref[...] + b_ref[...]
        for o_ref in o_refs:
            o_ref[...] = y.astype(o_ref.dtype)

    row = pl.BlockSpec((rb, D), lambda i: (i, 0))
    vec = pl.BlockSpec((1, D), lambda i: (0, 0))
    dtypes = [F32, BF16] if wide else [BF16]
    return pl.pallas_call(
        body, grid=(T // rb,), in_specs=[row, row, vec, vec], out_specs=[row] * len(dtypes),
        out_shape=[jax.ShapeDtypeStruct((T, D), dt) for dt in dtypes],
        compiler_params=_params(("parallel",)), name=name,
    )(x, h, g.reshape(1, D), b.reshape(1, D))


def _add_ln_bwd(x, h, g, terms, *, after=None, name):
    T, D = x.shape
    rb = _pick(T, 512)
    coefs = [c for c, _ in terms]
    nt = len(terms)

    def body(*refs):
        x_ref, h_ref, g_ref = refs[:3]
        t_refs = refs[3:3 + nt]
        dpb_ref, dg_ref, db_ref = refs[3 + nt:]
        dy = None
        for c, r in zip(coefs, t_refs):
            v = r[...].astype(F32) if c == 1.0 else c * r[...].astype(F32)
            dy = v if dy is None else dy + v
        pre = ALPHA * x_ref[...].astype(F32) + h_ref[...].astype(F32)
        mu = jnp.mean(pre, axis=1, keepdims=True)
        cen = pre - mu
        var = jnp.mean(cen * cen, axis=1, keepdims=True)
        rstd = lax.rsqrt(var + LN_EPS)
        xhat = cen * rstd
        dxh = dy * g_ref[...]
        m1 = jnp.mean(dxh, axis=1, keepdims=True)
        m2 = jnp.mean(dxh * xhat, axis=1, keepdims=True)
        dpre = rstd * (dxh - m1 - xhat * m2)
        dpb_ref[...] = dpre.astype(BF16)
        dg = jnp.sum(dy * xhat, axis=0, keepdims=True)
        db = jnp.sum(dy, axis=0, keepdims=True)

        @pl.when(pl.program_id(0) == 0)
        def _():
            dg_ref[...] = dg
            db_ref[...] = db

        @pl.when(pl.program_id(0) > 0)
        def _():
            dg_ref[...] += dg
            db_ref[...] += db

    row = pl.BlockSpec((rb, D), lambda i: (i, 0))
    vec = pl.BlockSpec((1, D), lambda i: (0, 0))
    body, in_specs, args = _ordered(body, [row, row, vec] + [row] * nt,
                                    (x, h, g.reshape(1, D), *[a for _, a in terms]), after)
    return pl.pallas_call(
        body, grid=(T // rb,), in_specs=in_specs,
        out_specs=[row, vec, vec],
        out_shape=[jax.ShapeDtypeStruct((T, D), BF16),
                   jax.ShapeDtypeStruct((1, D), F32), jax.ShapeDtypeStruct((1, D), F32)],
        compiler_params=_params(("arbitrary",)), name=name,
    )(*args)


def _lincomb(terms, out_dtype, *, name):
    R, C = terms[0][1].shape
    rb = _pick(R, 512)
    coefs = [c for c, _ in terms]
    nt = len(terms)

    def body(*refs):
        acc = None
        for c, r in zip(coefs, refs[:nt]):
            v = r[...].astype(F32)
            v = v if c == 1.0 else c * v
            acc = v if acc is None else acc + v
        refs[nt][...] = acc.astype(out_dtype)

    row = pl.BlockSpec((rb, C), lambda i: (i, 0))
    return pl.pallas_call(
        body, grid=(R // rb,), in_specs=[row] * nt, out_specs=row,
        out_shape=jax.ShapeDtypeStruct((R, C), out_dtype),
        compiler_params=_params(("parallel",)), name=name,
    )(*[a for _, a in terms])


def _loss_grad(y, tgt, *, name):
    T, D = y.shape
    rb = _pick(T, 512)

    def body(y_ref, t_ref, dy_ref, l_ref):
        err = y_ref[...] - t_ref[...]
        dy_ref[...] = err * (1.0 / D)
        part = jnp.sum(jnp.sum(err * err, axis=1, keepdims=True), axis=0, keepdims=True) * (0.5 / D)
        part = jnp.broadcast_to(part, (1, LANES))

        @pl.when(pl.program_id(0) == 0)
        def _():
            l_ref[...] = part

        @pl.when(pl.program_id(0) > 0)
        def _():
            l_ref[...] += part

    row = pl.BlockSpec((rb, D), lambda i: (i, 0))
    return pl.pallas_call(
        body, grid=(T // rb,), in_specs=[row, row],
        out_specs=[row, pl.BlockSpec((1, LANES), lambda i: (0, 0))],
        out_shape=[jax.ShapeDtypeStruct((T, D), F32), jax.ShapeDtypeStruct((1, LANES), F32)],
        compiler_params=_params(("arbitrary",)), name=name,
    )(y, tgt)


def _sgu_fwd(zp, ln_g, ln_b, ws, bst, *, name):
    T, E2 = zp.shape
    E = E2 // 2
    G = ws.shape[0]
    cg = E // G
    rb = 2 * CHUNK

    def body(z_ref, g_ref, b_ref, ws_ref, bs_ref, y_ref):
        u = _gelu(z_ref[:, :E].astype(F32))
        v = _gelu(z_ref[:, E:].astype(F32))
        mu = jnp.mean(v, axis=1, keepdims=True)
        cen = v - mu
        var = jnp.mean(cen * cen, axis=1, keepdims=True)
        vn = (cen * lax.rsqrt(var + LN_EPS) * g_ref[...] + b_ref[...]).astype(BF16)
        for ci in range(rb // CHUNK):
            rows = slice(ci * CHUNK, (ci + 1) * CHUNK)
            for gi in range(G):
                cols = slice(gi * cg, (gi + 1) * cg)
                sv = jnp.dot(ws_ref[gi], vn[rows, cols], preferred_element_type=F32)
                sv = sv + bs_ref[:, gi:gi + 1]
                y_ref[rows, cols] = (u[rows, cols] * sv).astype(BF16)

    return pl.pallas_call(
        body, grid=(T // rb,),
        in_specs=[pl.BlockSpec((rb, E2), lambda i: (i, 0)),
                  pl.BlockSpec((1, E), lambda i: (0, 0)), pl.BlockSpec((1, E), lambda i: (0, 0)),
                  pl.BlockSpec((G, CHUNK, CHUNK), lambda i: (0, 0, 0)),
                  pl.BlockSpec((CHUNK, G), lambda i: (0, 0))],
        out_specs=pl.BlockSpec((rb, E), lambda i: (i, 0)),
        out_shape=jax.ShapeDtypeStruct((T, E), BF16),
        compiler_params=_params(("parallel",)), name=name,
    )(zp, ln_g.reshape(1, E), ln_b.reshape(1, E), ws, bst)


def _sgu_bwd(zp, dy, ln_g, ln_b, ws, wst, bst, *, after=None, name):
    T, E2 = zp.shape
    E = E2 // 2
    G = ws.shape[0]
    cg = E // G
    rb = CHUNK
    nsteps = T // rb

    def body(z_ref, dy_ref, g_ref, b_ref, ws_ref, wst_ref, bs_ref,
             dz_ref, dg_ref, db_ref, dws_ref, dbs_ref, dsv_acc):
        step = pl.program_id(0)

        @pl.when(step == 0)
        def _():
            dg_ref[...] = jnp.zeros_like(dg_ref)
            db_ref[...] = jnp.zeros_like(db_ref)
            dws_ref[...] = jnp.zeros_like(dws_ref)
            dsv_acc[...] = jnp.zeros_like(dsv_acc)

        zu = z_ref[:, :E].astype(F32)
        zv = z_ref[:, E:].astype(F32)
        u = _gelu(zu)
        v = _gelu(zv)
        mu = jnp.mean(v, axis=1, keepdims=True)
        cen = v - mu
        var = jnp.mean(cen * cen, axis=1, keepdims=True)
        rstd = lax.rsqrt(var + LN_EPS)
        xhat = cen * rstd
        vn = (xhat * g_ref[...] + b_ref[...]).astype(BF16)
        dyv = dy_ref[...].astype(F32)
        dsv = dyv * u
        dsv_acc[...] += dsv
        dsvb = dsv.astype(BF16)
        tril = (lax.broadcasted_iota(jnp.int32, (CHUNK, CHUNK), 0)
                >= lax.broadcasted_iota(jnp.int32, (CHUNK, CHUNK), 1))
        du_parts = []
        dvn_parts = []
        for gi in range(G):
            cols = slice(gi * cg, (gi + 1) * cg)
            sv = jnp.dot(ws_ref[gi], vn[:, cols], preferred_element_type=F32) + bs_ref[:, gi:gi + 1]
            du_parts.append(dyv[:, cols] * sv)
            dvn_parts.append(jnp.dot(wst_ref[gi], dsvb[:, cols], preferred_element_type=F32))
            dw = lax.dot_general(dsvb[:, cols], vn[:, cols], (((1,), (1,)), ((), ())),
                                 preferred_element_type=F32)
            dws_ref[gi] += jnp.where(tril, dw, 0.0)
        du = jnp.concatenate(du_parts, axis=1)
        dvn = jnp.concatenate(dvn_parts, axis=1)
        dg_ref[...] += jnp.sum(dvn * xhat, axis=0, keepdims=True)
        db_ref[...] += jnp.sum(dvn, axis=0, keepdims=True)
        dxh = dvn * g_ref[...]
        m1 = jnp.mean(dxh, axis=1, keepdims=True)
        m2 = jnp.mean(dxh * xhat, axis=1, keepdims=True)
        dv = rstd * (dxh - m1 - xhat * m2)
        dz_ref[:, :E] = (du * _gelu_grad(zu)).astype(BF16)
        dz_ref[:, E:] = (dv * _gelu_grad(zv)).astype(BF16)

        @pl.when(step == nsteps - 1)
        def _():
            lane = lax.broadcasted_iota(jnp.int32, (CHUNK, LANES), 1)
            out = jnp.zeros((CHUNK, LANES), F32)
            for gi in range(G):
                s = jnp.sum(dsv_acc[:, gi * cg:(gi + 1) * cg], axis=1, keepdims=True)
                out = jnp.where(lane == gi, s, out)
            dbs_ref[...] = out

    vecE = pl.BlockSpec((1, E), lambda i: (0, 0))
    wspec = pl.BlockSpec((G, CHUNK, CHUNK), lambda i: (0, 0, 0))
    body, in_specs, args = _ordered(
        body, [pl.BlockSpec((rb, E2), lambda i: (i, 0)), pl.BlockSpec((rb, E), lambda i: (i, 0)),
               vecE, vecE, wspec, wspec, pl.BlockSpec((CHUNK, G), lambda i: (0, 0))],
        (zp, dy, ln_g.reshape(1, E), ln_b.reshape(1, E), ws, wst, bst), after)
    return pl.pallas_call(
        body, grid=(nsteps,), in_specs=in_specs,
        out_specs=[pl.BlockSpec((rb, E2), lambda i: (i, 0)), vecE, vecE, wspec,
                   pl.BlockSpec((CHUNK, LANES), lambda i: (0, 0))],
        out_shape=[jax.ShapeDtypeStruct((T, E2), BF16), jax.ShapeDtypeStruct((1, E), F32),
                   jax.ShapeDtypeStruct((1, E), F32), jax.ShapeDtypeStruct((G, CHUNK, CHUNK), F32),
                   jax.ShapeDtypeStruct((CHUNK, LANES), F32)],
        scratch_shapes=[pltpu.VMEM((CHUNK, E), F32)],
        compiler_params=_params(("arbitrary",)), name=name,
    )(*args)


def _shift_down(x, k, row):
    return jnp.where(row >= k, pltpu.roll(x, k, 0), 0.0)


def _shift_up(x, k, row, T):
    return jnp.where(row < T - k, pltpu.roll(x, T - k, 0), 0.0)


def _conv3(x, w_ref, b_ref, row):
    return (w_ref[0:1, :] * _shift_down(x, 2, row) + w_ref[1:2, :] * _shift_down(x, 1, row)
            + w_ref[2:3, :] * x + b_ref[...])


def _convgate_fwd(hh, cw, cb, *, name):
    T, F2 = hh.shape
    F = F2 // 2
    ns = F // LANES

    def body(a_ref, g_ref, wa_ref, wg_ref, ba_ref, bg_ref, o_ref, ca_ref, cg_ref):
        row = lax.broadcasted_iota(jnp.int32, (T, LANES), 0)
        ca = _conv3(a_ref[...].astype(F32), wa_ref, ba_ref, row)
        cgv = _conv3(g_ref[...].astype(F32), wg_ref, bg_ref, row)
        o_ref[...] = (_gelu(ca) * cgv).astype(BF16)
        ca_ref[...] = ca.astype(ACT)
        cg_ref[...] = cgv.astype(ACT)

    sa = lambda r: pl.BlockSpec((r, LANES), lambda j: (0, j))
    sg = lambda r: pl.BlockSpec((r, LANES), lambda j: (0, j + ns))
    return pl.pallas_call(
        body, grid=(ns,), in_specs=[sa(T), sg(T), sa(3), sg(3), sa(1), sg(1)],
        out_specs=[sa(T)] * 3,
        out_shape=[jax.ShapeDtypeStruct((T, F), BF16), jax.ShapeDtypeStruct((T, F), ACT),
                   jax.ShapeDtypeStruct((T, F), ACT)],
        compiler_params=_params(("parallel",)), name=name,
    )(hh, hh, cw, cw, cb, cb)


def _convgate_bwd(hh, hca, hcg, dact, cw, *, name):
    T, F2 = hh.shape
    F = F2 // 2
    ns = F // LANES

    def body(a_ref, g_ref, ca_ref, cg_ref, d_ref, wa_ref, wg_ref,
             da_ref, dg_ref, dwa_ref, dwg_ref, dba_ref, dbg_ref):
        row = lax.broadcasted_iota(jnp.int32, (T, LANES), 0)
        d = d_ref[...].astype(F32)
        ca = ca_ref[...].astype(F32)
        cgv = cg_ref[...].astype(F32)
        cdf = 0.5 * (1.0 + lax.erf(ca * INV_SQRT2))
        dca = d * cgv * (cdf + ca * jnp.exp(-0.5 * ca * ca) * INV_SQRT_2PI)
        dcg = d * (ca * cdf)
        for x_ref, w_ref, dc, dx_ref, dw_ref, db_ref in (
                (a_ref, wa_ref, dca, da_ref, dwa_ref, dba_ref),
                (g_ref, wg_ref, dcg, dg_ref, dwg_ref, dbg_ref)):
            x = x_ref[...].astype(F32)
            up1, up2 = _shift_up(dc, 1, row, T), _shift_up(dc, 2, row, T)
            dx_ref[...] = (w_ref[2:3, :] * dc + w_ref[1:2, :] * up1 + w_ref[0:1, :] * up2).astype(BF16)
            dw_ref[0:1, :] = jnp.sum(up2 * x, axis=0, keepdims=True)
            dw_ref[1:2, :] = jnp.sum(up1 * x, axis=0, keepdims=True)
            dw_ref[2:3, :] = jnp.sum(dc * x, axis=0, keepdims=True)
            db_ref[...] = jnp.sum(dc, axis=0, keepdims=True)

    sa = lambda r: pl.BlockSpec((r, LANES), lambda j: (0, j))
    sg = lambda r: pl.BlockSpec((r, LANES), lambda j: (0, j + ns))
    return pl.pallas_call(
        body, grid=(ns,), in_specs=[sa(T), sg(T), sa(T), sa(T), sa(T), sa(3), sg(3)],
        out_specs=[sa(T), sa(T), sa(3), sa(3), sa(1), sa(1)],
        out_shape=[jax.ShapeDtypeStruct((T, F), BF16), jax.ShapeDtypeStruct((T, F), BF16),
                   jax.ShapeDtypeStruct((3, F), F32), jax.ShapeDtypeStruct((3, F), F32),
                   jax.ShapeDtypeStruct((1, F), F32), jax.ShapeDtypeStruct((1, F), F32)],
        compiler_params=_params(("parallel",)), name=name,
    )(hh, hh, hca, hcg, dact, cw, cw)


def _bucket_maps():
    iq = np.arange(BLK)[:, None]
    ik = np.arange(2 * BLK)[None, :]
    delta = iq + BLK - ik
    maps = []
    for win, dil in DILATED_GROUPS:
        n = np.clip(delta, 0, None) * dil
        max_exact = REL_BUCKETS // 2
        nf = np.maximum(n, 1).astype(np.float32)
        large = max_exact + (np.log(nf / np.float32(max_exact)) / np.float32(math.log(REL_MAX_DIST / max_exact))
                             * np.float32(REL_BUCKETS - max_exact)).astype(np.int32)
        large = np.minimum(large, REL_BUCKETS - 1)
        bucket = np.where(n < max_exact, n, large)
        valid = (delta >= 0) & (delta <= win // dil)
        maps.append(np.where(valid, bucket, -1).astype(np.int32))
    return np.stack(maps)


def _band_bias(rel_table, bmap, H, *, name):
    def body(t_ref, m_ref, o_ref):
        g = pl.program_id(0)
        bm = m_ref[0]
        for h in range(H):
            acc = jnp.full((BLK, 2 * BLK), NEG, F32)
            for b in range(REL_BUCKETS):
                acc = jnp.where(bm == b, t_ref[b, g * H + h], acc)
            o_ref[0, h] = acc

    return pl.pallas_call(
        body, grid=(N_GROUPS,),
        in_specs=[pl.BlockSpec(memory_space=pltpu.SMEM),
                  pl.BlockSpec((1, BLK, 2 * BLK), lambda g: (g, 0, 0))],
        out_specs=pl.BlockSpec((1, H, BLK, 2 * BLK), lambda g: (g, 0, 0, 0)),
        out_shape=jax.ShapeDtypeStruct((N_GROUPS, H, BLK, 2 * BLK), F32),
        compiler_params=_params(("parallel",)), name=name,
    )(rel_table, bmap)


def _band_bias_bwd(dbias, bmap, H, *, name):
    def body(d_ref, m_ref, o_ref):
        bm = m_ref[0]
        rowi = lax.broadcasted_iota(jnp.int32, (REL_BUCKETS, LANES), 0)
        lane = lax.broadcasted_iota(jnp.int32, (REL_BUCKETS, LANES), 1)
        out = jnp.zeros((REL_BUCKETS, LANES), F32)
        for h in range(H):
            dv = d_ref[0, h]
            for b in range(REL_BUCKETS):
                s = jnp.sum(jnp.sum(jnp.where(bm == b, dv, 0.0), axis=1, keepdims=True),
                            axis=0, keepdims=True)
                out = jnp.where((rowi == b) & (lane == h), s, out)
        o_ref[0] = out

    return pl.pallas_call(
        body, grid=(N_GROUPS,),
        in_specs=[pl.BlockSpec((1, H, BLK, 2 * BLK), lambda g: (g, 0, 0, 0)),
                  pl.BlockSpec((1, BLK, 2 * BLK), lambda g: (g, 0, 0))],
        out_specs=pl.BlockSpec((1, REL_BUCKETS, LANES), lambda g: (g, 0, 0)),
        out_shape=jax.ShapeDtypeStruct((N_GROUPS, REL_BUCKETS, LANES), F32),
        compiler_params=_params(("parallel",)), name=name,
    )(dbias, bmap)


def _head_masks():
    lane = lax.broadcasted_iota(jnp.int32, (BLK, LANES), 1)
    return (lane < HEAD_DIM, lane >= HEAD_DIM)


def _attn_fwd(q, kv, bias, gi, *, name):
    T = q.shape[0]
    HD = kv.shape[1] // 2
    d = DILATED_GROUPS[gi][1]
    S = T // d
    NB = S // BLK
    H = HD // HEAD_DIM
    qv, qcol = (q, gi) if d == 1 else (q[:, gi * HD:(gi + 1) * HD].reshape(S, d * HD), 0)
    kvv = kv.reshape(S, d * 2 * HD)

    def body(q_ref, kp_ref, kc_ref, vp_ref, vc_ref, b_ref, o_ref, l_ref):
        n = pl.program_id(1)
        col = lax.broadcasted_iota(jnp.int32, (BLK, 2 * BLK), 1)
        first = (n == 0) & (col < BLK)
        hm = _head_masks()
        for p in range(HD // LANES):
            sl = slice(p * LANES, (p + 1) * LANES)
            qp = q_ref[:, sl]
            kc = jnp.concatenate([kp_ref[:, sl], kc_ref[:, sl]], axis=0)
            vc = jnp.concatenate([vp_ref[:, sl], vc_ref[:, sl]], axis=0)
            outs = []
            lses = []
            for hh in range(2):
                qm = jnp.where(hm[hh], qp, jnp.zeros_like(qp))
                s = lax.dot_general(qm, kc, (((1,), (1,)), ((), ())), preferred_element_type=F32)
                s = jnp.where(first, NEG, s + b_ref[2 * p + hh])
                m = jnp.max(s, axis=1, keepdims=True)
                e = jnp.exp(s - m)
                den = jnp.sum(e, axis=1, keepdims=True)
                outs.append(jnp.dot((e / den).astype(BF16), vc, preferred_element_type=F32))
                lses.append(m + jnp.log(den))
            o_ref[:, sl] = jnp.where(hm[0], outs[0], outs[1])
            l_ref[:, sl] = jnp.where(hm[0], lses[0], lses[1])

    blk = lambda f: pl.BlockSpec((BLK, HD), f)
    prev = lambda n: jnp.maximum(n - 1, 0)
    return pl.pallas_call(
        body, grid=(d, NB),
        in_specs=[blk(lambda r, n: (n, r + qcol)),
                  blk(lambda r, n: (prev(n), r * 2)), blk(lambda r, n: (n, r * 2)),
                  blk(lambda r, n: (prev(n), r * 2 + 1)), blk(lambda r, n: (n, r * 2 + 1)),
                  pl.BlockSpec((H, BLK, 2 * BLK), lambda r, n: (0, 0, 0))],
        out_specs=[blk(lambda r, n: (n, r)), blk(lambda r, n: (n, r))],
        out_shape=[jax.ShapeDtypeStruct((S, d * HD), F32), jax.ShapeDtypeStruct((S, d * HD), F32)],
        compiler_params=_params(("parallel", "parallel")), name=name,
    )(qv, kvv, kvv, kvv, kvv, bias)


def _attn_combine(os, ls, *, name):
    T, HD = os[0].shape
    rb = _pick(T, 512)

    def body(o0, o1, o2, l0, l1, l2, o_ref, ob_ref, l_ref):
        la, lb, lc = l0[...], l1[...], l2[...]
        m = jnp.maximum(jnp.maximum(la, lb), lc)
        L = m + jnp.log(jnp.exp(la - m) + jnp.exp(lb - m) + jnp.exp(lc - m))
        o = jnp.exp(la - L) * o0[...] + jnp.exp(lb - L) * o1[...] + jnp.exp(lc - L) * o2[...]
        o_ref[...] = o
        ob_ref[...] = o.astype(BF16)
        l_ref[...] = L

    row = pl.BlockSpec((rb, HD), lambda i: (i, 0))
    return pl.pallas_call(
        body, grid=(T // rb,), in_specs=[row] * 6, out_specs=[row] * 3,
        out_shape=[jax.ShapeDtypeStruct((T, HD), F32), jax.ShapeDtypeStruct((T, HD), BF16),
                   jax.ShapeDtypeStruct((T, HD), F32)],
        compiler_params=_params(("parallel",)), name=name,
    )(*[a.reshape(T, HD) for a in os], *[a.reshape(T, HD) for a in ls])


def _attn_bwd(q, kv, bias, do, o, L, gi, *, name):
    T = q.shape[0]
    HD = kv.shape[1] // 2
    d = DILATED_GROUPS[gi][1]
    S = T // d
    NB = S // BLK
    H = HD // HEAD_DIM
    qv, qcol = (q, gi) if d == 1 else (q[:, gi * HD:(gi + 1) * HD].reshape(S, d * HD), 0)
    kvv = kv.reshape(S, d * 2 * HD)
    dov, ov, Lv = (a.reshape(S, d * HD) for a in (do, o, L))

    def body(q_ref, kp_ref, kc_ref, vp_ref, vc_ref, b_ref, do_ref, o_ref, L_ref,
             dq_ref, dk_ref, dv_ref, db_ref, ck_ref, cv_ref):
        r = pl.program_id(0)
        n = pl.program_id(1)

        @pl.when((r == 0) & (n == 0))
        def _():
            db_ref[...] = jnp.zeros_like(db_ref)

        @pl.when(n == 0)
        def _():
            ck_ref[...] = jnp.zeros_like(ck_ref)
            cv_ref[...] = jnp.zeros_like(cv_ref)

        @pl.when(n < NB)
        def _():
            col = lax.broadcasted_iota(jnp.int32, (BLK, 2 * BLK), 1)
            first = (n == 0) & (col < BLK)
            hm = _head_masks()
            for p in range(HD // LANES):
                sl = slice(p * LANES, (p + 1) * LANES)
                qp = q_ref[:, sl]
                kc = jnp.concatenate([kp_ref[:, sl], kc_ref[:, sl]], axis=0)
                vc = jnp.concatenate([vp_ref[:, sl], vc_ref[:, sl]], axis=0)
                dop = do_ref[:, sl]
                dob = dop.astype(BF16)
                prod = dop * o_ref[:, sl]
                Lp = L_ref[:, sl]
                dq_parts = []
                dkc = None
                dvc = None
                for hh in range(2):
                    qm = jnp.where(hm[hh], qp, jnp.zeros_like(qp))
                    dom = jnp.where(hm[hh], dob, jnp.zeros_like(dob))
                    s = lax.dot_general(qm, kc, (((1,), (1,)), ((), ())), preferred_element_type=F32)
                    s = jnp.where(first, NEG, s + b_ref[2 * p + hh])
                    lse = Lp[:, hh * HEAD_DIM:hh * HEAD_DIM + 1]
                    pr = jnp.exp(s - lse)
                    dp = lax.dot_general(dom, vc, (((1,), (1,)), ((), ())), preferred_element_type=F32)
                    delta = jnp.sum(jnp.where(hm[hh], prod, 0.0), axis=1, keepdims=True)
                    ds = pr * (dp - delta)
                    db_ref[2 * p + hh] += ds
                    dsb = ds.astype(BF16)
                    dq_parts.append(jnp.dot(dsb, kc, preferred_element_type=F32))
                    dkh = lax.dot_general(dsb, qm, (((0,), (0,)), ((), ())), preferred_element_type=F32)
                    dvh = lax.dot_general(pr.astype(BF16), dom, (((0,), (0,)), ((), ())),
                                          preferred_element_type=F32)
                    dkc = dkh if dkc is None else dkc + dkh
                    dvc = dvh if dvc is None else dvc + dvh
                dq = jnp.where(hm[0], dq_parts[0], dq_parts[1])
                dq_ref[:, sl] = (dq * (HEAD_DIM ** -0.5)).astype(BF16)
                dk_ref[:, sl] = ck_ref[:, sl] + dkc[:BLK]
                dv_ref[:, sl] = cv_ref[:, sl] + dvc[:BLK]
                ck_ref[:, sl] = dkc[BLK:]
                cv_ref[:, sl] = dvc[BLK:]

        @pl.when(n == NB)
        def _():
            dk_ref[...] = ck_ref[...]
            dv_ref[...] = cv_ref[...]

    blk = lambda f: pl.BlockSpec((BLK, HD), f)
    cur = lambda n: jnp.minimum(n, NB - 1)
    prev = lambda n: jnp.maximum(jnp.minimum(n, NB - 1) - 1, 0)
    lag = lambda n: jnp.maximum(n - 1, 0)
    return pl.pallas_call(
        body, grid=(d, NB + 1),
        in_specs=[blk(lambda r, n: (cur(n), r + qcol)),
                  blk(lambda r, n: (prev(n), r * 2)), blk(lambda r, n: (cur(n), r * 2)),
                  blk(lambda r, n: (prev(n), r * 2 + 1)), blk(lambda r, n: (cur(n), r * 2 + 1)),
                  pl.BlockSpec((H, BLK, 2 * BLK), lambda r, n: (0, 0, 0)),
                  blk(lambda r, n: (cur(n), r)), blk(lambda r, n: (cur(n), r)),
                  blk(lambda r, n: (cur(n), r))],
        out_specs=[blk(lambda r, n: (cur(n), r)), blk(lambda r, n: (lag(n), r)),
                   blk(lambda r, n: (lag(n), r)),
                   pl.BlockSpec((H, BLK, 2 * BLK), lambda r, n: (0, 0, 0))],
        out_shape=[jax.ShapeDtypeStruct((S, d * HD), BF16), jax.ShapeDtypeStruct((S, d * HD), F32),
                   jax.ShapeDtypeStruct((S, d * HD), F32),
                   jax.ShapeDtypeStruct((H, BLK, 2 * BLK), F32)],
        scratch_shapes=[pltpu.VMEM((BLK, HD), F32), pltpu.VMEM((BLK, HD), F32)],
        compiler_params=_params(("arbitrary", "arbitrary")), name=name,
    )(qv, kvv, kvv, kvv, kvv, bias, dov, ov, Lv)


SUPER = DILATED_GROUPS[-1][1] * BLK


def _band_rows(it, d):
    r, j = it % d, it // d
    if d == 1:
        at = lambda blk: pl.ds(pl.multiple_of(blk * BLK, BLK), BLK)
    else:
        at = lambda blk: pl.ds(r + d * BLK * blk, BLK, stride=d)
    return at(j), at(jnp.maximum(j - 1, 0))


def _stack_heads(x, hm):
    zero = jnp.zeros_like(x)
    return jnp.concatenate([jnp.where(hm[0], x, zero), jnp.where(hm[1], x, zero)], axis=0)


def _band_loops(step, d, unroll):
    n_it = SUPER // BLK

    def run(lo, hi, inside):
        if hi > lo:
            def body(it, carry):
                step(it, inside)
                return carry
            lax.fori_loop(lo, hi, body, 0, unroll=max(u for u in range(1, unroll + 1) if (hi - lo) % u == 0))

    run(0, d, False)
    run(d, n_it, True)


def _last_rows(it, d):
    m = SUPER // (d * BLK)
    if d == 1:
        return pl.ds((m - 1) * BLK, BLK)
    return pl.ds(it % d + d * BLK * (m - 1), BLK, stride=d)


def _attn_fwd_all(q, kv, bias, *, name):
    T = q.shape[0]
    HD = kv.shape[1] // 2
    PP = HD // LANES
    NS = T // SUPER

    def body(q0, q1, q2, kp_ref, kc_ref, vp_ref, vc_ref, b_ref, o_ref, ob_ref, l_ref, og, lg):
        n = pl.program_id(1)
        col = lax.broadcasted_iota(jnp.int32, (2 * BLK, 2 * BLK), 1)
        hm = _head_masks()
        for g, (q_ref, (_, d)) in enumerate(zip((q0, q1, q2), DILATED_GROUPS)):
            def step(it, inside, g=g, q_ref=q_ref, d=d):
                cur, prv = _band_rows(it, d)
                qp = q_ref[cur, :].astype(BF16)
                if inside:
                    kprev, vprev = kc_ref[prv, :], vc_ref[prv, :]
                else:
                    last = _last_rows(it, d)
                    kprev, vprev = kp_ref[last, :], vp_ref[last, :]
                kc = jnp.concatenate([kprev.astype(BF16), kc_ref[cur, :].astype(BF16)], axis=0)
                vc = jnp.concatenate([vprev.astype(BF16), vc_ref[cur, :].astype(BF16)], axis=0)
                s = lax.dot_general(_stack_heads(qp, hm), kc, (((1,), (1,)), ((), ())),
                                    preferred_element_type=F32)
                s = s + b_ref[g].reshape(2 * BLK, 2 * BLK)
                if not inside:
                    s = jnp.where((n == 0) & (col < BLK), NEG, s)
                mx = jnp.max(s, axis=1, keepdims=True)
                e = jnp.exp(s - mx)
                den = jnp.sum(e, axis=1, keepdims=True)
                out = jnp.dot((e / den).astype(BF16), vc, preferred_element_type=F32)
                lse = mx + jnp.log(den)
                og.at[g][cur, :] = jnp.where(hm[0], out[:BLK], out[BLK:])
                lg.at[g][cur, :] = jnp.where(hm[0], lse[:BLK], lse[BLK:])

            _band_loops(step, d, 8)
        la, lb, lc = lg[0], lg[1], lg[2]
        mx = jnp.maximum(jnp.maximum(la, lb), lc)
        L = mx + jnp.log(jnp.exp(la - mx) + jnp.exp(lb - mx) + jnp.exp(lc - mx))
        o = jnp.exp(la - L) * og[0] + jnp.exp(lb - L) * og[1] + jnp.exp(lc - L) * og[2]
        o_ref[...] = o
        ob_ref[...] = o.astype(BF16)
        l_ref[...] = L

    blk = lambda f: pl.BlockSpec((SUPER, LANES), f)
    prev = lambda n: jnp.maximum(n - 1, 0)
    qspec = lambda g: blk(lambda p, n: (n, g * PP + p))
    return pl.pallas_call(
        body, grid=(PP, NS),
        in_specs=[qspec(0), qspec(1), qspec(2),
                  blk(lambda p, n: (prev(n), p)), blk(lambda p, n: (n, p)),
                  blk(lambda p, n: (prev(n), PP + p)), blk(lambda p, n: (n, PP + p)),
                  pl.BlockSpec((N_GROUPS, 2, BLK, 2 * BLK), lambda p, n: (0, p, 0, 0))],
        out_specs=[blk(lambda p, n: (n, p))] * 3,
        out_shape=[jax.ShapeDtypeStruct((T, HD), F32), jax.ShapeDtypeStruct((T, HD), BF16),
                   jax.ShapeDtypeStruct((T, HD), F32)],
        scratch_shapes=[pltpu.VMEM((N_GROUPS, SUPER, LANES), F32), pltpu.VMEM((N_GROUPS, SUPER, LANES), F32)],
        compiler_params=_params(("parallel", "parallel")), name=name,
    )(q, q, q, kv, kv, kv, kv, bias)


def _attn_bwd_all(q, kv, bias, do, o, L, *, after=None, name):
    T = q.shape[0]
    HD = kv.shape[1] // 2
    PP = HD // LANES
    H = HD // HEAD_DIM
    NS = T // SUPER

    def body(q0, q1, q2, kp_ref, kc_ref, vp_ref, vc_ref, b_ref, do_ref, o_ref, L_ref,
             dq_ref, dk_ref, dv_ref, db_ref, ck_ref, cv_ref):
        n = pl.program_id(1)

        @pl.when(n == 0)
        def _():
            db_ref[...] = jnp.zeros_like(db_ref)
            ck_ref[...] = jnp.zeros_like(ck_ref)
            cv_ref[...] = jnp.zeros_like(cv_ref)

        dk_ref[...] = ck_ref[...]
        dv_ref[...] = cv_ref[...]
        ck_ref[...] = jnp.zeros_like(ck_ref)
        cv_ref[...] = jnp.zeros_like(cv_ref)

        @pl.when(n < NS)
        def _():
            col = lax.broadcasted_iota(jnp.int32, (2 * BLK, 2 * BLK), 1)
            hm = _head_masks()
            for g, (q_ref, (_, d)) in enumerate(zip((q0, q1, q2), DILATED_GROUPS)):
                def step(it, inside, g=g, q_ref=q_ref, d=d):
                    cur, prv = _band_rows(it, d)
                    last = _last_rows(it, d)
                    qp = q_ref[cur, :].astype(BF16)
                    if inside:
                        kprev, vprev = kc_ref[prv, :], vc_ref[prv, :]
                    else:
                        kprev, vprev = kp_ref[last, :], vp_ref[last, :]
                    kc = jnp.concatenate([kprev.astype(BF16), kc_ref[cur, :].astype(BF16)], axis=0)
                    vc = jnp.concatenate([vprev.astype(BF16), vc_ref[cur, :].astype(BF16)], axis=0)
                    dop = do_ref[cur, :]
                    prod = dop * o_ref[cur, :]
                    Lp = L_ref[cur, :]
                    qs = _stack_heads(qp, hm)
                    dos = _stack_heads(dop.astype(BF16), hm)
                    lse = jnp.concatenate([Lp[:, 0:1], Lp[:, HEAD_DIM:HEAD_DIM + 1]], axis=0)
                    delta = jnp.concatenate([jnp.sum(jnp.where(hm[0], prod, 0.0), axis=1, keepdims=True),
                                             jnp.sum(jnp.where(hm[1], prod, 0.0), axis=1, keepdims=True)], axis=0)
                    s = lax.dot_general(qs, kc, (((1,), (1,)), ((), ())), preferred_element_type=F32)
                    s = s + b_ref[g].reshape(2 * BLK, 2 * BLK)
                    if not inside:
                        s = jnp.where((n == 0) & (col < BLK), NEG, s)
                    pr = jnp.exp(s - lse)
                    dp = lax.dot_general(dos, vc, (((1,), (1,)), ((), ())), preferred_element_type=F32)
                    ds = pr * (dp - delta)
                    db_ref[g] += ds.reshape(2, BLK, 2 * BLK)
                    dsb = ds.astype(BF16)
                    dqs = jnp.dot(dsb, kc, preferred_element_type=F32)
                    dkc = lax.dot_general(dsb, qs, (((0,), (0,)), ((), ())), preferred_element_type=F32)
                    dvc = lax.dot_general(pr.astype(BF16), dos, (((0,), (0,)), ((), ())),
                                          preferred_element_type=F32)
                    dq_ref.at[g][cur, :] = jnp.where(hm[0], dqs[:BLK], dqs[BLK:]) * (HEAD_DIM ** -0.5)
                    ck_ref[cur, :] += dkc[BLK:]
                    cv_ref[cur, :] += dvc[BLK:]
                    if inside:
                        ck_ref[prv, :] += dkc[:BLK]
                        cv_ref[prv, :] += dvc[:BLK]
                    else:
                        dk_ref[last, :] += dkc[:BLK]
                        dv_ref[last, :] += dvc[:BLK]

                _band_loops(step, d, 4)

    blk = lambda f: pl.BlockSpec((SUPER, LANES), f)
    cur = lambda n: jnp.minimum(n, NS - 1)
    prev = lambda n: jnp.maximum(jnp.minimum(n, NS - 1) - 1, 0)
    lag = lambda n: jnp.maximum(n - 1, 0)
    qspec = lambda g: blk(lambda p, n: (cur(n), g * PP + p))
    bspec = pl.BlockSpec((N_GROUPS, 2, BLK, 2 * BLK), lambda p, n: (0, p, 0, 0))
    body, in_specs, args = _ordered(
        body, [qspec(0), qspec(1), qspec(2),
               blk(lambda p, n: (prev(n), p)), blk(lambda p, n: (cur(n), p)),
               blk(lambda p, n: (prev(n), PP + p)), blk(lambda p, n: (cur(n), PP + p)),
               bspec, blk(lambda p, n: (cur(n), p)), blk(lambda p, n: (cur(n), p)),
               blk(lambda p, n: (cur(n), p))],
        (q, q, q, kv, kv, kv, kv, bias, do, o, L), after)
    return pl.pallas_call(
        body, grid=(PP, NS + 1), in_specs=in_specs,
        out_specs=[pl.BlockSpec((N_GROUPS, SUPER, LANES), lambda p, n: (0, cur(n), p)),
                   blk(lambda p, n: (lag(n), p)), blk(lambda p, n: (lag(n), p)), bspec],
        out_shape=[jax.ShapeDtypeStruct((N_GROUPS, T, HD), F32), jax.ShapeDtypeStruct((T, HD), F32),
                   jax.ShapeDtypeStruct((T, HD), F32),
                   jax.ShapeDtypeStruct((N_GROUPS, H, BLK, 2 * BLK), F32)],
        scratch_shapes=[pltpu.VMEM((SUPER, LANES), F32), pltpu.VMEM((SUPER, LANES), F32)],
        compiler_params=_params(("arbitrary", "arbitrary")), name=name,
    )(*args)


class _Weights(dict):
    def __init__(self, base, fetch=None, emit=None, emit_small=None):
        super().__init__(base)
        self._fetch, self._emit, self._emit_small = fetch, emit, emit_small

    def fetch(self, group, after):
        if self._fetch is not None:
            for (key, layer), mat in self._fetch(group, after).items():
                self[key][layer] = mat

    def emit(self, group, mats):
        return None if self._emit is None else self._emit(group, mats)

    def emit_small(self, grads):
        return None if self._emit_small is None else self._emit_small(grads)


def _local_step(x, tgt, W):
    T, D = x.shape
    H = W["rel_table"].shape[1] // N_GROUPS
    HD = H * HEAD_DIM
    G = W["a_w_s"].shape[1]
    assert T % (DILATED_GROUPS[-1][1] * BLK) == 0

    tril = jnp.tril(jnp.ones((CHUNK, CHUNK), F32))
    bmap = jnp.asarray(_bucket_maps())
    bias = _band_bias(W["rel_table"], bmap, H, name="band_bias")

    saved = []
    xc, xcb = x, x.astype(BF16)
    kvb = None
    for i in range(DEPTH):
        s = {"x": xc, "xb": xcb}
        W.fetch(4 * i, xc)
        if i < N_A:
            ws_m = W["a_w_s"][i] * tril
            s["ws"] = ws_m.astype(BF16)
            s["wst"] = jnp.swapaxes(ws_m, 1, 2).astype(BF16)
            s["bst"] = W["a_b_s"][i].T
            s["zp"] = _mm(xcb, W["a_w_in"][i], out_dtype=ACT, name=f"a_in_{i}")
            s["y"] = _sgu_fwd(s["zp"], W["a_ln_g"][i], W["a_ln_b"][i], s["ws"], s["bst"], name=f"sgu_fwd_{i}")
            W.fetch(4 * i + 1, s["zp"])
            s["h"] = _mm(s["y"], W["a_w_out"][i], out_dtype=ACT, name=f"a_out_{i}")
        else:
            j = i - N_A
            if kvb is None:
                kvb = _mm(xcb, W["kv_w"][0], name="kv_proj")
            s["q"] = _mm(xcb, W["b_w_q_t"][j], tb=True, scale=HEAD_DIM ** -0.5, name=f"q_proj_{j}")
            s["o"], s["ob"], s["L"] = _attn_fwd_all(s["q"], kvb, bias, name=f"attn_fwd_{j}")
            W.fetch(4 * i + 1, s["q"])
            s["h"] = _mm(s["ob"], W["b_w_o"][j], out_dtype=ACT, name=f"o_proj_{j}")
        s["x1b"], = _add_ln_fwd(xc, s["h"], W["ln_g"][i, 0], W["ln_b"][i, 0], wide=False, name=f"ln1_fwd_{i}")
        s["x1"] = s["x1b"]
        W.fetch(4 * i + 2, s["x1"])
        s["hh"] = _mm(s["x1b"], W["ffn_w_up_t"][i], tb=True, out_dtype=ACT, name=f"ffn_up_{i}")
        s["cw"] = W["ffn_conv_w"][i]
        s["cb"] = W["ffn_conv_b"][i].reshape(1, -1)
        s["act"], s["hca"], s["hcg"] = _convgate_fwd(s["hh"], s["cw"], s["cb"], name=f"convgate_fwd_{i}")
        W.fetch(4 * i + 3, s["hh"])
        s["f"] = _mm(s["act"], W["ffn_w_down"][i], out_dtype=ACT, name=f"ffn_down_{i}")
        outs = _add_ln_fwd(s["x1"], s["f"], W["ln_g"][i, 1], W["ln_b"][i, 1], wide=i == DEPTH - 1,
                           name=f"ln2_fwd_{i}")
        xc, xcb = outs[0], outs[-1]
        saved.append(s)

    dy, lossv = _loss_grad(xc, tgt, name="loss_grad")
    loss = lossv[0, 0]

    gl = {k: [None] * DEPTH for k in ("ffn_w_up_t", "ffn_conv_w", "ffn_conv_b", "ffn_w_down", "ln_g", "ln_b")}
    ga = {k: [None] * N_A for k in ("a_w_in", "a_ln_g", "a_ln_b", "a_w_s", "a_b_s", "a_w_out")}
    gb = {k: [None] * (DEPTH - N_A) for k in ("b_w_q_t", "b_w_o")}
    mats = ("a_w_in", "a_w_out", "b_w_q_t", "b_w_o", "ffn_w_up_t", "ffn_w_down")
    dks, dvs, dbias = [], [], []
    grads = {}
    terms = [(1.0, dy)]
    tok = None
    small_keys = ("ffn_conv_w", "ffn_conv_b", "ln_g", "ln_b", "a_ln_g", "a_ln_b", "a_w_s", "a_b_s")
    for i in reversed(range(DEPTH)):
        s = saved[i]
        dp2b, dg2, db2 = _add_ln_bwd(s["x1"], s["f"], W["ln_g"][i, 1], terms, after=tok, name=f"ln2_bwd_{i}")
        dact = _mm(dp2b, W["ffn_w_down"][i], tb=True, out_dtype=ACT, name=f"ffn_down_dx_{i}")
        gl["ffn_w_down"][i] = _mm(s["act"], dp2b, ta=True, out_dtype=BF16, name=f"ffn_down_dw_{i}")
        dha, dhg, dwa, dwg, dba, dbg = _convgate_bwd(s["hh"], s["hca"], s["hcg"], dact, s["cw"],
                                                     name=f"convgate_bwd_{i}")
        dhh = (dha, dhg)
        gl["ffn_conv_w"][i] = jnp.concatenate([dwa, dwg], axis=1)
        gl["ffn_conv_b"][i] = jnp.concatenate([dba, dbg], axis=1)[0]
        dx1 = _mm(dhh, W["ffn_w_up_t"][i], out_dtype=ACT, name=f"ffn_up_dx_{i}")
        gl["ffn_w_up_t"][i] = _mm(dhh, s["x1b"], ta=True, out_dtype=BF16, name=f"ffn_up_dw_{i}")
        tok = W.emit(3 * i + 2, {("ffn_w_up_t", i): gl["ffn_w_up_t"][i], ("ffn_w_down", i): gl["ffn_w_down"][i]})
        dp1b, dg1, db1 = _add_ln_bwd(s["x"], s["h"], W["ln_g"][i, 0], [(ALPHA, dp2b), (1.0, dx1)],
                                     after=tok, name=f"ln1_bwd_{i}")
        gl["ln_g"][i] = jnp.concatenate([dg1, dg2], axis=0)
        gl["ln_b"][i] = jnp.concatenate([db1, db2], axis=0)
        terms = [(ALPHA, dp1b)]
        if i < N_A:
            dyy = _mm(dp1b, W["a_w_out"][i], tb=True, out_dtype=ACT, name=f"a_out_dx_{i}")
            ga["a_w_out"][i] = _mm(s["y"], dp1b, ta=True, out_dtype=BF16, name=f"a_out_dw_{i}")
            tok = W.emit(3 * i + 1, {("a_w_out", i): ga["a_w_out"][i]})
            dzp, dlg, dlb, dws, dbs = _sgu_bwd(s["zp"], dyy, W["a_ln_g"][i], W["a_ln_b"][i], s["ws"],
                                               s["wst"], s["bst"], after=tok, name=f"sgu_bwd_{i}")
            ga["a_ln_g"][i], ga["a_ln_b"][i], ga["a_w_s"][i] = dlg[0], dlb[0], dws
            ga["a_b_s"][i] = dbs[:, :G].T
            if i == 0:
                for dct in (gl, ga):
                    grads.update({k: jnp.stack(v) for k, v in dct.items() if k in small_keys})
                tok = W.emit_small(grads)
            ga["a_w_in"][i] = _mm(s["xb"], dzp, ta=True, out_dtype=BF16, after=tok, name=f"a_in_dw_{i}")
            tok = W.emit(3 * i, {("a_w_in", i): ga["a_w_in"][i]})
            terms.append((1.0, _mm(dzp, W["a_w_in"][i], tb=True, out_dtype=ACT, after=tok, name=f"a_in_dx_{i}")))
        else:
            j = i - N_A
            do = _mm(dp1b, W["b_w_o"][j], tb=True, name=f"o_proj_dx_{j}")
            gb["b_w_o"][j] = _mm(s["ob"], dp1b, ta=True, out_dtype=BF16, name=f"o_proj_dw_{j}")
            tok = W.emit(3 * i + 1, {("b_w_o", j): gb["b_w_o"][j]})
            dq, dk_j, dv_j, db_j = _attn_bwd_all(s["q"], kvb, bias, do, s["o"], s["L"], after=tok,
                                                 name=f"attn_bwd_{j}")
            dks.append((1.0, dk_j))
            dvs.append((1.0, dv_j))
            dbias.append(db_j)
            terms.append((1.0, _mm(dq, W["b_w_q_t"][j], out_dtype=ACT, name=f"q_proj_dx_{j}")))
            gb["b_w_q_t"][j] = _mm(dq, s["xb"], ta=True, out_dtype=BF16, name=f"q_proj_dw_{j}")
            out_b = {("b_w_q_t", j): gb["b_w_q_t"][j]}
            if i == N_A:
                dkv = jnp.concatenate([_lincomb(dks, BF16, name="dk_sum"), _lincomb(dvs, BF16, name="dv_sum")],
                                      axis=1)
                terms.append((1.0, _mm(dkv, W["kv_w"][0], tb=True, out_dtype=ACT, name="kv_proj_dx")))
                grads["kv_w"] = [_mm(s["xb"], dkv, ta=True, out_dtype=BF16, name="kv_proj_dw")]
                out_b[("kv_w", 0)] = grads["kv_w"][0]
                dbt = _lincomb([(1.0, a.reshape(-1, 2 * BLK)) for a in dbias], F32, name="dbias_sum")
                dtab = _band_bias_bwd(dbt.reshape(N_GROUPS, H, BLK, 2 * BLK), bmap, H, name="band_bias_bwd")
                grads["rel_table"] = jnp.transpose(dtab[:, :, :H], (1, 0, 2)).reshape(REL_BUCKETS, N_GROUPS * H)
            tok = W.emit(3 * i, out_b)
    grad_x = _lincomb(terms, F32, name="grad_x")
    for dct in (gl, ga, gb):
        grads.update({k: v for k, v in dct.items() if k in mats})
    return loss, grad_x, grads


def _my_index():
    return 4 * lax.axis_index("x") + 2 * lax.axis_index("y") + lax.axis_index("c")


HBM_SPEC = pl.BlockSpec(memory_space=pltpu.HBM)


def _block(ref, k, n, axis):
    off = pl.multiple_of(k * n, n)
    return ref.at[pl.ds(off, n), :] if axis == 0 else ref.at[:, pl.ds(off, n)]


def _gather_mats(local, axis, *, name):
    L, a, b = local.shape
    n = a if axis == 0 else b
    full = (a * N_DEV, b) if axis == 0 else (a, b * N_DEV)

    def body(x_ref, *rest):
        outs = rest[:L]
        send_sems, recv_sems, local_sems = rest[L:]
        x, y, c = lax.axis_index("x"), lax.axis_index("y"), lax.axis_index("c")
        me, sibling = (x, y, c), (x, y, 1 - c)
        chips = [(1 - x, y), (x, 1 - y), (1 - x, 1 - y)]

        def slot(l, px, py, pc):
            return _block(outs[l], 4 * px + 2 * py + pc, n, axis)

        def copy(l, k, blk, to, src=None):
            return pltpu.make_async_remote_copy(
                src_ref=slot(l, *blk) if src is None else src, dst_ref=slot(l, *blk),
                send_sem=send_sems.at[7 * l + k], recv_sem=recv_sems.at[7 * l + k],
                device_id=to, device_id_type=MESH)

        mine, first, passed = [], [], []
        for l in range(L):
            mine.append(pltpu.make_async_copy(x_ref.at[l], slot(l, *me), local_sems.at[l]))
            mine[-1].start()
            first.append(copy(l, 0, me, sibling, src=x_ref.at[l]))
            first += [copy(l, 1 + j, me, (*chip, c), src=x_ref.at[l]) for j, chip in enumerate(chips)]
        for cp in first:
            cp.start()
        for l in range(L):
            for j, chip in enumerate(chips):
                copy(l, 1 + j, (*chip, c), me).wait_recv()
                passed.append(copy(l, 4 + j, (*chip, c), sibling))
                passed[-1].start()
        for l in range(L):
            copy(l, 0, sibling, me).wait_recv()
            for j, chip in enumerate(chips):
                copy(l, 4 + j, (*chip, 1 - c), me).wait_recv()
        for cp in first + passed:
            cp.wait_send()
        for cp in mine:
            cp.wait()

    return pl.pallas_call(
        body, out_shape=[jax.ShapeDtypeStruct(full, local.dtype)] * L,
        in_specs=[HBM_SPEC], out_specs=[HBM_SPEC] * L,
        scratch_shapes=[pltpu.SemaphoreType.DMA((7 * L,)), pltpu.SemaphoreType.DMA((7 * L,)),
                        pltpu.SemaphoreType.DMA((L,))],
        name=name,
    )(local)


SEM_SPEC = pl.BlockSpec(memory_space=pltpu.SEMAPHORE)
FLOWING = pltpu.SideEffectType.DATAFLOW_SIDE_EFFECTING


def _peers(x, y, c):
    return [(1 - x if k & 4 else x, 1 - y if k & 2 else y, 1 - c if k & 1 else c) for k in range(1, N_DEV)]


def _ends(src_ref, land_ref, peer_index, me, n, axis, gather):
    if gather:
        return src_ref, _block(land_ref, me, n, axis)
    return _block(src_ref, peer_index, n, axis), land_ref.at[me]


def _send_start(groups, gather, *, name):
    flat = [(g, j, mat, axis) for g, items in enumerate(groups) for j, (mat, axis) in enumerate(items)]
    M, G = len(flat), len(groups)
    lands, ns = [], []
    for _, _, mat, axis in flat:
        A, B = mat.shape
        if gather:
            lands.append((A * N_DEV, B) if axis == 0 else (A, B * N_DEV))
            ns.append(A if axis == 0 else B)
        else:
            lands.append((N_DEV, A // N_DEV, B) if axis == 0 else (N_DEV, A, B // N_DEV))
            ns.append(A // N_DEV if axis == 0 else B // N_DEV)

    def body(*refs):
        src_refs, land_refs, sems = refs[:M], refs[M:2 * M], refs[2 * M:2 * M + 3 * G]
        token = refs[-1]
        x, y, c = lax.axis_index("x"), lax.axis_index("y"), lax.axis_index("c")
        me = 4 * x + 2 * y + c
        for i, (g, j, _, axis) in enumerate(flat):
            for k, (px, py, pc) in enumerate(_peers(x, y, c)):
                s, d = _ends(src_refs[i], land_refs[i], 4 * px + 2 * py + pc, me, ns[i], axis, gather)
                pltpu.make_async_remote_copy(
                    src_ref=s, dst_ref=d, send_sem=sems[3 * g].at[7 * j + k], recv_sem=sems[3 * g + 1].at[7 * j + k],
                    device_id=(px, py, pc), device_id_type=MESH).start()
            s, d = _ends(src_refs[i], land_refs[i], me, me, ns[i], axis, gather)
            pltpu.make_async_copy(s, d, sems[3 * g + 2].at[j]).start()
        token[...] = jnp.zeros_like(token)

    sem_shapes = []
    for items in groups:
        sem_shapes += [pltpu.SemaphoreType.DMA((7 * len(items),))] * 2 + [pltpu.SemaphoreType.DMA((len(items),))]
    outs = pl.pallas_call(
        body, name=name,
        out_shape=(*sem_shapes, *[pltpu.HBM(m.shape, m.dtype) for _, _, m, _ in flat],
                   *[pltpu.HBM(shp, m.dtype) for shp, (_, _, m, _) in zip(lands, flat)],
                   jax.ShapeDtypeStruct((8, LANES), F32)),
        in_specs=[HBM_SPEC] * (2 * M),
        out_specs=(*[SEM_SPEC] * (3 * G), *[HBM_SPEC] * (2 * M), pl.BlockSpec(memory_space=pltpu.VMEM)),
        input_output_aliases={i: 3 * G + i for i in range(2 * M)},
        compiler_params=pltpu.CompilerParams(has_side_effects=FLOWING),
    )(*[pltpu.with_memory_space_constraint(m, pltpu.HBM) for _, _, m, _ in flat],
      *[pltpu.with_memory_space_constraint(lax.empty(shp, m.dtype), pltpu.HBM)
        for shp, (_, _, m, _) in zip(lands, flat)])
    handles = []
    for g in range(G):
        idx = [i for i, f in enumerate(flat) if f[0] == g]
        handles.append((outs[3 * g], outs[3 * g + 1], outs[3 * g + 2], [outs[3 * G + i] for i in idx],
                        [outs[3 * G + M + i] for i in idx], [flat[i][3] for i in idx]))
    return handles, outs[-1]


def _send_wait(handle, gather, after, *, name):
    send_sems, recv_sems, local_sems, mats, lands, axes = handle
    n_m = len(mats)
    ns = []
    for mat, land, axis in zip(mats, lands, axes):
        ns.append(mat.shape[axis] if gather else land.shape[1 + axis])

    def body(*refs):
        src_refs, land_refs = refs[:n_m], refs[n_m:2 * n_m]
        ssem, rsem, lsem = refs[2 * n_m:2 * n_m + 3]
        x, y, c = lax.axis_index("x"), lax.axis_index("y"), lax.axis_index("c")
        me = 4 * x + 2 * y + c
        for j in range(n_m):
            for k, (px, py, pc) in enumerate(_peers(x, y, c)):
                s, d = _ends(src_refs[j], land_refs[j], 4 * px + 2 * py + pc, me, ns[j], axes[j], gather)
                cp = pltpu.make_async_remote_copy(
                    src_ref=s, dst_ref=d, send_sem=ssem.at[7 * j + k], recv_sem=rsem.at[7 * j + k],
                    device_id=(px, py, pc), device_id_type=MESH)
                cp.wait_send()
                cp.wait_recv()
            s, d = _ends(src_refs[j], land_refs[j], me, me, ns[j], axes[j], gather)
            pltpu.make_async_copy(s, d, lsem.at[j]).wait()

    outs = pl.pallas_call(
        body, name=name,
        out_shape=(*[pltpu.HBM(m.shape, m.dtype) for m in mats], *[pltpu.HBM(l.shape, l.dtype) for l in lands]),
        in_specs=[HBM_SPEC] * (2 * n_m) + [SEM_SPEC] * 3 + [pl.BlockSpec(memory_space=pl.ANY)],
        out_specs=tuple([HBM_SPEC] * (2 * n_m)),
        input_output_aliases={i: i for i in range(2 * n_m)},
        compiler_params=pltpu.CompilerParams(has_side_effects=FLOWING),
    )(*mats, *lands, send_sems, recv_sems, local_sems, after)
    return list(outs[n_m:])


def _sum_parts(parts, *, name):
    n, R, C = parts.shape
    rb = _pick(R, 512) if R % LANES == 0 else R

    def body(p_ref, o_ref):
        acc = p_ref[0].astype(F32)
        for k in range(1, n):
            acc = acc + p_ref[k].astype(F32)
        o_ref[...] = acc

    return pl.pallas_call(
        body, grid=(R // rb,), in_specs=[pl.BlockSpec((n, rb, C), lambda i: (0, i, 0))],
        out_specs=pl.BlockSpec((rb, C), lambda i: (i, 0)),
        out_shape=jax.ShapeDtypeStruct((R, C), F32),
        compiler_params=_params(("parallel",)), name=name,
    )(parts)


def _adamw(w, m, v, parts, *, name):
    L, R, C = w.shape
    n = parts[0].shape[0]
    cap = max(16, VMEM_LIMIT // 3 // (2 * L * n * C * parts[0].dtype.itemsize))
    rb = max([r for r in range(16, min(R, cap) + 1, 16) if R % r == 0], default=R)

    def body(w_ref, m_ref, v_ref, *rest):
        p_refs = rest[:L]
        g_ref, d_ref, nm_ref, nv_ref = rest[L:]
        for l in range(L):
            @pl.when(pl.program_id(0) == l)
            def _(p_ref=p_refs[l]):
                g = p_ref[0].astype(F32)
                for k in range(1, n):
                    g = g + p_ref[k].astype(F32)
                mn = ADAM_B1 * m_ref[...] + (1.0 - ADAM_B1) * g
                vn = ADAM_B2 * v_ref[...] + (1.0 - ADAM_B2) * jnp.square(g)
                m_hat = mn / (1.0 - ADAM_B1 ** ADAM_STEP)
                v_hat = vn / (1.0 - ADAM_B2 ** ADAM_STEP)
                g_ref[...] = g
                d_ref[...] = -ADAM_LR * (m_hat / (jnp.sqrt(v_hat) + ADAM_EPS) + ADAM_WD * w_ref[...])
                nm_ref[...] = mn
                nv_ref[...] = vn

    row = pl.BlockSpec((None, rb, C), lambda l, i: (l, i, 0))
    part = lambda k: pl.BlockSpec((n, rb, C), lambda l, i: (0, jnp.where(l == k, i, 0), 0))
    return pl.pallas_call(
        body, grid=(L, R // rb), in_specs=[row, row, row] + [part(k) for k in range(L)],
        out_specs=[row] * 4, out_shape=[jax.ShapeDtypeStruct((L, R, C), F32)] * 4,
        compiler_params=_params(("arbitrary", "arbitrary")), name=name,
    )(w, m, v, *parts)


BIG = (("a_w_in", "a_w_in", 1, False), ("a_w_out", "a_w_out", 0, False), ("kv_w", "kv_w", 0, False),
       ("b_w_q", "b_w_q_t", 0, True), ("b_w_o", "b_w_o", 1, False), ("ffn_w_up", "ffn_w_up_t", 0, True),
       ("ffn_w_down", "ffn_w_down", 0, False))
SMALL_SHARDED = (("a_ln_g", 1), ("a_ln_b", 1), ("ffn_conv_w", 2), ("ln_g", 2), ("ln_b", 2))
REPLICATED = ("a_w_s", "a_b_s", "rel_table", "ffn_conv_b")


def _pack_rows(arrs, lead=0):
    lshape = arrs[0].shape[:lead]
    p = jnp.concatenate([a.reshape(*lshape, -1, LANES) for a in arrs], axis=lead)
    pad = -p.shape[lead] % 8
    return jnp.pad(p, [(0, 0)] * lead + [(0, pad), (0, 0)])


def _unpack_rows(packed, shapes, lead=0):
    lshape = packed.shape[:lead]
    out, off = [], 0
    for shp in shapes:
        r = int(np.prod(shp)) // LANES
        out.append(lax.slice_in_dim(packed, off, off + r, axis=lead).reshape(*lshape, *shp))
        off += r
    return out


def _as_mats(a, transposed):
    a = a[None] if a.ndim == 2 else a
    return jnp.swapaxes(a, 1, 2) if transposed else a


def _merge_shards(stacked, axis):
    a = jnp.moveaxis(stacked, 0, axis)
    shp = list(a.shape)
    return a.reshape(shp[:axis] + [shp[axis] * shp[axis + 1]] + shp[axis + 2:])


def _split_shards(full, axis):
    shp = list(full.shape)
    a = full.reshape(shp[:axis] + [N_DEV, shp[axis] // N_DEV] + shp[axis + 1:])
    return jnp.moveaxis(a, axis, 0)


def kernel(x, a_w_in, a_ln_g, a_ln_b, a_w_s, a_b_s, a_w_out, kv_w, b_w_q, b_w_o, rel_table, ffn_w_up, ffn_conv_w, ffn_conv_b, ffn_w_down, ln_g, ln_b, loss_target, m_a_w_in, m_a_ln_g, m_a_ln_b, m_a_w_s, m_a_b_s, m_a_w_out, m_kv_w, m_b_w_q, m_b_w_o, m_rel_table, m_ffn_w_up, m_ffn_conv_w, m_ffn_conv_b, m_ffn_w_down, m_ln_g, m_ln_b, v_a_w_in, v_a_ln_g, v_a_ln_b, v_a_w_s, v_a_b_s, v_a_w_out, v_kv_w, v_b_w_q, v_b_w_o, v_rel_table, v_ffn_w_up, v_ffn_conv_w, v_ffn_conv_b, v_ffn_w_down, v_ln_g, v_ln_b):
    names = ["a_w_in", "a_ln_g", "a_ln_b", "a_w_s", "a_b_s", "a_w_out", "kv_w", "b_w_q", "b_w_o", "rel_table",
             "ffn_w_up", "ffn_conv_w", "ffn_conv_b", "ffn_w_down", "ln_g", "ln_b"]
    w = dict(zip(names, (a_w_in, a_ln_g, a_ln_b, a_w_s, a_b_s, a_w_out, kv_w, b_w_q, b_w_o, rel_table,
                         ffn_w_up, ffn_conv_w, ffn_conv_b, ffn_w_down, ln_g, ln_b)))
    m = dict(zip(names, (m_a_w_in, m_a_ln_g, m_a_ln_b, m_a_w_s, m_a_b_s, m_a_w_out, m_kv_w, m_b_w_q, m_b_w_o,
                         m_rel_table, m_ffn_w_up, m_ffn_conv_w, m_ffn_conv_b, m_ffn_w_down, m_ln_g, m_ln_b)))
    v = dict(zip(names, (v_a_w_in, v_a_ln_g, v_a_ln_b, v_a_w_s, v_a_b_s, v_a_w_out, v_kv_w, v_b_w_q, v_b_w_o,
                         v_rel_table, v_ffn_w_up, v_ffn_conv_w, v_ffn_conv_b, v_ffn_w_down, v_ln_g, v_ln_b)))
    small_names = [n for n, _ in SMALL_SHARDED]
    small_shapes = [w[n].shape for n in small_names]
    rep_shapes = [w[n].shape for n in REPLICATED]

    axis_of = {key: axis for _, key, axis, _ in BIG}
    src = {}
    for n, key, axis, tr in BIG:
        loc = _as_mats(w[n], tr).astype(BF16)
        for l in range(loc.shape[0]):
            src[(key, l)] = loc[l]
    order = []
    for i in range(DEPTH):
        if i < N_A:
            order += [[("a_w_in", i)], [("a_w_out", i)]]
        else:
            order += [([("kv_w", 0)] if i == N_A else []) + [("b_w_q_t", i - N_A)], [("b_w_o", i - N_A)]]
        order += [[("ffn_w_up_t", i)], [("ffn_w_down", i)]]
    small_src = _pack_rows([w[n] for n in small_names])
    srows = small_src.shape[0]
    handles, _ = _send_start([[(small_src, 0)]] + [[(src[kl], axis_of[kl[0]]) for kl in grp] for grp in order],
                             True, name="gather_start")
    small_all = _send_wait(handles[0], True, x, name="gather_wait_small")[0]
    small_st = _unpack_rows(small_all.reshape(N_DEV, srows, LANES), small_shapes, lead=1)
    base = {n: w[n] for n in REPLICATED}
    for (n, ax), st in zip(SMALL_SHARDED, small_st):
        base[n] = _merge_shards(st, ax)
    for n, key, _, tr in BIG:
        base[key] = [None] * (1 if w[n].ndim == 2 else w[n].shape[0])

    def fetch(group, after):
        mats = _send_wait(handles[1 + group], True, after, name=f"gather_wait_{group}")
        return dict(zip(order[group], mats))

    sent = {}

    def emit(group, mats):
        keys = list(mats)
        hs, token = _send_start([[(mats[kl], axis_of[kl[0]]) for kl in keys]], False, name=f"exchange_start_{group}")
        sent[group] = (keys, hs[0])
        return token

    small_sent = []

    def emit_small(grads):
        small_pack = _pack_rows([_split_shards(grads[n], ax) for n, ax in SMALL_SHARDED], lead=1)
        rest = _pack_rows([grads[n] for n in REPLICATED[1:]])
        mine = jnp.concatenate([small_pack.reshape(N_DEV * srows, LANES), rest], axis=0)
        gating = grads[REPLICATED[0]].reshape(-1, LANES).astype(BF16)
        hs, token = _send_start([[(mine, 0), (gating, 0)]], True, name="small_grads_start")
        small_sent.append(hs[0])
        return token

    loss, grad_x, grads = _local_step(x[0], loss_target[0], _Weights(base, fetch, emit, emit_small))
    loss = lax.psum(loss, ("x", "y", "c"))
    out = {}

    landed = {}
    for group in sorted(sent, reverse=True):
        keys, h = sent[group]
        landed.update(zip(keys, _send_wait(h, False, grad_x, name=f"exchange_wait_{group}")))
    last = grad_x
    for n, key, axis, tr in BIG:
        shp = w[n].shape
        parts = [landed[(key, l)] for l in range(1 if len(shp) == 2 else shp[0])]
        res = _adamw(_as_mats(w[n], tr), _as_mats(m[n], tr), _as_mats(v[n], tr), parts, name=f"adamw_{n}")
        out[n] = [(jnp.swapaxes(r, 1, 2) if tr else r).reshape(shp) for r in res]
        last = res[0]

    allp, allg = _send_wait(small_sent[0], True, last, name="small_grads_wait")
    gsum = _sum_parts(allp.reshape(N_DEV, -1, LANES), name="sum_small_grads")
    gating = _sum_parts(allg.reshape(N_DEV, -1, LANES), name="sum_gating_grads")
    g_small = lax.dynamic_slice_in_dim(gsum, _my_index() * srows, srows, axis=0)
    pack_sr = lambda d: jnp.concatenate([_pack_rows([d[n] for n in small_names]),
                                         _pack_rows([d[n] for n in REPLICATED])], axis=0)
    n_rest = sum(int(np.prod(s)) for s in rep_shapes[1:]) // LANES
    gs_in = jnp.concatenate([g_small, gating, gsum[N_DEV * srows:N_DEV * srows + n_rest]], axis=0)
    gs_in = jnp.pad(gs_in, ((0, pack_sr(w).shape[0] - gs_in.shape[0]), (0, 0)))[None]
    res = _adamw(pack_sr(w)[None], pack_sr(m)[None], pack_sr(v)[None], [gs_in], name="adamw_small")
    for n, vals in zip(small_names, zip(*[_unpack_rows(r[0, :srows], small_shapes) for r in res])):
        out[n] = list(vals)
    for n, vals in zip(REPLICATED, zip(*[_unpack_rows(r[0, srows:], rep_shapes) for r in res])):
        out[n] = list(vals)

    return (loss, grad_x[None], *[out[n][0] for n in names], *[out[n][1] for n in names],
            *[out[n][2] for n in names], *[out[n][3] for n in names])
```

```python
import math

import numpy as np
import jax
import jax.numpy as jnp
from jax import lax
from jax.experimental import pallas as pl
from jax.experimental.pallas import tpu as pltpu

F32 = jnp.float32
BF16 = jnp.bfloat16
ACT = jnp.bfloat16
MESH = pl.DeviceIdType.MESH

N_DEV = 8
DEPTH = 4
N_A = 2
CHUNK = 128
BLK = 128
HEAD_DIM = 64
DILATED_GROUPS = ((128, 1), (512, 4), (2048, 16))
N_GROUPS = 3
REL_BUCKETS = 32
REL_MAX_DIST = 2048
ALPHA = (2 * DEPTH) ** 0.25
LN_EPS = 1e-5
NEG = -1e30
ADAM_LR = 0.001
ADAM_B1 = 0.9
ADAM_B2 = 0.999
ADAM_EPS = 1e-08
ADAM_WD = 0.01
ADAM_STEP = 10

LANES = 128
VMEM_LIMIT = 56 * 1024 * 1024
MM_TILE_CAP = 1408
MM_VMEM_BUDGET = 46 * 1024 * 1024
INV_SQRT2 = 1.0 / math.sqrt(2.0)
INV_SQRT_2PI = 1.0 / math.sqrt(2.0 * math.pi)


def _pick(n, cap):
    best = None
    for t in range(LANES, min(n, cap) + 1, LANES):
        if n % t == 0:
            best = t
    return best if best is not None else n


def _params(sem):
    return pltpu.CompilerParams(dimension_semantics=sem, vmem_limit_bytes=VMEM_LIMIT)


def _ordered(body, in_specs, args, after):
    if after is None:
        return body, list(in_specs), tuple(args)
    return (lambda _, *refs: body(*refs)), [pl.BlockSpec(memory_space=pl.ANY), *in_specs], (after, *args)


def _gelu(x):
    return 0.5 * x * (1.0 + lax.erf(x * INV_SQRT2))


def _gelu_grad(x):
    return 0.5 * (1.0 + lax.erf(x * INV_SQRT2)) + x * jnp.exp(-0.5 * x * x) * INV_SQRT_2PI


def _mm(a, b, *, ta=False, tb=False, out_dtype=F32, scale=None, after=None, name):
    halves = isinstance(a, tuple)
    parts = 1 if halves or a.ndim == 2 else a.shape[0]
    ash = (a[0].shape[0], 2 * a[0].shape[1]) if halves else (a.shape if parts == 1
                                                               else (a.shape[1], parts * a.shape[2]))
    if ta:
        K, M = ash
    else:
        M, K = ash
    if tb:
        N, Kb = b.shape
    else:
        Kb, N = b.shape
    assert K == Kb, (ash, b.shape, ta, tb)
    split = 2 if halves else parts
    tm = _pick(M // split if ta else M, MM_TILE_CAP)
    tn = _pick(N, MM_TILE_CAP)
    kspan = K if ta or split == 1 else K // split
    abytes = (a[0] if halves else a).dtype.itemsize * (2 if halves else 1)
    fixed = 2 * tm * tn * jnp.dtype(out_dtype).itemsize + tm * tn * 4
    fits = [t for t in range(LANES, kspan + 1, LANES)
            if kspan % t == 0 and 2 * t * (tm * abytes + tn * b.dtype.itemsize) + fixed <= MM_VMEM_BUDGET]
    tk = max(fits) if fits else _pick(kspan, MM_TILE_CAP)
    nk = K // tk
    nh = (M // split // tm if ta else K // split // tk) if split > 1 else 0
    dn = (((0 if ta else 1,), (1 if tb else 0,)), ((), ()))

    def body(*refs):
        n_tail = 3 if nk > 1 else 2
        a_refs, b_ref, o_ref = refs[:-n_tail], refs[-n_tail], refs[-n_tail + 1]
        k = pl.program_id(2)

        def finish(r):
            if scale is not None:
                r = r * scale
            o_ref[...] = r.astype(out_dtype)

        def accumulate(a_ref):
            part = lax.dot_general(a_ref[...].astype(BF16), b_ref[...].astype(BF16), dn,
                                   preferred_element_type=F32)
            if nk == 1:
                finish(part)
                return
            acc_ref = refs[-1]

            @pl.when(k == 0)
            def _():
                acc_ref[...] = part

            @pl.when((k > 0) & (k < nk - 1))
            def _():
                acc_ref[...] += part

            @pl.when(k == nk - 1)
            def _():
                finish(acc_ref[...] + part)

        if halves:
            first = (pl.program_id(0) if ta else k) < nh
            pl.when(first)(lambda: accumulate(a_refs[0]))
            pl.when(jnp.logical_not(first))(lambda: accumulate(a_refs[1]))
        else:
            accumulate(a_refs[0])

    if halves and ta:
        a_specs = [pl.BlockSpec((tk, tm), lambda i, j, k: (jnp.where(i < nh, k, 0), jnp.minimum(i, nh - 1))),
                   pl.BlockSpec((tk, tm), lambda i, j, k: (jnp.where(i >= nh, k, 0), jnp.maximum(i - nh, 0)))]
    elif halves:
        a_specs = [pl.BlockSpec((tm, tk), lambda i, j, k: (i, jnp.minimum(k, nh - 1))),
                   pl.BlockSpec((tm, tk), lambda i, j, k: (i, jnp.maximum(k - nh, 0)))]
    elif parts > 1:
        a_specs = [pl.BlockSpec((None, tk, tm), lambda i, j, k: (i // nh, k, i % nh)) if ta
                   else pl.BlockSpec((None, tm, tk), lambda i, j, k: (k // nh, i, k % nh))]
    else:
        a_specs = [pl.BlockSpec((tk, tm), lambda i, j, k: (k, i)) if ta
                   else pl.BlockSpec((tm, tk), lambda i, j, k: (i, k))]
    b_spec = (pl.BlockSpec((tn, tk), lambda i, j, k: (j, k)) if tb
              else pl.BlockSpec((tk, tn), lambda i, j, k: (k, j)))
    body, in_specs, args = _ordered(body, [*a_specs, b_spec], (*(a if halves else (a,)), b), after)
    return pl.pallas_call(
        body, grid=(M // tm, N // tn, nk), in_specs=in_specs,
        out_specs=pl.BlockSpec((tm, tn), lambda i, j, k: (i, j)),
        out_shape=jax.ShapeDtypeStruct((M, N), out_dtype),
        scratch_shapes=[pltpu.VMEM((tm, tn), F32)] if nk > 1 else [],
        compiler_params=_params(("parallel", "parallel", "arbitrary")), name=name,
    )(*args)


def _add_ln_fwd(x, h, g, b, *, wide, name):
    T, D = x.shape
    rb = _pick(T, 512)

    def body(x_ref, h_ref, g_ref, b_ref, *o_refs):
        pre = ALPHA * x_ref[...].astype(F32) + h_ref[...].astype(F32)
        mu = jnp.mean(pre, axis=1, keepdims=True)
        cen = pre - mu
        var = jnp.mean(cen * cen, axis=1, keepdims=True)
        y = cen * lax.rsqrt(var + LN_EPS) * g_ref[...] + b_ref[...]
        for o_ref in o_refs:
            o_ref[...] = y.astype(o_ref.dtype)

    row = pl.BlockSpec((rb, D), lambda i: (i, 0))
    vec = pl.BlockSpec((1, D), lambda i: (0, 0))
    dtypes = [F32, BF16] if wide else [BF16]
    return pl.pallas_call(
        body, grid=(T // rb,), in_specs=[row, row, vec, vec], out_specs=[row] * len(dtypes),
        out_shape=[jax.ShapeDtypeStruct((T, D), dt) for dt in dtypes],
        compiler_params=_params(("parallel",)), name=name,
    )(x, h, g.reshape(1, D), b.reshape(1, D))


def _add_ln_bwd(x, h, g, terms, *, after=None, name):
    T, D = x.shape
    rb = _pick(T, 512)
    coefs = [c for c, _ in terms]
    nt = len(terms)

    def body(*refs):
        x_ref, h_ref, g_ref = refs[:3]
        t_refs = refs[3:3 + nt]
        dpb_ref, dg_ref, db_ref = refs[3 + nt:]
        dy = None
        for c, r in zip(coefs, t_refs):
            v = r[...].astype(F32) if c == 1.0 else c * r[...].astype(F32)
            dy = v if dy is None else dy + v
        pre = ALPHA * x_ref[...].astype(F32) + h_ref[...].astype(F32)
        mu = jnp.mean(pre, axis=1, keepdims=True)
        cen = pre - mu
        var = jnp.mean(cen * cen, axis=1, keepdims=True)
        rstd = lax.rsqrt(var + LN_EPS)
        xhat = cen * rstd
        dxh = dy * g_ref[...]
        m1 = jnp.mean(dxh, axis=1, keepdims=True)
        m2 = jnp.mean(dxh * xhat, axis=1, keepdims=True)
        dpre = rstd * (dxh - m1 - xhat * m2)
        dpb_ref[...] = dpre.astype(BF16)
        dg = jnp.sum(dy * xhat, axis=0, keepdims=True)
        db = jnp.sum(dy, axis=0, keepdims=True)

        @pl.when(pl.program_id(0) == 0)
        def _():
            dg_ref[...] = dg
            db_ref[...] = db

        @pl.when(pl.program_id(0) > 0)
        def _():
            dg_ref[...] += dg
            db_ref[...] += db

    row = pl.BlockSpec((rb, D), lambda i: (i, 0))
    vec = pl.BlockSpec((1, D), lambda i: (0, 0))
    body, in_specs, args = _ordered(body, [row, row, vec] + [row] * nt,
                                    (x, h, g.reshape(1, D), *[a for _, a in terms]), after)
    return pl.pallas_call(
        body, grid=(T // rb,), in_specs=in_specs,
        out_specs=[row, vec, vec],
        out_shape=[jax.ShapeDtypeStruct((T, D), BF16),
                   jax.ShapeDtypeStruct((1, D), F32), jax.ShapeDtypeStruct((1, D), F32)],
        compiler_params=_params(("arbitrary",)), name=name,
    )(*args)


def _lincomb(terms, out_dtype, *, name):
    R, C = terms[0][1].shape
    rb = _pick(R, 512)
    coefs = [c for c, _ in terms]
    nt = len(terms)

    def body(*refs):
        acc = None
        for c, r in zip(coefs, refs[:nt]):
            v = r[...].astype(F32)
            v = v if c == 1.0 else c * v
            acc = v if acc is None else acc + v
        refs[nt][...] = acc.astype(out_dtype)

    row = pl.BlockSpec((rb, C), lambda i: (i, 0))
    return pl.pallas_call(
        body, grid=(R // rb,), in_specs=[row] * nt, out_specs=row,
        out_shape=jax.ShapeDtypeStruct((R, C), out_dtype),
        compiler_params=_params(("parallel",)), name=name,
    )(*[a for _, a in terms])


def _loss_grad(y, tgt, *, name):
    T, D = y.shape
    rb = _pick(T, 512)

    def body(y_ref, t_ref, dy_ref, l_ref):
        err = y_ref[...] - t_ref[...]
        dy_ref[...] = err * (1.0 / D)
        part = jnp.sum(jnp.sum(err * err, axis=1, keepdims=True), axis=0, keepdims=True) * (0.5 / D)
        part = jnp.broadcast_to(part, (1, LANES))

        @pl.when(pl.program_id(0) == 0)
        def _():
            l_ref[...] = part

        @pl.when(pl.program_id(0) > 0)
        def _():
            l_ref[...] += part

    row = pl.BlockSpec((rb, D), lambda i: (i, 0))
    return pl.pallas_call(
        body, grid=(T // rb,), in_specs=[row, row],
        out_specs=[row, pl.BlockSpec((1, LANES), lambda i: (0, 0))],
        out_shape=[jax.ShapeDtypeStruct((T, D), F32), jax.ShapeDtypeStruct((1, LANES), F32)],
        compiler_params=_params(("arbitrary",)), name=name,
    )(y, tgt)


def _sgu_fwd(zp, ln_g, ln_b, ws, bst, *, name):
    T, E2 = zp.shape
    E = E2 // 2
    G = ws.shape[0]
    cg = E // G
    rb = 2 * CHUNK

    def body(z_ref, g_ref, b_ref, ws_ref, bs_ref, y_ref):
        u = _gelu(z_ref[:, :E].astype(F32))
        v = _gelu(z_ref[:, E:].astype(F32))
        mu = jnp.mean(v, axis=1, keepdims=True)
        cen = v - mu
        var = jnp.mean(cen * cen, axis=1, keepdims=True)
        vn = (cen * lax.rsqrt(var + LN_EPS) * g_ref[...] + b_ref[...]).astype(BF16)
        for ci in range(rb // CHUNK):
            rows = slice(ci * CHUNK, (ci + 1) * CHUNK)
            for gi in range(G):
                cols = slice(gi * cg, (gi + 1) * cg)
                sv = jnp.dot(ws_ref[gi], vn[rows, cols], preferred_element_type=F32)
                sv = sv + bs_ref[:, gi:gi + 1]
                y_ref[rows, cols] = (u[rows, cols] * sv).astype(BF16)

    return pl.pallas_call(
        body, grid=(T // rb,),
        in_specs=[pl.BlockSpec((rb, E2), lambda i: (i, 0)),
                  pl.BlockSpec((1, E), lambda i: (0, 0)), pl.BlockSpec((1, E), lambda i: (0, 0)),
                  pl.BlockSpec((G, CHUNK, CHUNK), lambda i: (0, 0, 0)),
                  pl.BlockSpec((CHUNK, G), lambda i: (0, 0))],
        out_specs=pl.BlockSpec((rb, E), lambda i: (i, 0)),
        out_shape=jax.ShapeDtypeStruct((T, E), BF16),
        compiler_params=_params(("parallel",)), name=name,
    )(zp, ln_g.reshape(1, E), ln_b.reshape(1, E), ws, bst)


def _sgu_bwd(zp, dy, ln_g, ln_b, ws, wst, bst, *, after=None, name):
    T, E2 = zp.shape
    E = E2 // 2
    G = ws.shape[0]
    cg = E // G
    rb = CHUNK
    nsteps = T // rb

    def body(z_ref, dy_ref, g_ref, b_ref, ws_ref, wst_ref, bs_ref,
             dz_ref, dg_ref, db_ref, dws_ref, dbs_ref, dsv_acc):
        step = pl.program_id(0)

        @pl.when(step == 0)
        def _():
            dg_ref[...] = jnp.zeros_like(dg_ref)
            db_ref[...] = jnp.zeros_like(db_ref)
            dws_ref[...] = jnp.zeros_like(dws_ref)
            dsv_acc[...] = jnp.zeros_like(dsv_acc)

        zu = z_ref[:, :E].astype(F32)
        zv = z_ref[:, E:].astype(F32)
        u = _gelu(zu)
        v = _gelu(zv)
        mu = jnp.mean(v, axis=1, keepdims=True)
        cen = v - mu
        var = jnp.mean(cen * cen, axis=1, keepdims=True)
        rstd = lax.rsqrt(var + LN_EPS)
        xhat = cen * rstd
        vn = (xhat * g_ref[...] + b_ref[...]).astype(BF16)
        dyv = dy_ref[...].astype(F32)
        dsv = dyv * u
        dsv_acc[...] += dsv
        dsvb = dsv.astype(BF16)
        tril = (lax.broadcasted_iota(jnp.int32, (CHUNK, CHUNK), 0)
                >= lax.broadcasted_iota(jnp.int32, (CHUNK, CHUNK), 1))
        du_parts = []
        dvn_parts = []
        for gi in range(G):
            cols = slice(gi * cg, (gi + 1) * cg)
            sv = jnp.dot(ws_ref[gi], vn[:, cols], preferred_element_type=F32) + bs_ref[:, gi:gi + 1]
            du_parts.append(dyv[:, cols] * sv)
            dvn_parts.append(jnp.dot(wst_ref[gi], dsvb[:, cols], preferred_element_type=F32))
            dw = lax.dot_general(dsvb[:, cols], vn[:, cols], (((1,), (1,)), ((), ())),
                                 preferred_element_type=F32)
            dws_ref[gi] += jnp.where(tril, dw, 0.0)
        du = jnp.concatenate(du_parts, axis=1)
        dvn = jnp.concatenate(dvn_parts, axis=1)
        dg_ref[...] += jnp.sum(dvn * xhat, axis=0, keepdims=True)
        db_ref[...] += jnp.sum(dvn, axis=0, keepdims=True)
        dxh = dvn * g_ref[...]
        m1 = jnp.mean(dxh, axis=1, keepdims=True)
        m2 = jnp.mean(dxh * xhat, axis=1, keepdims=True)
        dv = rstd * (dxh - m1 - xhat * m2)
        dz_ref[:, :E] = (du * _gelu_grad(zu)).astype(BF16)
        dz_ref[:, E:] = (dv * _gelu_grad(zv)).astype(BF16)

        @pl.when(step == nsteps - 1)
        def _():
            lane = lax.broadcasted_iota(jnp.int32, (CHUNK, LANES), 1)
            out = jnp.zeros((CHUNK, LANES), F32)
            for gi in range(G):
                s = jnp.sum(dsv_acc[:, gi * cg:(gi + 1) * cg], axis=1, keepdims=True)
                out = jnp.where(lane == gi, s, out)
            dbs_ref[...] = out

    vecE = pl.BlockSpec((1, E), lambda i: (0, 0))
    wspec = pl.BlockSpec((G, CHUNK, CHUNK), lambda i: (0, 0, 0))
    body, in_specs, args = _ordered(
        body, [pl.BlockSpec((rb, E2), lambda i: (i, 0)), pl.BlockSpec((rb, E), lambda i: (i, 0)),
               vecE, vecE, wspec, wspec, pl.BlockSpec((CHUNK, G), lambda i: (0, 0))],
        (zp, dy, ln_g.reshape(1, E), ln_b.reshape(1, E), ws, wst, bst), after)
    return pl.pallas_call(
        body, grid=(nsteps,), in_specs=in_specs,
        out_specs=[pl.BlockSpec((rb, E2), lambda i: (i, 0)), vecE, vecE, wspec,
                   pl.BlockSpec((CHUNK, LANES), lambda i: (0, 0))],
        out_shape=[jax.ShapeDtypeStruct((T, E2), BF16), jax.ShapeDtypeStruct((1, E), F32),
                   jax.ShapeDtypeStruct((1, E), F32), jax.ShapeDtypeStruct((G, CHUNK, CHUNK), F32),
                   jax.ShapeDtypeStruct((CHUNK, LANES), F32)],
        scratch_shapes=[pltpu.VMEM((CHUNK, E), F32)],
        compiler_params=_params(("arbitrary",)), name=name,
    )(*args)


def _shift_down(x, k, row):
    return jnp.where(row >= k, pltpu.roll(x, k, 0), 0.0)


def _shift_up(x, k, row, T):
    return jnp.where(row < T - k, pltpu.roll(x, T - k, 0), 0.0)


def _conv3(x, w_ref, b_ref, row):
    return (w_ref[0:1, :] * _shift_down(x, 2, row) + w_ref[1:2, :] * _shift_down(x, 1, row)
            + w_ref[2:3, :] * x + b_ref[...])


def _convgate_fwd(hh, cw, cb, *, name):
    T, F2 = hh.shape
    F = F2 // 2
    ns = F // LANES

    def body(a_ref, g_ref, wa_ref, wg_ref, ba_ref, bg_ref, o_ref, ca_ref, cg_ref):
        row = lax.broadcasted_iota(jnp.int32, (T, LANES), 0)
        ca = _conv3(a_ref[...].astype(F32), wa_ref, ba_ref, row)
        cgv = _conv3(g_ref[...].astype(F32), wg_ref, bg_ref, row)
        o_ref[...] = (_gelu(ca) * cgv).astype(BF16)
        ca_ref[...] = ca.astype(ACT)
        cg_ref[...] = cgv.astype(ACT)

    sa = lambda r: pl.BlockSpec((r, LANES), lambda j: (0, j))
    sg = lambda r: pl.BlockSpec((r, LANES), lambda j: (0, j + ns))
    return pl.pallas_call(
        body, grid=(ns,), in_specs=[sa(T), sg(T), sa(3), sg(3), sa(1), sg(1)],
        out_specs=[sa(T)] * 3,
        out_shape=[jax.ShapeDtypeStruct((T, F), BF16), jax.ShapeDtypeStruct((T, F), ACT),
                   jax.ShapeDtypeStruct((T, F), ACT)],
        compiler_params=_params(("parallel",)), name=name,
    )(hh, hh, cw, cw, cb, cb)


def _convgate_bwd(hh, hca, hcg, dact, cw, *, name):
    T, F2 = hh.shape
    F = F2 // 2
    ns = F // LANES

    def body(a_ref, g_ref, ca_ref, cg_ref, d_ref, wa_ref, wg_ref,
             da_ref, dg_ref, dwa_ref, dwg_ref, dba_ref, dbg_ref):
        row = lax.broadcasted_iota(jnp.int32, (T, LANES), 0)
        d = d_ref[...].astype(F32)
        ca = ca_ref[...].astype(F32)
        cgv = cg_ref[...].astype(F32)
        cdf = 0.5 * (1.0 + lax.erf(ca * INV_SQRT2))
        dca = d * cgv * (cdf + ca * jnp.exp(-0.5 * ca * ca) * INV_SQRT_2PI)
        dcg = d * (ca * cdf)
        for x_ref, w_ref, dc, dx_ref, dw_ref, db_ref in (
                (a_ref, wa_ref, dca, da_ref, dwa_ref, dba_ref),
                (g_ref, wg_ref, dcg, dg_ref, dwg_ref, dbg_ref)):
            x = x_ref[...].astype(F32)
            up1, up2 = _shift_up(dc, 1, row, T), _shift_up(dc, 2, row, T)
            dx_ref[...] = (w_ref[2:3, :] * dc + w_ref[1:2, :] * up1 + w_ref[0:1, :] * up2).astype(BF16)
            dw_ref[0:1, :] = jnp.sum(up2 * x, axis=0, keepdims=True)
            dw_ref[1:2, :] = jnp.sum(up1 * x, axis=0, keepdims=True)
            dw_ref[2:3, :] = jnp.sum(dc * x, axis=0, keepdims=True)
            db_ref[...] = jnp.sum(dc, axis=0, keepdims=True)

    sa = lambda r: pl.BlockSpec((r, LANES), lambda j: (0, j))
    sg = lambda r: pl.BlockSpec((r, LANES), lambda j: (0, j + ns))
    return pl.pallas_call(
        body, grid=(ns,), in_specs=[sa(T), sg(T), sa(T), sa(T), sa(T), sa(3), sg(3)],
        out_specs=[sa(T), sa(T), sa(3), sa(3), sa(1), sa(1)],
        out_shape=[jax.ShapeDtypeStruct((T, F), BF16), jax.ShapeDtypeStruct((T, F), BF16),
                   jax.ShapeDtypeStruct((3, F), F32), jax.ShapeDtypeStruct((3, F), F32),
                   jax.ShapeDtypeStruct((1, F), F32), jax.ShapeDtypeStruct((1, F), F32)],
        compiler_params=_params(("parallel",)), name=name,
    )(hh, hh, hca, hcg, dact, cw, cw)


def _bucket_maps():
    iq = np.arange(BLK)[:, None]
    ik = np.arange(2 * BLK)[None, :]
    delta = iq + BLK - ik
    maps = []
    for win, dil in DILATED_GROUPS:
        n = np.clip(delta, 0, None) * dil
        max_exact = REL_BUCKETS // 2
        nf = np.maximum(n, 1).astype(np.float32)
        large = max_exact + (np.log(nf / np.float32(max_exact)) / np.float32(math.log(REL_MAX_DIST / max_exact))
                             * np.float32(REL_BUCKETS - max_exact)).astype(np.int32)
        large = np.minimum(large, REL_BUCKETS - 1)
        bucket = np.where(n < max_exact, n, large)
        valid = (delta >= 0) & (delta <= win // dil)
        maps.append(np.where(valid, bucket, -1).astype(np.int32))
    return np.stack(maps)


def _band_bias(rel_table, bmap, H, *, name):
    def body(t_ref, m_ref, o_ref):
        g = pl.program_id(0)
        bm = m_ref[0]
        for h in range(H):
            acc = jnp.full((BLK, 2 * BLK), NEG, F32)
            for b in range(REL_BUCKETS):
                acc = jnp.where(bm == b, t_ref[b, g * H + h], acc)
            o_ref[0, h] = acc

    return pl.pallas_call(
        body, grid=(N_GROUPS,),
        in_specs=[pl.BlockSpec(memory_space=pltpu.SMEM),
                  pl.BlockSpec((1, BLK, 2 * BLK), lambda g: (g, 0, 0))],
        out_specs=pl.BlockSpec((1, H, BLK, 2 * BLK), lambda g: (g, 0, 0, 0)),
        out_shape=jax.ShapeDtypeStruct((N_GROUPS, H, BLK, 2 * BLK), F32),
        compiler_params=_params(("parallel",)), name=name,
    )(rel_table, bmap)


def _band_bias_bwd(dbias, bmap, H, *, name):
    def body(d_ref, m_ref, o_ref):
        bm = m_ref[0]
        rowi = lax.broadcasted_iota(jnp.int32, (REL_BUCKETS, LANES), 0)
        lane = lax.broadcasted_iota(jnp.int32, (REL_BUCKETS, LANES), 1)
        out = jnp.zeros((REL_BUCKETS, LANES), F32)
        for h in range(H):
            dv = d_ref[0, h]
            for b in range(REL_BUCKETS):
                s = jnp.sum(jnp.sum(jnp.where(bm == b, dv, 0.0), axis=1, keepdims=True),
                            axis=0, keepdims=True)
                out = jnp.where((rowi == b) & (lane == h), s, out)
        o_ref[0] = out

    return pl.pallas_call(
        body, grid=(N_GROUPS,),
        in_specs=[pl.BlockSpec((1, H, BLK, 2 * BLK), lambda g: (g, 0, 0, 0)),
                  pl.BlockSpec((1, BLK, 2 * BLK), lambda g: (g, 0, 0))],
        out_specs=pl.BlockSpec((1, REL_BUCKETS, LANES), lambda g: (g, 0, 0)),
        out_shape=jax.ShapeDtypeStruct((N_GROUPS, REL_BUCKETS, LANES), F32),
        compiler_params=_params(("parallel",)), name=name,
    )(dbias, bmap)


def _head_masks():
    lane = lax.broadcasted_iota(jnp.int32, (BLK, LANES), 1)
    return (lane < HEAD_DIM, lane >= HEAD_DIM)


def _attn_fwd(q, kv, bias, gi, *, name):
    T = q.shape[0]
    HD = kv.shape[1] // 2
    d = DILATED_GROUPS[gi][1]
    S = T // d
    NB = S // BLK
    H = HD // HEAD_DIM
    qv, qcol = (q, gi) if d == 1 else (q[:, gi * HD:(gi + 1) * HD].reshape(S, d * HD), 0)
    kvv = kv.reshape(S, d * 2 * HD)

    def body(q_ref, kp_ref, kc_ref, vp_ref, vc_ref, b_ref, o_ref, l_ref):
        n = pl.program_id(1)
        col = lax.broadcasted_iota(jnp.int32, (BLK, 2 * BLK), 1)
        first = (n == 0) & (col < BLK)
        hm = _head_masks()
        for p in range(HD // LANES):
            sl = slice(p * LANES, (p + 1) * LANES)
            qp = q_ref[:, sl]
            kc = jnp.concatenate([kp_ref[:, sl], kc_ref[:, sl]], axis=0)
            vc = jnp.concatenate([vp_ref[:, sl], vc_ref[:, sl]], axis=0)
            outs = []
            lses = []
            for hh in range(2):
                qm = jnp.where(hm[hh], qp, jnp.zeros_like(qp))
                s = lax.dot_general(qm, kc, (((1,), (1,)), ((), ())), preferred_element_type=F32)
                s = jnp.where(first, NEG, s + b_ref[2 * p + hh])
                m = jnp.max(s, axis=1, keepdims=True)
                e = jnp.exp(s - m)
                den = jnp.sum(e, axis=1, keepdims=True)
                outs.append(jnp.dot((e / den).astype(BF16), vc, preferred_element_type=F32))
                lses.append(m + jnp.log(den))
            o_ref[:, sl] = jnp.where(hm[0], outs[0], outs[1])
            l_ref[:, sl] = jnp.where(hm[0], lses[0], lses[1])

    blk = lambda f: pl.BlockSpec((BLK, HD), f)
    prev = lambda n: jnp.maximum(n - 1, 0)
    return pl.pallas_call(
        body, grid=(d, NB),
        in_specs=[blk(lambda r, n: (n, r + qcol)),
                  blk(lambda r, n: (prev(n), r * 2)), blk(lambda r, n: (n, r * 2)),
                  blk(lambda r, n: (prev(n), r * 2 + 1)), blk(lambda r, n: (n, r * 2 + 1)),
                  pl.BlockSpec((H, BLK, 2 * BLK), lambda r, n: (0, 0, 0))],
        out_specs=[blk(lambda r, n: (n, r)), blk(lambda r, n: (n, r))],
        out_shape=[jax.ShapeDtypeStruct((S, d * HD), F32), jax.ShapeDtypeStruct((S, d * HD), F32)],
        compiler_params=_params(("parallel", "parallel")), name=name,
    )(qv, kvv, kvv, kvv, kvv, bias)


def _attn_combine(os, ls, *, name):
    T, HD = os[0].shape
    rb = _pick(T, 512)

    def body(o0, o1, o2, l0, l1, l2, o_ref, ob_ref, l_ref):
        la, lb, lc = l0[...], l1[...], l2[...]
        m = jnp.maximum(jnp.maximum(la, lb), lc)
        L = m + jnp.log(jnp.exp(la - m) + jnp.exp(lb - m) + jnp.exp(lc - m))
        o = jnp.exp(la - L) * o0[...] + jnp.exp(lb - L) * o1[...] + jnp.exp(lc - L) * o2[...]
        o_ref[...] = o
        ob_ref[...] = o.astype(BF16)
        l_ref[...] = L

    row = pl.BlockSpec((rb, HD), lambda i: (i, 0))
    return pl.pallas_call(
        body, grid=(T // rb,), in_specs=[row] * 6, out_specs=[row] * 3,
        out_shape=[jax.ShapeDtypeStruct((T, HD), F32), jax.ShapeDtypeStruct((T, HD), BF16),
                   jax.ShapeDtypeStruct((T, HD), F32)],
        compiler_params=_params(("parallel",)), name=name,
    )(*[a.reshape(T, HD) for a in os], *[a.reshape(T, HD) for a in ls])


def _attn_bwd(q, kv, bias, do, o, L, gi, *, name):
    T = q.shape[0]
    HD = kv.shape[1] // 2
    d = DILATED_GROUPS[gi][1]
    S = T // d
    NB = S // BLK
    H = HD // HEAD_DIM
    qv, qcol = (q, gi) if d == 1 else (q[:, gi * HD:(gi + 1) * HD].reshape(S, d * HD), 0)
    kvv = kv.reshape(S, d * 2 * HD)
    dov, ov, Lv = (a.reshape(S, d * HD) for a in (do, o, L))

    def body(q_ref, kp_ref, kc_ref, vp_ref, vc_ref, b_ref, do_ref, o_ref, L_ref,
             dq_ref, dk_ref, dv_ref, db_ref, ck_ref, cv_ref):
        r = pl.program_id(0)
        n = pl.program_id(1)

        @pl.when((r == 0) & (n == 0))
        def _():
            db_ref[...] = jnp.zeros_like(db_ref)

        @pl.when(n == 0)
        def _():
            ck_ref[...] = jnp.zeros_like(ck_ref)
            cv_ref[...] = jnp.zeros_like(cv_ref)

        @pl.when(n < NB)
        def _():
            col = lax.broadcasted_iota(jnp.int32, (BLK, 2 * BLK), 1)
            first = (n == 0) & (col < BLK)
            hm = _head_masks()
            for p in range(HD // LANES):
                sl = slice(p * LANES, (p + 1) * LANES)
                qp = q_ref[:, sl]
                kc = jnp.concatenate([kp_ref[:, sl], kc_ref[:, sl]], axis=0)
                vc = jnp.concatenate([vp_ref[:, sl], vc_ref[:, sl]], axis=0)
                dop = do_ref[:, sl]
                dob = dop.astype(BF16)
                prod = dop * o_ref[:, sl]
                Lp = L_ref[:, sl]
                dq_parts = []
                dkc = None
                dvc = None
                for hh in range(2):
                    qm = jnp.where(hm[hh], qp, jnp.zeros_like(qp))
                    dom = jnp.where(hm[hh], dob, jnp.zeros_like(dob))
                    s = lax.dot_general(qm, kc, (((1,), (1,)), ((), ())), preferred_element_type=F32)
                    s = jnp.where(first, NEG, s + b_ref[2 * p + hh])
                    lse = Lp[:, hh * HEAD_DIM:hh * HEAD_DIM + 1]
                    pr = jnp.exp(s - lse)
                    dp = lax.dot_general(dom, vc, (((1,), (1,)), ((), ())), preferred_element_type=F32)
                    delta = jnp.sum(jnp.where(hm[hh], prod, 0.0), axis=1, keepdims=True)
                    ds = pr * (dp - delta)
                    db_ref[2 * p + hh] += ds
                    dsb = ds.astype(BF16)
                    dq_parts.append(jnp.dot(dsb, kc, preferred_element_type=F32))
                    dkh = lax.dot_general(dsb, qm, (((0,), (0,)), ((), ())), preferred_element_type=F32)
                    dvh = lax.dot_general(pr.astype(BF16), dom, (((0,), (0,)), ((), ())),
                                          preferred_element_type=F32)
                    dkc = dkh if dkc is None else dkc + dkh
                    dvc = dvh if dvc is None else dvc + dvh
                dq = jnp.where(hm[0], dq_parts[0], dq_parts[1])
                dq_ref[:, sl] = (dq * (HEAD_DIM ** -0.5)).astype(BF16)
                dk_ref[:, sl] = ck_ref[:, sl] + dkc[:BLK]
                dv_ref[:, sl] = cv_ref[:, sl] + dvc[:BLK]
                ck_ref[:, sl] = dkc[BLK:]
                cv_ref[:, sl] = dvc[BLK:]

        @pl.when(n == NB)
        def _():
            dk_ref[...] = ck_ref[...]
            dv_ref[...] = cv_ref[...]

    blk = lambda f: pl.BlockSpec((BLK, HD), f)
    cur = lambda n: jnp.minimum(n, NB - 1)
    prev = lambda n: jnp.maximum(jnp.minimum(n, NB - 1) - 1, 0)
    lag = lambda n: jnp.maximum(n - 1, 0)
    return pl.pallas_call(
        body, grid=(d, NB + 1),
        in_specs=[blk(lambda r, n: (cur(n), r + qcol)),
                  blk(lambda r, n: (prev(n), r * 2)), blk(lambda r, n: (cur(n), r * 2)),
                  blk(lambda r, n: (prev(n), r * 2 + 1)), blk(lambda r, n: (cur(n), r * 2 + 1)),
                  pl.BlockSpec((H, BLK, 2 * BLK), lambda r, n: (0, 0, 0)),
                  blk(lambda r, n: (cur(n), r)), blk(lambda r, n: (cur(n), r)),
                  blk(lambda r, n: (cur(n), r))],
        out_specs=[blk(lambda r, n: (cur(n), r)), blk(lambda r, n: (lag(n), r)),
                   blk(lambda r, n: (lag(n), r)),
                   pl.BlockSpec((H, BLK, 2 * BLK), lambda r, n: (0, 0, 0))],
        out_shape=[jax.ShapeDtypeStruct((S, d * HD), BF16), jax.ShapeDtypeStruct((S, d * HD), F32),
                   jax.ShapeDtypeStruct((S, d * HD), F32),
                   jax.ShapeDtypeStruct((H, BLK, 2 * BLK), F32)],
        scratch_shapes=[pltpu.VMEM((BLK, HD), F32), pltpu.VMEM((BLK, HD), F32)],
        compiler_params=_params(("arbitrary", "arbitrary")), name=name,
    )(qv, kvv, kvv, kvv, kvv, bias, dov, ov, Lv)


SUPER = DILATED_GROUPS[-1][1] * BLK


def _band_rows(it, d):
    r, j = it % d, it // d
    if d == 1:
        at = lambda blk: pl.ds(pl.multiple_of(blk * BLK, BLK), BLK)
    else:
        at = lambda blk: pl.ds(r + d * BLK * blk, BLK, stride=d)
    return at(j), at(jnp.maximum(j - 1, 0))


def _stack_heads(x, hm):
    zero = jnp.zeros_like(x)
    return jnp.concatenate([jnp.where(hm[0], x, zero), jnp.where(hm[1], x, zero)], axis=0)


def _band_loops(step, d, unroll):
    n_it = SUPER // BLK

    def run(lo, hi, inside):
        if hi > lo:
            def body(it, carry):
                step(it, inside)
                return carry
            lax.fori_loop(lo, hi, body, 0, unroll=max(u for u in range(1, unroll + 1) if (hi - lo) % u == 0))

    run(0, d, False)
    run(d, n_it, True)


def _last_rows(it, d):
    m = SUPER // (d * BLK)
    if d == 1:
        return pl.ds((m - 1) * BLK, BLK)
    return pl.ds(it % d + d * BLK * (m - 1), BLK, stride=d)


def _attn_fwd_all(q, kv, bias, *, name):
    T = q.shape[0]
    HD = kv.shape[1] // 2
    PP = HD // LANES
    NS = T // SUPER

    def body(q0, q1, q2, kp_ref, kc_ref, vp_ref, vc_ref, b_ref, o_ref, ob_ref, l_ref, og, lg):
        n = pl.program_id(1)
        col = lax.broadcasted_iota(jnp.int32, (2 * BLK, 2 * BLK), 1)
        hm = _head_masks()
        for g, (q_ref, (_, d)) in enumerate(zip((q0, q1, q2), DILATED_GROUPS)):
            def step(it, inside, g=g, q_ref=q_ref, d=d):
                cur, prv = _band_rows(it, d)
                qp = q_ref[cur, :].astype(BF16)
                if inside:
                    kprev, vprev = kc_ref[prv, :], vc_ref[prv, :]
                else:
                    last = _last_rows(it, d)
                    kprev, vprev = kp_ref[last, :], vp_ref[last, :]
                kc = jnp.concatenate([kprev.astype(BF16), kc_ref[cur, :].astype(BF16)], axis=0)
                vc = jnp.concatenate([vprev.astype(BF16), vc_ref[cur, :].astype(BF16)], axis=0)
                s = lax.dot_general(_stack_heads(qp, hm), kc, (((1,), (1,)), ((), ())),
                                    preferred_element_type=F32)
                s = s + b_ref[g].reshape(2 * BLK, 2 * BLK)
                if not inside:
                    s = jnp.where((n == 0) & (col < BLK), NEG, s)
                mx = jnp.max(s, axis=1, keepdims=True)
                e = jnp.exp(s - mx)
                den = jnp.sum(e, axis=1, keepdims=True)
                out = jnp.dot((e / den).astype(BF16), vc, preferred_element_type=F32)
                lse = mx + jnp.log(den)
                og.at[g][cur, :] = jnp.where(hm[0], out[:BLK], out[BLK:])
                lg.at[g][cur, :] = jnp.where(hm[0], lse[:BLK], lse[BLK:])

            _band_loops(step, d, 8)
        la, lb, lc = lg[0], lg[1], lg[2]
        mx = jnp.maximum(jnp.maximum(la, lb), lc)
        L = mx + jnp.log(jnp.exp(la - mx) + jnp.exp(lb - mx) + jnp.exp(lc - mx))
        o = jnp.exp(la - L) * og[0] + jnp.exp(lb - L) * og[1] + jnp.exp(lc - L) * og[2]
        o_ref[...] = o
        ob_ref[...] = o.astype(BF16)
        l_ref[...] = L

    blk = lambda f: pl.BlockSpec((SUPER, LANES), f)
    prev = lambda n: jnp.maximum(n - 1, 0)
    qspec = lambda g: blk(lambda p, n: (n, g * PP + p))
    return pl.pallas_call(
        body, grid=(PP, NS),
        in_specs=[qspec(0), qspec(1), qspec(2),
                  blk(lambda p, n: (prev(n), p)), blk(lambda p, n: (n, p)),
                  blk(lambda p, n: (prev(n), PP + p)), blk(lambda p, n: (n, PP + p)),
                  pl.BlockSpec((N_GROUPS, 2, BLK, 2 * BLK), lambda p, n: (0, p, 0, 0))],
        out_specs=[blk(lambda p, n: (n, p))] * 3,
        out_shape=[jax.ShapeDtypeStruct((T, HD), F32), jax.ShapeDtypeStruct((T, HD), BF16),
                   jax.ShapeDtypeStruct((T, HD), F32)],
        scratch_shapes=[pltpu.VMEM((N_GROUPS, SUPER, LANES), F32), pltpu.VMEM((N_GROUPS, SUPER, LANES), F32)],
        compiler_params=_params(("parallel", "parallel")), name=name,
    )(q, q, q, kv, kv, kv, kv, bias)


def _attn_bwd_all(q, kv, bias, do, o, L, *, after=None, name):
    T = q.shape[0]
    HD = kv.shape[1] // 2
    PP = HD // LANES
    H = HD // HEAD_DIM
    NS = T // SUPER

    def body(q0, q1, q2, kp_ref, kc_ref, vp_ref, vc_ref, b_ref, do_ref, o_ref, L_ref,
             dq_ref, dk_ref, dv_ref, db_ref, ck_ref, cv_ref):
        n = pl.program_id(1)

        @pl.when(n == 0)
        def _():
            db_ref[...] = jnp.zeros_like(db_ref)
            ck_ref[...] = jnp.zeros_like(ck_ref)
            cv_ref[...] = jnp.zeros_like(cv_ref)

        dk_ref[...] = ck_ref[...]
        dv_ref[...] = cv_ref[...]
        ck_ref[...] = jnp.zeros_like(ck_ref)
        cv_ref[...] = jnp.zeros_like(cv_ref)

        @pl.when(n < NS)
        def _():
            col = lax.broadcasted_iota(jnp.int32, (2 * BLK, 2 * BLK), 1)
            hm = _head_masks()
            for g, (q_ref, (_, d)) in enumerate(zip((q0, q1, q2), DILATED_GROUPS)):
                def step(it, inside, g=g, q_ref=q_ref, d=d):
                    cur, prv = _band_rows(it, d)
                    last = _last_rows(it, d)
                    qp = q_ref[cur, :].astype(BF16)
                    if inside:
                        kprev, vprev = kc_ref[prv, :], vc_ref[prv, :]
                    else:
                        kprev, vprev = kp_ref[last, :], vp_ref[last, :]
                    kc = jnp.concatenate([kprev.astype(BF16), kc_ref[cur, :].astype(BF16)], axis=0)
                    vc = jnp.concatenate([vprev.astype(BF16), vc_ref[cur, :].astype(BF16)], axis=0)
                    dop = do_ref[cur, :]
                    prod = dop * o_ref[cur, :]
                    Lp = L_ref[cur, :]
                    qs = _stack_heads(qp, hm)
                    dos = _stack_heads(dop.astype(BF16), hm)
                    lse = jnp.concatenate([Lp[:, 0:1], Lp[:, HEAD_DIM:HEAD_DIM + 1]], axis=0)
                    delta = jnp.concatenate([jnp.sum(jnp.where(hm[0], prod, 0.0), axis=1, keepdims=True),
                                             jnp.sum(jnp.where(hm[1], prod, 0.0), axis=1, keepdims=True)], axis=0)
                    s = lax.dot_general(qs, kc, (((1,), (1,)), ((), ())), preferred_element_type=F32)
                    s = s + b_ref[g].reshape(2 * BLK, 2 * BLK)
                    if not inside:
                        s = jnp.where((n == 0) & (col < BLK), NEG, s)
                    pr = jnp.exp(s - lse)
                    dp = lax.dot_general(dos, vc, (((1,), (1,)), ((), ())), preferred_element_type=F32)
                    ds = pr * (dp - delta)
                    db_ref[g] += ds.reshape(2, BLK, 2 * BLK)
                    dsb = ds.astype(BF16)
                    dqs = jnp.dot(dsb, kc, preferred_element_type=F32)
                    dkc = lax.dot_general(dsb, qs, (((0,), (0,)), ((), ())), preferred_element_type=F32)
                    dvc = lax.dot_general(pr.astype(BF16), dos, (((0,), (0,)), ((), ())),
                                          preferred_element_type=F32)
                    dq_ref.at[g][cur, :] = jnp.where(hm[0], dqs[:BLK], dqs[BLK:]) * (HEAD_DIM ** -0.5)
                    ck_ref[cur, :] += dkc[BLK:]
                    cv_ref[cur, :] += dvc[BLK:]
                    if inside:
                        ck_ref[prv, :] += dkc[:BLK]
                        cv_ref[prv, :] += dvc[:BLK]
                    else:
                        dk_ref[last, :] += dkc[:BLK]
                        dv_ref[last, :] += dvc[:BLK]

                _band_loops(step, d, 4)

    blk = lambda f: pl.BlockSpec((SUPER, LANES), f)
    cur = lambda n: jnp.minimum(n, NS - 1)
    prev = lambda n: jnp.maximum(jnp.minimum(n, NS - 1) - 1, 0)
    lag = lambda n: jnp.maximum(n - 1, 0)
    qspec = lambda g: blk(lambda p, n: (cur(n), g * PP + p))
    bspec = pl.BlockSpec((N_GROUPS, 2, BLK, 2 * BLK), lambda p, n: (0, p, 0, 0))
    body, in_specs, args = _ordered(
        body, [qspec(0), qspec(1), qspec(2),
               blk(lambda p, n: (prev(n), p)), blk(lambda p, n: (cur(n), p)),
               blk(lambda p, n: (prev(n), PP + p)), blk(lambda p, n: (cur(n), PP + p)),
               bspec, blk(lambda p, n: (cur(n), p)), blk(lambda p, n: (cur(n), p)),
               blk(lambda p, n: (cur(n), p))],
        (q, q, q, kv, kv, kv, kv, bias, do, o, L), after)
    return pl.pallas_call(
        body, grid=(PP, NS + 1), in_specs=in_specs,
        out_specs=[pl.BlockSpec((N_GROUPS, SUPER, LANES), lambda p, n: (0, cur(n), p)),
                   blk(lambda p, n: (lag(n), p)), blk(lambda p, n: (lag(n), p)), bspec],
        out_shape=[jax.ShapeDtypeStruct((N_GROUPS, T, HD), F32), jax.ShapeDtypeStruct((T, HD), F32),
                   jax.ShapeDtypeStruct((T, HD), F32),
                   jax.ShapeDtypeStruct((N_GROUPS, H, BLK, 2 * BLK), F32)],
        scratch_shapes=[pltpu.VMEM((SUPER, LANES), F32), pltpu.VMEM((SUPER, LANES), F32)],
        compiler_params=_params(("arbitrary", "arbitrary")), name=name,
    )(*args)


class _Weights(dict):
    def __init__(self, base, fetch=None, emit=None, emit_small=None):
        super().__init__(base)
        self._fetch, self._emit, self._emit_small = fetch, emit, emit_small

    def fetch(self, group, after):
        if self._fetch is not None:
            for (key, layer), mat in self._fetch(group, after).items():
                self[key][layer] = mat

    def emit(self, group, mats):
        return None if self._emit is None else self._emit(group, mats)

    def emit_small(self, grads):
        return None if self._emit_small is None else self._emit_small(grads)


def _local_step(x, tgt, W):
    T, D = x.shape
    H = W["rel_table"].shape[1] // N_GROUPS
    HD = H * HEAD_DIM
    G = W["a_w_s"].shape[1]
    assert T % (DILATED_GROUPS[-1][1] * BLK) == 0

    tril = jnp.tril(jnp.ones((CHUNK, CHUNK), F32))
    bmap = jnp.asarray(_bucket_maps())
    bias = _band_bias(W["rel_table"], bmap, H, name="band_bias")

    saved = []
    xc, xcb = x, x.astype(BF16)
    kvb = None
    for i in range(DEPTH):
        s = {"x": xc, "xb": xcb}
        W.fetch(4 * i, xc)
        if i < N_A:
            ws_m = W["a_w_s"][i] * tril
            s["ws"] = ws_m.astype(BF16)
            s["wst"] = jnp.swapaxes(ws_m, 1, 2).astype(BF16)
            s["bst"] = W["a_b_s"][i].T
            s["zp"] = _mm(xcb, W["a_w_in"][i], out_dtype=ACT, name=f"a_in_{i}")
            s["y"] = _sgu_fwd(s["zp"], W["a_ln_g"][i], W["a_ln_b"][i], s["ws"], s["bst"], name=f"sgu_fwd_{i}")
            W.fetch(4 * i + 1, s["zp"])
            s["h"] = _mm(s["y"], W["a_w_out"][i], out_dtype=ACT, name=f"a_out_{i}")
        else:
            j = i - N_A
            if kvb is None:
                kvb = _mm(xcb, W["kv_w"][0], name="kv_proj")
            s["q"] = _mm(xcb, W["b_w_q_t"][j], tb=True, scale=HEAD_DIM ** -0.5, name=f"q_proj_{j}")
            s["o"], s["ob"], s["L"] = _attn_fwd_all(s["q"], kvb, bias, name=f"attn_fwd_{j}")
            W.fetch(4 * i + 1, s["q"])
            s["h"] = _mm(s["ob"], W["b_w_o"][j], out_dtype=ACT, name=f"o_proj_{j}")
        s["x1b"], = _add_ln_fwd(xc, s["h"], W["ln_g"][i, 0], W["ln_b"][i, 0], wide=False, name=f"ln1_fwd_{i}")
        s["x1"] = s["x1b"]
        W.fetch(4 * i + 2, s["x1"])
        s["hh"] = _mm(s["x1b"], W["ffn_w_up_t"][i], tb=True, out_dtype=ACT, name=f"ffn_up_{i}")
        s["cw"] = W["ffn_conv_w"][i]
        s["cb"] = W["ffn_conv_b"][i].reshape(1, -1)
        s["act"], s["hca"], s["hcg"] = _convgate_fwd(s["hh"], s["cw"], s["cb"], name=f"convgate_fwd_{i}")
        W.fetch(4 * i + 3, s["hh"])
        s["f"] = _mm(s["act"], W["ffn_w_down"][i], out_dtype=ACT, name=f"ffn_down_{i}")
        outs = _add_ln_fwd(s["x1"], s["f"], W["ln_g"][i, 1], W["ln_b"][i, 1], wide=i == DEPTH - 1,
                           name=f"ln2_fwd_{i}")
        xc, xcb = outs[0], outs[-1]
        saved.append(s)

    dy, lossv = _loss_grad(xc, tgt, name="loss_grad")
    loss = lossv[0, 0]

    gl = {k: [None] * DEPTH for k in ("ffn_w_up_t", "ffn_conv_w", "ffn_conv_b", "ffn_w_down", "ln_g", "ln_b")}
    ga = {k: [None] * N_A for k in ("a_w_in", "a_ln_g", "a_ln_b", "a_w_s", "a_b_s", "a_w_out")}
    gb = {k: [None] * (DEPTH - N_A) for k in ("b_w_q_t", "b_w_o")}
    mats = ("a_w_in", "a_w_out", "b_w_q_t", "b_w_o", "ffn_w_up_t", "ffn_w_down")
    dks, dvs, dbias = [], [], []
    grads = {}
    terms = [(1.0, dy)]
    tok = None
    small_keys = ("ffn_conv_w", "ffn_conv_b", "ln_g", "ln_b", "a_ln_g", "a_ln_b", "a_w_s", "a_b_s")
    for i in reversed(range(DEPTH)):
        s = saved[i]
        dp2b, dg2, db2 = _add_ln_bwd(s["x1"], s["f"], W["ln_g"][i, 1], terms, after=tok, name=f"ln2_bwd_{i}")
        dact = _mm(dp2b, W["ffn_w_down"][i], tb=True, out_dtype=ACT, name=f"ffn_down_dx_{i}")
        gl["ffn_w_down"][i] = _mm(s["act"], dp2b, ta=True, out_dtype=BF16, name=f"ffn_down_dw_{i}")
        dha, dhg, dwa, dwg, dba, dbg = _convgate_bwd(s["hh"], s["hca"], s["hcg"], dact, s["cw"],
                                                     name=f"convgate_bwd_{i}")
        dhh = (dha, dhg)
        gl["ffn_conv_w"][i] = jnp.concatenate([dwa, dwg], axis=1)
        gl["ffn_conv_b"][i] = jnp.concatenate([dba, dbg], axis=1)[0]
        dx1 = _mm(dhh, W["ffn_w_up_t"][i], out_dtype=ACT, name=f"ffn_up_dx_{i}")
        gl["ffn_w_up_t"][i] = _mm(dhh, s["x1b"], ta=True, out_dtype=BF16, name=f"ffn_up_dw_{i}")
        tok = W.emit(3 * i + 2, {("ffn_w_up_t", i): gl["ffn_w_up_t"][i], ("ffn_w_down", i): gl["ffn_w_down"][i]})
        dp1b, dg1, db1 = _add_ln_bwd(s["x"], s["h"], W["ln_g"][i, 0], [(ALPHA, dp2b), (1.0, dx1)],
                                     after=tok, name=f"ln1_bwd_{i}")
        gl["ln_g"][i] = jnp.concatenate([dg1, dg2], axis=0)
        gl["ln_b"][i] = jnp.concatenate([db1, db2], axis=0)
        terms = [(ALPHA, dp1b)]
        if i < N_A:
            dyy = _mm(dp1b, W["a_w_out"][i], tb=True, out_dtype=ACT, name=f"a_out_dx_{i}")
            ga["a_w_out"][i] = _mm(s["y"], dp1b, ta=True, out_dtype=BF16, name=f"a_out_dw_{i}")
            if i == 0:
                tok = W.emit(3 * i + 1, {("a_w_out", i): ga["a_w_out"][i]})
            dzp, dlg, dlb, dws, dbs = _sgu_bwd(s["zp"], dyy, W["a_ln_g"][i], W["a_ln_b"][i], s["ws"],
                                               s["wst"], s["bst"], after=tok, name=f"sgu_bwd_{i}")
            ga["a_ln_g"][i], ga["a_ln_b"][i], ga["a_w_s"][i] = dlg[0], dlb[0], dws
            ga["a_b_s"][i] = dbs[:, :G].T
            if i == 0:
                for dct in (gl, ga):
                    grads.update({k: jnp.stack(v) for k, v in dct.items() if k in small_keys})
                tok = W.emit_small(grads)
            ga["a_w_in"][i] = _mm(s["xb"], dzp, ta=True, out_dtype=BF16, after=tok, name=f"a_in_dw_{i}")
            out_a = {("a_w_in", i): ga["a_w_in"][i]}
            if i > 0:
                out_a[("a_w_out", i)] = ga["a_w_out"][i]
            tok = W.emit(3 * i, out_a)
            terms.append((1.0, _mm(dzp, W["a_w_in"][i], tb=True, out_dtype=ACT, after=tok, name=f"a_in_dx_{i}")))
        else:
            j = i - N_A
            do = _mm(dp1b, W["b_w_o"][j], tb=True, name=f"o_proj_dx_{j}")
            gb["b_w_o"][j] = _mm(s["ob"], dp1b, ta=True, out_dtype=BF16, name=f"o_proj_dw_{j}")
            dq, dk_j, dv_j, db_j = _attn_bwd_all(s["q"], kvb, bias, do, s["o"], s["L"], after=tok,
                                                 name=f"attn_bwd_{j}")
            dks.append((1.0, dk_j))
            dvs.append((1.0, dv_j))
            dbias.append(db_j)
            terms.append((1.0, _mm(dq, W["b_w_q_t"][j], out_dtype=ACT, name=f"q_proj_dx_{j}")))
            gb["b_w_q_t"][j] = _mm(dq, s["xb"], ta=True, out_dtype=BF16, name=f"q_proj_dw_{j}")
            out_b = {("b_w_q_t", j): gb["b_w_q_t"][j], ("b_w_o", j): gb["b_w_o"][j]}
            if i == N_A:
                dkv = jnp.concatenate([_lincomb(dks, BF16, name="dk_sum"), _lincomb(dvs, BF16, name="dv_sum")],
                                      axis=1)
                terms.append((1.0, _mm(dkv, W["kv_w"][0], tb=True, out_dtype=ACT, name="kv_proj_dx")))
                grads["kv_w"] = [_mm(s["xb"], dkv, ta=True, out_dtype=BF16, name="kv_proj_dw")]
                out_b[("kv_w", 0)] = grads["kv_w"][0]
                dbt = _lincomb([(1.0, a.reshape(-1, 2 * BLK)) for a in dbias], F32, name="dbias_sum")
                dtab = _band_bias_bwd(dbt.reshape(N_GROUPS, H, BLK, 2 * BLK), bmap, H, name="band_bias_bwd")
                grads["rel_table"] = jnp.transpose(dtab[:, :, :H], (1, 0, 2)).reshape(REL_BUCKETS, N_GROUPS * H)
            tok = W.emit(3 * i, out_b)
    grad_x = _lincomb(terms, F32, name="grad_x")
    for dct in (gl, ga, gb):
        grads.update({k: v for k, v in dct.items() if k in mats})
    return loss, grad_x, grads


def _my_index():
    return 4 * lax.axis_index("x") + 2 * lax.axis_index("y") + lax.axis_index("c")


HBM_SPEC = pl.BlockSpec(memory_space=pltpu.HBM)


def _block(ref, k, n, axis):
    off = pl.multiple_of(k * n, n)
    return ref.at[pl.ds(off, n), :] if axis == 0 else ref.at[:, pl.ds(off, n)]


def _gather_mats(local, axis, *, name):
    L, a, b = local.shape
    n = a if axis == 0 else b
    full = (a * N_DEV, b) if axis == 0 else (a, b * N_DEV)

    def body(x_ref, *rest):
        outs = rest[:L]
        send_sems, recv_sems, local_sems = rest[L:]
        x, y, c = lax.axis_index("x"), lax.axis_index("y"), lax.axis_index("c")
        me, sibling = (x, y, c), (x, y, 1 - c)
        chips = [(1 - x, y), (x, 1 - y), (1 - x, 1 - y)]

        def slot(l, px, py, pc):
            return _block(outs[l], 4 * px + 2 * py + pc, n, axis)

        def copy(l, k, blk, to, src=None):
            return pltpu.make_async_remote_copy(
                src_ref=slot(l, *blk) if src is None else src, dst_ref=slot(l, *blk),
                send_sem=send_sems.at[7 * l + k], recv_sem=recv_sems.at[7 * l + k],
                device_id=to, device_id_type=MESH)

        mine, first, passed = [], [], []
        for l in range(L):
            mine.append(pltpu.make_async_copy(x_ref.at[l], slot(l, *me), local_sems.at[l]))
            mine[-1].start()
            first.append(copy(l, 0, me, sibling, src=x_ref.at[l]))
            first += [copy(l, 1 + j, me, (*chip, c), src=x_ref.at[l]) for j, chip in enumerate(chips)]
        for cp in first:
            cp.start()
        for l in range(L):
            for j, chip in enumerate(chips):
                copy(l, 1 + j, (*chip, c), me).wait_recv()
                passed.append(copy(l, 4 + j, (*chip, c), sibling))
                passed[-1].start()
        for l in range(L):
            copy(l, 0, sibling, me).wait_recv()
            for j, chip in enumerate(chips):
                copy(l, 4 + j, (*chip, 1 - c), me).wait_recv()
        for cp in first + passed:
            cp.wait_send()
        for cp in mine:
            cp.wait()

    return pl.pallas_call(
        body, out_shape=[jax.ShapeDtypeStruct(full, local.dtype)] * L,
        in_specs=[HBM_SPEC], out_specs=[HBM_SPEC] * L,
        scratch_shapes=[pltpu.SemaphoreType.DMA((7 * L,)), pltpu.SemaphoreType.DMA((7 * L,)),
                        pltpu.SemaphoreType.DMA((L,))],
        name=name,
    )(local)


SEM_SPEC = pl.BlockSpec(memory_space=pltpu.SEMAPHORE)
FLOWING = pltpu.SideEffectType.DATAFLOW_SIDE_EFFECTING


def _peers(x, y, c):
    return [(1 - x if k & 4 else x, 1 - y if k & 2 else y, 1 - c if k & 1 else c) for k in range(1, N_DEV)]


def _ends(src_ref, land_ref, peer_index, me, n, axis, gather):
    if gather:
        return src_ref, _block(land_ref, me, n, axis)
    return _block(src_ref, peer_index, n, axis), land_ref.at[me]


def _send_start(groups, gather, *, name):
    flat = [(g, j, mat, axis) for g, items in enumerate(groups) for j, (mat, axis) in enumerate(items)]
    M, G = len(flat), len(groups)
    lands, ns = [], []
    for _, _, mat, axis in flat:
        A, B = mat.shape
        if gather:
            lands.append((A * N_DEV, B) if axis == 0 else (A, B * N_DEV))
            ns.append(A if axis == 0 else B)
        else:
            lands.append((N_DEV, A // N_DEV, B) if axis == 0 else (N_DEV, A, B // N_DEV))
            ns.append(A // N_DEV if axis == 0 else B // N_DEV)

    def body(*refs):
        src_refs, land_refs, sems = refs[:M], refs[M:2 * M], refs[2 * M:2 * M + 3 * G]
        token = refs[-1]
        x, y, c = lax.axis_index("x"), lax.axis_index("y"), lax.axis_index("c")
        me = 4 * x + 2 * y + c
        for i, (g, j, _, axis) in enumerate(flat):
            for k, (px, py, pc) in enumerate(_peers(x, y, c)):
                s, d = _ends(src_refs[i], land_refs[i], 4 * px + 2 * py + pc, me, ns[i], axis, gather)
                pltpu.make_async_remote_copy(
                    src_ref=s, dst_ref=d, send_sem=sems[3 * g].at[7 * j + k], recv_sem=sems[3 * g + 1].at[7 * j + k],
                    device_id=(px, py, pc), device_id_type=MESH).start()
            s, d = _ends(src_refs[i], land_refs[i], me, me, ns[i], axis, gather)
            pltpu.make_async_copy(s, d, sems[3 * g + 2].at[j]).start()
        token[...] = jnp.zeros_like(token)

    sem_shapes = []
    for items in groups:
        sem_shapes += [pltpu.SemaphoreType.DMA((7 * len(items),))] * 2 + [pltpu.SemaphoreType.DMA((len(items),))]
    outs = pl.pallas_call(
        body, name=name,
        out_shape=(*sem_shapes, *[pltpu.HBM(m.shape, m.dtype) for _, _, m, _ in flat],
                   *[pltpu.HBM(shp, m.dtype) for shp, (_, _, m, _) in zip(lands, flat)],
                   jax.ShapeDtypeStruct((8, LANES), F32)),
        in_specs=[HBM_SPEC] * (2 * M),
        out_specs=(*[SEM_SPEC] * (3 * G), *[HBM_SPEC] * (2 * M), pl.BlockSpec(memory_space=pltpu.VMEM)),
        input_output_aliases={i: 3 * G + i for i in range(2 * M)},
        compiler_params=pltpu.CompilerParams(has_side_effects=FLOWING),
    )(*[pltpu.with_memory_space_constraint(m, pltpu.HBM) for _, _, m, _ in flat],
      *[pltpu.with_memory_space_constraint(lax.empty(shp, m.dtype), pltpu.HBM)
        for shp, (_, _, m, _) in zip(lands, flat)])
    handles = []
    for g in range(G):
        idx = [i for i, f in enumerate(flat) if f[0] == g]
        handles.append((outs[3 * g], outs[3 * g + 1], outs[3 * g + 2], [outs[3 * G + i] for i in idx],
                        [outs[3 * G + M + i] for i in idx], [flat[i][3] for i in idx]))
    return handles, outs[-1]


def _send_wait(handle, gather, after, *, name):
    send_sems, recv_sems, local_sems, mats, lands, axes = handle
    n_m = len(mats)
    ns = []
    for mat, land, axis in zip(mats, lands, axes):
        ns.append(mat.shape[axis] if gather else land.shape[1 + axis])

    def body(*refs):
        src_refs, land_refs = refs[:n_m], refs[n_m:2 * n_m]
        ssem, rsem, lsem = refs[2 * n_m:2 * n_m + 3]
        x, y, c = lax.axis_index("x"), lax.axis_index("y"), lax.axis_index("c")
        me = 4 * x + 2 * y + c
        for j in range(n_m):
            for k, (px, py, pc) in enumerate(_peers(x, y, c)):
                s, d = _ends(src_refs[j], land_refs[j], 4 * px + 2 * py + pc, me, ns[j], axes[j], gather)
                cp = pltpu.make_async_remote_copy(
                    src_ref=s, dst_ref=d, send_sem=ssem.at[7 * j + k], recv_sem=rsem.at[7 * j + k],
                    device_id=(px, py, pc), device_id_type=MESH)
                cp.wait_send()
                cp.wait_recv()
            s, d = _ends(src_refs[j], land_refs[j], me, me, ns[j], axes[j], gather)
            pltpu.make_async_copy(s, d, lsem.at[j]).wait()

    outs = pl.pallas_call(
        body, name=name,
        out_shape=(*[pltpu.HBM(m.shape, m.dtype) for m in mats], *[pltpu.HBM(l.shape, l.dtype) for l in lands]),
        in_specs=[HBM_SPEC] * (2 * n_m) + [SEM_SPEC] * 3 + [pl.BlockSpec(memory_space=pl.ANY)],
        out_specs=tuple([HBM_SPEC] * (2 * n_m)),
        input_output_aliases={i: i for i in range(2 * n_m)},
        compiler_params=pltpu.CompilerParams(has_side_effects=FLOWING),
    )(*mats, *lands, send_sems, recv_sems, local_sems, after)
    return list(outs[n_m:])


def _sum_parts(parts, *, name):
    n, R, C = parts.shape
    rb = _pick(R, 512) if R % LANES == 0 else R

    def body(p_ref, o_ref):
        acc = p_ref[0].astype(F32)
        for k in range(1, n):
            acc = acc + p_ref[k].astype(F32)
        o_ref[...] = acc

    return pl.pallas_call(
        body, grid=(R // rb,), in_specs=[pl.BlockSpec((n, rb, C), lambda i: (0, i, 0))],
        out_specs=pl.BlockSpec((rb, C), lambda i: (i, 0)),
        out_shape=jax.ShapeDtypeStruct((R, C), F32),
        compiler_params=_params(("parallel",)), name=name,
    )(parts)


def _adamw(w, m, v, parts, *, name):
    L, R, C = w.shape
    n = parts[0].shape[0]
    cap = max(16, VMEM_LIMIT // 3 // (2 * L * n * C * parts[0].dtype.itemsize))
    rb = max([r for r in range(16, min(R, cap) + 1, 16) if R % r == 0], default=R)

    def body(w_ref, m_ref, v_ref, *rest):
        p_refs = rest[:L]
        g_ref, d_ref, nm_ref, nv_ref = rest[L:]
        for l in range(L):
            @pl.when(pl.program_id(0) == l)
            def _(p_ref=p_refs[l]):
                g = p_ref[0].astype(F32)
                for k in range(1, n):
                    g = g + p_ref[k].astype(F32)
                mn = ADAM_B1 * m_ref[...] + (1.0 - ADAM_B1) * g
                vn = ADAM_B2 * v_ref[...] + (1.0 - ADAM_B2) * jnp.square(g)
                m_hat = mn / (1.0 - ADAM_B1 ** ADAM_STEP)
                v_hat = vn / (1.0 - ADAM_B2 ** ADAM_STEP)
                g_ref[...] = g
                d_ref[...] = -ADAM_LR * (m_hat / (jnp.sqrt(v_hat) + ADAM_EPS) + ADAM_WD * w_ref[...])
                nm_ref[...] = mn
                nv_ref[...] = vn

    row = pl.BlockSpec((None, rb, C), lambda l, i: (l, i, 0))
    part = lambda k: pl.BlockSpec((n, rb, C), lambda l, i: (0, jnp.where(l == k, i, 0), 0))
    return pl.pallas_call(
        body, grid=(L, R // rb), in_specs=[row, row, row] + [part(k) for k in range(L)],
        out_specs=[row] * 4, out_shape=[jax.ShapeDtypeStruct((L, R, C), F32)] * 4,
        compiler_params=_params(("arbitrary", "arbitrary")), name=name,
    )(w, m, v, *parts)


BIG = (("a_w_in", "a_w_in", 1, False), ("a_w_out", "a_w_out", 0, False), ("kv_w", "kv_w", 0, False),
       ("b_w_q", "b_w_q_t", 0, True), ("b_w_o", "b_w_o", 1, False), ("ffn_w_up", "ffn_w_up_t", 0, True),
       ("ffn_w_down", "ffn_w_down", 0, False))
SMALL_SHARDED = (("a_ln_g", 1), ("a_ln_b", 1), ("ffn_conv_w", 2), ("ln_g", 2), ("ln_b", 2))
REPLICATED = ("a_w_s", "a_b_s", "rel_table", "ffn_conv_b")


def _pack_rows(arrs, lead=0):
    lshape = arrs[0].shape[:lead]
    p = jnp.concatenate([a.reshape(*lshape, -1, LANES) for a in arrs], axis=lead)
    pad = -p.shape[lead] % 8
    return jnp.pad(p, [(0, 0)] * lead + [(0, pad), (0, 0)])


def _unpack_rows(packed, shapes, lead=0):
    lshape = packed.shape[:lead]
    out, off = [], 0
    for shp in shapes:
        r = int(np.prod(shp)) // LANES
        out.append(lax.slice_in_dim(packed, off, off + r, axis=lead).reshape(*lshape, *shp))
        off += r
    return out


def _as_mats(a, transposed):
    a = a[None] if a.ndim == 2 else a
    return jnp.swapaxes(a, 1, 2) if transposed else a


def _merge_shards(stacked, axis):
    a = jnp.moveaxis(stacked, 0, axis)
    shp = list(a.shape)
    return a.reshape(shp[:axis] + [shp[axis] * shp[axis + 1]] + shp[axis + 2:])


def _split_shards(full, axis):
    shp = list(full.shape)
    a = full.reshape(shp[:axis] + [N_DEV, shp[axis] // N_DEV] + shp[axis + 1:])
    return jnp.moveaxis(a, axis, 0)


def kernel(x, a_w_in, a_ln_g, a_ln_b, a_w_s, a_b_s, a_w_out, kv_w, b_w_q, b_w_o, rel_table, ffn_w_up, ffn_conv_w, ffn_conv_b, ffn_w_down, ln_g, ln_b, loss_target, m_a_w_in, m_a_ln_g, m_a_ln_b, m_a_w_s, m_a_b_s, m_a_w_out, m_kv_w, m_b_w_q, m_b_w_o, m_rel_table, m_ffn_w_up, m_ffn_conv_w, m_ffn_conv_b, m_ffn_w_down, m_ln_g, m_ln_b, v_a_w_in, v_a_ln_g, v_a_ln_b, v_a_w_s, v_a_b_s, v_a_w_out, v_kv_w, v_b_w_q, v_b_w_o, v_rel_table, v_ffn_w_up, v_ffn_conv_w, v_ffn_conv_b, v_ffn_w_down, v_ln_g, v_ln_b):
    names = ["a_w_in", "a_ln_g", "a_ln_b", "a_w_s", "a_b_s", "a_w_out", "kv_w", "b_w_q", "b_w_o", "rel_table",
             "ffn_w_up", "ffn_conv_w", "ffn_conv_b", "ffn_w_down", "ln_g", "ln_b"]
    w = dict(zip(names, (a_w_in, a_ln_g, a_ln_b, a_w_s, a_b_s, a_w_out, kv_w, b_w_q, b_w_o, rel_table,
                         ffn_w_up, ffn_conv_w, ffn_conv_b, ffn_w_down, ln_g, ln_b)))
    m = dict(zip(names, (m_a_w_in, m_a_ln_g, m_a_ln_b, m_a_w_s, m_a_b_s, m_a_w_out, m_kv_w, m_b_w_q, m_b_w_o,
                         m_rel_table, m_ffn_w_up, m_ffn_conv_w, m_ffn_conv_b, m_ffn_w_down, m_ln_g, m_ln_b)))
    v = dict(zip(names, (v_a_w_in, v_a_ln_g, v_a_ln_b, v_a_w_s, v_a_b_s, v_a_w_out, v_kv_w, v_b_w_q, v_b_w_o,
                         v_rel_table, v_ffn_w_up, v_ffn_conv_w, v_ffn_conv_b, v_ffn_w_down, v_ln_g, v_ln_b)))
    small_names = [n for n, _ in SMALL_SHARDED]
    small_shapes = [w[n].shape for n in small_names]
    rep_shapes = [w[n].shape for n in REPLICATED]

    axis_of = {key: axis for _, key, axis, _ in BIG}
    src = {}
    for n, key, axis, tr in BIG:
        loc = _as_mats(w[n], tr).astype(BF16)
        for l in range(loc.shape[0]):
            src[(key, l)] = loc[l]
    order = []
    for i in range(DEPTH):
        if i < N_A:
            order += [[("a_w_in", i)], [("a_w_out", i)]]
        else:
            order += [([("kv_w", 0)] if i == N_A else []) + [("b_w_q_t", i - N_A)], [("b_w_o", i - N_A)]]
        order += [[("ffn_w_up_t", i)], [("ffn_w_down", i)]]
    small_src = _pack_rows([w[n] for n in small_names])
    srows = small_src.shape[0]
    handles, _ = _send_start([[(small_src, 0)]] + [[(src[kl], axis_of[kl[0]]) for kl in grp] for grp in order],
                             True, name="gather_start")
    small_all = _send_wait(handles[0], True, x, name="gather_wait_small")[0]
    small_st = _unpack_rows(small_all.reshape(N_DEV, srows, LANES), small_shapes, lead=1)
    base = {n: w[n] for n in REPLICATED}
    for (n, ax), st in zip(SMALL_SHARDED, small_st):
        base[n] = _merge_shards(st, ax)
    for n, key, _, tr in BIG:
        base[key] = [None] * (1 if w[n].ndim == 2 else w[n].shape[0])

    def fetch(group, after):
        mats = _send_wait(handles[1 + group], True, after, name=f"gather_wait_{group}")
        return dict(zip(order[group], mats))

    sent = {}

    def emit(group, mats):
        keys = list(mats)
        hs, token = _send_start([[(mats[kl], axis_of[kl[0]]) for kl in keys]], False, name=f"exchange_start_{group}")
        sent[group] = (keys, hs[0])
        return token

    small_sent = []

    def emit_small(grads):
        small_pack = _pack_rows([_split_shards(grads[n], ax) for n, ax in SMALL_SHARDED], lead=1)
        rest = _pack_rows([grads[n] for n in REPLICATED[1:]])
        mine = jnp.concatenate([small_pack.reshape(N_DEV * srows, LANES), rest], axis=0)
        gating = grads[REPLICATED[0]].reshape(-1, LANES).astype(BF16)
        hs, token = _send_start([[(mine, 0), (gating, 0)]], True, name="small_grads_start")
        small_sent.append(hs[0])
        return token

    loss, grad_x, grads = _local_step(x[0], loss_target[0], _Weights(base, fetch, emit, emit_small))
    loss = lax.psum(loss, ("x", "y", "c"))
    out = {}

    landed = {}
    last = grad_x
    left = lambda e: min(g for g, (keys, _) in sent.items() if any(k[0] == e[1] for k in keys))
    for n, key, axis, tr in sorted(BIG, key=left, reverse=True):
        shp = w[n].shape
        for group in sorted(sent, reverse=True):
            keys, h = sent[group]
            if keys[0] not in landed and any(k[0] == key for k in keys):
                landed.update(zip(keys, _send_wait(h, False, last, name=f"exchange_wait_{group}")))
        parts = [landed[(key, l)] for l in range(1 if len(shp) == 2 else shp[0])]
        res = _adamw(_as_mats(w[n], tr), _as_mats(m[n], tr), _as_mats(v[n], tr), parts, name=f"adamw_{n}")
        out[n] = [(jnp.swapaxes(r, 1, 2) if tr else r).reshape(shp) for r in res]
        last = res[0]

    allp, allg = _send_wait(small_sent[0], True, last, name="small_grads_wait")
    gsum = _sum_parts(allp.reshape(N_DEV, -1, LANES), name="sum_small_grads")
    gating = _sum_parts(allg.reshape(N_DEV, -1, LANES), name="sum_gating_grads")
    g_small = lax.dynamic_slice_in_dim(gsum, _my_index() * srows, srows, axis=0)
    pack_sr = lambda d: jnp.concatenate([_pack_rows([d[n] for n in small_names]),
                                         _pack_rows([d[n] for n in REPLICATED])], axis=0)
    n_rest = sum(int(np.prod(s)) for s in rep_shapes[1:]) // LANES
    gs_in = jnp.concatenate([g_small, gating, gsum[N_DEV * srows:N_DEV * srows + n_rest]], axis=0)
    gs_in = jnp.pad(gs_in, ((0, pack_sr(w).shape[0] - gs_in.shape[0]), (0, 0)))[None]
    res = _adamw(pack_sr(w)[None], pack_sr(m)[None], pack_sr(v)[None], [gs_in], name="adamw_small")
    for n, vals in zip(small_names, zip(*[_unpack_rows(r[0, :srows], small_shapes) for r in res])):
        out[n] = list(vals)
    for n, vals in zip(REPLICATED, zip(*[_unpack_rows(r[0, srows:], rep_shapes) for r in res])):
        out[n] = list(vals)

    return (loss, grad_x[None], *[out[n][0] for n in names], *[out[n][1] for n in names],
            *[out[n][2] for n in names], *[out[n][3] for n in names])
```

```python
import math

import numpy as np
import jax
import jax.numpy as jnp
from jax import lax
from jax.experimental import pallas as pl
from jax.experimental.pallas import tpu as pltpu

F32 = jnp.float32
BF16 = jnp.bfloat16
ACT = jnp.bfloat16
MESH = pl.DeviceIdType.MESH

N_DEV = 8
DEPTH = 4
N_A = 2
CHUNK = 128
BLK = 128
HEAD_DIM = 64
DILATED_GROUPS = ((128, 1), (512, 4), (2048, 16))
N_GROUPS = 3
REL_BUCKETS = 32
REL_MAX_DIST = 2048
ALPHA = (2 * DEPTH) ** 0.25
LN_EPS = 1e-5
NEG = -1e30
ADAM_LR = 0.001
ADAM_B1 = 0.9
ADAM_B2 = 0.999
ADAM_EPS = 1e-08
ADAM_WD = 0.01
ADAM_STEP = 10

LANES = 128
VMEM_LIMIT = 56 * 1024 * 1024
MM_TILE_CAP = 1408
MM_VMEM_BUDGET = 46 * 1024 * 1024
INV_SQRT2 = 1.0 / math.sqrt(2.0)
INV_SQRT_2PI = 1.0 / math.sqrt(2.0 * math.pi)


def _pick(n, cap):
    best = None
    for t in range(LANES, min(n, cap) + 1, LANES):
        if n % t == 0:
            best = t
    return best if best is not None else n


def _params(sem):
    return pltpu.CompilerParams(dimension_semantics=sem, vmem_limit_bytes=VMEM_LIMIT)


def _ordered(body, in_specs, args, after):
    if after is None:
        return body, list(in_specs), tuple(args)
    return (lambda _, *refs: body(*refs)), [pl.BlockSpec(memory_space=pl.ANY), *in_specs], (after, *args)


def _gelu(x):
    return 0.5 * x * (1.0 + lax.erf(x * INV_SQRT2))


def _gelu_grad(x):
    return 0.5 * (1.0 + lax.erf(x * INV_SQRT2)) + x * jnp.exp(-0.5 * x * x) * INV_SQRT_2PI


def _mm(a, b, *, ta=False, tb=False, out_dtype=F32, scale=None, after=None, name):
    halves = isinstance(a, tuple)
    parts = 1 if halves or a.ndim == 2 else a.shape[0]
    ash = (a[0].shape[0], 2 * a[0].shape[1]) if halves else (a.shape if parts == 1
                                                               else (a.shape[1], parts * a.shape[2]))
    if ta:
        K, M = ash
    else:
        M, K = ash
    if tb:
        N, Kb = b.shape
    else:
        Kb, N = b.shape
    assert K == Kb, (ash, b.shape, ta, tb)
    split = 2 if halves else parts
    tm = _pick(M // split if ta else M, MM_TILE_CAP)
    tn = _pick(N, MM_TILE_CAP)
    kspan = K if ta or split == 1 else K // split
    abytes = (a[0] if halves else a).dtype.itemsize * (2 if halves else 1)
    fixed = 2 * tm * tn * jnp.dtype(out_dtype).itemsize + tm * tn * 4
    fits = [t for t in range(LANES, kspan + 1, LANES)
            if kspan % t == 0 and 2 * t * (tm * abytes + tn * b.dtype.itemsize) + fixed <= MM_VMEM_BUDGET]
    tk = max(fits) if fits else _pick(kspan, MM_TILE_CAP)
    nk = K // tk
    nh = (M // split // tm if ta else K // split // tk) if split > 1 else 0
    dn = (((0 if ta else 1,), (1 if tb else 0,)), ((), ()))

    def body(*refs):
        n_tail = 3 if nk > 1 else 2
        a_refs, b_ref, o_ref = refs[:-n_tail], refs[-n_tail], refs[-n_tail + 1]
        k = pl.program_id(2)

        def finish(r):
            if scale is not None:
                r = r * scale
            o_ref[...] = r.astype(out_dtype)

        def accumulate(a_ref):
            part = lax.dot_general(a_ref[...].astype(BF16), b_ref[...].astype(BF16), dn,
                                   preferred_element_type=F32)
            if nk == 1:
                finish(part)
                return
            acc_ref = refs[-1]

            @pl.when(k == 0)
            def _():
                acc_ref[...] = part

            @pl.when((k > 0) & (k < nk - 1))
            def _():
                acc_ref[...] += part

            @pl.when(k == nk - 1)
            def _():
                finish(acc_ref[...] + part)

        if halves:
            first = (pl.program_id(0) if ta else k) < nh
            pl.when(first)(lambda: accumulate(a_refs[0]))
            pl.when(jnp.logical_not(first))(lambda: accumulate(a_refs[1]))
        else:
            accumulate(a_refs[0])

    if halves and ta:
        a_specs = [pl.BlockSpec((tk, tm), lambda i, j, k: (jnp.where(i < nh, k, 0), jnp.minimum(i, nh - 1))),
                   pl.BlockSpec((tk, tm), lambda i, j, k: (jnp.where(i >= nh, k, 0), jnp.maximum(i - nh, 0)))]
    elif halves:
        a_specs = [pl.BlockSpec((tm, tk), lambda i, j, k: (i, jnp.minimum(k, nh - 1))),
                   pl.BlockSpec((tm, tk), lambda i, j, k: (i, jnp.maximum(k - nh, 0)))]
    elif parts > 1:
        a_specs = [pl.BlockSpec((None, tk, tm), lambda i, j, k: (i // nh, k, i % nh)) if ta
                   else pl.BlockSpec((None, tm, tk), lambda i, j, k: (k // nh, i, k % nh))]
    else:
        a_specs = [pl.BlockSpec((tk, tm), lambda i, j, k: (k, i)) if ta
                   else pl.BlockSpec((tm, tk), lambda i, j, k: (i, k))]
    b_spec = (pl.BlockSpec((tn, tk), lambda i, j, k: (j, k)) if tb
              else pl.BlockSpec((tk, tn), lambda i, j, k: (k, j)))
    body, in_specs, args = _ordered(body, [*a_specs, b_spec], (*(a if halves else (a,)), b), after)
    return pl.pallas_call(
        body, grid=(M // tm, N // tn, nk), in_specs=in_specs,
        out_specs=pl.BlockSpec((tm, tn), lambda i, j, k: (i, j)),
        out_shape=jax.ShapeDtypeStruct((M, N), out_dtype),
        scratch_shapes=[pltpu.VMEM((tm, tn), F32)] if nk > 1 else [],
        compiler_params=_params(("parallel", "parallel", "arbitrary")), name=name,
    )(*args)


def _add_ln_fwd(x, h, g, b, *, wide, name):
    T, D = x.shape
    rb = _pick(T, 512)

    def body(x_ref, h_ref, g_ref, b_ref, *o_refs):
        pre = ALPHA * x_ref[...].astype(F32) + h_ref[...].astype(F32)
        mu = jnp.mean(pre, axis=1, keepdims=True)
        cen = pre - mu
        var = jnp.mean(cen * cen, axis=1, keepdims=True)
        y = cen * lax.rsqrt(var + LN_EPS) * g_ref[...] + b_ref[...]
        for o_ref in o_refs:
            o_ref[...] = y.astype(o_ref.dtype)

    row = pl.BlockSpec((rb, D), lambda i: (i, 0))
    vec = pl.BlockSpec((1, D), lambda i: (0, 0))
    dtypes = [F32, BF16] if wide else [BF16]
    return pl.pallas_call(
        body, grid=(T // rb,), in_specs=[row, row, vec, vec], out_specs=[row] * len(dtypes),
        out_shape=[jax.ShapeDtypeStruct((T, D), dt) for dt in dtypes],
        compiler_params=_params(("parallel",)), name=name,
    )(x, h, g.reshape(1, D), b.reshape(1, D))


def _add_ln_bwd(x, h, g, terms, *, after=None, name):
    T, D = x.shape
    rb = _pick(T, 512)
    coefs = [c for c, _ in terms]
    nt = len(terms)

    def body(*refs):
        x_ref, h_ref, g_ref = refs[:3]
        t_refs = refs[3:3 + nt]
        dpb_ref, dg_ref, db_ref = refs[3 + nt:]
        dy = None
        for c, r in zip(coefs, t_refs):
            v = r[...].astype(F32) if c == 1.0 else c * r[...].astype(F32)
            dy = v if dy is None else dy + v
        pre = ALPHA * x_ref[...].astype(F32) + h_ref[...].astype(F32)
        mu = jnp.mean(pre, axis=1, keepdims=True)
        cen = pre - mu
        var = jnp.mean(cen * cen, axis=1, keepdims=True)
        rstd = lax.rsqrt(var + LN_EPS)
        xhat = cen * rstd
        dxh = dy * g_ref[...]
        m1 = jnp.mean(dxh, axis=1, keepdims=True)
        m2 = jnp.mean(dxh * xhat, axis=1, keepdims=True)
        dpre = rstd * (dxh - m1 - xhat * m2)
        dpb_ref[...] = dpre.astype(BF16)
        dg = jnp.sum(dy * xhat, axis=0, keepdims=True)
        db = jnp.sum(dy, axis=0, keepdims=True)

        @pl.when(pl.program_id(0) == 0)
        def _():
            dg_ref[...] = dg
            db_ref[...] = db

        @pl.when(pl.program_id(0) > 0)
        def _():
            dg_ref[...] += dg
            db_ref[...] += db

    row = pl.BlockSpec((rb, D), lambda i: (i, 0))
    vec = pl.BlockSpec((1, D), lambda i: (0, 0))
    body, in_specs, args = _ordered(body, [row, row, vec] + [row] * nt,
                                    (x, h, g.reshape(1, D), *[a for _, a in terms]), after)
    return pl.pallas_call(
        body, grid=(T // rb,), in_specs=in_specs,
        out_specs=[row, vec, vec],
        out_shape=[jax.ShapeDtypeStruct((T, D), BF16),
                   jax.ShapeDtypeStruct((1, D), F32), jax.ShapeDtypeStruct((1, D), F32)],
        compiler_params=_params(("arbitrary",)), name=name,
    )(*args)


def _lincomb(terms, out_dtype, *, name):
    R, C = terms[0][1].shape
    rb = _pick(R, 512)
    coefs = [c for c, _ in terms]
    nt = len(terms)

    def body(*refs):
        acc = None
        for c, r in zip(coefs, refs[:nt]):
            v = r[...].astype(F32)
            v = v if c == 1.0 else c * v
            acc = v if acc is None else acc + v
        refs[nt][...] = acc.astype(out_dtype)

    row = pl.BlockSpec((rb, C), lambda i: (i, 0))
    return pl.pallas_call(
        body, grid=(R // rb,), in_specs=[row] * nt, out_specs=row,
        out_shape=jax.ShapeDtypeStruct((R, C), out_dtype),
        compiler_params=_params(("parallel",)), name=name,
    )(*[a for _, a in terms])


def _loss_grad(y, tgt, *, name):
    T, D = y.shape
    rb = _pick(T, 512)

    def body(y_ref, t_ref, dy_ref, l_ref):
        err = y_ref[...] - t_ref[...]
        dy_ref[...] = err * (1.0 / D)
        part = jnp.sum(jnp.sum(err * err, axis=1, keepdims=True), axis=0, keepdims=True) * (0.5 / D)
        part = jnp.broadcast_to(part, (1, LANES))

        @pl.when(pl.program_id(0) == 0)
        def _():
            l_ref[...] = part

        @pl.when(pl.program_id(0) > 0)
        def _():
            l_ref[...] += part

    row = pl.BlockSpec((rb, D), lambda i: (i, 0))
    return pl.pallas_call(
        body, grid=(T // rb,), in_specs=[row, row],
        out_specs=[row, pl.BlockSpec((1, LANES), lambda i: (0, 0))],
        out_shape=[jax.ShapeDtypeStruct((T, D), F32), jax.ShapeDtypeStruct((1, LANES), F32)],
        compiler_params=_params(("arbitrary",)), name=name,
    )(y, tgt)


def _sgu_fwd(zp, ln_g, ln_b, ws, bst, *, name):
    T, E2 = zp.shape
    E = E2 // 2
    G = ws.shape[0]
    cg = E // G
    rb = 2 * CHUNK

    def body(z_ref, g_ref, b_ref, ws_ref, bs_ref, y_ref):
        u = _gelu(z_ref[:, :E].astype(F32))
        v = _gelu(z_ref[:, E:].astype(F32))
        mu = jnp.mean(v, axis=1, keepdims=True)
        cen = v - mu
        var = jnp.mean(cen * cen, axis=1, keepdims=True)
        vn = (cen * lax.rsqrt(var + LN_EPS) * g_ref[...] + b_ref[...]).astype(BF16)
        for ci in range(rb // CHUNK):
            rows = slice(ci * CHUNK, (ci + 1) * CHUNK)
            for gi in range(G):
                cols = slice(gi * cg, (gi + 1) * cg)
                sv = jnp.dot(ws_ref[gi], vn[rows, cols], preferred_element_type=F32)
                sv = sv + bs_ref[:, gi:gi + 1]
                y_ref[rows, cols] = (u[rows, cols] * sv).astype(BF16)

    return pl.pallas_call(
        body, grid=(T // rb,),
        in_specs=[pl.BlockSpec((rb, E2), lambda i: (i, 0)),
                  pl.BlockSpec((1, E), lambda i: (0, 0)), pl.BlockSpec((1, E), lambda i: (0, 0)),
                  pl.BlockSpec((G, CHUNK, CHUNK), lambda i: (0, 0, 0)),
                  pl.BlockSpec((CHUNK, G), lambda i: (0, 0))],
        out_specs=pl.BlockSpec((rb, E), lambda i: (i, 0)),
        out_shape=jax.ShapeDtypeStruct((T, E), BF16),
        compiler_params=_params(("parallel",)), name=name,
    )(zp, ln_g.reshape(1, E), ln_b.reshape(1, E), ws, bst)


def _sgu_bwd(zp, dy, ln_g, ln_b, ws, wst, bst, *, after=None, name):
    T, E2 = zp.shape
    E = E2 // 2
    G = ws.shape[0]
    cg = E // G
    rb = CHUNK
    nsteps = T // rb

    def body(z_ref, dy_ref, g_ref, b_ref, ws_ref, wst_ref, bs_ref,
             dz_ref, dg_ref, db_ref, dws_ref, dbs_ref, dsv_acc):
        step = pl.program_id(0)

        @pl.when(step == 0)
        def _():
            dg_ref[...] = jnp.zeros_like(dg_ref)
            db_ref[...] = jnp.zeros_like(db_ref)
            dws_ref[...] = jnp.zeros_like(dws_ref)
            dsv_acc[...] = jnp.zeros_like(dsv_acc)

        zu = z_ref[:, :E].astype(F32)
        zv = z_ref[:, E:].astype(F32)
        u = _gelu(zu)
        v = _gelu(zv)
        mu = jnp.mean(v, axis=1, keepdims=True)
        cen = v - mu
        var = jnp.mean(cen * cen, axis=1, keepdims=True)
        rstd = lax.rsqrt(var + LN_EPS)
        xhat = cen * rstd
        vn = (xhat * g_ref[...] + b_ref[...]).astype(BF16)
        dyv = dy_ref[...].astype(F32)
        dsv = dyv * u
        dsv_acc[...] += dsv
        dsvb = dsv.astype(BF16)
        tril = (lax.broadcasted_iota(jnp.int32, (CHUNK, CHUNK), 0)
                >= lax.broadcasted_iota(jnp.int32, (CHUNK, CHUNK), 1))
        du_parts = []
        dvn_parts = []
        for gi in range(G):
            cols = slice(gi * cg, (gi + 1) * cg)
            sv = jnp.dot(ws_ref[gi], vn[:, cols], preferred_element_type=F32) + bs_ref[:, gi:gi + 1]
            du_parts.append(dyv[:, cols] * sv)
            dvn_parts.append(jnp.dot(wst_ref[gi], dsvb[:, cols], preferred_element_type=F32))
            dw = lax.dot_general(dsvb[:, cols], vn[:, cols], (((1,), (1,)), ((), ())),
                                 preferred_element_type=F32)
            dws_ref[gi] += jnp.where(tril, dw, 0.0)
        du = jnp.concatenate(du_parts, axis=1)
        dvn = jnp.concatenate(dvn_parts, axis=1)
        dg_ref[...] += jnp.sum(dvn * xhat, axis=0, keepdims=True)
        db_ref[...] += jnp.sum(dvn, axis=0, keepdims=True)
        dxh = dvn * g_ref[...]
        m1 = jnp.mean(dxh, axis=1, keepdims=True)
        m2 = jnp.mean(dxh * xhat, axis=1, keepdims=True)
        dv = rstd * (dxh - m1 - xhat * m2)
        dz_ref[:, :E] = (du * _gelu_grad(zu)).astype(BF16)
        dz_ref[:, E:] = (dv * _gelu_grad(zv)).astype(BF16)

        @pl.when(step == nsteps - 1)
        def _():
            lane = lax.broadcasted_iota(jnp.int32, (CHUNK, LANES), 1)
            out = jnp.zeros((CHUNK, LANES), F32)
            for gi in range(G):
                s = jnp.sum(dsv_acc[:, gi * cg:(gi + 1) * cg], axis=1, keepdims=True)
                out = jnp.where(lane == gi, s, out)
            dbs_ref[...] = out

    vecE = pl.BlockSpec((1, E), lambda i: (0, 0))
    wspec = pl.BlockSpec((G, CHUNK, CHUNK), lambda i: (0, 0, 0))
    body, in_specs, args = _ordered(
        body, [pl.BlockSpec((rb, E2), lambda i: (i, 0)), pl.BlockSpec((rb, E), lambda i: (i, 0)),
               vecE, vecE, wspec, wspec, pl.BlockSpec((CHUNK, G), lambda i: (0, 0))],
        (zp, dy, ln_g.reshape(1, E), ln_b.reshape(1, E), ws, wst, bst), after)
    return pl.pallas_call(
        body, grid=(nsteps,), in_specs=in_specs,
        out_specs=[pl.BlockSpec((rb, E2), lambda i: (i, 0)), vecE, vecE, wspec,
                   pl.BlockSpec((CHUNK, LANES), lambda i: (0, 0))],
        out_shape=[jax.ShapeDtypeStruct((T, E2), BF16), jax.ShapeDtypeStruct((1, E), F32),
                   jax.ShapeDtypeStruct((1, E), F32), jax.ShapeDtypeStruct((G, CHUNK, CHUNK), F32),
                   jax.ShapeDtypeStruct((CHUNK, LANES), F32)],
        scratch_shapes=[pltpu.VMEM((CHUNK, E), F32)],
        compiler_params=_params(("arbitrary",)), name=name,
    )(*args)


def _shift_down(x, k, row):
    return jnp.where(row >= k, pltpu.roll(x, k, 0), 0.0)


def _shift_up(x, k, row, T):
    return jnp.where(row < T - k, pltpu.roll(x, T - k, 0), 0.0)


def _conv3(x, w_ref, b_ref, row):
    return (w_ref[0:1, :] * _shift_down(x, 2, row) + w_ref[1:2, :] * _shift_down(x, 1, row)
            + w_ref[2:3, :] * x + b_ref[...])


def _convgate_fwd(hh, cw, cb, *, name):
    T, F2 = hh.shape
    F = F2 // 2
    ns = F // LANES

    def body(a_ref, g_ref, wa_ref, wg_ref, ba_ref, bg_ref, o_ref, ca_ref, cg_ref):
        row = lax.broadcasted_iota(jnp.int32, (T, LANES), 0)
        ca = _conv3(a_ref[...].astype(F32), wa_ref, ba_ref, row)
        cgv = _conv3(g_ref[...].astype(F32), wg_ref, bg_ref, row)
        o_ref[...] = (_gelu(ca) * cgv).astype(BF16)
        ca_ref[...] = ca.astype(ACT)
        cg_ref[...] = cgv.astype(ACT)

    sa = lambda r: pl.BlockSpec((r, LANES), lambda j: (0, j))
    sg = lambda r: pl.BlockSpec((r, LANES), lambda j: (0, j + ns))
    return pl.pallas_call(
        body, grid=(ns,), in_specs=[sa(T), sg(T), sa(3), sg(3), sa(1), sg(1)],
        out_specs=[sa(T)] * 3,
        out_shape=[jax.ShapeDtypeStruct((T, F), BF16), jax.ShapeDtypeStruct((T, F), ACT),
                   jax.ShapeDtypeStruct((T, F), ACT)],
        compiler_params=_params(("parallel",)), name=name,
    )(hh, hh, cw, cw, cb, cb)


def _convgate_bwd(hh, hca, hcg, dact, cw, *, name):
    T, F2 = hh.shape
    F = F2 // 2
    ns = F // LANES

    def body(a_ref, g_ref, ca_ref, cg_ref, d_ref, wa_ref, wg_ref,
             da_ref, dg_ref, dwa_ref, dwg_ref, dba_ref, dbg_ref):
        row = lax.broadcasted_iota(jnp.int32, (T, LANES), 0)
        d = d_ref[...].astype(F32)
        ca = ca_ref[...].astype(F32)
        cgv = cg_ref[...].astype(F32)
        cdf = 0.5 * (1.0 + lax.erf(ca * INV_SQRT2))
        dca = d * cgv * (cdf + ca * jnp.exp(-0.5 * ca * ca) * INV_SQRT_2PI)
        dcg = d * (ca * cdf)
        for x_ref, w_ref, dc, dx_ref, dw_ref, db_ref in (
                (a_ref, wa_ref, dca, da_ref, dwa_ref, dba_ref),
                (g_ref, wg_ref, dcg, dg_ref, dwg_ref, dbg_ref)):
            x = x_ref[...].astype(F32)
            up1, up2 = _shift_up(dc, 1, row, T), _shift_up(dc, 2, row, T)
            dx_ref[...] = (w_ref[2:3, :] * dc + w_ref[1:2, :] * up1 + w_ref[0:1, :] * up2).astype(BF16)
            dw_ref[0:1, :] = jnp.sum(up2 * x, axis=0, keepdims=True)
            dw_ref[1:2, :] = jnp.sum(up1 * x, axis=0, keepdims=True)
            dw_ref[2:3, :] = jnp.sum(dc * x, axis=0, keepdims=True)
            db_ref[...] = jnp.sum(dc, axis=0, keepdims=True)

    sa = lambda r: pl.BlockSpec((r, LANES), lambda j: (0, j))
    sg = lambda r: pl.BlockSpec((r, LANES), lambda j: (0, j + ns))
    return pl.pallas_call(
        body, grid=(ns,), in_specs=[sa(T), sg(T), sa(T), sa(T), sa(T), sa(3), sg(3)],
        out_specs=[sa(T), sa(T), sa(3), sa(3), sa(1), sa(1)],
        out_shape=[jax.ShapeDtypeStruct((T, F), BF16), jax.ShapeDtypeStruct((T, F), BF16),
                   jax.ShapeDtypeStruct((3, F), F32), jax.ShapeDtypeStruct((3, F), F32),
                   jax.ShapeDtypeStruct((1, F), F32), jax.ShapeDtypeStruct((1, F), F32)],
        compiler_params=_params(("parallel",)), name=name,
    )(hh, hh, hca, hcg, dact, cw, cw)


def _bucket_maps():
    iq = np.arange(BLK)[:, None]
    ik = np.arange(2 * BLK)[None, :]
    delta = iq + BLK - ik
    maps = []
    for win, dil in DILATED_GROUPS:
        n = np.clip(delta, 0, None) * dil
        max_exact = REL_BUCKETS // 2
        nf = np.maximum(n, 1).astype(np.float32)
        large = max_exact + (np.log(nf / np.float32(max_exact)) / np.float32(math.log(REL_MAX_DIST / max_exact))
                             * np.float32(REL_BUCKETS - max_exact)).astype(np.int32)
        large = np.minimum(large, REL_BUCKETS - 1)
        bucket = np.where(n < max_exact, n, large)
        valid = (delta >= 0) & (delta <= win // dil)
        maps.append(np.where(valid, bucket, -1).astype(np.int32))
    return np.stack(maps)


def _band_bias(rel_table, bmap, H, *, name):
    def body(t_ref, m_ref, o_ref):
        g = pl.program_id(0)
        bm = m_ref[0]
        for h in range(H):
            acc = jnp.full((BLK, 2 * BLK), NEG, F32)
            for b in range(REL_BUCKETS):
                acc = jnp.where(bm == b, t_ref[b, g * H + h], acc)
            o_ref[0, h] = acc

    return pl.pallas_call(
        body, grid=(N_GROUPS,),
        in_specs=[pl.BlockSpec(memory_space=pltpu.SMEM),
                  pl.BlockSpec((1, BLK, 2 * BLK), lambda g: (g, 0, 0))],
        out_specs=pl.BlockSpec((1, H, BLK, 2 * BLK), lambda g: (g, 0, 0, 0)),
        out_shape=jax.ShapeDtypeStruct((N_GROUPS, H, BLK, 2 * BLK), F32),
        compiler_params=_params(("parallel",)), name=name,
    )(rel_table, bmap)


def _band_bias_bwd(dbias, bmap, H, *, name):
    def body(d_ref, m_ref, o_ref):
        bm = m_ref[0]
        rowi = lax.broadcasted_iota(jnp.int32, (REL_BUCKETS, LANES), 0)
        lane = lax.broadcasted_iota(jnp.int32, (REL_BUCKETS, LANES), 1)
        out = jnp.zeros((REL_BUCKETS, LANES), F32)
        for h in range(H):
            dv = d_ref[0, h]
            for b in range(REL_BUCKETS):
                s = jnp.sum(jnp.sum(jnp.where(bm == b, dv, 0.0), axis=1, keepdims=True),
                            axis=0, keepdims=True)
                out = jnp.where((rowi == b) & (lane == h), s, out)
        o_ref[0] = out

    return pl.pallas_call(
        body, grid=(N_GROUPS,),
        in_specs=[pl.BlockSpec((1, H, BLK, 2 * BLK), lambda g: (g, 0, 0, 0)),
                  pl.BlockSpec((1, BLK, 2 * BLK), lambda g: (g, 0, 0))],
        out_specs=pl.BlockSpec((1, REL_BUCKETS, LANES), lambda g: (g, 0, 0)),
        out_shape=jax.ShapeDtypeStruct((N_GROUPS, REL_BUCKETS, LANES), F32),
        compiler_params=_params(("parallel",)), name=name,
    )(dbias, bmap)


def _head_masks():
    lane = lax.broadcasted_iota(jnp.int32, (BLK, LANES), 1)
    return (lane < HEAD_DIM, lane >= HEAD_DIM)


SUPER =DILATED_GROUPS[-1][1] * BLK


def _band_rows(it, d):
    r, j = it % d, it // d
    if d == 1:
        at = lambda blk: pl.ds(pl.multiple_of(blk * BLK, BLK), BLK)
    else:
        at = lambda blk: pl.ds(r + d * BLK * blk, BLK, stride=d)
    return at(j), at(jnp.maximum(j - 1, 0))


def _stack_heads(x, hm):
    zero = jnp.zeros_like(x)
    return jnp.concatenate([jnp.where(hm[0], x, zero), jnp.where(hm[1], x, zero)], axis=0)


def _band_loops(step, d, unroll):
    n_it = SUPER // BLK

    def run(lo, hi, inside):
        if hi > lo:
            def body(it, carry):
                step(it, inside)
                return carry
            lax.fori_loop(lo, hi, body, 0, unroll=max(u for u in range(1, unroll + 1) if (hi - lo) % u == 0))

    run(0, d, False)
    run(d, n_it, True)


def _last_rows(it, d):
    m = SUPER // (d * BLK)
    if d == 1:
        return pl.ds((m - 1) * BLK, BLK)
    return pl.ds(it % d + d * BLK * (m - 1), BLK, stride=d)


def _attn_fwd_all(q, kv, bias, *, name):
    T = q.shape[0]
    HD = kv.shape[1] // 2
    PP = HD // LANES
    NS = T // SUPER

    def body(q0, q1, q2, kp_ref, kc_ref, vp_ref, vc_ref, b_ref, o_ref, ob_ref, l_ref, og, lg):
        n = pl.program_id(1)
        col = lax.broadcasted_iota(jnp.int32, (2 * BLK, 2 * BLK), 1)
        hm = _head_masks()
        for g, (q_ref, (_, d)) in enumerate(zip((q0, q1, q2), DILATED_GROUPS)):
            def step(it, inside, g=g, q_ref=q_ref, d=d):
                cur, prv = _band_rows(it, d)
                qp = q_ref[cur, :].astype(BF16)
                if inside:
                    kprev, vprev = kc_ref[prv, :], vc_ref[prv, :]
                else:
                    last = _last_rows(it, d)
                    kprev, vprev = kp_ref[last, :], vp_ref[last, :]
                kc = jnp.concatenate([kprev.astype(BF16), kc_ref[cur, :].astype(BF16)], axis=0)
                vc = jnp.concatenate([vprev.astype(BF16), vc_ref[cur, :].astype(BF16)], axis=0)
                s = lax.dot_general(_stack_heads(qp, hm), kc, (((1,), (1,)), ((), ())),
                                    preferred_element_type=F32)
                s = s + b_ref[g].reshape(2 * BLK, 2 * BLK)
                if not inside:
                    s = jnp.where((n == 0) & (col < BLK), NEG, s)
                mx = jnp.max(s, axis=1, keepdims=True)
                e = jnp.exp(s - mx)
                den = jnp.sum(e, axis=1, keepdims=True)
                out = jnp.dot((e / den).astype(BF16), vc, preferred_element_type=F32)
                lse = mx + jnp.log(den)
                og.at[g][cur, :] = jnp.where(hm[0], out[:BLK], out[BLK:])
                lg.at[g][cur, :] = jnp.where(hm[0], lse[:BLK], lse[BLK:])

            _band_loops(step, d, 8)
        la, lb, lc = lg[0], lg[1], lg[2]
        mx = jnp.maximum(jnp.maximum(la, lb), lc)
        L = mx + jnp.log(jnp.exp(la - mx) + jnp.exp(lb - mx) + jnp.exp(lc - mx))
        o = jnp.exp(la - L) * og[0] + jnp.exp(lb - L) * og[1] + jnp.exp(lc - L) * og[2]
        o_ref[...] = o
        ob_ref[...] = o.astype(BF16)
        l_ref[...] = L

    blk = lambda f: pl.BlockSpec((SUPER, LANES), f)
    prev = lambda n: jnp.maximum(n - 1, 0)
    qspec = lambda g: blk(lambda p, n: (n, g * PP + p))
    return pl.pallas_call(
        body, grid=(PP, NS),
        in_specs=[qspec(0), qspec(1), qspec(2),
                  blk(lambda p, n: (prev(n), p)), blk(lambda p, n: (n, p)),
                  blk(lambda p, n: (prev(n), PP + p)), blk(lambda p, n: (n, PP + p)),
                  pl.BlockSpec((N_GROUPS, 2, BLK, 2 * BLK), lambda p, n: (0, p, 0, 0))],
        out_specs=[blk(lambda p, n: (n, p))] * 3,
        out_shape=[jax.ShapeDtypeStruct((T, HD), F32), jax.ShapeDtypeStruct((T, HD), BF16),
                   jax.ShapeDtypeStruct((T, HD), F32)],
        scratch_shapes=[pltpu.VMEM((N_GROUPS, SUPER, LANES), F32), pltpu.VMEM((N_GROUPS, SUPER, LANES), F32)],
        compiler_params=_params(("parallel", "parallel")), name=name,
    )(q, q, q, kv, kv, kv, kv, bias)


def _attn_bwd_all(q, kv, bias, do, o, L, *, after=None, name):
    T = q.shape[0]
    HD = kv.shape[1] // 2
    PP = HD // LANES
    H = HD // HEAD_DIM
    NS = T // SUPER

    def body(q0, q1, q2, kp_ref, kc_ref, vp_ref, vc_ref, b_ref, do_ref, o_ref, L_ref,
             dq_ref, dk_ref, dv_ref, db_ref, ck_ref, cv_ref):
        n = pl.program_id(1)

        @pl.when(n == 0)
        def _():
            db_ref[...] = jnp.zeros_like(db_ref)
            ck_ref[...] = jnp.zeros_like(ck_ref)
            cv_ref[...] = jnp.zeros_like(cv_ref)

        dk_ref[...] = ck_ref[...]
        dv_ref[...] = cv_ref[...]
        ck_ref[...] = jnp.zeros_like(ck_ref)
        cv_ref[...] = jnp.zeros_like(cv_ref)

        @pl.when(n < NS)
        def _():
            col = lax.broadcasted_iota(jnp.int32, (2 * BLK, 2 * BLK), 1)
            hm = _head_masks()
            for g, (q_ref, (_, d)) in enumerate(zip((q0, q1, q2), DILATED_GROUPS)):
                def step(it, inside, g=g, q_ref=q_ref, d=d):
                    cur, prv = _band_rows(it, d)
                    last = _last_rows(it, d)
                    qp = q_ref[cur, :].astype(BF16)
                    if inside:
                        kprev, vprev = kc_ref[prv, :], vc_ref[prv, :]
                    else:
                        kprev, vprev = kp_ref[last, :], vp_ref[last, :]
                    kc = jnp.concatenate([kprev.astype(BF16), kc_ref[cur, :].astype(BF16)], axis=0)
                    vc = jnp.concatenate([vprev.astype(BF16), vc_ref[cur, :].astype(BF16)], axis=0)
                    dop = do_ref[cur, :]
                    prod = dop * o_ref[cur, :]
                    Lp = L_ref[cur, :]
                    qs = _stack_heads(qp, hm)
                    dos = _stack_heads(dop.astype(BF16), hm)
                    lse = jnp.concatenate([Lp[:, 0:1], Lp[:, HEAD_DIM:HEAD_DIM + 1]], axis=0)
                    delta = jnp.concatenate([jnp.sum(jnp.where(hm[0], prod, 0.0), axis=1, keepdims=True),
                                             jnp.sum(jnp.where(hm[1], prod, 0.0), axis=1, keepdims=True)], axis=0)
                    s = lax.dot_general(qs, kc, (((1,), (1,)), ((), ())), preferred_element_type=F32)
                    s = s + b_ref[g].reshape(2 * BLK, 2 * BLK)
                    if not inside:
                        s = jnp.where((n == 0) & (col < BLK), NEG, s)
                    pr = jnp.exp(s - lse)
                    dp = lax.dot_general(dos, vc, (((1,), (1,)), ((), ())), preferred_element_type=F32)
                    ds = pr * (dp - delta)
                    db_ref[g] += ds.reshape(2, BLK, 2 * BLK)
                    dsb = ds.astype(BF16)
                    dqs = jnp.dot(dsb, kc, preferred_element_type=F32)
                    dkc = lax.dot_general(dsb, qs, (((0,), (0,)), ((), ())), preferred_element_type=F32)
                    dvc = lax.dot_general(pr.astype(BF16), dos, (((0,), (0,)), ((), ())),
                                          preferred_element_type=F32)
                    dq_ref.at[g][cur, :] = jnp.where(hm[0], dqs[:BLK], dqs[BLK:]) * (HEAD_DIM ** -0.5)
                    ck_ref[cur, :] += dkc[BLK:]
                    cv_ref[cur, :] += dvc[BLK:]
                    if inside:
                        ck_ref[prv, :] += dkc[:BLK]
                        cv_ref[prv, :] += dvc[:BLK]
                    else:
                        dk_ref[last, :] += dkc[:BLK]
                        dv_ref[last, :] += dvc[:BLK]

                _band_loops(step, d, 4)

    blk = lambda f: pl.BlockSpec((SUPER, LANES), f)
    cur = lambda n: jnp.minimum(n, NS - 1)
    prev = lambda n: jnp.maximum(jnp.minimum(n, NS - 1) - 1, 0)
    lag = lambda n: jnp.maximum(n - 1, 0)
    qspec = lambda g: blk(lambda p, n: (cur(n), g * PP + p))
    bspec = pl.BlockSpec((N_GROUPS, 2, BLK, 2 * BLK), lambda p, n: (0, p, 0, 0))
    body, in_specs, args = _ordered(
        body, [qspec(0), qspec(1), qspec(2),
               blk(lambda p, n: (prev(n), p)), blk(lambda p, n: (cur(n), p)),
               blk(lambda p, n: (prev(n), PP + p)), blk(lambda p, n: (cur(n), PP + p)),
               bspec, blk(lambda p, n: (cur(n), p)), blk(lambda p, n: (cur(n), p)),
               blk(lambda p, n: (cur(n), p))],
        (q, q, q, kv, kv, kv, kv, bias, do, o, L), after)
    return pl.pallas_call(
        body, grid=(PP, NS + 1), in_specs=in_specs,
        out_specs=[pl.BlockSpec((N_GROUPS, SUPER, LANES), lambda p, n: (0, cur(n), p)),
                   blk(lambda p, n: (lag(n), p)), blk(lambda p, n: (lag(n), p)), bspec],
        out_shape=[jax.ShapeDtypeStruct((N_GROUPS, T, HD), F32), jax.ShapeDtypeStruct((T, HD), F32),
                   jax.ShapeDtypeStruct((T, HD), F32),
                   jax.ShapeDtypeStruct((N_GROUPS, H, BLK, 2 * BLK), F32)],
        scratch_shapes=[pltpu.VMEM((SUPER, LANES), F32), pltpu.VMEM((SUPER, LANES), F32)],
        compiler_params=_params(("arbitrary", "arbitrary")), name=name,
    )(*args)


class _Weights(dict):
    def __init__(self, base, fetch=None, emit=None, emit_small=None):
        super().__init__(base)
        self._fetch, self._emit, self._emit_small = fetch, emit, emit_small

    def fetch(self, group, after):
        if self._fetch is not None:
            for (key, layer), mat in self._fetch(group, after).items():
                self[key][layer] = mat

    def emit(self, group, mats):
        return None if self._emit is None else self._emit(group, mats)

    def emit_small(self, grads):
        return None if self._emit_small is None else self._emit_small(grads)


def _local_step(x, tgt, W):
    T, D = x.shape
    H = W["rel_table"].shape[1] // N_GROUPS
    HD = H * HEAD_DIM
    G = W["a_w_s"].shape[1]
    assert T % (DILATED_GROUPS[-1][1] * BLK) == 0

    tril = jnp.tril(jnp.ones((CHUNK, CHUNK), F32))
    bmap = jnp.asarray(_bucket_maps())
    bias = _band_bias(W["rel_table"], bmap, H, name="band_bias")

    saved = []
    xc, xcb = x, x.astype(BF16)
    kvb = None
    for i in range(DEPTH):
        s = {"x": xc, "xb": xcb}
        W.fetch(4 * i, xc)
        if i < N_A:
            ws_m = W["a_w_s"][i] * tril
            s["ws"] = ws_m.astype(BF16)
            s["wst"] = jnp.swapaxes(ws_m, 1, 2).astype(BF16)
            s["bst"] = W["a_b_s"][i].T
            s["zp"] = _mm(xcb, W["a_w_in"][i], out_dtype=ACT, name=f"a_in_{i}")
            s["y"] = _sgu_fwd(s["zp"], W["a_ln_g"][i], W["a_ln_b"][i], s["ws"], s["bst"], name=f"sgu_fwd_{i}")
            W.fetch(4 * i + 1, s["zp"])
            s["h"] = _mm(s["y"], W["a_w_out"][i], out_dtype=ACT, name=f"a_out_{i}")
        else:
            j = i - N_A
            if kvb is None:
                kvb = _mm(xcb, W["kv_w"][0], name="kv_proj")
            s["q"] = _mm(xcb, W["b_w_q_t"][j], tb=True, scale=HEAD_DIM ** -0.5, name=f"q_proj_{j}")
            s["o"], s["ob"], s["L"] = _attn_fwd_all(s["q"], kvb, bias, name=f"attn_fwd_{j}")
            W.fetch(4 * i + 1, s["q"])
            s["h"] = _mm(s["ob"], W["b_w_o"][j], out_dtype=ACT, name=f"o_proj_{j}")
        s["x1b"], = _add_ln_fwd(xc, s["h"], W["ln_g"][i, 0], W["ln_b"][i, 0], wide=False, name=f"ln1_fwd_{i}")
        s["x1"] = s["x1b"]
        W.fetch(4 * i + 2, s["x1"])
        s["hh"] = _mm(s["x1b"], W["ffn_w_up_t"][i], tb=True, out_dtype=ACT, name=f"ffn_up_{i}")
        s["cw"] = W["ffn_conv_w"][i]
        s["cb"] = W["ffn_conv_b"][i].reshape(1, -1)
        s["act"], s["hca"], s["hcg"] = _convgate_fwd(s["hh"], s["cw"], s["cb"], name=f"convgate_fwd_{i}")
        W.fetch(4 * i + 3, s["hh"])
        s["f"] = _mm(s["act"], W["ffn_w_down"][i], out_dtype=ACT, name=f"ffn_down_{i}")
        outs = _add_ln_fwd(s["x1"], s["f"], W["ln_g"][i, 1], W["ln_b"][i, 1], wide=i == DEPTH - 1,
                           name=f"ln2_fwd_{i}")
        xc, xcb = outs[0], outs[-1]
        saved.append(s)

    dy, lossv = _loss_grad(xc, tgt, name="loss_grad")
    loss = lossv[0, 0]

    gl = {k: [None] * DEPTH for k in ("ffn_w_up_t", "ffn_conv_w", "ffn_conv_b", "ffn_w_down", "ln_g", "ln_b")}
    ga = {k: [None] * N_A for k in ("a_w_in", "a_ln_g", "a_ln_b", "a_w_s", "a_b_s", "a_w_out")}
    gb = {k: [None] * (DEPTH - N_A) for k in ("b_w_q_t", "b_w_o")}
    mats = ("a_w_in", "a_w_out", "b_w_q_t", "b_w_o", "ffn_w_up_t", "ffn_w_down")
    dks, dvs, dbias = [], [], []
    grads = {}
    terms = [(1.0, dy)]
    tok = None
    small_keys = ("ffn_conv_w", "ffn_conv_b", "ln_g", "ln_b", "a_ln_g", "a_ln_b", "a_w_s", "a_b_s")
    for i in reversed(range(DEPTH)):
        s = saved[i]
        dp2b, dg2, db2 = _add_ln_bwd(s["x1"], s["f"], W["ln_g"][i, 1], terms, after=tok, name=f"ln2_bwd_{i}")
        dact = _mm(dp2b, W["ffn_w_down"][i], tb=True, out_dtype=ACT, name=f"ffn_down_dx_{i}")
        gl["ffn_w_down"][i] = _mm(s["act"], dp2b, ta=True, out_dtype=BF16, name=f"ffn_down_dw_{i}")
        dha, dhg, dwa, dwg, dba, dbg = _convgate_bwd(s["hh"], s["hca"], s["hcg"], dact, s["cw"],
                                                     name=f"convgate_bwd_{i}")
        dhh = (dha, dhg)
        gl["ffn_conv_w"][i] = jnp.concatenate([dwa, dwg], axis=1)
        gl["ffn_conv_b"][i] = jnp.concatenate([dba, dbg], axis=1)[0]
        dx1 = _mm(dhh, W["ffn_w_up_t"][i], out_dtype=ACT, name=f"ffn_up_dx_{i}")
        gl["ffn_w_up_t"][i] = _mm(dhh, s["x1b"], ta=True, out_dtype=BF16, name=f"ffn_up_dw_{i}")
        tok = W.emit(3 * i + 2, {("ffn_w_up_t", i): gl["ffn_w_up_t"][i], ("ffn_w_down", i): gl["ffn_w_down"][i]})
        dp1b, dg1, db1 = _add_ln_bwd(s["x"], s["h"], W["ln_g"][i, 0], [(ALPHA, dp2b), (1.0, dx1)],
                                     after=tok, name=f"ln1_bwd_{i}")
        gl["ln_g"][i] = jnp.concatenate([dg1, dg2], axis=0)
        gl["ln_b"][i] = jnp.concatenate([db1, db2], axis=0)
        terms = [(ALPHA, dp1b)]
        if i < N_A:
            dyy = _mm(dp1b, W["a_w_out"][i], tb=True, out_dtype=ACT, name=f"a_out_dx_{i}")
            ga["a_w_out"][i] = _mm(s["y"], dp1b, ta=True, out_dtype=BF16, name=f"a_out_dw_{i}")
            if i == 0:
                tok = W.emit(3 * i + 1, {("a_w_out", i): ga["a_w_out"][i]})
            dzp, dlg, dlb, dws, dbs = _sgu_bwd(s["zp"], dyy, W["a_ln_g"][i], W["a_ln_b"][i], s["ws"],
                                               s["wst"], s["bst"], after=tok, name=f"sgu_bwd_{i}")
            ga["a_ln_g"][i], ga["a_ln_b"][i], ga["a_w_s"][i] = dlg[0], dlb[0], dws
            ga["a_b_s"][i] = dbs[:, :G].T
            if i == 0:
                for dct in (gl, ga):
                    grads.update({k: jnp.stack(v) for k, v in dct.items() if k in small_keys})
                tok = W.emit_small(grads)
            ga["a_w_in"][i] = _mm(s["xb"], dzp, ta=True, out_dtype=BF16, after=tok, name=f"a_in_dw_{i}")
            out_a = {("a_w_in", i): ga["a_w_in"][i]}
            if i > 0:
                out_a[("a_w_out", i)] = ga["a_w_out"][i]
            tok = W.emit(3 * i, out_a)
            terms.append((1.0, _mm(dzp, W["a_w_in"][i], tb=True, out_dtype=ACT, after=tok, name=f"a_in_dx_{i}")))
        else:
            j = i - N_A
            do = _mm(dp1b, W["b_w_o"][j], tb=True, name=f"o_proj_dx_{j}")
            gb["b_w_o"][j] = _mm(s["ob"], dp1b, ta=True, out_dtype=BF16, name=f"o_proj_dw_{j}")
            dq, dk_j, dv_j, db_j = _attn_bwd_all(s["q"], kvb, bias, do, s["o"], s["L"], after=tok,
                                                 name=f"attn_bwd_{j}")
            dks.append((1.0, dk_j))
            dvs.append((1.0, dv_j))
            dbias.append(db_j)
            terms.append((1.0, _mm(dq, W["b_w_q_t"][j], out_dtype=ACT, name=f"q_proj_dx_{j}")))
            gb["b_w_q_t"][j] = _mm(dq, s["xb"], ta=True, out_dtype=BF16, name=f"q_proj_dw_{j}")
            out_b = {("b_w_q_t", j): gb["b_w_q_t"][j], ("b_w_o", j): gb["b_w_o"][j]}
            if i == N_A:
                dkv = jnp.concatenate([_lincomb(dks, BF16, name="dk_sum"), _lincomb(dvs, BF16, name="dv_sum")],
                                      axis=1)
                terms.append((1.0, _mm(dkv, W["kv_w"][0], tb=True, out_dtype=ACT, name="kv_proj_dx")))
                grads["kv_w"] = [_mm(s["xb"], dkv, ta=True, out_dtype=BF16, name="kv_proj_dw")]
                out_b[("kv_w", 0)] = grads["kv_w"][0]
                dbt = _lincomb([(1.0, a.reshape(-1, 2 * BLK)) for a in dbias], F32, name="dbias_sum")
                dtab = _band_bias_bwd(dbt.reshape(N_GROUPS, H, BLK, 2 * BLK), bmap, H, name="band_bias_bwd")
                grads["rel_table"] = jnp.transpose(dtab[:, :, :H], (1, 0, 2)).reshape(REL_BUCKETS, N_GROUPS * H)
            tok = W.emit(3 * i, out_b)
    grad_x = _lincomb(terms, F32, name="grad_x")
    for dct in (gl, ga, gb):
        grads.update({k: v for k, v in dct.items() if k in mats})
    return loss, grad_x, grads


def _my_index():
    return 4 * lax.axis_index("x") + 2 * lax.axis_index("y") + lax.axis_index("c")


HBM_SPEC = pl.BlockSpec(memory_space=pltpu.HBM)


def _block(ref, k, n, axis):
    off = pl.multiple_of(k * n, n)
    return ref.at[pl.ds(off, n), :] if axis == 0 else ref.at[:, pl.ds(off, n)]


SEM_SPEC =pl.BlockSpec(memory_space=pltpu.SEMAPHORE)
FLOWING = pltpu.SideEffectType.DATAFLOW_SIDE_EFFECTING


def _peers(x, y, c):
    return [(1 - x if k & 4 else x, 1 - y if k & 2 else y, 1 - c if k & 1 else c) for k in range(1, N_DEV)]


def _ends(src_ref, land_ref, peer_index, me, n, axis, gather):
    if gather:
        return src_ref, _block(land_ref, me, n, axis)
    return _block(src_ref, peer_index, n, axis), land_ref.at[me]


def _send_start(groups, gather, *, name):
    flat = [(g, j, mat, axis) for g, items in enumerate(groups) for j, (mat, axis) in enumerate(items)]
    M, G = len(flat), len(groups)
    lands, ns = [], []
    for _, _, mat, axis in flat:
        A, B = mat.shape
        if gather:
            lands.append((A * N_DEV, B) if axis == 0 else (A, B * N_DEV))
            ns.append(A if axis == 0 else B)
        else:
            lands.append((N_DEV, A // N_DEV, B) if axis == 0 else (N_DEV, A, B // N_DEV))
            ns.append(A // N_DEV if axis == 0 else B // N_DEV)

    def body(*refs):
        src_refs, land_refs, sems = refs[:M], refs[M:2 * M], refs[2 * M:2 * M + 3 * G]
        token = refs[-1]
        x, y, c = lax.axis_index("x"), lax.axis_index("y"), lax.axis_index("c")
        me = 4 * x + 2 * y + c
        for i, (g, j, _, axis) in enumerate(flat):
            for k, (px, py, pc) in enumerate(_peers(x, y, c)):
                s, d = _ends(src_refs[i], land_refs[i], 4 * px + 2 * py + pc, me, ns[i], axis, gather)
                pltpu.make_async_remote_copy(
                    src_ref=s, dst_ref=d, send_sem=sems[3 * g].at[7 * j + k], recv_sem=sems[3 * g + 1].at[7 * j + k],
                    device_id=(px, py, pc), device_id_type=MESH).start()
            s, d = _ends(src_refs[i], land_refs[i], me, me, ns[i], axis, gather)
            pltpu.make_async_copy(s, d, sems[3 * g + 2].at[j]).start()
        token[...] = jnp.zeros_like(token)

    sem_shapes = []
    for items in groups:
        sem_shapes += [pltpu.SemaphoreType.DMA((7 * len(items),))] * 2 + [pltpu.SemaphoreType.DMA((len(items),))]
    outs = pl.pallas_call(
        body, name=name,
        out_shape=(*sem_shapes, *[pltpu.HBM(m.shape, m.dtype) for _, _, m, _ in flat],
                   *[pltpu.HBM(shp, m.dtype) for shp, (_, _, m, _) in zip(lands, flat)],
                   jax.ShapeDtypeStruct((8, LANES), F32)),
        in_specs=[HBM_SPEC] * (2 * M),
        out_specs=(*[SEM_SPEC] * (3 * G), *[HBM_SPEC] * (2 * M), pl.BlockSpec(memory_space=pltpu.VMEM)),
        input_output_aliases={i: 3 * G + i for i in range(2 * M)},
        compiler_params=pltpu.CompilerParams(has_side_effects=FLOWING),
    )(*[pltpu.with_memory_space_constraint(m, pltpu.HBM) for _, _, m, _ in flat],
      *[pltpu.with_memory_space_constraint(lax.empty(shp, m.dtype), pltpu.HBM)
        for shp, (_, _, m, _) in zip(lands, flat)])
    handles = []
    for g in range(G):
        idx = [i for i, f in enumerate(flat) if f[0] == g]
        handles.append((outs[3 * g], outs[3 * g + 1], outs[3 * g + 2], [outs[3 * G + i] for i in idx],
                        [outs[3 * G + M + i] for i in idx], [flat[i][3] for i in idx]))
    return handles, outs[-1]


def _send_wait(handle, gather, after, *, name):
    send_sems, recv_sems, local_sems, mats, lands, axes = handle
    n_m = len(mats)
    ns = []
    for mat, land, axis in zip(mats, lands, axes):
        ns.append(mat.shape[axis] if gather else land.shape[1 + axis])

    def body(*refs):
        src_refs, land_refs = refs[:n_m], refs[n_m:2 * n_m]
        ssem, rsem, lsem = refs[2 * n_m:2 * n_m + 3]
        x, y, c = lax.axis_index("x"), lax.axis_index("y"), lax.axis_index("c")
        me = 4 * x + 2 * y + c
        for j in range(n_m):
            for k, (px, py, pc) in enumerate(_peers(x, y, c)):
                s, d = _ends(src_refs[j], land_refs[j], 4 * px + 2 * py + pc, me, ns[j], axes[j], gather)
                cp = pltpu.make_async_remote_copy(
                    src_ref=s, dst_ref=d, send_sem=ssem.at[7 * j + k], recv_sem=rsem.at[7 * j + k],
                    device_id=(px, py, pc), device_id_type=MESH)
                cp.wait_send()
                cp.wait_recv()
            s, d = _ends(src_refs[j], land_refs[j], me, me, ns[j], axes[j], gather)
            pltpu.make_async_copy(s, d, lsem.at[j]).wait()

    outs = pl.pallas_call(
        body, name=name,
        out_shape=(*[pltpu.HBM(m.shape, m.dtype) for m in mats], *[pltpu.HBM(l.shape, l.dtype) for l in lands]),
        in_specs=[HBM_SPEC] * (2 * n_m) + [SEM_SPEC] * 3 + [pl.BlockSpec(memory_space=pl.ANY)],
        out_specs=tuple([HBM_SPEC] * (2 * n_m)),
        input_output_aliases={i: i for i in range(2 * n_m)},
        compiler_params=pltpu.CompilerParams(has_side_effects=FLOWING),
    )(*mats, *lands, send_sems, recv_sems, local_sems, after)
    return list(outs[n_m:])


def _sum_parts(parts, *, name):
    n, R, C = parts.shape
    rb = _pick(R, 512) if R % LANES == 0 else R

    def body(p_ref, o_ref):
        acc = p_ref[0].astype(F32)
        for k in range(1, n):
            acc = acc + p_ref[k].astype(F32)
        o_ref[...] = acc

    return pl.pallas_call(
        body, grid=(R // rb,), in_specs=[pl.BlockSpec((n, rb, C), lambda i: (0, i, 0))],
        out_specs=pl.BlockSpec((rb, C), lambda i: (i, 0)),
        out_shape=jax.ShapeDtypeStruct((R, C), F32),
        compiler_params=_params(("parallel",)), name=name,
    )(parts)


def _adamw(w, m, v, parts, *, name):
    L, R, C = w.shape
    n = parts[0].shape[0]
    cap = max(16, VMEM_LIMIT // 2 // (2 * L * n * C * parts[0].dtype.itemsize))
    rb = max([r for r in range(16, min(R, cap) + 1, 16) if R % r == 0], default=R)

    def body(w_ref, m_ref, v_ref, *rest):
        p_refs = rest[:L]
        g_ref, d_ref, nm_ref, nv_ref = rest[L:]
        for l in range(L):
            @pl.when(pl.program_id(0) == l)
            def _(p_ref=p_refs[l]):
                g = p_ref[0].astype(F32)
                for k in range(1, n):
                    g = g + p_ref[k].astype(F32)
                mn = ADAM_B1 * m_ref[...] + (1.0 - ADAM_B1) * g
                vn = ADAM_B2 * v_ref[...] + (1.0 - ADAM_B2) * jnp.square(g)
                m_hat = mn / (1.0 - ADAM_B1 ** ADAM_STEP)
                v_hat = vn / (1.0 - ADAM_B2 ** ADAM_STEP)
                g_ref[...] = g
                d_ref[...] = -ADAM_LR * (m_hat / (jnp.sqrt(v_hat) + ADAM_EPS) + ADAM_WD * w_ref[...])
                nm_ref[...] = mn
                nv_ref[...] = vn

    row = pl.BlockSpec((None, rb, C), lambda l, i: (l, i, 0))
    part = lambda k: pl.BlockSpec((n, rb, C), lambda l, i: (0, jnp.where(l == k, i, 0), 0))
    return pl.pallas_call(
        body, grid=(L, R // rb), in_specs=[row, row, row] + [part(k) for k in range(L)],
        out_specs=[row] * 4, out_shape=[jax.ShapeDtypeStruct((L, R, C), F32)] * 4,
        compiler_params=_params(("arbitrary", "arbitrary")), name=name,
    )(w, m, v, *parts)


BIG = (("a_w_in", "a_w_in", 1, False), ("a_w_out", "a_w_out", 0, False), ("kv_w", "kv_w", 0, False),
       ("b_w_q", "b_w_q_t", 0, True), ("b_w_o", "b_w_o", 1, False), ("ffn_w_up", "ffn_w_up_t", 0, True),
       ("ffn_w_down", "ffn_w_down", 0, False))
SMALL_SHARDED = (("a_ln_g", 1), ("a_ln_b", 1), ("ffn_conv_w", 2), ("ln_g", 2), ("ln_b", 2))
REPLICATED = ("a_w_s", "a_b_s", "rel_table", "ffn_conv_b")


def _pack_rows(arrs, lead=0):
    lshape = arrs[0].shape[:lead]
    p = jnp.concatenate([a.reshape(*lshape, -1, LANES) for a in arrs], axis=lead)
    pad = -p.shape[lead] % 8
    return jnp.pad(p, [(0, 0)] * lead + [(0, pad), (0, 0)])


def _unpack_rows(packed, shapes, lead=0):
    lshape = packed.shape[:lead]
    out, off = [], 0
    for shp in shapes:
        r = int(np.prod(shp)) // LANES
        out.append(lax.slice_in_dim(packed, off, off + r, axis=lead).reshape(*lshape, *shp))
        off += r
    return out


def _as_mats(a, transposed):
    a = a[None] if a.ndim == 2 else a
    return jnp.swapaxes(a, 1, 2) if transposed else a


def _merge_shards(stacked, axis):
    a = jnp.moveaxis(stacked, 0, axis)
    shp = list(a.shape)
    return a.reshape(shp[:axis] + [shp[axis] * shp[axis + 1]] + shp[axis + 2:])


def _split_shards(full, axis):
    shp = list(full.shape)
    a = full.reshape(shp[:axis] + [N_DEV, shp[axis] // N_DEV] + shp[axis + 1:])
    return jnp.moveaxis(a, axis, 0)


def kernel(x, a_w_in, a_ln_g, a_ln_b, a_w_s, a_b_s, a_w_out, kv_w, b_w_q, b_w_o, rel_table, ffn_w_up, ffn_conv_w, ffn_conv_b, ffn_w_down, ln_g, ln_b, loss_target, m_a_w_in, m_a_ln_g, m_a_ln_b, m_a_w_s, m_a_b_s, m_a_w_out, m_kv_w, m_b_w_q, m_b_w_o, m_rel_table, m_ffn_w_up, m_ffn_conv_w, m_ffn_conv_b, m_ffn_w_down, m_ln_g, m_ln_b, v_a_w_in, v_a_ln_g, v_a_ln_b, v_a_w_s, v_a_b_s, v_a_w_out, v_kv_w, v_b_w_q, v_b_w_o, v_rel_table, v_ffn_w_up, v_ffn_conv_w, v_ffn_conv_b, v_ffn_w_down, v_ln_g, v_ln_b):
    names = ["a_w_in", "a_ln_g", "a_ln_b", "a_w_s", "a_b_s", "a_w_out", "kv_w", "b_w_q", "b_w_o", "rel_table",
             "ffn_w_up", "ffn_conv_w", "ffn_conv_b", "ffn_w_down", "ln_g", "ln_b"]
    w = dict(zip(names, (a_w_in, a_ln_g, a_ln_b, a_w_s, a_b_s, a_w_out, kv_w, b_w_q, b_w_o, rel_table,
                         ffn_w_up, ffn_conv_w, ffn_conv_b, ffn_w_down, ln_g, ln_b)))
    m = dict(zip(names, (m_a_w_in, m_a_ln_g, m_a_ln_b, m_a_w_s, m_a_b_s, m_a_w_out, m_kv_w, m_b_w_q, m_b_w_o,
                         m_rel_table, m_ffn_w_up, m_ffn_conv_w, m_ffn_conv_b, m_ffn_w_down, m_ln_g, m_ln_b)))
    v = dict(zip(names, (v_a_w_in, v_a_ln_g, v_a_ln_b, v_a_w_s, v_a_b_s, v_a_w_out, v_kv_w, v_b_w_q, v_b_w_o,
                         v_rel_table, v_ffn_w_up, v_ffn_conv_w, v_ffn_conv_b, v_ffn_w_down, v_ln_g, v_ln_b)))
    small_names = [n for n, _ in SMALL_SHARDED]
    small_shapes = [w[n].shape for n in small_names]
    rep_shapes = [w[n].shape for n in REPLICATED]

    axis_of = {key: axis for _, key, axis, _ in BIG}
    src = {}
    for n, key, axis, tr in BIG:
        loc = _as_mats(w[n], tr).astype(BF16)
        for l in range(loc.shape[0]):
            src[(key, l)] = loc[l]
    order = []
    for i in range(DEPTH):
        if i < N_A:
            order += [[("a_w_in", i)], [("a_w_out", i)]]
        else:
            order += [([("kv_w", 0)] if i == N_A else []) + [("b_w_q_t", i - N_A)], [("b_w_o", i - N_A)]]
        order += [[("ffn_w_up_t", i)], [("ffn_w_down", i)]]
    small_src = _pack_rows([w[n] for n in small_names])
    srows = small_src.shape[0]
    handles, _ = _send_start([[(small_src, 0)]] + [[(src[kl], axis_of[kl[0]]) for kl in grp] for grp in order],
                             True, name="gather_start")
    small_all = _send_wait(handles[0], True, x, name="gather_wait_small")[0]
    small_st = _unpack_rows(small_all.reshape(N_DEV, srows, LANES), small_shapes, lead=1)
    base = {n: w[n] for n in REPLICATED}
    for (n, ax), st in zip(SMALL_SHARDED, small_st):
        base[n] = _merge_shards(st, ax)
    for n, key, _, tr in BIG:
        base[key] = [None] * (1 if w[n].ndim == 2 else w[n].shape[0])

    def fetch(group, after):
        mats = _send_wait(handles[1 + group], True, after, name=f"gather_wait_{group}")
        return dict(zip(order[group], mats))

    sent = {}

    def emit(group, mats):
        keys = list(mats)
        hs, token = _send_start([[(mats[kl], axis_of[kl[0]]) for kl in keys]], False, name=f"exchange_start_{group}")
        sent[group] = (keys, hs[0])
        return token

    small_sent = []

    def emit_small(grads):
        small_pack = _pack_rows([_split_shards(grads[n], ax) for n, ax in SMALL_SHARDED], lead=1)
        rest = _pack_rows([grads[n] for n in REPLICATED[1:]])
        mine = jnp.concatenate([small_pack.reshape(N_DEV * srows, LANES), rest], axis=0)
        gating = grads[REPLICATED[0]].reshape(-1, LANES).astype(BF16)
        hs, token = _send_start([[(mine, 0), (gating, 0)]], True, name="small_grads_start")
        small_sent.append(hs[0])
        return token

    loss, grad_x, grads = _local_step(x[0], loss_target[0], _Weights(base, fetch, emit, emit_small))
    loss = lax.psum(loss, ("x", "y", "c"))
    out = {}

    landed = {}
    last = grad_x
    left = lambda e: min(g for g, (keys, _) in sent.items() if any(k[0] == e[1] for k in keys))
    for n, key, axis, tr in sorted(BIG, key=left, reverse=True):
        shp = w[n].shape
        for group in sorted(sent, reverse=True):
            keys, h = sent[group]
            if keys[0] not in landed and any(k[0] == key for k in keys):
                landed.update(zip(keys, _send_wait(h, False, last, name=f"exchange_wait_{group}")))
        parts = [landed[(key, l)] for l in range(1 if len(shp) == 2 else shp[0])]
        res = _adamw(_as_mats(w[n], tr), _as_mats(m[n], tr), _as_mats(v[n], tr), parts, name=f"adamw_{n}")
        out[n] = [(jnp.swapaxes(r, 1, 2) if tr else r).reshape(shp) for r in res]
        last = res[0]

    allp, allg = _send_wait(small_sent[0], True, last, name="small_grads_wait")
    gsum = _sum_parts(allp.reshape(N_DEV, -1, LANES), name="sum_small_grads")
    gating = _sum_parts(allg.reshape(N_DEV, -1, LANES), name="sum_gating_grads")
    g_small = lax.dynamic_slice_in_dim(gsum, _my_index() * srows, srows, axis=0)
    pack_sr = lambda d: jnp.concatenate([_pack_rows([d[n] for n in small_names]),
                                         _pack_rows([d[n] for n in REPLICATED])], axis=0)
    n_rest = sum(int(np.prod(s)) for s in rep_shapes[1:]) // LANES
    gs_in = jnp.concatenate([g_small, gating, gsum[N_DEV * srows:N_DEV * srows + n_rest]], axis=0)
    gs_in = jnp.pad(gs_in, ((0, pack_sr(w).shape[0] - gs_in.shape[0]), (0, 0)))[None]
    res = _adamw(pack_sr(w)[None], pack_sr(m)[None], pack_sr(v)[None], [gs_in], name="adamw_small")
    for n, vals in zip(small_names, zip(*[_unpack_rows(r[0, :srows], small_shapes) for r in res])):
        out[n] = list(vals)
    for n, vals in zip(REPLICATED, zip(*[_unpack_rows(r[0, srows:], rep_shapes) for r in res])):
        out[n] = list(vals)

    return (loss, grad_x[None], *[out[n][0] for n in names], *[out[n][1] for n in names],
            *[out[n][2] for n in names], *[out[n][3] for n in names])
```

```python
import math

import numpy as np
import jax
import jax.numpy as jnp
from jax import lax
from jax.experimental import pallas as pl
from jax.experimental.pallas import tpu as pltpu

F32 = jnp.float32
BF16 = jnp.bfloat16
ACT = jnp.bfloat16
MESH = pl.DeviceIdType.MESH

N_DEV = 8
DEPTH = 4
N_A = 2
CHUNK = 128
BLK = 128
HEAD_DIM = 64
DILATED_GROUPS = ((128, 1), (512, 4), (2048, 16))
N_GROUPS = 3
REL_BUCKETS = 32
REL_MAX_DIST = 2048
ALPHA = (2 * DEPTH) ** 0.25
LN_EPS = 1e-5
NEG = -1e30
ADAM_LR = 0.001
ADAM_B1 = 0.9
ADAM_B2 = 0.999
ADAM_EPS = 1e-08
ADAM_WD = 0.01
ADAM_STEP = 10

LANES = 128
VMEM_LIMIT = 56 * 1024 * 1024
MM_TILE_CAP = 1408
MM_VMEM_BUDGET = 46 * 1024 * 1024
INV_SQRT2 = 1.0 / math.sqrt(2.0)
INV_SQRT_2PI = 1.0 / math.sqrt(2.0 * math.pi)


def _pick(n, cap):
    best = None
    for t in range(LANES, min(n, cap) + 1, LANES):
        if n % t == 0:
            best = t
    return best if best is not None else n


def _params(sem):
    return pltpu.CompilerParams(dimension_semantics=sem, vmem_limit_bytes=VMEM_LIMIT)


def _ordered(body, in_specs, args, after):
    if after is None:
        return body, list(in_specs), tuple(args)
    return (lambda _, *refs: body(*refs)), [pl.BlockSpec(memory_space=pl.ANY), *in_specs], (after, *args)


def _gelu(x):
    return 0.5 * x * (1.0 + lax.erf(x * INV_SQRT2))


def _gelu_grad(x):
    return 0.5 * (1.0 + lax.erf(x * INV_SQRT2)) + x * jnp.exp(-0.5 * x * x) * INV_SQRT_2PI


def _mm(a, b, *, ta=False, tb=False, out_dtype=F32, scale=None, after=None, name):
    halves = isinstance(a, tuple)
    parts = 1 if halves or a.ndim == 2 else a.shape[0]
    ash = (a[0].shape[0], 2 * a[0].shape[1]) if halves else (a.shape if parts == 1
                                                               else (a.shape[1], parts * a.shape[2]))
    if ta:
        K, M = ash
    else:
        M, K = ash
    if tb:
        N, Kb = b.shape
    else:
        Kb, N = b.shape
    assert K == Kb, (ash, b.shape, ta, tb)
    split = 2 if halves else parts
    tm = _pick(M // split if ta else M, MM_TILE_CAP)
    tn = _pick(N, MM_TILE_CAP)
    kspan = K if ta or split == 1 else K // split
    abytes = (a[0] if halves else a).dtype.itemsize * (2 if halves else 1)
    fixed = 2 * tm * tn * jnp.dtype(out_dtype).itemsize + tm * tn * 4
    fits = [t for t in range(LANES, kspan + 1, LANES)
            if kspan % t == 0 and 2 * t * (tm * abytes + tn * b.dtype.itemsize) + fixed <= MM_VMEM_BUDGET]
    tk = max(fits) if fits else _pick(kspan, MM_TILE_CAP)
    nk = K // tk
    nh = (M // split // tm if ta else K // split // tk) if split > 1 else 0
    dn = (((0 if ta else 1,), (1 if tb else 0,)), ((), ()))

    def body(*refs):
        n_tail = 3 if nk > 1 else 2
        a_refs, b_ref, o_ref = refs[:-n_tail], refs[-n_tail], refs[-n_tail + 1]
        k = pl.program_id(2)

        def finish(r):
            if scale is not None:
                r = r * scale
            o_ref[...] = r.astype(out_dtype)

        def accumulate(a_ref):
            part = lax.dot_general(a_ref[...].astype(BF16), b_ref[...].astype(BF16), dn,
                                   preferred_element_type=F32)
            if nk == 1:
                finish(part)
                return
            acc_ref = refs[-1]

            @pl.when(k == 0)
            def _():
                acc_ref[...] = part

            @pl.when((k > 0) & (k < nk - 1))
            def _():
                acc_ref[...] += part

            @pl.when(k == nk - 1)
            def _():
                finish(acc_ref[...] + part)

        if halves:
            first = (pl.program_id(0) if ta else k) < nh
            pl.when(first)(lambda: accumulate(a_refs[0]))
            pl.when(jnp.logical_not(first))(lambda: accumulate(a_refs[1]))
        else:
            accumulate(a_refs[0])

    if halves and ta:
        a_specs = [pl.BlockSpec((tk, tm), lambda i, j, k: (jnp.where(i < nh, k, 0), jnp.minimum(i, nh - 1))),
                   pl.BlockSpec((tk, tm), lambda i, j, k: (jnp.where(i >= nh, k, 0), jnp.maximum(i - nh, 0)))]
    elif halves:
        a_specs = [pl.BlockSpec((tm, tk), lambda i, j, k: (i, jnp.minimum(k, nh - 1))),
                   pl.BlockSpec((tm, tk), lambda i, j, k: (i, jnp.maximum(k - nh, 0)))]
    elif parts > 1:
        a_specs = [pl.BlockSpec((None, tk, tm), lambda i, j, k: (i // nh, k, i % nh)) if ta
                   else pl.BlockSpec((None, tm, tk), lambda i, j, k: (k // nh, i, k % nh))]
    else:
        a_specs = [pl.BlockSpec((tk, tm), lambda i, j, k: (k, i)) if ta
                   else pl.BlockSpec((tm, tk), lambda i, j, k: (i, k))]
    b_spec = (pl.BlockSpec((tn, tk), lambda i, j, k: (j, k)) if tb
              else pl.BlockSpec((tk, tn), lambda i, j, k: (k, j)))
    body, in_specs, args = _ordered(body, [*a_specs, b_spec], (*(a if halves else (a,)), b), after)
    return pl.pallas_call(
        body, grid=(M // tm, N // tn, nk), in_specs=in_specs,
        out_specs=pl.BlockSpec((tm, tn), lambda i, j, k: (i, j)),
        out_shape=jax.ShapeDtypeStruct((M, N), out_dtype),
        scratch_shapes=[pltpu.VMEM((tm, tn), F32)] if nk > 1 else [],
        compiler_params=_params(("parallel", "parallel", "arbitrary")), name=name,
    )(*args)


def _add_ln_fwd(x, h, g, b, *, wide, name):
    T, D = x.shape
    rb = _pick(T, 512)

    def body(x_ref, h_ref, g_ref, b_ref, *o_refs):
        pre = ALPHA * x_ref[...].astype(F32) + h_ref[...].astype(F32)
        mu = jnp.mean(pre, axis=1, keepdims=True)
        cen = pre - mu
        var = jnp.mean(cen * cen, axis=1, keepdims=True)
        y = cen * lax.rsqrt(var + LN_EPS) * g_ref[...] + b_ref[...]
        for o_ref in o_refs:
            o_ref[...] = y.astype(o_ref.dtype)

    row = pl.BlockSpec((rb, D), lambda i: (i, 0))
    vec = pl.BlockSpec((1, D), lambda i: (0, 0))
    dtypes = [F32, BF16] if wide else [BF16]
    return pl.pallas_call(
        body, grid=(T // rb,), in_specs=[row, row, vec, vec], out_specs=[row] * len(dtypes),
        out_shape=[jax.ShapeDtypeStruct((T, D), dt) for dt in dtypes],
        compiler_params=_params(("parallel",)), name=name,
    )(x, h, g.reshape(1, D), b.reshape(1, D))


def _add_ln_bwd(x, h, g, terms, *, after=None, name):
    T, D = x.shape
    rb = _pick(T, 512)
    coefs = [c for c, _ in terms]
    nt = len(terms)

    def body(*refs):
        x_ref, h_ref, g_ref = refs[:3]
        t_refs = refs[3:3 + nt]
        dpb_ref, dg_ref, db_ref = refs[3 + nt:]
        dy = None
        for c, r in zip(coefs, t_refs):
            v = r[...].astype(F32) if c == 1.0 else c * r[...].astype(F32)
            dy = v if dy is None else dy + v
        pre = ALPHA * x_ref[...].astype(F32) + h_ref[...].astype(F32)
        mu = jnp.mean(pre, axis=1, keepdims=True)
        cen = pre - mu
        var = jnp.mean(cen * cen, axis=1, keepdims=True)
        rstd = lax.rsqrt(var + LN_EPS)
        xhat = cen * rstd
        dxh = dy * g_ref[...]
        m1 = jnp.mean(dxh, axis=1, keepdims=True)
        m2 = jnp.mean(dxh * xhat, axis=1, keepdims=True)
        dpre = rstd * (dxh - m1 - xhat * m2)
        dpb_ref[...] = dpre.astype(BF16)
        dg = jnp.sum(dy * xhat, axis=0, keepdims=True)
        db = jnp.sum(dy, axis=0, keepdims=True)

        @pl.when(pl.program_id(0) == 0)
        def _():
            dg_ref[...] = dg
            db_ref[...] = db

        @pl.when(pl.program_id(0) > 0)
        def _():
            dg_ref[...] += dg
            db_ref[...] += db

    row = pl.BlockSpec((rb, D), lambda i: (i, 0))
    vec = pl.BlockSpec((1, D), lambda i: (0, 0))
    body, in_specs, args = _ordered(body, [row, row, vec] + [row] * nt,
                                    (x, h, g.reshape(1, D), *[a for _, a in terms]), after)
    return pl.pallas_call(
        body, grid=(T // rb,), in_specs=in_specs,
        out_specs=[row, vec, vec],
        out_shape=[jax.ShapeDtypeStruct((T, D), BF16),
                   jax.ShapeDtypeStruct((1, D), F32), jax.ShapeDtypeStruct((1, D), F32)],
        compiler_params=_params(("arbitrary",)), name=name,
    )(*args)


def _lincomb(terms, out_dtype, *, name):
    R, C = terms[0][1].shape
    rb = _pick(R, 512)
    coefs = [c for c, _ in terms]
    nt = len(terms)

    def body(*refs):
        acc = None
        for c, r in zip(coefs, refs[:nt]):
            v = r[...].astype(F32)
            v = v if c == 1.0 else c * v
            acc = v if acc is None else acc + v
        refs[nt][...] = acc.astype(out_dtype)

    row = pl.BlockSpec((rb, C), lambda i: (i, 0))
    return pl.pallas_call(
        body, grid=(R // rb,), in_specs=[row] * nt, out_specs=row,
        out_shape=jax.ShapeDtypeStruct((R, C), out_dtype),
        compiler_params=_params(("parallel",)), name=name,
    )(*[a for _, a in terms])


def _loss_grad(y, tgt, *, name):
    T, D = y.shape
    rb = _pick(T, 512)

    def body(y_ref, t_ref, dy_ref, l_ref):
        err = y_ref[...] - t_ref[...]
        dy_ref[...] = err * (1.0 / D)
        part = jnp.sum(jnp.sum(err * err, axis=1, keepdims=True), axis=0, keepdims=True) * (0.5 / D)
        part = jnp.broadcast_to(part, (1, LANES))

        @pl.when(pl.program_id(0) == 0)
        def _():
            l_ref[...] = part

        @pl.when(pl.program_id(0) > 0)
        def _():
            l_ref[...] += part

    row = pl.BlockSpec((rb, D), lambda i: (i, 0))
    return pl.pallas_call(
        body, grid=(T // rb,), in_specs=[row, row],
        out_specs=[row, pl.BlockSpec((1, LANES), lambda i: (0, 0))],
        out_shape=[jax.ShapeDtypeStruct((T, D), F32), jax.ShapeDtypeStruct((1, LANES), F32)],
        compiler_params=_params(("arbitrary",)), name=name,
    )(y, tgt)


def _sgu_fwd(zp, ln_g, ln_b, ws, bst, *, name):
    T, E2 = zp.shape
    E = E2 // 2
    G = ws.shape[0]
    cg = E // G
    rb = 2 * CHUNK

    def body(z_ref, g_ref, b_ref, ws_ref, bs_ref, y_ref):
        u = _gelu(z_ref[:, :E].astype(F32))
        v = _gelu(z_ref[:, E:].astype(F32))
        mu = jnp.mean(v, axis=1, keepdims=True)
        cen = v - mu
        var = jnp.mean(cen * cen, axis=1, keepdims=True)
        vn = (cen * lax.rsqrt(var + LN_EPS) * g_ref[...] + b_ref[...]).astype(BF16)
        for ci in range(rb // CHUNK):
            rows = slice(ci * CHUNK, (ci + 1) * CHUNK)
            for gi in range(G):
                cols = slice(gi * cg, (gi + 1) * cg)
                sv = jnp.dot(ws_ref[gi], vn[rows, cols], preferred_element_type=F32)
                sv = sv + bs_ref[:, gi:gi + 1]
                y_ref[rows, cols] = (u[rows, cols] * sv).astype(BF16)

    return pl.pallas_call(
        body, grid=(T // rb,),
        in_specs=[pl.BlockSpec((rb, E2), lambda i: (i, 0)),
                  pl.BlockSpec((1, E), lambda i: (0, 0)), pl.BlockSpec((1, E), lambda i: (0, 0)),
                  pl.BlockSpec((G, CHUNK, CHUNK), lambda i: (0, 0, 0)),
                  pl.BlockSpec((CHUNK, G), lambda i: (0, 0))],
        out_specs=pl.BlockSpec((rb, E), lambda i: (i, 0)),
        out_shape=jax.ShapeDtypeStruct((T, E), BF16),
        compiler_params=_params(("parallel",)), name=name,
    )(zp, ln_g.reshape(1, E), ln_b.reshape(1, E), ws, bst)


def _sgu_bwd(zp, dy, ln_g, ln_b, ws, wst, bst, *, after=None, name):
    T, E2 = zp.shape
    E = E2 // 2
    G = ws.shape[0]
    cg = E // G
    rb = CHUNK
    nsteps = T // rb

    def body(z_ref, dy_ref, g_ref, b_ref, ws_ref, wst_ref, bs_ref,
             dz_ref, dg_ref, db_ref, dws_ref, dbs_ref, dsv_acc):
        step = pl.program_id(0)

        @pl.when(step == 0)
        def _():
            dg_ref[...] = jnp.zeros_like(dg_ref)
            db_ref[...] = jnp.zeros_like(db_ref)
            dws_ref[...] = jnp.zeros_like(dws_ref)
            dsv_acc[...] = jnp.zeros_like(dsv_acc)

        zu = z_ref[:, :E].astype(F32)
        zv = z_ref[:, E:].astype(F32)
        u = _gelu(zu)
        v = _gelu(zv)
        mu = jnp.mean(v, axis=1, keepdims=True)
        cen = v - mu
        var = jnp.mean(cen * cen, axis=1, keepdims=True)
        rstd = lax.rsqrt(var + LN_EPS)
        xhat = cen * rstd
        vn = (xhat * g_ref[...] + b_ref[...]).astype(BF16)
        dyv = dy_ref[...].astype(F32)
        dsv = dyv * u
        dsv_acc[...] += dsv
        dsvb = dsv.astype(BF16)
        tril = (lax.broadcasted_iota(jnp.int32, (CHUNK, CHUNK), 0)
                >= lax.broadcasted_iota(jnp.int32, (CHUNK, CHUNK), 1))
        du_parts = []
        dvn_parts = []
        for gi in range(G):
            cols = slice(gi * cg, (gi + 1) * cg)
            sv = jnp.dot(ws_ref[gi], vn[:, cols], preferred_element_type=F32) + bs_ref[:, gi:gi + 1]
            du_parts.append(dyv[:, cols] * sv)
            dvn_parts.append(jnp.dot(wst_ref[gi], dsvb[:, cols], preferred_element_type=F32))
            dw = lax.dot_general(dsvb[:, cols], vn[:, cols], (((1,), (1,)), ((), ())),
                                 preferred_element_type=F32)
            dws_ref[gi] += jnp.where(tril, dw, 0.0)
        du = jnp.concatenate(du_parts, axis=1)
        dvn = jnp.concatenate(dvn_parts, axis=1)
        dg_ref[...] += jnp.sum(dvn * xhat, axis=0, keepdims=True)
        db_ref[...] += jnp.sum(dvn, axis=0, keepdims=True)
        dxh = dvn * g_ref[...]
        m1 = jnp.mean(dxh, axis=1, keepdims=True)
        m2 = jnp.mean(dxh * xhat, axis=1, keepdims=True)
        dv = rstd * (dxh - m1 - xhat * m2)
        dz_ref[:, :E] = (du * _gelu_grad(zu)).astype(BF16)
        dz_ref[:, E:] = (dv * _gelu_grad(zv)).astype(BF16)

        @pl.when(step == nsteps - 1)
        def _():
            lane = lax.broadcasted_iota(jnp.int32, (CHUNK, LANES), 1)
            out = jnp.zeros((CHUNK, LANES), F32)
            for gi in range(G):
                s = jnp.sum(dsv_acc[:, gi * cg:(gi + 1) * cg], axis=1, keepdims=True)
                out = jnp.where(lane == gi, s, out)
            dbs_ref[...] = out

    vecE = pl.BlockSpec((1, E), lambda i: (0, 0))
    wspec = pl.BlockSpec((G, CHUNK, CHUNK), lambda i: (0, 0, 0))
    body, in_specs, args = _ordered(
        body, [pl.BlockSpec((rb, E2), lambda i: (i, 0)), pl.BlockSpec((rb, E), lambda i: (i, 0)),
               vecE, vecE, wspec, wspec, pl.BlockSpec((CHUNK, G), lambda i: (0, 0))],
        (zp, dy, ln_g.reshape(1, E), ln_b.reshape(1, E), ws, wst, bst), after)
    return pl.pallas_call(
        body, grid=(nsteps,), in_specs=in_specs,
        out_specs=[pl.BlockSpec((rb, E2), lambda i: (i, 0)), vecE, vecE, wspec,
                   pl.BlockSpec((CHUNK, LANES), lambda i: (0, 0))],
        out_shape=[jax.ShapeDtypeStruct((T, E2), BF16), jax.ShapeDtypeStruct((1, E), F32),
                   jax.ShapeDtypeStruct((1, E), F32), jax.ShapeDtypeStruct((G, CHUNK, CHUNK), F32),
                   jax.ShapeDtypeStruct((CHUNK, LANES), F32)],
        scratch_shapes=[pltpu.VMEM((CHUNK, E), F32)],
        compiler_params=_params(("arbitrary",)), name=name,
    )(*args)


def _shift_down(x, k, row):
    return jnp.where(row >= k, pltpu.roll(x, k, 0), 0.0)


def _shift_up(x, k, row, T):
    return jnp.where(row < T - k, pltpu.roll(x, T - k, 0), 0.0)


def _conv3(x, w_ref, b_ref, row):
    return (w_ref[0:1, :] * _shift_down(x, 2, row) + w_ref[1:2, :] * _shift_down(x, 1, row)
            + w_ref[2:3, :] * x + b_ref[...])


def _convgate_fwd(hh, cw, cb, *, name):
    T, F2 = hh.shape
    F = F2 // 2
    ns = F // LANES

    def body(a_ref, g_ref, wa_ref, wg_ref, ba_ref, bg_ref, o_ref, ca_ref, cg_ref):
        row = lax.broadcasted_iota(jnp.int32, (T, LANES), 0)
        ca = _conv3(a_ref[...].astype(F32), wa_ref, ba_ref, row)
        cgv = _conv3(g_ref[...].astype(F32), wg_ref, bg_ref, row)
        o_ref[...] = (_gelu(ca) * cgv).astype(BF16)
        ca_ref[...] = ca.astype(ACT)
        cg_ref[...] = cgv.astype(ACT)

    sa = lambda r: pl.BlockSpec((r, LANES), lambda j: (0, j))
    sg = lambda r: pl.BlockSpec((r, LANES), lambda j: (0, j + ns))
    return pl.pallas_call(
        body, grid=(ns,), in_specs=[sa(T), sg(T), sa(3), sg(3), sa(1), sg(1)],
        out_specs=[sa(T)] * 3,
        out_shape=[jax.ShapeDtypeStruct((T, F), BF16), jax.ShapeDtypeStruct((T, F), ACT),
                   jax.ShapeDtypeStruct((T, F), ACT)],
        compiler_params=_params(("parallel",)), name=name,
    )(hh, hh, cw, cw, cb, cb)


def _convgate_bwd(hh, hca, hcg, dact, cw, *, name):
    T, F2 = hh.shape
    F = F2 // 2
    ns = F // LANES

    def body(a_ref, g_ref, ca_ref, cg_ref, d_ref, wa_ref, wg_ref,
             da_ref, dg_ref, dwa_ref, dwg_ref, dba_ref, dbg_ref):
        row = lax.broadcasted_iota(jnp.int32, (T, LANES), 0)
        d = d_ref[...].astype(F32)
        ca = ca_ref[...].astype(F32)
        cgv = cg_ref[...].astype(F32)
        cdf = 0.5 * (1.0 + lax.erf(ca * INV_SQRT2))
        dca = d * cgv * (cdf + ca * jnp.exp(-0.5 * ca * ca) * INV_SQRT_2PI)
        dcg = d * (ca * cdf)
        for x_ref, w_ref, dc, dx_ref, dw_ref, db_ref in (
                (a_ref, wa_ref, dca, da_ref, dwa_ref, dba_ref),
                (g_ref, wg_ref, dcg, dg_ref, dwg_ref, dbg_ref)):
            x = x_ref[...].astype(F32)
            up1, up2 = _shift_up(dc, 1, row, T), _shift_up(dc, 2, row, T)
            dx_ref[...] = (w_ref[2:3, :] * dc + w_ref[1:2, :] * up1 + w_ref[0:1, :] * up2).astype(BF16)
            dw_ref[0:1, :] = jnp.sum(up2 * x, axis=0, keepdims=True)
            dw_ref[1:2, :] = jnp.sum(up1 * x, axis=0, keepdims=True)
            dw_ref[2:3, :] = jnp.sum(dc * x, axis=0, keepdims=True)
            db_ref[...] = jnp.sum(dc, axis=0, keepdims=True)

    sa = lambda r: pl.BlockSpec((r, LANES), lambda j: (0, j))
    sg = lambda r: pl.BlockSpec((r, LANES), lambda j: (0, j + ns))
    return pl.pallas_call(
        body, grid=(ns,), in_specs=[sa(T), sg(T), sa(T), sa(T), sa(T), sa(3), sg(3)],
        out_specs=[sa(T), sa(T), sa(3), sa(3), sa(1), sa(1)],
        out_shape=[jax.ShapeDtypeStruct((T, F), BF16), jax.ShapeDtypeStruct((T, F), BF16),
                   jax.ShapeDtypeStruct((3, F), F32), jax.ShapeDtypeStruct((3, F), F32),
                   jax.ShapeDtypeStruct((1, F), F32), jax.ShapeDtypeStruct((1, F), F32)],
        compiler_params=_params(("parallel",)), name=name,
    )(hh, hh, hca, hcg, dact, cw, cw)


def _bucket_maps():
    iq = np.arange(BLK)[:, None]
    ik = np.arange(2 * BLK)[None, :]
    delta = iq + BLK - ik
    maps = []
    for win, dil in DILATED_GROUPS:
        n = np.clip(delta, 0, None) * dil
        max_exact = REL_BUCKETS // 2
        nf = np.maximum(n, 1).astype(np.float32)
        large = max_exact + (np.log(nf / np.float32(max_exact)) / np.float32(math.log(REL_MAX_DIST / max_exact))
                             * np.float32(REL_BUCKETS - max_exact)).astype(np.int32)
        large = np.minimum(large, REL_BUCKETS - 1)
        bucket = np.where(n < max_exact, n, large)
        valid = (delta >= 0) & (delta <= win // dil)
        maps.append(np.where(valid, bucket, -1).astype(np.int32))
    return np.stack(maps)


def _band_bias(rel_table, bmap, H, *, name):
    def body(t_ref, m_ref, o_ref):
        g = pl.program_id(0)
        bm = m_ref[0]
        for h in range(H):
            acc = jnp.full((BLK, 2 * BLK), NEG, F32)
            for b in range(REL_BUCKETS):
                acc = jnp.where(bm == b, t_ref[b, g * H + h], acc)
            o_ref[0, h] = acc

    return pl.pallas_call(
        body, grid=(N_GROUPS,),
        in_specs=[pl.BlockSpec(memory_space=pltpu.SMEM),
                  pl.BlockSpec((1, BLK, 2 * BLK), lambda g: (g, 0, 0))],
        out_specs=pl.BlockSpec((1, H, BLK, 2 * BLK), lambda g: (g, 0, 0, 0)),
        out_shape=jax.ShapeDtypeStruct((N_GROUPS, H, BLK, 2 * BLK), F32),
        compiler_params=_params(("parallel",)), name=name,
    )(rel_table, bmap)


def _band_bias_bwd(dbias, bmap, H, *, name):
    def body(d_ref, m_ref, o_ref):
        bm = m_ref[0]
        rowi = lax.broadcasted_iota(jnp.int32, (REL_BUCKETS, LANES), 0)
        lane = lax.broadcasted_iota(jnp.int32, (REL_BUCKETS, LANES), 1)
        out = jnp.zeros((REL_BUCKETS, LANES), F32)
        for h in range(H):
            dv = d_ref[0, h]
            for b in range(REL_BUCKETS):
                s = jnp.sum(jnp.sum(jnp.where(bm == b, dv, 0.0), axis=1, keepdims=True),
                            axis=0, keepdims=True)
                out = jnp.where((rowi == b) & (lane == h), s, out)
        o_ref[0] = out

    return pl.pallas_call(
        body, grid=(N_GROUPS,),
        in_specs=[pl.BlockSpec((1, H, BLK, 2 * BLK), lambda g: (g, 0, 0, 0)),
                  pl.BlockSpec((1, BLK, 2 * BLK), lambda g: (g, 0, 0))],
        out_specs=pl.BlockSpec((1, REL_BUCKETS, LANES), lambda g: (g, 0, 0)),
        out_shape=jax.ShapeDtypeStruct((N_GROUPS, REL_BUCKETS, LANES), F32),
        compiler_params=_params(("parallel",)), name=name,
    )(dbias, bmap)


def _head_masks():
    lane = lax.broadcasted_iota(jnp.int32, (BLK, LANES), 1)
    return (lane < HEAD_DIM, lane >= HEAD_DIM)


SUPER =DILATED_GROUPS[-1][1] * BLK


def _band_rows(it, d):
    r, j = it % d, it // d
    if d == 1:
        at = lambda blk: pl.ds(pl.multiple_of(blk * BLK, BLK), BLK)
    else:
        at = lambda blk: pl.ds(r + d * BLK * blk, BLK, stride=d)
    return at(j), at(jnp.maximum(j - 1, 0))


def _stack_heads(x, hm):
    zero = jnp.zeros_like(x)
    return jnp.concatenate([jnp.where(hm[0], x, zero), jnp.where(hm[1], x, zero)], axis=0)


def _band_loops(step, d, unroll):
    n_it = SUPER // BLK

    def run(lo, hi, inside):
        if hi > lo:
            def body(it, carry):
                step(it, inside)
                return carry
            lax.fori_loop(lo, hi, body, 0, unroll=max(u for u in range(1, unroll + 1) if (hi - lo) % u == 0))

    run(0, d, False)
    run(d, n_it, True)


def _last_rows(it, d):
    m = SUPER // (d * BLK)
    if d == 1:
        return pl.ds((m - 1) * BLK, BLK)
    return pl.ds(it % d + d * BLK * (m - 1), BLK, stride=d)


def _attn_fwd_all(q, kv, bias, *, name):
    T = q.shape[0]
    HD = kv.shape[1] // 2
    PP = HD // LANES
    NS = T // SUPER

    def body(q0, q1, q2, kp_ref, kc_ref, vp_ref, vc_ref, b_ref, o_ref, ob_ref, l_ref, og, lg):
        n = pl.program_id(1)
        col = lax.broadcasted_iota(jnp.int32, (2 * BLK, 2 * BLK), 1)
        hm = _head_masks()
        for g, (q_ref, (_, d)) in enumerate(zip((q0, q1, q2), DILATED_GROUPS)):
            def step(it, inside, g=g, q_ref=q_ref, d=d):
                cur, prv = _band_rows(it, d)
                qp = q_ref[cur, :].astype(BF16)
                if inside:
                    kprev, vprev = kc_ref[prv, :], vc_ref[prv, :]
                else:
                    last = _last_rows(it, d)
                    kprev, vprev = kp_ref[last, :], vp_ref[last, :]
                kc = jnp.concatenate([kprev.astype(BF16), kc_ref[cur, :].astype(BF16)], axis=0)
                vc = jnp.concatenate([vprev.astype(BF16), vc_ref[cur, :].astype(BF16)], axis=0)
                s = lax.dot_general(_stack_heads(qp, hm), kc, (((1,), (1,)), ((), ())),
                                    preferred_element_type=F32)
                s = s + b_ref[g].reshape(2 * BLK, 2 * BLK)
                if not inside:
                    s = jnp.where((n == 0) & (col < BLK), NEG, s)
                mx = jnp.max(s, axis=1, keepdims=True)
                e = jnp.exp(s - mx)
                den = jnp.sum(e, axis=1, keepdims=True)
                out = jnp.dot((e / den).astype(BF16), vc, preferred_element_type=F32)
                lse = mx + jnp.log(den)
                og.at[g][cur, :] = jnp.where(hm[0], out[:BLK], out[BLK:])
                lg.at[g][cur, :] = jnp.where(hm[0], lse[:BLK], lse[BLK:])

            _band_loops(step, d, 8)
        la, lb, lc = lg[0], lg[1], lg[2]
        mx = jnp.maximum(jnp.maximum(la, lb), lc)
        L = mx + jnp.log(jnp.exp(la - mx) + jnp.exp(lb - mx) + jnp.exp(lc - mx))
        o = jnp.exp(la - L) * og[0] + jnp.exp(lb - L) * og[1] + jnp.exp(lc - L) * og[2]
        o_ref[...] = o
        ob_ref[...] = o.astype(BF16)
        l_ref[...] = L

    blk = lambda f: pl.BlockSpec((SUPER, LANES), f)
    prev = lambda n: jnp.maximum(n - 1, 0)
    qspec = lambda g: blk(lambda p, n: (n, g * PP + p))
    return pl.pallas_call(
        body, grid=(PP, NS),
        in_specs=[qspec(0), qspec(1), qspec(2),
                  blk(lambda p, n: (prev(n), p)), blk(lambda p, n: (n, p)),
                  blk(lambda p, n: (prev(n), PP + p)), blk(lambda p, n: (n, PP + p)),
                  pl.BlockSpec((N_GROUPS, 2, BLK, 2 * BLK), lambda p, n: (0, p, 0, 0))],
        out_specs=[blk(lambda p, n: (n, p))] * 3,
        out_shape=[jax.ShapeDtypeStruct((T, HD), F32), jax.ShapeDtypeStruct((T, HD), BF16),
                   jax.ShapeDtypeStruct((T, HD), F32)],
        scratch_shapes=[pltpu.VMEM((N_GROUPS, SUPER, LANES), F32), pltpu.VMEM((N_GROUPS, SUPER, LANES), F32)],
        compiler_params=_params(("parallel", "parallel")), name=name,
    )(q, q, q, kv, kv, kv, kv, bias)


def _attn_bwd_all(q, kv, bias, do, o, L, *, after=None, name):
    T = q.shape[0]
    HD = kv.shape[1] // 2
    PP = HD // LANES
    H = HD // HEAD_DIM
    NS = T // SUPER

    def body(q0, q1, q2, kp_ref, kc_ref, vp_ref, vc_ref, b_ref, do_ref, o_ref, L_ref,
             dq_ref, dk_ref, dv_ref, db_ref, ck_ref, cv_ref):
        n = pl.program_id(1)

        @pl.when(n == 0)
        def _():
            db_ref[...] = jnp.zeros_like(db_ref)
            ck_ref[...] = jnp.zeros_like(ck_ref)
            cv_ref[...] = jnp.zeros_like(cv_ref)

        dk_ref[...] = ck_ref[...]
        dv_ref[...] = cv_ref[...]
        ck_ref[...] = jnp.zeros_like(ck_ref)
        cv_ref[...] = jnp.zeros_like(cv_ref)

        @pl.when(n < NS)
        def _():
            col = lax.broadcasted_iota(jnp.int32, (2 * BLK, 2 * BLK), 1)
            hm = _head_masks()
            for g, (q_ref, (_, d)) in enumerate(zip((q0, q1, q2), DILATED_GROUPS)):
                def step(it, inside, g=g, q_ref=q_ref, d=d):
                    cur, prv = _band_rows(it, d)
                    last = _last_rows(it, d)
                    qp = q_ref[cur, :].astype(BF16)
                    if inside:
                        kprev, vprev = kc_ref[prv, :], vc_ref[prv, :]
                    else:
                        kprev, vprev = kp_ref[last, :], vp_ref[last, :]
                    kc = jnp.concatenate([kprev.astype(BF16), kc_ref[cur, :].astype(BF16)], axis=0)
                    vc = jnp.concatenate([vprev.astype(BF16), vc_ref[cur, :].astype(BF16)], axis=0)
                    dop = do_ref[cur, :]
                    prod = dop * o_ref[cur, :]
                    Lp = L_ref[cur, :]
                    qs = _stack_heads(qp, hm)
                    dos = _stack_heads(dop.astype(BF16), hm)
                    lse = jnp.concatenate([Lp[:, 0:1], Lp[:, HEAD_DIM:HEAD_DIM + 1]], axis=0)
                    delta = jnp.concatenate([jnp.sum(jnp.where(hm[0], prod, 0.0), axis=1, keepdims=True),
                                             jnp.sum(jnp.where(hm[1], prod, 0.0), axis=1, keepdims=True)], axis=0)
                    s = lax.dot_general(qs, kc, (((1,), (1,)), ((), ())), preferred_element_type=F32)
                    s = s + b_ref[g].reshape(2 * BLK, 2 * BLK)
                    if not inside:
                        s = jnp.where((n == 0) & (col < BLK), NEG, s)
                    pr = jnp.exp(s - lse)
                    dp = lax.dot_general(dos, vc, (((1,), (1,)), ((), ())), preferred_element_type=F32)
                    ds = pr * (dp - delta)
                    db_ref[g] += ds.reshape(2, BLK, 2 * BLK)
                    dsb = ds.astype(BF16)
                    dqs = jnp.dot(dsb, kc, preferred_element_type=F32)
                    dkc = lax.dot_general(dsb, qs, (((0,), (0,)), ((), ())), preferred_element_type=F32)
                    dvc = lax.dot_general(pr.astype(BF16), dos, (((0,), (0,)), ((), ())),
                                          preferred_element_type=F32)
                    dq_ref.at[g][cur, :] = jnp.where(hm[0], dqs[:BLK], dqs[BLK:]) * (HEAD_DIM ** -0.5)
                    ck_ref[cur, :] += dkc[BLK:]
                    cv_ref[cur, :] += dvc[BLK:]
                    if inside:
                        ck_ref[prv, :] += dkc[:BLK]
                        cv_ref[prv, :] += dvc[:BLK]
                    else:
                        dk_ref[last, :] += dkc[:BLK]
                        dv_ref[last, :] += dvc[:BLK]

                _band_loops(step, d, 4)

    blk = lambda f: pl.BlockSpec((SUPER, LANES), f)
    cur = lambda n: jnp.minimum(n, NS - 1)
    prev = lambda n: jnp.maximum(jnp.minimum(n, NS - 1) - 1, 0)
    lag = lambda n: jnp.maximum(n - 1, 0)
    qspec = lambda g: blk(lambda p, n: (cur(n), g * PP + p))
    bspec = pl.BlockSpec((N_GROUPS, 2, BLK, 2 * BLK), lambda p, n: (0, p, 0, 0))
    body, in_specs, args = _ordered(
        body, [qspec(0), qspec(1), qspec(2),
               blk(lambda p, n: (prev(n), p)), blk(lambda p, n: (cur(n), p)),
               blk(lambda p, n: (prev(n), PP + p)), blk(lambda p, n: (cur(n), PP + p)),
               bspec, blk(lambda p, n: (cur(n), p)), blk(lambda p, n: (cur(n), p)),
               blk(lambda p, n: (cur(n), p))],
        (q, q, q, kv, kv, kv, kv, bias, do, o, L), after)
    return pl.pallas_call(
        body, grid=(PP, NS + 1), in_specs=in_specs,
        out_specs=[pl.BlockSpec((N_GROUPS, SUPER, LANES), lambda p, n: (0, cur(n), p)),
                   blk(lambda p, n: (lag(n), p)), blk(lambda p, n: (lag(n), p)), bspec],
        out_shape=[jax.ShapeDtypeStruct((N_GROUPS, T, HD), F32), jax.ShapeDtypeStruct((T, HD), F32),
                   jax.ShapeDtypeStruct((T, HD), F32),
                   jax.ShapeDtypeStruct((N_GROUPS, H, BLK, 2 * BLK), F32)],
        scratch_shapes=[pltpu.VMEM((SUPER, LANES), F32), pltpu.VMEM((SUPER, LANES), F32)],
        compiler_params=_params(("arbitrary", "arbitrary")), name=name,
    )(*args)


class _Weights(dict):
    def __init__(self, base, fetch=None, emit=None, emit_small=None):
        super().__init__(base)
        self._fetch, self._emit, self._emit_small = fetch, emit, emit_small

    def fetch(self, group, after):
        if self._fetch is not None:
            for (key, layer), mat in self._fetch(group, after).items():
                self[key][layer] = mat

    def emit(self, group, mats):
        return None if self._emit is None else self._emit(group, mats)

    def emit_small(self, grads):
        return None if self._emit_small is None else self._emit_small(grads)


def _local_step(x, tgt, W):
    T, D = x.shape
    H = W["rel_table"].shape[1] // N_GROUPS
    HD = H * HEAD_DIM
    G = W["a_w_s"].shape[1]
    assert T % (DILATED_GROUPS[-1][1] * BLK) == 0

    tril = jnp.tril(jnp.ones((CHUNK, CHUNK), F32))
    bmap = jnp.asarray(_bucket_maps())
    bias = _band_bias(W["rel_table"], bmap, H, name="band_bias")

    saved = []
    xc, xcb = x, x.astype(BF16)
    kvb = None
    for i in range(DEPTH):
        s = {"x": xc, "xb": xcb}
        W.fetch(4 * i, xc)
        if i < N_A:
            ws_m = W["a_w_s"][i] * tril
            s["ws"] = ws_m.astype(BF16)
            s["wst"] = jnp.swapaxes(ws_m, 1, 2).astype(BF16)
            s["bst"] = W["a_b_s"][i].T
            s["zp"] = _mm(xcb, W["a_w_in"][i], out_dtype=ACT, name=f"a_in_{i}")
            s["y"] = _sgu_fwd(s["zp"], W["a_ln_g"][i], W["a_ln_b"][i], s["ws"], s["bst"], name=f"sgu_fwd_{i}")
            W.fetch(4 * i + 1, s["zp"])
            s["h"] = _mm(s["y"], W["a_w_out"][i], out_dtype=ACT, name=f"a_out_{i}")
        else:
            j = i - N_A
            if kvb is None:
                kvb = _mm(xcb, W["kv_w"][0], name="kv_proj")
            s["q"] = _mm(xcb, W["b_w_q_t"][j], tb=True, scale=HEAD_DIM ** -0.5, name=f"q_proj_{j}")
            s["o"], s["ob"], s["L"] = _attn_fwd_all(s["q"], kvb, bias, name=f"attn_fwd_{j}")
            W.fetch(4 * i + 1, s["q"])
            s["h"] = _mm(s["ob"], W["b_w_o"][j], out_dtype=ACT, name=f"o_proj_{j}")
        s["x1b"], = _add_ln_fwd(xc, s["h"], W["ln_g"][i, 0], W["ln_b"][i, 0], wide=False, name=f"ln1_fwd_{i}")
        s["x1"] = s["x1b"]
        W.fetch(4 * i + 2, s["x1"])
        s["hh"] = _mm(s["x1b"], W["ffn_w_up_t"][i], tb=True, out_dtype=ACT, name=f"ffn_up_{i}")
        s["cw"] = W["ffn_conv_w"][i]
        s["cb"] = W["ffn_conv_b"][i].reshape(1, -1)
        s["act"], s["hca"], s["hcg"] = _convgate_fwd(s["hh"], s["cw"], s["cb"], name=f"convgate_fwd_{i}")
        W.fetch(4 * i + 3, s["hh"])
        s["f"] = _mm(s["act"], W["ffn_w_down"][i], out_dtype=ACT, name=f"ffn_down_{i}")
        outs = _add_ln_fwd(s["x1"], s["f"], W["ln_g"][i, 1], W["ln_b"][i, 1], wide=i == DEPTH - 1,
                           name=f"ln2_fwd_{i}")
        xc, xcb = outs[0], outs[-1]
        saved.append(s)

    dy, lossv = _loss_grad(xc, tgt, name="loss_grad")
    loss = lossv[0, 0]

    gl = {k: [None] * DEPTH for k in ("ffn_w_up_t", "ffn_conv_w", "ffn_conv_b", "ffn_w_down", "ln_g", "ln_b")}
    ga = {k: [None] * N_A for k in ("a_w_in", "a_ln_g", "a_ln_b", "a_w_s", "a_b_s", "a_w_out")}
    gb = {k: [None] * (DEPTH - N_A) for k in ("b_w_q_t", "b_w_o")}
    mats = ("a_w_in", "a_w_out", "b_w_q_t", "b_w_o", "ffn_w_up_t", "ffn_w_down")
    dks, dvs, dbias = [], [], []
    grads = {}
    terms = [(1.0, dy)]
    tok = None
    small_keys = ("ffn_conv_w", "ffn_conv_b", "ln_g", "ln_b", "a_ln_g", "a_ln_b", "a_w_s", "a_b_s")
    for i in reversed(range(DEPTH)):
        s = saved[i]
        dp2b, dg2, db2 = _add_ln_bwd(s["x1"], s["f"], W["ln_g"][i, 1], terms, after=tok, name=f"ln2_bwd_{i}")
        dact = _mm(dp2b, W["ffn_w_down"][i], tb=True, out_dtype=ACT, name=f"ffn_down_dx_{i}")
        gl["ffn_w_down"][i] = _mm(s["act"], dp2b, ta=True, out_dtype=BF16, name=f"ffn_down_dw_{i}")
        dha, dhg, dwa, dwg, dba, dbg = _convgate_bwd(s["hh"], s["hca"], s["hcg"], dact, s["cw"],
                                                     name=f"convgate_bwd_{i}")
        dhh = (dha, dhg)
        gl["ffn_conv_w"][i] = jnp.concatenate([dwa, dwg], axis=1)
        gl["ffn_conv_b"][i] = jnp.concatenate([dba, dbg], axis=1)[0]
        dx1 = _mm(dhh, W["ffn_w_up_t"][i], out_dtype=ACT, name=f"ffn_up_dx_{i}")
        gl["ffn_w_up_t"][i] = _mm(dhh, s["x1b"], ta=True, out_dtype=BF16, name=f"ffn_up_dw_{i}")
        tok = W.emit(3 * i + 2, {("ffn_w_up_t", i): gl["ffn_w_up_t"][i], ("ffn_w_down", i): gl["ffn_w_down"][i]})
        dp1b, dg1, db1 = _add_ln_bwd(s["x"], s["h"], W["ln_g"][i, 0], [(ALPHA, dp2b), (1.0, dx1)],
                                     after=tok, name=f"ln1_bwd_{i}")
        gl["ln_g"][i] = jnp.concatenate([dg1, dg2], axis=0)
        gl["ln_b"][i] = jnp.concatenate([db1, db2], axis=0)
        terms = [(ALPHA, dp1b)]
        if i < N_A:
            dyy = _mm(dp1b, W["a_w_out"][i], tb=True, out_dtype=ACT, name=f"a_out_dx_{i}")
            ga["a_w_out"][i] = _mm(s["y"], dp1b, ta=True, out_dtype=BF16, name=f"a_out_dw_{i}")
            if i == 0:
                tok = W.emit(3 * i + 1, {("a_w_out", i): ga["a_w_out"][i]})
            dzp, dlg, dlb, dws, dbs = _sgu_bwd(s["zp"], dyy, W["a_ln_g"][i], W["a_ln_b"][i], s["ws"],
                                               s["wst"], s["bst"], after=tok, name=f"sgu_bwd_{i}")
            ga["a_ln_g"][i], ga["a_ln_b"][i], ga["a_w_s"][i] = dlg[0], dlb[0], dws
            ga["a_b_s"][i] = dbs[:, :G].T
            if i == 0:
                for dct in (gl, ga):
                    grads.update({k: jnp.stack(v) for k, v in dct.items() if k in small_keys})
                tok = W.emit_small(grads)
            ga["a_w_in"][i] = _mm(s["xb"], dzp, ta=True, out_dtype=BF16, after=tok, name=f"a_in_dw_{i}")
            out_a = {("a_w_in", i): ga["a_w_in"][i]}
            if i > 0:
                out_a[("a_w_out", i)] = ga["a_w_out"][i]
            tok = W.emit(3 * i, out_a)
            terms.append((1.0, _mm(dzp, W["a_w_in"][i], tb=True, out_dtype=ACT, after=tok, name=f"a_in_dx_{i}")))
        else:
            j = i - N_A
            do = _mm(dp1b, W["b_w_o"][j], tb=True, name=f"o_proj_dx_{j}")
            gb["b_w_o"][j] = _mm(s["ob"], dp1b, ta=True, out_dtype=BF16, name=f"o_proj_dw_{j}")
            dq, dk_j, dv_j, db_j = _attn_bwd_all(s["q"], kvb, bias, do, s["o"], s["L"], after=tok,
                                                 name=f"attn_bwd_{j}")
            dks.append((1.0, dk_j))
            dvs.append((1.0, dv_j))
            dbias.append(db_j)
            terms.append((1.0, _mm(dq, W["b_w_q_t"][j], out_dtype=ACT, name=f"q_proj_dx_{j}")))
            gb["b_w_q_t"][j] = _mm(dq, s["xb"], ta=True, out_dtype=BF16, name=f"q_proj_dw_{j}")
            out_b = {("b_w_q_t", j): gb["b_w_q_t"][j], ("b_w_o", j): gb["b_w_o"][j]}
            if i == N_A:
                dkv = jnp.concatenate([_lincomb(dks, BF16, name="dk_sum"), _lincomb(dvs, BF16, name="dv_sum")],
                                      axis=1)
                terms.append((1.0, _mm(dkv, W["kv_w"][0], tb=True, out_dtype=ACT, name="kv_proj_dx")))
                grads["kv_w"] = [_mm(s["xb"], dkv, ta=True, out_dtype=BF16, name="kv_proj_dw")]
                out_b[("kv_w", 0)] = grads["kv_w"][0]
                dbt = _lincomb([(1.0, a.reshape(-1, 2 * BLK)) for a in dbias], F32, name="dbias_sum")
                dtab = _band_bias_bwd(dbt.reshape(N_GROUPS, H, BLK, 2 * BLK), bmap, H, name="band_bias_bwd")
                grads["rel_table"] = jnp.transpose(dtab[:, :, :H], (1, 0, 2)).reshape(REL_BUCKETS, N_GROUPS * H)
            tok = W.emit(3 * i, out_b)
    grad_x = _lincomb(terms, F32, name="grad_x")
    for dct in (gl, ga, gb):
        grads.update({k: v for k, v in dct.items() if k in mats})
    return loss, grad_x, grads


def _my_index():
    return 4 * lax.axis_index("x") + 2 * lax.axis_index("y") + lax.axis_index("c")


HBM_SPEC = pl.BlockSpec(memory_space=pltpu.HBM)


def _block(ref, k, n, axis):
    off = pl.multiple_of(k * n, n)
    return ref.at[pl.ds(off, n), :] if axis == 0 else ref.at[:, pl.ds(off, n)]


def _gather_now(local, axis, *, name):
    a, b = local.shape
    n = a if axis == 0 else b
    full = (a * N_DEV, b) if axis == 0 else (a, b * N_DEV)

    def body(x_ref, out_ref, send_sems, recv_sems, local_sem):
        x, y, c = lax.axis_index("x"), lax.axis_index("y"), lax.axis_index("c")
        me, sibling = (x, y, c), (x, y, 1 - c)
        chips = [(1 - x, y), (x, 1 - y), (1 - x, 1 - y)]

        def slot(px, py, pc):
            return _block(out_ref, 4 * px + 2 * py + pc, n, axis)

        def copy(k, blk, to, src=None):
            return pltpu.make_async_remote_copy(
                src_ref=slot(*blk) if src is None else src, dst_ref=slot(*blk),
                send_sem=send_sems.at[k], recv_sem=recv_sems.at[k], device_id=to, device_id_type=MESH)

        mine = pltpu.make_async_copy(x_ref, slot(*me), local_sem)
        mine.start()
        first = [copy(0, me, sibling, src=x_ref)]
        first += [copy(1 + j, me, (*chip, c), src=x_ref) for j, chip in enumerate(chips)]
        for cp in first:
            cp.start()
        passed = [copy(4 + j, (*chip, c), sibling) for j, chip in enumerate(chips)]
        for j, chip in enumerate(chips):
            copy(1 + j, (*chip, c), me).wait_recv()
            passed[j].start()
        copy(0, sibling, me).wait_recv()
        for j, chip in enumerate(chips):
            copy(4 + j, (*chip, 1 - c), me).wait_recv()
        for cp in first + passed:
            cp.wait_send()
        mine.wait()

    return pl.pallas_call(
        body, out_shape=jax.ShapeDtypeStruct(full, local.dtype),
        in_specs=[HBM_SPEC], out_specs=HBM_SPEC,
        scratch_shapes=[pltpu.SemaphoreType.DMA((7,)), pltpu.SemaphoreType.DMA((7,)),
                        pltpu.SemaphoreType.DMA],
        name=name,
    )(local)


SEM_SPEC = pl.BlockSpec(memory_space=pltpu.SEMAPHORE)
FLOWING = pltpu.SideEffectType.DATAFLOW_SIDE_EFFECTING


def _peers(x, y, c):
    return [(1 - x if k & 4 else x, 1 - y if k & 2 else y, 1 - c if k & 1 else c) for k in range(1, N_DEV)]


def _ends(src_ref, land_ref, peer_index, me, n, axis, gather):
    if gather:
        return src_ref, _block(land_ref, me, n, axis)
    return _block(src_ref, peer_index, n, axis), land_ref.at[me]


def _send_start(groups, gather, *, after=None, name):
    flat = [(g, j, mat, axis) for g, items in enumerate(groups) for j, (mat, axis) in enumerate(items)]
    M, G = len(flat), len(groups)
    lands, ns = [], []
    for _, _, mat, axis in flat:
        A, B = mat.shape
        if gather:
            lands.append((A * N_DEV, B) if axis == 0 else (A, B * N_DEV))
            ns.append(A if axis == 0 else B)
        else:
            lands.append((N_DEV, A // N_DEV, B) if axis == 0 else (N_DEV, A, B // N_DEV))
            ns.append(A // N_DEV if axis == 0 else B // N_DEV)

    n_in = 2 * M + (after is not None)

    def body(*refs):
        src_refs, land_refs, sems = refs[:M], refs[M:2 * M], refs[n_in:n_in + 3 * G]
        token = refs[-1]
        x, y, c = lax.axis_index("x"), lax.axis_index("y"), lax.axis_index("c")
        me = 4 * x + 2 * y + c
        for i, (g, j, _, axis) in enumerate(flat):
            for k, (px, py, pc) in enumerate(_peers(x, y, c)):
                s, d = _ends(src_refs[i], land_refs[i], 4 * px + 2 * py + pc, me, ns[i], axis, gather)
                pltpu.make_async_remote_copy(
                    src_ref=s, dst_ref=d, send_sem=sems[3 * g].at[7 * j + k], recv_sem=sems[3 * g + 1].at[7 * j + k],
                    device_id=(px, py, pc), device_id_type=MESH).start()
            s, d = _ends(src_refs[i], land_refs[i], me, me, ns[i], axis, gather)
            pltpu.make_async_copy(s, d, sems[3 * g + 2].at[j]).start()
        token[...] = jnp.zeros_like(token)

    sem_shapes = []
    for items in groups:
        sem_shapes += [pltpu.SemaphoreType.DMA((7 * len(items),))] * 2 + [pltpu.SemaphoreType.DMA((len(items),))]
    outs = pl.pallas_call(
        body, name=name,
        out_shape=(*sem_shapes, *[pltpu.HBM(m.shape, m.dtype) for _, _, m, _ in flat],
                   *[pltpu.HBM(shp, m.dtype) for shp, (_, _, m, _) in zip(lands, flat)],
                   jax.ShapeDtypeStruct((8, LANES), F32)),
        in_specs=[HBM_SPEC] * (2 * M) + [pl.BlockSpec(memory_space=pl.ANY)] * (n_in - 2 * M),
        out_specs=(*[SEM_SPEC] * (3 * G), *[HBM_SPEC] * (2 * M), pl.BlockSpec(memory_space=pltpu.VMEM)),
        input_output_aliases={i: 3 * G + i for i in range(2 * M)},
        compiler_params=pltpu.CompilerParams(has_side_effects=FLOWING),
    )(*[pltpu.with_memory_space_constraint(m, pltpu.HBM) for _, _, m, _ in flat],
      *[pltpu.with_memory_space_constraint(lax.empty(shp, m.dtype), pltpu.HBM)
        for shp, (_, _, m, _) in zip(lands, flat)], *([] if after is None else [after]))
    handles = []
    for g in range(G):
        idx = [i for i, f in enumerate(flat) if f[0] == g]
        handles.append((outs[3 * g], outs[3 * g + 1], outs[3 * g + 2], [outs[3 * G + i] for i in idx],
                        [outs[3 * G + M + i] for i in idx], [flat[i][3] for i in idx]))
    return handles, outs[-1]


def _send_wait(handle, gather, after, *, name):
    send_sems, recv_sems, local_sems, mats, lands, axes = handle
    n_m = len(mats)
    ns = []
    for mat, land, axis in zip(mats, lands, axes):
        ns.append(mat.shape[axis] if gather else land.shape[1 + axis])

    def body(*refs):
        src_refs, land_refs = refs[:n_m], refs[n_m:2 * n_m]
        ssem, rsem, lsem = refs[2 * n_m:2 * n_m + 3]
        x, y, c = lax.axis_index("x"), lax.axis_index("y"), lax.axis_index("c")
        me = 4 * x + 2 * y + c
        for j in range(n_m):
            for k, (px, py, pc) in enumerate(_peers(x, y, c)):
                s, d = _ends(src_refs[j], land_refs[j], 4 * px + 2 * py + pc, me, ns[j], axes[j], gather)
                cp = pltpu.make_async_remote_copy(
                    src_ref=s, dst_ref=d, send_sem=ssem.at[7 * j + k], recv_sem=rsem.at[7 * j + k],
                    device_id=(px, py, pc), device_id_type=MESH)
                cp.wait_send()
                cp.wait_recv()
            s, d = _ends(src_refs[j], land_refs[j], me, me, ns[j], axes[j], gather)
            pltpu.make_async_copy(s, d, lsem.at[j]).wait()

    outs = pl.pallas_call(
        body, name=name,
        out_shape=(*[pltpu.HBM(m.shape, m.dtype) for m in mats], *[pltpu.HBM(l.shape, l.dtype) for l in lands]),
        in_specs=[HBM_SPEC] * (2 * n_m) + [SEM_SPEC] * 3 + [pl.BlockSpec(memory_space=pl.ANY)],
        out_specs=tuple([HBM_SPEC] * (2 * n_m)),
        input_output_aliases={i: i for i in range(2 * n_m)},
        compiler_params=pltpu.CompilerParams(has_side_effects=FLOWING),
    )(*mats, *lands, send_sems, recv_sems, local_sems, after)
    return list(outs[n_m:])


def _sum_parts(parts, *, name):
    n, R, C = parts.shape
    rb = _pick(R, 512) if R % LANES == 0 else R

    def body(p_ref, o_ref):
        acc = p_ref[0].astype(F32)
        for k in range(1, n):
            acc = acc + p_ref[k].astype(F32)
        o_ref[...] = acc

    return pl.pallas_call(
        body, grid=(R // rb,), in_specs=[pl.BlockSpec((n, rb, C), lambda i: (0, i, 0))],
        out_specs=pl.BlockSpec((rb, C), lambda i: (i, 0)),
        out_shape=jax.ShapeDtypeStruct((R, C), F32),
        compiler_params=_params(("parallel",)), name=name,
    )(parts)


def _adamw(w, m, v, parts, *, name):
    L, R, C = w.shape
    n = parts[0].shape[0]
    cap = max(16, VMEM_LIMIT // 2 // (2 * L * n * C * parts[0].dtype.itemsize))
    rb = max([r for r in range(16, min(R, cap) + 1, 16) if R % r == 0], default=R)

    def body(w_ref, m_ref, v_ref, *rest):
        p_refs = rest[:L]
        g_ref, d_ref, nm_ref, nv_ref = rest[L:]
        for l in range(L):
            @pl.when(pl.program_id(0) == l)
            def _(p_ref=p_refs[l]):
                g = p_ref[0].astype(F32)
                for k in range(1, n):
                    g = g + p_ref[k].astype(F32)
                mn = ADAM_B1 * m_ref[...] + (1.0 - ADAM_B1) * g
                vn = ADAM_B2 * v_ref[...] + (1.0 - ADAM_B2) * jnp.square(g)
                m_hat = mn / (1.0 - ADAM_B1 ** ADAM_STEP)
                v_hat = vn / (1.0 - ADAM_B2 ** ADAM_STEP)
                g_ref[...] = g
                d_ref[...] = -ADAM_LR * (m_hat / (jnp.sqrt(v_hat) + ADAM_EPS) + ADAM_WD * w_ref[...])
                nm_ref[...] = mn
                nv_ref[...] = vn

    row = pl.BlockSpec((None, rb, C), lambda l, i: (l, i, 0))
    part = lambda k: pl.BlockSpec((n, rb, C), lambda l, i: (0, jnp.where(l == k, i, 0), 0))
    return pl.pallas_call(
        body, grid=(L, R // rb), in_specs=[row, row, row] + [part(k) for k in range(L)],
        out_specs=[row] * 4, out_shape=[jax.ShapeDtypeStruct((L, R, C), F32)] * 4,
        compiler_params=_params(("arbitrary", "arbitrary")), name=name,
    )(w, m, v, *parts)


BIG = (("a_w_in", "a_w_in", 1, False), ("a_w_out", "a_w_out", 0, False), ("kv_w", "kv_w", 0, False),
       ("b_w_q", "b_w_q_t", 0, True), ("b_w_o", "b_w_o", 1, False), ("ffn_w_up", "ffn_w_up_t", 0, True),
       ("ffn_w_down", "ffn_w_down", 0, False))
SMALL_SHARDED = (("a_ln_g", 1), ("a_ln_b", 1), ("ffn_conv_w", 2), ("ln_g", 2), ("ln_b", 2))
REPLICATED = ("a_w_s", "a_b_s", "rel_table", "ffn_conv_b")


def _pack_rows(arrs, lead=0):
    lshape = arrs[0].shape[:lead]
    p = jnp.concatenate([a.reshape(*lshape, -1, LANES) for a in arrs], axis=lead)
    pad = -p.shape[lead] % 8
    return jnp.pad(p, [(0, 0)] * lead + [(0, pad), (0, 0)])


def _unpack_rows(packed, shapes, lead=0):
    lshape = packed.shape[:lead]
    out, off = [], 0
    for shp in shapes:
        r = int(np.prod(shp)) // LANES
        out.append(lax.slice_in_dim(packed, off, off + r, axis=lead).reshape(*lshape, *shp))
        off += r
    return out


def _as_mats(a, transposed):
    a = a[None] if a.ndim == 2 else a
    return jnp.swapaxes(a, 1, 2) if transposed else a


def _merge_shards(stacked, axis):
    a = jnp.moveaxis(stacked, 0, axis)
    shp = list(a.shape)
    return a.reshape(shp[:axis] + [shp[axis] * shp[axis + 1]] + shp[axis + 2:])


def _split_shards(full, axis):
    shp = list(full.shape)
    a = full.reshape(shp[:axis] + [N_DEV, shp[axis] // N_DEV] + shp[axis + 1:])
    return jnp.moveaxis(a, axis, 0)


def kernel(x, a_w_in, a_ln_g, a_ln_b, a_w_s, a_b_s, a_w_out, kv_w, b_w_q, b_w_o, rel_table, ffn_w_up, ffn_conv_w, ffn_conv_b, ffn_w_down, ln_g, ln_b, loss_target, m_a_w_in, m_a_ln_g, m_a_ln_b, m_a_w_s, m_a_b_s, m_a_w_out, m_kv_w, m_b_w_q, m_b_w_o, m_rel_table, m_ffn_w_up, m_ffn_conv_w, m_ffn_conv_b, m_ffn_w_down, m_ln_g, m_ln_b, v_a_w_in, v_a_ln_g, v_a_ln_b, v_a_w_s, v_a_b_s, v_a_w_out, v_kv_w, v_b_w_q, v_b_w_o, v_rel_table, v_ffn_w_up, v_ffn_conv_w, v_ffn_conv_b, v_ffn_w_down, v_ln_g, v_ln_b):
    names = ["a_w_in", "a_ln_g", "a_ln_b", "a_w_s", "a_b_s", "a_w_out", "kv_w", "b_w_q", "b_w_o", "rel_table",
             "ffn_w_up", "ffn_conv_w", "ffn_conv_b", "ffn_w_down", "ln_g", "ln_b"]
    w = dict(zip(names, (a_w_in, a_ln_g, a_ln_b, a_w_s, a_b_s, a_w_out, kv_w, b_w_q, b_w_o, rel_table,
                         ffn_w_up, ffn_conv_w, ffn_conv_b, ffn_w_down, ln_g, ln_b)))
    m = dict(zip(names, (m_a_w_in, m_a_ln_g, m_a_ln_b, m_a_w_s, m_a_b_s, m_a_w_out, m_kv_w, m_b_w_q, m_b_w_o,
                         m_rel_table, m_ffn_w_up, m_ffn_conv_w, m_ffn_conv_b, m_ffn_w_down, m_ln_g, m_ln_b)))
    v = dict(zip(names, (v_a_w_in, v_a_ln_g, v_a_ln_b, v_a_w_s, v_a_b_s, v_a_w_out, v_kv_w, v_b_w_q, v_b_w_o,
                         v_rel_table, v_ffn_w_up, v_ffn_conv_w, v_ffn_conv_b, v_ffn_w_down, v_ln_g, v_ln_b)))
    small_names = [n for n, _ in SMALL_SHARDED]
    small_shapes = [w[n].shape for n in small_names]
    rep_shapes = [w[n].shape for n in REPLICATED]

    axis_of = {key: axis for _, key, axis, _ in BIG}
    src = {}
    for n, key, axis, tr in BIG:
        loc = _as_mats(w[n], tr).astype(BF16)
        for l in range(loc.shape[0]):
            src[(key, l)] = loc[l]
    order = []
    for i in range(DEPTH):
        if i < N_A:
            order += [[("a_w_in", i)], [("a_w_out", i)]]
        else:
            order += [([("kv_w", 0)] if i == N_A else []) + [("b_w_q_t", i - N_A)], [("b_w_o", i - N_A)]]
        order += [[("ffn_w_up_t", i)], [("ffn_w_down", i)]]
    small_src = _pack_rows([w[n] for n in small_names])
    srows = small_src.shape[0]
    first = {kl: _gather_now(src[kl], axis_of[kl[0]], name="gather_first") for kl in order[0]}
    handles, _ = _send_start([[(small_src, 0)]] + [[(src[kl], axis_of[kl[0]]) for kl in grp] for grp in order[1:]],
                             True, after=first[order[0][0]], name="gather_start")
    small_all = _send_wait(handles[0], True, x, name="gather_wait_small")[0]
    small_st = _unpack_rows(small_all.reshape(N_DEV, srows, LANES), small_shapes, lead=1)
    base = {n: w[n] for n in REPLICATED}
    for (n, ax), st in zip(SMALL_SHARDED, small_st):
        base[n] = _merge_shards(st, ax)
    for n, key, _, tr in BIG:
        base[key] = [None] * (1 if w[n].ndim == 2 else w[n].shape[0])

    def fetch(group, after):
        if group == 0:
            return first
        mats = _send_wait(handles[group], True, after, name=f"gather_wait_{group}")
        return dict(zip(order[group], mats))

    sent = {}

    def emit(group, mats):
        keys = list(mats)
        hs, token = _send_start([[(mats[kl], axis_of[kl[0]]) for kl in keys]], False, name=f"exchange_start_{group}")
        sent[group] = (keys, hs[0])
        return token

    small_sent = []

    def emit_small(grads):
        small_pack = _pack_rows([_split_shards(grads[n], ax) for n, ax in SMALL_SHARDED], lead=1)
        rest = _pack_rows([grads[n] for n in REPLICATED[1:]])
        mine = jnp.concatenate([small_pack.reshape(N_DEV * srows, LANES), rest], axis=0)
        gating = grads[REPLICATED[0]].reshape(-1, LANES).astype(BF16)
        hs, token = _send_start([[(mine, 0), (gating, 0)]], True, name="small_grads_start")
        small_sent.append(hs[0])
        return token

    loss, grad_x, grads = _local_step(x[0], loss_target[0], _Weights(base, fetch, emit, emit_small))
    loss = lax.psum(loss, ("x", "y", "c"))
    out = {}

    landed = {}
    last = grad_x
    left = lambda e: min(g for g, (keys, _) in sent.items() if any(k[0] == e[1] for k in keys))
    for n, key, axis, tr in sorted(BIG, key=left, reverse=True):
        shp = w[n].shape
        for group in sorted(sent, reverse=True):
            keys, h = sent[group]
            if keys[0] not in landed and any(k[0] == key for k in keys):
                landed.update(zip(keys, _send_wait(h, False, last, name=f"exchange_wait_{group}")))
        parts = [landed[(key, l)] for l in range(1 if len(shp) == 2 else shp[0])]
        res = _adamw(_as_mats(w[n], tr), _as_mats(m[n], tr), _as_mats(v[n], tr), parts, name=f"adamw_{n}")
        out[n] = [(jnp.swapaxes(r, 1, 2) if tr else r).reshape(shp) for r in res]
        last = res[0]

    allp, allg = _send_wait(small_sent[0], True, last, name="small_grads_wait")
    gsum = _sum_parts(allp.reshape(N_DEV, -1, LANES), name="sum_small_grads")
    gating = _sum_parts(allg.reshape(N_DEV, -1, LANES), name="sum_gating_grads")
    g_small = lax.dynamic_slice_in_dim(gsum, _my_index() * srows, srows, axis=0)
    pack_sr = lambda d: jnp.concatenate([_pack_rows([d[n] for n in small_names]),
                                         _pack_rows([d[n] for n in REPLICATED])], axis=0)
    n_rest = sum(int(np.prod(s)) for s in rep_shapes[1:]) // LANES
    gs_in = jnp.concatenate([g_small, gating, gsum[N_DEV * srows:N_DEV * srows + n_rest]], axis=0)
    gs_in = jnp.pad(gs_in, ((0, pack_sr(w).shape[0] - gs_in.shape[0]), (0, 0)))[None]
    res = _adamw(pack_sr(w)[None], pack_sr(m)[None], pack_sr(v)[None], [gs_in], name="adamw_small")
    for n, vals in zip(small_names, zip(*[_unpack_rows(r[0, :srows], small_shapes) for r in res])):
        out[n] = list(vals)
    for n, vals in zip(REPLICATED, zip(*[_unpack_rows(r[0, srows:], rep_shapes) for r in res])):
        out[n] = list(vals)

    return (loss, grad_x[None], *[out[n][0] for n in names], *[out[n][1] for n in names],
            *[out[n][2] for n in names], *[out[n][3] for n in names])
```

```python
import math

import numpy as np
import jax
import jax.numpy as jnp
from jax import lax
from jax.experimental import pallas as pl
from jax.experimental.pallas import tpu as pltpu

F32 = jnp.float32
BF16 = jnp.bfloat16
ACT = jnp.bfloat16
MESH = pl.DeviceIdType.MESH

N_DEV = 8
DEPTH = 4
N_A = 2
CHUNK = 128
BLK = 128
HEAD_DIM = 64
DILATED_GROUPS = ((128, 1), (512, 4), (2048, 16))
N_GROUPS = 3
REL_BUCKETS = 32
REL_MAX_DIST = 2048
ALPHA = (2 * DEPTH) ** 0.25
LN_EPS = 1e-5
NEG = -1e30
ADAM_LR = 0.001
ADAM_B1 = 0.9
ADAM_B2 = 0.999
ADAM_EPS = 1e-08
ADAM_WD = 0.01
ADAM_STEP = 10

LANES = 128
VMEM_LIMIT = 56 * 1024 * 1024
MM_TILE_CAP = 1408
MM_VMEM_BUDGET = 46 * 1024 * 1024
INV_SQRT2 = 1.0 / math.sqrt(2.0)
INV_SQRT_2PI = 1.0 / math.sqrt(2.0 * math.pi)


def _pick(n, cap):
    best = None
    for t in range(LANES, min(n, cap) + 1, LANES):
        if n % t == 0:
            best = t
    return best if best is not None else n


def _params(sem):
    return pltpu.CompilerParams(dimension_semantics=sem, vmem_limit_bytes=VMEM_LIMIT)


def _ordered(body, in_specs, args, after):
    if after is None:
        return body, list(in_specs), tuple(args)
    return (lambda _, *refs: body(*refs)), [pl.BlockSpec(memory_space=pl.ANY), *in_specs], (after, *args)


def _gelu(x):
    return 0.5 * x * (1.0 + lax.erf(x * INV_SQRT2))


def _gelu_grad(x):
    return 0.5 * (1.0 + lax.erf(x * INV_SQRT2)) + x * jnp.exp(-0.5 * x * x) * INV_SQRT_2PI


def _mm(a, b, *, ta=False, tb=False, out_dtype=F32, scale=None, after=None, name):
    halves = isinstance(a, tuple)
    parts = 1 if halves or a.ndim == 2 else a.shape[0]
    ash = (a[0].shape[0], 2 * a[0].shape[1]) if halves else (a.shape if parts == 1
                                                               else (a.shape[1], parts * a.shape[2]))
    if ta:
        K, M = ash
    else:
        M, K = ash
    if tb:
        N, Kb = b.shape
    else:
        Kb, N = b.shape
    assert K == Kb, (ash, b.shape, ta, tb)
    split = 2 if halves else parts
    tm = _pick(M // split if ta else M, MM_TILE_CAP)
    tn = _pick(N, MM_TILE_CAP)
    kspan = K if ta or split == 1 else K // split
    abytes = (a[0] if halves else a).dtype.itemsize * (2 if halves else 1)
    fixed = 2 * tm * tn * jnp.dtype(out_dtype).itemsize + tm * tn * 4
    fits = [t for t in range(LANES, kspan + 1, LANES)
            if kspan % t == 0 and 2 * t * (tm * abytes + tn * b.dtype.itemsize) + fixed <= MM_VMEM_BUDGET]
    tk = max(fits) if fits else _pick(kspan, MM_TILE_CAP)
    nk = K // tk
    nh = (M // split // tm if ta else K // split // tk) if split > 1 else 0
    dn = (((0 if ta else 1,), (1 if tb else 0,)), ((), ()))

    def body(*refs):
        n_tail = 3 if nk > 1 else 2
        a_refs, b_ref, o_ref = refs[:-n_tail], refs[-n_tail], refs[-n_tail + 1]
        k = pl.program_id(2)

        def finish(r):
            if scale is not None:
                r = r * scale
            o_ref[...] = r.astype(out_dtype)

        def accumulate(a_ref):
            part = lax.dot_general(a_ref[...].astype(BF16), b_ref[...].astype(BF16), dn,
                                   preferred_element_type=F32)
            if nk == 1:
                finish(part)
                return
            acc_ref = refs[-1]

            @pl.when(k == 0)
            def _():
                acc_ref[...] = part

            @pl.when((k > 0) & (k < nk - 1))
            def _():
                acc_ref[...] += part

            @pl.when(k == nk - 1)
            def _():
                finish(acc_ref[...] + part)

        if halves:
            first = (pl.program_id(0) if ta else k) < nh
            pl.when(first)(lambda: accumulate(a_refs[0]))
            pl.when(jnp.logical_not(first))(lambda: accumulate(a_refs[1]))
        else:
            accumulate(a_refs[0])

    if halves and ta:
        a_specs = [pl.BlockSpec((tk, tm), lambda i, j, k: (jnp.where(i < nh, k, 0), jnp.minimum(i, nh - 1))),
                   pl.BlockSpec((tk, tm), lambda i, j, k: (jnp.where(i >= nh, k, 0), jnp.maximum(i - nh, 0)))]
    elif halves:
        a_specs = [pl.BlockSpec((tm, tk), lambda i, j, k: (i, jnp.minimum(k, nh - 1))),
                   pl.BlockSpec((tm, tk), lambda i, j, k: (i, jnp.maximum(k - nh, 0)))]
    elif parts > 1:
        a_specs = [pl.BlockSpec((None, tk, tm), lambda i, j, k: (i // nh, k, i % nh)) if ta
                   else pl.BlockSpec((None, tm, tk), lambda i, j, k: (k // nh, i, k % nh))]
    else:
        a_specs = [pl.BlockSpec((tk, tm), lambda i, j, k: (k, i)) if ta
                   else pl.BlockSpec((tm, tk), lambda i, j, k: (i, k))]
    b_spec = (pl.BlockSpec((tn, tk), lambda i, j, k: (j, k)) if tb
              else pl.BlockSpec((tk, tn), lambda i, j, k: (k, j)))
    body, in_specs, args = _ordered(body, [*a_specs, b_spec], (*(a if halves else (a,)), b), after)
    return pl.pallas_call(
        body, grid=(M // tm, N // tn, nk), in_specs=in_specs,
        out_specs=pl.BlockSpec((tm, tn), lambda i, j, k: (i, j)),
        out_shape=jax.ShapeDtypeStruct((M, N), out_dtype),
        scratch_shapes=[pltpu.VMEM((tm, tn), F32)] if nk > 1 else [],
        compiler_params=_params(("parallel", "parallel", "arbitrary")), name=name,
    )(*args)


def _add_ln_fwd(x, h, g, b, *, wide, name):
    T, D = x.shape
    rb = _pick(T, 1024)

    def body(x_ref, h_ref, g_ref, b_ref, *o_refs):
        pre = ALPHA * x_ref[...].astype(F32) + h_ref[...].astype(F32)
        mu = jnp.mean(pre, axis=1, keepdims=True)
        cen = pre - mu
        var = jnp.mean(cen * cen, axis=1, keepdims=True)
        y = cen * lax.rsqrt(var + LN_EPS) * g_ref[...] + b_ref[...]
        for o_ref in o_refs:
            o_ref[...] = y.astype(o_ref.dtype)

    row = pl.BlockSpec((rb, D), lambda i: (i, 0))
    vec = pl.BlockSpec((1, D), lambda i: (0, 0))
    dtypes = [F32, BF16] if wide else [BF16]
    return pl.pallas_call(
        body, grid=(T // rb,), in_specs=[row, row, vec, vec], out_specs=[row] * len(dtypes),
        out_shape=[jax.ShapeDtypeStruct((T, D), dt) for dt in dtypes],
        compiler_params=_params(("parallel",)), name=name,
    )(x, h, g.reshape(1, D), b.reshape(1, D))


def _add_ln_bwd(x, h, g, terms, *, after=None, name):
    T, D = x.shape
    rb = _pick(T, 1024)
    coefs = [c for c, _ in terms]
    nt = len(terms)

    def body(*refs):
        x_ref, h_ref, g_ref = refs[:3]
        t_refs = refs[3:3 + nt]
        dpb_ref, dg_ref, db_ref = refs[3 + nt:]
        dy = None
        for c, r in zip(coefs, t_refs):
            v = r[...].astype(F32) if c == 1.0 else c * r[...].astype(F32)
            dy = v if dy is None else dy + v
        pre = ALPHA * x_ref[...].astype(F32) + h_ref[...].astype(F32)
        mu = jnp.mean(pre, axis=1, keepdims=True)
        cen = pre - mu
        var = jnp.mean(cen * cen, axis=1, keepdims=True)
        rstd = lax.rsqrt(var + LN_EPS)
        xhat = cen * rstd
        dxh = dy * g_ref[...]
        m1 = jnp.mean(dxh, axis=1, keepdims=True)
        m2 = jnp.mean(dxh * xhat, axis=1, keepdims=True)
        dpre = rstd * (dxh - m1 - xhat * m2)
        dpb_ref[...] = dpre.astype(BF16)
        dg = jnp.sum(dy * xhat, axis=0, keepdims=True)
        db = jnp.sum(dy, axis=0, keepdims=True)

        @pl.when(pl.program_id(0) == 0)
        def _():
            dg_ref[...] = dg
            db_ref[...] = db

        @pl.when(pl.program_id(0) > 0)
        def _():
            dg_ref[...] += dg
            db_ref[...] += db

    row = pl.BlockSpec((rb, D), lambda i: (i, 0))
    vec = pl.BlockSpec((1, D), lambda i: (0, 0))
    body, in_specs, args = _ordered(body, [row, row, vec] + [row] * nt,
                                    (x, h, g.reshape(1, D), *[a for _, a in terms]), after)
    return pl.pallas_call(
        body, grid=(T // rb,), in_specs=in_specs,
        out_specs=[row, vec, vec],
        out_shape=[jax.ShapeDtypeStruct((T, D), BF16),
                   jax.ShapeDtypeStruct((1, D), F32), jax.ShapeDtypeStruct((1, D), F32)],
        compiler_params=_params(("arbitrary",)), name=name,
    )(*args)


def _lincomb(terms, out_dtype, *, name):
    R, C = terms[0][1].shape
    rb = _pick(R, 1024)
    coefs = [c for c, _ in terms]
    nt = len(terms)

    def body(*refs):
        acc = None
        for c, r in zip(coefs, refs[:nt]):
            v = r[...].astype(F32)
            v = v if c == 1.0 else c * v
            acc = v if acc is None else acc + v
        refs[nt][...] = acc.astype(out_dtype)

    row = pl.BlockSpec((rb, C), lambda i: (i, 0))
    return pl.pallas_call(
        body, grid=(R // rb,), in_specs=[row] * nt, out_specs=row,
        out_shape=jax.ShapeDtypeStruct((R, C), out_dtype),
        compiler_params=_params(("parallel",)), name=name,
    )(*[a for _, a in terms])


def _loss_grad(y, tgt, *, name):
    T, D = y.shape
    rb = _pick(T, 1024)

    def body(y_ref, t_ref, dy_ref, l_ref):
        err = y_ref[...] - t_ref[...]
        dy_ref[...] = err * (1.0 / D)
        part = jnp.sum(jnp.sum(err * err, axis=1, keepdims=True), axis=0, keepdims=True) * (0.5 / D)
        part = jnp.broadcast_to(part, (1, LANES))

        @pl.when(pl.program_id(0) == 0)
        def _():
            l_ref[...] = part

        @pl.when(pl.program_id(0) > 0)
        def _():
            l_ref[...] += part

    row = pl.BlockSpec((rb, D), lambda i: (i, 0))
    return pl.pallas_call(
        body, grid=(T // rb,), in_specs=[row, row],
        out_specs=[row, pl.BlockSpec((1, LANES), lambda i: (0, 0))],
        out_shape=[jax.ShapeDtypeStruct((T, D), F32), jax.ShapeDtypeStruct((1, LANES), F32)],
        compiler_params=_params(("arbitrary",)), name=name,
    )(y, tgt)


def _sgu_fwd(zp, ln_g, ln_b, ws, bst, *, name):
    T, E2 = zp.shape
    E = E2 // 2
    G = ws.shape[0]
    cg = E // G
    rb = 4 * CHUNK

    def body(z_ref, g_ref, b_ref, ws_ref, bs_ref, y_ref):
        u = _gelu(z_ref[:, :E].astype(F32))
        v = _gelu(z_ref[:, E:].astype(F32))
        mu = jnp.mean(v, axis=1, keepdims=True)
        cen = v - mu
        var = jnp.mean(cen * cen, axis=1, keepdims=True)
        vn = (cen * lax.rsqrt(var + LN_EPS) * g_ref[...] + b_ref[...]).astype(BF16)
        for ci in range(rb // CHUNK):
            rows = slice(ci * CHUNK, (ci + 1) * CHUNK)
            for gi in range(G):
                cols = slice(gi * cg, (gi + 1) * cg)
                sv = jnp.dot(ws_ref[gi], vn[rows, cols], preferred_element_type=F32)
                sv = sv + bs_ref[:, gi:gi + 1]
                y_ref[rows, cols] = (u[rows, cols] * sv).astype(BF16)

    return pl.pallas_call(
        body, grid=(T // rb,),
        in_specs=[pl.BlockSpec((rb, E2), lambda i: (i, 0)),
                  pl.BlockSpec((1, E), lambda i: (0, 0)), pl.BlockSpec((1, E), lambda i: (0, 0)),
                  pl.BlockSpec((G, CHUNK, CHUNK), lambda i: (0, 0, 0)),
                  pl.BlockSpec((CHUNK, G), lambda i: (0, 0))],
        out_specs=pl.BlockSpec((rb, E), lambda i: (i, 0)),
        out_shape=jax.ShapeDtypeStruct((T, E), BF16),
        compiler_params=_params(("parallel",)), name=name,
    )(zp, ln_g.reshape(1, E), ln_b.reshape(1, E), ws, bst)


def _sgu_bwd(zp, dy, ln_g, ln_b, ws, wst, bst, *, after=None, name):
    T, E2 = zp.shape
    E = E2 // 2
    G = ws.shape[0]
    cg = E // G
    rb = CHUNK
    nsteps = T // rb

    def body(z_ref, dy_ref, g_ref, b_ref, ws_ref, wst_ref, bs_ref,
             dz_ref, dg_ref, db_ref, dws_ref, dbs_ref, dsv_acc):
        step = pl.program_id(0)

        @pl.when(step == 0)
        def _():
            dg_ref[...] = jnp.zeros_like(dg_ref)
            db_ref[...] = jnp.zeros_like(db_ref)
            dws_ref[...] = jnp.zeros_like(dws_ref)
            dsv_acc[...] = jnp.zeros_like(dsv_acc)

        zu = z_ref[:, :E].astype(F32)
        zv = z_ref[:, E:].astype(F32)
        u = _gelu(zu)
        v = _gelu(zv)
        mu = jnp.mean(v, axis=1, keepdims=True)
        cen = v - mu
        var = jnp.mean(cen * cen, axis=1, keepdims=True)
        rstd = lax.rsqrt(var + LN_EPS)
        xhat = cen * rstd
        vn = (xhat * g_ref[...] + b_ref[...]).astype(BF16)
        dyv = dy_ref[...].astype(F32)
        dsv = dyv * u
        dsv_acc[...] += dsv
        dsvb = dsv.astype(BF16)
        tril = (lax.broadcasted_iota(jnp.int32, (CHUNK, CHUNK), 0)
                >= lax.broadcasted_iota(jnp.int32, (CHUNK, CHUNK), 1))
        du_parts = []
        dvn_parts = []
        for gi in range(G):
            cols = slice(gi * cg, (gi + 1) * cg)
            sv = jnp.dot(ws_ref[gi], vn[:, cols], preferred_element_type=F32) + bs_ref[:, gi:gi + 1]
            du_parts.append(dyv[:, cols] * sv)
            dvn_parts.append(jnp.dot(wst_ref[gi], dsvb[:, cols], preferred_element_type=F32))
            dw = lax.dot_general(dsvb[:, cols], vn[:, cols], (((1,), (1,)), ((), ())),
                                 preferred_element_type=F32)
            dws_ref[gi] += jnp.where(tril, dw, 0.0)
        du = jnp.concatenate(du_parts, axis=1)
        dvn = jnp.concatenate(dvn_parts, axis=1)
        dg_ref[...] += jnp.sum(dvn * xhat, axis=0, keepdims=True)
        db_ref[...] += jnp.sum(dvn, axis=0, keepdims=True)
        dxh = dvn * g_ref[...]
        m1 = jnp.mean(dxh, axis=1, keepdims=True)
        m2 = jnp.mean(dxh * xhat, axis=1, keepdims=True)
        dv = rstd * (dxh - m1 - xhat * m2)
        dz_ref[:, :E] = (du * _gelu_grad(zu)).astype(BF16)
        dz_ref[:, E:] = (dv * _gelu_grad(zv)).astype(BF16)

        @pl.when(step == nsteps - 1)
        def _():
            lane = lax.broadcasted_iota(jnp.int32, (CHUNK, LANES), 1)
            out = jnp.zeros((CHUNK, LANES), F32)
            for gi in range(G):
                s = jnp.sum(dsv_acc[:, gi * cg:(gi + 1) * cg], axis=1, keepdims=True)
                out = jnp.where(lane == gi, s, out)
            dbs_ref[...] = out

    vecE = pl.BlockSpec((1, E), lambda i: (0, 0))
    wspec = pl.BlockSpec((G, CHUNK, CHUNK), lambda i: (0, 0, 0))
    body, in_specs, args = _ordered(
        body, [pl.BlockSpec((rb, E2), lambda i: (i, 0)), pl.BlockSpec((rb, E), lambda i: (i, 0)),
               vecE, vecE, wspec, wspec, pl.BlockSpec((CHUNK, G), lambda i: (0, 0))],
        (zp, dy, ln_g.reshape(1, E), ln_b.reshape(1, E), ws, wst, bst), after)
    return pl.pallas_call(
        body, grid=(nsteps,), in_specs=in_specs,
        out_specs=[pl.BlockSpec((rb, E2), lambda i: (i, 0)), vecE, vecE, wspec,
                   pl.BlockSpec((CHUNK, LANES), lambda i: (0, 0))],
        out_shape=[jax.ShapeDtypeStruct((T, E2), BF16), jax.ShapeDtypeStruct((1, E), F32),
                   jax.ShapeDtypeStruct((1, E), F32), jax.ShapeDtypeStruct((G, CHUNK, CHUNK), F32),
                   jax.ShapeDtypeStruct((CHUNK, LANES), F32)],
        scratch_shapes=[pltpu.VMEM((CHUNK, E), F32)],
        compiler_params=_params(("arbitrary",)), name=name,
    )(*args)


def _shift_down(x, k, row):
    return jnp.where(row >= k, pltpu.roll(x, k, 0), 0.0)


def _shift_up(x, k, row, T):
    return jnp.where(row < T - k, pltpu.roll(x, T - k, 0), 0.0)


def _conv3(x, w_ref, b_ref, row):
    return (w_ref[0:1, :] * _shift_down(x, 2, row) + w_ref[1:2, :] * _shift_down(x, 1, row)
            + w_ref[2:3, :] * x + b_ref[...])


def _convgate_fwd(hh, cw, cb, *, name):
    T, F2 = hh.shape
    F = F2 // 2
    ns = F // LANES

    def body(a_ref, g_ref, wa_ref, wg_ref, ba_ref, bg_ref, o_ref, ca_ref, cg_ref):
        row = lax.broadcasted_iota(jnp.int32, (T, LANES), 0)
        ca = _conv3(a_ref[...].astype(F32), wa_ref, ba_ref, row)
        cgv = _conv3(g_ref[...].astype(F32), wg_ref, bg_ref, row)
        o_ref[...] = (_gelu(ca) * cgv).astype(BF16)
        ca_ref[...] = ca.astype(ACT)
        cg_ref[...] = cgv.astype(ACT)

    sa = lambda r: pl.BlockSpec((r, LANES), lambda j: (0, j))
    sg = lambda r: pl.BlockSpec((r, LANES), lambda j: (0, j + ns))
    return pl.pallas_call(
        body, grid=(ns,), in_specs=[sa(T), sg(T), sa(3), sg(3), sa(1), sg(1)],
        out_specs=[sa(T)] * 3,
        out_shape=[jax.ShapeDtypeStruct((T, F), BF16), jax.ShapeDtypeStruct((T, F), ACT),
                   jax.ShapeDtypeStruct((T, F), ACT)],
        compiler_params=_params(("parallel",)), name=name,
    )(hh, hh, cw, cw, cb, cb)


def _convgate_bwd(hh, hca, hcg, dact, cw, *, name):
    T, F2 = hh.shape
    F = F2 // 2
    ns = F // LANES

    def body(a_ref, g_ref, ca_ref, cg_ref, d_ref, wa_ref, wg_ref,
             da_ref, dg_ref, dwa_ref, dwg_ref, dba_ref, dbg_ref):
        row = lax.broadcasted_iota(jnp.int32, (T, LANES), 0)
        d = d_ref[...].astype(F32)
        ca = ca_ref[...].astype(F32)
        cgv = cg_ref[...].astype(F32)
        cdf = 0.5 * (1.0 + lax.erf(ca * INV_SQRT2))
        dca = d * cgv * (cdf + ca * jnp.exp(-0.5 * ca * ca) * INV_SQRT_2PI)
        dcg = d * (ca * cdf)
        for x_ref, w_ref, dc, dx_ref, dw_ref, db_ref in (
                (a_ref, wa_ref, dca, da_ref, dwa_ref, dba_ref),
                (g_ref, wg_ref, dcg, dg_ref, dwg_ref, dbg_ref)):
            x = x_ref[...].astype(F32)
            up1, up2 = _shift_up(dc, 1, row, T), _shift_up(dc, 2, row, T)
            dx_ref[...] = (w_ref[2:3, :] * dc + w_ref[1:2, :] * up1 + w_ref[0:1, :] * up2).astype(BF16)
            dw_ref[0:1, :] = jnp.sum(up2 * x, axis=0, keepdims=True)
            dw_ref[1:2, :] = jnp.sum(up1 * x, axis=0, keepdims=True)
            dw_ref[2:3, :] = jnp.sum(dc * x, axis=0, keepdims=True)
            db_ref[...] = jnp.sum(dc, axis=0, keepdims=True)

    sa = lambda r: pl.BlockSpec((r, LANES), lambda j: (0, j))
    sg = lambda r: pl.BlockSpec((r, LANES), lambda j: (0, j + ns))
    return pl.pallas_call(
        body, grid=(ns,), in_specs=[sa(T), sg(T), sa(T), sa(T), sa(T), sa(3), sg(3)],
        out_specs=[sa(T), sa(T), sa(3), sa(3), sa(1), sa(1)],
        out_shape=[jax.ShapeDtypeStruct((T, F), BF16), jax.ShapeDtypeStruct((T, F), BF16),
                   jax.ShapeDtypeStruct((3, F), F32), jax.ShapeDtypeStruct((3, F), F32),
                   jax.ShapeDtypeStruct((1, F), F32), jax.ShapeDtypeStruct((1, F), F32)],
        compiler_params=_params(("parallel",)), name=name,
    )(hh, hh, hca, hcg, dact, cw, cw)


def _bucket_maps():
    iq = np.arange(BLK)[:, None]
    ik = np.arange(2 * BLK)[None, :]
    delta = iq + BLK - ik
    maps = []
    for win, dil in DILATED_GROUPS:
        n = np.clip(delta, 0, None) * dil
        max_exact = REL_BUCKETS // 2
        nf = np.maximum(n, 1).astype(np.float32)
        large = max_exact + (np.log(nf / np.float32(max_exact)) / np.float32(math.log(REL_MAX_DIST / max_exact))
                             * np.float32(REL_BUCKETS - max_exact)).astype(np.int32)
        large = np.minimum(large, REL_BUCKETS - 1)
        bucket = np.where(n < max_exact, n, large)
        valid = (delta >= 0) & (delta <= win // dil)
        maps.append(np.where(valid, bucket, -1).astype(np.int32))
    return np.stack(maps)


def _band_bias(rel_table, bmap, H, *, name):
    def body(t_ref, m_ref, o_ref):
        g = pl.program_id(0)
        bm = m_ref[0]
        for h in range(H):
            acc = jnp.full((BLK, 2 * BLK), NEG, F32)
            for b in range(REL_BUCKETS):
                acc = jnp.where(bm == b, t_ref[b, g * H + h], acc)
            o_ref[0, h] = acc

    return pl.pallas_call(
        body, grid=(N_GROUPS,),
        in_specs=[pl.BlockSpec(memory_space=pltpu.SMEM),
                  pl.BlockSpec((1, BLK, 2 * BLK), lambda g: (g, 0, 0))],
        out_specs=pl.BlockSpec((1, H, BLK, 2 * BLK), lambda g: (g, 0, 0, 0)),
        out_shape=jax.ShapeDtypeStruct((N_GROUPS, H, BLK, 2 * BLK), F32),
        compiler_params=_params(("parallel",)), name=name,
    )(rel_table, bmap)


def _band_bias_bwd(dbias, bmap, H, *, name):
    def body(d_ref, m_ref, o_ref):
        bm = m_ref[0]
        rowi = lax.broadcasted_iota(jnp.int32, (REL_BUCKETS, LANES), 0)
        lane = lax.broadcasted_iota(jnp.int32, (REL_BUCKETS, LANES), 1)
        out = jnp.zeros((REL_BUCKETS, LANES), F32)
        for h in range(H):
            dv = d_ref[0, h]
            for b in range(REL_BUCKETS):
                s = jnp.sum(jnp.sum(jnp.where(bm == b, dv, 0.0), axis=1, keepdims=True),
                            axis=0, keepdims=True)
                out = jnp.where((rowi == b) & (lane == h), s, out)
        o_ref[0] = out

    return pl.pallas_call(
        body, grid=(N_GROUPS,),
        in_specs=[pl.BlockSpec((1, H, BLK, 2 * BLK), lambda g: (g, 0, 0, 0)),
                  pl.BlockSpec((1, BLK, 2 * BLK), lambda g: (g, 0, 0))],
        out_specs=pl.BlockSpec((1, REL_BUCKETS, LANES), lambda g: (g, 0, 0)),
        out_shape=jax.ShapeDtypeStruct((N_GROUPS, REL_BUCKETS, LANES), F32),
        compiler_params=_params(("parallel",)), name=name,
    )(dbias, bmap)


def _head_masks():
    lane = lax.broadcasted_iota(jnp.int32, (BLK, LANES), 1)
    return (lane < HEAD_DIM, lane >= HEAD_DIM)


SUPER =DILATED_GROUPS[-1][1] * BLK


def _band_rows(it, d):
    r, j = it % d, it // d
    if d == 1:
        at = lambda blk: pl.ds(pl.multiple_of(blk * BLK, BLK), BLK)
    else:
        at = lambda blk: pl.ds(r + d * BLK * blk, BLK, stride=d)
    return at(j), at(jnp.maximum(j - 1, 0))


def _stack_heads(x, hm):
    zero = jnp.zeros_like(x)
    return jnp.concatenate([jnp.where(hm[0], x, zero), jnp.where(hm[1], x, zero)], axis=0)


def _band_loops(step, d, unroll):
    n_it = SUPER // BLK

    def run(lo, hi, inside):
        if hi > lo:
            def body(it, carry):
                step(it, inside)
                return carry
            lax.fori_loop(lo, hi, body, 0, unroll=max(u for u in range(1, unroll + 1) if (hi - lo) % u == 0))

    run(0, d, False)
    run(d, n_it, True)


def _last_rows(it, d):
    m = SUPER // (d * BLK)
    if d == 1:
        return pl.ds((m - 1) * BLK, BLK)
    return pl.ds(it % d + d * BLK * (m - 1), BLK, stride=d)


def _attn_fwd_all(q, kv, bias, *, name):
    T = q.shape[0]
    HD = kv.shape[1] // 2
    PP = HD // LANES
    NS = T // SUPER

    def body(q0, q1, q2, kp_ref, kc_ref, vp_ref, vc_ref, b_ref, o_ref, ob_ref, l_ref, og, lg):
        n = pl.program_id(1)
        col = lax.broadcasted_iota(jnp.int32, (2 * BLK, 2 * BLK), 1)
        hm = _head_masks()
        for g, (q_ref, (_, d)) in enumerate(zip((q0, q1, q2), DILATED_GROUPS)):
            def step(it, inside, g=g, q_ref=q_ref, d=d):
                cur, prv = _band_rows(it, d)
                qp = q_ref[cur, :].astype(BF16)
                if inside:
                    kprev, vprev = kc_ref[prv, :], vc_ref[prv, :]
                else:
                    last = _last_rows(it, d)
                    kprev, vprev = kp_ref[last, :], vp_ref[last, :]
                kc = jnp.concatenate([kprev.astype(BF16), kc_ref[cur, :].astype(BF16)], axis=0)
                vc = jnp.concatenate([vprev.astype(BF16), vc_ref[cur, :].astype(BF16)], axis=0)
                s = lax.dot_general(_stack_heads(qp, hm), kc, (((1,), (1,)), ((), ())),
                                    preferred_element_type=F32)
                s = s + b_ref[g].reshape(2 * BLK, 2 * BLK)
                if not inside:
                    s = jnp.where((n == 0) & (col < BLK), NEG, s)
                mx = jnp.max(s, axis=1, keepdims=True)
                e = jnp.exp(s - mx)
                den = jnp.sum(e, axis=1, keepdims=True)
                out = jnp.dot((e / den).astype(BF16), vc, preferred_element_type=F32)
                lse = mx + jnp.log(den)
                og.at[g][cur, :] = jnp.where(hm[0], out[:BLK], out[BLK:])
                lg.at[g][cur, :] = jnp.where(hm[0], lse[:BLK], lse[BLK:])

            _band_loops(step, d, 8)
        la, lb, lc = lg[0], lg[1], lg[2]
        mx = jnp.maximum(jnp.maximum(la, lb), lc)
        L = mx + jnp.log(jnp.exp(la - mx) + jnp.exp(lb - mx) + jnp.exp(lc - mx))
        o = jnp.exp(la - L) * og[0] + jnp.exp(lb - L) * og[1] + jnp.exp(lc - L) * og[2]
        o_ref[...] = o
        ob_ref[...] = o.astype(BF16)
        l_ref[...] = L

    blk = lambda f: pl.BlockSpec((SUPER, LANES), f)
    prev = lambda n: jnp.maximum(n - 1, 0)
    qspec = lambda g: blk(lambda p, n: (n, g * PP + p))
    return pl.pallas_call(
        body, grid=(PP, NS),
        in_specs=[qspec(0), qspec(1), qspec(2),
                  blk(lambda p, n: (prev(n), p)), blk(lambda p, n: (n, p)),
                  blk(lambda p, n: (prev(n), PP + p)), blk(lambda p, n: (n, PP + p)),
                  pl.BlockSpec((N_GROUPS, 2, BLK, 2 * BLK), lambda p, n: (0, p, 0, 0))],
        out_specs=[blk(lambda p, n: (n, p))] * 3,
        out_shape=[jax.ShapeDtypeStruct((T, HD), F32), jax.ShapeDtypeStruct((T, HD), BF16),
                   jax.ShapeDtypeStruct((T, HD), F32)],
        scratch_shapes=[pltpu.VMEM((N_GROUPS, SUPER, LANES), F32), pltpu.VMEM((N_GROUPS, SUPER, LANES), F32)],
        compiler_params=_params(("parallel", "parallel")), name=name,
    )(q, q, q, kv, kv, kv, kv, bias)


def _attn_bwd_all(q, kv, bias, do, o, L, *, after=None, name):
    T = q.shape[0]
    HD = kv.shape[1] // 2
    PP = HD // LANES
    H = HD // HEAD_DIM
    NS = T // SUPER

    def body(q0, q1, q2, kp_ref, kc_ref, vp_ref, vc_ref, b_ref, do_ref, o_ref, L_ref,
             dq_ref, dk_ref, dv_ref, db_ref, ck_ref, cv_ref):
        n = pl.program_id(1)

        @pl.when(n == 0)
        def _():
            db_ref[...] = jnp.zeros_like(db_ref)
            ck_ref[...] = jnp.zeros_like(ck_ref)
            cv_ref[...] = jnp.zeros_like(cv_ref)

        dk_ref[...] = ck_ref[...]
        dv_ref[...] = cv_ref[...]
        ck_ref[...] = jnp.zeros_like(ck_ref)
        cv_ref[...] = jnp.zeros_like(cv_ref)

        @pl.when(n < NS)
        def _():
            col = lax.broadcasted_iota(jnp.int32, (2 * BLK, 2 * BLK), 1)
            hm = _head_masks()
            for g, (q_ref, (_, d)) in enumerate(zip((q0, q1, q2), DILATED_GROUPS)):
                def step(it, inside, g=g, q_ref=q_ref, d=d):
                    cur, prv = _band_rows(it, d)
                    last = _last_rows(it, d)
                    qp = q_ref[cur, :].astype(BF16)
                    if inside:
                        kprev, vprev = kc_ref[prv, :], vc_ref[prv, :]
                    else:
                        kprev, vprev = kp_ref[last, :], vp_ref[last, :]
                    kc = jnp.concatenate([kprev.astype(BF16), kc_ref[cur, :].astype(BF16)], axis=0)
                    vc = jnp.concatenate([vprev.astype(BF16), vc_ref[cur, :].astype(BF16)], axis=0)
                    dop = do_ref[cur, :]
                    prod = dop * o_ref[cur, :]
                    Lp = L_ref[cur, :]
                    qs = _stack_heads(qp, hm)
                    dos = _stack_heads(dop.astype(BF16), hm)
                    lse = jnp.concatenate([Lp[:, 0:1], Lp[:, HEAD_DIM:HEAD_DIM + 1]], axis=0)
                    delta = jnp.concatenate([jnp.sum(jnp.where(hm[0], prod, 0.0), axis=1, keepdims=True),
                                             jnp.sum(jnp.where(hm[1], prod, 0.0), axis=1, keepdims=True)], axis=0)
                    s = lax.dot_general(qs, kc, (((1,), (1,)), ((), ())), preferred_element_type=F32)
                    s = s + b_ref[g].reshape(2 * BLK, 2 * BLK)
                    if not inside:
                        s = jnp.where((n == 0) & (col < BLK), NEG, s)
                    pr = jnp.exp(s - lse)
                    dp = lax.dot_general(dos, vc, (((1,), (1,)), ((), ())), preferred_element_type=F32)
                    ds = pr * (dp - delta)
                    db_ref[g] += ds.reshape(2, BLK, 2 * BLK)
                    dsb = ds.astype(BF16)
                    dqs = jnp.dot(dsb, kc, preferred_element_type=F32)
                    dkc = lax.dot_general(dsb, qs, (((0,), (0,)), ((), ())), preferred_element_type=F32)
                    dvc = lax.dot_general(pr.astype(BF16), dos, (((0,), (0,)), ((), ())),
                                          preferred_element_type=F32)
                    dq_ref.at[g][cur, :] = jnp.where(hm[0], dqs[:BLK], dqs[BLK:]) * (HEAD_DIM ** -0.5)
                    ck_ref[cur, :] += dkc[BLK:]
                    cv_ref[cur, :] += dvc[BLK:]
                    if inside:
                        ck_ref[prv, :] += dkc[:BLK]
                        cv_ref[prv, :] += dvc[:BLK]
                    else:
                        dk_ref[last, :] += dkc[:BLK]
                        dv_ref[last, :] += dvc[:BLK]

                _band_loops(step, d, 4)

    blk = lambda f: pl.BlockSpec((SUPER, LANES), f)
    cur = lambda n: jnp.minimum(n, NS - 1)
    prev = lambda n: jnp.maximum(jnp.minimum(n, NS - 1) - 1, 0)
    lag = lambda n: jnp.maximum(n - 1, 0)
    qspec = lambda g: blk(lambda p, n: (cur(n), g * PP + p))
    bspec = pl.BlockSpec((N_GROUPS, 2, BLK, 2 * BLK), lambda p, n: (0, p, 0, 0))
    body, in_specs, args = _ordered(
        body, [qspec(0), qspec(1), qspec(2),
               blk(lambda p, n: (prev(n), p)), blk(lambda p, n: (cur(n), p)),
               blk(lambda p, n: (prev(n), PP + p)), blk(lambda p, n: (cur(n), PP + p)),
               bspec, blk(lambda p, n: (cur(n), p)), blk(lambda p, n: (cur(n), p)),
               blk(lambda p, n: (cur(n), p))],
        (q, q, q, kv, kv, kv, kv, bias, do, o, L), after)
    return pl.pallas_call(
        body, grid=(PP, NS + 1), in_specs=in_specs,
        out_specs=[pl.BlockSpec((N_GROUPS, SUPER, LANES), lambda p, n: (0, cur(n), p)),
                   blk(lambda p, n: (lag(n), p)), blk(lambda p, n: (lag(n), p)), bspec],
        out_shape=[jax.ShapeDtypeStruct((N_GROUPS, T, HD), F32), jax.ShapeDtypeStruct((T, HD), F32),
                   jax.ShapeDtypeStruct((T, HD), F32),
                   jax.ShapeDtypeStruct((N_GROUPS, H, BLK, 2 * BLK), F32)],
        scratch_shapes=[pltpu.VMEM((SUPER, LANES), F32), pltpu.VMEM((SUPER, LANES), F32)],
        compiler_params=_params(("arbitrary", "arbitrary")), name=name,
    )(*args)


class _Weights(dict):
    def __init__(self, base, fetch=None, emit=None, emit_small=None):
        super().__init__(base)
        self._fetch, self._emit, self._emit_small = fetch, emit, emit_small

    def fetch(self, group, after):
        if self._fetch is not None:
            for (key, layer), mat in self._fetch(group, after).items():
                self[key][layer] = mat

    def emit(self, group, mats):
        return None if self._emit is None else self._emit(group, mats)

    def emit_small(self, grads):
        return None if self._emit_small is None else self._emit_small(grads)


def _local_step(x, tgt, W):
    T, D = x.shape
    H = W["rel_table"].shape[1] // N_GROUPS
    HD = H * HEAD_DIM
    G = W["a_w_s"].shape[1]
    assert T % (DILATED_GROUPS[-1][1] * BLK) == 0

    tril = jnp.tril(jnp.ones((CHUNK, CHUNK), F32))
    bmap = jnp.asarray(_bucket_maps())
    bias = _band_bias(W["rel_table"], bmap, H, name="band_bias")

    saved = []
    xc, xcb = x, x.astype(BF16)
    kvb = None
    for i in range(DEPTH):
        s = {"x": xc, "xb": xcb}
        W.fetch(4 * i, xc)
        if i < N_A:
            ws_m = W["a_w_s"][i] * tril
            s["ws"] = ws_m.astype(BF16)
            s["wst"] = jnp.swapaxes(ws_m, 1, 2).astype(BF16)
            s["bst"] = W["a_b_s"][i].T
            s["zp"] = _mm(xcb, W["a_w_in"][i], out_dtype=ACT, name=f"a_in_{i}")
            s["y"] = _sgu_fwd(s["zp"], W["a_ln_g"][i], W["a_ln_b"][i], s["ws"], s["bst"], name=f"sgu_fwd_{i}")
            W.fetch(4 * i + 1, s["zp"])
            s["h"] = _mm(s["y"], W["a_w_out"][i], out_dtype=ACT, name=f"a_out_{i}")
        else:
            j = i - N_A
            if kvb is None:
                kvb = _mm(xcb, W["kv_w"][0], name="kv_proj")
            s["q"] = _mm(xcb, W["b_w_q_t"][j], tb=True, scale=HEAD_DIM ** -0.5, name=f"q_proj_{j}")
            s["o"], s["ob"], s["L"] = _attn_fwd_all(s["q"], kvb, bias, name=f"attn_fwd_{j}")
            W.fetch(4 * i + 1, s["q"])
            s["h"] = _mm(s["ob"], W["b_w_o"][j], out_dtype=ACT, name=f"o_proj_{j}")
        s["x1b"], = _add_ln_fwd(xc, s["h"], W["ln_g"][i, 0], W["ln_b"][i, 0], wide=False, name=f"ln1_fwd_{i}")
        s["x1"] = s["x1b"]
        W.fetch(4 * i + 2, s["x1"])
        s["hh"] = _mm(s["x1b"], W["ffn_w_up_t"][i], tb=True, out_dtype=ACT, name=f"ffn_up_{i}")
        s["cw"] = W["ffn_conv_w"][i]
        s["cb"] = W["ffn_conv_b"][i].reshape(1, -1)
        s["act"], s["hca"], s["hcg"] = _convgate_fwd(s["hh"], s["cw"], s["cb"], name=f"convgate_fwd_{i}")
        W.fetch(4 * i + 3, s["hh"])
        s["f"] = _mm(s["act"], W["ffn_w_down"][i], out_dtype=ACT, name=f"ffn_down_{i}")
        outs = _add_ln_fwd(s["x1"], s["f"], W["ln_g"][i, 1], W["ln_b"][i, 1], wide=i == DEPTH - 1,
                           name=f"ln2_fwd_{i}")
        xc, xcb = outs[0], outs[-1]
        saved.append(s)

    dy, lossv = _loss_grad(xc, tgt, name="loss_grad")
    loss = lossv[0, 0]

    gl = {k: [None] * DEPTH for k in ("ffn_w_up_t", "ffn_conv_w", "ffn_conv_b", "ffn_w_down", "ln_g", "ln_b")}
    ga = {k: [None] * N_A for k in ("a_w_in", "a_ln_g", "a_ln_b", "a_w_s", "a_b_s", "a_w_out")}
    gb = {k: [None] * (DEPTH - N_A) for k in ("b_w_q_t", "b_w_o")}
    mats = ("a_w_in", "a_w_out", "b_w_q_t", "b_w_o", "ffn_w_up_t", "ffn_w_down")
    dks, dvs, dbias = [], [], []
    grads = {}
    terms = [(1.0, dy)]
    tok = None
    small_keys = ("ffn_conv_w", "ffn_conv_b", "ln_g", "ln_b", "a_ln_g", "a_ln_b", "a_w_s", "a_b_s")
    for i in reversed(range(DEPTH)):
        s = saved[i]
        dp2b, dg2, db2 = _add_ln_bwd(s["x1"], s["f"], W["ln_g"][i, 1], terms, after=tok, name=f"ln2_bwd_{i}")
        dact = _mm(dp2b, W["ffn_w_down"][i], tb=True, out_dtype=ACT, name=f"ffn_down_dx_{i}")
        gl["ffn_w_down"][i] = _mm(s["act"], dp2b, ta=True, out_dtype=BF16, name=f"ffn_down_dw_{i}")
        dha, dhg, dwa, dwg, dba, dbg = _convgate_bwd(s["hh"], s["hca"], s["hcg"], dact, s["cw"],
                                                     name=f"convgate_bwd_{i}")
        dhh = (dha, dhg)
        gl["ffn_conv_w"][i] = jnp.concatenate([dwa, dwg], axis=1)
        gl["ffn_conv_b"][i] = jnp.concatenate([dba, dbg], axis=1)[0]
        dx1 = _mm(dhh, W["ffn_w_up_t"][i], out_dtype=ACT, name=f"ffn_up_dx_{i}")
        gl["ffn_w_up_t"][i] = _mm(dhh, s["x1b"], ta=True, out_dtype=BF16, name=f"ffn_up_dw_{i}")
        tok = W.emit(3 * i + 2, {("ffn_w_up_t", i): gl["ffn_w_up_t"][i], ("ffn_w_down", i): gl["ffn_w_down"][i]})
        dp1b, dg1, db1 = _add_ln_bwd(s["x"], s["h"], W["ln_g"][i, 0], [(ALPHA, dp2b), (1.0, dx1)],
                                     after=tok, name=f"ln1_bwd_{i}")
        gl["ln_g"][i] = jnp.concatenate([dg1, dg2], axis=0)
        gl["ln_b"][i] = jnp.concatenate([db1, db2], axis=0)
        terms = [(ALPHA, dp1b)]
        if i < N_A:
            dyy = _mm(dp1b, W["a_w_out"][i], tb=True, out_dtype=ACT, name=f"a_out_dx_{i}")
            ga["a_w_out"][i] = _mm(s["y"], dp1b, ta=True, out_dtype=BF16, name=f"a_out_dw_{i}")
            if i == 0:
                tok = W.emit(3 * i + 1, {("a_w_out", i): ga["a_w_out"][i]})
            dzp, dlg, dlb, dws, dbs = _sgu_bwd(s["zp"], dyy, W["a_ln_g"][i], W["a_ln_b"][i], s["ws"],
                                               s["wst"], s["bst"], after=tok, name=f"sgu_bwd_{i}")
            ga["a_ln_g"][i], ga["a_ln_b"][i], ga["a_w_s"][i] = dlg[0], dlb[0], dws
            ga["a_b_s"][i] = dbs[:, :G].T
            if i == 0:
                for dct in (gl, ga):
                    grads.update({k: jnp.stack(v) for k, v in dct.items() if k in small_keys})
                tok = W.emit_small(grads)
            ga["a_w_in"][i] = _mm(s["xb"], dzp, ta=True, out_dtype=BF16, after=tok, name=f"a_in_dw_{i}")
            out_a = {("a_w_in", i): ga["a_w_in"][i]}
            if i > 0:
                out_a[("a_w_out", i)] = ga["a_w_out"][i]
            tok = W.emit(3 * i, out_a)
            terms.append((1.0, _mm(dzp, W["a_w_in"][i], tb=True, out_dtype=ACT, after=tok, name=f"a_in_dx_{i}")))
        else:
            j = i - N_A
            do = _mm(dp1b, W["b_w_o"][j], tb=True, name=f"o_proj_dx_{j}")
            gb["b_w_o"][j] = _mm(s["ob"], dp1b, ta=True, out_dtype=BF16, name=f"o_proj_dw_{j}")
            dq, dk_j, dv_j, db_j = _attn_bwd_all(s["q"], kvb, bias, do, s["o"], s["L"], after=tok,
                                                 name=f"attn_bwd_{j}")
            dks.append((1.0, dk_j))
            dvs.append((1.0, dv_j))
            dbias.append(db_j)
            terms.append((1.0, _mm(dq, W["b_w_q_t"][j], out_dtype=ACT, name=f"q_proj_dx_{j}")))
            gb["b_w_q_t"][j] = _mm(dq, s["xb"], ta=True, out_dtype=BF16, name=f"q_proj_dw_{j}")
            out_b = {("b_w_q_t", j): gb["b_w_q_t"][j], ("b_w_o", j): gb["b_w_o"][j]}
            if i == N_A:
                dkv = jnp.concatenate([_lincomb(dks, BF16, name="dk_sum"), _lincomb(dvs, BF16, name="dv_sum")],
                                      axis=1)
                terms.append((1.0, _mm(dkv, W["kv_w"][0], tb=True, out_dtype=ACT, name="kv_proj_dx")))
                grads["kv_w"] = [_mm(s["xb"], dkv, ta=True, out_dtype=BF16, name="kv_proj_dw")]
                out_b[("kv_w", 0)] = grads["kv_w"][0]
                dbt = _lincomb([(1.0, a.reshape(-1, 2 * BLK)) for a in dbias], F32, name="dbias_sum")
                dtab = _band_bias_bwd(dbt.reshape(N_GROUPS, H, BLK, 2 * BLK), bmap, H, name="band_bias_bwd")
                grads["rel_table"] = jnp.transpose(dtab[:, :, :H], (1, 0, 2)).reshape(REL_BUCKETS, N_GROUPS * H)
            tok = W.emit(3 * i, out_b)
    grad_x = _lincomb(terms, F32, name="grad_x")
    for dct in (gl, ga, gb):
        grads.update({k: v for k, v in dct.items() if k in mats})
    return loss, grad_x, grads


def _my_index():
    return 4 * lax.axis_index("x") + 2 * lax.axis_index("y") + lax.axis_index("c")


HBM_SPEC = pl.BlockSpec(memory_space=pltpu.HBM)


def _block(ref, k, n, axis):
    off = pl.multiple_of(k * n, n)
    return ref.at[pl.ds(off, n), :] if axis == 0 else ref.at[:, pl.ds(off, n)]


SEM_SPEC =pl.BlockSpec(memory_space=pltpu.SEMAPHORE)
FLOWING = pltpu.SideEffectType.DATAFLOW_SIDE_EFFECTING


def _peers(x, y, c):
    return [(1 - x if k & 4 else x, 1 - y if k & 2 else y, 1 - c if k & 1 else c) for k in range(1, N_DEV)]


def _ends(src_ref, land_ref, peer_index, me, n, axis, gather):
    if gather:
        return src_ref, _block(land_ref, me, n, axis)
    return _block(src_ref, peer_index, n, axis), land_ref.at[me]


def _send_start(groups, gather, *, name):
    flat = [(g, j, mat, axis) for g, items in enumerate(groups) for j, (mat, axis) in enumerate(items)]
    M, G = len(flat), len(groups)
    lands, ns = [], []
    for _, _, mat, axis in flat:
        A, B = mat.shape
        if gather:
            lands.append((A * N_DEV, B) if axis == 0 else (A, B * N_DEV))
            ns.append(A if axis == 0 else B)
        else:
            lands.append((N_DEV, A // N_DEV, B) if axis == 0 else (N_DEV, A, B // N_DEV))
            ns.append(A // N_DEV if axis == 0 else B // N_DEV)

    def body(*refs):
        src_refs, land_refs, sems = refs[:M], refs[M:2 * M], refs[2 * M:2 * M + 3 * G]
        token = refs[-1]
        x, y, c = lax.axis_index("x"), lax.axis_index("y"), lax.axis_index("c")
        me = 4 * x + 2 * y + c
        for i, (g, j, _, axis) in enumerate(flat):
            for k, (px, py, pc) in enumerate(_peers(x, y, c)):
                s, d = _ends(src_refs[i], land_refs[i], 4 * px + 2 * py + pc, me, ns[i], axis, gather)
                pltpu.make_async_remote_copy(
                    src_ref=s, dst_ref=d, send_sem=sems[3 * g].at[7 * j + k], recv_sem=sems[3 * g + 1].at[7 * j + k],
                    device_id=(px, py, pc), device_id_type=MESH).start()
            s, d = _ends(src_refs[i], land_refs[i], me, me, ns[i], axis, gather)
            pltpu.make_async_copy(s, d, sems[3 * g + 2].at[j]).start()
        token[...] = jnp.zeros_like(token)

    sem_shapes = []
    for items in groups:
        sem_shapes += [pltpu.SemaphoreType.DMA((7 * len(items),))] * 2 + [pltpu.SemaphoreType.DMA((len(items),))]
    outs = pl.pallas_call(
        body, name=name,
        out_shape=(*sem_shapes, *[pltpu.HBM(m.shape, m.dtype) for _, _, m, _ in flat],
                   *[pltpu.HBM(shp, m.dtype) for shp, (_, _, m, _) in zip(lands, flat)],
                   jax.ShapeDtypeStruct((8, LANES), F32)),
        in_specs=[HBM_SPEC] * (2 * M),
        out_specs=(*[SEM_SPEC] * (3 * G), *[HBM_SPEC] * (2 * M), pl.BlockSpec(memory_space=pltpu.VMEM)),
        input_output_aliases={i: 3 * G + i for i in range(2 * M)},
        compiler_params=pltpu.CompilerParams(has_side_effects=FLOWING),
    )(*[pltpu.with_memory_space_constraint(m, pltpu.HBM) for _, _, m, _ in flat],
      *[pltpu.with_memory_space_constraint(lax.empty(shp, m.dtype), pltpu.HBM)
        for shp, (_, _, m, _) in zip(lands, flat)])
    handles = []
    for g in range(G):
        idx = [i for i, f in enumerate(flat) if f[0] == g]
        handles.append((outs[3 * g], outs[3 * g + 1], outs[3 * g + 2], [outs[3 * G + i] for i in idx],
                        [outs[3 * G + M + i] for i in idx], [flat[i][3] for i in idx]))
    return handles, outs[-1]


def _send_wait(handle, gather, after, *, name):
    send_sems, recv_sems, local_sems, mats, lands, axes = handle
    n_m = len(mats)
    ns = []
    for mat, land, axis in zip(mats, lands, axes):
        ns.append(mat.shape[axis] if gather else land.shape[1 + axis])

    def body(*refs):
        src_refs, land_refs = refs[:n_m], refs[n_m:2 * n_m]
        ssem, rsem, lsem = refs[2 * n_m:2 * n_m + 3]
        x, y, c = lax.axis_index("x"), lax.axis_index("y"), lax.axis_index("c")
        me = 4 * x + 2 * y + c
        for j in range(n_m):
            for k, (px, py, pc) in enumerate(_peers(x, y, c)):
                s, d = _ends(src_refs[j], land_refs[j], 4 * px + 2 * py + pc, me, ns[j], axes[j], gather)
                cp = pltpu.make_async_remote_copy(
                    src_ref=s, dst_ref=d, send_sem=ssem.at[7 * j + k], recv_sem=rsem.at[7 * j + k],
                    device_id=(px, py, pc), device_id_type=MESH)
                cp.wait_send()
                cp.wait_recv()
            s, d = _ends(src_refs[j], land_refs[j], me, me, ns[j], axes[j], gather)
            pltpu.make_async_copy(s, d, lsem.at[j]).wait()

    outs = pl.pallas_call(
        body, name=name,
        out_shape=(*[pltpu.HBM(m.shape, m.dtype) for m in mats], *[pltpu.HBM(l.shape, l.dtype) for l in lands]),
        in_specs=[HBM_SPEC] * (2 * n_m) + [SEM_SPEC] * 3 + [pl.BlockSpec(memory_space=pl.ANY)],
        out_specs=tuple([HBM_SPEC] * (2 * n_m)),
        input_output_aliases={i: i for i in range(2 * n_m)},
        compiler_params=pltpu.CompilerParams(has_side_effects=FLOWING),
    )(*mats, *lands, send_sems, recv_sems, local_sems, after)
    return list(outs[n_m:])


def _sum_parts(parts, *, name):
    n, R, C = parts.shape
    rb = _pick(R, 512) if R % LANES == 0 else R

    def body(p_ref, o_ref):
        acc = p_ref[0].astype(F32)
        for k in range(1, n):
            acc = acc + p_ref[k].astype(F32)
        o_ref[...] = acc

    return pl.pallas_call(
        body, grid=(R // rb,), in_specs=[pl.BlockSpec((n, rb, C), lambda i: (0, i, 0))],
        out_specs=pl.BlockSpec((rb, C), lambda i: (i, 0)),
        out_shape=jax.ShapeDtypeStruct((R, C), F32),
        compiler_params=_params(("parallel",)), name=name,
    )(parts)


def _adamw(w, m, v, parts, *, name):
    L, R, C = w.shape
    n = parts[0].shape[0]
    cap = max(16, VMEM_LIMIT // 2 // (2 * L * n * C * parts[0].dtype.itemsize))
    rb = max([r for r in range(16, min(R, cap) + 1, 16) if R % r == 0], default=R)

    def body(w_ref, m_ref, v_ref, *rest):
        p_refs = rest[:L]
        g_ref, d_ref, nm_ref, nv_ref = rest[L:]
        for l in range(L):
            @pl.when(pl.program_id(0) == l)
            def _(p_ref=p_refs[l]):
                g = p_ref[0].astype(F32)
                for k in range(1, n):
                    g = g + p_ref[k].astype(F32)
                mn = ADAM_B1 * m_ref[...] + (1.0 - ADAM_B1) * g
                vn = ADAM_B2 * v_ref[...] + (1.0 - ADAM_B2) * jnp.square(g)
                m_hat = mn / (1.0 - ADAM_B1 ** ADAM_STEP)
                v_hat = vn / (1.0 - ADAM_B2 ** ADAM_STEP)
                g_ref[...] = g
                d_ref[...] = -ADAM_LR * (m_hat / (jnp.sqrt(v_hat) + ADAM_EPS) + ADAM_WD * w_ref[...])
                nm_ref[...] = mn
                nv_ref[...] = vn

    row = pl.BlockSpec((None, rb, C), lambda l, i: (l, i, 0))
    part = lambda k: pl.BlockSpec((n, rb, C), lambda l, i: (0, jnp.where(l == k, i, 0), 0))
    return pl.pallas_call(
        body, grid=(L, R // rb), in_specs=[row, row, row] + [part(k) for k in range(L)],
        out_specs=[row] * 4, out_shape=[jax.ShapeDtypeStruct((L, R, C), F32)] * 4,
        compiler_params=_params(("arbitrary", "arbitrary")), name=name,
    )(w, m, v, *parts)


BIG = (("a_w_in", "a_w_in", 1, False), ("a_w_out", "a_w_out", 0, False), ("kv_w", "kv_w", 0, False),
       ("b_w_q", "b_w_q_t", 0, True), ("b_w_o", "b_w_o", 1, False), ("ffn_w_up", "ffn_w_up_t", 0, True),
       ("ffn_w_down", "ffn_w_down", 0, False))
SMALL_SHARDED = (("a_ln_g", 1), ("a_ln_b", 1), ("ffn_conv_w", 2), ("ln_g", 2), ("ln_b", 2))
REPLICATED = ("a_w_s", "a_b_s", "rel_table", "ffn_conv_b")


def _pack_rows(arrs, lead=0):
    lshape = arrs[0].shape[:lead]
    p = jnp.concatenate([a.reshape(*lshape, -1, LANES) for a in arrs], axis=lead)
    pad = -p.shape[lead] % 8
    return jnp.pad(p, [(0, 0)] * lead + [(0, pad), (0, 0)])


def _unpack_rows(packed, shapes, lead=0):
    lshape = packed.shape[:lead]
    out, off = [], 0
    for shp in shapes:
        r = int(np.prod(shp)) // LANES
        out.append(lax.slice_in_dim(packed, off, off + r, axis=lead).reshape(*lshape, *shp))
        off += r
    return out


def _as_mats(a, transposed):
    a = a[None] if a.ndim == 2 else a
    return jnp.swapaxes(a, 1, 2) if transposed else a


def _merge_shards(stacked, axis):
    a = jnp.moveaxis(stacked, 0, axis)
    shp = list(a.shape)
    return a.reshape(shp[:axis] + [shp[axis] * shp[axis + 1]] + shp[axis + 2:])


def _split_shards(full, axis):
    shp = list(full.shape)
    a = full.reshape(shp[:axis] + [N_DEV, shp[axis] // N_DEV] + shp[axis + 1:])
    return jnp.moveaxis(a, axis, 0)


def kernel(x, a_w_in, a_ln_g, a_ln_b, a_w_s, a_b_s, a_w_out, kv_w, b_w_q, b_w_o, rel_table, ffn_w_up, ffn_conv_w, ffn_conv_b, ffn_w_down, ln_g, ln_b, loss_target, m_a_w_in, m_a_ln_g, m_a_ln_b, m_a_w_s, m_a_b_s, m_a_w_out, m_kv_w, m_b_w_q, m_b_w_o, m_rel_table, m_ffn_w_up, m_ffn_conv_w, m_ffn_conv_b, m_ffn_w_down, m_ln_g, m_ln_b, v_a_w_in, v_a_ln_g, v_a_ln_b, v_a_w_s, v_a_b_s, v_a_w_out, v_kv_w, v_b_w_q, v_b_w_o, v_rel_table, v_ffn_w_up, v_ffn_conv_w, v_ffn_conv_b, v_ffn_w_down, v_ln_g, v_ln_b):
    names = ["a_w_in", "a_ln_g", "a_ln_b", "a_w_s", "a_b_s", "a_w_out", "kv_w", "b_w_q", "b_w_o", "rel_table",
             "ffn_w_up", "ffn_conv_w", "ffn_conv_b", "ffn_w_down", "ln_g", "ln_b"]
    w = dict(zip(names, (a_w_in, a_ln_g, a_ln_b, a_w_s, a_b_s, a_w_out, kv_w, b_w_q, b_w_o, rel_table,
                         ffn_w_up, ffn_conv_w, ffn_conv_b, ffn_w_down, ln_g, ln_b)))
    m = dict(zip(names, (m_a_w_in, m_a_ln_g, m_a_ln_b, m_a_w_s, m_a_b_s, m_a_w_out, m_kv_w, m_b_w_q, m_b_w_o,
                         m_rel_table, m_ffn_w_up, m_ffn_conv_w, m_ffn_conv_b, m_ffn_w_down, m_ln_g, m_ln_b)))
    v = dict(zip(names, (v_a_w_in, v_a_ln_g, v_a_ln_b, v_a_w_s, v_a_b_s, v_a_w_out, v_kv_w, v_b_w_q, v_b_w_o,
                         v_rel_table, v_ffn_w_up, v_ffn_conv_w, v_ffn_conv_b, v_ffn_w_down, v_ln_g, v_ln_b)))
    small_names = [n for n, _ in SMALL_SHARDED]
    small_shapes = [w[n].shape for n in small_names]
    rep_shapes = [w[n].shape for n in REPLICATED]

    axis_of = {key: axis for _, key, axis, _ in BIG}
    src = {}
    for n, key, axis, tr in BIG:
        loc = _as_mats(w[n], tr).astype(BF16)
        for l in range(loc.shape[0]):
            src[(key, l)] = loc[l]
    order = []
    for i in range(DEPTH):
        if i < N_A:
            order += [[("a_w_in", i)], [("a_w_out", i)]]
        else:
            order += [([("kv_w", 0)] if i == N_A else []) + [("b_w_q_t", i - N_A)], [("b_w_o", i - N_A)]]
        order += [[("ffn_w_up_t", i)], [("ffn_w_down", i)]]
    small_src = _pack_rows([w[n] for n in small_names])
    srows = small_src.shape[0]
    handles, _ = _send_start([[(small_src, 0)]] + [[(src[kl], axis_of[kl[0]]) for kl in grp] for grp in order],
                             True, name="gather_start")
    small_all = _send_wait(handles[0], True, x, name="gather_wait_small")[0]
    small_st = _unpack_rows(small_all.reshape(N_DEV, srows, LANES), small_shapes, lead=1)
    base = {n: w[n] for n in REPLICATED}
    for (n, ax), st in zip(SMALL_SHARDED, small_st):
        base[n] = _merge_shards(st, ax)
    for n, key, _, tr in BIG:
        base[key] = [None] * (1 if w[n].ndim == 2 else w[n].shape[0])

    def fetch(group, after):
        mats = _send_wait(handles[1 + group], True, after, name=f"gather_wait_{group}")
        return dict(zip(order[group], mats))

    sent = {}

    def emit(group, mats):
        keys = list(mats)
        hs, token = _send_start([[(mats[kl], axis_of[kl[0]]) for kl in keys]], False, name=f"exchange_start_{group}")
        sent[group] = (keys, hs[0])
        return token

    small_sent = []

    def emit_small(grads):
        small_pack = _pack_rows([_split_shards(grads[n], ax) for n, ax in SMALL_SHARDED], lead=1)
        rest = _pack_rows([grads[n] for n in REPLICATED[1:]])
        mine = jnp.concatenate([small_pack.reshape(N_DEV * srows, LANES), rest], axis=0)
        gating = grads[REPLICATED[0]].reshape(-1, LANES).astype(BF16)
        hs, token = _send_start([[(mine, 0), (gating, 0)]], True, name="small_grads_start")
        small_sent.append(hs[0])
        return token

    loss, grad_x, grads = _local_step(x[0], loss_target[0], _Weights(base, fetch, emit, emit_small))
    loss = lax.psum(loss, ("x", "y", "c"))
    out = {}

    landed = {}
    last = grad_x
    left = lambda e: min(g for g, (keys, _) in sent.items() if any(k[0] == e[1] for k in keys))
    for n, key, axis, tr in sorted(BIG, key=left, reverse=True):
        shp = w[n].shape
        for group in sorted(sent, reverse=True):
            keys, h = sent[group]
            if keys[0] not in landed and any(k[0] == key for k in keys):
                landed.update(zip(keys, _send_wait(h, False, last, name=f"exchange_wait_{group}")))
        parts = [landed[(key, l)] for l in range(1 if len(shp) == 2 else shp[0])]
        res = _adamw(_as_mats(w[n], tr), _as_mats(m[n], tr), _as_mats(v[n], tr), parts, name=f"adamw_{n}")
        out[n] = [(jnp.swapaxes(r, 1, 2) if tr else r).reshape(shp) for r in res]
        last = res[0]

    allp, allg = _send_wait(small_sent[0], True, last, name="small_grads_wait")
    gsum = _sum_parts(allp.reshape(N_DEV, -1, LANES), name="sum_small_grads")
    gating = _sum_parts(allg.reshape(N_DEV, -1, LANES), name="sum_gating_grads")
    g_small = lax.dynamic_slice_in_dim(gsum, _my_index() * srows, srows, axis=0)
    pack_sr = lambda d: jnp.concatenate([_pack_rows([d[n] for n in small_names]),
                                         _pack_rows([d[n] for n in REPLICATED])], axis=0)
    n_rest = sum(int(np.prod(s)) for s in rep_shapes[1:]) // LANES
    gs_in = jnp.concatenate([g_small, gating, gsum[N_DEV * srows:N_DEV * srows + n_rest]], axis=0)
    gs_in = jnp.pad(gs_in, ((0, pack_sr(w).shape[0] - gs_in.shape[0]), (0, 0)))[None]
    res = _adamw(pack_sr(w)[None], pack_sr(m)[None], pack_sr(v)[None], [gs_in], name="adamw_small")
    for n, vals in zip(small_names, zip(*[_unpack_rows(r[0, :srows], small_shapes) for r in res])):
        out[n] = list(vals)
    for n, vals in zip(REPLICATED, zip(*[_unpack_rows(r[0, srows:], rep_shapes) for r in res])):
        out[n] = list(vals)

    return (loss, grad_x[None], *[out[n][0] for n in names], *[out[n][1] for n in names],
            *[out[n][2] for n in names], *[out[n][3] for n in names])
```

```python
import math

import numpy as np
import jax
import jax.numpy as jnp
from jax import lax
from jax.experimental import pallas as pl
from jax.experimental.pallas import tpu as pltpu

F32 = jnp.float32
BF16 = jnp.bfloat16
ACT = jnp.bfloat16
MESH = pl.DeviceIdType.MESH

N_DEV = 8
DEPTH = 4
N_A = 2
CHUNK = 128
BLK = 128
HEAD_DIM = 64
DILATED_GROUPS = ((128, 1), (512, 4), (2048, 16))
N_GROUPS = 3
REL_BUCKETS = 32
REL_MAX_DIST = 2048
ALPHA = (2 * DEPTH) ** 0.25
LN_EPS = 1e-5
NEG = -1e30
ADAM_LR = 0.001
ADAM_B1 = 0.9
ADAM_B2 = 0.999
ADAM_EPS = 1e-08
ADAM_WD = 0.01
ADAM_STEP = 10

LANES = 128
VMEM_LIMIT = 56 * 1024 * 1024
MM_TILE_CAP = 1408
MM_VMEM_BUDGET = 46 * 1024 * 1024
INV_SQRT2 = 1.0 / math.sqrt(2.0)
INV_SQRT_2PI = 1.0 / math.sqrt(2.0 * math.pi)


def _pick(n, cap):
    best = None
    for t in range(LANES, min(n, cap) + 1, LANES):
        if n % t == 0:
            best = t
    return best if best is not None else n


def _params(sem):
    return pltpu.CompilerParams(dimension_semantics=sem, vmem_limit_bytes=VMEM_LIMIT)


def _ordered(body, in_specs, args, after):
    if after is None:
        return body, list(in_specs), tuple(args)
    return (lambda _, *refs: body(*refs)), [pl.BlockSpec(memory_space=pl.ANY), *in_specs], (after, *args)


def _gelu(x):
    return 0.5 * x * (1.0 + lax.erf(x * INV_SQRT2))


def _gelu_grad(x):
    return 0.5 * (1.0 + lax.erf(x * INV_SQRT2)) + x * jnp.exp(-0.5 * x * x) * INV_SQRT_2PI


def _mm(a, b, *, ta=False, tb=False, out_dtype=F32, scale=None, after=None, name):
    halves = isinstance(a, tuple)
    parts = 1 if halves or a.ndim == 2 else a.shape[0]
    ash = (a[0].shape[0], 2 * a[0].shape[1]) if halves else (a.shape if parts == 1
                                                               else (a.shape[1], parts * a.shape[2]))
    if ta:
        K, M = ash
    else:
        M, K = ash
    if tb:
        N, Kb = b.shape
    else:
        Kb, N = b.shape
    assert K == Kb, (ash, b.shape, ta, tb)
    split = 2 if halves else parts
    tm = _pick(M // split if ta else M, MM_TILE_CAP)
    tn = _pick(N, MM_TILE_CAP)
    kspan = K if ta or split == 1 else K // split
    abytes = (a[0] if halves else a).dtype.itemsize * (2 if halves else 1)
    fixed = 2 * tm * tn * jnp.dtype(out_dtype).itemsize + tm * tn * 4
    fits = [t for t in range(LANES, kspan + 1, LANES)
            if kspan % t == 0 and 2 * t * (tm * abytes + tn * b.dtype.itemsize) + fixed <= MM_VMEM_BUDGET]
    tk = max(fits) if fits else _pick(kspan, MM_TILE_CAP)
    nk = K // tk
    nh = (M // split // tm if ta else K // split // tk) if split > 1 else 0
    dn = (((0 if ta else 1,), (1 if tb else 0,)), ((), ()))

    def body(*refs):
        n_tail = 3 if nk > 1 else 2
        a_refs, b_ref, o_ref = refs[:-n_tail], refs[-n_tail], refs[-n_tail + 1]
        k = pl.program_id(2)

        def finish(r):
            if scale is not None:
                r = r * scale
            o_ref[...] = r.astype(out_dtype)

        def accumulate(a_ref):
            part = lax.dot_general(a_ref[...].astype(BF16), b_ref[...].astype(BF16), dn,
                                   preferred_element_type=F32)
            if nk == 1:
                finish(part)
                return
            acc_ref = refs[-1]

            @pl.when(k == 0)
            def _():
                acc_ref[...] = part

            @pl.when((k > 0) & (k < nk - 1))
            def _():
                acc_ref[...] += part

            @pl.when(k == nk - 1)
            def _():
                finish(acc_ref[...] + part)

        if halves:
            first = (pl.program_id(0) if ta else k) < nh
            pl.when(first)(lambda: accumulate(a_refs[0]))
            pl.when(jnp.logical_not(first))(lambda: accumulate(a_refs[1]))
        else:
            accumulate(a_refs[0])

    if halves and ta:
        a_specs = [pl.BlockSpec((tk, tm), lambda i, j, k: (jnp.where(i < nh, k, 0), jnp.minimum(i, nh - 1))),
                   pl.BlockSpec((tk, tm), lambda i, j, k: (jnp.where(i >= nh, k, 0), jnp.maximum(i - nh, 0)))]
    elif halves:
        a_specs = [pl.BlockSpec((tm, tk), lambda i, j, k: (i, jnp.minimum(k, nh - 1))),
                   pl.BlockSpec((tm, tk), lambda i, j, k: (i, jnp.maximum(k - nh, 0)))]
    elif parts > 1:
        a_specs = [pl.BlockSpec((None, tk, tm), lambda i, j, k: (i // nh, k, i % nh)) if ta
                   else pl.BlockSpec((None, tm, tk), lambda i, j, k: (k // nh, i, k % nh))]
    else:
        a_specs = [pl.BlockSpec((tk, tm), lambda i, j, k: (k, i)) if ta
                   else pl.BlockSpec((tm, tk), lambda i, j, k: (i, k))]
    b_spec = (pl.BlockSpec((tn, tk), lambda i, j, k: (j, k)) if tb
              else pl.BlockSpec((tk, tn), lambda i, j, k: (k, j)))
    body, in_specs, args = _ordered(body, [*a_specs, b_spec], (*(a if halves else (a,)), b), after)
    return pl.pallas_call(
        body, grid=(M // tm, N // tn, nk), in_specs=in_specs,
        out_specs=pl.BlockSpec((tm, tn), lambda i, j, k: (i, j)),
        out_shape=jax.ShapeDtypeStruct((M, N), out_dtype),
        scratch_shapes=[pltpu.VMEM((tm, tn), F32)] if nk > 1 else [],
        compiler_params=_params(("parallel", "parallel", "arbitrary")), name=name,
    )(*args)


def _add_ln_fwd(x, h, g, b, *, wide, name):
    T, D = x.shape
    rb = _pick(T, 1024)

    def body(x_ref, h_ref, g_ref, b_ref, *o_refs):
        pre = ALPHA * x_ref[...].astype(F32) + h_ref[...].astype(F32)
        mu = jnp.mean(pre, axis=1, keepdims=True)
        cen = pre - mu
        var = jnp.mean(cen * cen, axis=1, keepdims=True)
        y = cen * lax.rsqrt(var + LN_EPS) * g_ref[...] + b_ref[...]
        for o_ref in o_refs:
            o_ref[...] = y.astype(o_ref.dtype)

    row = pl.BlockSpec((rb, D), lambda i: (i, 0))
    vec = pl.BlockSpec((1, D), lambda i: (0, 0))
    dtypes = [F32, BF16] if wide else [BF16]
    return pl.pallas_call(
        body, grid=(T // rb,), in_specs=[row, row, vec, vec], out_specs=[row] * len(dtypes),
        out_shape=[jax.ShapeDtypeStruct((T, D), dt) for dt in dtypes],
        compiler_params=_params(("parallel",)), name=name,
    )(x, h, g.reshape(1, D), b.reshape(1, D))


def _add_ln_bwd(x, h, g, terms, *, after=None, name):
    T, D = x.shape
    rb = _pick(T, 1024)
    coefs = [c for c, _ in terms]
    nt = len(terms)

    def body(*refs):
        x_ref, h_ref, g_ref = refs[:3]
        t_refs = refs[3:3 + nt]
        dpb_ref, dg_ref, db_ref = refs[3 + nt:]
        dy = None
        for c, r in zip(coefs, t_refs):
            v = r[...].astype(F32) if c == 1.0 else c * r[...].astype(F32)
            dy = v if dy is None else dy + v
        pre = ALPHA * x_ref[...].astype(F32) + h_ref[...].astype(F32)
        mu = jnp.mean(pre, axis=1, keepdims=True)
        cen = pre - mu
        var = jnp.mean(cen * cen, axis=1, keepdims=True)
        rstd = lax.rsqrt(var + LN_EPS)
        xhat = cen * rstd
        dxh = dy * g_ref[...]
        m1 = jnp.mean(dxh, axis=1, keepdims=True)
        m2 = jnp.mean(dxh * xhat, axis=1, keepdims=True)
        dpre = rstd * (dxh - m1 - xhat * m2)
        dpb_ref[...] = dpre.astype(BF16)
        dg = jnp.sum(dy * xhat, axis=0, keepdims=True)
        db = jnp.sum(dy, axis=0, keepdims=True)

        @pl.when(pl.program_id(0) == 0)
        def _():
            dg_ref[...] = dg
            db_ref[...] = db

        @pl.when(pl.program_id(0) > 0)
        def _():
            dg_ref[...] += dg
            db_ref[...] += db

    row = pl.BlockSpec((rb, D), lambda i: (i, 0))
    vec = pl.BlockSpec((1, D), lambda i: (0, 0))
    body, in_specs, args = _ordered(body, [row, row, vec] + [row] * nt,
                                    (x, h, g.reshape(1, D), *[a for _, a in terms]), after)
    return pl.pallas_call(
        body, grid=(T // rb,), in_specs=in_specs,
        out_specs=[row, vec, vec],
        out_shape=[jax.ShapeDtypeStruct((T, D), BF16),
                   jax.ShapeDtypeStruct((1, D), F32), jax.ShapeDtypeStruct((1, D), F32)],
        compiler_params=_params(("arbitrary",)), name=name,
    )(*args)


def _lincomb(terms, out_dtype, *, name):
    R, C = terms[0][1].shape
    rb = _pick(R, 1024)
    coefs = [c for c, _ in terms]
    nt = len(terms)

    def body(*refs):
        acc = None
        for c, r in zip(coefs, refs[:nt]):
            v = r[...].astype(F32)
            v = v if c == 1.0 else c * v
            acc = v if acc is None else acc + v
        refs[nt][...] = acc.astype(out_dtype)

    row = pl.BlockSpec((rb, C), lambda i: (i, 0))
    return pl.pallas_call(
        body, grid=(R // rb,), in_specs=[row] * nt, out_specs=row,
        out_shape=jax.ShapeDtypeStruct((R, C), out_dtype),
        compiler_params=_params(("parallel",)), name=name,
    )(*[a for _, a in terms])


def _loss_grad(y, tgt, *, name):
    T, D = y.shape
    rb = _pick(T, 1024)

    def body(y_ref, t_ref, dy_ref, l_ref):
        err = y_ref[...] - t_ref[...]
        dy_ref[...] = err * (1.0 / D)
        part = jnp.sum(jnp.sum(err * err, axis=1, keepdims=True), axis=0, keepdims=True) * (0.5 / D)
        part = jnp.broadcast_to(part, (1, LANES))

        @pl.when(pl.program_id(0) == 0)
        def _():
            l_ref[...] = part

        @pl.when(pl.program_id(0) > 0)
        def _():
            l_ref[...] += part

    row = pl.BlockSpec((rb, D), lambda i: (i, 0))
    return pl.pallas_call(
        body, grid=(T // rb,), in_specs=[row, row],
        out_specs=[row, pl.BlockSpec((1, LANES), lambda i: (0, 0))],
        out_shape=[jax.ShapeDtypeStruct((T, D), F32), jax.ShapeDtypeStruct((1, LANES), F32)],
        compiler_params=_params(("arbitrary",)), name=name,
    )(y, tgt)


def _sgu_fwd(zp, ln_g, ln_b, ws, bst, *, name):
    T, E2 = zp.shape
    E = E2 // 2
    G = ws.shape[0]
    cg = E // G
    rb = 4 * CHUNK

    def body(z_ref, g_ref, b_ref, ws_ref, bs_ref, y_ref):
        u = _gelu(z_ref[:, :E].astype(F32))
        v = _gelu(z_ref[:, E:].astype(F32))
        mu = jnp.mean(v, axis=1, keepdims=True)
        cen = v - mu
        var = jnp.mean(cen * cen, axis=1, keepdims=True)
        vn = (cen * lax.rsqrt(var + LN_EPS) * g_ref[...] + b_ref[...]).astype(BF16)
        for ci in range(rb // CHUNK):
            rows = slice(ci * CHUNK, (ci + 1) * CHUNK)
            for gi in range(G):
                cols = slice(gi * cg, (gi + 1) * cg)
                sv = jnp.dot(ws_ref[gi], vn[rows, cols], preferred_element_type=F32)
                sv = sv + bs_ref[:, gi:gi + 1]
                y_ref[rows, cols] = (u[rows, cols] * sv).astype(BF16)

    return pl.pallas_call(
        body, grid=(T // rb,),
        in_specs=[pl.BlockSpec((rb, E2), lambda i: (i, 0)),
                  pl.BlockSpec((1, E), lambda i: (0, 0)), pl.BlockSpec((1, E), lambda i: (0, 0)),
                  pl.BlockSpec((G, CHUNK, CHUNK), lambda i: (0, 0, 0)),
                  pl.BlockSpec((CHUNK, G), lambda i: (0, 0))],
        out_specs=pl.BlockSpec((rb, E), lambda i: (i, 0)),
        out_shape=jax.ShapeDtypeStruct((T, E), BF16),
        compiler_params=_params(("parallel",)), name=name,
    )(zp, ln_g.reshape(1, E), ln_b.reshape(1, E), ws, bst)


def _sgu_bwd(zp, dy, ln_g, ln_b, ws, wst, bst, *, after=None, name):
    T, E2 = zp.shape
    E = E2 // 2
    G = ws.shape[0]
    cg = E // G
    rb = CHUNK
    nsteps = T // rb

    def body(z_ref, dy_ref, g_ref, b_ref, ws_ref, wst_ref, bs_ref,
             dz_ref, dg_ref, db_ref, dws_ref, dbs_ref, dsv_acc):
        step = pl.program_id(0)

        @pl.when(step == 0)
        def _():
            dg_ref[...] = jnp.zeros_like(dg_ref)
            db_ref[...] = jnp.zeros_like(db_ref)
            dws_ref[...] = jnp.zeros_like(dws_ref)
            dsv_acc[...] = jnp.zeros_like(dsv_acc)

        zu = z_ref[:, :E].astype(F32)
        zv = z_ref[:, E:].astype(F32)
        u = _gelu(zu)
        v = _gelu(zv)
        mu = jnp.mean(v, axis=1, keepdims=True)
        cen = v - mu
        var = jnp.mean(cen * cen, axis=1, keepdims=True)
        rstd = lax.rsqrt(var + LN_EPS)
        xhat = cen * rstd
        vn = (xhat * g_ref[...] + b_ref[...]).astype(BF16)
        dyv = dy_ref[...].astype(F32)
        dsv = dyv * u
        dsv_acc[...] += dsv
        dsvb = dsv.astype(BF16)
        tril = (lax.broadcasted_iota(jnp.int32, (CHUNK, CHUNK), 0)
                >= lax.broadcasted_iota(jnp.int32, (CHUNK, CHUNK), 1))
        du_parts = []
        dvn_parts = []
        for gi in range(G):
            cols = slice(gi * cg, (gi + 1) * cg)
            sv = jnp.dot(ws_ref[gi], vn[:, cols], preferred_element_type=F32) + bs_ref[:, gi:gi + 1]
            du_parts.append(dyv[:, cols] * sv)
            dvn_parts.append(jnp.dot(wst_ref[gi], dsvb[:, cols], preferred_element_type=F32))
            dw = lax.dot_general(dsvb[:, cols], vn[:, cols], (((1,), (1,)), ((), ())),
                                 preferred_element_type=F32)
            dws_ref[gi] += jnp.where(tril, dw, 0.0)
        du = jnp.concatenate(du_parts, axis=1)
        dvn = jnp.concatenate(dvn_parts, axis=1)
        dg_ref[...] += jnp.sum(dvn * xhat, axis=0, keepdims=True)
        db_ref[...] += jnp.sum(dvn, axis=0, keepdims=True)
        dxh = dvn * g_ref[...]
        m1 = jnp.mean(dxh, axis=1, keepdims=True)
        m2 = jnp.mean(dxh * xhat, axis=1, keepdims=True)
        dv = rstd * (dxh - m1 - xhat * m2)
        dz_ref[:, :E] = (du * _gelu_grad(zu)).astype(BF16)
        dz_ref[:, E:] = (dv * _gelu_grad(zv)).astype(BF16)

        @pl.when(step == nsteps - 1)
        def _():
            lane = lax.broadcasted_iota(jnp.int32, (CHUNK, LANES), 1)
            out = jnp.zeros((CHUNK, LANES), F32)
            for gi in range(G):
                s = jnp.sum(dsv_acc[:, gi * cg:(gi + 1) * cg], axis=1, keepdims=True)
                out = jnp.where(lane == gi, s, out)
            dbs_ref[...] = out

    vecE = pl.BlockSpec((1, E), lambda i: (0, 0))
    wspec = pl.BlockSpec((G, CHUNK, CHUNK), lambda i: (0, 0, 0))
    body, in_specs, args = _ordered(
        body, [pl.BlockSpec((rb, E2), lambda i: (i, 0)), pl.BlockSpec((rb, E), lambda i: (i, 0)),
               vecE, vecE, wspec, wspec, pl.BlockSpec((CHUNK, G), lambda i: (0, 0))],
        (zp, dy, ln_g.reshape(1, E), ln_b.reshape(1, E), ws, wst, bst), after)
    return pl.pallas_call(
        body, grid=(nsteps,), in_specs=in_specs,
        out_specs=[pl.BlockSpec((rb, E2), lambda i: (i, 0)), vecE, vecE, wspec,
                   pl.BlockSpec((CHUNK, LANES), lambda i: (0, 0))],
        out_shape=[jax.ShapeDtypeStruct((T, E2), BF16), jax.ShapeDtypeStruct((1, E), F32),
                   jax.ShapeDtypeStruct((1, E), F32), jax.ShapeDtypeStruct((G, CHUNK, CHUNK), F32),
                   jax.ShapeDtypeStruct((CHUNK, LANES), F32)],
        scratch_shapes=[pltpu.VMEM((CHUNK, E), F32)],
        compiler_params=_params(("arbitrary",)), name=name,
    )(*args)


def _shift_down(x, k, row):
    return jnp.where(row >= k, pltpu.roll(x, k, 0), 0.0)


def _shift_up(x, k, row, T):
    return jnp.where(row < T - k, pltpu.roll(x, T - k, 0), 0.0)


def _conv3(x, w_ref, b_ref, row):
    return (w_ref[0:1, :] * _shift_down(x, 2, row) + w_ref[1:2, :] * _shift_down(x, 1, row)
            + w_ref[2:3, :] * x + b_ref[...])


def _convgate_fwd(hh, cw, cb, *, name):
    T, F2 = hh.shape
    F = F2 // 2
    ns = F // LANES

    def body(a_ref, g_ref, wa_ref, wg_ref, ba_ref, bg_ref, o_ref, ca_ref, cg_ref):
        row = lax.broadcasted_iota(jnp.int32, (T, LANES), 0)
        ca = _conv3(a_ref[...].astype(F32), wa_ref, ba_ref, row)
        cgv = _conv3(g_ref[...].astype(F32), wg_ref, bg_ref, row)
        o_ref[...] = (_gelu(ca) * cgv).astype(BF16)
        ca_ref[...] = ca.astype(ACT)
        cg_ref[...] = cgv.astype(ACT)

    sa = lambda r: pl.BlockSpec((r, LANES), lambda j: (0, j))
    sg = lambda r: pl.BlockSpec((r, LANES), lambda j: (0, j + ns))
    return pl.pallas_call(
        body, grid=(ns,), in_specs=[sa(T), sg(T), sa(3), sg(3), sa(1), sg(1)],
        out_specs=[sa(T)] * 3,
        out_shape=[jax.ShapeDtypeStruct((T, F), BF16), jax.ShapeDtypeStruct((T, F), ACT),
                   jax.ShapeDtypeStruct((T, F), ACT)],
        compiler_params=_params(("parallel",)), name=name,
    )(hh, hh, cw, cw, cb, cb)


def _convgate_bwd(hh, hca, hcg, dact, cw, *, name):
    T, F2 = hh.shape
    F = F2 // 2
    ns = F // LANES

    def body(a_ref, g_ref, ca_ref, cg_ref, d_ref, wa_ref, wg_ref,
             da_ref, dg_ref, dwa_ref, dwg_ref, dba_ref, dbg_ref):
        row = lax.broadcasted_iota(jnp.int32, (T, LANES), 0)
        d = d_ref[...].astype(F32)
        ca = ca_ref[...].astype(F32)
        cgv = cg_ref[...].astype(F32)
        cdf = 0.5 * (1.0 + lax.erf(ca * INV_SQRT2))
        dca = d * cgv * (cdf + ca * jnp.exp(-0.5 * ca * ca) * INV_SQRT_2PI)
        dcg = d * (ca * cdf)
        for x_ref, w_ref, dc, dx_ref, dw_ref, db_ref in (
                (a_ref, wa_ref, dca, da_ref, dwa_ref, dba_ref),
                (g_ref, wg_ref, dcg, dg_ref, dwg_ref, dbg_ref)):
            x = x_ref[...].astype(F32)
            up1, up2 = _shift_up(dc, 1, row, T), _shift_up(dc, 2, row, T)
            dx_ref[...] = (w_ref[2:3, :] * dc + w_ref[1:2, :] * up1 + w_ref[0:1, :] * up2).astype(BF16)
            dw_ref[0:1, :] = jnp.sum(up2 * x, axis=0, keepdims=True)
            dw_ref[1:2, :] = jnp.sum(up1 * x, axis=0, keepdims=True)
            dw_ref[2:3, :] = jnp.sum(dc * x, axis=0, keepdims=True)
            db_ref[...] = jnp.sum(dc, axis=0, keepdims=True)

    sa = lambda r: pl.BlockSpec((r, LANES), lambda j: (0, j))
    sg = lambda r: pl.BlockSpec((r, LANES), lambda j: (0, j + ns))
    return pl.pallas_call(
        body, grid=(ns,), in_specs=[sa(T), sg(T), sa(T), sa(T), sa(T), sa(3), sg(3)],
        out_specs=[sa(T), sa(T), sa(3), sa(3), sa(1), sa(1)],
        out_shape=[jax.ShapeDtypeStruct((T, F), BF16), jax.ShapeDtypeStruct((T, F), BF16),
                   jax.ShapeDtypeStruct((3, F), F32), jax.ShapeDtypeStruct((3, F), F32),
                   jax.ShapeDtypeStruct((1, F), F32), jax.ShapeDtypeStruct((1, F), F32)],
        compiler_params=_params(("parallel",)), name=name,
    )(hh, hh, hca, hcg, dact, cw, cw)


def _bucket_maps():
    iq = np.arange(BLK)[:, None]
    ik = np.arange(2 * BLK)[None, :]
    delta = iq + BLK - ik
    maps = []
    for win, dil in DILATED_GROUPS:
        n = np.clip(delta, 0, None) * dil
        max_exact = REL_BUCKETS // 2
        nf = np.maximum(n, 1).astype(np.float32)
        large = max_exact + (np.log(nf / np.float32(max_exact)) / np.float32(math.log(REL_MAX_DIST / max_exact))
                             * np.float32(REL_BUCKETS - max_exact)).astype(np.int32)
        large = np.minimum(large, REL_BUCKETS - 1)
        bucket = np.where(n < max_exact, n, large)
        valid = (delta >= 0) & (delta <= win // dil)
        maps.append(np.where(valid, bucket, -1).astype(np.int32))
    return np.stack(maps)


def _band_bias(rel_table, bmap, H, *, name):
    def body(t_ref, m_ref, o_ref):
        g = pl.program_id(0)
        bm = m_ref[0]
        for h in range(H):
            acc = jnp.full((BLK, 2 * BLK), NEG, F32)
            for b in range(REL_BUCKETS):
                acc = jnp.where(bm == b, t_ref[b, g * H + h], acc)
            o_ref[0, h] = acc

    return pl.pallas_call(
        body, grid=(N_GROUPS,),
        in_specs=[pl.BlockSpec(memory_space=pltpu.SMEM),
                  pl.BlockSpec((1, BLK, 2 * BLK), lambda g: (g, 0, 0))],
        out_specs=pl.BlockSpec((1, H, BLK, 2 * BLK), lambda g: (g, 0, 0, 0)),
        out_shape=jax.ShapeDtypeStruct((N_GROUPS, H, BLK, 2 * BLK), F32),
        compiler_params=_params(("parallel",)), name=name,
    )(rel_table, bmap)


def _band_bias_bwd(dbias, bmap, H, *, name):
    def body(d_ref, m_ref, o_ref):
        bm = m_ref[0]
        rowi = lax.broadcasted_iota(jnp.int32, (REL_BUCKETS, LANES), 0)
        lane = lax.broadcasted_iota(jnp.int32, (REL_BUCKETS, LANES), 1)
        out = jnp.zeros((REL_BUCKETS, LANES), F32)
        for h in range(H):
            dv = d_ref[0, h]
            for b in range(REL_BUCKETS):
                s = jnp.sum(jnp.sum(jnp.where(bm == b, dv, 0.0), axis=1, keepdims=True),
                            axis=0, keepdims=True)
                out = jnp.where((rowi == b) & (lane == h), s, out)
        o_ref[0] = out

    return pl.pallas_call(
        body, grid=(N_GROUPS,),
        in_specs=[pl.BlockSpec((1, H, BLK, 2 * BLK), lambda g: (g, 0, 0, 0)),
                  pl.BlockSpec((1, BLK, 2 * BLK), lambda g: (g, 0, 0))],
        out_specs=pl.BlockSpec((1, REL_BUCKETS, LANES), lambda g: (g, 0, 0)),
        out_shape=jax.ShapeDtypeStruct((N_GROUPS, REL_BUCKETS, LANES), F32),
        compiler_params=_params(("parallel",)), name=name,
    )(dbias, bmap)


def _head_masks():
    lane = lax.broadcasted_iota(jnp.int32, (BLK, LANES), 1)
    return (lane < HEAD_DIM, lane >= HEAD_DIM)


SUPER =DILATED_GROUPS[-1][1] * BLK


def _band_rows(it, d):
    r, j = it % d, it // d
    if d == 1:
        at = lambda blk: pl.ds(pl.multiple_of(blk * BLK, BLK), BLK)
    else:
        at = lambda blk: pl.ds(r + d * BLK * blk, BLK, stride=d)
    return at(j), at(jnp.maximum(j - 1, 0))


def _stack_heads(x, hm):
    zero = jnp.zeros_like(x)
    return jnp.concatenate([jnp.where(hm[0], x, zero), jnp.where(hm[1], x, zero)], axis=0)


def _band_loops(step, d, unroll):
    n_it = SUPER // BLK

    def run(lo, hi, inside):
        if hi > lo:
            def body(it, carry):
                step(it, inside)
                return carry
            lax.fori_loop(lo, hi, body, 0, unroll=max(u for u in range(1, unroll + 1) if (hi - lo) % u == 0))

    run(0, d, False)
    run(d, n_it, True)


def _last_rows(it, d):
    m = SUPER // (d * BLK)
    if d == 1:
        return pl.ds((m - 1) * BLK, BLK)
    return pl.ds(it % d + d * BLK * (m - 1), BLK, stride=d)


def _attn_fwd_all(q, kv, bias, *, name):
    T = q.shape[0]
    HD = kv.shape[1] // 2
    PP = HD // LANES
    NS = T // SUPER

    def body(q0, q1, q2, kp_ref, kc_ref, vp_ref, vc_ref, b_ref, o_ref, ob_ref, l_ref, og, lg):
        n = pl.program_id(1)
        col = lax.broadcasted_iota(jnp.int32, (2 * BLK, 2 * BLK), 1)
        hm = _head_masks()
        for g, (q_ref, (_, d)) in enumerate(zip((q0, q1, q2), DILATED_GROUPS)):
            def step(it, inside, g=g, q_ref=q_ref, d=d):
                cur, prv = _band_rows(it, d)
                qp = q_ref[cur, :].astype(BF16)
                if inside:
                    kprev, vprev = kc_ref[prv, :], vc_ref[prv, :]
                else:
                    last = _last_rows(it, d)
                    kprev, vprev = kp_ref[last, :], vp_ref[last, :]
                kc = jnp.concatenate([kprev.astype(BF16), kc_ref[cur, :].astype(BF16)], axis=0)
                vc = jnp.concatenate([vprev.astype(BF16), vc_ref[cur, :].astype(BF16)], axis=0)
                s = lax.dot_general(_stack_heads(qp, hm), kc, (((1,), (1,)), ((), ())),
                                    preferred_element_type=F32)
                s = s + b_ref[g].reshape(2 * BLK, 2 * BLK)
                if not inside:
                    s = jnp.where((n == 0) & (col < BLK), NEG, s)
                mx = jnp.max(s, axis=1, keepdims=True)
                e = jnp.exp(s - mx)
                den = jnp.sum(e, axis=1, keepdims=True)
                out = jnp.dot((e / den).astype(BF16), vc, preferred_element_type=F32)
                lse = mx + jnp.log(den)
                og.at[g][cur, :] = jnp.where(hm[0], out[:BLK], out[BLK:])
                lg.at[g][cur, :] = jnp.where(hm[0], lse[:BLK], lse[BLK:])

            _band_loops(step, d, 8)
        la, lb, lc = lg[0], lg[1], lg[2]
        mx = jnp.maximum(jnp.maximum(la, lb), lc)
        L = mx + jnp.log(jnp.exp(la - mx) + jnp.exp(lb - mx) + jnp.exp(lc - mx))
        o = jnp.exp(la - L) * og[0] + jnp.exp(lb - L) * og[1] + jnp.exp(lc - L) * og[2]
        o_ref[...] = o
        ob_ref[...] = o.astype(BF16)
        l_ref[...] = L

    blk = lambda f: pl.BlockSpec((SUPER, LANES), f)
    prev = lambda n: jnp.maximum(n - 1, 0)
    qspec = lambda g: blk(lambda p, n: (n, g * PP + p))
    return pl.pallas_call(
        body, grid=(PP, NS),
        in_specs=[qspec(0), qspec(1), qspec(2),
                  blk(lambda p, n: (prev(n), p)), blk(lambda p, n: (n, p)),
                  blk(lambda p, n: (prev(n), PP + p)), blk(lambda p, n: (n, PP + p)),
                  pl.BlockSpec((N_GROUPS, 2, BLK, 2 * BLK), lambda p, n: (0, p, 0, 0))],
        out_specs=[blk(lambda p, n: (n, p))] * 3,
        out_shape=[jax.ShapeDtypeStruct((T, HD), F32), jax.ShapeDtypeStruct((T, HD), BF16),
                   jax.ShapeDtypeStruct((T, HD), F32)],
        scratch_shapes=[pltpu.VMEM((N_GROUPS, SUPER, LANES), F32), pltpu.VMEM((N_GROUPS, SUPER, LANES), F32)],
        compiler_params=_params(("parallel", "parallel")), name=name,
    )(q, q, q, kv, kv, kv, kv, bias)


def _attn_bwd_all(q, kv, bias, do, o, L, *, after=None, name):
    T = q.shape[0]
    HD = kv.shape[1] // 2
    PP = HD // LANES
    H = HD // HEAD_DIM
    NS = T // SUPER

    def body(q0, q1, q2, kp_ref, kc_ref, vp_ref, vc_ref, b_ref, do_ref, o_ref, L_ref,
             dq_ref, dk_ref, dv_ref, db_ref, ck_ref, cv_ref):
        n = pl.program_id(1)

        @pl.when(n == 0)
        def _():
            db_ref[...] = jnp.zeros_like(db_ref)
            ck_ref[...] = jnp.zeros_like(ck_ref)
            cv_ref[...] = jnp.zeros_like(cv_ref)

        dk_ref[...] = ck_ref[...]
        dv_ref[...] = cv_ref[...]
        ck_ref[...] = jnp.zeros_like(ck_ref)
        cv_ref[...] = jnp.zeros_like(cv_ref)

        @pl.when(n < NS)
        def _():
            col = lax.broadcasted_iota(jnp.int32, (2 * BLK, 2 * BLK), 1)
            hm = _head_masks()
            for g, (q_ref, (_, d)) in enumerate(zip((q0, q1, q2), DILATED_GROUPS)):
                def step(it, inside, g=g, q_ref=q_ref, d=d):
                    cur, prv = _band_rows(it, d)
                    last = _last_rows(it, d)
                    qp = q_ref[cur, :].astype(BF16)
                    if inside:
                        kprev, vprev = kc_ref[prv, :], vc_ref[prv, :]
                    else:
                        kprev, vprev = kp_ref[last, :], vp_ref[last, :]
                    kc = jnp.concatenate([kprev.astype(BF16), kc_ref[cur, :].astype(BF16)], axis=0)
                    vc = jnp.concatenate([vprev.astype(BF16), vc_ref[cur, :].astype(BF16)], axis=0)
                    dop = do_ref[cur, :]
                    prod = dop * o_ref[cur, :]
                    Lp = L_ref[cur, :]
                    qs = _stack_heads(qp, hm)
                    dos = _stack_heads(dop.astype(BF16), hm)
                    lse = jnp.concatenate([Lp[:, 0:1], Lp[:, HEAD_DIM:HEAD_DIM + 1]], axis=0)
                    delta = jnp.concatenate([jnp.sum(jnp.where(hm[0], prod, 0.0), axis=1, keepdims=True),
                                             jnp.sum(jnp.where(hm[1], prod, 0.0), axis=1, keepdims=True)], axis=0)
                    s = lax.dot_general(qs, kc, (((1,), (1,)), ((), ())), preferred_element_type=F32)
                    s = s + b_ref[g].reshape(2 * BLK, 2 * BLK)
                    if not inside:
                        s = jnp.where((n == 0) & (col < BLK), NEG, s)
                    pr = jnp.exp(s - lse)
                    dp = lax.dot_general(dos, vc, (((1,), (1,)), ((), ())), preferred_element_type=F32)
                    ds = pr * (dp - delta)
                    db_ref[g] += ds.reshape(2, BLK, 2 * BLK)
                    dsb = ds.astype(BF16)
                    dqs = jnp.dot(dsb, kc, preferred_element_type=F32)
                    dkc = lax.dot_general(dsb, qs, (((0,), (0,)), ((), ())), preferred_element_type=F32)
                    dvc = lax.dot_general(pr.astype(BF16), dos, (((0,), (0,)), ((), ())),
                                          preferred_element_type=F32)
                    dq_ref.at[g][cur, :] = jnp.where(hm[0], dqs[:BLK], dqs[BLK:]) * (HEAD_DIM ** -0.5)
                    ck_ref[cur, :] += dkc[BLK:]
                    cv_ref[cur, :] += dvc[BLK:]
                    if inside:
                        ck_ref[prv, :] += dkc[:BLK]
                        cv_ref[prv, :] += dvc[:BLK]
                    else:
                        dk_ref[last, :] += dkc[:BLK]
                        dv_ref[last, :] += dvc[:BLK]

                _band_loops(step, d, 4)

    blk = lambda f: pl.BlockSpec((SUPER, LANES), f)
    cur = lambda n: jnp.minimum(n, NS - 1)
    prev = lambda n: jnp.maximum(jnp.minimum(n, NS - 1) - 1, 0)
    lag = lambda n: jnp.maximum(n - 1, 0)
    qspec = lambda g: blk(lambda p, n: (cur(n), g * PP + p))
    bspec = pl.BlockSpec((N_GROUPS, 2, BLK, 2 * BLK), lambda p, n: (0, p, 0, 0))
    body, in_specs, args = _ordered(
        body, [qspec(0), qspec(1), qspec(2),
               blk(lambda p, n: (prev(n), p)), blk(lambda p, n: (cur(n), p)),
               blk(lambda p, n: (prev(n), PP + p)), blk(lambda p, n: (cur(n), PP + p)),
               bspec, blk(lambda p, n: (cur(n), p)), blk(lambda p, n: (cur(n), p)),
               blk(lambda p, n: (cur(n), p))],
        (q, q, q, kv, kv, kv, kv, bias, do, o, L), after)
    return pl.pallas_call(
        body, grid=(PP, NS + 1), in_specs=in_specs,
        out_specs=[pl.BlockSpec((N_GROUPS, SUPER, LANES), lambda p, n: (0, cur(n), p)),
                   blk(lambda p, n: (lag(n), p)), blk(lambda p, n: (lag(n), p)), bspec],
        out_shape=[jax.ShapeDtypeStruct((N_GROUPS, T, HD), F32), jax.ShapeDtypeStruct((T, HD), F32),
                   jax.ShapeDtypeStruct((T, HD), F32),
                   jax.ShapeDtypeStruct((N_GROUPS, H, BLK, 2 * BLK), F32)],
        scratch_shapes=[pltpu.VMEM((SUPER, LANES), F32), pltpu.VMEM((SUPER, LANES), F32)],
        compiler_params=_params(("arbitrary", "arbitrary")), name=name,
    )(*args)


class _Weights(dict):
    def __init__(self, base, fetch=None, emit=None, emit_small=None):
        super().__init__(base)
        self._fetch, self._emit, self._emit_small = fetch, emit, emit_small

    def fetch(self, group, after):
        if self._fetch is not None:
            for (key, layer), mat in self._fetch(group, after).items():
                self[key][layer] = mat

    def emit(self, group, mats):
        return None if self._emit is None else self._emit(group, mats)

    def emit_small(self, grads):
        return None if self._emit_small is None else self._emit_small(grads)


def _local_step(x, tgt, W):
    T, D = x.shape
    H = W["rel_table"].shape[1] // N_GROUPS
    HD = H * HEAD_DIM
    G = W["a_w_s"].shape[1]
    assert T % (DILATED_GROUPS[-1][1] * BLK) == 0

    tril = jnp.tril(jnp.ones((CHUNK, CHUNK), F32))
    bmap = jnp.asarray(_bucket_maps())
    bias = _band_bias(W["rel_table"], bmap, H, name="band_bias")

    saved = []
    xc, xcb = x, x.astype(BF16)
    kvb = None
    for i in range(DEPTH):
        s = {"x": xc, "xb": xcb}
        W.fetch(4 * i, xc)
        if i < N_A:
            ws_m = W["a_w_s"][i] * tril
            s["ws"] = ws_m.astype(BF16)
            s["wst"] = jnp.swapaxes(ws_m, 1, 2).astype(BF16)
            s["bst"] = W["a_b_s"][i].T
            s["zp"] = _mm(xcb, W["a_w_in"][i], out_dtype=ACT, name=f"a_in_{i}")
            s["y"] = _sgu_fwd(s["zp"], W["a_ln_g"][i], W["a_ln_b"][i], s["ws"], s["bst"], name=f"sgu_fwd_{i}")
            W.fetch(4 * i + 1, s["zp"])
            s["h"] = _mm(s["y"], W["a_w_out"][i], out_dtype=ACT, name=f"a_out_{i}")
        else:
            j = i - N_A
            if kvb is None:
                kvb = _mm(xcb, W["kv_w"][0], name="kv_proj")
            s["q"] = _mm(xcb, W["b_w_q_t"][j], tb=True, scale=HEAD_DIM ** -0.5, name=f"q_proj_{j}")
            s["o"], s["ob"], s["L"] = _attn_fwd_all(s["q"], kvb, bias, name=f"attn_fwd_{j}")
            W.fetch(4 * i + 1, s["q"])
            s["h"] = _mm(s["ob"], W["b_w_o"][j], out_dtype=ACT, name=f"o_proj_{j}")
        s["x1b"], = _add_ln_fwd(xc, s["h"], W["ln_g"][i, 0], W["ln_b"][i, 0], wide=False, name=f"ln1_fwd_{i}")
        s["x1"] = s["x1b"]
        W.fetch(4 * i + 2, s["x1"])
        s["hh"] = _mm(s["x1b"], W["ffn_w_up_t"][i], tb=True, out_dtype=ACT, name=f"ffn_up_{i}")
        s["cw"] = W["ffn_conv_w"][i]
        s["cb"] = W["ffn_conv_b"][i].reshape(1, -1)
        s["act"], s["hca"], s["hcg"] = _convgate_fwd(s["hh"], s["cw"], s["cb"], name=f"convgate_fwd_{i}")
        W.fetch(4 * i + 3, s["hh"])
        s["f"] = _mm(s["act"], W["ffn_w_down"][i], out_dtype=ACT, name=f"ffn_down_{i}")
        outs = _add_ln_fwd(s["x1"], s["f"], W["ln_g"][i, 1], W["ln_b"][i, 1], wide=i == DEPTH - 1,
                           name=f"ln2_fwd_{i}")
        xc, xcb = outs[0], outs[-1]
        saved.append(s)

    dy, lossv = _loss_grad(xc, tgt, name="loss_grad")
    loss = lossv[0, 0]

    gl = {k: [None] * DEPTH for k in ("ffn_w_up_t", "ffn_conv_w", "ffn_conv_b", "ffn_w_down", "ln_g", "ln_b")}
    ga = {k: [None] * N_A for k in ("a_w_in", "a_ln_g", "a_ln_b", "a_w_s", "a_b_s", "a_w_out")}
    gb = {k: [None] * (DEPTH - N_A) for k in ("b_w_q_t", "b_w_o")}
    mats = ("a_w_in", "a_w_out", "b_w_q_t", "b_w_o", "ffn_w_up_t", "ffn_w_down")
    dks, dvs, dbias = [], [], []
    grads = {}
    terms = [(1.0, dy)]
    tok = None
    small_keys = ("ffn_conv_w", "ffn_conv_b", "ln_g", "ln_b", "a_ln_g", "a_ln_b", "a_w_s", "a_b_s")
    for i in reversed(range(DEPTH)):
        s = saved[i]
        dp2b, dg2, db2 = _add_ln_bwd(s["x1"], s["f"], W["ln_g"][i, 1], terms, after=tok, name=f"ln2_bwd_{i}")
        dact = _mm(dp2b, W["ffn_w_down"][i], tb=True, out_dtype=ACT, name=f"ffn_down_dx_{i}")
        gl["ffn_w_down"][i] = _mm(s["act"], dp2b, ta=True, out_dtype=BF16, name=f"ffn_down_dw_{i}")
        dha, dhg, dwa, dwg, dba, dbg = _convgate_bwd(s["hh"], s["hca"], s["hcg"], dact, s["cw"],
                                                     name=f"convgate_bwd_{i}")
        dhh = (dha, dhg)
        gl["ffn_conv_w"][i] = jnp.concatenate([dwa, dwg], axis=1)
        gl["ffn_conv_b"][i] = jnp.concatenate([dba, dbg], axis=1)[0]
        dx1 = _mm(dhh, W["ffn_w_up_t"][i], out_dtype=ACT, name=f"ffn_up_dx_{i}")
        gl["ffn_w_up_t"][i] = _mm(dhh, s["x1b"], ta=True, out_dtype=BF16, name=f"ffn_up_dw_{i}")
        out_f = {("ffn_w_up_t", i): gl["ffn_w_up_t"][i], ("ffn_w_down", i): gl["ffn_w_down"][i]}
        if i == 0:
            tok = W.emit(3 * i + 2, out_f)
        dp1b, dg1, db1 = _add_ln_bwd(s["x"], s["h"], W["ln_g"][i, 0], [(ALPHA, dp2b), (1.0, dx1)],
                                     after=tok, name=f"ln1_bwd_{i}")
        gl["ln_g"][i] = jnp.concatenate([dg1, dg2], axis=0)
        gl["ln_b"][i] = jnp.concatenate([db1, db2], axis=0)
        terms = [(ALPHA, dp1b)]
        if i < N_A:
            dyy = _mm(dp1b, W["a_w_out"][i], tb=True, out_dtype=ACT, name=f"a_out_dx_{i}")
            ga["a_w_out"][i] = _mm(s["y"], dp1b, ta=True, out_dtype=BF16, name=f"a_out_dw_{i}")
            if i == 0:
                tok = W.emit(3 * i + 1, {("a_w_out", i): ga["a_w_out"][i]})
            dzp, dlg, dlb, dws, dbs = _sgu_bwd(s["zp"], dyy, W["a_ln_g"][i], W["a_ln_b"][i], s["ws"],
                                               s["wst"], s["bst"], after=tok, name=f"sgu_bwd_{i}")
            ga["a_ln_g"][i], ga["a_ln_b"][i], ga["a_w_s"][i] = dlg[0], dlb[0], dws
            ga["a_b_s"][i] = dbs[:, :G].T
            if i == 0:
                for dct in (gl, ga):
                    grads.update({k: jnp.stack(v) for k, v in dct.items() if k in small_keys})
                tok = W.emit_small(grads)
            ga["a_w_in"][i] = _mm(s["xb"], dzp, ta=True, out_dtype=BF16, after=tok, name=f"a_in_dw_{i}")
            out_a = {("a_w_in", i): ga["a_w_in"][i]}
            if i > 0:
                out_a[("a_w_out", i)] = ga["a_w_out"][i]
                out_a.update(out_f)
            tok = W.emit(3 * i, out_a)
            terms.append((1.0, _mm(dzp, W["a_w_in"][i], tb=True, out_dtype=ACT, after=tok, name=f"a_in_dx_{i}")))
        else:
            j = i - N_A
            do = _mm(dp1b, W["b_w_o"][j], tb=True, name=f"o_proj_dx_{j}")
            gb["b_w_o"][j] = _mm(s["ob"], dp1b, ta=True, out_dtype=BF16, name=f"o_proj_dw_{j}")
            dq, dk_j, dv_j, db_j = _attn_bwd_all(s["q"], kvb, bias, do, s["o"], s["L"], after=tok,
                                                 name=f"attn_bwd_{j}")
            dks.append((1.0, dk_j))
            dvs.append((1.0, dv_j))
            dbias.append(db_j)
            terms.append((1.0, _mm(dq, W["b_w_q_t"][j], out_dtype=ACT, name=f"q_proj_dx_{j}")))
            gb["b_w_q_t"][j] = _mm(dq, s["xb"], ta=True, out_dtype=BF16, name=f"q_proj_dw_{j}")
            out_b = {("b_w_q_t", j): gb["b_w_q_t"][j], ("b_w_o", j): gb["b_w_o"][j], **out_f}
            if i == N_A:
                dkv = jnp.concatenate([_lincomb(dks, BF16, name="dk_sum"), _lincomb(dvs, BF16, name="dv_sum")],
                                      axis=1)
                terms.append((1.0, _mm(dkv, W["kv_w"][0], tb=True, out_dtype=ACT, name="kv_proj_dx")))
                grads["kv_w"] = [_mm(s["xb"], dkv, ta=True, out_dtype=BF16, name="kv_proj_dw")]
                out_b[("kv_w", 0)] = grads["kv_w"][0]
                dbt = _lincomb([(1.0, a.reshape(-1, 2 * BLK)) for a in dbias], F32, name="dbias_sum")
                dtab = _band_bias_bwd(dbt.reshape(N_GROUPS, H, BLK, 2 * BLK), bmap, H, name="band_bias_bwd")
                grads["rel_table"] = jnp.transpose(dtab[:, :, :H], (1, 0, 2)).reshape(REL_BUCKETS, N_GROUPS * H)
            tok = W.emit(3 * i, out_b)
    grad_x = _lincomb(terms, F32, name="grad_x")
    for dct in (gl, ga, gb):
        grads.update({k: v for k, v in dct.items() if k in mats})
    return loss, grad_x, grads


def _my_index():
    return 4 * lax.axis_index("x") + 2 * lax.axis_index("y") + lax.axis_index("c")


HBM_SPEC = pl.BlockSpec(memory_space=pltpu.HBM)


def _block(ref, k, n, axis):
    off = pl.multiple_of(k * n, n)
    return ref.at[pl.ds(off, n), :] if axis == 0 else ref.at[:, pl.ds(off, n)]


SEM_SPEC =pl.BlockSpec(memory_space=pltpu.SEMAPHORE)
FLOWING = pltpu.SideEffectType.DATAFLOW_SIDE_EFFECTING


def _peers(x, y, c):
    return [(1 - x if k & 4 else x, 1 - y if k & 2 else y, 1 - c if k & 1 else c) for k in range(1, N_DEV)]


def _ends(src_ref, land_ref, peer_index, me, n, axis, gather):
    if gather:
        return src_ref, _block(land_ref, me, n, axis)
    return _block(src_ref, peer_index, n, axis), land_ref.at[me]


def _send_start(groups, gather, *, name):
    flat = [(g, j, mat, axis) for g, items in enumerate(groups) for j, (mat, axis) in enumerate(items)]
    M, G = len(flat), len(groups)
    lands, ns = [], []
    for _, _, mat, axis in flat:
        A, B = mat.shape
        if gather:
            lands.append((A * N_DEV, B) if axis == 0 else (A, B * N_DEV))
            ns.append(A if axis == 0 else B)
        else:
            lands.append((N_DEV, A // N_DEV, B) if axis == 0 else (N_DEV, A, B // N_DEV))
            ns.append(A // N_DEV if axis == 0 else B // N_DEV)

    def body(*refs):
        src_refs, land_refs, sems = refs[:M], refs[M:2 * M], refs[2 * M:2 * M + 3 * G]
        token = refs[-1]
        x, y, c = lax.axis_index("x"), lax.axis_index("y"), lax.axis_index("c")
        me = 4 * x + 2 * y + c
        for i, (g, j, _, axis) in enumerate(flat):
            for k, (px, py, pc) in enumerate(_peers(x, y, c)):
                s, d = _ends(src_refs[i], land_refs[i], 4 * px + 2 * py + pc, me, ns[i], axis, gather)
                pltpu.make_async_remote_copy(
                    src_ref=s, dst_ref=d, send_sem=sems[3 * g].at[7 * j + k], recv_sem=sems[3 * g + 1].at[7 * j + k],
                    device_id=(px, py, pc), device_id_type=MESH).start()
            s, d = _ends(src_refs[i], land_refs[i], me, me, ns[i], axis, gather)
            pltpu.make_async_copy(s, d, sems[3 * g + 2].at[j]).start()
        token[...] = jnp.zeros_like(token)

    sem_shapes = []
    for items in groups:
        sem_shapes += [pltpu.SemaphoreType.DMA((7 * len(items),))] * 2 + [pltpu.SemaphoreType.DMA((len(items),))]
    outs = pl.pallas_call(
        body, name=name,
        out_shape=(*sem_shapes, *[pltpu.HBM(m.shape, m.dtype) for _, _, m, _ in flat],
                   *[pltpu.HBM(shp, m.dtype) for shp, (_, _, m, _) in zip(lands, flat)],
                   jax.ShapeDtypeStruct((8, LANES), F32)),
        in_specs=[HBM_SPEC] * (2 * M),
        out_specs=(*[SEM_SPEC] * (3 * G), *[HBM_SPEC] * (2 * M), pl.BlockSpec(memory_space=pltpu.VMEM)),
        input_output_aliases={i: 3 * G + i for i in range(2 * M)},
        compiler_params=pltpu.CompilerParams(has_side_effects=FLOWING),
    )(*[pltpu.with_memory_space_constraint(m, pltpu.HBM) for _, _, m, _ in flat],
      *[pltpu.with_memory_space_constraint(lax.empty(shp, m.dtype), pltpu.HBM)
        for shp, (_, _, m, _) in zip(lands, flat)])
    handles = []
    for g in range(G):
        idx = [i for i, f in enumerate(flat) if f[0] == g]
        handles.append((outs[3 * g], outs[3 * g + 1], outs[3 * g + 2], [outs[3 * G + i] for i in idx],
                        [outs[3 * G + M + i] for i in idx], [flat[i][3] for i in idx]))
    return handles, outs[-1]


def _send_wait(handle, gather, after, *, name):
    send_sems, recv_sems, local_sems, mats, lands, axes = handle
    n_m = len(mats)
    ns = []
    for mat, land, axis in zip(mats, lands, axes):
        ns.append(mat.shape[axis] if gather else land.shape[1 + axis])

    def body(*refs):
        src_refs, land_refs = refs[:n_m], refs[n_m:2 * n_m]
        ssem, rsem, lsem = refs[2 * n_m:2 * n_m + 3]
        x, y, c = lax.axis_index("x"), lax.axis_index("y"), lax.axis_index("c")
        me = 4 * x + 2 * y + c
        for j in range(n_m):
            for k, (px, py, pc) in enumerate(_peers(x, y, c)):
                s, d = _ends(src_refs[j], land_refs[j], 4 * px + 2 * py + pc, me, ns[j], axes[j], gather)
                cp = pltpu.make_async_remote_copy(
                    src_ref=s, dst_ref=d, send_sem=ssem.at[7 * j + k], recv_sem=rsem.at[7 * j + k],
                    device_id=(px, py, pc), device_id_type=MESH)
                cp.wait_send()
                cp.wait_recv()
            s, d = _ends(src_refs[j], land_refs[j], me, me, ns[j], axes[j], gather)
            pltpu.make_async_copy(s, d, lsem.at[j]).wait()

    outs = pl.pallas_call(
        body, name=name,
        out_shape=(*[pltpu.HBM(m.shape, m.dtype) for m in mats], *[pltpu.HBM(l.shape, l.dtype) for l in lands]),
        in_specs=[HBM_SPEC] * (2 * n_m) + [SEM_SPEC] * 3 + [pl.BlockSpec(memory_space=pl.ANY)],
        out_specs=tuple([HBM_SPEC] * (2 * n_m)),
        input_output_aliases={i: i for i in range(2 * n_m)},
        compiler_params=pltpu.CompilerParams(has_side_effects=FLOWING),
    )(*mats, *lands, send_sems, recv_sems, local_sems, after)
    return list(outs[n_m:])


def _sum_parts(parts, *, name):
    n, R, C = parts.shape
    rb = _pick(R, 512) if R % LANES == 0 else R

    def body(p_ref, o_ref):
        acc = p_ref[0].astype(F32)
        for k in range(1, n):
            acc = acc + p_ref[k].astype(F32)
        o_ref[...] = acc

    return pl.pallas_call(
        body, grid=(R // rb,), in_specs=[pl.BlockSpec((n, rb, C), lambda i: (0, i, 0))],
        out_specs=pl.BlockSpec((rb, C), lambda i: (i, 0)),
        out_shape=jax.ShapeDtypeStruct((R, C), F32),
        compiler_params=_params(("parallel",)), name=name,
    )(parts)


def _adamw(w, m, v, parts, *, name):
    L, R, C = w.shape
    n = parts[0].shape[0]
    cap = max(16, VMEM_LIMIT // 2 // (2 * L * n * C * parts[0].dtype.itemsize))
    rb = max([r for r in range(16, min(R, cap) + 1, 16) if R % r == 0], default=R)

    def body(w_ref, m_ref, v_ref, *rest):
        p_refs = rest[:L]
        g_ref, d_ref, nm_ref, nv_ref = rest[L:]
        for l in range(L):
            @pl.when(pl.program_id(0) == l)
            def _(p_ref=p_refs[l]):
                g = p_ref[0].astype(F32)
                for k in range(1, n):
                    g = g + p_ref[k].astype(F32)
                mn = ADAM_B1 * m_ref[...] + (1.0 - ADAM_B1) * g
                vn = ADAM_B2 * v_ref[...] + (1.0 - ADAM_B2) * jnp.square(g)
                m_hat = mn / (1.0 - ADAM_B1 ** ADAM_STEP)
                v_hat = vn / (1.0 - ADAM_B2 ** ADAM_STEP)
                g_ref[...] = g
                d_ref[...] = -ADAM_LR * (m_hat / (jnp.sqrt(v_hat) + ADAM_EPS) + ADAM_WD * w_ref[...])
                nm_ref[...] = mn
                nv_ref[...] = vn

    row = pl.BlockSpec((None, rb, C), lambda l, i: (l, i, 0))
    part = lambda k: pl.BlockSpec((n, rb, C), lambda l, i: (0, jnp.where(l == k, i, 0), 0))
    return pl.pallas_call(
        body, grid=(L, R // rb), in_specs=[row, row, row] + [part(k) for k in range(L)],
        out_specs=[row] * 4, out_shape=[jax.ShapeDtypeStruct((L, R, C), F32)] * 4,
        compiler_params=_params(("arbitrary", "arbitrary")), name=name,
    )(w, m, v, *parts)


BIG = (("a_w_in", "a_w_in", 1, False), ("a_w_out", "a_w_out", 0, False), ("kv_w", "kv_w", 0, False),
       ("b_w_q", "b_w_q_t", 0, True), ("b_w_o", "b_w_o", 1, False), ("ffn_w_up", "ffn_w_up_t", 0, True),
       ("ffn_w_down", "ffn_w_down", 0, False))
SMALL_SHARDED = (("a_ln_g", 1), ("a_ln_b", 1), ("ffn_conv_w", 2), ("ln_g", 2), ("ln_b", 2))
REPLICATED = ("a_w_s", "a_b_s", "rel_table", "ffn_conv_b")


def _pack_rows(arrs, lead=0):
    lshape = arrs[0].shape[:lead]
    p = jnp.concatenate([a.reshape(*lshape, -1, LANES) for a in arrs], axis=lead)
    pad = -p.shape[lead] % 8
    return jnp.pad(p, [(0, 0)] * lead + [(0, pad), (0, 0)])


def _unpack_rows(packed, shapes, lead=0):
    lshape = packed.shape[:lead]
    out, off = [], 0
    for shp in shapes:
        r = int(np.prod(shp)) // LANES
        out.append(lax.slice_in_dim(packed, off, off + r, axis=lead).reshape(*lshape, *shp))
        off += r
    return out


def _as_mats(a, transposed):
    a = a[None] if a.ndim == 2 else a
    return jnp.swapaxes(a, 1, 2) if transposed else a


def _merge_shards(stacked, axis):
    a = jnp.moveaxis(stacked, 0, axis)
    shp = list(a.shape)
    return a.reshape(shp[:axis] + [shp[axis] * shp[axis + 1]] + shp[axis + 2:])


def _split_shards(full, axis):
    shp = list(full.shape)
    a = full.reshape(shp[:axis] + [N_DEV, shp[axis] // N_DEV] + shp[axis + 1:])
    return jnp.moveaxis(a, axis, 0)


def kernel(x, a_w_in, a_ln_g, a_ln_b, a_w_s, a_b_s, a_w_out, kv_w, b_w_q, b_w_o, rel_table, ffn_w_up, ffn_conv_w, ffn_conv_b, ffn_w_down, ln_g, ln_b, loss_target, m_a_w_in, m_a_ln_g, m_a_ln_b, m_a_w_s, m_a_b_s, m_a_w_out, m_kv_w, m_b_w_q, m_b_w_o, m_rel_table, m_ffn_w_up, m_ffn_conv_w, m_ffn_conv_b, m_ffn_w_down, m_ln_g, m_ln_b, v_a_w_in, v_a_ln_g, v_a_ln_b, v_a_w_s, v_a_b_s, v_a_w_out, v_kv_w, v_b_w_q, v_b_w_o, v_rel_table, v_ffn_w_up, v_ffn_conv_w, v_ffn_conv_b, v_ffn_w_down, v_ln_g, v_ln_b):
    names = ["a_w_in", "a_ln_g", "a_ln_b", "a_w_s", "a_b_s", "a_w_out", "kv_w", "b_w_q", "b_w_o", "rel_table",
             "ffn_w_up", "ffn_conv_w", "ffn_conv_b", "ffn_w_down", "ln_g", "ln_b"]
    w = dict(zip(names, (a_w_in, a_ln_g, a_ln_b, a_w_s, a_b_s, a_w_out, kv_w, b_w_q, b_w_o, rel_table,
                         ffn_w_up, ffn_conv_w, ffn_conv_b, ffn_w_down, ln_g, ln_b)))
    m = dict(zip(names, (m_a_w_in, m_a_ln_g, m_a_ln_b, m_a_w_s, m_a_b_s, m_a_w_out, m_kv_w, m_b_w_q, m_b_w_o,
                         m_rel_table, m_ffn_w_up, m_ffn_conv_w, m_ffn_conv_b, m_ffn_w_down, m_ln_g, m_ln_b)))
    v = dict(zip(names, (v_a_w_in, v_a_ln_g, v_a_ln_b, v_a_w_s, v_a_b_s, v_a_w_out, v_kv_w, v_b_w_q, v_b_w_o,
                         v_rel_table, v_ffn_w_up, v_ffn_conv_w, v_ffn_conv_b, v_ffn_w_down, v_ln_g, v_ln_b)))
    small_names = [n for n, _ in SMALL_SHARDED]
    small_shapes = [w[n].shape for n in small_names]
    rep_shapes = [w[n].shape for n in REPLICATED]

    axis_of = {key: axis for _, key, axis, _ in BIG}
    src = {}
    for n, key, axis, tr in BIG:
        loc = _as_mats(w[n], tr).astype(BF16)
        for l in range(loc.shape[0]):
            src[(key, l)] = loc[l]
    order = []
    for i in range(DEPTH):
        if i < N_A:
            order += [[("a_w_in", i)], [("a_w_out", i)]]
        else:
            order += [([("kv_w", 0)] if i == N_A else []) + [("b_w_q_t", i - N_A)], [("b_w_o", i - N_A)]]
        order += [[("ffn_w_up_t", i)], [("ffn_w_down", i)]]
    small_src = _pack_rows([w[n] for n in small_names])
    srows = small_src.shape[0]
    handles, _ = _send_start([[(small_src, 0)]] + [[(src[kl], axis_of[kl[0]]) for kl in grp] for grp in order],
                             True, name="gather_start")
    small_all = _send_wait(handles[0], True, x, name="gather_wait_small")[0]
    small_st = _unpack_rows(small_all.reshape(N_DEV, srows, LANES), small_shapes, lead=1)
    base = {n: w[n] for n in REPLICATED}
    for (n, ax), st in zip(SMALL_SHARDED, small_st):
        base[n] = _merge_shards(st, ax)
    for n, key, _, tr in BIG:
        base[key] = [None] * (1 if w[n].ndim == 2 else w[n].shape[0])

    def fetch(group, after):
        mats = _send_wait(handles[1 + group], True, after, name=f"gather_wait_{group}")
        return dict(zip(order[group], mats))

    sent = {}

    def emit(group, mats):
        keys = list(mats)
        hs, token = _send_start([[(mats[kl], axis_of[kl[0]]) for kl in keys]], False, name=f"exchange_start_{group}")
        sent[group] = (keys, hs[0])
        return token

    small_sent = []

    def emit_small(grads):
        small_pack = _pack_rows([_split_shards(grads[n], ax) for n, ax in SMALL_SHARDED], lead=1)
        rest = _pack_rows([grads[n] for n in REPLICATED[1:]])
        mine = jnp.concatenate([small_pack.reshape(N_DEV * srows, LANES), rest], axis=0)
        gating = grads[REPLICATED[0]].reshape(-1, LANES).astype(BF16)
        hs, token = _send_start([[(mine, 0), (gating, 0)]], True, name="small_grads_start")
        small_sent.append(hs[0])
        return token

    loss, grad_x, grads = _local_step(x[0], loss_target[0], _Weights(base, fetch, emit, emit_small))
    loss = lax.psum(loss, ("x", "y", "c"))
    out = {}

    landed = {}
    last = grad_x
    left = lambda e: min(g for g, (keys, _) in sent.items() if any(k[0] == e[1] for k in keys))
    for n, key, axis, tr in sorted(BIG, key=left, reverse=True):
        shp = w[n].shape
        for group in sorted(sent, reverse=True):
            keys, h = sent[group]
            if keys[0] not in landed and any(k[0] == key for k in keys):
                landed.update(zip(keys, _send_wait(h, False, last, name=f"exchange_wait_{group}")))
        parts = [landed[(key, l)] for l in range(1 if len(shp) == 2 else shp[0])]
        res = _adamw(_as_mats(w[n], tr), _as_mats(m[n], tr), _as_mats(v[n], tr), parts, name=f"adamw_{n}")
        out[n] = [(jnp.swapaxes(r, 1, 2) if tr else r).reshape(shp) for r in res]
        last = res[0]

    allp, allg = _send_wait(small_sent[0], True, last, name="small_grads_wait")
    gsum = _sum_parts(allp.reshape(N_DEV, -1, LANES), name="sum_small_grads")
    gating = _sum_parts(allg.reshape(N_DEV, -1, LANES), name="sum_gating_grads")
    g_small = lax.dynamic_slice_in_dim(gsum, _my_index() * srows, srows, axis=0)
    pack_sr = lambda d: jnp.concatenate([_pack_rows([d[n] for n in small_names]),
                                         _pack_rows([d[n] for n in REPLICATED])], axis=0)
    n_rest = sum(int(np.prod(s)) for s in rep_shapes[1:]) // LANES
    gs_in = jnp.concatenate([g_small, gating, gsum[N_DEV * srows:N_DEV * srows + n_rest]], axis=0)
    gs_in = jnp.pad(gs_in, ((0, pack_sr(w).shape[0] - gs_in.shape[0]), (0, 0)))[None]
    res = _adamw(pack_sr(w)[None], pack_sr(m)[None], pack_sr(v)[None], [gs_in], name="adamw_small")
    for n, vals in zip(small_names, zip(*[_unpack_rows(r[0, :srows], small_shapes) for r in res])):
        out[n] = list(vals)
    for n, vals in zip(REPLICATED, zip(*[_unpack_rows(r[0, srows:], rep_shapes) for r in res])):
        out[n] = list(vals)

    return (loss, grad_x[None], *[out[n][0] for n in names], *[out[n][1] for n in names],
            *[out[n][2] for n in names], *[out[n][3] for n in names])
```

```python
import math

import numpy as np
import jax
import jax.numpy as jnp
from jax import lax
from jax.experimental import pallas as pl
from jax.experimental.pallas import tpu as pltpu

F32 = jnp.float32
BF16 = jnp.bfloat16
ACT = jnp.bfloat16
MESH = pl.DeviceIdType.MESH

N_DEV = 8
DEPTH = 4
N_A = 2
CHUNK = 128
BLK = 128
HEAD_DIM = 64
DILATED_GROUPS = ((128, 1), (512, 4), (2048, 16))
N_GROUPS = 3
REL_BUCKETS = 32
REL_MAX_DIST = 2048
ALPHA = (2 * DEPTH) ** 0.25
LN_EPS = 1e-5
NEG = -1e30
ADAM_LR = 0.001
ADAM_B1 = 0.9
ADAM_B2 = 0.999
ADAM_EPS = 1e-08
ADAM_WD = 0.01
ADAM_STEP = 10

LANES = 128
VMEM_LIMIT = 56 * 1024 * 1024
MM_TILE_CAP = 1408
MM_VMEM_BUDGET = 46 * 1024 * 1024
INV_SQRT2 = 1.0 / math.sqrt(2.0)
INV_SQRT_2PI = 1.0 / math.sqrt(2.0 * math.pi)


def _pick(n, cap):
    best = None
    for t in range(LANES, min(n, cap) + 1, LANES):
        if n % t == 0:
            best = t
    return best if best is not None else n


def _params(sem):
    return pltpu.CompilerParams(dimension_semantics=sem, vmem_limit_bytes=VMEM_LIMIT)


def _ordered(body, in_specs, args, after):
    if after is None:
        return body, list(in_specs), tuple(args)
    return (lambda _, *refs: body(*refs)), [pl.BlockSpec(memory_space=pl.ANY), *in_specs], (after, *args)


def _gelu(x):
    return 0.5 * x * (1.0 + lax.erf(x * INV_SQRT2))


def _gelu_grad(x):
    return 0.5 * (1.0 + lax.erf(x * INV_SQRT2)) + x * jnp.exp(-0.5 * x * x) * INV_SQRT_2PI


def _mm(a, b, *, ta=False, tb=False, out_dtype=F32, scale=None, after=None, name):
    halves = isinstance(a, tuple)
    parts = 1 if halves or a.ndim == 2 else a.shape[0]
    ash = (a[0].shape[0], 2 * a[0].shape[1]) if halves else (a.shape if parts == 1
                                                               else (a.shape[1], parts * a.shape[2]))
    if ta:
        K, M = ash
    else:
        M, K = ash
    if tb:
        N, Kb = b.shape
    else:
        Kb, N = b.shape
    assert K == Kb, (ash, b.shape, ta, tb)
    split = 2 if halves else parts
    tm = _pick(M // split if ta else M, MM_TILE_CAP)
    tn = _pick(N, MM_TILE_CAP)
    kspan = K if ta or split == 1 else K // split
    abytes = (a[0] if halves else a).dtype.itemsize * (2 if halves else 1)
    fixed = 2 * tm * tn * jnp.dtype(out_dtype).itemsize + tm * tn * 4
    fits = [t for t in range(LANES, kspan + 1, LANES)
            if kspan % t == 0 and 2 * t * (tm * abytes + tn * b.dtype.itemsize) + fixed <= MM_VMEM_BUDGET]
    tk = max(fits) if fits else _pick(kspan, MM_TILE_CAP)
    nk = K // tk
    nh = (M // split // tm if ta else K // split // tk) if split > 1 else 0
    dn = (((0 if ta else 1,), (1 if tb else 0,)), ((), ()))

    def body(*refs):
        n_tail = 3 if nk > 1 else 2
        a_refs, b_ref, o_ref = refs[:-n_tail], refs[-n_tail], refs[-n_tail + 1]
        k = pl.program_id(2)

        def finish(r):
            if scale is not None:
                r = r * scale
            o_ref[...] = r.astype(out_dtype)

        def accumulate(a_ref):
            part = lax.dot_general(a_ref[...].astype(BF16), b_ref[...].astype(BF16), dn,
                                   preferred_element_type=F32)
            if nk == 1:
                finish(part)
                return
            acc_ref = refs[-1]

            @pl.when(k == 0)
            def _():
                acc_ref[...] = part

            @pl.when((k > 0) & (k < nk - 1))
            def _():
                acc_ref[...] += part

            @pl.when(k == nk - 1)
            def _():
                finish(acc_ref[...] + part)

        if halves:
            first = (pl.program_id(0) if ta else k) < nh
            pl.when(first)(lambda: accumulate(a_refs[0]))
            pl.when(jnp.logical_not(first))(lambda: accumulate(a_refs[1]))
        else:
            accumulate(a_refs[0])

    if halves and ta:
        a_specs = [pl.BlockSpec((tk, tm), lambda i, j, k: (jnp.where(i < nh, k, 0), jnp.minimum(i, nh - 1))),
                   pl.BlockSpec((tk, tm), lambda i, j, k: (jnp.where(i >= nh, k, 0), jnp.maximum(i - nh, 0)))]
    elif halves:
        a_specs = [pl.BlockSpec((tm, tk), lambda i, j, k: (i, jnp.minimum(k, nh - 1))),
                   pl.BlockSpec((tm, tk), lambda i, j, k: (i, jnp.maximum(k - nh, 0)))]
    elif parts > 1:
        a_specs = [pl.BlockSpec((None, tk, tm), lambda i, j, k: (i // nh, k, i % nh)) if ta
                   else pl.BlockSpec((None, tm, tk), lambda i, j, k: (k // nh, i, k % nh))]
    else:
        a_specs = [pl.BlockSpec((tk, tm), lambda i, j, k: (k, i)) if ta
                   else pl.BlockSpec((tm, tk), lambda i, j, k: (i, k))]
    b_spec = (pl.BlockSpec((tn, tk), lambda i, j, k: (j, k)) if tb
              else pl.BlockSpec((tk, tn), lambda i, j, k: (k, j)))
    body, in_specs, args = _ordered(body, [*a_specs, b_spec], (*(a if halves else (a,)), b), after)
    return pl.pallas_call(
        body, grid=(M // tm, N // tn, nk), in_specs=in_specs,
        out_specs=pl.BlockSpec((tm, tn), lambda i, j, k: (i, j)),
        out_shape=jax.ShapeDtypeStruct((M, N), out_dtype),
        scratch_shapes=[pltpu.VMEM((tm, tn), F32)] if nk > 1 else [],
        compiler_params=_params(("parallel", "parallel", "arbitrary")), name=name,
    )(*args)


def _add_ln_fwd(x, h, g, b, *, wide, name):
    T, D = x.shape
    rb = _pick(T, 1024)

    def body(x_ref, h_ref, g_ref, b_ref, *o_refs):
        pre = ALPHA * x_ref[...].astype(F32) + h_ref[...].astype(F32)
        mu = jnp.mean(pre, axis=1, keepdims=True)
        cen = pre - mu
        var = jnp.mean(cen * cen, axis=1, keepdims=True)
        y = cen * lax.rsqrt(var + LN_EPS) * g_ref[...] + b_ref[...]
        for o_ref in o_refs:
            o_ref[...] = y.astype(o_ref.dtype)

    row = pl.BlockSpec((rb, D), lambda i: (i, 0))
    vec = pl.BlockSpec((1, D), lambda i: (0, 0))
    dtypes = [F32, BF16] if wide else [BF16]
    return pl.pallas_call(
        body, grid=(T // rb,), in_specs=[row, row, vec, vec], out_specs=[row] * len(dtypes),
        out_shape=[jax.ShapeDtypeStruct((T, D), dt) for dt in dtypes],
        compiler_params=_params(("parallel",)), name=name,
    )(x, h, g.reshape(1, D), b.reshape(1, D))


def _add_ln_bwd(x, h, g, terms, *, after=None, name):
    T, D = x.shape
    rb = _pick(T, 1024)
    coefs = [c for c, _ in terms]
    nt = len(terms)

    def body(*refs):
        x_ref, h_ref, g_ref = refs[:3]
        t_refs = refs[3:3 + nt]
        dpb_ref, dg_ref, db_ref = refs[3 + nt:]
        dy = None
        for c, r in zip(coefs, t_refs):
            v = r[...].astype(F32) if c == 1.0 else c * r[...].astype(F32)
            dy = v if dy is None else dy + v
        pre = ALPHA * x_ref[...].astype(F32) + h_ref[...].astype(F32)
        mu = jnp.mean(pre, axis=1, keepdims=True)
        cen = pre - mu
        var = jnp.mean(cen * cen, axis=1, keepdims=True)
        rstd = lax.rsqrt(var + LN_EPS)
        xhat = cen * rstd
        dxh = dy * g_ref[...]
        m1 = jnp.mean(dxh, axis=1, keepdims=True)
        m2 = jnp.mean(dxh * xhat, axis=1, keepdims=True)
        dpre = rstd * (dxh - m1 - xhat * m2)
        dpb_ref[...] = dpre.astype(BF16)
        dg = jnp.sum(dy * xhat, axis=0, keepdims=True)
        db = jnp.sum(dy, axis=0, keepdims=True)

        @pl.when(pl.program_id(0) == 0)
        def _():
            dg_ref[...] = dg
            db_ref[...] = db

        @pl.when(pl.program_id(0) > 0)
        def _():
            dg_ref[...] += dg
            db_ref[...] += db

    row = pl.BlockSpec((rb, D), lambda i: (i, 0))
    vec = pl.BlockSpec((1, D), lambda i: (0, 0))
    body, in_specs, args = _ordered(body, [row, row, vec] + [row] * nt,
                                    (x, h, g.reshape(1, D), *[a for _, a in terms]), after)
    return pl.pallas_call(
        body, grid=(T // rb,), in_specs=in_specs,
        out_specs=[row, vec, vec],
        out_shape=[jax.ShapeDtypeStruct((T, D), BF16),
                   jax.ShapeDtypeStruct((1, D), F32), jax.ShapeDtypeStruct((1, D), F32)],
        compiler_params=_params(("arbitrary",)), name=name,
    )(*args)


def _lincomb(terms, out_dtype, *, name):
    R, C = terms[0][1].shape
    rb = _pick(R, 1024)
    coefs = [c for c, _ in terms]
    nt = len(terms)

    def body(*refs):
        acc = None
        for c, r in zip(coefs, refs[:nt]):
            v = r[...].astype(F32)
            v = v if c == 1.0 else c * v
            acc = v if acc is None else acc + v
        refs[nt][...] = acc.astype(out_dtype)

    row = pl.BlockSpec((rb, C), lambda i: (i, 0))
    return pl.pallas_call(
        body, grid=(R // rb,), in_specs=[row] * nt, out_specs=row,
        out_shape=jax.ShapeDtypeStruct((R, C), out_dtype),
        compiler_params=_params(("parallel",)), name=name,
    )(*[a for _, a in terms])


def _loss_grad(y, tgt, *, name):
    T, D = y.shape
    rb = _pick(T, 1024)

    def body(y_ref, t_ref, dy_ref, l_ref):
        err = y_ref[...] - t_ref[...]
        dy_ref[...] = err * (1.0 / D)
        part = jnp.sum(jnp.sum(err * err, axis=1, keepdims=True), axis=0, keepdims=True) * (0.5 / D)
        part = jnp.broadcast_to(part, (1, LANES))

        @pl.when(pl.program_id(0) == 0)
        def _():
            l_ref[...] = part

        @pl.when(pl.program_id(0) > 0)
        def _():
            l_ref[...] += part

    row = pl.BlockSpec((rb, D), lambda i: (i, 0))
    return pl.pallas_call(
        body, grid=(T // rb,), in_specs=[row, row],
        out_specs=[row, pl.BlockSpec((1, LANES), lambda i: (0, 0))],
        out_shape=[jax.ShapeDtypeStruct((T, D), F32), jax.ShapeDtypeStruct((1, LANES), F32)],
        compiler_params=_params(("arbitrary",)), name=name,
    )(y, tgt)


def _sgu_fwd(zp, ln_g, ln_b, ws, bst, *, name):
    T, E2 = zp.shape
    E = E2 // 2
    G = ws.shape[0]
    cg = E // G
    rb = 4 * CHUNK

    def body(z_ref, g_ref, b_ref, ws_ref, bs_ref, y_ref):
        u = _gelu(z_ref[:, :E].astype(F32))
        v = _gelu(z_ref[:, E:].astype(F32))
        mu = jnp.mean(v, axis=1, keepdims=True)
        cen = v - mu
        var = jnp.mean(cen * cen, axis=1, keepdims=True)
        vn = (cen * lax.rsqrt(var + LN_EPS) * g_ref[...] + b_ref[...]).astype(BF16)
        for ci in range(rb // CHUNK):
            rows = slice(ci * CHUNK, (ci + 1) * CHUNK)
            for gi in range(G):
                cols = slice(gi * cg, (gi + 1) * cg)
                sv = jnp.dot(ws_ref[gi], vn[rows, cols], preferred_element_type=F32)
                sv = sv + bs_ref[:, gi:gi + 1]
                y_ref[rows, cols] = (u[rows, cols] * sv).astype(BF16)

    return pl.pallas_call(
        body, grid=(T // rb,),
        in_specs=[pl.BlockSpec((rb, E2), lambda i: (i, 0)),
                  pl.BlockSpec((1, E), lambda i: (0, 0)), pl.BlockSpec((1, E), lambda i: (0, 0)),
                  pl.BlockSpec((G, CHUNK, CHUNK), lambda i: (0, 0, 0)),
                  pl.BlockSpec((CHUNK, G), lambda i: (0, 0))],
        out_specs=pl.BlockSpec((rb, E), lambda i: (i, 0)),
        out_shape=jax.ShapeDtypeStruct((T, E), BF16),
        compiler_params=_params(("parallel",)), name=name,
    )(zp, ln_g.reshape(1, E), ln_b.reshape(1, E), ws, bst)


def _sgu_bwd(zp, dy, ln_g, ln_b, ws, wst, bst, *, after=None, name):
    T, E2 = zp.shape
    E = E2 // 2
    G = ws.shape[0]
    cg = E // G
    rb = CHUNK
    nsteps = T // rb

    def body(z_ref, dy_ref, g_ref, b_ref, ws_ref, wst_ref, bs_ref,
             dz_ref, dg_ref, db_ref, dws_ref, dbs_ref, dsv_acc):
        step = pl.program_id(0)

        @pl.when(step == 0)
        def _():
            dg_ref[...] = jnp.zeros_like(dg_ref)
            db_ref[...] = jnp.zeros_like(db_ref)
            dws_ref[...] = jnp.zeros_like(dws_ref)
            dsv_acc[...] = jnp.zeros_like(dsv_acc)

        zu = z_ref[:, :E].astype(F32)
        zv = z_ref[:, E:].astype(F32)
        u = _gelu(zu)
        v = _gelu(zv)
        mu = jnp.mean(v, axis=1, keepdims=True)
        cen = v - mu
        var = jnp.mean(cen * cen, axis=1, keepdims=True)
        rstd = lax.rsqrt(var + LN_EPS)
        xhat = cen * rstd
        vn = (xhat * g_ref[...] + b_ref[...]).astype(BF16)
        dyv = dy_ref[...].astype(F32)
        dsv = dyv * u
        dsv_acc[...] += dsv
        dsvb = dsv.astype(BF16)
        tril = (lax.broadcasted_iota(jnp.int32, (CHUNK, CHUNK), 0)
                >= lax.broadcasted_iota(jnp.int32, (CHUNK, CHUNK), 1))
        du_parts = []
        dvn_parts = []
        for gi in range(G):
            cols = slice(gi * cg, (gi + 1) * cg)
            sv = jnp.dot(ws_ref[gi], vn[:, cols], preferred_element_type=F32) + bs_ref[:, gi:gi + 1]
            du_parts.append(dyv[:, cols] * sv)
            dvn_parts.append(jnp.dot(wst_ref[gi], dsvb[:, cols], preferred_element_type=F32))
            dw = lax.dot_general(dsvb[:, cols], vn[:, cols], (((1,), (1,)), ((), ())),
                                 preferred_element_type=F32)
            dws_ref[gi] += jnp.where(tril, dw, 0.0)
        du = jnp.concatenate(du_parts, axis=1)
        dvn = jnp.concatenate(dvn_parts, axis=1)
        dg_ref[...] += jnp.sum(dvn * xhat, axis=0, keepdims=True)
        db_ref[...] += jnp.sum(dvn, axis=0, keepdims=True)
        dxh = dvn * g_ref[...]
        m1 = jnp.mean(dxh, axis=1, keepdims=True)
        m2 = jnp.mean(dxh * xhat, axis=1, keepdims=True)
        dv = rstd * (dxh - m1 - xhat * m2)
        dz_ref[:, :E] = (du * _gelu_grad(zu)).astype(BF16)
        dz_ref[:, E:] = (dv * _gelu_grad(zv)).astype(BF16)

        @pl.when(step == nsteps - 1)
        def _():
            lane = lax.broadcasted_iota(jnp.int32, (CHUNK, LANES), 1)
            out = jnp.zeros((CHUNK, LANES), F32)
            for gi in range(G):
                s = jnp.sum(dsv_acc[:, gi * cg:(gi + 1) * cg], axis=1, keepdims=True)
                out = jnp.where(lane == gi, s, out)
            dbs_ref[...] = out

    vecE = pl.BlockSpec((1, E), lambda i: (0, 0))
    wspec = pl.BlockSpec((G, CHUNK, CHUNK), lambda i: (0, 0, 0))
    body, in_specs, args = _ordered(
        body, [pl.BlockSpec((rb, E2), lambda i: (i, 0)), pl.BlockSpec((rb, E), lambda i: (i, 0)),
               vecE, vecE, wspec, wspec, pl.BlockSpec((CHUNK, G), lambda i: (0, 0))],
        (zp, dy, ln_g.reshape(1, E), ln_b.reshape(1, E), ws, wst, bst), after)
    return pl.pallas_call(
        body, grid=(nsteps,), in_specs=in_specs,
        out_specs=[pl.BlockSpec((rb, E2), lambda i: (i, 0)), vecE, vecE, wspec,
                   pl.BlockSpec((CHUNK, LANES), lambda i: (0, 0))],
        out_shape=[jax.ShapeDtypeStruct((T, E2), BF16), jax.ShapeDtypeStruct((1, E), F32),
                   jax.ShapeDtypeStruct((1, E), F32), jax.ShapeDtypeStruct((G, CHUNK, CHUNK), F32),
                   jax.ShapeDtypeStruct((CHUNK, LANES), F32)],
        scratch_shapes=[pltpu.VMEM((CHUNK, E), F32)],
        compiler_params=_params(("arbitrary",)), name=name,
    )(*args)


def _shift_down(x, k, row):
    return jnp.where(row >= k, pltpu.roll(x, k, 0), 0.0)


def _shift_up(x, k, row, T):
    return jnp.where(row < T - k, pltpu.roll(x, T - k, 0), 0.0)


def _conv3(x, w_ref, b_ref, row):
    return (w_ref[0:1, :] * _shift_down(x, 2, row) + w_ref[1:2, :] * _shift_down(x, 1, row)
            + w_ref[2:3, :] * x + b_ref[...])


def _convgate_fwd(hh, cw, cb, *, name):
    T, F2 = hh.shape
    F = F2 // 2
    ns = F // LANES

    def body(a_ref, g_ref, wa_ref, wg_ref, ba_ref, bg_ref, o_ref, ca_ref, cg_ref):
        row = lax.broadcasted_iota(jnp.int32, (T, LANES), 0)
        ca = _conv3(a_ref[...].astype(F32), wa_ref, ba_ref, row)
        cgv = _conv3(g_ref[...].astype(F32), wg_ref, bg_ref, row)
        o_ref[...] = (_gelu(ca) * cgv).astype(BF16)
        ca_ref[...] = ca.astype(ACT)
        cg_ref[...] = cgv.astype(ACT)

    sa = lambda r: pl.BlockSpec((r, LANES), lambda j: (0, j))
    sg = lambda r: pl.BlockSpec((r, LANES), lambda j: (0, j + ns))
    return pl.pallas_call(
        body, grid=(ns,), in_specs=[sa(T), sg(T), sa(3), sg(3), sa(1), sg(1)],
        out_specs=[sa(T)] * 3,
        out_shape=[jax.ShapeDtypeStruct((T, F), BF16), jax.ShapeDtypeStruct((T, F), ACT),
                   jax.ShapeDtypeStruct((T, F), ACT)],
        compiler_params=_params(("parallel",)), name=name,
    )(hh, hh, cw, cw, cb, cb)


def _convgate_bwd(hh, hca, hcg, dact, cw, *, name):
    T, F2 = hh.shape
    F = F2 // 2
    ns = F // LANES

    def body(a_ref, g_ref, ca_ref, cg_ref, d_ref, wa_ref, wg_ref,
             da_ref, dg_ref, dwa_ref, dwg_ref, dba_ref, dbg_ref):
        row = lax.broadcasted_iota(jnp.int32, (T, LANES), 0)
        d = d_ref[...].astype(F32)
        ca = ca_ref[...].astype(F32)
        cgv = cg_ref[...].astype(F32)
        cdf = 0.5 * (1.0 + lax.erf(ca * INV_SQRT2))
        dca = d * cgv * (cdf + ca * jnp.exp(-0.5 * ca * ca) * INV_SQRT_2PI)
        dcg = d * (ca * cdf)
        for x_ref, w_ref, dc, dx_ref, dw_ref, db_ref in (
                (a_ref, wa_ref, dca, da_ref, dwa_ref, dba_ref),
                (g_ref, wg_ref, dcg, dg_ref, dwg_ref, dbg_ref)):
            x = x_ref[...].astype(F32)
            up1, up2 = _shift_up(dc, 1, row, T), _shift_up(dc, 2, row, T)
            dx_ref[...] = (w_ref[2:3, :] * dc + w_ref[1:2, :] * up1 + w_ref[0:1, :] * up2).astype(BF16)
            dw_ref[0:1, :] = jnp.sum(up2 * x, axis=0, keepdims=True)
            dw_ref[1:2, :] = jnp.sum(up1 * x, axis=0, keepdims=True)
            dw_ref[2:3, :] = jnp.sum(dc * x, axis=0, keepdims=True)
            db_ref[...] = jnp.sum(dc, axis=0, keepdims=True)

    sa = lambda r: pl.BlockSpec((r, LANES), lambda j: (0, j))
    sg = lambda r: pl.BlockSpec((r, LANES), lambda j: (0, j + ns))
    return pl.pallas_call(
        body, grid=(ns,), in_specs=[sa(T), sg(T), sa(T), sa(T), sa(T), sa(3), sg(3)],
        out_specs=[sa(T), sa(T), sa(3), sa(3), sa(1), sa(1)],
        out_shape=[jax.ShapeDtypeStruct((T, F), BF16), jax.ShapeDtypeStruct((T, F), BF16),
                   jax.ShapeDtypeStruct((3, F), F32), jax.ShapeDtypeStruct((3, F), F32),
                   jax.ShapeDtypeStruct((1, F), F32), jax.ShapeDtypeStruct((1, F), F32)],
        compiler_params=_params(("parallel",)), name=name,
    )(hh, hh, hca, hcg, dact, cw, cw)


def _bucket_maps():
    iq = np.arange(BLK)[:, None]
    ik = np.arange(2 * BLK)[None, :]
    delta = iq + BLK - ik
    maps = []
    for win, dil in DILATED_GROUPS:
        n = np.clip(delta, 0, None) * dil
        max_exact = REL_BUCKETS // 2
        nf = np.maximum(n, 1).astype(np.float32)
        large = max_exact + (np.log(nf / np.float32(max_exact)) / np.float32(math.log(REL_MAX_DIST / max_exact))
                             * np.float32(REL_BUCKETS - max_exact)).astype(np.int32)
        large = np.minimum(large, REL_BUCKETS - 1)
        bucket = np.where(n < max_exact, n, large)
        valid = (delta >= 0) & (delta <= win // dil)
        maps.append(np.where(valid, bucket, -1).astype(np.int32))
    return np.stack(maps)


def _band_bias(rel_table, bmap, H, *, name):
    def body(t_ref, m_ref, o_ref):
        g = pl.program_id(0)
        bm = m_ref[0]
        for h in range(H):
            acc = jnp.full((BLK, 2 * BLK), NEG, F32)
            for b in range(REL_BUCKETS):
                acc = jnp.where(bm == b, t_ref[b, g * H + h], acc)
            o_ref[0, h] = acc

    return pl.pallas_call(
        body, grid=(N_GROUPS,),
        in_specs=[pl.BlockSpec(memory_space=pltpu.SMEM),
                  pl.BlockSpec((1, BLK, 2 * BLK), lambda g: (g, 0, 0))],
        out_specs=pl.BlockSpec((1, H, BLK, 2 * BLK), lambda g: (g, 0, 0, 0)),
        out_shape=jax.ShapeDtypeStruct((N_GROUPS, H, BLK, 2 * BLK), F32),
        compiler_params=_params(("parallel",)), name=name,
    )(rel_table, bmap)


def _band_bias_bwd(dbias, bmap, H, *, name):
    def body(d_ref, m_ref, o_ref):
        bm = m_ref[0]
        rowi = lax.broadcasted_iota(jnp.int32, (REL_BUCKETS, LANES), 0)
        lane = lax.broadcasted_iota(jnp.int32, (REL_BUCKETS, LANES), 1)
        out = jnp.zeros((REL_BUCKETS, LANES), F32)
        for h in range(H):
            dv = d_ref[0, h]
            for b in range(REL_BUCKETS):
                s = jnp.sum(jnp.sum(jnp.where(bm == b, dv, 0.0), axis=1, keepdims=True),
                            axis=0, keepdims=True)
                out = jnp.where((rowi == b) & (lane == h), s, out)
        o_ref[0] = out

    return pl.pallas_call(
        body, grid=(N_GROUPS,),
        in_specs=[pl.BlockSpec((1, H, BLK, 2 * BLK), lambda g: (g, 0, 0, 0)),
                  pl.BlockSpec((1, BLK, 2 * BLK), lambda g: (g, 0, 0))],
        out_specs=pl.BlockSpec((1, REL_BUCKETS, LANES), lambda g: (g, 0, 0)),
        out_shape=jax.ShapeDtypeStruct((N_GROUPS, REL_BUCKETS, LANES), F32),
        compiler_params=_params(("parallel",)), name=name,
    )(dbias, bmap)


def _head_masks():
    lane = lax.broadcasted_iota(jnp.int32, (BLK, LANES), 1)
    return (lane < HEAD_DIM, lane >= HEAD_DIM)


SUPER =DILATED_GROUPS[-1][1] * BLK


def _band_rows(it, d):
    r, j = it % d, it // d
    if d == 1:
        at = lambda blk: pl.ds(pl.multiple_of(blk * BLK, BLK), BLK)
    else:
        at = lambda blk: pl.ds(r + d * BLK * blk, BLK, stride=d)
    return at(j), at(jnp.maximum(j - 1, 0))


def _stack_heads(x, hm):
    zero = jnp.zeros_like(x)
    return jnp.concatenate([jnp.where(hm[0], x, zero), jnp.where(hm[1], x, zero)], axis=0)


def _band_loops(step, d, unroll):
    n_it = SUPER // BLK

    def run(lo, hi, inside):
        if hi > lo:
            def body(it, carry):
                step(it, inside)
                return carry
            lax.fori_loop(lo, hi, body, 0, unroll=max(u for u in range(1, unroll + 1) if (hi - lo) % u == 0))

    run(0, d, False)
    run(d, n_it, True)


def _last_rows(it, d):
    m = SUPER // (d * BLK)
    if d == 1:
        return pl.ds((m - 1) * BLK, BLK)
    return pl.ds(it % d + d * BLK * (m - 1), BLK, stride=d)


def _attn_fwd_all(q, kv, bias, *, name):
    T = q.shape[0]
    HD = kv.shape[1] // 2
    PP = HD // LANES
    NS = T // SUPER

    def body(q0, q1, q2, kp_ref, kc_ref, vp_ref, vc_ref, b_ref, o_ref, ob_ref, l_ref, og, lg):
        n = pl.program_id(1)
        col = lax.broadcasted_iota(jnp.int32, (2 * BLK, 2 * BLK), 1)
        hm = _head_masks()
        for g, (q_ref, (_, d)) in enumerate(zip((q0, q1, q2), DILATED_GROUPS)):
            def step(it, inside, g=g, q_ref=q_ref, d=d):
                cur, prv = _band_rows(it, d)
                qp = q_ref[cur, :].astype(BF16)
                if inside:
                    kprev, vprev = kc_ref[prv, :], vc_ref[prv, :]
                else:
                    last = _last_rows(it, d)
                    kprev, vprev = kp_ref[last, :], vp_ref[last, :]
                kc = jnp.concatenate([kprev.astype(BF16), kc_ref[cur, :].astype(BF16)], axis=0)
                vc = jnp.concatenate([vprev.astype(BF16), vc_ref[cur, :].astype(BF16)], axis=0)
                s = lax.dot_general(_stack_heads(qp, hm), kc, (((1,), (1,)), ((), ())),
                                    preferred_element_type=F32)
                s = s + b_ref[g].reshape(2 * BLK, 2 * BLK)
                if not inside:
                    s = jnp.where((n == 0) & (col < BLK), NEG, s)
                mx = jnp.max(s, axis=1, keepdims=True)
                e = jnp.exp(s - mx)
                den = jnp.sum(e, axis=1, keepdims=True)
                out = jnp.dot((e / den).astype(BF16), vc, preferred_element_type=F32)
                lse = mx + jnp.log(den)
                og.at[g][cur, :] = jnp.where(hm[0], out[:BLK], out[BLK:])
                lg.at[g][cur, :] = jnp.where(hm[0], lse[:BLK], lse[BLK:])

            _band_loops(step, d, 8)
        la, lb, lc = lg[0], lg[1], lg[2]
        mx = jnp.maximum(jnp.maximum(la, lb), lc)
        L = mx + jnp.log(jnp.exp(la - mx) + jnp.exp(lb - mx) + jnp.exp(lc - mx))
        o = jnp.exp(la - L) * og[0] + jnp.exp(lb - L) * og[1] + jnp.exp(lc - L) * og[2]
        o_ref[...] = o
        ob_ref[...] = o.astype(BF16)
        l_ref[...] = L

    blk = lambda f: pl.BlockSpec((SUPER, LANES), f)
    prev = lambda n: jnp.maximum(n - 1, 0)
    qspec = lambda g: blk(lambda p, n: (n, g * PP + p))
    return pl.pallas_call(
        body, grid=(PP, NS),
        in_specs=[qspec(0), qspec(1), qspec(2),
                  blk(lambda p, n: (prev(n), p)), blk(lambda p, n: (n, p)),
                  blk(lambda p, n: (prev(n), PP + p)), blk(lambda p, n: (n, PP + p)),
                  pl.BlockSpec((N_GROUPS, 2, BLK, 2 * BLK), lambda p, n: (0, p, 0, 0))],
        out_specs=[blk(lambda p, n: (n, p))] * 3,
        out_shape=[jax.ShapeDtypeStruct((T, HD), F32), jax.ShapeDtypeStruct((T, HD), BF16),
                   jax.ShapeDtypeStruct((T, HD), F32)],
        scratch_shapes=[pltpu.VMEM((N_GROUPS, SUPER, LANES), F32), pltpu.VMEM((N_GROUPS, SUPER, LANES), F32)],
        compiler_params=_params(("parallel", "parallel")), name=name,
    )(q, q, q, kv, kv, kv, kv, bias)


def _attn_bwd_all(q, kv, bias, do, o, L, *, after=None, name):
    T = q.shape[0]
    HD = kv.shape[1] // 2
    PP = HD // LANES
    H = HD // HEAD_DIM
    NS = T // SUPER

    def body(q0, q1, q2, kp_ref, kc_ref, vp_ref, vc_ref, b_ref, do_ref, o_ref, L_ref,
             dq_ref, dk_ref, dv_ref, db_ref, ck_ref, cv_ref):
        n = pl.program_id(1)

        @pl.when(n == 0)
        def _():
            db_ref[...] = jnp.zeros_like(db_ref)
            ck_ref[...] = jnp.zeros_like(ck_ref)
            cv_ref[...] = jnp.zeros_like(cv_ref)

        dk_ref[...] = ck_ref[...]
        dv_ref[...] = cv_ref[...]
        ck_ref[...] = jnp.zeros_like(ck_ref)
        cv_ref[...] = jnp.zeros_like(cv_ref)

        @pl.when(n < NS)
        def _():
            col = lax.broadcasted_iota(jnp.int32, (2 * BLK, 2 * BLK), 1)
            hm = _head_masks()
            for g, (q_ref, (_, d)) in enumerate(zip((q0, q1, q2), DILATED_GROUPS)):
                def step(it, inside, g=g, q_ref=q_ref, d=d):
                    cur, prv = _band_rows(it, d)
                    last = _last_rows(it, d)
                    qp = q_ref[cur, :].astype(BF16)
                    if inside:
                        kprev, vprev = kc_ref[prv, :], vc_ref[prv, :]
                    else:
                        kprev, vprev = kp_ref[last, :], vp_ref[last, :]
                    kc = jnp.concatenate([kprev.astype(BF16), kc_ref[cur, :].astype(BF16)], axis=0)
                    vc = jnp.concatenate([vprev.astype(BF16), vc_ref[cur, :].astype(BF16)], axis=0)
                    dop = do_ref[cur, :]
                    prod = dop * o_ref[cur, :]
                    Lp = L_ref[cur, :]
                    qs = _stack_heads(qp, hm)
                    dos = _stack_heads(dop.astype(BF16), hm)
                    lse = jnp.concatenate([Lp[:, 0:1], Lp[:, HEAD_DIM:HEAD_DIM + 1]], axis=0)
                    delta = jnp.concatenate([jnp.sum(jnp.where(hm[0], prod, 0.0), axis=1, keepdims=True),
                                             jnp.sum(jnp.where(hm[1], prod, 0.0), axis=1, keepdims=True)], axis=0)
                    s = lax.dot_general(qs, kc, (((1,), (1,)), ((), ())), preferred_element_type=F32)
                    s = s + b_ref[g].reshape(2 * BLK, 2 * BLK)
                    if not inside:
                        s = jnp.where((n == 0) & (col < BLK), NEG, s)
                    pr = jnp.exp(s - lse)
                    dp = lax.dot_general(dos, vc, (((1,), (1,)), ((), ())), preferred_element_type=F32)
                    ds = pr * (dp - delta)
                    db_ref[g] += ds.reshape(2, BLK, 2 * BLK)
                    dsb = ds.astype(BF16)
                    dqs = jnp.dot(dsb, kc, preferred_element_type=F32)
                    dkc = lax.dot_general(dsb, qs, (((0,), (0,)), ((), ())), preferred_element_type=F32)
                    dvc = lax.dot_general(pr.astype(BF16), dos, (((0,), (0,)), ((), ())),
                                          preferred_element_type=F32)
                    dq_ref.at[g][cur, :] = jnp.where(hm[0], dqs[:BLK], dqs[BLK:]) * (HEAD_DIM ** -0.5)
                    ck_ref[cur, :] += dkc[BLK:]
                    cv_ref[cur, :] += dvc[BLK:]
                    if inside:
                        ck_ref[prv, :] += dkc[:BLK]
                        cv_ref[prv, :] += dvc[:BLK]
                    else:
                        dk_ref[last, :] += dkc[:BLK]
                        dv_ref[last, :] += dvc[:BLK]

                _band_loops(step, d, 4)

    blk = lambda f: pl.BlockSpec((SUPER, LANES), f)
    cur = lambda n: jnp.minimum(n, NS - 1)
    prev = lambda n: jnp.maximum(jnp.minimum(n, NS - 1) - 1, 0)
    lag = lambda n: jnp.maximum(n - 1, 0)
    qspec = lambda g: blk(lambda p, n: (cur(n), g * PP + p))
    bspec = pl.BlockSpec((N_GROUPS, 2, BLK, 2 * BLK), lambda p, n: (0, p, 0, 0))
    body, in_specs, args = _ordered(
        body, [qspec(0), qspec(1), qspec(2),
               blk(lambda p, n: (prev(n), p)), blk(lambda p, n: (cur(n), p)),
               blk(lambda p, n: (prev(n), PP + p)), blk(lambda p, n: (cur(n), PP + p)),
               bspec, blk(lambda p, n: (cur(n), p)), blk(lambda p, n: (cur(n), p)),
               blk(lambda p, n: (cur(n), p))],
        (q, q, q, kv, kv, kv, kv, bias, do, o, L), after)
    return pl.pallas_call(
        body, grid=(PP, NS + 1), in_specs=in_specs,
        out_specs=[pl.BlockSpec((N_GROUPS, SUPER, LANES), lambda p, n: (0, cur(n), p)),
                   blk(lambda p, n: (lag(n), p)), blk(lambda p, n: (lag(n), p)), bspec],
        out_shape=[jax.ShapeDtypeStruct((N_GROUPS, T, HD), F32), jax.ShapeDtypeStruct((T, HD), F32),
                   jax.ShapeDtypeStruct((T, HD), F32),
                   jax.ShapeDtypeStruct((N_GROUPS, H, BLK, 2 * BLK), F32)],
        scratch_shapes=[pltpu.VMEM((SUPER, LANES), F32), pltpu.VMEM((SUPER, LANES), F32)],
        compiler_params=_params(("arbitrary", "arbitrary")), name=name,
    )(*args)


LAST_DOWN = 3 * DEPTH


class _Weights(dict):
    def __init__(self, base, fetch=None, emit=None, emit_small=None):
        super().__init__(base)
        self._fetch, self._emit, self._emit_small = fetch, emit, emit_small

    def fetch(self, group, after):
        if self._fetch is not None:
            for (key, layer), mat in self._fetch(group, after).items():
                self[key][layer] = mat

    def emit(self, group, mats):
        return None if self._emit is None else self._emit(group, mats)

    def emit_small(self, grads):
        return None if self._emit_small is None else self._emit_small(grads)


def _local_step(x, tgt, W):
    T, D = x.shape
    H = W["rel_table"].shape[1] // N_GROUPS
    HD = H * HEAD_DIM
    G = W["a_w_s"].shape[1]
    assert T % (DILATED_GROUPS[-1][1] * BLK) == 0

    tril = jnp.tril(jnp.ones((CHUNK, CHUNK), F32))
    bmap = jnp.asarray(_bucket_maps())
    bias = _band_bias(W["rel_table"], bmap, H, name="band_bias")

    saved = []
    xc, xcb = x, x.astype(BF16)
    kvb = None
    for i in range(DEPTH):
        s = {"x": xc, "xb": xcb}
        W.fetch(4 * i, xc)
        if i < N_A:
            ws_m = W["a_w_s"][i] * tril
            s["ws"] = ws_m.astype(BF16)
            s["wst"] = jnp.swapaxes(ws_m, 1, 2).astype(BF16)
            s["bst"] = W["a_b_s"][i].T
            s["zp"] = _mm(xcb, W["a_w_in"][i], out_dtype=ACT, name=f"a_in_{i}")
            s["y"] = _sgu_fwd(s["zp"], W["a_ln_g"][i], W["a_ln_b"][i], s["ws"], s["bst"], name=f"sgu_fwd_{i}")
            W.fetch(4 * i + 1, s["zp"])
            s["h"] = _mm(s["y"], W["a_w_out"][i], out_dtype=ACT, name=f"a_out_{i}")
        else:
            j = i - N_A
            if kvb is None:
                kvb = _mm(xcb, W["kv_w"][0], name="kv_proj")
            s["q"] = _mm(xcb, W["b_w_q_t"][j], tb=True, scale=HEAD_DIM ** -0.5, name=f"q_proj_{j}")
            s["o"], s["ob"], s["L"] = _attn_fwd_all(s["q"], kvb, bias, name=f"attn_fwd_{j}")
            W.fetch(4 * i + 1, s["q"])
            s["h"] = _mm(s["ob"], W["b_w_o"][j], out_dtype=ACT, name=f"o_proj_{j}")
        s["x1b"], = _add_ln_fwd(xc, s["h"], W["ln_g"][i, 0], W["ln_b"][i, 0], wide=False, name=f"ln1_fwd_{i}")
        s["x1"] = s["x1b"]
        W.fetch(4 * i + 2, s["x1"])
        s["hh"] = _mm(s["x1b"], W["ffn_w_up_t"][i], tb=True, out_dtype=ACT, name=f"ffn_up_{i}")
        s["cw"] = W["ffn_conv_w"][i]
        s["cb"] = W["ffn_conv_b"][i].reshape(1, -1)
        s["act"], s["hca"], s["hcg"] = _convgate_fwd(s["hh"], s["cw"], s["cb"], name=f"convgate_fwd_{i}")
        W.fetch(4 * i + 3, s["hh"])
        s["f"] = _mm(s["act"], W["ffn_w_down"][i], out_dtype=ACT, name=f"ffn_down_{i}")
        outs = _add_ln_fwd(s["x1"], s["f"], W["ln_g"][i, 1], W["ln_b"][i, 1], wide=i == DEPTH - 1,
                           name=f"ln2_fwd_{i}")
        xc, xcb = outs[0], outs[-1]
        saved.append(s)

    dy, lossv = _loss_grad(xc, tgt, name="loss_grad")
    loss = lossv[0, 0]

    gl = {k: [None] * DEPTH for k in ("ffn_w_up_t", "ffn_conv_w", "ffn_conv_b", "ffn_w_down", "ln_g", "ln_b")}
    ga = {k: [None] * N_A for k in ("a_w_in", "a_ln_g", "a_ln_b", "a_w_s", "a_b_s", "a_w_out")}
    gb = {k: [None] * (DEPTH - N_A) for k in ("b_w_q_t", "b_w_o")}
    mats = ("a_w_in", "a_w_out", "b_w_q_t", "b_w_o", "ffn_w_up_t", "ffn_w_down")
    dks, dvs, dbias = [], [], []
    grads = {}
    terms = [(1.0, dy)]
    tok = None
    small_keys = ("ffn_conv_w", "ffn_conv_b", "ln_g", "ln_b", "a_ln_g", "a_ln_b", "a_w_s", "a_b_s")
    for i in reversed(range(DEPTH)):
        s = saved[i]
        dp2b, dg2, db2 = _add_ln_bwd(s["x1"], s["f"], W["ln_g"][i, 1], terms, after=tok, name=f"ln2_bwd_{i}")
        dact = _mm(dp2b, W["ffn_w_down"][i], tb=True, out_dtype=ACT, name=f"ffn_down_dx_{i}")
        gl["ffn_w_down"][i] = _mm(s["act"], dp2b, ta=True, out_dtype=BF16, name=f"ffn_down_dw_{i}")
        out_f = {("ffn_w_down", i): gl["ffn_w_down"][i]}
        if i == 0:
            tok = W.emit(LAST_DOWN, out_f)
            out_f = {}
        dha, dhg, dwa, dwg, dba, dbg = _convgate_bwd(s["hh"], s["hca"], s["hcg"], dact, s["cw"],
                                                     name=f"convgate_bwd_{i}")
        dhh = (dha, dhg)
        gl["ffn_conv_w"][i] = jnp.concatenate([dwa, dwg], axis=1)
        gl["ffn_conv_b"][i] = jnp.concatenate([dba, dbg], axis=1)[0]
        dx1 = _mm(dhh, W["ffn_w_up_t"][i], out_dtype=ACT, after=tok, name=f"ffn_up_dx_{i}")
        gl["ffn_w_up_t"][i] = _mm(dhh, s["x1b"], ta=True, out_dtype=BF16, name=f"ffn_up_dw_{i}")
        out_f[("ffn_w_up_t", i)] = gl["ffn_w_up_t"][i]
        if i == 0:
            tok = W.emit(3 * i + 2, out_f)
        dp1b, dg1, db1 = _add_ln_bwd(s["x"], s["h"], W["ln_g"][i, 0], [(ALPHA, dp2b), (1.0, dx1)],
                                     after=tok, name=f"ln1_bwd_{i}")
        gl["ln_g"][i] = jnp.concatenate([dg1, dg2], axis=0)
        gl["ln_b"][i] = jnp.concatenate([db1, db2], axis=0)
        terms = [(ALPHA, dp1b)]
        if i < N_A:
            dyy = _mm(dp1b, W["a_w_out"][i], tb=True, out_dtype=ACT, name=f"a_out_dx_{i}")
            ga["a_w_out"][i] = _mm(s["y"], dp1b, ta=True, out_dtype=BF16, name=f"a_out_dw_{i}")
            if i == 0:
                tok = W.emit(3 * i + 1, {("a_w_out", i): ga["a_w_out"][i]})
            dzp, dlg, dlb, dws, dbs = _sgu_bwd(s["zp"], dyy, W["a_ln_g"][i], W["a_ln_b"][i], s["ws"],
                                               s["wst"], s["bst"], after=tok, name=f"sgu_bwd_{i}")
            ga["a_ln_g"][i], ga["a_ln_b"][i], ga["a_w_s"][i] = dlg[0], dlb[0], dws
            ga["a_b_s"][i] = dbs[:, :G].T
            if i == 0:
                for dct in (gl, ga):
                    grads.update({k: jnp.stack(v) for k, v in dct.items() if k in small_keys})
                tok = W.emit_small(grads)
            ga["a_w_in"][i] = _mm(s["xb"], dzp, ta=True, out_dtype=BF16, after=tok, name=f"a_in_dw_{i}")
            out_a = {("a_w_in", i): ga["a_w_in"][i]}
            if i > 0:
                out_a[("a_w_out", i)] = ga["a_w_out"][i]
                out_a.update(out_f)
            tok = W.emit(3 * i, out_a)
            terms.append((1.0, _mm(dzp, W["a_w_in"][i], tb=True, out_dtype=ACT, after=tok, name=f"a_in_dx_{i}")))
        else:
            j = i - N_A
            do = _mm(dp1b, W["b_w_o"][j], tb=True, name=f"o_proj_dx_{j}")
            gb["b_w_o"][j] = _mm(s["ob"], dp1b, ta=True, out_dtype=BF16, name=f"o_proj_dw_{j}")
            dq, dk_j, dv_j, db_j = _attn_bwd_all(s["q"], kvb, bias, do, s["o"], s["L"], after=tok,
                                                 name=f"attn_bwd_{j}")
            dks.append((1.0, dk_j))
            dvs.append((1.0, dv_j))
            dbias.append(db_j)
            terms.append((1.0, _mm(dq, W["b_w_q_t"][j], out_dtype=ACT, name=f"q_proj_dx_{j}")))
            gb["b_w_q_t"][j] = _mm(dq, s["xb"], ta=True, out_dtype=BF16, name=f"q_proj_dw_{j}")
            out_b = {("b_w_q_t", j): gb["b_w_q_t"][j], ("b_w_o", j): gb["b_w_o"][j], **out_f}
            if i == N_A:
                dkv = jnp.concatenate([_lincomb(dks, BF16, name="dk_sum"), _lincomb(dvs, BF16, name="dv_sum")],
                                      axis=1)
                terms.append((1.0, _mm(dkv, W["kv_w"][0], tb=True, out_dtype=ACT, name="kv_proj_dx")))
                grads["kv_w"] = [_mm(s["xb"], dkv, ta=True, out_dtype=BF16, name="kv_proj_dw")]
                out_b[("kv_w", 0)] = grads["kv_w"][0]
                dbt = _lincomb([(1.0, a.reshape(-1, 2 * BLK)) for a in dbias], F32, name="dbias_sum")
                dtab = _band_bias_bwd(dbt.reshape(N_GROUPS, H, BLK, 2 * BLK), bmap, H, name="band_bias_bwd")
                grads["rel_table"] = jnp.transpose(dtab[:, :, :H], (1, 0, 2)).reshape(REL_BUCKETS, N_GROUPS * H)
            tok = W.emit(3 * i, out_b)
    grad_x = _lincomb(terms, F32, name="grad_x")
    for dct in (gl, ga, gb):
        grads.update({k: v for k, v in dct.items() if k in mats})
    return loss, grad_x, grads


def _my_index():
    return 4 * lax.axis_index("x") + 2 * lax.axis_index("y") + lax.axis_index("c")


HBM_SPEC = pl.BlockSpec(memory_space=pltpu.HBM)


def _block(ref, k, n, axis):
    off = pl.multiple_of(k * n, n)
    return ref.at[pl.ds(off, n), :] if axis == 0 else ref.at[:, pl.ds(off, n)]


SEM_SPEC =pl.BlockSpec(memory_space=pltpu.SEMAPHORE)
FLOWING = pltpu.SideEffectType.DATAFLOW_SIDE_EFFECTING


def _peers(x, y, c):
    return [(1 - x if k & 4 else x, 1 - y if k & 2 else y, 1 - c if k & 1 else c) for k in range(1, N_DEV)]


def _ends(src_ref, land_ref, peer_index, me, n, axis, gather):
    if gather:
        return src_ref, _block(land_ref, me, n, axis)
    return _block(src_ref, peer_index, n, axis), land_ref.at[me]


def _send_start(groups, gather, *, name):
    flat = [(g, j, mat, axis) for g, items in enumerate(groups) for j, (mat, axis) in enumerate(items)]
    M, G = len(flat), len(groups)
    lands, ns = [], []
    for _, _, mat, axis in flat:
        A, B = mat.shape
        if gather:
            lands.append((A * N_DEV, B) if axis == 0 else (A, B * N_DEV))
            ns.append(A if axis == 0 else B)
        else:
            lands.append((N_DEV, A // N_DEV, B) if axis == 0 else (N_DEV, A, B // N_DEV))
            ns.append(A // N_DEV if axis == 0 else B // N_DEV)

    def body(*refs):
        src_refs, land_refs, sems = refs[:M], refs[M:2 * M], refs[2 * M:2 * M + 3 * G]
        token = refs[-1]
        x, y, c = lax.axis_index("x"), lax.axis_index("y"), lax.axis_index("c")
        me = 4 * x + 2 * y + c
        for i, (g, j, _, axis) in enumerate(flat):
            for k, (px, py, pc) in enumerate(_peers(x, y, c)):
                s, d = _ends(src_refs[i], land_refs[i], 4 * px + 2 * py + pc, me, ns[i], axis, gather)
                pltpu.make_async_remote_copy(
                    src_ref=s, dst_ref=d, send_sem=sems[3 * g].at[7 * j + k], recv_sem=sems[3 * g + 1].at[7 * j + k],
                    device_id=(px, py, pc), device_id_type=MESH).start()
            s, d = _ends(src_refs[i], land_refs[i], me, me, ns[i], axis, gather)
            pltpu.make_async_copy(s, d, sems[3 * g + 2].at[j]).start()
        token[...] = jnp.zeros_like(token)

    sem_shapes = []
    for items in groups:
        sem_shapes += [pltpu.SemaphoreType.DMA((7 * len(items),))] * 2 + [pltpu.SemaphoreType.DMA((len(items),))]
    outs = pl.pallas_call(
        body, name=name,
        out_shape=(*sem_shapes, *[pltpu.HBM(m.shape, m.dtype) for _, _, m, _ in flat],
                   *[pltpu.HBM(shp, m.dtype) for shp, (_, _, m, _) in zip(lands, flat)],
                   jax.ShapeDtypeStruct((8, LANES), F32)),
        in_specs=[HBM_SPEC] * (2 * M),
        out_specs=(*[SEM_SPEC] * (3 * G), *[HBM_SPEC] * (2 * M), pl.BlockSpec(memory_space=pltpu.VMEM)),
        input_output_aliases={i: 3 * G + i for i in range(2 * M)},
        compiler_params=pltpu.CompilerParams(has_side_effects=FLOWING),
    )(*[pltpu.with_memory_space_constraint(m, pltpu.HBM) for _, _, m, _ in flat],
      *[pltpu.with_memory_space_constraint(lax.empty(shp, m.dtype), pltpu.HBM)
        for shp, (_, _, m, _) in zip(lands, flat)])
    handles = []
    for g in range(G):
        idx = [i for i, f in enumerate(flat) if f[0] == g]
        handles.append((outs[3 * g], outs[3 * g + 1], outs[3 * g + 2], [outs[3 * G + i] for i in idx],
                        [outs[3 * G + M + i] for i in idx], [flat[i][3] for i in idx]))
    return handles, outs[-1]


def _send_wait(handle, gather, after, *, name):
    send_sems, recv_sems, local_sems, mats, lands, axes = handle
    n_m = len(mats)
    ns = []
    for mat, land, axis in zip(mats, lands, axes):
        ns.append(mat.shape[axis] if gather else land.shape[1 + axis])

    def body(*refs):
        src_refs, land_refs = refs[:n_m], refs[n_m:2 * n_m]
        ssem, rsem, lsem = refs[2 * n_m:2 * n_m + 3]
        x, y, c = lax.axis_index("x"), lax.axis_index("y"), lax.axis_index("c")
        me = 4 * x + 2 * y + c
        for j in range(n_m):
            for k, (px, py, pc) in enumerate(_peers(x, y, c)):
                s, d = _ends(src_refs[j], land_refs[j], 4 * px + 2 * py + pc, me, ns[j], axes[j], gather)
                cp = pltpu.make_async_remote_copy(
                    src_ref=s, dst_ref=d, send_sem=ssem.at[7 * j + k], recv_sem=rsem.at[7 * j + k],
                    device_id=(px, py, pc), device_id_type=MESH)
                cp.wait_send()
                cp.wait_recv()
            s, d = _ends(src_refs[j], land_refs[j], me, me, ns[j], axes[j], gather)
            pltpu.make_async_copy(s, d, lsem.at[j]).wait()

    outs = pl.pallas_call(
        body, name=name,
        out_shape=(*[pltpu.HBM(m.shape, m.dtype) for m in mats], *[pltpu.HBM(l.shape, l.dtype) for l in lands]),
        in_specs=[HBM_SPEC] * (2 * n_m) + [SEM_SPEC] * 3 + [pl.BlockSpec(memory_space=pl.ANY)],
        out_specs=tuple([HBM_SPEC] * (2 * n_m)),
        input_output_aliases={i: i for i in range(2 * n_m)},
        compiler_params=pltpu.CompilerParams(has_side_effects=FLOWING),
    )(*mats, *lands, send_sems, recv_sems, local_sems, after)
    return list(outs[n_m:])


def _sum_parts(parts, *, name):
    n, R, C = parts.shape
    rb = _pick(R, 512) if R % LANES == 0 else R

    def body(p_ref, o_ref):
        acc = p_ref[0].astype(F32)
        for k in range(1, n):
            acc = acc + p_ref[k].astype(F32)
        o_ref[...] = acc

    return pl.pallas_call(
        body, grid=(R // rb,), in_specs=[pl.BlockSpec((n, rb, C), lambda i: (0, i, 0))],
        out_specs=pl.BlockSpec((rb, C), lambda i: (i, 0)),
        out_shape=jax.ShapeDtypeStruct((R, C), F32),
        compiler_params=_params(("parallel",)), name=name,
    )(parts)


def _adamw(w, m, v, parts, *, name):
    L, R, C = w.shape
    n = parts[0].shape[0]
    cap = max(16, VMEM_LIMIT // 2 // (2 * L * n * C * parts[0].dtype.itemsize))
    rb = max([r for r in range(16, min(R, cap) + 1, 16) if R % r == 0], default=R)

    def body(w_ref, m_ref, v_ref, *rest):
        p_refs = rest[:L]
        g_ref, d_ref, nm_ref, nv_ref = rest[L:]
        for l in range(L):
            @pl.when(pl.program_id(0) == l)
            def _(p_ref=p_refs[l]):
                g = p_ref[0].astype(F32)
                for k in range(1, n):
                    g = g + p_ref[k].astype(F32)
                mn = ADAM_B1 * m_ref[...] + (1.0 - ADAM_B1) * g
                vn = ADAM_B2 * v_ref[...] + (1.0 - ADAM_B2) * jnp.square(g)
                m_hat = mn / (1.0 - ADAM_B1 ** ADAM_STEP)
                v_hat = vn / (1.0 - ADAM_B2 ** ADAM_STEP)
                g_ref[...] = g
                d_ref[...] = -ADAM_LR * (m_hat / (jnp.sqrt(v_hat) + ADAM_EPS) + ADAM_WD * w_ref[...])
                nm_ref[...] = mn
                nv_ref[...] = vn

    row = pl.BlockSpec((None, rb, C), lambda l, i: (l, i, 0))
    part = lambda k: pl.BlockSpec((n, rb, C), lambda l, i: (0, jnp.where(l == k, i, 0), 0))
    return pl.pallas_call(
        body, grid=(L, R // rb), in_specs=[row, row, row] + [part(k) for k in range(L)],
        out_specs=[row] * 4, out_shape=[jax.ShapeDtypeStruct((L, R, C), F32)] * 4,
        compiler_params=_params(("arbitrary", "arbitrary")), name=name,
    )(w, m, v, *parts)


BIG = (("a_w_in", "a_w_in", 1, False), ("a_w_out", "a_w_out", 0, False), ("kv_w", "kv_w", 0, False),
       ("b_w_q", "b_w_q_t", 0, True), ("b_w_o", "b_w_o", 1, False), ("ffn_w_up", "ffn_w_up_t", 0, True),
       ("ffn_w_down", "ffn_w_down", 0, False))
SMALL_SHARDED = (("a_ln_g", 1), ("a_ln_b", 1), ("ffn_conv_w", 2), ("ln_g", 2), ("ln_b", 2))
REPLICATED = ("a_w_s", "a_b_s", "rel_table", "ffn_conv_b")


def _pack_rows(arrs, lead=0):
    lshape = arrs[0].shape[:lead]
    p = jnp.concatenate([a.reshape(*lshape, -1, LANES) for a in arrs], axis=lead)
    pad = -p.shape[lead] % 8
    return jnp.pad(p, [(0, 0)] * lead + [(0, pad), (0, 0)])


def _unpack_rows(packed, shapes, lead=0):
    lshape = packed.shape[:lead]
    out, off = [], 0
    for shp in shapes:
        r = int(np.prod(shp)) // LANES
        out.append(lax.slice_in_dim(packed, off, off + r, axis=lead).reshape(*lshape, *shp))
        off += r
    return out


def _as_mats(a, transposed):
    a = a[None] if a.ndim == 2 else a
    return jnp.swapaxes(a, 1, 2) if transposed else a


def _merge_shards(stacked, axis):
    a = jnp.moveaxis(stacked, 0, axis)
    shp = list(a.shape)
    return a.reshape(shp[:axis] + [shp[axis] * shp[axis + 1]] + shp[axis + 2:])


def _split_shards(full, axis):
    shp = list(full.shape)
    a = full.reshape(shp[:axis] + [N_DEV, shp[axis] // N_DEV] + shp[axis + 1:])
    return jnp.moveaxis(a, axis, 0)


def kernel(x, a_w_in, a_ln_g, a_ln_b, a_w_s, a_b_s, a_w_out, kv_w, b_w_q, b_w_o, rel_table, ffn_w_up, ffn_conv_w, ffn_conv_b, ffn_w_down, ln_g, ln_b, loss_target, m_a_w_in, m_a_ln_g, m_a_ln_b, m_a_w_s, m_a_b_s, m_a_w_out, m_kv_w, m_b_w_q, m_b_w_o, m_rel_table, m_ffn_w_up, m_ffn_conv_w, m_ffn_conv_b, m_ffn_w_down, m_ln_g, m_ln_b, v_a_w_in, v_a_ln_g, v_a_ln_b, v_a_w_s, v_a_b_s, v_a_w_out, v_kv_w, v_b_w_q, v_b_w_o, v_rel_table, v_ffn_w_up, v_ffn_conv_w, v_ffn_conv_b, v_ffn_w_down, v_ln_g, v_ln_b):
    names = ["a_w_in", "a_ln_g", "a_ln_b", "a_w_s", "a_b_s", "a_w_out", "kv_w", "b_w_q", "b_w_o", "rel_table",
             "ffn_w_up", "ffn_conv_w", "ffn_conv_b", "ffn_w_down", "ln_g", "ln_b"]
    w = dict(zip(names, (a_w_in, a_ln_g, a_ln_b, a_w_s, a_b_s, a_w_out, kv_w, b_w_q, b_w_o, rel_table,
                         ffn_w_up, ffn_conv_w, ffn_conv_b, ffn_w_down, ln_g, ln_b)))
    m = dict(zip(names, (m_a_w_in, m_a_ln_g, m_a_ln_b, m_a_w_s, m_a_b_s, m_a_w_out, m_kv_w, m_b_w_q, m_b_w_o,
                         m_rel_table, m_ffn_w_up, m_ffn_conv_w, m_ffn_conv_b, m_ffn_w_down, m_ln_g, m_ln_b)))
    v = dict(zip(names, (v_a_w_in, v_a_ln_g, v_a_ln_b, v_a_w_s, v_a_b_s, v_a_w_out, v_kv_w, v_b_w_q, v_b_w_o,
                         v_rel_table, v_ffn_w_up, v_ffn_conv_w, v_ffn_conv_b, v_ffn_w_down, v_ln_g, v_ln_b)))
    small_names = [n for n, _ in SMALL_SHARDED]
    small_shapes = [w[n].shape for n in small_names]
    rep_shapes = [w[n].shape for n in REPLICATED]

    axis_of = {key: axis for _, key, axis, _ in BIG}
    src = {}
    for n, key, axis, tr in BIG:
        loc = _as_mats(w[n], tr).astype(BF16)
        for l in range(loc.shape[0]):
            src[(key, l)] = loc[l]
    order = []
    for i in range(DEPTH):
        if i < N_A:
            order += [[("a_w_in", i)], [("a_w_out", i)]]
        else:
            order += [([("kv_w", 0)] if i == N_A else []) + [("b_w_q_t", i - N_A)], [("b_w_o", i - N_A)]]
        order += [[("ffn_w_up_t", i)], [("ffn_w_down", i)]]
    small_src = _pack_rows([w[n] for n in small_names])
    srows = small_src.shape[0]
    handles, _ = _send_start([[(small_src, 0)]] + [[(src[kl], axis_of[kl[0]]) for kl in grp] for grp in order],
                             True, name="gather_start")
    small_all = _send_wait(handles[0], True, x, name="gather_wait_small")[0]
    small_st = _unpack_rows(small_all.reshape(N_DEV, srows, LANES), small_shapes, lead=1)
    base = {n: w[n] for n in REPLICATED}
    for (n, ax), st in zip(SMALL_SHARDED, small_st):
        base[n] = _merge_shards(st, ax)
    for n, key, _, tr in BIG:
        base[key] = [None] * (1 if w[n].ndim == 2 else w[n].shape[0])

    def fetch(group, after):
        mats = _send_wait(handles[1 + group], True, after, name=f"gather_wait_{group}")
        return dict(zip(order[group], mats))

    sent = {}

    def emit(group, mats):
        keys = list(mats)
        hs, token = _send_start([[(mats[kl], axis_of[kl[0]]) for kl in keys]], False, name=f"exchange_start_{group}")
        sent[group] = (keys, hs[0])
        return token

    small_sent = []

    def emit_small(grads):
        small_pack = _pack_rows([_split_shards(grads[n], ax) for n, ax in SMALL_SHARDED], lead=1)
        rest = _pack_rows([grads[n] for n in REPLICATED[1:]])
        mine = jnp.concatenate([small_pack.reshape(N_DEV * srows, LANES), rest], axis=0)
        gating = grads[REPLICATED[0]].reshape(-1, LANES).astype(BF16)
        hs, token = _send_start([[(mine, 0), (gating, 0)]], True, name="small_grads_start")
        small_sent.append(hs[0])
        return token

    loss, grad_x, grads = _local_step(x[0], loss_target[0], _Weights(base, fetch, emit, emit_small))
    loss = lax.psum(loss, ("x", "y", "c"))
    out = {}

    landed = {}
    last = grad_x
    left = lambda e: min(g for g, (keys, _) in sent.items() if any(k[0] == e[1] for k in keys))
    for n, key, axis, tr in sorted(BIG, key=left, reverse=True):
        shp = w[n].shape
        for group in sorted(sent, reverse=True):
            keys, h = sent[group]
            if keys[0] not in landed and any(k[0] == key for k in keys):
                landed.update(zip(keys, _send_wait(h, False, last, name=f"exchange_wait_{group}")))
        parts = [landed[(key, l)] for l in range(1 if len(shp) == 2 else shp[0])]
        res = _adamw(_as_mats(w[n], tr), _as_mats(m[n], tr), _as_mats(v[n], tr), parts, name=f"adamw_{n}")
        out[n] = [(jnp.swapaxes(r, 1, 2) if tr else r).reshape(shp) for r in res]
        last = res[0]

    allp, allg = _send_wait(small_sent[0], True, last, name="small_grads_wait")
    gsum = _sum_parts(allp.reshape(N_DEV, -1, LANES), name="sum_small_grads")
    gating = _sum_parts(allg.reshape(N_DEV, -1, LANES), name="sum_gating_grads")
    g_small = lax.dynamic_slice_in_dim(gsum, _my_index() * srows, srows, axis=0)
    pack_sr = lambda d: jnp.concatenate([_pack_rows([d[n] for n in small_names]),
                                         _pack_rows([d[n] for n in REPLICATED])], axis=0)
    n_rest = sum(int(np.prod(s)) for s in rep_shapes[1:]) // LANES
    gs_in = jnp.concatenate([g_small, gating, gsum[N_DEV * srows:N_DEV * srows + n_rest]], axis=0)
    gs_in = jnp.pad(gs_in, ((0, pack_sr(w).shape[0] - gs_in.shape[0]), (0, 0)))[None]
    res = _adamw(pack_sr(w)[None], pack_sr(m)[None], pack_sr(v)[None], [gs_in], name="adamw_small")
    for n, vals in zip(small_names, zip(*[_unpack_rows(r[0, :srows], small_shapes) for r in res])):
        out[n] = list(vals)
    for n, vals in zip(REPLICATED, zip(*[_unpack_rows(r[0, srows:], rep_shapes) for r in res])):
        out[n] = list(vals)

    return (loss, grad_x[None], *[out[n][0] for n in names], *[out[n][1] for n in names],
            *[out[n][2] for n in names], *[out[n][3] for n in names])
```

```python
import math

import numpy as np
import jax
import jax.numpy as jnp
from jax import lax
from jax.experimental import pallas as pl
from jax.experimental.pallas import tpu as pltpu

F32 = jnp.float32
BF16 = jnp.bfloat16
ACT = jnp.bfloat16
MESH = pl.DeviceIdType.MESH

N_DEV = 8
DEPTH = 4
N_A = 2
CHUNK = 128
BLK = 128
HEAD_DIM = 64
DILATED_GROUPS = ((128, 1), (512, 4), (2048, 16))
N_GROUPS = 3
REL_BUCKETS = 32
REL_MAX_DIST = 2048
ALPHA = (2 * DEPTH) ** 0.25
LN_EPS = 1e-5
NEG = -1e30
ADAM_LR = 0.001
ADAM_B1 = 0.9
ADAM_B2 = 0.999
ADAM_EPS = 1e-08
ADAM_WD = 0.01
ADAM_STEP = 10

LANES = 128
VMEM_LIMIT = 56 * 1024 * 1024
MM_TILE_CAP = 1408
MM_VMEM_BUDGET = 46 * 1024 * 1024
INV_SQRT2 = 1.0 / math.sqrt(2.0)
INV_SQRT_2PI = 1.0 / math.sqrt(2.0 * math.pi)


def _pick(n, cap):
    best = None
    for t in range(LANES, min(n, cap) + 1, LANES):
        if n % t == 0:
            best = t
    return best if best is not None else n


def _params(sem):
    return pltpu.CompilerParams(dimension_semantics=sem, vmem_limit_bytes=VMEM_LIMIT)


def _ordered(body, in_specs, args, after):
    if after is None:
        return body, list(in_specs), tuple(args)
    return (lambda _, *refs: body(*refs)), [pl.BlockSpec(memory_space=pl.ANY), *in_specs], (after, *args)


def _gelu(x):
    return 0.5 * x * (1.0 + lax.erf(x * INV_SQRT2))


def _gelu_grad(x):
    return 0.5 * (1.0 + lax.erf(x * INV_SQRT2)) + x * jnp.exp(-0.5 * x * x) * INV_SQRT_2PI


def _mm(a, b, *, ta=False, tb=False, out_dtype=F32, scale=None, after=None, name):
    halves = isinstance(a, tuple)
    parts = 1 if halves or a.ndim == 2 else a.shape[0]
    ash = (a[0].shape[0], 2 * a[0].shape[1]) if halves else (a.shape if parts == 1
                                                               else (a.shape[1], parts * a.shape[2]))
    if ta:
        K, M = ash
    else:
        M, K = ash
    if tb:
        N, Kb = b.shape
    else:
        Kb, N = b.shape
    assert K == Kb, (ash, b.shape, ta, tb)
    split = 2 if halves else parts
    tm = _pick(M // split if ta else M, MM_TILE_CAP)
    tn = _pick(N, MM_TILE_CAP)
    kspan = K if ta or split == 1 else K // split
    abytes = (a[0] if halves else a).dtype.itemsize * (2 if halves else 1)
    fixed = 2 * tm * tn * jnp.dtype(out_dtype).itemsize + tm * tn * 4
    fits = [t for t in range(LANES, kspan + 1, LANES)
            if kspan % t == 0 and 2 * t * (tm * abytes + tn * b.dtype.itemsize) + fixed <= MM_VMEM_BUDGET]
    tk = max(fits) if fits else _pick(kspan, MM_TILE_CAP)
    nk = K // tk
    nh = (M // split // tm if ta else K // split // tk) if split > 1 else 0
    dn = (((0 if ta else 1,), (1 if tb else 0,)), ((), ()))

    def body(*refs):
        n_tail = 3 if nk > 1 else 2
        a_refs, b_ref, o_ref = refs[:-n_tail], refs[-n_tail], refs[-n_tail + 1]
        k = pl.program_id(2)

        def finish(r):
            if scale is not None:
                r = r * scale
            o_ref[...] = r.astype(out_dtype)

        def accumulate(a_ref):
            part = lax.dot_general(a_ref[...].astype(BF16), b_ref[...].astype(BF16), dn,
                                   preferred_element_type=F32)
            if nk == 1:
                finish(part)
                return
            acc_ref = refs[-1]

            @pl.when(k == 0)
            def _():
                acc_ref[...] = part

            @pl.when((k > 0) & (k < nk - 1))
            def _():
                acc_ref[...] += part

            @pl.when(k == nk - 1)
            def _():
                finish(acc_ref[...] + part)

        if halves:
            first = (pl.program_id(0) if ta else k) < nh
            pl.when(first)(lambda: accumulate(a_refs[0]))
            pl.when(jnp.logical_not(first))(lambda: accumulate(a_refs[1]))
        else:
            accumulate(a_refs[0])

    if halves and ta:
        a_specs = [pl.BlockSpec((tk, tm), lambda i, j, k: (jnp.where(i < nh, k, 0), jnp.minimum(i, nh - 1))),
                   pl.BlockSpec((tk, tm), lambda i, j, k: (jnp.where(i >= nh, k, 0), jnp.maximum(i - nh, 0)))]
    elif halves:
        a_specs = [pl.BlockSpec((tm, tk), lambda i, j, k: (i, jnp.minimum(k, nh - 1))),
                   pl.BlockSpec((tm, tk), lambda i, j, k: (i, jnp.maximum(k - nh, 0)))]
    elif parts > 1:
        a_specs = [pl.BlockSpec((None, tk, tm), lambda i, j, k: (i // nh, k, i % nh)) if ta
                   else pl.BlockSpec((None, tm, tk), lambda i, j, k: (k // nh, i, k % nh))]
    else:
        a_specs = [pl.BlockSpec((tk, tm), lambda i, j, k: (k, i)) if ta
                   else pl.BlockSpec((tm, tk), lambda i, j, k: (i, k))]
    b_spec = (pl.BlockSpec((tn, tk), lambda i, j, k: (j, k)) if tb
              else pl.BlockSpec((tk, tn), lambda i, j, k: (k, j)))
    body, in_specs, args = _ordered(body, [*a_specs, b_spec], (*(a if halves else (a,)), b), after)
    return pl.pallas_call(
        body, grid=(M // tm, N // tn, nk), in_specs=in_specs,
        out_specs=pl.BlockSpec((tm, tn), lambda i, j, k: (i, j)),
        out_shape=jax.ShapeDtypeStruct((M, N), out_dtype),
        scratch_shapes=[pltpu.VMEM((tm, tn), F32)] if nk > 1 else [],
        compiler_params=_params(("parallel", "parallel", "arbitrary")), name=name,
    )(*args)


def _add_ln_fwd(x, h, g, b, *, wide, name):
    T, D = x.shape
    rb = _pick(T, 1024)

    def body(x_ref, h_ref, g_ref, b_ref, *o_refs):
        pre = ALPHA * x_ref[...].astype(F32) + h_ref[...].astype(F32)
        mu = jnp.mean(pre, axis=1, keepdims=True)
        cen = pre - mu
        var = jnp.mean(cen * cen, axis=1, keepdims=True)
        y = cen * lax.rsqrt(var + LN_EPS) * g_ref[...] + b_ref[...]
        for o_ref in o_refs:
            o_ref[...] = y.astype(o_ref.dtype)

    row = pl.BlockSpec((rb, D), lambda i: (i, 0))
    vec = pl.BlockSpec((1, D), lambda i: (0, 0))
    dtypes = [F32, BF16] if wide else [BF16]
    return pl.pallas_call(
        body, grid=(T // rb,), in_specs=[row, row, vec, vec], out_specs=[row] * len(dtypes),
        out_shape=[jax.ShapeDtypeStruct((T, D), dt) for dt in dtypes],
        compiler_params=_params(("parallel",)), name=name,
    )(x, h, g.reshape(1, D), b.reshape(1, D))


def _add_ln_bwd(x, h, g, terms, *, after=None, name):
    T, D = x.shape
    rb = _pick(T, 1024)
    coefs = [c for c, _ in terms]
    nt = len(terms)

    def body(*refs):
        x_ref, h_ref, g_ref = refs[:3]
        t_refs = refs[3:3 + nt]
        dpb_ref, dg_ref, db_ref = refs[3 + nt:]
        dy = None
        for c, r in zip(coefs, t_refs):
            v = r[...].astype(F32) if c == 1.0 else c * r[...].astype(F32)
            dy = v if dy is None else dy + v
        pre = ALPHA * x_ref[...].astype(F32) + h_ref[...].astype(F32)
        mu = jnp.mean(pre, axis=1, keepdims=True)
        cen = pre - mu
        var = jnp.mean(cen * cen, axis=1, keepdims=True)
        rstd = lax.rsqrt(var + LN_EPS)
        xhat = cen * rstd
        dxh = dy * g_ref[...]
        m1 = jnp.mean(dxh, axis=1, keepdims=True)
        m2 = jnp.mean(dxh * xhat, axis=1, keepdims=True)
        dpre = rstd * (dxh - m1 - xhat * m2)
        dpb_ref[...] = dpre.astype(BF16)
        dg = jnp.sum(dy * xhat, axis=0, keepdims=True)
        db = jnp.sum(dy, axis=0, keepdims=True)

        @pl.when(pl.program_id(0) == 0)
        def _():
            dg_ref[...] = dg
            db_ref[...] = db

        @pl.when(pl.program_id(0) > 0)
        def _():
            dg_ref[...] += dg
            db_ref[...] += db

    row = pl.BlockSpec((rb, D), lambda i: (i, 0))
    vec = pl.BlockSpec((1, D), lambda i: (0, 0))
    body, in_specs, args = _ordered(body, [row, row, vec] + [row] * nt,
                                    (x, h, g.reshape(1, D), *[a for _, a in terms]), after)
    return pl.pallas_call(
        body, grid=(T // rb,), in_specs=in_specs,
        out_specs=[row, vec, vec],
        out_shape=[jax.ShapeDtypeStruct((T, D), BF16),
                   jax.ShapeDtypeStruct((1, D), F32), jax.ShapeDtypeStruct((1, D), F32)],
        compiler_params=_params(("arbitrary",)), name=name,
    )(*args)


def _lincomb(terms, out_dtype, *, name):
    R, C = terms[0][1].shape
    rb = _pick(R, 1024)
    coefs = [c for c, _ in terms]
    nt = len(terms)

    def body(*refs):
        acc = None
        for c, r in zip(coefs, refs[:nt]):
            v = r[...].astype(F32)
            v = v if c == 1.0 else c * v
            acc = v if acc is None else acc + v
        refs[nt][...] = acc.astype(out_dtype)

    row = pl.BlockSpec((rb, C), lambda i: (i, 0))
    return pl.pallas_call(
        body, grid=(R // rb,), in_specs=[row] * nt, out_specs=row,
        out_shape=jax.ShapeDtypeStruct((R, C), out_dtype),
        compiler_params=_params(("parallel",)), name=name,
    )(*[a for _, a in terms])


def _loss_grad(y, tgt, *, name):
    T, D = y.shape
    rb = _pick(T, 1024)

    def body(y_ref, t_ref, dy_ref, l_ref):
        err = y_ref[...] - t_ref[...]
        dy_ref[...] = err * (1.0 / D)
        part = jnp.sum(jnp.sum(err * err, axis=1, keepdims=True), axis=0, keepdims=True) * (0.5 / D)
        part = jnp.broadcast_to(part, (1, LANES))

        @pl.when(pl.program_id(0) == 0)
        def _():
            l_ref[...] = part

        @pl.when(pl.program_id(0) > 0)
        def _():
            l_ref[...] += part

    row = pl.BlockSpec((rb, D), lambda i: (i, 0))
    return pl.pallas_call(
        body, grid=(T // rb,), in_specs=[row, row],
        out_specs=[row, pl.BlockSpec((1, LANES), lambda i: (0, 0))],
        out_shape=[jax.ShapeDtypeStruct((T, D), F32), jax.ShapeDtypeStruct((1, LANES), F32)],
        compiler_params=_params(("arbitrary",)), name=name,
    )(y, tgt)


def _sgu_fwd(zp, ln_g, ln_b, ws, bst, *, name):
    T, E2 = zp.shape
    E = E2 // 2
    G = ws.shape[0]
    cg = E // G
    rb = 4 * CHUNK

    def body(z_ref, g_ref, b_ref, ws_ref, bs_ref, y_ref):
        u = _gelu(z_ref[:, :E].astype(F32))
        v = _gelu(z_ref[:, E:].astype(F32))
        mu = jnp.mean(v, axis=1, keepdims=True)
        cen = v - mu
        var = jnp.mean(cen * cen, axis=1, keepdims=True)
        vn = (cen * lax.rsqrt(var + LN_EPS) * g_ref[...] + b_ref[...]).astype(BF16)
        for ci in range(rb // CHUNK):
            rows = slice(ci * CHUNK, (ci + 1) * CHUNK)
            for gi in range(G):
                cols = slice(gi * cg, (gi + 1) * cg)
                sv = jnp.dot(ws_ref[gi], vn[rows, cols], preferred_element_type=F32)
                sv = sv + bs_ref[:, gi:gi + 1]
                y_ref[rows, cols] = (u[rows, cols] * sv).astype(BF16)

    return pl.pallas_call(
        body, grid=(T // rb,),
        in_specs=[pl.BlockSpec((rb, E2), lambda i: (i, 0)),
                  pl.BlockSpec((1, E), lambda i: (0, 0)), pl.BlockSpec((1, E), lambda i: (0, 0)),
                  pl.BlockSpec((G, CHUNK, CHUNK), lambda i: (0, 0, 0)),
                  pl.BlockSpec((CHUNK, G), lambda i: (0, 0))],
        out_specs=pl.BlockSpec((rb, E), lambda i: (i, 0)),
        out_shape=jax.ShapeDtypeStruct((T, E), BF16),
        compiler_params=_params(("parallel",)), name=name,
    )(zp, ln_g.reshape(1, E), ln_b.reshape(1, E), ws, bst)


def _sgu_bwd(zp, dy, ln_g, ln_b, ws, wst, bst, *, after=None, name):
    T, E2 = zp.shape
    E = E2 // 2
    G = ws.shape[0]
    cg = E // G
    rb = CHUNK
    nsteps = T // rb

    def body(z_ref, dy_ref, g_ref, b_ref, ws_ref, wst_ref, bs_ref,
             dz_ref, dg_ref, db_ref, dws_ref, dbs_ref, dsv_acc):
        step = pl.program_id(0)

        @pl.when(step == 0)
        def _():
            dg_ref[...] = jnp.zeros_like(dg_ref)
            db_ref[...] = jnp.zeros_like(db_ref)
            dws_ref[...] = jnp.zeros_like(dws_ref)
            dsv_acc[...] = jnp.zeros_like(dsv_acc)

        zu = z_ref[:, :E].astype(F32)
        zv = z_ref[:, E:].astype(F32)
        u = _gelu(zu)
        v = _gelu(zv)
        mu = jnp.mean(v, axis=1, keepdims=True)
        cen = v - mu
        var = jnp.mean(cen * cen, axis=1, keepdims=True)
        rstd = lax.rsqrt(var + LN_EPS)
        xhat = cen * rstd
        vn = (xhat * g_ref[...] + b_ref[...]).astype(BF16)
        dyv = dy_ref[...].astype(F32)
        dsv = dyv * u
        dsv_acc[...] += dsv
        dsvb = dsv.astype(BF16)
        tril = (lax.broadcasted_iota(jnp.int32, (CHUNK, CHUNK), 0)
                >= lax.broadcasted_iota(jnp.int32, (CHUNK, CHUNK), 1))
        du_parts = []
        dvn_parts = []
        for gi in range(G):
            cols = slice(gi * cg, (gi + 1) * cg)
            sv = jnp.dot(ws_ref[gi], vn[:, cols], preferred_element_type=F32) + bs_ref[:, gi:gi + 1]
            du_parts.append(dyv[:, cols] * sv)
            dvn_parts.append(jnp.dot(wst_ref[gi], dsvb[:, cols], preferred_element_type=F32))
            dw = lax.dot_general(dsvb[:, cols], vn[:, cols], (((1,), (1,)), ((), ())),
                                 preferred_element_type=F32)
            dws_ref[gi] += jnp.where(tril, dw, 0.0)
        du = jnp.concatenate(du_parts, axis=1)
        dvn = jnp.concatenate(dvn_parts, axis=1)
        dg_ref[...] += jnp.sum(dvn * xhat, axis=0, keepdims=True)
        db_ref[...] += jnp.sum(dvn, axis=0, keepdims=True)
        dxh = dvn * g_ref[...]
        m1 = jnp.mean(dxh, axis=1, keepdims=True)
        m2 = jnp.mean(dxh * xhat, axis=1, keepdims=True)
        dv = rstd * (dxh - m1 - xhat * m2)
        dz_ref[:, :E] = (du * _gelu_grad(zu)).astype(BF16)
        dz_ref[:, E:] = (dv * _gelu_grad(zv)).astype(BF16)

        @pl.when(step == nsteps - 1)
        def _():
            lane = lax.broadcasted_iota(jnp.int32, (CHUNK, LANES), 1)
            out = jnp.zeros((CHUNK, LANES), F32)
            for gi in range(G):
                s = jnp.sum(dsv_acc[:, gi * cg:(gi + 1) * cg], axis=1, keepdims=True)
                out = jnp.where(lane == gi, s, out)
            dbs_ref[...] = out

    vecE = pl.BlockSpec((1, E), lambda i: (0, 0))
    wspec = pl.BlockSpec((G, CHUNK, CHUNK), lambda i: (0, 0, 0))
    body, in_specs, args = _ordered(
        body, [pl.BlockSpec((rb, E2), lambda i: (i, 0)), pl.BlockSpec((rb, E), lambda i: (i, 0)),
               vecE, vecE, wspec, wspec, pl.BlockSpec((CHUNK, G), lambda i: (0, 0))],
        (zp, dy, ln_g.reshape(1, E), ln_b.reshape(1, E), ws, wst, bst), after)
    return pl.pallas_call(
        body, grid=(nsteps,), in_specs=in_specs,
        out_specs=[pl.BlockSpec((rb, E2), lambda i: (i, 0)), vecE, vecE, wspec,
                   pl.BlockSpec((CHUNK, LANES), lambda i: (0, 0))],
        out_shape=[jax.ShapeDtypeStruct((T, E2), BF16), jax.ShapeDtypeStruct((1, E), F32),
                   jax.ShapeDtypeStruct((1, E), F32), jax.ShapeDtypeStruct((G, CHUNK, CHUNK), F32),
                   jax.ShapeDtypeStruct((CHUNK, LANES), F32)],
        scratch_shapes=[pltpu.VMEM((CHUNK, E), F32)],
        compiler_params=_params(("arbitrary",)), name=name,
    )(*args)


def _shift_down(x, k, row):
    return jnp.where(row >= k, pltpu.roll(x, k, 0), 0.0)


def _shift_up(x, k, row, T):
    return jnp.where(row < T - k, pltpu.roll(x, T - k, 0), 0.0)


def _conv3(x, w_ref, b_ref, row):
    return (w_ref[0:1, :] * _shift_down(x, 2, row) + w_ref[1:2, :] * _shift_down(x, 1, row)
            + w_ref[2:3, :] * x + b_ref[...])


def _convgate_fwd(hh, cw, cb, *, name):
    T, F2 = hh.shape
    F = F2 // 2
    ns = F // LANES

    def body(a_ref, g_ref, wa_ref, wg_ref, ba_ref, bg_ref, o_ref, ca_ref, cg_ref):
        row = lax.broadcasted_iota(jnp.int32, (T, LANES), 0)
        ca = _conv3(a_ref[...].astype(F32), wa_ref, ba_ref, row)
        cgv = _conv3(g_ref[...].astype(F32), wg_ref, bg_ref, row)
        o_ref[...] = (_gelu(ca) * cgv).astype(BF16)
        ca_ref[...] = ca.astype(ACT)
        cg_ref[...] = cgv.astype(ACT)

    sa = lambda r: pl.BlockSpec((r, LANES), lambda j: (0, j))
    sg = lambda r: pl.BlockSpec((r, LANES), lambda j: (0, j + ns))
    return pl.pallas_call(
        body, grid=(ns,), in_specs=[sa(T), sg(T), sa(3), sg(3), sa(1), sg(1)],
        out_specs=[sa(T)] * 3,
        out_shape=[jax.ShapeDtypeStruct((T, F), BF16), jax.ShapeDtypeStruct((T, F), ACT),
                   jax.ShapeDtypeStruct((T, F), ACT)],
        compiler_params=_params(("parallel",)), name=name,
    )(hh, hh, cw, cw, cb, cb)


def _convgate_bwd(hh, hca, hcg, dact, cw, *, name):
    T, F2 = hh.shape
    F = F2 // 2
    ns = F // LANES

    def body(a_ref, g_ref, ca_ref, cg_ref, d_ref, wa_ref, wg_ref,
             da_ref, dg_ref, dwa_ref, dwg_ref, dba_ref, dbg_ref):
        row = lax.broadcasted_iota(jnp.int32, (T, LANES), 0)
        d = d_ref[...].astype(F32)
        ca = ca_ref[...].astype(F32)
        cgv = cg_ref[...].astype(F32)
        cdf = 0.5 * (1.0 + lax.erf(ca * INV_SQRT2))
        dca = d * cgv * (cdf + ca * jnp.exp(-0.5 * ca * ca) * INV_SQRT_2PI)
        dcg = d * (ca * cdf)
        for x_ref, w_ref, dc, dx_ref, dw_ref, db_ref in (
                (a_ref, wa_ref, dca, da_ref, dwa_ref, dba_ref),
                (g_ref, wg_ref, dcg, dg_ref, dwg_ref, dbg_ref)):
            x = x_ref[...].astype(F32)
            up1, up2 = _shift_up(dc, 1, row, T), _shift_up(dc, 2, row, T)
            dx_ref[...] = (w_ref[2:3, :] * dc + w_ref[1:2, :] * up1 + w_ref[0:1, :] * up2).astype(BF16)
            dw_ref[0:1, :] = jnp.sum(up2 * x, axis=0, keepdims=True)
            dw_ref[1:2, :] = jnp.sum(up1 * x, axis=0, keepdims=True)
            dw_ref[2:3, :] = jnp.sum(dc * x, axis=0, keepdims=True)
            db_ref[...] = jnp.sum(dc, axis=0, keepdims=True)

    sa = lambda r: pl.BlockSpec((r, LANES), lambda j: (0, j))
    sg = lambda r: pl.BlockSpec((r, LANES), lambda j: (0, j + ns))
    return pl.pallas_call(
        body, grid=(ns,), in_specs=[sa(T), sg(T), sa(T), sa(T), sa(T), sa(3), sg(3)],
        out_specs=[sa(T), sa(T), sa(3), sa(3), sa(1), sa(1)],
        out_shape=[jax.ShapeDtypeStruct((T, F), BF16), jax.ShapeDtypeStruct((T, F), BF16),
                   jax.ShapeDtypeStruct((3, F), F32), jax.ShapeDtypeStruct((3, F), F32),
                   jax.ShapeDtypeStruct((1, F), F32), jax.ShapeDtypeStruct((1, F), F32)],
        compiler_params=_params(("parallel",)), name=name,
    )(hh, hh, hca, hcg, dact, cw, cw)


def _bucket_maps():
    iq = np.arange(BLK)[:, None]
    ik = np.arange(2 * BLK)[None, :]
    delta = iq + BLK - ik
    maps = []
    for win, dil in DILATED_GROUPS:
        n = np.clip(delta, 0, None) * dil
        max_exact = REL_BUCKETS // 2
        nf = np.maximum(n, 1).astype(np.float32)
        large = max_exact + (np.log(nf / np.float32(max_exact)) / np.float32(math.log(REL_MAX_DIST / max_exact))
                             * np.float32(REL_BUCKETS - max_exact)).astype(np.int32)
        large = np.minimum(large, REL_BUCKETS - 1)
        bucket = np.where(n < max_exact, n, large)
        valid = (delta >= 0) & (delta <= win // dil)
        maps.append(np.where(valid, bucket, -1).astype(np.int32))
    return np.stack(maps)


def _band_bias(rel_table, bmap, H, *, name):
    def body(t_ref, m_ref, o_ref):
        g = pl.program_id(0)
        bm = m_ref[0]
        for h in range(H):
            acc = jnp.full((BLK, 2 * BLK), NEG, F32)
            for b in range(REL_BUCKETS):
                acc = jnp.where(bm == b, t_ref[b, g * H + h], acc)
            o_ref[0, h] = acc

    return pl.pallas_call(
        body, grid=(N_GROUPS,),
        in_specs=[pl.BlockSpec(memory_space=pltpu.SMEM),
                  pl.BlockSpec((1, BLK, 2 * BLK), lambda g: (g, 0, 0))],
        out_specs=pl.BlockSpec((1, H, BLK, 2 * BLK), lambda g: (g, 0, 0, 0)),
        out_shape=jax.ShapeDtypeStruct((N_GROUPS, H, BLK, 2 * BLK), F32),
        compiler_params=_params(("parallel",)), name=name,
    )(rel_table, bmap)


def _band_bias_bwd(dbias, bmap, H, *, name):
    def body(d_ref, m_ref, o_ref):
        bm = m_ref[0]
        rowi = lax.broadcasted_iota(jnp.int32, (REL_BUCKETS, LANES), 0)
        lane = lax.broadcasted_iota(jnp.int32, (REL_BUCKETS, LANES), 1)
        out = jnp.zeros((REL_BUCKETS, LANES), F32)
        for h in range(H):
            dv = d_ref[0, h]
            for b in range(REL_BUCKETS):
                s = jnp.sum(jnp.sum(jnp.where(bm == b, dv, 0.0), axis=1, keepdims=True),
                            axis=0, keepdims=True)
                out = jnp.where((rowi == b) & (lane == h), s, out)
        o_ref[0] = out

    return pl.pallas_call(
        body, grid=(N_GROUPS,),
        in_specs=[pl.BlockSpec((1, H, BLK, 2 * BLK), lambda g: (g, 0, 0, 0)),
                  pl.BlockSpec((1, BLK, 2 * BLK), lambda g: (g, 0, 0))],
        out_specs=pl.BlockSpec((1, REL_BUCKETS, LANES), lambda g: (g, 0, 0)),
        out_shape=jax.ShapeDtypeStruct((N_GROUPS, REL_BUCKETS, LANES), F32),
        compiler_params=_params(("parallel",)), name=name,
    )(dbias, bmap)


def _head_masks():
    lane = lax.broadcasted_iota(jnp.int32, (BLK, LANES), 1)
    return (lane < HEAD_DIM, lane >= HEAD_DIM)


SUPER = DILATED_GROUPS[-1][1] * BLK


def _band_rows(it, d):
    r, j = it % d, it // d
    if d == 1:
        at = lambda blk: pl.ds(pl.multiple_of(blk * BLK, BLK), BLK)
    else:
        at = lambda blk: pl.ds(r + d * BLK * blk, BLK, stride=d)
    return at(j), at(jnp.maximum(j - 1, 0))


def _stack_heads(x, hm):
    zero = jnp.zeros_like(x)
    return jnp.concatenate([jnp.where(hm[0], x, zero), jnp.where(hm[1], x, zero)], axis=0)


def _band_loops(step, d, unroll):
    n_it = SUPER // BLK

    def run(lo, hi, inside):
        if hi > lo:
            def body(it, carry):
                step(it, inside)
                return carry
            lax.fori_loop(lo, hi, body, 0, unroll=max(u for u in range(1, unroll + 1) if (hi - lo) % u == 0))

    run(0, d, False)
    run(d, n_it, True)


def _last_rows(it, d):
    m = SUPER // (d * BLK)
    if d == 1:
        return pl.ds((m - 1) * BLK, BLK)
    return pl.ds(it % d + d * BLK * (m - 1), BLK, stride=d)


def _attn_fwd_all(q, kv, bias, *, name):
    T = q.shape[0]
    HD = kv.shape[1] // 2
    PP = HD // LANES
    NS = T // SUPER

    def body(q0, q1, q2, kp_ref, kc_ref, vp_ref, vc_ref, b_ref, o_ref, ob_ref, l_ref, og, lg):
        n = pl.program_id(1)
        col = lax.broadcasted_iota(jnp.int32, (2 * BLK, 2 * BLK), 1)
        hm = _head_masks()
        for g, (q_ref, (_, d)) in enumerate(zip((q0, q1, q2), DILATED_GROUPS)):
            def step(it, inside, g=g, q_ref=q_ref, d=d):
                cur, prv = _band_rows(it, d)
                qp = q_ref[cur, :].astype(BF16)
                if inside:
                    kprev, vprev = kc_ref[prv, :], vc_ref[prv, :]
                else:
                    last = _last_rows(it, d)
                    kprev, vprev = kp_ref[last, :], vp_ref[last, :]
                kc = jnp.concatenate([kprev.astype(BF16), kc_ref[cur, :].astype(BF16)], axis=0)
                vc = jnp.concatenate([vprev.astype(BF16), vc_ref[cur, :].astype(BF16)], axis=0)
                s = lax.dot_general(_stack_heads(qp, hm), kc, (((1,), (1,)), ((), ())),
                                    preferred_element_type=F32)
                s = s + b_ref[g].reshape(2 * BLK, 2 * BLK)
                if not inside:
                    s = jnp.where((n == 0) & (col < BLK), NEG, s)
                mx = jnp.max(s, axis=1, keepdims=True)
                e = jnp.exp(s - mx)
                den = jnp.sum(e, axis=1, keepdims=True)
                out = jnp.dot((e / den).astype(BF16), vc, preferred_element_type=F32)
                lse = mx + jnp.log(den)
                og.at[g][cur, :] = jnp.where(hm[0], out[:BLK], out[BLK:])
                lg.at[g][cur, :] = jnp.where(hm[0], lse[:BLK], lse[BLK:])

            _band_loops(step, d, 8)
        la, lb, lc = lg[0], lg[1], lg[2]
        mx = jnp.maximum(jnp.maximum(la, lb), lc)
        L = mx + jnp.log(jnp.exp(la - mx) + jnp.exp(lb - mx) + jnp.exp(lc - mx))
        o = jnp.exp(la - L) * og[0] + jnp.exp(lb - L) * og[1] + jnp.exp(lc - L) * og[2]
        o_ref[...] = o
        ob_ref[...] = o.astype(BF16)
        l_ref[...] = L

    blk = lambda f: pl.BlockSpec((SUPER, LANES), f)
    prev = lambda n: jnp.maximum(n - 1, 0)
    qspec = lambda g: blk(lambda p, n: (n, g * PP + p))
    return pl.pallas_call(
        body, grid=(PP, NS),
        in_specs=[qspec(0), qspec(1), qspec(2),
                  blk(lambda p, n: (prev(n), p)), blk(lambda p, n: (n, p)),
                  blk(lambda p, n: (prev(n), PP + p)), blk(lambda p, n: (n, PP + p)),
                  pl.BlockSpec((N_GROUPS, 2, BLK, 2 * BLK), lambda p, n: (0, p, 0, 0))],
        out_specs=[blk(lambda p, n: (n, p))] * 3,
        out_shape=[jax.ShapeDtypeStruct((T, HD), F32), jax.ShapeDtypeStruct((T, HD), BF16),
                   jax.ShapeDtypeStruct((T, HD), F32)],
        scratch_shapes=[pltpu.VMEM((N_GROUPS, SUPER, LANES), F32), pltpu.VMEM((N_GROUPS, SUPER, LANES), F32)],
        compiler_params=_params(("parallel", "parallel")), name=name,
    )(q, q, q, kv, kv, kv, kv, bias)


def _attn_bwd_all(q, kv, bias, do, o, L, *, after=None, name):
    T = q.shape[0]
    HD = kv.shape[1] // 2
    PP = HD // LANES
    H = HD // HEAD_DIM
    NS = T // SUPER

    def body(q0, q1, q2, kp_ref, kc_ref, vp_ref, vc_ref, b_ref, do_ref, o_ref, L_ref,
             dq_ref, dk_ref, dv_ref, db_ref, ck_ref, cv_ref):
        n = pl.program_id(1)

        @pl.when(n == 0)
        def _():
            db_ref[...] = jnp.zeros_like(db_ref)
            ck_ref[...] = jnp.zeros_like(ck_ref)
            cv_ref[...] = jnp.zeros_like(cv_ref)

        dk_ref[...] = ck_ref[...]
        dv_ref[...] = cv_ref[...]
        ck_ref[...] = jnp.zeros_like(ck_ref)
        cv_ref[...] = jnp.zeros_like(cv_ref)

        @pl.when(n < NS)
        def _():
            col = lax.broadcasted_iota(jnp.int32, (2 * BLK, 2 * BLK), 1)
            hm = _head_masks()
            for g, (q_ref, (_, d)) in enumerate(zip((q0, q1, q2), DILATED_GROUPS)):
                def step(it, inside, g=g, q_ref=q_ref, d=d):
                    cur, prv = _band_rows(it, d)
                    last = _last_rows(it, d)
                    qp = q_ref[cur, :].astype(BF16)
                    if inside:
                        kprev, vprev = kc_ref[prv, :], vc_ref[prv, :]
                    else:
                        kprev, vprev = kp_ref[last, :], vp_ref[last, :]
                    kc = jnp.concatenate([kprev.astype(BF16), kc_ref[cur, :].astype(BF16)], axis=0)
                    vc = jnp.concatenate([vprev.astype(BF16), vc_ref[cur, :].astype(BF16)], axis=0)
                    dop = do_ref[cur, :]
                    prod = dop * o_ref[cur, :]
                    Lp = L_ref[cur, :]
                    qs = _stack_heads(qp, hm)
                    dos = _stack_heads(dop.astype(BF16), hm)
                    lse = jnp.concatenate([Lp[:, 0:1], Lp[:, HEAD_DIM:HEAD_DIM + 1]], axis=0)
                    delta = jnp.concatenate([jnp.sum(jnp.where(hm[0], prod, 0.0), axis=1, keepdims=True),
                                             jnp.sum(jnp.where(hm[1], prod, 0.0), axis=1, keepdims=True)], axis=0)
                    s = lax.dot_general(qs, kc, (((1,), (1,)), ((), ())), preferred_element_type=F32)
                    s = s + b_ref[g].reshape(2 * BLK, 2 * BLK)
                    if not inside:
                        s = jnp.where((n == 0) & (col < BLK), NEG, s)
                    pr = jnp.exp(s - lse)
                    dp = lax.dot_general(dos, vc, (((1,), (1,)), ((), ())), preferred_element_type=F32)
                    ds = pr * (dp - delta)
                    db_ref[g] += ds.reshape(2, BLK, 2 * BLK)
                    dsb = ds.astype(BF16)
                    dqs = jnp.dot(dsb, kc, preferred_element_type=F32)
                    dkc = lax.dot_general(dsb, qs, (((0,), (0,)), ((), ())), preferred_element_type=F32)
                    dvc = lax.dot_general(pr.astype(BF16), dos, (((0,), (0,)), ((), ())),
                                          preferred_element_type=F32)
                    dq_ref.at[g][cur, :] = jnp.where(hm[0], dqs[:BLK], dqs[BLK:]) * (HEAD_DIM ** -0.5)
                    ck_ref[cur, :] += dkc[BLK:]
                    cv_ref[cur, :] += dvc[BLK:]
                    if inside:
                        ck_ref[prv, :] += dkc[:BLK]
                        cv_ref[prv, :] += dvc[:BLK]
                    else:
                        dk_ref[last, :] += dkc[:BLK]
                        dv_ref[last, :] += dvc[:BLK]

                _band_loops(step, d, 4)

    blk = lambda f: pl.BlockSpec((SUPER, LANES), f)
    cur = lambda n: jnp.minimum(n, NS - 1)
    prev = lambda n: jnp.maximum(jnp.minimum(n, NS - 1) - 1, 0)
    lag = lambda n: jnp.maximum(n - 1, 0)
    qspec = lambda g: blk(lambda p, n: (cur(n), g * PP + p))
    bspec = pl.BlockSpec((N_GROUPS, 2, BLK, 2 * BLK), lambda p, n: (0, p, 0, 0))
    body, in_specs, args = _ordered(
        body, [qspec(0), qspec(1), qspec(2),
               blk(lambda p, n: (prev(n), p)), blk(lambda p, n: (cur(n), p)),
               blk(lambda p, n: (prev(n), PP + p)), blk(lambda p, n: (cur(n), PP + p)),
               bspec, blk(lambda p, n: (cur(n), p)), blk(lambda p, n: (cur(n), p)),
               blk(lambda p, n: (cur(n), p))],
        (q, q, q, kv, kv, kv, kv, bias, do, o, L), after)
    return pl.pallas_call(
        body, grid=(PP, NS + 1), in_specs=in_specs,
        out_specs=[pl.BlockSpec((N_GROUPS, SUPER, LANES), lambda p, n: (0, cur(n), p)),
                   blk(lambda p, n: (lag(n), p)), blk(lambda p, n: (lag(n), p)), bspec],
        out_shape=[jax.ShapeDtypeStruct((N_GROUPS, T, HD), F32), jax.ShapeDtypeStruct((T, HD), F32),
                   jax.ShapeDtypeStruct((T, HD), F32),
                   jax.ShapeDtypeStruct((N_GROUPS, H, BLK, 2 * BLK), F32)],
        scratch_shapes=[pltpu.VMEM((SUPER, LANES), F32), pltpu.VMEM((SUPER, LANES), F32)],
        compiler_params=_params(("arbitrary", "arbitrary")), name=name,
    )(*args)


LAST_DOWN = 3 * DEPTH


class _Weights(dict):
    def __init__(self, base, fetch=None, emit=None, emit_small=None):
        super().__init__(base)
        self._fetch, self._emit, self._emit_small = fetch, emit, emit_small

    def fetch(self, group, after):
        if self._fetch is not None:
            for (key, layer), mat in self._fetch(group, after).items():
                self[key][layer] = mat

    def emit(self, group, mats):
        return None if self._emit is None else self._emit(group, mats)

    def emit_small(self, grads):
        return None if self._emit_small is None else self._emit_small(grads)


def _local_step(x, tgt, W):
    T, D = x.shape
    H = W["rel_table"].shape[1] // N_GROUPS
    HD = H * HEAD_DIM
    G = W["a_w_s"].shape[1]
    assert T % (DILATED_GROUPS[-1][1] * BLK) == 0

    tril = jnp.tril(jnp.ones((CHUNK, CHUNK), F32))
    bmap = jnp.asarray(_bucket_maps())
    bias = _band_bias(W["rel_table"], bmap, H, name="band_bias")

    saved = []
    xc, xcb = x, x.astype(BF16)
    kvb = None
    for i in range(DEPTH):
        s = {"x": xc, "xb": xcb}
        W.fetch(4 * i, xc)
        if i < N_A:
            ws_m = W["a_w_s"][i] * tril
            s["ws"] = ws_m.astype(BF16)
            s["wst"] = jnp.swapaxes(ws_m, 1, 2).astype(BF16)
            s["bst"] = W["a_b_s"][i].T
            s["zp"] = _mm(xcb, W["a_w_in"][i], out_dtype=ACT, name=f"a_in_{i}")
            s["y"] = _sgu_fwd(s["zp"], W["a_ln_g"][i], W["a_ln_b"][i], s["ws"], s["bst"], name=f"sgu_fwd_{i}")
            W.fetch(4 * i + 1, s["zp"])
            s["h"] = _mm(s["y"], W["a_w_out"][i], out_dtype=ACT, name=f"a_out_{i}")
        else:
            j = i - N_A
            if kvb is None:
                kvb = _mm(xcb, W["kv_w"][0], name="kv_proj")
            s["q"] = _mm(xcb, W["b_w_q_t"][j], tb=True, scale=HEAD_DIM ** -0.5, name=f"q_proj_{j}")
            s["o"], s["ob"], s["L"] = _attn_fwd_all(s["q"], kvb, bias, name=f"attn_fwd_{j}")
            W.fetch(4 * i + 1, s["q"])
            s["h"] = _mm(s["ob"], W["b_w_o"][j], out_dtype=ACT, name=f"o_proj_{j}")
        s["x1b"], = _add_ln_fwd(xc, s["h"], W["ln_g"][i, 0], W["ln_b"][i, 0], wide=False, name=f"ln1_fwd_{i}")
        s["x1"] = s["x1b"]
        W.fetch(4 * i + 2, s["x1"])
        s["hh"] = _mm(s["x1b"], W["ffn_w_up_t"][i], tb=True, out_dtype=ACT, name=f"ffn_up_{i}")
        s["cw"] = W["ffn_conv_w"][i]
        s["cb"] = W["ffn_conv_b"][i].reshape(1, -1)
        s["act"], s["hca"], s["hcg"] = _convgate_fwd(s["hh"], s["cw"], s["cb"], name=f"convgate_fwd_{i}")
        W.fetch(4 * i + 3, s["hh"])
        s["f"] = _mm(s["act"], W["ffn_w_down"][i], out_dtype=ACT, name=f"ffn_down_{i}")
        outs = _add_ln_fwd(s["x1"], s["f"], W["ln_g"][i, 1], W["ln_b"][i, 1], wide=i == DEPTH - 1,
                           name=f"ln2_fwd_{i}")
        xc, xcb = outs[0], outs[-1]
        saved.append(s)

    dy, lossv = _loss_grad(xc, tgt, name="loss_grad")
    loss = lossv[0, 0]

    gl = {k: [None] * DEPTH for k in ("ffn_w_up_t", "ffn_conv_w", "ffn_conv_b", "ffn_w_down", "ln_g", "ln_b")}
    ga = {k: [None] * N_A for k in ("a_w_in", "a_ln_g", "a_ln_b", "a_w_s", "a_b_s", "a_w_out")}
    gb = {k: [None] * (DEPTH - N_A) for k in ("b_w_q_t", "b_w_o")}
    mats = ("a_w_in", "a_w_out", "b_w_q_t", "b_w_o", "ffn_w_up_t", "ffn_w_down")
    dks, dvs, dbias = [], [], []
    grads = {}
    terms = [(1.0, dy)]
    tok = None
    small_keys = ("ffn_conv_w", "ffn_conv_b", "ln_g", "ln_b", "a_ln_g", "a_ln_b", "a_w_s", "a_b_s")
    for i in reversed(range(DEPTH)):
        s = saved[i]
        dp2b, dg2, db2 = _add_ln_bwd(s["x1"], s["f"], W["ln_g"][i, 1], terms, after=tok, name=f"ln2_bwd_{i}")
        dact = _mm(dp2b, W["ffn_w_down"][i], tb=True, out_dtype=ACT, name=f"ffn_down_dx_{i}")
        gl["ffn_w_down"][i] = _mm(s["act"], dp2b, ta=True, out_dtype=BF16, name=f"ffn_down_dw_{i}")
        out_f = {("ffn_w_down", i): gl["ffn_w_down"][i]}
        if i == 0:
            tok = W.emit(LAST_DOWN, out_f)
            out_f = {}
        dha, dhg, dwa, dwg, dba, dbg = _convgate_bwd(s["hh"], s["hca"], s["hcg"], dact, s["cw"],
                                                     name=f"convgate_bwd_{i}")
        dhh = (dha, dhg)
        gl["ffn_conv_w"][i] = jnp.concatenate([dwa, dwg], axis=1)
        gl["ffn_conv_b"][i] = jnp.concatenate([dba, dbg], axis=1)[0]
        dx1 = _mm(dhh, W["ffn_w_up_t"][i], out_dtype=ACT, after=tok, name=f"ffn_up_dx_{i}")
        gl["ffn_w_up_t"][i] = _mm(dhh, s["x1b"], ta=True, out_dtype=BF16, name=f"ffn_up_dw_{i}")
        out_f[("ffn_w_up_t", i)] = gl["ffn_w_up_t"][i]
        if i == 0:
            tok = W.emit(3 * i + 2, out_f)
        dp1b, dg1, db1 = _add_ln_bwd(s["x"], s["h"], W["ln_g"][i, 0], [(ALPHA, dp2b), (1.0, dx1)],
                                     after=tok, name=f"ln1_bwd_{i}")
        gl["ln_g"][i] = jnp.concatenate([dg1, dg2], axis=0)
        gl["ln_b"][i] = jnp.concatenate([db1, db2], axis=0)
        terms = [(ALPHA, dp1b)]
        if i < N_A:
            dyy = _mm(dp1b, W["a_w_out"][i], tb=True, out_dtype=ACT, name=f"a_out_dx_{i}")
            ga["a_w_out"][i] = _mm(s["y"], dp1b, ta=True, out_dtype=BF16, name=f"a_out_dw_{i}")
            if i == 0:
                tok = W.emit(3 * i + 1, {("a_w_out", i): ga["a_w_out"][i]})
            dzp, dlg, dlb, dws, dbs = _sgu_bwd(s["zp"], dyy, W["a_ln_g"][i], W["a_ln_b"][i], s["ws"],
                                               s["wst"], s["bst"], after=tok, name=f"sgu_bwd_{i}")
            ga["a_ln_g"][i], ga["a_ln_b"][i], ga["a_w_s"][i] = dlg[0], dlb[0], dws
            ga["a_b_s"][i] = dbs[:, :G].T
            if i == 0:
                for dct in (gl, ga):
                    grads.update({k: jnp.stack(v) for k, v in dct.items() if k in small_keys})
                tok = W.emit_small(grads)
            ga["a_w_in"][i] = _mm(s["xb"], dzp, ta=True, out_dtype=BF16, after=tok, name=f"a_in_dw_{i}")
            out_a = {("a_w_in", i): ga["a_w_in"][i]}
            if i > 0:
                out_a[("a_w_out", i)] = ga["a_w_out"][i]
                out_a.update(out_f)
            tok = W.emit(3 * i, out_a)
            terms.append((1.0, _mm(dzp, W["a_w_in"][i], tb=True, out_dtype=ACT, after=tok, name=f"a_in_dx_{i}")))
        else:
            j = i - N_A
            do = _mm(dp1b, W["b_w_o"][j], tb=True, name=f"o_proj_dx_{j}")
            gb["b_w_o"][j] = _mm(s["ob"], dp1b, ta=True, out_dtype=BF16, name=f"o_proj_dw_{j}")
            dq, dk_j, dv_j, db_j = _attn_bwd_all(s["q"], kvb, bias, do, s["o"], s["L"], after=tok,
                                                 name=f"attn_bwd_{j}")
            dks.append((1.0, dk_j))
            dvs.append((1.0, dv_j))
            dbias.append(db_j)
            terms.append((1.0, _mm(dq, W["b_w_q_t"][j], out_dtype=ACT, name=f"q_proj_dx_{j}")))
            gb["b_w_q_t"][j] = _mm(dq, s["xb"], ta=True, out_dtype=BF16, name=f"q_proj_dw_{j}")
            out_b = {("b_w_q_t", j): gb["b_w_q_t"][j], ("b_w_o", j): gb["b_w_o"][j], **out_f}
            if i == N_A:
                dkv = jnp.concatenate([_lincomb(dks, BF16, name="dk_sum"), _lincomb(dvs, BF16, name="dv_sum")],
                                      axis=1)
                terms.append((1.0, _mm(dkv, W["kv_w"][0], tb=True, out_dtype=ACT, name="kv_proj_dx")))
                grads["kv_w"] = [_mm(s["xb"], dkv, ta=True, out_dtype=BF16, name="kv_proj_dw")]
                out_b[("kv_w", 0)] = grads["kv_w"][0]
                dbt = _lincomb([(1.0, a.reshape(-1, 2 * BLK)) for a in dbias], F32, name="dbias_sum")
                dtab = _band_bias_bwd(dbt.reshape(N_GROUPS, H, BLK, 2 * BLK), bmap, H, name="band_bias_bwd")
                grads["rel_table"] = jnp.transpose(dtab[:, :, :H], (1, 0, 2)).reshape(REL_BUCKETS, N_GROUPS * H)
            tok = W.emit(3 * i, out_b)
    grad_x = _lincomb(terms, F32, name="grad_x")
    for dct in (gl, ga, gb):
        grads.update({k: v for k, v in dct.items() if k in mats})
    return loss, grad_x, grads


def _my_index():
    return 4 * lax.axis_index("x") + 2 * lax.axis_index("y") + lax.axis_index("c")


HBM_SPEC = pl.BlockSpec(memory_space=pltpu.HBM)


def _block(ref, k, n, axis):
    off = pl.multiple_of(k * n, n)
    return ref.at[pl.ds(off, n), :] if axis == 0 else ref.at[:, pl.ds(off, n)]


SEM_SPEC = pl.BlockSpec(memory_space=pltpu.SEMAPHORE)
FLOWING = pltpu.SideEffectType.DATAFLOW_SIDE_EFFECTING


def _peers(x, y, c):
    return [(1 - x if k & 4 else x, 1 - y if k & 2 else y, 1 - c if k & 1 else c) for k in range(1, N_DEV)]


def _ends(src_ref, land_ref, peer_index, me, n, axis, gather):
    if gather:
        return src_ref, _block(land_ref, me, n, axis)
    return _block(src_ref, peer_index, n, axis), land_ref.at[me]


def _send_start(groups, gather, *, name):
    flat = [(g, j, mat, axis) for g, items in enumerate(groups) for j, (mat, axis) in enumerate(items)]
    M, G = len(flat), len(groups)
    lands, ns = [], []
    for _, _, mat, axis in flat:
        A, B = mat.shape
        if gather:
            lands.append((A * N_DEV, B) if axis == 0 else (A, B * N_DEV))
            ns.append(A if axis == 0 else B)
        else:
            lands.append((N_DEV, A // N_DEV, B) if axis == 0 else (N_DEV, A, B // N_DEV))
            ns.append(A // N_DEV if axis == 0 else B // N_DEV)

    def body(*refs):
        src_refs, land_refs, sems = refs[:M], refs[M:2 * M], refs[2 * M:2 * M + 3 * G]
        token = refs[-1]
        x, y, c = lax.axis_index("x"), lax.axis_index("y"), lax.axis_index("c")
        me = 4 * x + 2 * y + c
        for i, (g, j, _, axis) in enumerate(flat):
            for k, (px, py, pc) in enumerate(_peers(x, y, c)):
                s, d = _ends(src_refs[i], land_refs[i], 4 * px + 2 * py + pc, me, ns[i], axis, gather)
                pltpu.make_async_remote_copy(
                    src_ref=s, dst_ref=d, send_sem=sems[3 * g].at[7 * j + k], recv_sem=sems[3 * g + 1].at[7 * j + k],
                    device_id=(px, py, pc), device_id_type=MESH).start()
            s, d = _ends(src_refs[i], land_refs[i], me, me, ns[i], axis, gather)
            pltpu.make_async_copy(s, d, sems[3 * g + 2].at[j]).start(priority=1)
        token[...] = jnp.zeros_like(token)

    sem_shapes = []
    for items in groups:
        sem_shapes += [pltpu.SemaphoreType.DMA((7 * len(items),))] * 2 + [pltpu.SemaphoreType.DMA((len(items),))]
    outs = pl.pallas_call(
        body, name=name,
        out_shape=(*sem_shapes, *[pltpu.HBM(m.shape, m.dtype) for _, _, m, _ in flat],
                   *[pltpu.HBM(shp, m.dtype) for shp, (_, _, m, _) in zip(lands, flat)],
                   jax.ShapeDtypeStruct((8, LANES), F32)),
        in_specs=[HBM_SPEC] * (2 * M),
        out_specs=(*[SEM_SPEC] * (3 * G), *[HBM_SPEC] * (2 * M), pl.BlockSpec(memory_space=pltpu.VMEM)),
        input_output_aliases={i: 3 * G + i for i in range(2 * M)},
        compiler_params=pltpu.CompilerParams(has_side_effects=FLOWING),
    )(*[pltpu.with_memory_space_constraint(m, pltpu.HBM) for _, _, m, _ in flat],
      *[pltpu.with_memory_space_constraint(lax.empty(shp, m.dtype), pltpu.HBM)
        for shp, (_, _, m, _) in zip(lands, flat)])
    handles = []
    for g in range(G):
        idx = [i for i, f in enumerate(flat) if f[0] == g]
        handles.append((outs[3 * g], outs[3 * g + 1], outs[3 * g + 2], [outs[3 * G + i] for i in idx],
                        [outs[3 * G + M + i] for i in idx], [flat[i][3] for i in idx]))
    return handles, outs[-1]


def _send_wait(handle, gather, after, *, name):
    send_sems, recv_sems, local_sems, mats, lands, axes = handle
    n_m = len(mats)
    ns = []
    for mat, land, axis in zip(mats, lands, axes):
        ns.append(mat.shape[axis] if gather else land.shape[1 + axis])

    def body(*refs):
        src_refs, land_refs = refs[:n_m], refs[n_m:2 * n_m]
        ssem, rsem, lsem = refs[2 * n_m:2 * n_m + 3]
        x, y, c = lax.axis_index("x"), lax.axis_index("y"), lax.axis_index("c")
        me = 4 * x + 2 * y + c
        for j in range(n_m):
            for k, (px, py, pc) in enumerate(_peers(x, y, c)):
                s, d = _ends(src_refs[j], land_refs[j], 4 * px + 2 * py + pc, me, ns[j], axes[j], gather)
                cp = pltpu.make_async_remote_copy(
                    src_ref=s, dst_ref=d, send_sem=ssem.at[7 * j + k], recv_sem=rsem.at[7 * j + k],
                    device_id=(px, py, pc), device_id_type=MESH)
                cp.wait_send()
                cp.wait_recv()
            s, d = _ends(src_refs[j], land_refs[j], me, me, ns[j], axes[j], gather)
            pltpu.make_async_copy(s, d, lsem.at[j]).wait()

    outs = pl.pallas_call(
        body, name=name,
        out_shape=(*[pltpu.HBM(m.shape, m.dtype) for m in mats], *[pltpu.HBM(l.shape, l.dtype) for l in lands]),
        in_specs=[HBM_SPEC] * (2 * n_m) + [SEM_SPEC] * 3 + [pl.BlockSpec(memory_space=pl.ANY)],
        out_specs=tuple([HBM_SPEC] * (2 * n_m)),
        input_output_aliases={i: i for i in range(2 * n_m)},
        compiler_params=pltpu.CompilerParams(has_side_effects=FLOWING),
    )(*mats, *lands, send_sems, recv_sems, local_sems, after)
    return list(outs[n_m:])


def _sum_parts(parts, *, name):
    n, R, C = parts.shape
    rb = _pick(R, 512) if R % LANES == 0 else R

    def body(p_ref, o_ref):
        acc = p_ref[0].astype(F32)
        for k in range(1, n):
            acc = acc + p_ref[k].astype(F32)
        o_ref[...] = acc

    return pl.pallas_call(
        body, grid=(R // rb,), in_specs=[pl.BlockSpec((n, rb, C), lambda i: (0, i, 0))],
        out_specs=pl.BlockSpec((rb, C), lambda i: (i, 0)),
        out_shape=jax.ShapeDtypeStruct((R, C), F32),
        compiler_params=_params(("parallel",)), name=name,
    )(parts)


def _adamw(w, m, v, parts, *, name):
    L, R, C = w.shape
    n = parts[0].shape[0]
    cap = max(16, VMEM_LIMIT // 2 // (2 * L * n * C * parts[0].dtype.itemsize))
    rb = max([r for r in range(16, min(R, cap) + 1, 16) if R % r == 0], default=R)

    def body(w_ref, m_ref, v_ref, *rest):
        p_refs = rest[:L]
        g_ref, d_ref, nm_ref, nv_ref = rest[L:]
        for l in range(L):
            @pl.when(pl.program_id(0) == l)
            def _(p_ref=p_refs[l]):
                g = p_ref[0].astype(F32)
                for k in range(1, n):
                    g = g + p_ref[k].astype(F32)
                mn = ADAM_B1 * m_ref[...] + (1.0 - ADAM_B1) * g
                vn = ADAM_B2 * v_ref[...] + (1.0 - ADAM_B2) * jnp.square(g)
                m_hat = mn / (1.0 - ADAM_B1 ** ADAM_STEP)
                v_hat = vn / (1.0 - ADAM_B2 ** ADAM_STEP)
                g_ref[...] = g
                d_ref[...] = -ADAM_LR * (m_hat / (jnp.sqrt(v_hat) + ADAM_EPS) + ADAM_WD * w_ref[...])
                nm_ref[...] = mn
                nv_ref[...] = vn

    row = pl.BlockSpec((None, rb, C), lambda l, i: (l, i, 0))
    part = lambda k: pl.BlockSpec((n, rb, C), lambda l, i: (0, jnp.where(l == k, i, 0), 0))
    return pl.pallas_call(
        body, grid=(L, R // rb), in_specs=[row, row, row] + [part(k) for k in range(L)],
        out_specs=[row] * 4, out_shape=[jax.ShapeDtypeStruct((L, R, C), F32)] * 4,
        compiler_params=_params(("arbitrary", "arbitrary")), name=name,
    )(w, m, v, *parts)


BIG = (("a_w_in", "a_w_in", 1, False), ("a_w_out", "a_w_out", 0, False), ("kv_w", "kv_w", 0, False),
       ("b_w_q", "b_w_q_t", 0, True), ("b_w_o", "b_w_o", 1, False), ("ffn_w_up", "ffn_w_up_t", 0, True),
       ("ffn_w_down", "ffn_w_down", 0, False))
SMALL_SHARDED = (("a_ln_g", 1), ("a_ln_b", 1), ("ffn_conv_w", 2), ("ln_g", 2), ("ln_b", 2))
REPLICATED = ("a_w_s", "a_b_s", "rel_table", "ffn_conv_b")


def _pack_rows(arrs, lead=0):
    lshape = arrs[0].shape[:lead]
    p = jnp.concatenate([a.reshape(*lshape, -1, LANES) for a in arrs], axis=lead)
    pad = -p.shape[lead] % 8
    return jnp.pad(p, [(0, 0)] * lead + [(0, pad), (0, 0)])


def _unpack_rows(packed, shapes, lead=0):
    lshape = packed.shape[:lead]
    out, off = [], 0
    for shp in shapes:
        r = int(np.prod(shp)) // LANES
        out.append(lax.slice_in_dim(packed, off, off + r, axis=lead).reshape(*lshape, *shp))
        off += r
    return out


def _as_mats(a, transposed):
    a = a[None] if a.ndim == 2 else a
    return jnp.swapaxes(a, 1, 2) if transposed else a


def _merge_shards(stacked, axis):
    a = jnp.moveaxis(stacked, 0, axis)
    shp = list(a.shape)
    return a.reshape(shp[:axis] + [shp[axis] * shp[axis + 1]] + shp[axis + 2:])


def _split_shards(full, axis):
    shp = list(full.shape)
    a = full.reshape(shp[:axis] + [N_DEV, shp[axis] // N_DEV] + shp[axis + 1:])
    return jnp.moveaxis(a, axis, 0)


def kernel(x, a_w_in, a_ln_g, a_ln_b, a_w_s, a_b_s, a_w_out, kv_w, b_w_q, b_w_o, rel_table, ffn_w_up, ffn_conv_w, ffn_conv_b, ffn_w_down, ln_g, ln_b, loss_target, m_a_w_in, m_a_ln_g, m_a_ln_b, m_a_w_s, m_a_b_s, m_a_w_out, m_kv_w, m_b_w_q, m_b_w_o, m_rel_table, m_ffn_w_up, m_ffn_conv_w, m_ffn_conv_b, m_ffn_w_down, m_ln_g, m_ln_b, v_a_w_in, v_a_ln_g, v_a_ln_b, v_a_w_s, v_a_b_s, v_a_w_out, v_kv_w, v_b_w_q, v_b_w_o, v_rel_table, v_ffn_w_up, v_ffn_conv_w, v_ffn_conv_b, v_ffn_w_down, v_ln_g, v_ln_b):
    names = ["a_w_in", "a_ln_g", "a_ln_b", "a_w_s", "a_b_s", "a_w_out", "kv_w", "b_w_q", "b_w_o", "rel_table",
             "ffn_w_up", "ffn_conv_w", "ffn_conv_b", "ffn_w_down", "ln_g", "ln_b"]
    w = dict(zip(names, (a_w_in, a_ln_g, a_ln_b, a_w_s, a_b_s, a_w_out, kv_w, b_w_q, b_w_o, rel_table,
                         ffn_w_up, ffn_conv_w, ffn_conv_b, ffn_w_down, ln_g, ln_b)))
    m = dict(zip(names, (m_a_w_in, m_a_ln_g, m_a_ln_b, m_a_w_s, m_a_b_s, m_a_w_out, m_kv_w, m_b_w_q, m_b_w_o,
                         m_rel_table, m_ffn_w_up, m_ffn_conv_w, m_ffn_conv_b, m_ffn_w_down, m_ln_g, m_ln_b)))
    v = dict(zip(names, (v_a_w_in, v_a_ln_g, v_a_ln_b, v_a_w_s, v_a_b_s, v_a_w_out, v_kv_w, v_b_w_q, v_b_w_o,
                         v_rel_table, v_ffn_w_up, v_ffn_conv_w, v_ffn_conv_b, v_ffn_w_down, v_ln_g, v_ln_b)))
    small_names = [n for n, _ in SMALL_SHARDED]
    small_shapes = [w[n].shape for n in small_names]
    rep_shapes = [w[n].shape for n in REPLICATED]

    axis_of = {key: axis for _, key, axis, _ in BIG}
    src = {}
    for n, key, axis, tr in BIG:
        loc = _as_mats(w[n], tr).astype(BF16)
        for l in range(loc.shape[0]):
            src[(key, l)] = loc[l]
    order = []
    for i in range(DEPTH):
        if i < N_A:
            order += [[("a_w_in", i)], [("a_w_out", i)]]
        else:
            order += [([("kv_w", 0)] if i == N_A else []) + [("b_w_q_t", i - N_A)], [("b_w_o", i - N_A)]]
        order += [[("ffn_w_up_t", i)], [("ffn_w_down", i)]]
    small_src = _pack_rows([w[n] for n in small_names])
    srows = small_src.shape[0]
    handles, _ = _send_start([[(small_src, 0)]] + [[(src[kl], axis_of[kl[0]]) for kl in grp] for grp in order],
                             True, name="gather_start")
    small_all = _send_wait(handles[0], True, x, name="gather_wait_small")[0]
    small_st = _unpack_rows(small_all.reshape(N_DEV, srows, LANES), small_shapes, lead=1)
    base = {n: w[n] for n in REPLICATED}
    for (n, ax), st in zip(SMALL_SHARDED, small_st):
        base[n] = _merge_shards(st, ax)
    for n, key, _, tr in BIG:
        base[key] = [None] * (1 if w[n].ndim == 2 else w[n].shape[0])

    def fetch(group, after):
        mats = _send_wait(handles[1 + group], True, after, name=f"gather_wait_{group}")
        return dict(zip(order[group], mats))

    sent = {}

    def emit(group, mats):
        keys = list(mats)
        hs, token = _send_start([[(mats[kl], axis_of[kl[0]]) for kl in keys]], False, name=f"exchange_start_{group}")
        sent[group] = (keys, hs[0])
        return token

    small_sent = []

    def emit_small(grads):
        small_pack = _pack_rows([_split_shards(grads[n], ax) for n, ax in SMALL_SHARDED], lead=1)
        rest = _pack_rows([grads[n] for n in REPLICATED[1:]])
        mine = jnp.concatenate([small_pack.reshape(N_DEV * srows, LANES), rest], axis=0)
        gating = grads[REPLICATED[0]].reshape(-1, LANES).astype(BF16)
        hs, token = _send_start([[(mine, 0), (gating, 0)]], True, name="small_grads_start")
        small_sent.append(hs[0])
        return token

    loss, grad_x, grads = _local_step(x[0], loss_target[0], _Weights(base, fetch, emit, emit_small))
    loss = lax.psum(loss, ("x", "y", "c"))
    out = {}

    landed = {}
    last = grad_x
    left = lambda e: min(g for g, (keys, _) in sent.items() if any(k[0] == e[1] for k in keys))
    for n, key, axis, tr in sorted(BIG, key=left, reverse=True):
        shp = w[n].shape
        for group in sorted(sent, reverse=True):
            keys, h = sent[group]
            if keys[0] not in landed and any(k[0] == key for k in keys):
                landed.update(zip(keys, _send_wait(h, False, last, name=f"exchange_wait_{group}")))
        parts = [landed[(key, l)] for l in range(1 if len(shp) == 2 else shp[0])]
        res = _adamw(_as_mats(w[n], tr), _as_mats(m[n], tr), _as_mats(v[n], tr), parts, name=f"adamw_{n}")
        out[n] = [(jnp.swapaxes(r, 1, 2) if tr else r).reshape(shp) for r in res]
        last = res[0]

    allp, allg = _send_wait(small_sent[0], True, last, name="small_grads_wait")
    gsum = _sum_parts(allp.reshape(N_DEV, -1, LANES), name="sum_small_grads")
    gating = _sum_parts(allg.reshape(N_DEV, -1, LANES), name="sum_gating_grads")
    g_small = lax.dynamic_slice_in_dim(gsum, _my_index() * srows, srows, axis=0)
    pack_sr = lambda d: jnp.concatenate([_pack_rows([d[n] for n in small_names]),
                                         _pack_rows([d[n] for n in REPLICATED])], axis=0)
    n_rest = sum(int(np.prod(s)) for s in rep_shapes[1:]) // LANES
    gs_in = jnp.concatenate([g_small, gating, gsum[N_DEV * srows:N_DEV * srows + n_rest]], axis=0)
    gs_in = jnp.pad(gs_in, ((0, pack_sr(w).shape[0] - gs_in.shape[0]), (0, 0)))[None]
    res = _adamw(pack_sr(w)[None], pack_sr(m)[None], pack_sr(v)[None], [gs_in], name="adamw_small")
    for n, vals in zip(small_names, zip(*[_unpack_rows(r[0, :srows], small_shapes) for r in res])):
        out[n] = list(vals)
    for n, vals in zip(REPLICATED, zip(*[_unpack_rows(r[0, srows:], rep_shapes) for r in res])):
        out[n] = list(vals)

    return (loss, grad_x[None], *[out[n][0] for n in names], *[out[n][1] for n in names],
            *[out[n][2] for n in names], *[out[n][3] for n in names])
```

```python
import math

import numpy as np
import jax
import jax.numpy as jnp
from jax import lax
from jax.experimental import pallas as pl
from jax.experimental.pallas import tpu as pltpu

F32 = jnp.float32
BF16 = jnp.bfloat16
ACT = jnp.bfloat16
MESH = pl.DeviceIdType.MESH

N_DEV = 8
DEPTH = 4
N_A = 2
CHUNK = 128
BLK = 128
HEAD_DIM = 64
DILATED_GROUPS = ((128, 1), (512, 4), (2048, 16))
N_GROUPS = 3
REL_BUCKETS = 32
REL_MAX_DIST = 2048
ALPHA = (2 * DEPTH) ** 0.25
LN_EPS = 1e-5
NEG = -1e30
ADAM_LR = 0.001
ADAM_B1 = 0.9
ADAM_B2 = 0.999
ADAM_EPS = 1e-08
ADAM_WD = 0.01
ADAM_STEP = 10

LANES = 128
VMEM_LIMIT = 56 * 1024 * 1024
MM_TILE_CAP = 1408
MM_VMEM_BUDGET = 46 * 1024 * 1024
INV_SQRT2 = 1.0 / math.sqrt(2.0)
INV_SQRT_2PI = 1.0 / math.sqrt(2.0 * math.pi)


def _pick(n, cap):
    best = None
    for t in range(LANES, min(n, cap) + 1, LANES):
        if n % t == 0:
            best = t
    return best if best is not None else n


def _params(sem):
    return pltpu.CompilerParams(dimension_semantics=sem, vmem_limit_bytes=VMEM_LIMIT)


def _ordered(body, in_specs, args, after):
    if after is None:
        return body, list(in_specs), tuple(args)
    return (lambda _, *refs: body(*refs)), [pl.BlockSpec(memory_space=pl.ANY), *in_specs], (after, *args)


def _gelu(x):
    return 0.5 * x * (1.0 + lax.erf(x * INV_SQRT2))


def _gelu_grad(x):
    return 0.5 * (1.0 + lax.erf(x * INV_SQRT2)) + x * jnp.exp(-0.5 * x * x) * INV_SQRT_2PI


def _mm(a, b, *, ta=False, tb=False, out_dtype=F32, scale=None, after=None, name):
    halves = isinstance(a, tuple)
    parts = 1 if halves or a.ndim == 2 else a.shape[0]
    ash = (a[0].shape[0], 2 * a[0].shape[1]) if halves else (a.shape if parts == 1
                                                               else (a.shape[1], parts * a.shape[2]))
    if ta:
        K, M = ash
    else:
        M, K = ash
    if tb:
        N, Kb = b.shape
    else:
        Kb, N = b.shape
    assert K == Kb, (ash, b.shape, ta, tb)
    split = 2 if halves else parts
    tm = _pick(M // split if ta else M, MM_TILE_CAP)
    tn = _pick(N, MM_TILE_CAP)
    kspan = K if ta or split == 1 else K // split
    abytes = (a[0] if halves else a).dtype.itemsize * (2 if halves else 1)
    fixed = 2 * tm * tn * jnp.dtype(out_dtype).itemsize + tm * tn * 4
    fits = [t for t in range(LANES, kspan + 1, LANES)
            if kspan % t == 0 and 2 * t * (tm * abytes + tn * b.dtype.itemsize) + fixed <= MM_VMEM_BUDGET]
    tk = max(fits) if fits else _pick(kspan, MM_TILE_CAP)
    nk = K // tk
    nh = (M // split // tm if ta else K // split // tk) if split > 1 else 0
    dn = (((0 if ta else 1,), (1 if tb else 0,)), ((), ()))

    def body(*refs):
        n_tail = 3 if nk > 1 else 2
        a_refs, b_ref, o_ref = refs[:-n_tail], refs[-n_tail], refs[-n_tail + 1]
        k = pl.program_id(2)

        def finish(r):
            if scale is not None:
                r = r * scale
            o_ref[...] = r.astype(out_dtype)

        def accumulate(a_ref):
            part = lax.dot_general(a_ref[...].astype(BF16), b_ref[...].astype(BF16), dn,
                                   preferred_element_type=F32)
            if nk == 1:
                finish(part)
                return
            acc_ref = refs[-1]

            @pl.when(k == 0)
            def _():
                acc_ref[...] = part

            @pl.when((k > 0) & (k < nk - 1))
            def _():
                acc_ref[...] += part

            @pl.when(k == nk - 1)
            def _():
                finish(acc_ref[...] + part)

        if halves:
            first = (pl.program_id(0) if ta else k) < nh
            pl.when(first)(lambda: accumulate(a_refs[0]))
            pl.when(jnp.logical_not(first))(lambda: accumulate(a_refs[1]))
        else:
            accumulate(a_refs[0])

    if halves and ta:
        a_specs = [pl.BlockSpec((tk, tm), lambda i, j, k: (jnp.where(i < nh, k, 0), jnp.minimum(i, nh - 1))),
                   pl.BlockSpec((tk, tm), lambda i, j, k: (jnp.where(i >= nh, k, 0), jnp.maximum(i - nh, 0)))]
    elif halves:
        a_specs = [pl.BlockSpec((tm, tk), lambda i, j, k: (i, jnp.minimum(k, nh - 1))),
                   pl.BlockSpec((tm, tk), lambda i, j, k: (i, jnp.maximum(k - nh, 0)))]
    elif parts > 1:
        a_specs = [pl.BlockSpec((None, tk, tm), lambda i, j, k: (i // nh, k, i % nh)) if ta
                   else pl.BlockSpec((None, tm, tk), lambda i, j, k: (k // nh, i, k % nh))]
    else:
        a_specs = [pl.BlockSpec((tk, tm), lambda i, j, k: (k, i)) if ta
                   else pl.BlockSpec((tm, tk), lambda i, j, k: (i, k))]
    b_spec = (pl.BlockSpec((tn, tk), lambda i, j, k: (j, k)) if tb
              else pl.BlockSpec((tk, tn), lambda i, j, k: (k, j)))
    body, in_specs, args = _ordered(body, [*a_specs, b_spec], (*(a if halves else (a,)), b), after)
    return pl.pallas_call(
        body, grid=(M // tm, N // tn, nk), in_specs=in_specs,
        out_specs=pl.BlockSpec((tm, tn), lambda i, j, k: (i, j)),
        out_shape=jax.ShapeDtypeStruct((M, N), out_dtype),
        scratch_shapes=[pltpu.VMEM((tm, tn), F32)] if nk > 1 else [],
        compiler_params=_params(("parallel", "parallel", "arbitrary")), name=name,
    )(*args)


def _add_ln_fwd(x, h, g, b, *, wide, name):
    T, D = x.shape
    rb = _pick(T, 1024)

    def body(x_ref, h_ref, g_ref, b_ref, *o_refs):
        pre = ALPHA * x_ref[...].astype(F32) + h_ref[...].astype(F32)
        mu = jnp.mean(pre, axis=1, keepdims=True)
        cen = pre - mu
        var = jnp.mean(cen * cen, axis=1, keepdims=True)
        y = cen * lax.rsqrt(var + LN_EPS) * g_ref[...] + b_ref[...]
        for o_ref in o_refs:
            o_ref[...] = y.astype(o_ref.dtype)

    row = pl.BlockSpec((rb, D), lambda i: (i, 0))
    vec = pl.BlockSpec((1, D), lambda i: (0, 0))
    dtypes = [F32, BF16] if wide else [BF16]
    return pl.pallas_call(
        body, grid=(T // rb,), in_specs=[row, row, vec, vec], out_specs=[row] * len(dtypes),
        out_shape=[jax.ShapeDtypeStruct((T, D), dt) for dt in dtypes],
        compiler_params=_params(("parallel",)), name=name,
    )(x, h, g.reshape(1, D), b.reshape(1, D))


def _add_ln_bwd(x, h, g, terms, *, after=None, name):
    T, D = x.shape
    rb = _pick(T, 1024)
    coefs = [c for c, _ in terms]
    nt = len(terms)

    def body(*refs):
        x_ref, h_ref, g_ref = refs[:3]
        t_refs = refs[3:3 + nt]
        dpb_ref, dg_ref, db_ref = refs[3 + nt:]
        dy = None
        for c, r in zip(coefs, t_refs):
            v = r[...].astype(F32) if c == 1.0 else c * r[...].astype(F32)
            dy = v if dy is None else dy + v
        pre = ALPHA * x_ref[...].astype(F32) + h_ref[...].astype(F32)
        mu = jnp.mean(pre, axis=1, keepdims=True)
        cen = pre - mu
        var = jnp.mean(cen * cen, axis=1, keepdims=True)
        rstd = lax.rsqrt(var + LN_EPS)
        xhat = cen * rstd
        dxh = dy * g_ref[...]
        m1 = jnp.mean(dxh, axis=1, keepdims=True)
        m2 = jnp.mean(dxh * xhat, axis=1, keepdims=True)
        dpre = rstd * (dxh - m1 - xhat * m2)
        dpb_ref[...] = dpre.astype(BF16)
        dg = jnp.sum(dy * xhat, axis=0, keepdims=True)
        db = jnp.sum(dy, axis=0, keepdims=True)

        @pl.when(pl.program_id(0) == 0)
        def _():
            dg_ref[...] = dg
            db_ref[...] = db

        @pl.when(pl.program_id(0) > 0)
        def _():
            dg_ref[...] += dg
            db_ref[...] += db

    row = pl.BlockSpec((rb, D), lambda i: (i, 0))
    vec = pl.BlockSpec((1, D), lambda i: (0, 0))
    body, in_specs, args = _ordered(body, [row, row, vec] + [row] * nt,
                                    (x, h, g.reshape(1, D), *[a for _, a in terms]), after)
    return pl.pallas_call(
        body, grid=(T // rb,), in_specs=in_specs,
        out_specs=[row, vec, vec],
        out_shape=[jax.ShapeDtypeStruct((T, D), BF16),
                   jax.ShapeDtypeStruct((1, D), F32), jax.ShapeDtypeStruct((1, D), F32)],
        compiler_params=_params(("arbitrary",)), name=name,
    )(*args)


def _lincomb(terms, out_dtype, *, name):
    R, C = terms[0][1].shape
    rb = _pick(R, 1024)
    coefs = [c for c, _ in terms]
    nt = len(terms)

    def body(*refs):
        acc = None
        for c, r in zip(coefs, refs[:nt]):
            v = r[...].astype(F32)
            v = v if c == 1.0 else c * v
            acc = v if acc is None else acc + v
        refs[nt][...] = acc.astype(out_dtype)

    row = pl.BlockSpec((rb, C), lambda i: (i, 0))
    return pl.pallas_call(
        body, grid=(R // rb,), in_specs=[row] * nt, out_specs=row,
        out_shape=jax.ShapeDtypeStruct((R, C), out_dtype),
        compiler_params=_params(("parallel",)), name=name,
    )(*[a for _, a in terms])


def _loss_grad(y, tgt, *, name):
    T, D = y.shape
    rb = _pick(T, 1024)

    def body(y_ref, t_ref, dy_ref, l_ref):
        err = y_ref[...] - t_ref[...]
        dy_ref[...] = err * (1.0 / D)
        part = jnp.sum(jnp.sum(err * err, axis=1, keepdims=True), axis=0, keepdims=True) * (0.5 / D)
        part = jnp.broadcast_to(part, (1, LANES))

        @pl.when(pl.program_id(0) == 0)
        def _():
            l_ref[...] = part

        @pl.when(pl.program_id(0) > 0)
        def _():
            l_ref[...] += part

    row = pl.BlockSpec((rb, D), lambda i: (i, 0))
    return pl.pallas_call(
        body, grid=(T // rb,), in_specs=[row, row],
        out_specs=[row, pl.BlockSpec((1, LANES), lambda i: (0, 0))],
        out_shape=[jax.ShapeDtypeStruct((T, D), F32), jax.ShapeDtypeStruct((1, LANES), F32)],
        compiler_params=_params(("arbitrary",)), name=name,
    )(y, tgt)


def _sgu_fwd(zp, ln_g, ln_b, ws, bst, *, name):
    T, E2 = zp.shape
    E = E2 // 2
    G = ws.shape[0]
    cg = E // G
    rb = 4 * CHUNK

    def body(z_ref, g_ref, b_ref, ws_ref, bs_ref, y_ref):
        u = _gelu(z_ref[:, :E].astype(F32))
        v = _gelu(z_ref[:, E:].astype(F32))
        mu = jnp.mean(v, axis=1, keepdims=True)
        cen = v - mu
        var = jnp.mean(cen * cen, axis=1, keepdims=True)
        vn = (cen * lax.rsqrt(var + LN_EPS) * g_ref[...] + b_ref[...]).astype(BF16)
        for ci in range(rb // CHUNK):
            rows = slice(ci * CHUNK, (ci + 1) * CHUNK)
            for gi in range(G):
                cols = slice(gi * cg, (gi + 1) * cg)
                sv = jnp.dot(ws_ref[gi], vn[rows, cols], preferred_element_type=F32)
                sv = sv + bs_ref[:, gi:gi + 1]
                y_ref[rows, cols] = (u[rows, cols] * sv).astype(BF16)

    return pl.pallas_call(
        body, grid=(T // rb,),
        in_specs=[pl.BlockSpec((rb, E2), lambda i: (i, 0)),
                  pl.BlockSpec((1, E), lambda i: (0, 0)), pl.BlockSpec((1, E), lambda i: (0, 0)),
                  pl.BlockSpec((G, CHUNK, CHUNK), lambda i: (0, 0, 0)),
                  pl.BlockSpec((CHUNK, G), lambda i: (0, 0))],
        out_specs=pl.BlockSpec((rb, E), lambda i: (i, 0)),
        out_shape=jax.ShapeDtypeStruct((T, E), BF16),
        compiler_params=_params(("parallel",)), name=name,
    )(zp, ln_g.reshape(1, E), ln_b.reshape(1, E), ws, bst)


def _sgu_bwd(zp, dy, ln_g, ln_b, ws, wst, bst, *, after=None, name):
    T, E2 = zp.shape
    E = E2 // 2
    G = ws.shape[0]
    cg = E // G
    rb = CHUNK
    nsteps = T // rb

    def body(z_ref, dy_ref, g_ref, b_ref, ws_ref, wst_ref, bs_ref,
             dz_ref, dg_ref, db_ref, dws_ref, dbs_ref, dsv_acc):
        step = pl.program_id(0)

        @pl.when(step == 0)
        def _():
            dg_ref[...] = jnp.zeros_like(dg_ref)
            db_ref[...] = jnp.zeros_like(db_ref)
            dws_ref[...] = jnp.zeros_like(dws_ref)
            dsv_acc[...] = jnp.zeros_like(dsv_acc)

        zu = z_ref[:, :E].astype(F32)
        zv = z_ref[:, E:].astype(F32)
        u = _gelu(zu)
        v = _gelu(zv)
        mu = jnp.mean(v, axis=1, keepdims=True)
        cen = v - mu
        var = jnp.mean(cen * cen, axis=1, keepdims=True)
        rstd = lax.rsqrt(var + LN_EPS)
        xhat = cen * rstd
        vn = (xhat * g_ref[...] + b_ref[...]).astype(BF16)
        dyv = dy_ref[...].astype(F32)
        dsv = dyv * u
        dsv_acc[...] += dsv
        dsvb = dsv.astype(BF16)
        tril = (lax.broadcasted_iota(jnp.int32, (CHUNK, CHUNK), 0)
                >= lax.broadcasted_iota(jnp.int32, (CHUNK, CHUNK), 1))
        du_parts = []
        dvn_parts = []
        for gi in range(G):
            cols = slice(gi * cg, (gi + 1) * cg)
            sv = jnp.dot(ws_ref[gi], vn[:, cols], preferred_element_type=F32) + bs_ref[:, gi:gi + 1]
            du_parts.append(dyv[:, cols] * sv)
            dvn_parts.append(jnp.dot(wst_ref[gi], dsvb[:, cols], preferred_element_type=F32))
            dw = lax.dot_general(dsvb[:, cols], vn[:, cols], (((1,), (1,)), ((), ())),
                                 preferred_element_type=F32)
            dws_ref[gi] += jnp.where(tril, dw, 0.0)
        du = jnp.concatenate(du_parts, axis=1)
        dvn = jnp.concatenate(dvn_parts, axis=1)
        dg_ref[...] += jnp.sum(dvn * xhat, axis=0, keepdims=True)
        db_ref[...] += jnp.sum(dvn, axis=0, keepdims=True)
        dxh = dvn * g_ref[...]
        m1 = jnp.mean(dxh, axis=1, keepdims=True)
        m2 = jnp.mean(dxh * xhat, axis=1, keepdims=True)
        dv = rstd * (dxh - m1 - xhat * m2)
        dz_ref[:, :E] = (du * _gelu_grad(zu)).astype(BF16)
        dz_ref[:, E:] = (dv * _gelu_grad(zv)).astype(BF16)

        @pl.when(step == nsteps - 1)
        def _():
            lane = lax.broadcasted_iota(jnp.int32, (CHUNK, LANES), 1)
            out = jnp.zeros((CHUNK, LANES), F32)
            for gi in range(G):
                s = jnp.sum(dsv_acc[:, gi * cg:(gi + 1) * cg], axis=1, keepdims=True)
                out = jnp.where(lane == gi, s, out)
            dbs_ref[...] = out

    vecE = pl.BlockSpec((1, E), lambda i: (0, 0))
    wspec = pl.BlockSpec((G, CHUNK, CHUNK), lambda i: (0, 0, 0))
    body, in_specs, args = _ordered(
        body, [pl.BlockSpec((rb, E2), lambda i: (i, 0)), pl.BlockSpec((rb, E), lambda i: (i, 0)),
               vecE, vecE, wspec, wspec, pl.BlockSpec((CHUNK, G), lambda i: (0, 0))],
        (zp, dy, ln_g.reshape(1, E), ln_b.reshape(1, E), ws, wst, bst), after)
    return pl.pallas_call(
        body, grid=(nsteps,), in_specs=in_specs,
        out_specs=[pl.BlockSpec((rb, E2), lambda i: (i, 0)), vecE, vecE, wspec,
                   pl.BlockSpec((CHUNK, LANES), lambda i: (0, 0))],
        out_shape=[jax.ShapeDtypeStruct((T, E2), BF16), jax.ShapeDtypeStruct((1, E), F32),
                   jax.ShapeDtypeStruct((1, E), F32), jax.ShapeDtypeStruct((G, CHUNK, CHUNK), F32),
                   jax.ShapeDtypeStruct((CHUNK, LANES), F32)],
        scratch_shapes=[pltpu.VMEM((CHUNK, E), F32)],
        compiler_params=_params(("arbitrary",)), name=name,
    )(*args)


def _shift_down(x, k, row):
    return jnp.where(row >= k, pltpu.roll(x, k, 0), 0.0)


def _shift_up(x, k, row, T):
    return jnp.where(row < T - k, pltpu.roll(x, T - k, 0), 0.0)


def _conv3(x, w_ref, b_ref, row):
    return (w_ref[0:1, :] * _shift_down(x, 2, row) + w_ref[1:2, :] * _shift_down(x, 1, row)
            + w_ref[2:3, :] * x + b_ref[...])


def _convgate_fwd(hh, cw, cb, *, name):
    T, F2 = hh.shape
    F = F2 // 2
    ns = F // LANES

    def body(a_ref, g_ref, wa_ref, wg_ref, ba_ref, bg_ref, o_ref, ca_ref, cg_ref):
        row = lax.broadcasted_iota(jnp.int32, (T, LANES), 0)
        ca = _conv3(a_ref[...].astype(F32), wa_ref, ba_ref, row)
        cgv = _conv3(g_ref[...].astype(F32), wg_ref, bg_ref, row)
        o_ref[...] = (_gelu(ca) * cgv).astype(BF16)
        ca_ref[...] = ca.astype(ACT)
        cg_ref[...] = cgv.astype(ACT)

    sa = lambda r: pl.BlockSpec((r, LANES), lambda j: (0, j))
    sg = lambda r: pl.BlockSpec((r, LANES), lambda j: (0, j + ns))
    return pl.pallas_call(
        body, grid=(ns,), in_specs=[sa(T), sg(T), sa(3), sg(3), sa(1), sg(1)],
        out_specs=[sa(T)] * 3,
        out_shape=[jax.ShapeDtypeStruct((T, F), BF16), jax.ShapeDtypeStruct((T, F), ACT),
                   jax.ShapeDtypeStruct((T, F), ACT)],
        compiler_params=_params(("parallel",)), name=name,
    )(hh, hh, cw, cw, cb, cb)


def _convgate_bwd(hh, hca, hcg, dact, cw, *, name):
    T, F2 = hh.shape
    F = F2 // 2
    ns = F // LANES

    def body(a_ref, g_ref, ca_ref, cg_ref, d_ref, wa_ref, wg_ref,
             da_ref, dg_ref, dwa_ref, dwg_ref, dba_ref, dbg_ref):
        row = lax.broadcasted_iota(jnp.int32, (T, LANES), 0)
        d = d_ref[...].astype(F32)
        ca = ca_ref[...].astype(F32)
        cgv = cg_ref[...].astype(F32)
        cdf = 0.5 * (1.0 + lax.erf(ca * INV_SQRT2))
        dca = d * cgv * (cdf + ca * jnp.exp(-0.5 * ca * ca) * INV_SQRT_2PI)
        dcg = d * (ca * cdf)
        for x_ref, w_ref, dc, dx_ref, dw_ref, db_ref in (
                (a_ref, wa_ref, dca, da_ref, dwa_ref, dba_ref),
                (g_ref, wg_ref, dcg, dg_ref, dwg_ref, dbg_ref)):
            x = x_ref[...].astype(F32)
            up1, up2 = _shift_up(dc, 1, row, T), _shift_up(dc, 2, row, T)
            dx_ref[...] = (w_ref[2:3, :] * dc + w_ref[1:2, :] * up1 + w_ref[0:1, :] * up2).astype(BF16)
            dw_ref[0:1, :] = jnp.sum(up2 * x, axis=0, keepdims=True)
            dw_ref[1:2, :] = jnp.sum(up1 * x, axis=0, keepdims=True)
            dw_ref[2:3, :] = jnp.sum(dc * x, axis=0, keepdims=True)
            db_ref[...] = jnp.sum(dc, axis=0, keepdims=True)

    sa = lambda r: pl.BlockSpec((r, LANES), lambda j: (0, j))
    sg = lambda r: pl.BlockSpec((r, LANES), lambda j: (0, j + ns))
    return pl.pallas_call(
        body, grid=(ns,), in_specs=[sa(T), sg(T), sa(T), sa(T), sa(T), sa(3), sg(3)],
        out_specs=[sa(T), sa(T), sa(3), sa(3), sa(1), sa(1)],
        out_shape=[jax.ShapeDtypeStruct((T, F), BF16), jax.ShapeDtypeStruct((T, F), BF16),
                   jax.ShapeDtypeStruct((3, F), F32), jax.ShapeDtypeStruct((3, F), F32),
                   jax.ShapeDtypeStruct((1, F), F32), jax.ShapeDtypeStruct((1, F), F32)],
        compiler_params=_params(("parallel",)), name=name,
    )(hh, hh, hca, hcg, dact, cw, cw)


def _bucket_maps():
    iq = np.arange(BLK)[:, None]
    ik = np.arange(2 * BLK)[None, :]
    delta = iq + BLK - ik
    maps = []
    for win, dil in DILATED_GROUPS:
        n = np.clip(delta, 0, None) * dil
        max_exact = REL_BUCKETS // 2
        nf = np.maximum(n, 1).astype(np.float32)
        large = max_exact + (np.log(nf / np.float32(max_exact)) / np.float32(math.log(REL_MAX_DIST / max_exact))
                             * np.float32(REL_BUCKETS - max_exact)).astype(np.int32)
        large = np.minimum(large, REL_BUCKETS - 1)
        bucket = np.where(n < max_exact, n, large)
        valid = (delta >= 0) & (delta <= win // dil)
        maps.append(np.where(valid, bucket, -1).astype(np.int32))
    return np.stack(maps)


def _band_bias(rel_table, bmap, H, *, name):
    def body(t_ref, m_ref, o_ref):
        g = pl.program_id(0)
        bm = m_ref[0]
        for h in range(H):
            acc = jnp.full((BLK, 2 * BLK), NEG, F32)
            for b in range(REL_BUCKETS):
                acc = jnp.where(bm == b, t_ref[b, g * H + h], acc)
            o_ref[0, h] = acc

    return pl.pallas_call(
        body, grid=(N_GROUPS,),
        in_specs=[pl.BlockSpec(memory_space=pltpu.SMEM),
                  pl.BlockSpec((1, BLK, 2 * BLK), lambda g: (g, 0, 0))],
        out_specs=pl.BlockSpec((1, H, BLK, 2 * BLK), lambda g: (g, 0, 0, 0)),
        out_shape=jax.ShapeDtypeStruct((N_GROUPS, H, BLK, 2 * BLK), F32),
        compiler_params=_params(("parallel",)), name=name,
    )(rel_table, bmap)


def _band_bias_bwd(dbias, bmap, H, *, name):
    def body(d_ref, m_ref, o_ref):
        bm = m_ref[0]
        rowi = lax.broadcasted_iota(jnp.int32, (REL_BUCKETS, LANES), 0)
        lane = lax.broadcasted_iota(jnp.int32, (REL_BUCKETS, LANES), 1)
        out = jnp.zeros((REL_BUCKETS, LANES), F32)
        for h in range(H):
            dv = d_ref[0, h]
            for b in range(REL_BUCKETS):
                s = jnp.sum(jnp.sum(jnp.where(bm == b, dv, 0.0), axis=1, keepdims=True),
                            axis=0, keepdims=True)
                out = jnp.where((rowi == b) & (lane == h), s, out)
        o_ref[0] = out

    return pl.pallas_call(
        body, grid=(N_GROUPS,),
        in_specs=[pl.BlockSpec((1, H, BLK, 2 * BLK), lambda g: (g, 0, 0, 0)),
                  pl.BlockSpec((1, BLK, 2 * BLK), lambda g: (g, 0, 0))],
        out_specs=pl.BlockSpec((1, REL_BUCKETS, LANES), lambda g: (g, 0, 0)),
        out_shape=jax.ShapeDtypeStruct((N_GROUPS, REL_BUCKETS, LANES), F32),
        compiler_params=_params(("parallel",)), name=name,
    )(dbias, bmap)


def _head_masks():
    lane = lax.broadcasted_iota(jnp.int32, (BLK, LANES), 1)
    return (lane < HEAD_DIM, lane >= HEAD_DIM)


SUPER = DILATED_GROUPS[-1][1] * BLK


def _band_rows(it, d):
    r, j = it % d, it // d
    if d == 1:
        at = lambda blk: pl.ds(pl.multiple_of(blk * BLK, BLK), BLK)
    else:
        at = lambda blk: pl.ds(r + d * BLK * blk, BLK, stride=d)
    return at(j), at(jnp.maximum(j - 1, 0))


def _stack_heads(x, hm):
    zero = jnp.zeros_like(x)
    return jnp.concatenate([jnp.where(hm[0], x, zero), jnp.where(hm[1], x, zero)], axis=0)


def _band_loops(step, d, unroll):
    n_it = SUPER // BLK

    def run(lo, hi, inside):
        if hi > lo:
            def body(it, carry):
                step(it, inside)
                return carry
            lax.fori_loop(lo, hi, body, 0, unroll=max(u for u in range(1, unroll + 1) if (hi - lo) % u == 0))

    run(0, d, False)
    run(d, n_it, True)


def _last_rows(it, d):
    m = SUPER // (d * BLK)
    if d == 1:
        return pl.ds((m - 1) * BLK, BLK)
    return pl.ds(it % d + d * BLK * (m - 1), BLK, stride=d)


def _attn_fwd_all(q, kv, bias, *, name):
    T = q.shape[0]
    HD = kv.shape[1] // 2
    PP = HD // LANES
    NS = T // SUPER

    def body(q0, q1, q2, kp_ref, kc_ref, vp_ref, vc_ref, b_ref, o_ref, ob_ref, l_ref, og, lg):
        n = pl.program_id(1)
        col = lax.broadcasted_iota(jnp.int32, (2 * BLK, 2 * BLK), 1)
        hm = _head_masks()
        for g, (q_ref, (_, d)) in enumerate(zip((q0, q1, q2), DILATED_GROUPS)):
            def step(it, inside, g=g, q_ref=q_ref, d=d):
                cur, prv = _band_rows(it, d)
                qp = q_ref[cur, :].astype(BF16)
                if inside:
                    kprev, vprev = kc_ref[prv, :], vc_ref[prv, :]
                else:
                    last = _last_rows(it, d)
                    kprev, vprev = kp_ref[last, :], vp_ref[last, :]
                kc = jnp.concatenate([kprev.astype(BF16), kc_ref[cur, :].astype(BF16)], axis=0)
                vc = jnp.concatenate([vprev.astype(BF16), vc_ref[cur, :].astype(BF16)], axis=0)
                s = lax.dot_general(_stack_heads(qp, hm), kc, (((1,), (1,)), ((), ())),
                                    preferred_element_type=F32)
                s = s + b_ref[g].reshape(2 * BLK, 2 * BLK)
                if not inside:
                    s = jnp.where((n == 0) & (col < BLK), NEG, s)
                mx = jnp.max(s, axis=1, keepdims=True)
                e = jnp.exp(s - mx)
                den = jnp.sum(e, axis=1, keepdims=True)
                out = jnp.dot((e / den).astype(BF16), vc, preferred_element_type=F32)
                lse = mx + jnp.log(den)
                og.at[g][cur, :] = jnp.where(hm[0], out[:BLK], out[BLK:])
                lg.at[g][cur, :] = jnp.where(hm[0], lse[:BLK], lse[BLK:])

            _band_loops(step, d, 8)
        la, lb, lc = lg[0], lg[1], lg[2]
        mx = jnp.maximum(jnp.maximum(la, lb), lc)
        L = mx + jnp.log(jnp.exp(la - mx) + jnp.exp(lb - mx) + jnp.exp(lc - mx))
        o = jnp.exp(la - L) * og[0] + jnp.exp(lb - L) * og[1] + jnp.exp(lc - L) * og[2]
        o_ref[...] = o
        ob_ref[...] = o.astype(BF16)
        l_ref[...] = L

    blk = lambda f: pl.BlockSpec((SUPER, LANES), f)
    prev = lambda n: jnp.maximum(n - 1, 0)
    qspec = lambda g: blk(lambda p, n: (n, g * PP + p))
    return pl.pallas_call(
        body, grid=(PP, NS),
        in_specs=[qspec(0), qspec(1), qspec(2),
                  blk(lambda p, n: (prev(n), p)), blk(lambda p, n: (n, p)),
                  blk(lambda p, n: (prev(n), PP + p)), blk(lambda p, n: (n, PP + p)),
                  pl.BlockSpec((N_GROUPS, 2, BLK, 2 * BLK), lambda p, n: (0, p, 0, 0))],
        out_specs=[blk(lambda p, n: (n, p))] * 3,
        out_shape=[jax.ShapeDtypeStruct((T, HD), F32), jax.ShapeDtypeStruct((T, HD), BF16),
                   jax.ShapeDtypeStruct((T, HD), F32)],
        scratch_shapes=[pltpu.VMEM((N_GROUPS, SUPER, LANES), F32), pltpu.VMEM((N_GROUPS, SUPER, LANES), F32)],
        compiler_params=_params(("parallel", "parallel")), name=name,
    )(q, q, q, kv, kv, kv, kv, bias)


def _attn_bwd_all(q, kv, bias, do, o, L, *, after=None, name):
    T = q.shape[0]
    HD = kv.shape[1] // 2
    PP = HD // LANES
    H = HD // HEAD_DIM
    NS = T // SUPER

    def body(q0, q1, q2, kp_ref, kc_ref, vp_ref, vc_ref, b_ref, do_ref, o_ref, L_ref,
             dq_ref, dk_ref, dv_ref, db_ref, ck_ref, cv_ref):
        n = pl.program_id(1)

        @pl.when(n == 0)
        def _():
            db_ref[...] = jnp.zeros_like(db_ref)
            ck_ref[...] = jnp.zeros_like(ck_ref)
            cv_ref[...] = jnp.zeros_like(cv_ref)

        dk_ref[...] = ck_ref[...]
        dv_ref[...] = cv_ref[...]
        ck_ref[...] = jnp.zeros_like(ck_ref)
        cv_ref[...] = jnp.zeros_like(cv_ref)

        @pl.when(n < NS)
        def _():
            col = lax.broadcasted_iota(jnp.int32, (2 * BLK, 2 * BLK), 1)
            hm = _head_masks()
            for g, (q_ref, (_, d)) in enumerate(zip((q0, q1, q2), DILATED_GROUPS)):
                def step(it, inside, g=g, q_ref=q_ref, d=d):
                    cur, prv = _band_rows(it, d)
                    last = _last_rows(it, d)
                    qp = q_ref[cur, :].astype(BF16)
                    if inside:
                        kprev, vprev = kc_ref[prv, :], vc_ref[prv, :]
                    else:
                        kprev, vprev = kp_ref[last, :], vp_ref[last, :]
                    kc = jnp.concatenate([kprev.astype(BF16), kc_ref[cur, :].astype(BF16)], axis=0)
                    vc = jnp.concatenate([vprev.astype(BF16), vc_ref[cur, :].astype(BF16)], axis=0)
                    dop = do_ref[cur, :]
                    prod = dop * o_ref[cur, :]
                    Lp = L_ref[cur, :]
                    qs = _stack_heads(qp, hm)
                    dos = _stack_heads(dop.astype(BF16), hm)
                    lse = jnp.concatenate([Lp[:, 0:1], Lp[:, HEAD_DIM:HEAD_DIM + 1]], axis=0)
                    delta = jnp.concatenate([jnp.sum(jnp.where(hm[0], prod, 0.0), axis=1, keepdims=True),
                                             jnp.sum(jnp.where(hm[1], prod, 0.0), axis=1, keepdims=True)], axis=0)
                    s = lax.dot_general(qs, kc, (((1,), (1,)), ((), ())), preferred_element_type=F32)
                    s = s + b_ref[g].reshape(2 * BLK, 2 * BLK)
                    if not inside:
                        s = jnp.where((n == 0) & (col < BLK), NEG, s)
                    pr = jnp.exp(s - lse)
                    dp = lax.dot_general(dos, vc, (((1,), (1,)), ((), ())), preferred_element_type=F32)
                    ds = pr * (dp - delta)
                    db_ref[g] += ds.reshape(2, BLK, 2 * BLK)
                    dsb = ds.astype(BF16)
                    dqs = jnp.dot(dsb, kc, preferred_element_type=F32)
                    dkc = lax.dot_general(dsb, qs, (((0,), (0,)), ((), ())), preferred_element_type=F32)
                    dvc = lax.dot_general(pr.astype(BF16), dos, (((0,), (0,)), ((), ())),
                                          preferred_element_type=F32)
                    dq_ref.at[g][cur, :] = jnp.where(hm[0], dqs[:BLK], dqs[BLK:]) * (HEAD_DIM ** -0.5)
                    ck_ref[cur, :] += dkc[BLK:]
                    cv_ref[cur, :] += dvc[BLK:]
                    if inside:
                        ck_ref[prv, :] += dkc[:BLK]
                        cv_ref[prv, :] += dvc[:BLK]
                    else:
                        dk_ref[last, :] += dkc[:BLK]
                        dv_ref[last, :] += dvc[:BLK]

                _band_loops(step, d, 4)

    blk = lambda f: pl.BlockSpec((SUPER, LANES), f)
    cur = lambda n: jnp.minimum(n, NS - 1)
    prev = lambda n: jnp.maximum(jnp.minimum(n, NS - 1) - 1, 0)
    lag = lambda n: jnp.maximum(n - 1, 0)
    qspec = lambda g: blk(lambda p, n: (cur(n), g * PP + p))
    bspec = pl.BlockSpec((N_GROUPS, 2, BLK, 2 * BLK), lambda p, n: (0, p, 0, 0))
    body, in_specs, args = _ordered(
        body, [qspec(0), qspec(1), qspec(2),
               blk(lambda p, n: (prev(n), p)), blk(lambda p, n: (cur(n), p)),
               blk(lambda p, n: (prev(n), PP + p)), blk(lambda p, n: (cur(n), PP + p)),
               bspec, blk(lambda p, n: (cur(n), p)), blk(lambda p, n: (cur(n), p)),
               blk(lambda p, n: (cur(n), p))],
        (q, q, q, kv, kv, kv, kv, bias, do, o, L), after)
    return pl.pallas_call(
        body, grid=(PP, NS + 1), in_specs=in_specs,
        out_specs=[pl.BlockSpec((N_GROUPS, SUPER, LANES), lambda p, n: (0, cur(n), p)),
                   blk(lambda p, n: (lag(n), p)), blk(lambda p, n: (lag(n), p)), bspec],
        out_shape=[jax.ShapeDtypeStruct((N_GROUPS, T, HD), F32), jax.ShapeDtypeStruct((T, HD), F32),
                   jax.ShapeDtypeStruct((T, HD), F32),
                   jax.ShapeDtypeStruct((N_GROUPS, H, BLK, 2 * BLK), F32)],
        scratch_shapes=[pltpu.VMEM((SUPER, LANES), F32), pltpu.VMEM((SUPER, LANES), F32)],
        compiler_params=_params(("arbitrary", "arbitrary")), name=name,
    )(*args)


LAST_DOWN = 3 * DEPTH


class _Weights(dict):
    def __init__(self, base, fetch=None, emit=None, emit_small=None):
        super().__init__(base)
        self._fetch, self._emit, self._emit_small = fetch, emit, emit_small

    def fetch(self, group, after):
        if self._fetch is not None:
            for (key, layer), mat in self._fetch(group, after).items():
                self[key][layer] = mat

    def emit(self, group, mats):
        return None if self._emit is None else self._emit(group, mats)

    def emit_small(self, grads):
        return None if self._emit_small is None else self._emit_small(grads)


def _local_step(x, tgt, W):
    T, D = x.shape
    H = W["rel_table"].shape[1] // N_GROUPS
    HD = H * HEAD_DIM
    G = W["a_w_s"].shape[1]
    assert T % (DILATED_GROUPS[-1][1] * BLK) == 0

    tril = jnp.tril(jnp.ones((CHUNK, CHUNK), F32))
    bmap = jnp.asarray(_bucket_maps())
    bias = _band_bias(W["rel_table"], bmap, H, name="band_bias")

    saved = []
    xc, xcb = x, x.astype(BF16)
    kvb = None
    for i in range(DEPTH):
        s = {"x": xc, "xb": xcb}
        W.fetch(4 * i, xc)
        if i < N_A:
            ws_m = W["a_w_s"][i] * tril
            s["ws"] = ws_m.astype(BF16)
            s["wst"] = jnp.swapaxes(ws_m, 1, 2).astype(BF16)
            s["bst"] = W["a_b_s"][i].T
            s["zp"] = _mm(xcb, W["a_w_in"][i], out_dtype=ACT, name=f"a_in_{i}")
            s["y"] = _sgu_fwd(s["zp"], W["a_ln_g"][i], W["a_ln_b"][i], s["ws"], s["bst"], name=f"sgu_fwd_{i}")
            W.fetch(4 * i + 1, s["zp"])
            s["h"] = _mm(s["y"], W["a_w_out"][i], out_dtype=ACT, name=f"a_out_{i}")
        else:
            j = i - N_A
            if kvb is None:
                kvb = _mm(xcb, W["kv_w"][0], name="kv_proj")
            s["q"] = _mm(xcb, W["b_w_q_t"][j], tb=True, scale=HEAD_DIM ** -0.5, name=f"q_proj_{j}")
            s["o"], s["ob"], s["L"] = _attn_fwd_all(s["q"], kvb, bias, name=f"attn_fwd_{j}")
            W.fetch(4 * i + 1, s["q"])
            s["h"] = _mm(s["ob"], W["b_w_o"][j], out_dtype=ACT, name=f"o_proj_{j}")
        s["x1b"], = _add_ln_fwd(xc, s["h"], W["ln_g"][i, 0], W["ln_b"][i, 0], wide=False, name=f"ln1_fwd_{i}")
        s["x1"] = s["x1b"]
        W.fetch(4 * i + 2, s["x1"])
        s["hh"] = _mm(s["x1b"], W["ffn_w_up_t"][i], tb=True, out_dtype=ACT, name=f"ffn_up_{i}")
        s["cw"] = W["ffn_conv_w"][i]
        s["cb"] = W["ffn_conv_b"][i].reshape(1, -1)
        s["act"], s["hca"], s["hcg"] = _convgate_fwd(s["hh"], s["cw"], s["cb"], name=f"convgate_fwd_{i}")
        W.fetch(4 * i + 3, s["hh"])
        s["f"] = _mm(s["act"], W["ffn_w_down"][i], out_dtype=ACT, name=f"ffn_down_{i}")
        outs = _add_ln_fwd(s["x1"], s["f"], W["ln_g"][i, 1], W["ln_b"][i, 1], wide=i == DEPTH - 1,
                           name=f"ln2_fwd_{i}")
        xc, xcb = outs[0], outs[-1]
        saved.append(s)

    dy, lossv = _loss_grad(xc, tgt, name="loss_grad")
    loss = lossv[0, 0]

    gl = {k: [None] * DEPTH for k in ("ffn_w_up_t", "ffn_conv_w", "ffn_conv_b", "ffn_w_down", "ln_g", "ln_b")}
    ga = {k: [None] * N_A for k in ("a_w_in", "a_ln_g", "a_ln_b", "a_w_s", "a_b_s", "a_w_out")}
    gb = {k: [None] * (DEPTH - N_A) for k in ("b_w_q_t", "b_w_o")}
    mats = ("a_w_in", "a_w_out", "b_w_q_t", "b_w_o", "ffn_w_up_t", "ffn_w_down")
    dks, dvs, dbias = [], [], []
    grads = {}
    terms = [(1.0, dy)]
    tok = None
    small_keys = ("ffn_conv_w", "ffn_conv_b", "ln_g", "ln_b", "a_ln_g", "a_ln_b", "a_w_s", "a_b_s")
    for i in reversed(range(DEPTH)):
        s = saved[i]
        dp2b, dg2, db2 = _add_ln_bwd(s["x1"], s["f"], W["ln_g"][i, 1], terms, after=tok, name=f"ln2_bwd_{i}")
        dact = _mm(dp2b, W["ffn_w_down"][i], tb=True, out_dtype=ACT, name=f"ffn_down_dx_{i}")
        gl["ffn_w_down"][i] = _mm(s["act"], dp2b, ta=True, out_dtype=BF16, name=f"ffn_down_dw_{i}")
        out_f = {("ffn_w_down", i): gl["ffn_w_down"][i]}
        if i == 0:
            tok = W.emit(LAST_DOWN, out_f)
            out_f = {}
        dha, dhg, dwa, dwg, dba, dbg = _convgate_bwd(s["hh"], s["hca"], s["hcg"], dact, s["cw"],
                                                     name=f"convgate_bwd_{i}")
        dhh = (dha, dhg)
        gl["ffn_conv_w"][i] = jnp.concatenate([dwa, dwg], axis=1)
        gl["ffn_conv_b"][i] = jnp.concatenate([dba, dbg], axis=1)[0]
        dx1 = _mm(dhh, W["ffn_w_up_t"][i], out_dtype=ACT, after=tok, name=f"ffn_up_dx_{i}")
        gl["ffn_w_up_t"][i] = _mm(dhh, s["x1b"], ta=True, out_dtype=BF16, name=f"ffn_up_dw_{i}")
        out_f[("ffn_w_up_t", i)] = gl["ffn_w_up_t"][i]
        if i == 0:
            tok = W.emit(3 * i + 2, out_f)
        dp1b, dg1, db1 = _add_ln_bwd(s["x"], s["h"], W["ln_g"][i, 0], [(ALPHA, dp2b), (1.0, dx1)],
                                     after=tok, name=f"ln1_bwd_{i}")
        gl["ln_g"][i] = jnp.concatenate([dg1, dg2], axis=0)
        gl["ln_b"][i] = jnp.concatenate([db1, db2], axis=0)
        terms = [(ALPHA, dp1b)]
        if i < N_A:
            dyy = _mm(dp1b, W["a_w_out"][i], tb=True, out_dtype=ACT, name=f"a_out_dx_{i}")
            ga["a_w_out"][i] = _mm(s["y"], dp1b, ta=True, out_dtype=BF16, name=f"a_out_dw_{i}")
            if i == 0:
                tok = W.emit(3 * i + 1, {("a_w_out", i): ga["a_w_out"][i]})
            dzp, dlg, dlb, dws, dbs = _sgu_bwd(s["zp"], dyy, W["a_ln_g"][i], W["a_ln_b"][i], s["ws"],
                                               s["wst"], s["bst"], after=tok, name=f"sgu_bwd_{i}")
            ga["a_ln_g"][i], ga["a_ln_b"][i], ga["a_w_s"][i] = dlg[0], dlb[0], dws
            ga["a_b_s"][i] = dbs[:, :G].T
            if i == 0:
                for dct in (gl, ga):
                    grads.update({k: jnp.stack(v) for k, v in dct.items() if k in small_keys})
                tok = W.emit_small(grads)
            ga["a_w_in"][i] = _mm(s["xb"], dzp, ta=True, out_dtype=BF16, after=tok, name=f"a_in_dw_{i}")
            out_a = {("a_w_in", i): ga["a_w_in"][i]}
            if i > 0:
                out_a[("a_w_out", i)] = ga["a_w_out"][i]
                out_a.update(out_f)
            tok = W.emit(3 * i, out_a)
            terms.append((1.0, _mm(dzp, W["a_w_in"][i], tb=True, out_dtype=ACT, after=tok, name=f"a_in_dx_{i}")))
        else:
            j = i - N_A
            do = _mm(dp1b, W["b_w_o"][j], tb=True, name=f"o_proj_dx_{j}")
            gb["b_w_o"][j] = _mm(s["ob"], dp1b, ta=True, out_dtype=BF16, name=f"o_proj_dw_{j}")
            dq, dk_j, dv_j, db_j = _attn_bwd_all(s["q"], kvb, bias, do, s["o"], s["L"], after=tok,
                                                 name=f"attn_bwd_{j}")
            dks.append((1.0, dk_j))
            dvs.append((1.0, dv_j))
            dbias.append(db_j)
            terms.append((1.0, _mm(dq, W["b_w_q_t"][j], out_dtype=ACT, name=f"q_proj_dx_{j}")))
            gb["b_w_q_t"][j] = _mm(dq, s["xb"], ta=True, out_dtype=BF16, name=f"q_proj_dw_{j}")
            out_b = {("b_w_q_t", j): gb["b_w_q_t"][j], ("b_w_o", j): gb["b_w_o"][j], **out_f}
            if i == N_A:
                dkv = jnp.concatenate([_lincomb(dks, BF16, name="dk_sum"), _lincomb(dvs, BF16, name="dv_sum")],
                                      axis=1)
                terms.append((1.0, _mm(dkv, W["kv_w"][0], tb=True, out_dtype=ACT, name="kv_proj_dx")))
                grads["kv_w"] = [_mm(s["xb"], dkv, ta=True, out_dtype=BF16, name="kv_proj_dw")]
                out_b[("kv_w", 0)] = grads["kv_w"][0]
                dbt = _lincomb([(1.0, a.reshape(-1, 2 * BLK)) for a in dbias], F32, name="dbias_sum")
                dtab = _band_bias_bwd(dbt.reshape(N_GROUPS, H, BLK, 2 * BLK), bmap, H, name="band_bias_bwd")
                grads["rel_table"] = jnp.transpose(dtab[:, :, :H], (1, 0, 2)).reshape(REL_BUCKETS, N_GROUPS * H)
            tok = W.emit(3 * i, out_b)
    grad_x = _lincomb(terms, F32, name="grad_x")
    for dct in (gl, ga, gb):
        grads.update({k: v for k, v in dct.items() if k in mats})
    return loss, grad_x, grads


def _my_index():
    return 4 * lax.axis_index("x") + 2 * lax.axis_index("y") + lax.axis_index("c")


HBM_SPEC = pl.BlockSpec(memory_space=pltpu.HBM)


def _block(ref, k, n, axis):
    off = pl.multiple_of(k * n, n)
    return ref.at[pl.ds(off, n), :] if axis == 0 else ref.at[:, pl.ds(off, n)]


SEM_SPEC = pl.BlockSpec(memory_space=pltpu.SEMAPHORE)
FLOWING = pltpu.SideEffectType.DATAFLOW_SIDE_EFFECTING


def _peers(x, y, c):
    return [(1 - x if k & 4 else x, 1 - y if k & 2 else y, 1 - c if k & 1 else c) for k in range(1, N_DEV)]


def _ends(src_ref, land_ref, peer_index, me, n, axis, gather):
    if gather:
        return src_ref, _block(land_ref, me, n, axis)
    return _block(src_ref, peer_index, n, axis), land_ref.at[me]


def _send_start(groups, gather, *, name):
    flat = [(g, j, mat, axis) for g, items in enumerate(groups) for j, (mat, axis) in enumerate(items)]
    M, G = len(flat), len(groups)
    lands, ns = [], []
    for _, _, mat, axis in flat:
        A, B = mat.shape
        if gather:
            lands.append((A * N_DEV, B) if axis == 0 else (A, B * N_DEV))
            ns.append(A if axis == 0 else B)
        else:
            lands.append((N_DEV, A // N_DEV, B) if axis == 0 else (N_DEV, A, B // N_DEV))
            ns.append(A // N_DEV if axis == 0 else B // N_DEV)

    def body(*refs):
        src_refs, land_refs, sems = refs[:M], refs[M:2 * M], refs[2 * M:2 * M + 3 * G]
        token = refs[-1]
        x, y, c = lax.axis_index("x"), lax.axis_index("y"), lax.axis_index("c")
        me = 4 * x + 2 * y + c
        peers = _peers(x, y, c)
        for i, (g, j, _, axis) in enumerate(flat):
            for k in (1, 3, 5, 2, 4, 6, 0):
                px, py, pc = peers[k]
                s, d = _ends(src_refs[i], land_refs[i], 4 * px + 2 * py + pc, me, ns[i], axis, gather)
                pltpu.make_async_remote_copy(
                    src_ref=s, dst_ref=d, send_sem=sems[3 * g].at[7 * j + k], recv_sem=sems[3 * g + 1].at[7 * j + k],
                    device_id=(px, py, pc), device_id_type=MESH).start()
            s, d = _ends(src_refs[i], land_refs[i], me, me, ns[i], axis, gather)
            pltpu.make_async_copy(s, d, sems[3 * g + 2].at[j]).start(priority=1)
        token[...] = jnp.zeros_like(token)

    sem_shapes = []
    for items in groups:
        sem_shapes += [pltpu.SemaphoreType.DMA((7 * len(items),))] * 2 + [pltpu.SemaphoreType.DMA((len(items),))]
    outs = pl.pallas_call(
        body, name=name,
        out_shape=(*sem_shapes, *[pltpu.HBM(m.shape, m.dtype) for _, _, m, _ in flat],
                   *[pltpu.HBM(shp, m.dtype) for shp, (_, _, m, _) in zip(lands, flat)],
                   jax.ShapeDtypeStruct((8, LANES), F32)),
        in_specs=[HBM_SPEC] * (2 * M),
        out_specs=(*[SEM_SPEC] * (3 * G), *[HBM_SPEC] * (2 * M), pl.BlockSpec(memory_space=pltpu.VMEM)),
        input_output_aliases={i: 3 * G + i for i in range(2 * M)},
        compiler_params=pltpu.CompilerParams(has_side_effects=FLOWING),
    )(*[pltpu.with_memory_space_constraint(m, pltpu.HBM) for _, _, m, _ in flat],
      *[pltpu.with_memory_space_constraint(lax.empty(shp, m.dtype), pltpu.HBM)
        for shp, (_, _, m, _) in zip(lands, flat)])
    handles = []
    for g in range(G):
        idx = [i for i, f in enumerate(flat) if f[0] == g]
        handles.append((outs[3 * g], outs[3 * g + 1], outs[3 * g + 2], [outs[3 * G + i] for i in idx],
                        [outs[3 * G + M + i] for i in idx], [flat[i][3] for i in idx]))
    return handles, outs[-1]


def _send_wait(handle, gather, after, *, name):
    send_sems, recv_sems, local_sems, mats, lands, axes = handle
    n_m = len(mats)
    ns = []
    for mat, land, axis in zip(mats, lands, axes):
        ns.append(mat.shape[axis] if gather else land.shape[1 + axis])

    def body(*refs):
        src_refs, land_refs = refs[:n_m], refs[n_m:2 * n_m]
        ssem, rsem, lsem = refs[2 * n_m:2 * n_m + 3]
        x, y, c = lax.axis_index("x"), lax.axis_index("y"), lax.axis_index("c")
        me = 4 * x + 2 * y + c
        for j in range(n_m):
            for k, (px, py, pc) in enumerate(_peers(x, y, c)):
                s, d = _ends(src_refs[j], land_refs[j], 4 * px + 2 * py + pc, me, ns[j], axes[j], gather)
                cp = pltpu.make_async_remote_copy(
                    src_ref=s, dst_ref=d, send_sem=ssem.at[7 * j + k], recv_sem=rsem.at[7 * j + k],
                    device_id=(px, py, pc), device_id_type=MESH)
                cp.wait_send()
                cp.wait_recv()
            s, d = _ends(src_refs[j], land_refs[j], me, me, ns[j], axes[j], gather)
            pltpu.make_async_copy(s, d, lsem.at[j]).wait()

    outs = pl.pallas_call(
        body, name=name,
        out_shape=(*[pltpu.HBM(m.shape, m.dtype) for m in mats], *[pltpu.HBM(l.shape, l.dtype) for l in lands]),
        in_specs=[HBM_SPEC] * (2 * n_m) + [SEM_SPEC] * 3 + [pl.BlockSpec(memory_space=pl.ANY)],
        out_specs=tuple([HBM_SPEC] * (2 * n_m)),
        input_output_aliases={i: i for i in range(2 * n_m)},
        compiler_params=pltpu.CompilerParams(has_side_effects=FLOWING),
    )(*mats, *lands, send_sems, recv_sems, local_sems, after)
    return list(outs[n_m:])


def _sum_parts(parts, *, name):
    n, R, C = parts.shape
    rb = _pick(R, 512) if R % LANES == 0 else R

    def body(p_ref, o_ref):
        acc = p_ref[0].astype(F32)
        for k in range(1, n):
            acc = acc + p_ref[k].astype(F32)
        o_ref[...] = acc

    return pl.pallas_call(
        body, grid=(R // rb,), in_specs=[pl.BlockSpec((n, rb, C), lambda i: (0, i, 0))],
        out_specs=pl.BlockSpec((rb, C), lambda i: (i, 0)),
        out_shape=jax.ShapeDtypeStruct((R, C), F32),
        compiler_params=_params(("parallel",)), name=name,
    )(parts)


def _adamw(w, m, v, parts, *, name):
    L, R, C = w.shape
    n = parts[0].shape[0]
    cap = max(16, VMEM_LIMIT // 2 // (2 * L * n * C * parts[0].dtype.itemsize))
    rb = max([r for r in range(16, min(R, cap) + 1, 16) if R % r == 0], default=R)

    def body(w_ref, m_ref, v_ref, *rest):
        p_refs = rest[:L]
        g_ref, d_ref, nm_ref, nv_ref = rest[L:]
        for l in range(L):
            @pl.when(pl.program_id(0) == l)
            def _(p_ref=p_refs[l]):
                g = p_ref[0].astype(F32)
                for k in range(1, n):
                    g = g + p_ref[k].astype(F32)
                mn = ADAM_B1 * m_ref[...] + (1.0 - ADAM_B1) * g
                vn = ADAM_B2 * v_ref[...] + (1.0 - ADAM_B2) * jnp.square(g)
                m_hat = mn / (1.0 - ADAM_B1 ** ADAM_STEP)
                v_hat = vn / (1.0 - ADAM_B2 ** ADAM_STEP)
                g_ref[...] = g
                d_ref[...] = -ADAM_LR * (m_hat / (jnp.sqrt(v_hat) + ADAM_EPS) + ADAM_WD * w_ref[...])
                nm_ref[...] = mn
                nv_ref[...] = vn

    row = pl.BlockSpec((None, rb, C), lambda l, i: (l, i, 0))
    part = lambda k: pl.BlockSpec((n, rb, C), lambda l, i: (0, jnp.where(l == k, i, 0), 0))
    return pl.pallas_call(
        body, grid=(L, R // rb), in_specs=[row, row, row] + [part(k) for k in range(L)],
        out_specs=[row] * 4, out_shape=[jax.ShapeDtypeStruct((L, R, C), F32)] * 4,
        compiler_params=_params(("arbitrary", "arbitrary")), name=name,
    )(w, m, v, *parts)


BIG = (("a_w_in", "a_w_in", 1, False), ("a_w_out", "a_w_out", 0, False), ("kv_w", "kv_w", 0, False),
       ("b_w_q", "b_w_q_t", 0, True), ("b_w_o", "b_w_o", 1, False), ("ffn_w_up", "ffn_w_up_t", 0, True),
       ("ffn_w_down", "ffn_w_down", 0, False))
SMALL_SHARDED = (("a_ln_g", 1), ("a_ln_b", 1), ("ffn_conv_w", 2), ("ln_g", 2), ("ln_b", 2))
REPLICATED = ("a_w_s", "a_b_s", "rel_table", "ffn_conv_b")


def _pack_rows(arrs, lead=0):
    lshape = arrs[0].shape[:lead]
    p = jnp.concatenate([a.reshape(*lshape, -1, LANES) for a in arrs], axis=lead)
    pad = -p.shape[lead] % 8
    return jnp.pad(p, [(0, 0)] * lead + [(0, pad), (0, 0)])


def _unpack_rows(packed, shapes, lead=0):
    lshape = packed.shape[:lead]
    out, off = [], 0
    for shp in shapes:
        r = int(np.prod(shp)) // LANES
        out.append(lax.slice_in_dim(packed, off, off + r, axis=lead).reshape(*lshape, *shp))
        off += r
    return out


def _as_mats(a, transposed):
    a = a[None] if a.ndim == 2 else a
    return jnp.swapaxes(a, 1, 2) if transposed else a


def _merge_shards(stacked, axis):
    a = jnp.moveaxis(stacked, 0, axis)
    shp = list(a.shape)
    return a.reshape(shp[:axis] + [shp[axis] * shp[axis + 1]] + shp[axis + 2:])


def _split_shards(full, axis):
    shp = list(full.shape)
    a = full.reshape(shp[:axis] + [N_DEV, shp[axis] // N_DEV] + shp[axis + 1:])
    return jnp.moveaxis(a, axis, 0)


def kernel(x, a_w_in, a_ln_g, a_ln_b, a_w_s, a_b_s, a_w_out, kv_w, b_w_q, b_w_o, rel_table, ffn_w_up, ffn_conv_w, ffn_conv_b, ffn_w_down, ln_g, ln_b, loss_target, m_a_w_in, m_a_ln_g, m_a_ln_b, m_a_w_s, m_a_b_s, m_a_w_out, m_kv_w, m_b_w_q, m_b_w_o, m_rel_table, m_ffn_w_up, m_ffn_conv_w, m_ffn_conv_b, m_ffn_w_down, m_ln_g, m_ln_b, v_a_w_in, v_a_ln_g, v_a_ln_b, v_a_w_s, v_a_b_s, v_a_w_out, v_kv_w, v_b_w_q, v_b_w_o, v_rel_table, v_ffn_w_up, v_ffn_conv_w, v_ffn_conv_b, v_ffn_w_down, v_ln_g, v_ln_b):
    names = ["a_w_in", "a_ln_g", "a_ln_b", "a_w_s", "a_b_s", "a_w_out", "kv_w", "b_w_q", "b_w_o", "rel_table",
             "ffn_w_up", "ffn_conv_w", "ffn_conv_b", "ffn_w_down", "ln_g", "ln_b"]
    w = dict(zip(names, (a_w_in, a_ln_g, a_ln_b, a_w_s, a_b_s, a_w_out, kv_w, b_w_q, b_w_o, rel_table,
                         ffn_w_up, ffn_conv_w, ffn_conv_b, ffn_w_down, ln_g, ln_b)))
    m = dict(zip(names, (m_a_w_in, m_a_ln_g, m_a_ln_b, m_a_w_s, m_a_b_s, m_a_w_out, m_kv_w, m_b_w_q, m_b_w_o,
                         m_rel_table, m_ffn_w_up, m_ffn_conv_w, m_ffn_conv_b, m_ffn_w_down, m_ln_g, m_ln_b)))
    v = dict(zip(names, (v_a_w_in, v_a_ln_g, v_a_ln_b, v_a_w_s, v_a_b_s, v_a_w_out, v_kv_w, v_b_w_q, v_b_w_o,
                         v_rel_table, v_ffn_w_up, v_ffn_conv_w, v_ffn_conv_b, v_ffn_w_down, v_ln_g, v_ln_b)))
    small_names = [n for n, _ in SMALL_SHARDED]
    small_shapes = [w[n].shape for n in small_names]
    rep_shapes = [w[n].shape for n in REPLICATED]

    axis_of = {key: axis for _, key, axis, _ in BIG}
    src = {}
    for n, key, axis, tr in BIG:
        loc = _as_mats(w[n], tr).astype(BF16)
        for l in range(loc.shape[0]):
            src[(key, l)] = loc[l]
    order = []
    for i in range(DEPTH):
        if i < N_A:
            order += [[("a_w_in", i)], [("a_w_out", i)]]
        else:
            order += [([("kv_w", 0)] if i == N_A else []) + [("b_w_q_t", i - N_A)], [("b_w_o", i - N_A)]]
        order += [[("ffn_w_up_t", i)], [("ffn_w_down", i)]]
    small_src = _pack_rows([w[n] for n in small_names])
    srows = small_src.shape[0]
    handles, _ = _send_start([[(small_src, 0)]] + [[(src[kl], axis_of[kl[0]]) for kl in grp] for grp in order],
                             True, name="gather_start")
    small_all = _send_wait(handles[0], True, x, name="gather_wait_small")[0]
    small_st = _unpack_rows(small_all.reshape(N_DEV, srows, LANES), small_shapes, lead=1)
    base = {n: w[n] for n in REPLICATED}
    for (n, ax), st in zip(SMALL_SHARDED, small_st):
        base[n] = _merge_shards(st, ax)
    for n, key, _, tr in BIG:
        base[key] = [None] * (1 if w[n].ndim == 2 else w[n].shape[0])

    def fetch(group, after):
        mats = _send_wait(handles[1 + group], True, after, name=f"gather_wait_{group}")
        return dict(zip(order[group], mats))

    sent = {}

    def emit(group, mats):
        keys = list(mats)
        hs, token = _send_start([[(mats[kl], axis_of[kl[0]]) for kl in keys]], False, name=f"exchange_start_{group}")
        sent[group] = (keys, hs[0])
        return token

    small_sent = []

    def emit_small(grads):
        small_pack = _pack_rows([_split_shards(grads[n], ax) for n, ax in SMALL_SHARDED], lead=1)
        rest = _pack_rows([grads[n] for n in REPLICATED[1:]])
        mine = jnp.concatenate([small_pack.reshape(N_DEV * srows, LANES), rest], axis=0)
        gating = grads[REPLICATED[0]].reshape(-1, LANES).astype(BF16)
        hs, token = _send_start([[(mine, 0), (gating, 0)]], True, name="small_grads_start")
        small_sent.append(hs[0])
        return token

    loss, grad_x, grads = _local_step(x[0], loss_target[0], _Weights(base, fetch, emit, emit_small))
    loss = lax.psum(loss, ("x", "y", "c"))
    out = {}

    landed = {}
    last = grad_x
    left = lambda e: min(g for g, (keys, _) in sent.items() if any(k[0] == e[1] for k in keys))
    for n, key, axis, tr in sorted(BIG, key=left, reverse=True):
        shp = w[n].shape
        for group in sorted(sent, reverse=True):
            keys, h = sent[group]
            if keys[0] not in landed and any(k[0] == key for k in keys):
                landed.update(zip(keys, _send_wait(h, False, last, name=f"exchange_wait_{group}")))
        parts = [landed[(key, l)] for l in range(1 if len(shp) == 2 else shp[0])]
        res = _adamw(_as_mats(w[n], tr), _as_mats(m[n], tr), _as_mats(v[n], tr), parts, name=f"adamw_{n}")
        out[n] = [(jnp.swapaxes(r, 1, 2) if tr else r).reshape(shp) for r in res]
        last = res[0]

    allp, allg = _send_wait(small_sent[0], True, last, name="small_grads_wait")
    gsum = _sum_parts(allp.reshape(N_DEV, -1, LANES), name="sum_small_grads")
    gating = _sum_parts(allg.reshape(N_DEV, -1, LANES), name="sum_gating_grads")
    g_small = lax.dynamic_slice_in_dim(gsum, _my_index() * srows, srows, axis=0)
    pack_sr = lambda d: jnp.concatenate([_pack_rows([d[n] for n in small_names]),
                                         _pack_rows([d[n] for n in REPLICATED])], axis=0)
    n_rest = sum(int(np.prod(s)) for s in rep_shapes[1:]) // LANES
    gs_in = jnp.concatenate([g_small, gating, gsum[N_DEV * srows:N_DEV * srows + n_rest]], axis=0)
    gs_in = jnp.pad(gs_in, ((0, pack_sr(w).shape[0] - gs_in.shape[0]), (0, 0)))[None]
    res = _adamw(pack_sr(w)[None], pack_sr(m)[None], pack_sr(v)[None], [gs_in], name="adamw_small")
    for n, vals in zip(small_names, zip(*[_unpack_rows(r[0, :srows], small_shapes) for r in res])):
        out[n] = list(vals)
    for n, vals in zip(REPLICATED, zip(*[_unpack_rows(r[0, srows:], rep_shapes) for r in res])):
        out[n] = list(vals)

    return (loss, grad_x[None], *[out[n][0] for n in names], *[out[n][1] for n in names],
            *[out[n][2] for n in names], *[out[n][3] for n in names])
```
